```python
import math, functools
import jax, jax.numpy as jnp
from jax import lax
import numpy as np

D_MODEL = 1024
BATCH = 8
SEQ = 8192
DEPTH = 2

N_MEM = 256
D_MAIN = D_MODEL
N_POOL_GROUPS = 4
POOL_WINDOWS = (2, 4, 8, 16)
POOL_GROUP = D_MAIN // N_POOL_GROUPS
FOX_HEADS = 16
FOX_HEAD_DIM = D_MAIN // FOX_HEADS
MEM_HEADS = 4
MEM_HEAD_DIM = 128
D_MEM = MEM_HEADS * MEM_HEAD_DIM
D_MIX = D_MAIN + D_MEM
D_IN = 2 * D_MIX
N_A = DEPTH // 2
N_B = DEPTH - N_A
Q_BLOCK = 128
ALPHA = (2 * DEPTH) ** 0.25
BETA = (8 * DEPTH) ** -0.25
LN_EPS = 1e-5

kernel_name = "yoco_pool_fox_memory_deepnorm"


def layer_norm(x, g, b):
    xf = x.astype(jnp.float32)
    mu = jnp.mean(xf, axis=-1, keepdims=True)
    var = jnp.mean(jnp.square(xf - mu), axis=-1, keepdims=True)
    y = (xf - mu) * lax.rsqrt(var + LN_EPS) * g.astype(jnp.float32) + b.astype(jnp.float32)
    return y.astype(x.dtype)


def causal_multiscale_pool(u, pool_w, pool_scale):
    B, S, _ = u.shape
    ug = u.reshape(B, S, N_POOL_GROUPS, POOL_GROUP)
    cs = jnp.cumsum(ug.astype(jnp.float32), axis=1)
    cs = jnp.pad(cs, ((0, 0), (1, 0), (0, 0), (0, 0)))
    t = jnp.arange(S)
    outs = []
    for gi, w in enumerate(POOL_WINDOWS):
        c = cs[:, :, gi]
        upper = c[:, 1:]
        lower = jnp.concatenate(
            [jnp.zeros((B, w - 1, POOL_GROUP), jnp.float32), c[:, :S + 1 - w]], axis=1)
        count = jnp.minimum(t + 1, w).astype(jnp.float32)[None, :, None]
        outs.append((upper - lower) / count - ug[:, :, gi].astype(jnp.float32))
    pm = jnp.stack(outs, axis=2).astype(u.dtype)
    mixed = jnp.einsum('bsgc,gcd->bsgd', pm, pool_w)
    return mixed.reshape(B, S, D_MAIN) * pool_scale


def shared_kv(x, w_kv_shared, b_forget):
    B, S, _ = x.shape
    h = x @ w_kv_shared
    k = h[..., :D_MAIN].reshape(B, S, FOX_HEADS, FOX_HEAD_DIM)
    v = h[..., D_MAIN:2 * D_MAIN].reshape(B, S, FOX_HEADS, FOX_HEAD_DIM)
    f_logit = h[..., 2 * D_MAIN:].astype(jnp.float32) + b_forget.astype(jnp.float32)
    log_f = jax.nn.log_sigmoid(f_logit)
    cum = jnp.cumsum(log_f, axis=1)
    return k, v, cum


def forgetting_attention(u, k, v, cum):
    B, S, _ = u.shape
    q = u.reshape(B, S, FOX_HEADS, FOX_HEAD_DIM)
    nb = S // Q_BLOCK
    qb = q.reshape(B, nb, Q_BLOCK, FOX_HEADS, FOX_HEAD_DIM).transpose(1, 0, 2, 3, 4)
    cb = cum.reshape(B, nb, Q_BLOCK, FOX_HEADS).transpose(1, 0, 3, 2)
    cum_k = cum.transpose(0, 2, 1)
    starts = jnp.arange(nb) * Q_BLOCK
    kpos = jnp.arange(S)
    scale = FOX_HEAD_DIM ** -0.5

    def block(args):
        qi, ci, s0 = args
        logits = jnp.einsum('bqhd,bkhd->bhqk', qi, k,
                            preferred_element_type=jnp.float32) * scale
        logits = logits + (ci[..., :, None] - cum_k[:, :, None, :])
        qpos = s0 + jnp.arange(Q_BLOCK)
        mask = kpos[None, :] <= qpos[:, None]
        logits = jnp.where(mask[None, None], logits, -jnp.inf)
        p = jax.nn.softmax(logits, axis=-1)
        return jnp.einsum('bhqk,bkhd->bqhd', p.astype(v.dtype), v)

    out = lax.map(block, (qb, cb, starts))
    return out.transpose(1, 0, 2, 3, 4).reshape(B, S, D_MAIN)


def memory_attention(q_mem, mem, w_mem_kv):
    B, S, _ = q_mem.shape
    M = mem.shape[1]
    mkv = mem @ w_mem_kv
    mk = mkv[..., :D_MEM].reshape(B, M, MEM_HEADS, MEM_HEAD_DIM)
    mv = mkv[..., D_MEM:].reshape(B, M, MEM_HEADS, MEM_HEAD_DIM)
    q = q_mem.reshape(B, S, MEM_HEADS, MEM_HEAD_DIM)
    logits = jnp.einsum('bshd,bmhd->bhsm', q, mk,
                        preferred_element_type=jnp.float32) * (MEM_HEAD_DIM ** -0.5)
    p = jax.nn.softmax(logits, axis=-1)
    return jnp.einsum('bhsm,bmhd->bshd', p.astype(mv.dtype), mv).reshape(B, S, D_MEM)


def mixer_sublayer(x, mem, w_in, w_mem_kv, w_out, main_fn):
    h = x @ w_in
    u_main = h[..., :D_MAIN]
    q_mem = h[..., D_MAIN:D_MIX]
    g_main = h[..., D_MIX:D_MIX + D_MAIN]
    g_mem = h[..., D_MIX + D_MAIN:]
    y_main = main_fn(u_main)
    y_mem = memory_attention(q_mem, mem, w_mem_kv)
    y = jnp.concatenate([y_main * jax.nn.silu(g_main), y_mem * jax.nn.silu(g_mem)], axis=-1)
    return y @ w_out


def _fwd_setup_inputs(seed: int = 0) -> dict:
    key = jax.random.key(seed)
    ks = jax.random.split(key, 12)
    x = jax.random.normal(ks[0], (BATCH, SEQ, D_MODEL), jnp.float32)
    mem = jax.random.normal(ks[1], (BATCH, N_MEM, D_MODEL), jnp.float32)
    w_in = jax.random.normal(ks[2], (DEPTH, D_MODEL, D_IN), jnp.float32) * D_MODEL ** -0.5
    w_mem_kv = jax.random.normal(ks[3], (DEPTH, D_MODEL, 2 * D_MEM), jnp.float32) * D_MODEL ** -0.5
    w_out = jax.random.normal(ks[4], (DEPTH, D_MIX, D_MODEL), jnp.float32) * (D_MIX ** -0.5 * BETA)
    ln_g = 1.0 + 0.02 * jax.random.normal(ks[5], (DEPTH, D_MODEL), jnp.float32)
    ln_b = 0.02 * jax.random.normal(ks[6], (DEPTH, D_MODEL), jnp.float32)
    pool_w = jax.random.normal(ks[7], (N_A, N_POOL_GROUPS, POOL_GROUP, POOL_GROUP), jnp.float32) * POOL_GROUP ** -0.5
    pool_scale = 1.0 + 0.1 * jax.random.normal(ks[8], (N_A, D_MAIN), jnp.float32)
    w_kv_shared = jax.random.normal(ks[9], (D_MODEL, 2 * D_MAIN + FOX_HEADS), jnp.float32) * D_MODEL ** -0.5
    b_forget = jax.random.uniform(ks[10], (FOX_HEADS,), jnp.float32, 1.0, 5.0)
    return {"x": x, "mem": mem, "w_in": w_in, "w_mem_kv": w_mem_kv, "w_out": w_out,
            "ln_g": ln_g, "ln_b": ln_b, "pool_w": pool_w, "pool_scale": pool_scale,
            "w_kv_shared": w_kv_shared, "b_forget": b_forget}


def _fwd_reference(x, mem, w_in, w_mem_kv, w_out, ln_g, ln_b, pool_w, pool_scale,
              w_kv_shared, b_forget):
    k_sh = v_sh = cum_sh = None
    for layer in range(DEPTH):
        if layer < N_A:
            main_fn = functools.partial(causal_multiscale_pool,
                                        pool_w=pool_w[layer], pool_scale=pool_scale[layer])
        else:
            if layer == N_A:
                k_sh, v_sh, cum_sh = shared_kv(x, w_kv_shared, b_forget)
            main_fn = functools.partial(forgetting_attention, k=k_sh, v=v_sh, cum=cum_sh)
        y = mixer_sublayer(x, mem, w_in[layer], w_mem_kv[layer], w_out[layer], main_fn)
        x = layer_norm(ALPHA * x + y, ln_g[layer], ln_b[layer])
    return x


import jax as _jax
import jax.numpy as _jnp

TWIN_FORMAT = 'train_step'
FWD_PARAMS = ['x', 'mem', 'w_in', 'w_mem_kv', 'w_out', 'ln_g', 'ln_b', 'pool_w', 'pool_scale', 'w_kv_shared', 'b_forget']
TWIN_WEIGHTS = ['w_in', 'w_mem_kv', 'w_out', 'ln_g', 'ln_b', 'pool_w', 'pool_scale', 'w_kv_shared', 'b_forget']
TWIN_DIFF_INPUT = 'x'
TWIN_INPUTS = ['x', 'mem', 'w_in', 'w_mem_kv', 'w_out', 'ln_g', 'ln_b', 'pool_w', 'pool_scale', 'w_kv_shared', 'b_forget', 'loss_target', 'm_w_in', 'm_w_mem_kv', 'm_w_out', 'm_ln_g', 'm_ln_b', 'm_pool_w', 'm_pool_scale', 'm_w_kv_shared', 'm_b_forget', 'v_w_in', 'v_w_mem_kv', 'v_w_out', 'v_ln_g', 'v_ln_b', 'v_pool_w', 'v_pool_scale', 'v_w_kv_shared', 'v_b_forget']
TWIN_OUTPUTS = ['loss', 'grad_x', 'grad_w_in', 'grad_w_mem_kv', 'grad_w_out', 'grad_ln_g', 'grad_ln_b', 'grad_pool_w', 'grad_pool_scale', 'grad_w_kv_shared', 'grad_b_forget', 'delta_w_in', 'delta_w_mem_kv', 'delta_w_out', 'delta_ln_g', 'delta_ln_b', 'delta_pool_w', 'delta_pool_scale', 'delta_w_kv_shared', 'delta_b_forget', 'new_m_w_in', 'new_m_w_mem_kv', 'new_m_w_out', 'new_m_ln_g', 'new_m_ln_b', 'new_m_pool_w', 'new_m_pool_scale', 'new_m_w_kv_shared', 'new_m_b_forget', 'new_v_w_in', 'new_v_w_mem_kv', 'new_v_w_out', 'new_v_ln_g', 'new_v_ln_b', 'new_v_pool_w', 'new_v_pool_scale', 'new_v_w_kv_shared', 'new_v_b_forget']
TWIN_LEAF_KINDS = {'loss': 'loss', 'grad_x': 'grad_x', 'grad_w_in': 'grad_w', 'grad_w_mem_kv': 'grad_w', 'grad_w_out': 'grad_w', 'grad_ln_g': 'grad_w', 'grad_ln_b': 'grad_w', 'grad_pool_w': 'grad_w', 'grad_pool_scale': 'grad_w', 'grad_w_kv_shared': 'grad_w', 'grad_b_forget': 'grad_w', 'delta_w_in': 'delta_w', 'delta_w_mem_kv': 'delta_w', 'delta_w_out': 'delta_w', 'delta_ln_g': 'delta_w', 'delta_ln_b': 'delta_w', 'delta_pool_w': 'delta_w', 'delta_pool_scale': 'delta_w', 'delta_w_kv_shared': 'delta_w', 'delta_b_forget': 'delta_w', 'new_m_w_in': 'new_m', 'new_m_w_mem_kv': 'new_m', 'new_m_w_out': 'new_m', 'new_m_ln_g': 'new_m', 'new_m_ln_b': 'new_m', 'new_m_pool_w': 'new_m', 'new_m_pool_scale': 'new_m', 'new_m_w_kv_shared': 'new_m', 'new_m_b_forget': 'new_m', 'new_v_w_in': 'new_v', 'new_v_w_mem_kv': 'new_v', 'new_v_w_out': 'new_v', 'new_v_ln_g': 'new_v', 'new_v_ln_b': 'new_v', 'new_v_pool_w': 'new_v', 'new_v_pool_scale': 'new_v', 'new_v_w_kv_shared': 'new_v', 'new_v_b_forget': 'new_v'}


def _forward(args):
    return _fwd_reference(*[args[k] for k in FWD_PARAMS])


def _output_shape():
    def fwd():
        inp = _fwd_setup_inputs(0)
        return _fwd_reference(*[inp[k] for k in FWD_PARAMS])
    out = _jax.eval_shape(fwd)
    return out.shape, out.dtype

N_MICROBATCH = 1
ADAM_LR = 0.001
ADAM_B1 = 0.9
ADAM_B2 = 0.999
ADAM_EPS = 1e-08
ADAM_WD = 0.01
ADAM_STEP = 10
PER_EXAMPLE_BATCH_AXIS = {'x': 0, 'mem': 0, 'loss_target': 0}
SHARED_INPUTS = []
_WEIGHT_DTYPES = {'w_in': _jnp.float32, 'w_mem_kv': _jnp.float32, 'w_out': _jnp.float32, 'ln_g': _jnp.float32, 'ln_b': _jnp.float32, 'pool_w': _jnp.float32, 'pool_scale': _jnp.float32, 'w_kv_shared': _jnp.float32, 'b_forget': _jnp.float32}
MOMENT_SCALE = {'w_in': 2.354812e-02, 'w_mem_kv': 4.568884e-03, 'w_out': 5.753931e-02, 'ln_g': 4.530745e+01, 'ln_b': 1.201595e+00, 'pool_w': 3.782407e-02, 'pool_scale': 3.788478e-02, 'w_kv_shared': 1.338846e-02, 'b_forget': 7.860108e-02}


def _to_microbatches(a, axis):
    t = _jnp.moveaxis(a, axis, 0)
    t = t.reshape((N_MICROBATCH, t.shape[0] // N_MICROBATCH) + t.shape[1:])
    return _jnp.moveaxis(t, 1, axis + 1)


def setup_inputs(seed: int = 0) -> dict:
    inp = _fwd_setup_inputs(seed)
    key = _jax.random.fold_in(_jax.random.key(seed), 7919)
    shape, _ = _output_shape()
    out = dict(inp)
    out["loss_target"] = _jax.random.normal(_jax.random.fold_in(key, 0), shape, _jnp.float32)
    for i, name in enumerate(TWIN_WEIGHTS):
        w = inp[name].astype(_jnp.float32)
        if MOMENT_SCALE is None:
            s = _jnp.sqrt(_jnp.mean(_jnp.square(w)) + 1e-30)
        else:
            s = MOMENT_SCALE[name]
        km, kv = _jax.random.split(_jax.random.fold_in(key, i + 1))
        out[name] = w
        out["m_" + name] = s * _jax.random.normal(km, w.shape, _jnp.float32)
        out["v_" + name] = (s * s) * _jax.random.uniform(kv, w.shape, _jnp.float32, 0.5, 1.5)
    if N_MICROBATCH > 1:
        for name, axis in PER_EXAMPLE_BATCH_AXIS.items():
            out[name] = _to_microbatches(out[name], axis)
    return {'x': out['x'], 'mem': out['mem'], 'w_in': out['w_in'], 'w_mem_kv': out['w_mem_kv'], 'w_out': out['w_out'], 'ln_g': out['ln_g'], 'ln_b': out['ln_b'], 'pool_w': out['pool_w'], 'pool_scale': out['pool_scale'], 'w_kv_shared': out['w_kv_shared'], 'b_forget': out['b_forget'], 'loss_target': out['loss_target'], 'm_w_in': out['m_w_in'], 'm_w_mem_kv': out['m_w_mem_kv'], 'm_w_out': out['m_w_out'], 'm_ln_g': out['m_ln_g'], 'm_ln_b': out['m_ln_b'], 'm_pool_w': out['m_pool_w'], 'm_pool_scale': out['m_pool_scale'], 'm_w_kv_shared': out['m_w_kv_shared'], 'm_b_forget': out['m_b_forget'], 'v_w_in': out['v_w_in'], 'v_w_mem_kv': out['v_w_mem_kv'], 'v_w_out': out['v_w_out'], 'v_ln_g': out['v_ln_g'], 'v_ln_b': out['v_ln_b'], 'v_pool_w': out['v_pool_w'], 'v_pool_scale': out['v_pool_scale'], 'v_w_kv_shared': out['v_w_kv_shared'], 'v_b_forget': out['v_b_forget']}


def _loss(weights, diff, rest, loss_target):
    with _jax.named_scope("forward"):
        args = {**rest, TWIN_DIFF_INPUT: diff, **{k: w.astype(_WEIGHT_DTYPES[k]) for k, w in weights.items()}}
        y = _forward(args)
    with _jax.named_scope("loss_head"):
        err = _jnp.square(y.astype(_jnp.float32) - loss_target)
        return 0.5 * _jnp.sum(_jnp.mean(err, axis=-1)) if err.ndim else 0.5 * err


def _adamw(w, g, m, v):
    m = ADAM_B1 * m + (1.0 - ADAM_B1) * g
    v = ADAM_B2 * v + (1.0 - ADAM_B2) * _jnp.square(g)
    m_hat = m / (1.0 - ADAM_B1 ** ADAM_STEP)
    v_hat = v / (1.0 - ADAM_B2 ** ADAM_STEP)
    delta = -ADAM_LR * (m_hat / (_jnp.sqrt(v_hat) + ADAM_EPS) + ADAM_WD * w)
    return delta, m, v


def reference(x, mem, w_in, w_mem_kv, w_out, ln_g, ln_b, pool_w, pool_scale, w_kv_shared, b_forget, loss_target, m_w_in, m_w_mem_kv, m_w_out, m_ln_g, m_ln_b, m_pool_w, m_pool_scale, m_w_kv_shared, m_b_forget, v_w_in, v_w_mem_kv, v_w_out, v_ln_g, v_ln_b, v_pool_w, v_pool_scale, v_w_kv_shared, v_b_forget):
    given = dict(x=x, mem=mem, w_in=w_in, w_mem_kv=w_mem_kv, w_out=w_out, ln_g=ln_g, ln_b=ln_b, pool_w=pool_w, pool_scale=pool_scale, w_kv_shared=w_kv_shared, b_forget=b_forget, loss_target=loss_target, m_w_in=m_w_in, m_w_mem_kv=m_w_mem_kv, m_w_out=m_w_out, m_ln_g=m_ln_g, m_ln_b=m_ln_b, m_pool_w=m_pool_w, m_pool_scale=m_pool_scale, m_w_kv_shared=m_w_kv_shared, m_b_forget=m_b_forget, v_w_in=v_w_in, v_w_mem_kv=v_w_mem_kv, v_w_out=v_w_out, v_ln_g=v_ln_g, v_ln_b=v_ln_b, v_pool_w=v_pool_w, v_pool_scale=v_pool_scale, v_w_kv_shared=v_w_kv_shared, v_b_forget=v_b_forget)
    weights = {n: given[n] for n in TWIN_WEIGHTS}
    shared = {n: given[n] for n in SHARED_INPUTS}
    per_example = {n: given[n] for n in ['x', 'mem']}
    grad_fn = _jax.value_and_grad(_loss, argnums=(0, 1))

    def one_microbatch(ex, loss_target):
        ex = dict(ex)
        diff = ex.pop(TWIN_DIFF_INPUT)
        return grad_fn(weights, diff, {**shared, **ex}, loss_target)

    if N_MICROBATCH == 1:
        loss, (grad_w, grad_x) = one_microbatch(per_example, given["loss_target"])
    else:
        def body(carry, xs):
            loss_sum, grad_sum = carry
            l_k, (gw_k, gx_k) = one_microbatch(xs[0], xs[1])
            with _jax.named_scope("update"):
                return (loss_sum + l_k, _jax.tree.map(_jnp.add, grad_sum, gw_k)), gx_k

        init = (_jnp.zeros((), _jnp.float32), _jax.tree.map(_jnp.zeros_like, weights))
        (loss, grad_w), grad_x = _jax.lax.scan(body, init, (per_example, given["loss_target"]))
    with _jax.named_scope("update"):
        delta_w, new_m, new_v = {}, {}, {}
        for n in TWIN_WEIGHTS:
            delta_w[n], new_m[n], new_v[n] = _adamw(weights[n], grad_w[n], given["m_" + n], given["v_" + n])
    return (loss, grad_x, *[grad_w[n] for n in TWIN_WEIGHTS], *[delta_w[n] for n in TWIN_WEIGHTS],
            *[new_m[n] for n in TWIN_WEIGHTS], *[new_v[n] for n in TWIN_WEIGHTS])
```

```python
import functools
import math

import jax
import jax.numpy as jnp
from jax import lax
from jax.experimental import pallas as pl
from jax.experimental.pallas import tpu as pltpu

F32 = jnp.float32
BF16 = jnp.bfloat16

D_MODEL = 1024
D_MAIN = 1024
D_MEM = 512
D_MIX = D_MAIN + D_MEM
D_IN = 2 * D_MIX
N_MEM = 256
MEM_HEADS = 4
MEM_HEAD_DIM = 128
FOX_HEADS = 16
FOX_HEAD_DIM = 64
FOX_PAIRS = FOX_HEADS // 2
POOL_WINDOWS = (2, 4, 8, 16)
POOL_GROUP = 256
POOL_HALO = 16
ALPHA = 4.0 ** 0.25
LN_EPS = 1e-5
NEG = -1e30
LANES = 128
N_DEV = 8

ADAM_LR = 0.001
ADAM_B1 = 0.9
ADAM_B2 = 0.999
ADAM_EPS = 1e-08
ADAM_WD = 0.01
ADAM_STEP = 10

VMEM_LIMIT = 56 * 1024 * 1024

NN = (((1,), (0,)), ((), ()))
NT = (((1,), (1,)), ((), ()))
TN = (((0,), (0,)), ((), ()))


def _cparams(*sem):
    return pltpu.CompilerParams(dimension_semantics=sem, vmem_limit_bytes=VMEM_LIMIT)


def _sigmoid(z):
    return 1.0 / (1.0 + jnp.exp(-z))


def _mm(a, b, *, mode, out_dtype, tm, tn, tk, name, add=None, add_scale=1.0):
    if mode == "nn":
        (M, K), (K2, N) = a.shape, b.shape
    elif mode == "nt":
        (M, K), (N, K2) = a.shape, b.shape
    else:
        (K, M), (K2, N) = a.shape, b.shape
    assert K == K2, (a.shape, b.shape, mode)
    tm, tn, tk = min(tm, M), min(tn, N), min(tk, K)
    assert M % tm == 0 and N % tn == 0 and K % tk == 0, (M, N, K, tm, tn, tk)
    gm, gn, gk = M // tm, N // tn, K // tk
    dims = {"nn": NN, "nt": NT, "tn": TN}[mode]
    if mode == "tn":
        a_spec = pl.BlockSpec((tk, tm), lambda i, j, k: (k, i))
    else:
        a_spec = pl.BlockSpec((tm, tk), lambda i, j, k: (i, k))
    if mode == "nt":
        b_spec = pl.BlockSpec((tn, tk), lambda i, j, k: (j, k))
    else:
        b_spec = pl.BlockSpec((tk, tn), lambda i, j, k: (k, j))
    o_spec = pl.BlockSpec((tm, tn), lambda i, j, k: (i, j))
    has_add = add is not None

    def body(*refs):
        a_ref, b_ref = refs[0], refs[1]
        add_ref = refs[2] if has_add else None
        o_ref = refs[3] if has_add else refs[2]
        prod = lax.dot_general(a_ref[...].astype(BF16), b_ref[...].astype(BF16), dims,
                               preferred_element_type=F32)

        def finish(r):
            if has_add:
                r = r + add_scale * add_ref[...]
            o_ref[...] = r.astype(out_dtype)

        if gk == 1:
            finish(prod)
        else:
            acc_ref = refs[-1]
            k = pl.program_id(2)

            @pl.when(k == 0)
            def _():
                acc_ref[...] = prod

            @pl.when(k > 0)
            def _():
                acc_ref[...] += prod

            @pl.when(k == gk - 1)
            def _():
                finish(acc_ref[...])

    in_specs = [a_spec, b_spec] + ([o_spec] if has_add else [])
    args = (a, b) + ((add,) if has_add else ())
    return pl.pallas_call(
        body, name=name, grid=(gm, gn, gk), in_specs=in_specs, out_specs=o_spec,
        out_shape=jax.ShapeDtypeStruct((M, N), out_dtype),
        scratch_shapes=[pltpu.VMEM((tm, tn), F32)] if gk > 1 else [],
        compiler_params=_cparams("parallel", "parallel", "arbitrary"),
    )(*args)


def _ln_fwd(x, o, g, b, *, tb, name):
    S = x.shape[0]
    tb = min(tb, S)

    def body(x_ref, o_ref, g_ref, b_ref, y_ref, xhat_ref, rstd_ref):
        z = ALPHA * x_ref[...] + o_ref[...]
        mu = jnp.mean(z, axis=1, keepdims=True)
        zc = z - mu
        var = jnp.mean(zc * zc, axis=1, keepdims=True)
        rstd = lax.rsqrt(var + LN_EPS)
        xhat = zc * rstd
        xhat_ref[...] = xhat
        rstd_ref[...] = rstd
        y_ref[...] = xhat * g_ref[...] + b_ref[...]

    row = pl.BlockSpec((tb, D_MODEL), lambda i: (i, 0))
    vec = pl.BlockSpec((1, D_MODEL), lambda i: (0, 0))
    return pl.pallas_call(
        body, name=name, grid=(S // tb,), in_specs=[row, row, vec, vec],
        out_specs=[row, row, pl.BlockSpec((tb, 1), lambda i: (i, 0))],
        out_shape=[jax.ShapeDtypeStruct((S, D_MODEL), F32), jax.ShapeDtypeStruct((S, D_MODEL), F32),
                   jax.ShapeDtypeStruct((S, 1), F32)],
        compiler_params=_cparams("parallel"),
    )(x, o, g, b)


def _ln_bwd_math(dy, xhat, rstd, g):
    dxh = dy * g
    m1 = jnp.mean(dxh, axis=1, keepdims=True)
    m2 = jnp.mean(dxh * xhat, axis=1, keepdims=True)
    return rstd * (dxh - m1 - xhat * m2)


def _ln_bwd(dy, xhat, rstd, g, *, tb, name):
    S = dy.shape[0]
    tb = min(tb, S)

    def body(dy_ref, xhat_ref, rstd_ref, g_ref, dz_ref, dg_ref, db_ref):
        @pl.when(pl.program_id(0) == 0)
        def _():
            dg_ref[...] = jnp.zeros_like(dg_ref)
            db_ref[...] = jnp.zeros_like(db_ref)

        dy_, xhat_ = dy_ref[...], xhat_ref[...]
        dz_ref[...] = _ln_bwd_math(dy_, xhat_, rstd_ref[...], g_ref[...])
        dg_ref[...] += jnp.sum(dy_ * xhat_, axis=0, keepdims=True)
        db_ref[...] += jnp.sum(dy_, axis=0, keepdims=True)

    row = pl.BlockSpec((tb, D_MODEL), lambda i: (i, 0))
    vec = pl.BlockSpec((1, D_MODEL), lambda i: (0, 0))
    return pl.pallas_call(
        body, name=name, grid=(S // tb,),
        in_specs=[row, row, pl.BlockSpec((tb, 1), lambda i: (i, 0)), vec],
        out_specs=[row, vec, vec],
        out_shape=[jax.ShapeDtypeStruct((S, D_MODEL), F32), jax.ShapeDtypeStruct((1, D_MODEL), F32),
                   jax.ShapeDtypeStruct((1, D_MODEL), F32)],
        compiler_params=_cparams("arbitrary"),
    )(dy, xhat, rstd, g)


def _ln_loss(x, o, g, b, target, *, tb, name):
    S = x.shape[0]
    tb = min(tb, S)

    def body(x_ref, o_ref, g_ref, b_ref, t_ref, dz_ref, dg_ref, db_ref, sq_ref):
        @pl.when(pl.program_id(0) == 0)
        def _():
            dg_ref[...] = jnp.zeros_like(dg_ref)
            db_ref[...] = jnp.zeros_like(db_ref)
            sq_ref[...] = jnp.zeros_like(sq_ref)

        z = ALPHA * x_ref[...] + o_ref[...]
        mu = jnp.mean(z, axis=1, keepdims=True)
        zc = z - mu
        var = jnp.mean(zc * zc, axis=1, keepdims=True)
        rstd = lax.rsqrt(var + LN_EPS)
        xhat = zc * rstd
        err = xhat * g_ref[...] + b_ref[...] - t_ref[...]
        sq_ref[...] += jnp.sum(err * err, axis=0, keepdims=True)
        dy = err * (1.0 / D_MODEL)
        dz_ref[...] = _ln_bwd_math(dy, xhat, rstd, g_ref[...])
        dg_ref[...] += jnp.sum(dy * xhat, axis=0, keepdims=True)
        db_ref[...] += jnp.sum(dy, axis=0, keepdims=True)

    row = pl.BlockSpec((tb, D_MODEL), lambda i: (i, 0))
    vec = pl.BlockSpec((1, D_MODEL), lambda i: (0, 0))
    vshape = jax.ShapeDtypeStruct((1, D_MODEL), F32)
    return pl.pallas_call(
        body, name=name, grid=(S // tb,), in_specs=[row, row, vec, vec, row],
        out_specs=[row, vec, vec, vec],
        out_shape=[jax.ShapeDtypeStruct((S, D_MODEL), F32), vshape, vshape, vshape],
        compiler_params=_cparams("arbitrary"),
    )(x, o, g, b, target)


def _gate_fwd(ysrc, scale, h, ymem, *, tb, name):
    S = ysrc.shape[0]
    tb = min(tb, S)

    def body(ys_ref, sc_ref, ga_ref, gb_ref, gc_ref, ym_ref, yg_ref):
        ymain = ys_ref[...] * sc_ref[...]
        for k, g_ref in enumerate((ga_ref, gb_ref)):
            gv = g_ref[...]
            yg_ref[:, 512 * k:512 * (k + 1)] = (ymain[:, 512 * k:512 * (k + 1)] * gv * _sigmoid(gv)).astype(BF16)
        gv = gc_ref[...]
        yg_ref[:, 1024:1536] = (ym_ref[...] * gv * _sigmoid(gv)).astype(BF16)

    slab = lambda c: pl.BlockSpec((tb, 512), lambda i, c=c: (i, c))
    return pl.pallas_call(
        body, name=name, grid=(S // tb,),
        in_specs=[pl.BlockSpec((tb, D_MAIN), lambda i: (i, 0)), pl.BlockSpec((1, D_MAIN), lambda i: (0, 0)),
                  slab(3), slab(4), slab(5), pl.BlockSpec((tb, D_MEM), lambda i: (i, 0))],
        out_specs=pl.BlockSpec((tb, D_MIX), lambda i: (i, 0)),
        out_shape=jax.ShapeDtypeStruct((S, D_MIX), BF16),
        compiler_params=_cparams("parallel"),
    )(ysrc, scale, h, h, h, ymem)


def _gate_bwd(dyg, ysrc, scale, h, ymem, *, tb, name):
    S = ysrc.shape[0]
    tb = min(tb, S)

    def dsilu(gv):
        sg = _sigmoid(gv)
        return sg, sg * (1.0 + gv * (1.0 - sg))

    def body(da_ref, db_ref, dc_ref, ys_ref, sc_ref, ga_ref, gb_ref, gc_ref, ym_ref, dym_ref, dymem_ref, dh_ref):
        ymain = ys_ref[...] * sc_ref[...]
        for k, (d_ref, g_ref) in enumerate(((da_ref, ga_ref), (db_ref, gb_ref))):
            gv, d = g_ref[...], d_ref[...]
            sg, ds = dsilu(gv)
            dym_ref[:, 512 * k:512 * (k + 1)] = d * gv * sg
            dh_ref[:, 512 * k:512 * (k + 1)] = (d * ymain[:, 512 * k:512 * (k + 1)] * ds).astype(BF16)
        gv, d = gc_ref[...], dc_ref[...]
        sg, ds = dsilu(gv)
        dymem_ref[...] = d * gv * sg
        dh_ref[:, 1024:1536] = (d * ym_ref[...] * ds).astype(BF16)

    slab = lambda c: pl.BlockSpec((tb, 512), lambda i, c=c: (i, c))
    return pl.pallas_call(
        body, name=name, grid=(S // tb,),
        in_specs=[slab(0), slab(1), slab(2),
                  pl.BlockSpec((tb, D_MAIN), lambda i: (i, 0)), pl.BlockSpec((1, D_MAIN), lambda i: (0, 0)),
                  slab(3), slab(4), slab(5), pl.BlockSpec((tb, D_MEM), lambda i: (i, 0))],
        out_specs=[pl.BlockSpec((tb, D_MAIN), lambda i: (i, 0)), pl.BlockSpec((tb, D_MEM), lambda i: (i, 0)),
                   pl.BlockSpec((tb, D_MIX), lambda i: (i, 1))],
        out_shape=[jax.ShapeDtypeStruct((S, D_MAIN), F32), jax.ShapeDtypeStruct((S, D_MEM), F32),
                   jax.ShapeDtypeStruct((S, D_IN), BF16)],
        compiler_params=_cparams("parallel"),
    )(dyg, dyg, dyg, ysrc, scale, h, h, h, ymem)


def _window_count(t0, rows, w):
    t = t0 + lax.broadcasted_iota(jnp.int32, (rows, POOL_GROUP), 0)
    return jnp.minimum(t + 1, w).astype(F32)


def _pool_fwd(h, pw, *, tb, name):
    S = h.shape[0]
    tb = min(tb, S)
    n = tb + POOL_HALO

    def body(u_ref, pw_ref, pm_ref, mixed_ref, tail_ref):
        i = pl.program_id(0)

        @pl.when(i == 0)
        def _():
            tail_ref[...] = jnp.zeros_like(tail_ref)

        u = u_ref[...]
        xfull = jnp.concatenate([tail_ref[...], u], axis=0)
        for gi, w in enumerate(POOL_WINDOWS):
            cols = slice(POOL_GROUP * gi, POOL_GROUP * (gi + 1))
            s = xfull[:, cols]
            sh = 1
            while sh < w:
                s = s + pltpu.roll(s, sh, 0)
                sh *= 2
            pm = s[POOL_HALO:, :] / _window_count(i * tb, tb, w) - u[:, cols]
            pmb = pm.astype(BF16)
            pm_ref[:, cols] = pmb
            mixed_ref[:, cols] = jnp.dot(pmb, pw_ref[gi], preferred_element_type=F32)
        tail_ref[...] = u[tb - POOL_HALO:, :]

    return pl.pallas_call(
        body, name=name, grid=(S // tb,),
        in_specs=[pl.BlockSpec((tb, D_MAIN), lambda i: (i, 0)),
                  pl.BlockSpec((4, POOL_GROUP, POOL_GROUP), lambda i: (0, 0, 0))],
        out_specs=[pl.BlockSpec((tb, D_MAIN), lambda i: (i, 0)), pl.BlockSpec((tb, D_MAIN), lambda i: (i, 0))],
        out_shape=[jax.ShapeDtypeStruct((S, D_MAIN), BF16), jax.ShapeDtypeStruct((S, D_MAIN), F32)],
        scratch_shapes=[pltpu.VMEM((POOL_HALO, D_MAIN), F32)],
        compiler_params=_cparams("arbitrary"),
    )(h, pw)


def _pool_bwd(dymain, pm, mixed, pw, scale, dh, *, tb, name):
    S = dymain.shape[0]
    tb = min(tb, S)
    nb = S // tb
    n = tb + POOL_HALO

    def body(dy_ref, pm_ref, mixed_ref, pw_ref, sc_ref, dh_in, dh_ref, dpw_ref, dsc_ref, head_ref):
        del dh_in
        i = pl.program_id(0)

        @pl.when(i == 0)
        def _():
            head_ref[...] = jnp.zeros_like(head_ref)
            dpw_ref[...] = jnp.zeros_like(dpw_ref)
            dsc_ref[...] = jnp.zeros_like(dsc_ref)

        dy = dy_ref[...]
        dsc_ref[...] += jnp.sum(dy * mixed_ref[...], axis=0, keepdims=True)
        dmixed = dy * sc_ref[...]
        t0 = (nb - 1 - i) * tb
        for gi, w in enumerate(POOL_WINDOWS):
            cols = slice(POOL_GROUP * gi, POOL_GROUP * (gi + 1))
            dm = dmixed[:, cols].astype(BF16)
            dpw_ref[gi] += lax.dot_general(pm_ref[:, cols], dm, TN, preferred_element_type=F32)
            dpm = lax.dot_general(dm, pw_ref[gi], NT, preferred_element_type=F32)
            e = dpm / _window_count(t0, tb, w)
            s = jnp.concatenate([e, head_ref[:, cols]], axis=0)
            sh = 1
            while sh < w:
                s = s + pltpu.roll(s, n - sh, 0)
                sh *= 2
            dh_ref[:, cols] = (s[:tb, :] - dpm).astype(BF16)
            head_ref[:, cols] = e[:POOL_HALO, :]

    rev = lambda i: (nb - 1 - i, 0)
    return pl.pallas_call(
        body, name=name, grid=(nb,),
        in_specs=[pl.BlockSpec((tb, D_MAIN), rev), pl.BlockSpec((tb, D_MAIN), rev), pl.BlockSpec((tb, D_MAIN), rev),
                  pl.BlockSpec((4, POOL_GROUP, POOL_GROUP), lambda i: (0, 0, 0)),
                  pl.BlockSpec((1, D_MAIN), lambda i: (0, 0)), pl.BlockSpec(memory_space=pl.ANY)],
        out_specs=[pl.BlockSpec((tb, D_MAIN), rev),
                   pl.BlockSpec((4, POOL_GROUP, POOL_GROUP), lambda i: (0, 0, 0)),
                   pl.BlockSpec((1, D_MAIN), lambda i: (0, 0))],
        out_shape=[jax.ShapeDtypeStruct(dh.shape, dh.dtype),
                   jax.ShapeDtypeStruct((4, POOL_GROUP, POOL_GROUP), F32), jax.ShapeDtypeStruct((1, D_MAIN), F32)],
        scratch_shapes=[pltpu.VMEM((POOL_HALO, D_MAIN), F32)],
        input_output_aliases={5: 0},
        compiler_params=_cparams("arbitrary"),
    )(dymain, pm, mixed, pw, scale, dh)


MEM_SCALE = MEM_HEAD_DIM ** -0.5


def _mem_probs(q_ref, mkv_ref, hd):
    cols = slice(MEM_HEAD_DIM * hd, MEM_HEAD_DIM * (hd + 1))
    q = (q_ref[:, cols] * MEM_SCALE).astype(BF16)
    mk = mkv_ref[:, cols].astype(BF16)
    mv = mkv_ref[:, D_MEM + MEM_HEAD_DIM * hd:D_MEM + MEM_HEAD_DIM * (hd + 1)].astype(BF16)
    s = lax.dot_general(q, mk, NT, preferred_element_type=F32)
    e = jnp.exp(s - jnp.max(s, axis=1, keepdims=True))
    return cols, q, mk, mv, e, jnp.sum(e, axis=1, keepdims=True)


def _memattn_fwd(h, mkv, *, tb, name):
    S = h.shape[0]
    tb = min(tb, S)

    def body(q_ref, mkv_ref, y_ref):
        for hd in range(MEM_HEADS):
            cols, _, _, mv, e, l = _mem_probs(q_ref, mkv_ref, hd)
            y_ref[:, cols] = jnp.dot(e.astype(BF16), mv, preferred_element_type=F32) / l

    return pl.pallas_call(
        body, name=name, grid=(S // tb,),
        in_specs=[pl.BlockSpec((tb, D_MEM), lambda i: (i, 2)), pl.BlockSpec((N_MEM, 2 * D_MEM), lambda i: (0, 0))],
        out_specs=pl.BlockSpec((tb, D_MEM), lambda i: (i, 0)),
        out_shape=jax.ShapeDtypeStruct((S, D_MEM), F32),
        compiler_params=_cparams("parallel"),
    )(h, mkv)


def _memattn_bwd(h, mkv, dy, dh, *, tb, name):
    S = h.shape[0]
    tb = min(tb, S)

    def body(q_ref, mkv_ref, dy_ref, dh_in, dh_ref, dmkv_ref):
        del dh_in

        @pl.when(pl.program_id(0) == 0)
        def _():
            dmkv_ref[...] = jnp.zeros_like(dmkv_ref)

        for hd in range(MEM_HEADS):
            cols, q, mk, mv, e, l = _mem_probs(q_ref, mkv_ref, hd)
            p = e / l
            dyh = dy_ref[:, cols].astype(BF16)
            dp = lax.dot_general(dyh, mv, NT, preferred_element_type=F32)
            ds = p * (dp - jnp.sum(dp * p, axis=1, keepdims=True))
            dsb = ds.astype(BF16)
            dh_ref[:, cols] = (jnp.dot(dsb, mk, preferred_element_type=F32) * MEM_SCALE).astype(BF16)
            dmkv_ref[:, cols] += lax.dot_general(dsb, q, TN, preferred_element_type=F32)
            vcols = slice(D_MEM + MEM_HEAD_DIM * hd, D_MEM + MEM_HEAD_DIM * (hd + 1))
            dmkv_ref[:, vcols] += lax.dot_general(p.astype(BF16), dyh, TN, preferred_element_type=F32)

    return pl.pallas_call(
        body, name=name, grid=(S // tb,),
        in_specs=[pl.BlockSpec((tb, D_MEM), lambda i: (i, 2)), pl.BlockSpec((N_MEM, 2 * D_MEM), lambda i: (0, 0)),
                  pl.BlockSpec((tb, D_MEM), lambda i: (i, 0)), pl.BlockSpec(memory_space=pl.ANY)],
        out_specs=[pl.BlockSpec((tb, D_MEM), lambda i: (i, 2)), pl.BlockSpec((N_MEM, 2 * D_MEM), lambda i: (0, 0))],
        out_shape=[jax.ShapeDtypeStruct(dh.shape, dh.dtype), jax.ShapeDtypeStruct((N_MEM, 2 * D_MEM), F32)],
        input_output_aliases={3: 0},
        compiler_params=_cparams("arbitrary"),
    )(h, mkv, dy, dh)


def _forget_fwd(fl, bias, *, tb, name):
    S = fl.shape[0]
    tb = min(tb, S)

    def body(fl_ref, b_ref, o_ref, carry_ref):
        @pl.when(pl.program_id(0) == 0)
        def _():
            carry_ref[...] = jnp.zeros_like(carry_ref)

        z = fl_ref[...] + b_ref[...]
        lf = jnp.minimum(z, 0.0) - jnp.log(1.0 + jnp.exp(-jnp.abs(z)))
        row = lax.broadcasted_iota(jnp.int32, (tb, LANES), 0)
        c = lf
        sh = 1
        while sh < tb:
            c = c + jnp.where(row >= sh, pltpu.roll(c, sh, 0), 0.0)
            sh *= 2
        o_ref[...] = -(carry_ref[...] + c)
        carry_ref[...] += jnp.sum(lf, axis=0, keepdims=True)

    return pl.pallas_call(
        body, name=name, grid=(S // tb,),
        in_specs=[pl.BlockSpec((tb, LANES), lambda i: (i, 0)), pl.BlockSpec((1, LANES), lambda i: (0, 0))],
        out_specs=pl.BlockSpec((tb, LANES), lambda i: (i, 0)),
        out_shape=jax.ShapeDtypeStruct((S, LANES), F32),
        scratch_shapes=[pltpu.VMEM((1, LANES), F32)],
        compiler_params=_cparams("arbitrary"),
    )(fl, bias)


def _forget_bwd(dnck, fl, bias, dh2, *, tb, name):
    S = fl.shape[0]
    tb = min(tb, S)
    nb = S // tb

    def body(d_ref, fl_ref, b_ref, dh_in, dh_ref, db_ref, carry_ref):
        del dh_in

        @pl.when(pl.program_id(0) == 0)
        def _():
            carry_ref[...] = jnp.zeros_like(carry_ref)
            db_ref[...] = jnp.zeros_like(db_ref)

        dcum = -d_ref[...]
        row = lax.broadcasted_iota(jnp.int32, (tb, LANES), 0)
        c = dcum
        sh = 1
        while sh < tb:
            c = c + jnp.where(row < tb - sh, pltpu.roll(c, tb - sh, 0), 0.0)
            sh *= 2
        dlf = carry_ref[...] + c
        carry_ref[...] += jnp.sum(dcum, axis=0, keepdims=True)
        z = fl_ref[...] + b_ref[...]
        lane = lax.broadcasted_iota(jnp.int32, (tb, LANES), 1)
        dfl = jnp.where(lane < FOX_HEADS, dlf / (1.0 + jnp.exp(z)), 0.0)
        db_ref[...] += jnp.sum(dfl, axis=0, keepdims=True)
        dh_ref[...] = dfl.astype(BF16)

    rev = lambda i: (nb - 1 - i, 0)
    return pl.pallas_call(
        body, name=name, grid=(nb,),
        in_specs=[pl.BlockSpec((tb, LANES), rev), pl.BlockSpec((tb, LANES), rev),
                  pl.BlockSpec((1, LANES), lambda i: (0, 0)), pl.BlockSpec(memory_space=pl.ANY)],
        out_specs=[pl.BlockSpec((tb, LANES), lambda i: (nb - 1 - i, 2 * D_MAIN // LANES)),
                   pl.BlockSpec((1, LANES), lambda i: (0, 0))],
        out_shape=[jax.ShapeDtypeStruct(dh2.shape, dh2.dtype), jax.ShapeDtypeStruct((1, LANES), F32)],
        scratch_shapes=[pltpu.VMEM((1, LANES), F32)],
        input_output_aliases={3: 0},
        compiler_params=_cparams("arbitrary"),
    )(dnck, fl, bias, dh2)


FOX_SCALE = FOX_HEAD_DIM ** -0.5


def _halves(x, first):
    return jnp.where(first, x, 0.0).astype(BF16), jnp.where(first, 0.0, x).astype(BF16)


def _fox_fwd(h, kv, nckT, *, tq, name):
    S = h.shape[0]
    tq = min(tq, S)
    nq = S // tq
    rep = tq // LANES

    def body(q_ref, kv_ref, nck_ref, y_ref, lse_ref, qm_ref, m_ref, l_ref, acc_ref):
        i, j = pl.program_id(1), pl.program_id(2)
        first = lax.broadcasted_iota(jnp.int32, (tq, LANES), 1) < FOX_HEAD_DIM

        @pl.when(j == 0)
        def _():
            qa, qb = _halves(q_ref[...] * FOX_SCALE, first)
            qm_ref[0] = qa
            qm_ref[1] = qb
            m_ref[...] = jnp.full_like(m_ref, NEG)
            l_ref[...] = jnp.zeros_like(l_ref)
            acc_ref[...] = jnp.zeros_like(acc_ref)

        def step(masked):
            k, v = kv_ref[:, :LANES], kv_ref[:, LANES:]
            for hh in range(2):
                s = lax.dot_general(qm_ref[hh], k, NT, preferred_element_type=F32) + nck_ref[0, hh:hh + 1, :]
                if masked:
                    r = lax.broadcasted_iota(jnp.int32, (tq, tq), 0)
                    c = lax.broadcasted_iota(jnp.int32, (tq, tq), 1)
                    s = jnp.where(c <= r, s, NEG)
                m_prev = m_ref[hh]
                m_new = jnp.maximum(m_prev, jnp.max(s, axis=1, keepdims=True))
                p = jnp.exp(s - jnp.tile(m_new, (1, rep)))
                alpha = jnp.exp(m_prev - m_new)
                l_ref[hh] = alpha * l_ref[hh] + jnp.sum(p, axis=1, keepdims=True)
                acc_ref[hh] = alpha * acc_ref[hh] + jnp.dot(p.astype(BF16), v, preferred_element_type=F32)
                m_ref[hh] = m_new

        @pl.when(j < i)
        def _():
            step(False)

        @pl.when(j == i)
        def _():
            step(True)
            y_ref[...] = jnp.where(first, acc_ref[0] / l_ref[0], acc_ref[1] / l_ref[1])
            for hh in range(2):
                lse = m_ref[hh] + jnp.log(l_ref[hh])
                lse_ref[0, hh:hh + 1, :] = lse.T[0:1, :]

    kj = lambda i, j: jnp.minimum(i, j)
    return pl.pallas_call(
        body, name=name, grid=(FOX_PAIRS, nq, nq),
        in_specs=[pl.BlockSpec((tq, LANES), lambda g, i, j: (i, g)),
                  pl.BlockSpec((tq, 2 * LANES), lambda g, i, j: (kj(i, j), g)),
                  pl.BlockSpec((1, 2, tq), lambda g, i, j: (g, 0, kj(i, j)))],
        out_specs=[pl.BlockSpec((tq, LANES), lambda g, i, j: (i, g)),
                   pl.BlockSpec((1, 2, tq), lambda g, i, j: (g, 0, i))],
        out_shape=[jax.ShapeDtypeStruct((S, D_MAIN), F32), jax.ShapeDtypeStruct((FOX_PAIRS, 2, S), F32)],
        scratch_shapes=[pltpu.VMEM((2, tq, LANES), BF16), pltpu.VMEM((2, tq, LANES), F32),
                        pltpu.VMEM((2, tq, LANES), F32), pltpu.VMEM((2, tq, LANES), F32)],
        compiler_params=_cparams("parallel", "arbitrary", "arbitrary"),
    )(h, kv, nckT)


def _fox_delta(dy, y, *, tq, name):
    S = dy.shape[0]
    tq = min(tq, S)

    def body(dy_ref, y_ref, o_ref):
        first = lax.broadcasted_iota(jnp.int32, (tq, LANES), 1) < FOX_HEAD_DIM
        prod = dy_ref[...] * y_ref[...]
        for hh in range(2):
            d = jnp.sum(jnp.where(first == (hh == 0), prod, 0.0), axis=1, keepdims=True)
            o_ref[0, hh:hh + 1, :] = jnp.broadcast_to(d, (tq, LANES)).T[0:1, :]

    return pl.pallas_call(
        body, name=name, grid=(FOX_PAIRS, S // tq),
        in_specs=[pl.BlockSpec((tq, LANES), lambda g, i: (i, g)), pl.BlockSpec((tq, LANES), lambda g, i: (i, g))],
        out_specs=pl.BlockSpec((1, 2, tq), lambda g, i: (g, 0, i)),
        out_shape=jax.ShapeDtypeStruct((FOX_PAIRS, 2, S), F32),
        compiler_params=_cparams("parallel", "parallel"),
    )(dy, y)


def _fox_bwd(h, kv, negcum, lseT, deltaT, dy, dh, *, tq, name):
    S = h.shape[0]
    tq = min(tq, S)
    nq = S // tq
    rep = tq // LANES

    def body(q_ref, kv_ref, nc_ref, lse_ref, dl_ref, dy_ref, dh_in,
             dq_ref, dkv_ref, dnck_ref, drow_ref, km_ref, ncol_ref, dq_acc, dk_acc, dv_acc, dn_acc):
        del dh_in
        g, j, i = pl.program_id(0), pl.program_id(1), pl.program_id(2)
        lane = lax.broadcasted_iota(jnp.int32, (tq, LANES), 1)
        first = lane < FOX_HEAD_DIM

        @pl.when((j == 0) & (i == 0))
        def _():
            dq_acc[...] = jnp.zeros_like(dq_acc)
            drow_ref[...] = jnp.zeros_like(drow_ref)

        @pl.when(i == j)
        def _():
            ka, kb = _halves(kv_ref[:, :LANES].astype(F32), first)
            km_ref[0] = ka
            km_ref[1] = kb
            nc = nc_ref[...]
            for hh in range(2):
                col = jnp.sum(jnp.where(lane == 2 * g + hh, nc, 0.0), axis=1, keepdims=True)
                ncol_ref[hh] = jnp.broadcast_to(col, (tq, LANES))
            dk_acc[...] = jnp.zeros_like(dk_acc)
            dv_acc[...] = jnp.zeros_like(dv_acc)
            dn_acc[...] = jnp.zeros_like(dn_acc)

        def step(masked):
            k, v = kv_ref[:, :LANES], kv_ref[:, LANES:]
            qm = _halves(q_ref[...] * FOX_SCALE, first)
            dym = _halves(dy_ref[...], first)
            rows = pl.ds(pl.multiple_of(i * tq, tq), tq)
            for hh in range(2):
                sT = lax.dot_general(k, qm[hh], NT, preferred_element_type=F32) + jnp.tile(ncol_ref[hh], (1, rep))
                if masked:
                    r = lax.broadcasted_iota(jnp.int32, (tq, tq), 0)
                    c = lax.broadcasted_iota(jnp.int32, (tq, tq), 1)
                    sT = jnp.where(r <= c, sT, NEG)
                pT = jnp.exp(sT - lse_ref[0, hh:hh + 1, :])
                dpT = lax.dot_general(v, dym[hh], NT, preferred_element_type=F32)
                dsT = pT * (dpT - dl_ref[0, hh:hh + 1, :])
                dn_acc[hh] += jnp.broadcast_to(jnp.sum(dsT, axis=1, keepdims=True), (tq, LANES))
                drow_ref[0, hh:hh + 1, rows] += jnp.sum(dsT, axis=0, keepdims=True)
                dsb = dsT.astype(BF16)
                dv_acc[...] += jnp.dot(pT.astype(BF16), dym[hh], preferred_element_type=F32)
                dk_acc[...] += jnp.dot(dsb, qm[hh], preferred_element_type=F32)
                dq_acc[rows, :] += lax.dot_general(dsb, km_ref[hh], TN, preferred_element_type=F32)

        @pl.when(i > j)
        def _():
            step(False)

        @pl.when(i == j)
        def _():
            step(True)

        @pl.when(i == nq - 1)
        def _():
            dkv_ref[:, :LANES] = dk_acc[...].astype(BF16)
            dkv_ref[:, LANES:] = dv_acc[...].astype(BF16)
            dnck_ref[...] = jnp.where(first, dn_acc[0], dn_acc[1])

        @pl.when((j == nq - 1) & (i == nq - 1))
        def _():
            dq_ref[...] = (dq_acc[...] * FOX_SCALE).astype(BF16)

    qi = lambda j, i: jnp.maximum(i, j)
    return pl.pallas_call(
        body, name=name, grid=(FOX_PAIRS, nq, nq),
        in_specs=[pl.BlockSpec((tq, LANES), lambda g, j, i: (qi(j, i), g)),
                  pl.BlockSpec((tq, 2 * LANES), lambda g, j, i: (j, g)),
                  pl.BlockSpec((tq, LANES), lambda g, j, i: (j, 0)),
                  pl.BlockSpec((1, 2, tq), lambda g, j, i: (g, 0, qi(j, i))),
                  pl.BlockSpec((1, 2, tq), lambda g, j, i: (g, 0, qi(j, i))),
                  pl.BlockSpec((tq, LANES), lambda g, j, i: (qi(j, i), g)),
                  pl.BlockSpec(memory_space=pl.ANY)],
        out_specs=[pl.BlockSpec((S, LANES), lambda g, j, i: (0, g)),
                   pl.BlockSpec((tq, 2 * LANES), lambda g, j, i: (j, g)),
                   pl.BlockSpec((tq, LANES), lambda g, j, i: (j, g)),
                   pl.BlockSpec((1, 2, S), lambda g, j, i: (g, 0, 0))],
        out_shape=[jax.ShapeDtypeStruct(dh.shape, dh.dtype), jax.ShapeDtypeStruct((S, 2 * D_MAIN + LANES), BF16),
                   jax.ShapeDtypeStruct((S, D_MAIN), F32), jax.ShapeDtypeStruct((FOX_PAIRS, 2, S), F32)],
        scratch_shapes=[pltpu.VMEM((2, tq, LANES), BF16), pltpu.VMEM((2, tq, LANES), F32),
                        pltpu.VMEM((S, LANES), F32), pltpu.VMEM((tq, LANES), F32),
                        pltpu.VMEM((tq, LANES), F32), pltpu.VMEM((2, tq, LANES), F32)],
        input_output_aliases={6: 0},
        compiler_params=_cparams("parallel", "arbitrary", "arbitrary"),
    )(h, kv, negcum, lseT, deltaT, dy, dh)


TB_ROWS = 256
TB_SEQ = 512


def _mixer_fwd_tail(ysrc, scale, h, mem, wmkv, wout, tag):
    mkv = _mm(mem, wmkv, mode="nn", out_dtype=F32, tm=256, tn=1024, tk=1024, name=tag + "_mkv")
    ymem = _memattn_fwd(h, mkv, tb=TB_SEQ, name=tag + "_mem_fwd")
    yg = _gate_fwd(ysrc, scale, h, ymem, tb=TB_ROWS, name=tag + "_gate_fwd")
    o = _mm(yg, wout, mode="nn", out_dtype=F32, tm=512, tn=1024, tk=D_MIX, name=tag + "_out")
    return mkv, ymem, yg, o


def _local_step(x, mem, target, win, wmkv, wout, pw, pscale, wkvp, wf, ln_g, ln_b, bias):
    S = x.shape[0]
    ones = jnp.ones((1, D_MAIN), F32)
    g0, b0, g1, b1 = ln_g[0:1], ln_b[0:1], ln_g[1:2], ln_b[1:2]

    h0 = _mm(x, win[0], mode="nn", out_dtype=F32, tm=512, tn=1024, tk=D_MODEL, name="l0_in")
    pm, mixed = _pool_fwd(h0, pw, tb=TB_SEQ, name="l0_pool_fwd")
    mkv0, ymem0, yg0, o0 = _mixer_fwd_tail(mixed, pscale, h0, mem, wmkv[0], wout[0], "l0")
    x1, xhat0, rstd0 = _ln_fwd(x, o0, g0, b0, tb=TB_ROWS, name="l0_ln")

    kv = _mm(x1, wkvp, mode="nn", out_dtype=BF16, tm=512, tn=1024, tk=D_MODEL, name="kv_proj")
    fl = _mm(x1, wf, mode="nn", out_dtype=F32, tm=512, tn=LANES, tk=D_MODEL, name="f_proj")
    negcum = _forget_fwd(fl, bias, tb=TB_SEQ, name="forget_fwd")
    nckT = negcum[:, :FOX_HEADS].T.reshape(FOX_PAIRS, 2, S)

    h1 = _mm(x1, win[1], mode="nn", out_dtype=F32, tm=512, tn=1024, tk=D_MODEL, name="l1_in")
    y1, lseT = _fox_fwd(h1, kv, nckT, tq=TB_SEQ, name="fox_fwd")
    mkv1, ymem1, yg1, o1 = _mixer_fwd_tail(y1, ones, h1, mem, wmkv[1], wout[1], "l1")
    dz1, dg1, db1, sq = _ln_loss(x1, o1, g1, b1, target, tb=TB_ROWS, name="l1_ln_loss")

    dwout1 = _mm(yg1, dz1, mode="tn", out_dtype=F32, tm=512, tn=1024, tk=512, name="l1_dwout")
    dyg1 = _mm(dz1, wout[1], mode="nt", out_dtype=F32, tm=512, tn=512, tk=D_MODEL, name="l1_dyg")
    dy1, dymem1, dh1 = _gate_bwd(dyg1, y1, ones, h1, ymem1, tb=TB_ROWS, name="l1_gate_bwd")
    deltaT = _fox_delta(dy1, y1, tq=TB_SEQ, name="fox_delta")
    dh1, dh2, dnckp, drowT = _fox_bwd(h1, kv, negcum, lseT, deltaT, dy1, dh1, tq=TB_SEQ, name="fox_bwd")
    dnck = dnckp.reshape(S, FOX_HEADS, FOX_HEAD_DIM)[:, :, 0] - drowT.reshape(FOX_HEADS, S).T
    dnck = jnp.pad(dnck, ((0, 0), (0, LANES - FOX_HEADS)))
    dh2, dbias = _forget_bwd(dnck, fl, bias, dh2, tb=TB_SEQ, name="forget_bwd")
    dh1, dmkv1 = _memattn_bwd(h1, mkv1, dymem1, dh1, tb=TB_SEQ, name="l1_mem_bwd")
    dwmkv1 = _mm(mem, dmkv1, mode="tn", out_dtype=F32, tm=512, tn=1024, tk=N_MEM, name="l1_dwmkv")
    dwin1 = _mm(x1, dh1, mode="tn", out_dtype=F32, tm=512, tn=1024, tk=512, name="l1_dwin")
    dwkvf = _mm(x1, dh2, mode="tn", out_dtype=F32, tm=512, tn=2 * D_MAIN + LANES, tk=512, name="dwkv")
    dx1 = _mm(dh1, win[1], mode="nt", out_dtype=F32, tm=512, tn=512, tk=1024, add=dz1, add_scale=ALPHA,
              name="l1_dx")
    wkvf = jnp.concatenate([wkvp, wf], axis=1)
    dx1 = _mm(dh2, wkvf, mode="nt", out_dtype=F32, tm=512, tn=512, tk=2 * D_MAIN + LANES, add=dx1, name="kv_dx")

    dz0, dg0, db0 = _ln_bwd(dx1, xhat0, rstd0, g0, tb=TB_ROWS, name="l0_ln_bwd")
    dwout0 = _mm(yg0, dz0, mode="tn", out_dtype=F32, tm=512, tn=1024, tk=512, name="l0_dwout")
    dyg0 = _mm(dz0, wout[0], mode="nt", out_dtype=F32, tm=512, tn=512, tk=D_MODEL, name="l0_dyg")
    dy0, dymem0, dh0 = _gate_bwd(dyg0, mixed, pscale, h0, ymem0, tb=TB_ROWS, name="l0_gate_bwd")
    dh0, dpw, dpscale = _pool_bwd(dy0, pm, mixed, pw, pscale, dh0, tb=TB_SEQ, name="l0_pool_bwd")
    dh0, dmkv0 = _memattn_bwd(h0, mkv0, dymem0, dh0, tb=TB_SEQ, name="l0_mem_bwd")
    dwmkv0 = _mm(mem, dmkv0, mode="tn", out_dtype=F32, tm=512, tn=1024, tk=N_MEM, name="l0_dwmkv")
    dwin0 = _mm(x, dh0, mode="tn", out_dtype=F32, tm=512, tn=1024, tk=512, name="l0_dwin")
    gx = _mm(dh0, win[0], mode="nt", out_dtype=F32, tm=512, tn=512, tk=1024, add=dz0, add_scale=ALPHA,
             name="l0_dx")

    dkv = dwkvf[:, :2 * D_MAIN].reshape(D_MODEL, FOX_PAIRS, 2, LANES)
    dwkv = jnp.concatenate([dkv[:, :, 0, :].reshape(D_MODEL, D_MAIN), dkv[:, :, 1, :].reshape(D_MODEL, D_MAIN),
                            dwkvf[:, 2 * D_MAIN:2 * D_MAIN + FOX_HEADS]], axis=1)
    grads = dict(w_in=jnp.stack([dwin0, dwin1]), w_mem_kv=jnp.stack([dwmkv0, dwmkv1]),
                 w_out=jnp.stack([dwout0, dwout1]), pool_w=dpw, pool_scale=dpscale, w_kv_shared=dwkv,
                 ln_g=jnp.concatenate([dg0, dg1]), ln_b=jnp.concatenate([db0, db1]),
                 b_forget=dbias[0, :FOX_HEADS])
    return sq, gx, grads


MESH_ID = pl.DeviceIdType.MESH
HBM = pl.BlockSpec(memory_space=pl.ANY)


def _place():
    return lax.axis_index("x"), lax.axis_index("y"), lax.axis_index("c")


def _slot(p):
    return 4 * p[0] + 2 * p[1] + p[2]


def _all_gather(big, small, *, name):
    def body(big_ref, small_ref, obig, osmall, send_sems, recv_sems, local_sems):
        x, y, c = _place()
        me, sibling = (x, y, c), (x, y, 1 - c)
        chips = [(1 - x, y), (x, 1 - y), (1 - x, 1 - y)]

        def copies(k, block, to, from_input=False):
            s = _slot(block)
            mk = lambda src, dst, kk: pltpu.make_async_remote_copy(
                src_ref=src, dst_ref=dst, send_sem=send_sems.at[kk], recv_sem=recv_sems.at[kk],
                device_id=to, device_id_type=MESH_ID)
            return (mk(big_ref if from_input else obig.at[s], obig.at[s], 2 * k),
                    mk(small_ref if from_input else osmall.at[s], osmall.at[s], 2 * k + 1))

        mine = (pltpu.make_async_copy(big_ref, obig.at[_slot(me)], local_sems.at[0]),
                pltpu.make_async_copy(small_ref, osmall.at[_slot(me)], local_sems.at[1]))
        for cp in mine:
            cp.start()
        first = [copies(0, me, sibling, True)] + [copies(1 + j, me, (*chip, c), True) for j, chip in enumerate(chips)]
        for pair in first:
            for cp in pair:
                cp.start()
        passed = [copies(4 + j, (*chip, c), sibling) for j, chip in enumerate(chips)]
        for j, chip in enumerate(chips):
            for cp in copies(1 + j, (*chip, c), me):
                cp.wait_recv()
            for cp in passed[j]:
                cp.start()
        for cp in copies(0, sibling, me):
            cp.wait_recv()
        for j, chip in enumerate(chips):
            for cp in copies(4 + j, (*chip, 1 - c), me):
                cp.wait_recv()
        for pair in first + passed:
            for cp in pair:
                cp.wait_send()
        for cp in mine:
            cp.wait()

    return pl.pallas_call(
        body, name=name, in_specs=[HBM, HBM], out_specs=[HBM, HBM],
        out_shape=[jax.ShapeDtypeStruct((N_DEV,) + big.shape, big.dtype),
                   jax.ShapeDtypeStruct((N_DEV,) + small.shape, small.dtype)],
        scratch_shapes=[pltpu.SemaphoreType.DMA((14,)), pltpu.SemaphoreType.DMA((14,)), pltpu.SemaphoreType.DMA((2,))],
    )(big, small)


def _exchange(big, small, *, name):
    def body(big_ref, small_ref, obig, osmall, send_sems, recv_sems, local_sems):
        x, y, c = _place()
        me = _slot((x, y, c))
        flip = lambda v, bit: 1 - v if bit else v
        started = []
        for k in range(1, N_DEV):
            peer = (flip(x, k & 4), flip(y, k & 2), flip(c, k & 1))
            ps = _slot(peer)
            for r, (src, dst) in enumerate(((big_ref, obig), (small_ref, osmall))):
                kk = 2 * (k - 1) + r
                cp = pltpu.make_async_remote_copy(
                    src_ref=src.at[ps], dst_ref=dst.at[me], send_sem=send_sems.at[kk], recv_sem=recv_sems.at[kk],
                    device_id=peer, device_id_type=MESH_ID)
                cp.start()
                started.append((cp, pltpu.make_async_remote_copy(
                    src_ref=src.at[ps], dst_ref=dst.at[ps], send_sem=send_sems.at[kk], recv_sem=recv_sems.at[kk],
                    device_id=peer, device_id_type=MESH_ID)))
        mine = (pltpu.make_async_copy(big_ref.at[me], obig.at[me], local_sems.at[0]),
                pltpu.make_async_copy(small_ref.at[me], osmall.at[me], local_sems.at[1]))
        for cp in mine:
            cp.start()
        for sent, landed in started:
            landed.wait_recv()
            sent.wait_send()
        for cp in mine:
            cp.wait()

    return pl.pallas_call(
        body, name=name, in_specs=[HBM, HBM], out_specs=[HBM, HBM],
        out_shape=[jax.ShapeDtypeStruct(big.shape, big.dtype), jax.ShapeDtypeStruct(small.shape, small.dtype)],
        scratch_shapes=[pltpu.SemaphoreType.DMA((14,)), pltpu.SemaphoreType.DMA((14,)), pltpu.SemaphoreType.DMA((2,))],
    )(big, small)


def _adamw(recv, w, m, v, *, tb, name):
    R = w.shape[0]
    tb = min(tb, R)

    def body(r_ref, w_ref, m_ref, v_ref, g_ref, d_ref, nm_ref, nv_ref):
        g = r_ref[0].astype(F32)
        for j in range(1, N_DEV):
            g = g + r_ref[j].astype(F32)
        nm = ADAM_B1 * m_ref[...] + (1.0 - ADAM_B1) * g
        nv = ADAM_B2 * v_ref[...] + (1.0 - ADAM_B2) * (g * g)
        m_hat = nm / (1.0 - ADAM_B1 ** ADAM_STEP)
        v_hat = nv / (1.0 - ADAM_B2 ** ADAM_STEP)
        g_ref[...] = g
        nm_ref[...] = nm
        nv_ref[...] = nv
        d_ref[...] = -ADAM_LR * (m_hat / (jnp.sqrt(v_hat) + ADAM_EPS) + ADAM_WD * w_ref[...])

    row = pl.BlockSpec((tb, LANES), lambda i: (i, 0))
    shp = jax.ShapeDtypeStruct((R, LANES), F32)
    return pl.pallas_call(
        body, name=name, grid=(R // tb,),
        in_specs=[pl.BlockSpec((N_DEV, tb, LANES), lambda i: (0, i, 0)), row, row, row],
        out_specs=[row, row, row, row], out_shape=[shp, shp, shp, shp],
        compiler_params=_cparams("parallel"),
    )(recv, w, m, v)


BIG = ("w_in", "w_mem_kv", "w_out", "pool_w", "w_kv_shared")
SMALL = ("pool_scale", "ln_g", "ln_b", "b_forget")
BIG_ROWS = 13824
SMALL_ROWS = 40
ADAM_TB = 512


def _flat(parts, rows):
    v = jnp.concatenate([p.reshape(-1) for p in parts])
    return jnp.pad(v, (0, rows * LANES - v.shape[0])).reshape(rows, LANES)


def _unflat(flat, shapes):
    v, out, off = flat.reshape(-1), [], 0
    for s in shapes:
        n = math.prod(s)
        out.append(v[off:off + n].reshape(s))
        off += n
    return out


def _by_dest(name, g):
    if name == "w_in":
        return g.reshape(2, D_MODEL, N_DEV, D_IN // N_DEV).transpose(2, 0, 1, 3).reshape(N_DEV, -1)
    if name in ("w_mem_kv", "w_out"):
        return g.reshape(2, N_DEV, g.shape[1] // N_DEV, g.shape[2]).transpose(1, 0, 2, 3).reshape(N_DEV, -1)
    if name == "pool_w":
        return g.reshape(4, N_DEV, POOL_GROUP // N_DEV, POOL_GROUP).transpose(1, 0, 2, 3).reshape(N_DEV, -1)
    if name == "w_kv_shared":
        return g.reshape(D_MODEL, N_DEV, -1).transpose(1, 0, 2).reshape(N_DEV, -1)
    if name == "pool_scale":
        return g.reshape(N_DEV, -1)
    return jnp.broadcast_to(g.reshape(1, -1), (N_DEV, g.size))


def _from_gathered(name, rows, shard_shape):
    a = rows.reshape((N_DEV,) + shard_shape)
    if name == "w_in":
        return a.transpose(1, 2, 0, 3).reshape(2, D_MODEL, D_IN)
    if name in ("w_mem_kv", "w_out"):
        return a.transpose(1, 0, 2, 3).reshape(2, -1, D_MODEL)
    if name == "pool_w":
        return a[:, 0].transpose(1, 0, 2, 3).reshape(4, POOL_GROUP, POOL_GROUP)
    if name == "w_kv_shared":
        return a.transpose(1, 0, 2).reshape(D_MODEL, -1)
    raise ValueError(name)


def kernel(x, mem, w_in, w_mem_kv, w_out, ln_g, ln_b, pool_w, pool_scale, w_kv_shared, b_forget, loss_target, m_w_in, m_w_mem_kv, m_w_out, m_ln_g, m_ln_b, m_pool_w, m_pool_scale, m_w_kv_shared, m_b_forget, v_w_in, v_w_mem_kv, v_w_out, v_ln_g, v_ln_b, v_pool_w, v_pool_scale, v_w_kv_shared, v_b_forget):
    w = dict(w_in=w_in, w_mem_kv=w_mem_kv, w_out=w_out, ln_g=ln_g, ln_b=ln_b, pool_w=pool_w,
             pool_scale=pool_scale, w_kv_shared=w_kv_shared, b_forget=b_forget)
    m = dict(w_in=m_w_in, w_mem_kv=m_w_mem_kv, w_out=m_w_out, ln_g=m_ln_g, ln_b=m_ln_b, pool_w=m_pool_w,
             pool_scale=m_pool_scale, w_kv_shared=m_w_kv_shared, b_forget=m_b_forget)
    v = dict(w_in=v_w_in, w_mem_kv=v_w_mem_kv, w_out=v_w_out, ln_g=v_ln_g, ln_b=v_ln_b, pool_w=v_pool_w,
             pool_scale=v_pool_scale, w_kv_shared=v_w_kv_shared, b_forget=v_b_forget)
    big_shapes = [w[n].shape for n in BIG]
    small_shapes = [w[n].shape for n in SMALL]

    gathered, gscale = _all_gather(_flat([w[n].astype(BF16) for n in BIG], BIG_ROWS),
                                   _flat([pool_scale], 8), name="gather_weights")
    full, off = {}, 0
    flat_g = gathered.reshape(N_DEV, -1)
    for n, s in zip(BIG, big_shapes):
        k = math.prod(s)
        full[n] = _from_gathered(n, flat_g[:, off:off + k], s)
        off += k
    pscale = gscale.reshape(N_DEV, -1)[:, :pool_scale.size].reshape(1, D_MAIN)
    wkv = full["w_kv_shared"]
    wkvp = jnp.stack([wkv[:, :D_MAIN].reshape(D_MODEL, FOX_PAIRS, LANES),
                      wkv[:, D_MAIN:2 * D_MAIN].reshape(D_MODEL, FOX_PAIRS, LANES)], axis=2).reshape(D_MODEL, 2 * D_MAIN)
    wf = jnp.pad(wkv[:, 2 * D_MAIN:], ((0, 0), (0, LANES - FOX_HEADS)))
    bias = jnp.pad(b_forget, (0, LANES - FOX_HEADS)).reshape(1, LANES)

    sq, gx, grads = _local_step(x[0], mem[0], loss_target[0], full["w_in"], full["w_mem_kv"], full["w_out"],
                                full["pool_w"], pscale, wkvp, wf, ln_g, ln_b, bias)
    loss = lax.psum((0.5 / D_MODEL) * jnp.sum(sq), ("x", "y", "c"))

    send_big = jnp.concatenate([_by_dest(n, grads[n]) for n in BIG], axis=1)
    send_big = jnp.pad(send_big, ((0, 0), (0, BIG_ROWS * LANES - send_big.shape[1]))).astype(BF16)
    send_small = jnp.concatenate([_by_dest(n, grads[n]) for n in SMALL], axis=1)
    send_small = jnp.pad(send_small, ((0, 0), (0, SMALL_ROWS * LANES - send_small.shape[1])))
    recv_big, recv_small = _exchange(send_big.reshape(N_DEV, BIG_ROWS, LANES),
                                     send_small.reshape(N_DEV, SMALL_ROWS, LANES), name="exchange_grads")

    outs = {}
    for names, shapes, recv, rows, tag in ((BIG, big_shapes, recv_big, BIG_ROWS, "adamw_big"),
                                           (SMALL, small_shapes, recv_small, SMALL_ROWS, "adamw_small")):
        res = _adamw(recv, _flat([w[n] for n in names], rows), _flat([m[n] for n in names], rows),
                     _flat([v[n] for n in names], rows), tb=ADAM_TB, name=tag)
        for kind, flat in zip(("grad", "delta", "new_m", "new_v"), res):
            for n, a in zip(names, _unflat(flat, shapes)):
                outs[kind, n] = a
    order = ("w_in", "w_mem_kv", "w_out", "ln_g", "ln_b", "pool_w", "pool_scale", "w_kv_shared", "b_forget")
    return (loss, gx[None], *[outs[kind, n] for kind in ("grad", "delta", "new_m", "new_v") for n in order])
```

```python
import functools
import math

import jax
import jax.numpy as jnp
from jax import lax
from jax.experimental import pallas as pl
from jax.experimental.pallas import tpu as pltpu

F32 = jnp.float32
BF16 = jnp.bfloat16

D_MODEL = 1024
D_MAIN = 1024
D_MEM = 512
D_MIX = D_MAIN + D_MEM
D_IN = 2 * D_MIX
N_MEM = 256
MEM_HEADS = 4
MEM_HEAD_DIM = 128
FOX_HEADS = 16
FOX_HEAD_DIM = 64
FOX_PAIRS = FOX_HEADS // 2
POOL_WINDOWS = (2, 4, 8, 16)
POOL_GROUP = 256
POOL_HALO = 16
ALPHA = 4.0 ** 0.25
LN_EPS = 1e-5
NEG = -1e30
LANES = 128
N_DEV = 8

ADAM_LR = 0.001
ADAM_B1 = 0.9
ADAM_B2 = 0.999
ADAM_EPS = 1e-08
ADAM_WD = 0.01
ADAM_STEP = 10

VMEM_LIMIT = 56 * 1024 * 1024

NN = (((1,), (0,)), ((), ()))
NT = (((1,), (1,)), ((), ()))
TN = (((0,), (0,)), ((), ()))


def _cparams(*sem):
    return pltpu.CompilerParams(dimension_semantics=sem, vmem_limit_bytes=VMEM_LIMIT)


def _sigmoid(z):
    return 1.0 / (1.0 + jnp.exp(-z))


def _mm(a, b, *, mode, out_dtype, tm, tn, tk, name, add=None, add_scale=1.0):
    if mode == "nn":
        (M, K), (K2, N) = a.shape, b.shape
    elif mode == "nt":
        (M, K), (N, K2) = a.shape, b.shape
    else:
        (K, M), (K2, N) = a.shape, b.shape
    assert K == K2, (a.shape, b.shape, mode)
    tm, tn, tk = min(tm, M), min(tn, N), min(tk, K)
    assert M % tm == 0 and N % tn == 0 and K % tk == 0, (M, N, K, tm, tn, tk)
    gm, gn, gk = M // tm, N // tn, K // tk
    dims = {"nn": NN, "nt": NT, "tn": TN}[mode]
    if mode == "tn":
        a_spec = pl.BlockSpec((tk, tm), lambda i, j, k: (k, i))
    else:
        a_spec = pl.BlockSpec((tm, tk), lambda i, j, k: (i, k))
    if mode == "nt":
        b_spec = pl.BlockSpec((tn, tk), lambda i, j, k: (j, k))
    else:
        b_spec = pl.BlockSpec((tk, tn), lambda i, j, k: (k, j))
    o_spec = pl.BlockSpec((tm, tn), lambda i, j, k: (i, j))
    has_add = add is not None
    acc_in_out = out_dtype == F32

    def body(*refs):
        a_ref, b_ref = refs[0], refs[1]
        add_ref = refs[2] if has_add else None
        o_ref = refs[3] if has_add else refs[2]
        prod = lax.dot_general(a_ref[...].astype(BF16), b_ref[...].astype(BF16), dims,
                               preferred_element_type=F32)

        def finish(r):
            if has_add:
                r = r + add_scale * add_ref[...]
            o_ref[...] = r.astype(out_dtype)

        if gk == 1:
            finish(prod)
        else:
            acc_ref = o_ref if acc_in_out else refs[-1]
            k = pl.program_id(2)

            @pl.when(k == 0)
            def _():
                acc_ref[...] = prod

            @pl.when(k > 0)
            def _():
                acc_ref[...] += prod

            if has_add or not acc_in_out:
                @pl.when(k == gk - 1)
                def _():
                    finish(acc_ref[...])

    in_specs = [a_spec, b_spec] + ([o_spec] if has_add else [])
    args = (a, b) + ((add,) if has_add else ())
    return pl.pallas_call(
        body, name=name, grid=(gm, gn, gk), in_specs=in_specs, out_specs=o_spec,
        out_shape=jax.ShapeDtypeStruct((M, N), out_dtype),
        scratch_shapes=[pltpu.VMEM((tm, tn), F32)] if gk > 1 and not acc_in_out else [],
        compiler_params=_cparams("parallel", "parallel", "arbitrary"),
    )(*args)


def _ln_fwd(x, o, g, b, *, tb, name):
    S = x.shape[0]
    tb = min(tb, S)

    def body(x_ref, o_ref, g_ref, b_ref, y_ref, xhat_ref, rstd_ref):
        z = ALPHA * x_ref[...] + o_ref[...]
        mu = jnp.mean(z, axis=1, keepdims=True)
        zc = z - mu
        var = jnp.mean(zc * zc, axis=1, keepdims=True)
        rstd = lax.rsqrt(var + LN_EPS)
        xhat = zc * rstd
        xhat_ref[...] = xhat
        rstd_ref[...] = rstd
        y_ref[...] = xhat * g_ref[...] + b_ref[...]

    row = pl.BlockSpec((tb, D_MODEL), lambda i: (i, 0))
    vec = pl.BlockSpec((1, D_MODEL), lambda i: (0, 0))
    return pl.pallas_call(
        body, name=name, grid=(S // tb,), in_specs=[row, row, vec, vec],
        out_specs=[row, row, pl.BlockSpec((tb, 1), lambda i: (i, 0))],
        out_shape=[jax.ShapeDtypeStruct((S, D_MODEL), F32), jax.ShapeDtypeStruct((S, D_MODEL), F32),
                   jax.ShapeDtypeStruct((S, 1), F32)],
        compiler_params=_cparams("parallel"),
    )(x, o, g, b)


def _ln_bwd_math(dy, xhat, rstd, g):
    dxh = dy * g
    m1 = jnp.mean(dxh, axis=1, keepdims=True)
    m2 = jnp.mean(dxh * xhat, axis=1, keepdims=True)
    return rstd * (dxh - m1 - xhat * m2)


def _ln_bwd(dy, xhat, rstd, g, *, tb, name):
    S = dy.shape[0]
    tb = min(tb, S)

    def body(dy_ref, xhat_ref, rstd_ref, g_ref, dz_ref, dg_ref, db_ref):
        @pl.when(pl.program_id(0) == 0)
        def _():
            dg_ref[...] = jnp.zeros_like(dg_ref)
            db_ref[...] = jnp.zeros_like(db_ref)

        dy_, xhat_ = dy_ref[...], xhat_ref[...]
        dz_ref[...] = _ln_bwd_math(dy_, xhat_, rstd_ref[...], g_ref[...])
        dg_ref[...] += jnp.sum(dy_ * xhat_, axis=0, keepdims=True)
        db_ref[...] += jnp.sum(dy_, axis=0, keepdims=True)

    row = pl.BlockSpec((tb, D_MODEL), lambda i: (i, 0))
    vec = pl.BlockSpec((1, D_MODEL), lambda i: (0, 0))
    return pl.pallas_call(
        body, name=name, grid=(S // tb,),
        in_specs=[row, row, pl.BlockSpec((tb, 1), lambda i: (i, 0)), vec],
        out_specs=[row, vec, vec],
        out_shape=[jax.ShapeDtypeStruct((S, D_MODEL), F32), jax.ShapeDtypeStruct((1, D_MODEL), F32),
                   jax.ShapeDtypeStruct((1, D_MODEL), F32)],
        compiler_params=_cparams("arbitrary"),
    )(dy, xhat, rstd, g)


def _ln_loss(x, o, g, b, target, *, tb, name):
    S = x.shape[0]
    tb = min(tb, S)

    def body(x_ref, o_ref, g_ref, b_ref, t_ref, dz_ref, dg_ref, db_ref, sq_ref):
        @pl.when(pl.program_id(0) == 0)
        def _():
            dg_ref[...] = jnp.zeros_like(dg_ref)
            db_ref[...] = jnp.zeros_like(db_ref)
            sq_ref[...] = jnp.zeros_like(sq_ref)

        z = ALPHA * x_ref[...] + o_ref[...]
        mu = jnp.mean(z, axis=1, keepdims=True)
        zc = z - mu
        var = jnp.mean(zc * zc, axis=1, keepdims=True)
        rstd = lax.rsqrt(var + LN_EPS)
        xhat = zc * rstd
        err = xhat * g_ref[...] + b_ref[...] - t_ref[...]
        sq_ref[...] += jnp.sum(err * err, axis=0, keepdims=True)
        dy = err * (1.0 / D_MODEL)
        dz_ref[...] = _ln_bwd_math(dy, xhat, rstd, g_ref[...])
        dg_ref[...] += jnp.sum(dy * xhat, axis=0, keepdims=True)
        db_ref[...] += jnp.sum(dy, axis=0, keepdims=True)

    row = pl.BlockSpec((tb, D_MODEL), lambda i: (i, 0))
    vec = pl.BlockSpec((1, D_MODEL), lambda i: (0, 0))
    vshape = jax.ShapeDtypeStruct((1, D_MODEL), F32)
    return pl.pallas_call(
        body, name=name, grid=(S // tb,), in_specs=[row, row, vec, vec, row],
        out_specs=[row, vec, vec, vec],
        out_shape=[jax.ShapeDtypeStruct((S, D_MODEL), F32), vshape, vshape, vshape],
        compiler_params=_cparams("arbitrary"),
    )(x, o, g, b, target)


def _gate_fwd(ysrc, scale, h, ymem, *, tb, name):
    S = ysrc.shape[0]
    tb = min(tb, S)

    def body(ys_ref, sc_ref, ga_ref, gb_ref, gc_ref, ym_ref, yg_ref):
        ymain = ys_ref[...] * sc_ref[...]
        for k, g_ref in enumerate((ga_ref, gb_ref)):
            gv = g_ref[...]
            yg_ref[:, 512 * k:512 * (k + 1)] = (ymain[:, 512 * k:512 * (k + 1)] * gv * _sigmoid(gv)).astype(BF16)
        gv = gc_ref[...]
        yg_ref[:, 1024:1536] = (ym_ref[...] * gv * _sigmoid(gv)).astype(BF16)

    slab = lambda c: pl.BlockSpec((tb, 512), lambda i, c=c: (i, c))
    return pl.pallas_call(
        body, name=name, grid=(S // tb,),
        in_specs=[pl.BlockSpec((tb, D_MAIN), lambda i: (i, 0)), pl.BlockSpec((1, D_MAIN), lambda i: (0, 0)),
                  slab(3), slab(4), slab(5), pl.BlockSpec((tb, D_MEM), lambda i: (i, 0))],
        out_specs=pl.BlockSpec((tb, D_MIX), lambda i: (i, 0)),
        out_shape=jax.ShapeDtypeStruct((S, D_MIX), BF16),
        compiler_params=_cparams("parallel"),
    )(ysrc, scale, h, h, h, ymem)


def _gate_bwd(dyg, ysrc, scale, h, ymem, *, tb, name):
    S = ysrc.shape[0]
    tb = min(tb, S)

    def dsilu(gv):
        sg = _sigmoid(gv)
        return sg, sg * (1.0 + gv * (1.0 - sg))

    def body(da_ref, db_ref, dc_ref, ys_ref, sc_ref, ga_ref, gb_ref, gc_ref, ym_ref, dym_ref, dymem_ref, dh_ref):
        ymain = ys_ref[...] * sc_ref[...]
        for k, (d_ref, g_ref) in enumerate(((da_ref, ga_ref), (db_ref, gb_ref))):
            gv, d = g_ref[...], d_ref[...]
            sg, ds = dsilu(gv)
            dym_ref[:, 512 * k:512 * (k + 1)] = d * gv * sg
            dh_ref[:, 512 * k:512 * (k + 1)] = (d * ymain[:, 512 * k:512 * (k + 1)] * ds).astype(BF16)
        gv, d = gc_ref[...], dc_ref[...]
        sg, ds = dsilu(gv)
        dymem_ref[...] = d * gv * sg
        dh_ref[:, 1024:1536] = (d * ym_ref[...] * ds).astype(BF16)

    slab = lambda c: pl.BlockSpec((tb, 512), lambda i, c=c: (i, c))
    return pl.pallas_call(
        body, name=name, grid=(S // tb,),
        in_specs=[slab(0), slab(1), slab(2),
                  pl.BlockSpec((tb, D_MAIN), lambda i: (i, 0)), pl.BlockSpec((1, D_MAIN), lambda i: (0, 0)),
                  slab(3), slab(4), slab(5), pl.BlockSpec((tb, D_MEM), lambda i: (i, 0))],
        out_specs=[pl.BlockSpec((tb, D_MAIN), lambda i: (i, 0)), pl.BlockSpec((tb, D_MEM), lambda i: (i, 0)),
                   pl.BlockSpec((tb, D_MIX), lambda i: (i, 1))],
        out_shape=[jax.ShapeDtypeStruct((S, D_MAIN), F32), jax.ShapeDtypeStruct((S, D_MEM), F32),
                   jax.ShapeDtypeStruct((S, D_IN), BF16)],
        compiler_params=_cparams("parallel"),
    )(dyg, dyg, dyg, ysrc, scale, h, h, h, ymem)


def _window_count(t0, rows, w):
    t = t0 + lax.broadcasted_iota(jnp.int32, (rows, POOL_GROUP), 0)
    return jnp.minimum(t + 1, w).astype(F32)


def _pool_fwd(h, pw, *, tb, name):
    S = h.shape[0]
    tb = min(tb, S)
    n = tb + POOL_HALO

    def body(u_ref, pw_ref, pm_ref, mixed_ref, tail_ref):
        i = pl.program_id(0)

        @pl.when(i == 0)
        def _():
            tail_ref[...] = jnp.zeros_like(tail_ref)

        u = u_ref[...]
        xfull = jnp.concatenate([tail_ref[...], u], axis=0)
        for gi, w in enumerate(POOL_WINDOWS):
            cols = slice(POOL_GROUP * gi, POOL_GROUP * (gi + 1))
            s = xfull[:, cols]
            sh = 1
            while sh < w:
                s = s + pltpu.roll(s, sh, 0)
                sh *= 2
            pm = s[POOL_HALO:, :] / _window_count(i * tb, tb, w) - u[:, cols]
            pmb = pm.astype(BF16)
            pm_ref[:, cols] = pmb
            mixed_ref[:, cols] = jnp.dot(pmb, pw_ref[gi], preferred_element_type=F32)
        tail_ref[...] = u[tb - POOL_HALO:, :]

    return pl.pallas_call(
        body, name=name, grid=(S // tb,),
        in_specs=[pl.BlockSpec((tb, D_MAIN), lambda i: (i, 0)),
                  pl.BlockSpec((4, POOL_GROUP, POOL_GROUP), lambda i: (0, 0, 0))],
        out_specs=[pl.BlockSpec((tb, D_MAIN), lambda i: (i, 0)), pl.BlockSpec((tb, D_MAIN), lambda i: (i, 0))],
        out_shape=[jax.ShapeDtypeStruct((S, D_MAIN), BF16), jax.ShapeDtypeStruct((S, D_MAIN), F32)],
        scratch_shapes=[pltpu.VMEM((POOL_HALO, D_MAIN), F32)],
        compiler_params=_cparams("arbitrary"),
    )(h, pw)


def _pool_bwd(dymain, pm, mixed, pw, scale, dh, *, tb, name):
    S = dymain.shape[0]
    tb = min(tb, S)
    nb = S // tb
    n = tb + POOL_HALO

    def body(dy_ref, pm_ref, mixed_ref, pw_ref, sc_ref, dh_in, dh_ref, dpw_ref, dsc_ref, head_ref):
        del dh_in
        i = pl.program_id(0)

        @pl.when(i == 0)
        def _():
            head_ref[...] = jnp.zeros_like(head_ref)
            dpw_ref[...] = jnp.zeros_like(dpw_ref)
            dsc_ref[...] = jnp.zeros_like(dsc_ref)

        dy = dy_ref[...]
        dsc_ref[...] += jnp.sum(dy * mixed_ref[...], axis=0, keepdims=True)
        dmixed = dy * sc_ref[...]
        t0 = (nb - 1 - i) * tb
        for gi, w in enumerate(POOL_WINDOWS):
            cols = slice(POOL_GROUP * gi, POOL_GROUP * (gi + 1))
            dm = dmixed[:, cols].astype(BF16)
            dpw_ref[gi] += lax.dot_general(pm_ref[:, cols], dm, TN, preferred_element_type=F32)
            dpm = lax.dot_general(dm, pw_ref[gi], NT, preferred_element_type=F32)
            e = dpm / _window_count(t0, tb, w)
            s = jnp.concatenate([e, head_ref[:, cols]], axis=0)
            sh = 1
            while sh < w:
                s = s + pltpu.roll(s, n - sh, 0)
                sh *= 2
            dh_ref[:, cols] = (s[:tb, :] - dpm).astype(BF16)
            head_ref[:, cols] = e[:POOL_HALO, :]

    rev = lambda i: (nb - 1 - i, 0)
    return pl.pallas_call(
        body, name=name, grid=(nb,),
        in_specs=[pl.BlockSpec((tb, D_MAIN), rev), pl.BlockSpec((tb, D_MAIN), rev), pl.BlockSpec((tb, D_MAIN), rev),
                  pl.BlockSpec((4, POOL_GROUP, POOL_GROUP), lambda i: (0, 0, 0)),
                  pl.BlockSpec((1, D_MAIN), lambda i: (0, 0)), pl.BlockSpec(memory_space=pl.ANY)],
        out_specs=[pl.BlockSpec((tb, D_MAIN), rev),
                   pl.BlockSpec((4, POOL_GROUP, POOL_GROUP), lambda i: (0, 0, 0)),
                   pl.BlockSpec((1, D_MAIN), lambda i: (0, 0))],
        out_shape=[jax.ShapeDtypeStruct(dh.shape, dh.dtype),
                   jax.ShapeDtypeStruct((4, POOL_GROUP, POOL_GROUP), F32), jax.ShapeDtypeStruct((1, D_MAIN), F32)],
        scratch_shapes=[pltpu.VMEM((POOL_HALO, D_MAIN), F32)],
        input_output_aliases={5: 0},
        compiler_params=_cparams("arbitrary"),
    )(dymain, pm, mixed, pw, scale, dh)


MEM_SCALE = MEM_HEAD_DIM ** -0.5


def _mem_probs(q_ref, mkv_ref, hd):
    cols = slice(MEM_HEAD_DIM * hd, MEM_HEAD_DIM * (hd + 1))
    q = (q_ref[:, cols] * MEM_SCALE).astype(BF16)
    mk = mkv_ref[:, cols].astype(BF16)
    mv = mkv_ref[:, D_MEM + MEM_HEAD_DIM * hd:D_MEM + MEM_HEAD_DIM * (hd + 1)].astype(BF16)
    s = lax.dot_general(q, mk, NT, preferred_element_type=F32)
    e = jnp.exp(s - jnp.max(s, axis=1, keepdims=True))
    return cols, q, mk, mv, e, jnp.sum(e, axis=1, keepdims=True)


def _memattn_fwd(h, mkv, *, tb, name):
    S = h.shape[0]
    tb = min(tb, S)

    def body(q_ref, mkv_ref, y_ref):
        for hd in range(MEM_HEADS):
            cols, _, _, mv, e, l = _mem_probs(q_ref, mkv_ref, hd)
            y_ref[:, cols] = jnp.dot(e.astype(BF16), mv, preferred_element_type=F32) / l

    return pl.pallas_call(
        body, name=name, grid=(S // tb,),
        in_specs=[pl.BlockSpec((tb, D_MEM), lambda i: (i, 2)), pl.BlockSpec((N_MEM, 2 * D_MEM), lambda i: (0, 0))],
        out_specs=pl.BlockSpec((tb, D_MEM), lambda i: (i, 0)),
        out_shape=jax.ShapeDtypeStruct((S, D_MEM), F32),
        compiler_params=_cparams("parallel"),
    )(h, mkv)


def _memattn_bwd(h, mkv, dy, dh, *, tb, name):
    S = h.shape[0]
    tb = min(tb, S)

    def body(q_ref, mkv_ref, dy_ref, dh_in, dh_ref, dmkv_ref):
        del dh_in

        @pl.when(pl.program_id(0) == 0)
        def _():
            dmkv_ref[...] = jnp.zeros_like(dmkv_ref)

        for hd in range(MEM_HEADS):
            cols, q, mk, mv, e, l = _mem_probs(q_ref, mkv_ref, hd)
            p = e / l
            dyh = dy_ref[:, cols].astype(BF16)
            dp = lax.dot_general(dyh, mv, NT, preferred_element_type=F32)
            ds = p * (dp - jnp.sum(dp * p, axis=1, keepdims=True))
            dsb = ds.astype(BF16)
            dh_ref[:, cols] = (jnp.dot(dsb, mk, preferred_element_type=F32) * MEM_SCALE).astype(BF16)
            dmkv_ref[:, cols] += lax.dot_general(dsb, q, TN, preferred_element_type=F32)
            vcols = slice(D_MEM + MEM_HEAD_DIM * hd, D_MEM + MEM_HEAD_DIM * (hd + 1))
            dmkv_ref[:, vcols] += lax.dot_general(p.astype(BF16), dyh, TN, preferred_element_type=F32)

    return pl.pallas_call(
        body, name=name, grid=(S // tb,),
        in_specs=[pl.BlockSpec((tb, D_MEM), lambda i: (i, 2)), pl.BlockSpec((N_MEM, 2 * D_MEM), lambda i: (0, 0)),
                  pl.BlockSpec((tb, D_MEM), lambda i: (i, 0)), pl.BlockSpec(memory_space=pl.ANY)],
        out_specs=[pl.BlockSpec((tb, D_MEM), lambda i: (i, 2)), pl.BlockSpec((N_MEM, 2 * D_MEM), lambda i: (0, 0))],
        out_shape=[jax.ShapeDtypeStruct(dh.shape, dh.dtype), jax.ShapeDtypeStruct((N_MEM, 2 * D_MEM), F32)],
        input_output_aliases={3: 0},
        compiler_params=_cparams("arbitrary"),
    )(h, mkv, dy, dh)


def _forget_fwd(fl, bias, *, tb, name):
    S = fl.shape[0]
    tb = min(tb, S)

    def body(fl_ref, b_ref, o_ref, carry_ref):
        @pl.when(pl.program_id(0) == 0)
        def _():
            carry_ref[...] = jnp.zeros_like(carry_ref)

        z = fl_ref[...] + b_ref[...]
        lf = jnp.minimum(z, 0.0) - jnp.log(1.0 + jnp.exp(-jnp.abs(z)))
        row = lax.broadcasted_iota(jnp.int32, (tb, LANES), 0)
        c = lf
        sh = 1
        while sh < tb:
            c = c + jnp.where(row >= sh, pltpu.roll(c, sh, 0), 0.0)
            sh *= 2
        o_ref[...] = -(carry_ref[...] + c)
        carry_ref[...] += jnp.sum(lf, axis=0, keepdims=True)

    return pl.pallas_call(
        body, name=name, grid=(S // tb,),
        in_specs=[pl.BlockSpec((tb, LANES), lambda i: (i, 0)), pl.BlockSpec((1, LANES), lambda i: (0, 0))],
        out_specs=pl.BlockSpec((tb, LANES), lambda i: (i, 0)),
        out_shape=jax.ShapeDtypeStruct((S, LANES), F32),
        scratch_shapes=[pltpu.VMEM((1, LANES), F32)],
        compiler_params=_cparams("arbitrary"),
    )(fl, bias)


def _forget_bwd(dnck, fl, bias, dh2, *, tb, name):
    S = fl.shape[0]
    tb = min(tb, S)
    nb = S // tb

    def body(d_ref, fl_ref, b_ref, dh_in, dh_ref, db_ref, carry_ref):
        del dh_in

        @pl.when(pl.program_id(0) == 0)
        def _():
            carry_ref[...] = jnp.zeros_like(carry_ref)
            db_ref[...] = jnp.zeros_like(db_ref)

        dcum = -d_ref[...]
        row = lax.broadcasted_iota(jnp.int32, (tb, LANES), 0)
        c = dcum
        sh = 1
        while sh < tb:
            c = c + jnp.where(row < tb - sh, pltpu.roll(c, tb - sh, 0), 0.0)
            sh *= 2
        dlf = carry_ref[...] + c
        carry_ref[...] += jnp.sum(dcum, axis=0, keepdims=True)
        z = fl_ref[...] + b_ref[...]
        lane = lax.broadcasted_iota(jnp.int32, (tb, LANES), 1)
        dfl = jnp.where(lane < FOX_HEADS, dlf / (1.0 + jnp.exp(z)), 0.0)
        db_ref[...] += jnp.sum(dfl, axis=0, keepdims=True)
        dh_ref[...] = dfl.astype(BF16)

    rev = lambda i: (nb - 1 - i, 0)
    return pl.pallas_call(
        body, name=name, grid=(nb,),
        in_specs=[pl.BlockSpec((tb, LANES), rev), pl.BlockSpec((tb, LANES), rev),
                  pl.BlockSpec((1, LANES), lambda i: (0, 0)), pl.BlockSpec(memory_space=pl.ANY)],
        out_specs=[pl.BlockSpec((tb, LANES), lambda i: (nb - 1 - i, 2 * D_MAIN // LANES)),
                   pl.BlockSpec((1, LANES), lambda i: (0, 0))],
        out_shape=[jax.ShapeDtypeStruct(dh2.shape, dh2.dtype), jax.ShapeDtypeStruct((1, LANES), F32)],
        scratch_shapes=[pltpu.VMEM((1, LANES), F32)],
        input_output_aliases={3: 0},
        compiler_params=_cparams("arbitrary"),
    )(dnck, fl, bias, dh2)


FOX_SCALE = FOX_HEAD_DIM ** -0.5


def _halves(x, first):
    return jnp.where(first, x, 0.0).astype(BF16), jnp.where(first, 0.0, x).astype(BF16)


def _fox_fwd(h, kv, nckT, *, tq, name):
    S = h.shape[0]
    tq = min(tq, S)
    nq = S // tq
    rep = tq // LANES

    def body(q_ref, kv_ref, nck_ref, y_ref, lse_ref, qm_ref, m_ref, l_ref, acc_ref):
        i, j = pl.program_id(1), pl.program_id(2)
        first = lax.broadcasted_iota(jnp.int32, (tq, LANES), 1) < FOX_HEAD_DIM

        @pl.when(j == 0)
        def _():
            qa, qb = _halves(q_ref[...] * FOX_SCALE, first)
            qm_ref[0] = qa
            qm_ref[1] = qb
            m_ref[...] = jnp.full_like(m_ref, NEG)
            l_ref[...] = jnp.zeros_like(l_ref)
            acc_ref[...] = jnp.zeros_like(acc_ref)

        def step(masked):
            k, v = kv_ref[:, :LANES], kv_ref[:, LANES:]
            for hh in range(2):
                s = lax.dot_general(qm_ref[hh], k, NT, preferred_element_type=F32) + nck_ref[0, hh:hh + 1, :]
                if masked:
                    r = lax.broadcasted_iota(jnp.int32, (tq, tq), 0)
                    c = lax.broadcasted_iota(jnp.int32, (tq, tq), 1)
                    s = jnp.where(c <= r, s, NEG)
                m_prev = m_ref[hh]
                m_new = jnp.maximum(m_prev, jnp.max(s, axis=1, keepdims=True))
                p = jnp.exp(s - jnp.tile(m_new, (1, rep)))
                alpha = jnp.exp(m_prev - m_new)
                l_ref[hh] = alpha * l_ref[hh] + jnp.sum(p, axis=1, keepdims=True)
                acc_ref[hh] = alpha * acc_ref[hh] + jnp.dot(p.astype(BF16), v, preferred_element_type=F32)
                m_ref[hh] = m_new

        @pl.when(j < i)
        def _():
            step(False)

        @pl.when(j == i)
        def _():
            step(True)
            y_ref[...] = jnp.where(first, acc_ref[0] / l_ref[0], acc_ref[1] / l_ref[1])
            for hh in range(2):
                lse = m_ref[hh] + jnp.log(l_ref[hh])
                lse_ref[0, hh:hh + 1, :] = lse.T[0:1, :]

    kj = lambda i, j: jnp.minimum(i, j)
    return pl.pallas_call(
        body, name=name, grid=(FOX_PAIRS, nq, nq),
        in_specs=[pl.BlockSpec((tq, LANES), lambda g, i, j: (i, g)),
                  pl.BlockSpec((tq, 2 * LANES), lambda g, i, j: (kj(i, j), g)),
                  pl.BlockSpec((1, 2, tq), lambda g, i, j: (g, 0, kj(i, j)))],
        out_specs=[pl.BlockSpec((tq, LANES), lambda g, i, j: (i, g)),
                   pl.BlockSpec((1, 2, tq), lambda g, i, j: (g, 0, i))],
        out_shape=[jax.ShapeDtypeStruct((S, D_MAIN), F32), jax.ShapeDtypeStruct((FOX_PAIRS, 2, S), F32)],
        scratch_shapes=[pltpu.VMEM((2, tq, LANES), BF16), pltpu.VMEM((2, tq, LANES), F32),
                        pltpu.VMEM((2, tq, LANES), F32), pltpu.VMEM((2, tq, LANES), F32)],
        compiler_params=_cparams("parallel", "arbitrary", "arbitrary"),
    )(h, kv, nckT)


def _fox_delta(dy, y, *, tq, name):
    S = dy.shape[0]
    tq = min(tq, S)

    def body(dy_ref, y_ref, o_ref):
        first = lax.broadcasted_iota(jnp.int32, (tq, LANES), 1) < FOX_HEAD_DIM
        prod = dy_ref[...] * y_ref[...]
        for hh in range(2):
            d = jnp.sum(jnp.where(first == (hh == 0), prod, 0.0), axis=1, keepdims=True)
            o_ref[0, hh:hh + 1, :] = jnp.broadcast_to(d, (tq, LANES)).T[0:1, :]

    return pl.pallas_call(
        body, name=name, grid=(FOX_PAIRS, S // tq),
        in_specs=[pl.BlockSpec((tq, LANES), lambda g, i: (i, g)), pl.BlockSpec((tq, LANES), lambda g, i: (i, g))],
        out_specs=pl.BlockSpec((1, 2, tq), lambda g, i: (g, 0, i)),
        out_shape=jax.ShapeDtypeStruct((FOX_PAIRS, 2, S), F32),
        compiler_params=_cparams("parallel", "parallel"),
    )(dy, y)


def _fox_bwd(h, kv, negcum, lseT, deltaT, dy, dh, *, tq, name):
    S = h.shape[0]
    tq = min(tq, S)
    nq = S // tq
    rep = tq // LANES

    def body(q_ref, kv_ref, nc_ref, lse_ref, dl_ref, dy_ref, dh_in,
             dq_ref, dkv_ref, dnck_ref, drow_ref, km_ref, ncol_ref, dq_acc, dk_acc, dv_acc, dn_acc):
        del dh_in
        g, j, i = pl.program_id(0), pl.program_id(1), pl.program_id(2)
        lane = lax.broadcasted_iota(jnp.int32, (tq, LANES), 1)
        first = lane < FOX_HEAD_DIM

        @pl.when((j == 0) & (i == 0))
        def _():
            dq_acc[...] = jnp.zeros_like(dq_acc)
            drow_ref[...] = jnp.zeros_like(drow_ref)

        @pl.when(i == j)
        def _():
            ka, kb = _halves(kv_ref[:, :LANES].astype(F32), first)
            km_ref[0] = ka
            km_ref[1] = kb
            nc = nc_ref[...]
            for hh in range(2):
                col = jnp.sum(jnp.where(lane == 2 * g + hh, nc, 0.0), axis=1, keepdims=True)
                ncol_ref[hh] = jnp.broadcast_to(col, (tq, LANES))
            dk_acc[...] = jnp.zeros_like(dk_acc)
            dv_acc[...] = jnp.zeros_like(dv_acc)
            dn_acc[...] = jnp.zeros_like(dn_acc)

        def step(masked):
            k, v = kv_ref[:, :LANES], kv_ref[:, LANES:]
            qm = _halves(q_ref[...] * FOX_SCALE, first)
            dym = _halves(dy_ref[...], first)
            rows = pl.ds(pl.multiple_of(i * tq, tq), tq)
            for hh in range(2):
                sT = lax.dot_general(k, qm[hh], NT, preferred_element_type=F32) + jnp.tile(ncol_ref[hh], (1, rep))
                if masked:
                    r = lax.broadcasted_iota(jnp.int32, (tq, tq), 0)
                    c = lax.broadcasted_iota(jnp.int32, (tq, tq), 1)
                    sT = jnp.where(r <= c, sT, NEG)
                pT = jnp.exp(sT - lse_ref[0, hh:hh + 1, :])
                dpT = lax.dot_general(v, dym[hh], NT, preferred_element_type=F32)
                dsT = pT * (dpT - dl_ref[0, hh:hh + 1, :])
                dn_acc[hh] += jnp.broadcast_to(jnp.sum(dsT, axis=1, keepdims=True), (tq, LANES))
                drow_ref[0, hh:hh + 1, rows] += jnp.sum(dsT, axis=0, keepdims=True)
                dsb = dsT.astype(BF16)
                dv_acc[...] += jnp.dot(pT.astype(BF16), dym[hh], preferred_element_type=F32)
                dk_acc[...] += jnp.dot(dsb, qm[hh], preferred_element_type=F32)
                dq_acc[rows, :] += lax.dot_general(dsb, km_ref[hh], TN, preferred_element_type=F32)

        @pl.when(i > j)
        def _():
            step(False)

        @pl.when(i == j)
        def _():
            step(True)

        @pl.when(i == nq - 1)
        def _():
            dkv_ref[:, :LANES] = dk_acc[...].astype(BF16)
            dkv_ref[:, LANES:] = dv_acc[...].astype(BF16)
            dnck_ref[...] = jnp.where(first, dn_acc[0], dn_acc[1])

        @pl.when((j == nq - 1) & (i == nq - 1))
        def _():
            dq_ref[...] = (dq_acc[...] * FOX_SCALE).astype(BF16)

    qi = lambda j, i: jnp.maximum(i, j)
    return pl.pallas_call(
        body, name=name, grid=(FOX_PAIRS, nq, nq),
        in_specs=[pl.BlockSpec((tq, LANES), lambda g, j, i: (qi(j, i), g)),
                  pl.BlockSpec((tq, 2 * LANES), lambda g, j, i: (j, g)),
                  pl.BlockSpec((tq, LANES), lambda g, j, i: (j, 0)),
                  pl.BlockSpec((1, 2, tq), lambda g, j, i: (g, 0, qi(j, i))),
                  pl.BlockSpec((1, 2, tq), lambda g, j, i: (g, 0, qi(j, i))),
                  pl.BlockSpec((tq, LANES), lambda g, j, i: (qi(j, i), g)),
                  pl.BlockSpec(memory_space=pl.ANY)],
        out_specs=[pl.BlockSpec((S, LANES), lambda g, j, i: (0, g)),
                   pl.BlockSpec((tq, 2 * LANES), lambda g, j, i: (j, g)),
                   pl.BlockSpec((tq, LANES), lambda g, j, i: (j, g)),
                   pl.BlockSpec((1, 2, S), lambda g, j, i: (g, 0, 0))],
        out_shape=[jax.ShapeDtypeStruct(dh.shape, dh.dtype), jax.ShapeDtypeStruct((S, 2 * D_MAIN + LANES), BF16),
                   jax.ShapeDtypeStruct((S, D_MAIN), F32), jax.ShapeDtypeStruct((FOX_PAIRS, 2, S), F32)],
        scratch_shapes=[pltpu.VMEM((2, tq, LANES), BF16), pltpu.VMEM((2, tq, LANES), F32),
                        pltpu.VMEM((S, LANES), F32), pltpu.VMEM((tq, LANES), F32),
                        pltpu.VMEM((tq, LANES), F32), pltpu.VMEM((2, tq, LANES), F32)],
        input_output_aliases={6: 0},
        compiler_params=_cparams("parallel", "arbitrary", "arbitrary"),
    )(h, kv, negcum, lseT, deltaT, dy, dh)


LOG2E = 1.4426950408889634
LN2 = 0.6931471805599453
AUX = FOX_HEAD_DIM


def _split3(x):
    hi = x.astype(BF16).astype(F32)
    r = x - hi
    mid = r.astype(BF16).astype(F32)
    return hi, mid, (r - mid).astype(BF16).astype(F32)


def _lanes3(lane, base, parts, rest):
    return jnp.where(lane == base, parts[0], jnp.where(lane == base + 1, parts[1],
                                                       jnp.where(lane == base + 2, parts[2], rest)))


def _swap_halves(x):
    return pltpu.roll(x, FOX_HEAD_DIM, 1)


def _fox_prep_fwd(h, kv, negcum, *, tb, name):
    S = h.shape[0]
    tb = min(tb, S)

    def body(q_ref, kv_ref, nc_ref, qa_ref, ka_ref, va_ref):
        g = pl.program_id(0)
        lane = lax.broadcasted_iota(jnp.int32, (tb, LANES), 1)
        first = lane < FOX_HEAD_DIM
        q = q_ref[...] * (FOX_SCALE * LOG2E)
        k = kv_ref[:, :LANES].astype(F32)
        v = kv_ref[:, LANES:].astype(F32)
        nc = nc_ref[...]
        ones_q = jnp.where((lane >= AUX) & (lane < AUX + 3), 1.0, 0.0)
        ones_k = jnp.where((lane >= AUX + 3) & (lane < AUX + 6), 1.0, 0.0)
        for hh in range(2):
            sl = slice(LANES * hh, LANES * (hh + 1))
            qh, kh, vh = (q, k, v) if hh == 0 else (_swap_halves(q), _swap_halves(k), _swap_halves(v))
            ncol = jnp.sum(jnp.where(lane == 2 * g + hh, nc, 0.0), axis=1, keepdims=True) * LOG2E
            qa_ref[:, sl] = jnp.where(first, qh, ones_q).astype(BF16)
            ka_ref[:, sl] = jnp.where(first, kh, _lanes3(lane, AUX, _split3(ncol), ones_k)).astype(BF16)
            va_ref[:, sl] = jnp.where(first, vh, 1.0).astype(BF16)

    pair = pl.BlockSpec((tb, 2 * LANES), lambda g, i: (i, g))
    shp = jax.ShapeDtypeStruct((S, 2 * D_MAIN), BF16)
    return pl.pallas_call(
        body, name=name, grid=(FOX_PAIRS, S // tb),
        in_specs=[pl.BlockSpec((tb, LANES), lambda g, i: (i, g)), pair, pl.BlockSpec((tb, LANES), lambda g, i: (i, 0))],
        out_specs=[pair, pair, pair], out_shape=[shp, shp, shp],
        compiler_params=_cparams("parallel", "parallel"),
    )(h, kv, negcum)


def _fox_fwd2(qa, ka, va, *, tq, name):
    S = qa.shape[0]
    tq = min(tq, S)
    nq = S // tq
    rep = tq // LANES

    def body(qa_ref, ka_ref, va_ref, y_ref, lse_ref, m_ref, acc_ref):
        i, j = pl.program_id(1), pl.program_id(2)
        first = lax.broadcasted_iota(jnp.int32, (tq, LANES), 1) < FOX_HEAD_DIM

        @pl.when(j == 0)
        def _():
            m_ref[...] = jnp.full_like(m_ref, NEG)
            acc_ref[...] = jnp.zeros_like(acc_ref)

        def step(masked):
            for hh in range(2):
                sl = slice(LANES * hh, LANES * (hh + 1))
                s = lax.dot_general(qa_ref[:, sl], ka_ref[:, sl], NT, preferred_element_type=F32)
                if masked:
                    r = lax.broadcasted_iota(jnp.int32, (tq, tq), 0)
                    c = lax.broadcasted_iota(jnp.int32, (tq, tq), 1)
                    s = jnp.where(c <= r, s, NEG)
                m_prev = m_ref[hh]
                m_new = jnp.maximum(m_prev, jnp.max(s, axis=1, keepdims=True))
                p = jnp.exp2(s - jnp.tile(m_new, (1, rep))).astype(BF16)
                acc_ref[hh] = jnp.exp2(m_prev - m_new) * acc_ref[hh] + jnp.dot(p, va_ref[:, sl],
                                                                               preferred_element_type=F32)
                m_ref[hh] = m_new

        @pl.when(j < i)
        def _():
            step(False)

        @pl.when(j == i)
        def _():
            step(True)
            ys, lses = [], []
            for hh in range(2):
                a = acc_ref[hh]
                denom = _swap_halves(a)
                ys.append(a / denom)
                lses.append(m_ref[hh] + jnp.log(jnp.where(first, denom, a)) * LOG2E)
            y_ref[...] = jnp.where(first, ys[0], _swap_halves(ys[1]))
            lse_ref[...] = jnp.where(first, lses[0], lses[1])

    kj = lambda i, j: jnp.minimum(i, j)
    kblock = pl.BlockSpec((tq, 2 * LANES), lambda g, i, j: (kj(i, j), g))
    out = pl.BlockSpec((tq, LANES), lambda g, i, j: (i, g))
    return pl.pallas_call(
        body, name=name, grid=(FOX_PAIRS, nq, nq),
        in_specs=[pl.BlockSpec((tq, 2 * LANES), lambda g, i, j: (i, g)), kblock, kblock],
        out_specs=[out, out],
        out_shape=[jax.ShapeDtypeStruct((S, D_MAIN), F32), jax.ShapeDtypeStruct((S, D_MAIN), F32)],
        scratch_shapes=[pltpu.VMEM((2, tq, LANES), F32), pltpu.VMEM((2, tq, LANES), F32)],
        compiler_params=_cparams("parallel", "arbitrary", "arbitrary"),
    )(qa, ka, va)


def _fox_prep_bwd(qa, lse2, dy, y, *, tb, name):
    S = qa.shape[0]
    tb = min(tb, S)

    def body(qa_ref, lse_ref, dy_ref, y_ref, qb_ref, dya_ref):
        lane = lax.broadcasted_iota(jnp.int32, (tb, LANES), 1)
        first = lane < FOX_HEAD_DIM
        lse = lse_ref[...]
        lse_sw = _swap_halves(lse)
        dy = dy_ref[...]
        prod = dy * y_ref[...]
        for hh in range(2):
            sl = slice(LANES * hh, LANES * (hh + 1))
            lse_h = jnp.where(first, lse, lse_sw) if hh == 0 else jnp.where(first, lse_sw, lse)
            qb_ref[:, sl] = _lanes3(lane, AUX + 3, _split3(-lse_h), qa_ref[:, sl].astype(F32)).astype(BF16)
            delta = jnp.sum(jnp.where(first == (hh == 0), prod, 0.0), axis=1, keepdims=True)
            dyh = dy if hh == 0 else _swap_halves(dy)
            dya_ref[:, sl] = jnp.where(first, dyh, _lanes3(lane, AUX, _split3(-delta), 0.0)).astype(BF16)

    pair = pl.BlockSpec((tb, 2 * LANES), lambda g, i: (i, g))
    one = pl.BlockSpec((tb, LANES), lambda g, i: (i, g))
    shp = jax.ShapeDtypeStruct((S, 2 * D_MAIN), BF16)
    return pl.pallas_call(
        body, name=name, grid=(FOX_PAIRS, S // tb),
        in_specs=[pair, one, one, one], out_specs=[pair, pair], out_shape=[shp, shp],
        compiler_params=_cparams("parallel", "parallel"),
    )(qa, lse2, dy, y)


def _fox_bwd2(qb, ka, va, dya, dh, *, tq, name):
    S = qb.shape[0]
    tq = min(tq, S)
    nq = S // tq

    def body(qb_ref, ka_ref, va_ref, dya_ref, dh_in, dq_ref, dkv_ref, dn_ref, drow_ref, dq_acc, dk_acc, dv_acc):
        del dh_in
        j, i = pl.program_id(1), pl.program_id(2)
        first = lax.broadcasted_iota(jnp.int32, (tq, LANES), 1) < FOX_HEAD_DIM

        @pl.when((j == 0) & (i == 0))
        def _():
            dq_acc[...] = jnp.zeros_like(dq_acc)

        @pl.when(i == j)
        def _():
            dk_acc[...] = jnp.zeros_like(dk_acc)
            dv_acc[...] = jnp.zeros_like(dv_acc)

        def step(masked):
            rows = pl.ds(pl.multiple_of(i * tq, tq), tq)
            for hh in range(2):
                sl = slice(LANES * hh, LANES * (hh + 1))
                qbh, kah, dyah = qb_ref[:, sl], ka_ref[:, sl], dya_ref[:, sl]
                eT = lax.dot_general(kah, qbh, NT, preferred_element_type=F32)
                if masked:
                    r = lax.broadcasted_iota(jnp.int32, (tq, tq), 0)
                    c = lax.broadcasted_iota(jnp.int32, (tq, tq), 1)
                    eT = jnp.where(r <= c, eT, NEG)
                pT = jnp.exp2(eT)
                dsT = pT * lax.dot_general(va_ref[:, sl], dyah, NT, preferred_element_type=F32)
                dsb = dsT.astype(BF16)
                dv_acc[hh] += jnp.dot(pT.astype(BF16), dyah, preferred_element_type=F32)
                dk_acc[hh] += jnp.dot(dsb, qbh, preferred_element_type=F32)
                dq_acc[hh, rows, :] += lax.dot_general(dsb, kah, TN, preferred_element_type=F32)

        @pl.when(i > j)
        def _():
            step(False)

        @pl.when(i == j)
        def _():
            step(True)

        @pl.when(i == nq - 1)
        def _():
            dkv_ref[:, :LANES] = (jnp.where(first, dk_acc[0], _swap_halves(dk_acc[1])) * LN2).astype(BF16)
            dkv_ref[:, LANES:] = jnp.where(first, dv_acc[0], _swap_halves(dv_acc[1])).astype(BF16)
            dn_ref[...] = jnp.where(first, _swap_halves(dk_acc[0]), dk_acc[1])

        @pl.when((j == nq - 1) & (i == nq - 1))
        def _():
            first_s = lax.broadcasted_iota(jnp.int32, (S, LANES), 1) < FOX_HEAD_DIM
            dq_ref[...] = (jnp.where(first_s, dq_acc[0], _swap_halves(dq_acc[1])) * FOX_SCALE).astype(BF16)
            drow_ref[...] = jnp.where(first_s, _swap_halves(dq_acc[0]), dq_acc[1])

    qi = lambda j, i: jnp.maximum(i, j)
    qblock = pl.BlockSpec((tq, 2 * LANES), lambda g, j, i: (qi(j, i), g))
    kblock = pl.BlockSpec((tq, 2 * LANES), lambda g, j, i: (j, g))
    return pl.pallas_call(
        body, name=name, grid=(FOX_PAIRS, nq, nq),
        in_specs=[qblock, kblock, kblock, qblock, pl.BlockSpec(memory_space=pl.ANY)],
        out_specs=[pl.BlockSpec((S, LANES), lambda g, j, i: (0, g)), kblock,
                   pl.BlockSpec((tq, LANES), lambda g, j, i: (j, g)),
                   pl.BlockSpec((S, LANES), lambda g, j, i: (0, g))],
        out_shape=[jax.ShapeDtypeStruct(dh.shape, dh.dtype), jax.ShapeDtypeStruct((S, 2 * D_MAIN + LANES), BF16),
                   jax.ShapeDtypeStruct((S, D_MAIN), F32), jax.ShapeDtypeStruct((S, D_MAIN), F32)],
        scratch_shapes=[pltpu.VMEM((2, S, LANES), F32), pltpu.VMEM((2, tq, LANES), F32),
                        pltpu.VMEM((2, tq, LANES), F32)],
        input_output_aliases={4: 0},
        compiler_params=_cparams("parallel", "arbitrary", "arbitrary"),
    )(qb, ka, va, dya, dh)


TB_ROWS = 256
TB_SEQ = 512
TQ_FOX_FWD = 1024
TQ_FOX_BWD = 1024


def _mixer_fwd_tail(ysrc, scale, h, mem, wmkv, wout, tag):
    mkv = _mm(mem, wmkv, mode="nn", out_dtype=F32, tm=256, tn=1024, tk=1024, name=tag + "_mkv")
    ymem = _memattn_fwd(h, mkv, tb=TB_SEQ, name=tag + "_mem_fwd")
    yg = _gate_fwd(ysrc, scale, h, ymem, tb=TB_ROWS, name=tag + "_gate_fwd")
    o = _mm(yg, wout, mode="nn", out_dtype=F32, tm=512, tn=1024, tk=D_MIX, name=tag + "_out")
    return mkv, ymem, yg, o


def _local_step(x, mem, target, win, wmkv, wout, pw, pscale, wkvp, wf, ln_g, ln_b, bias):
    S = x.shape[0]
    ones = jnp.ones((1, D_MAIN), F32)
    g0, b0, g1, b1 = ln_g[0:1], ln_b[0:1], ln_g[1:2], ln_b[1:2]

    h0 = _mm(x, win[0], mode="nn", out_dtype=F32, tm=256, tn=D_IN, tk=D_MODEL, name="l0_in")
    pm, mixed = _pool_fwd(h0, pw, tb=TB_SEQ, name="l0_pool_fwd")
    mkv0, ymem0, yg0, o0 = _mixer_fwd_tail(mixed, pscale, h0, mem, wmkv[0], wout[0], "l0")
    x1, xhat0, rstd0 = _ln_fwd(x, o0, g0, b0, tb=TB_ROWS, name="l0_ln")

    kv = _mm(x1, wkvp, mode="nn", out_dtype=BF16, tm=512, tn=2 * D_MAIN, tk=D_MODEL, name="kv_proj")
    fl = _mm(x1, wf, mode="nn", out_dtype=F32, tm=512, tn=LANES, tk=D_MODEL, name="f_proj")
    negcum = _forget_fwd(fl, bias, tb=TB_SEQ, name="forget_fwd")

    h1 = _mm(x1, win[1], mode="nn", out_dtype=F32, tm=256, tn=D_IN, tk=D_MODEL, name="l1_in")
    qa, ka, va = _fox_prep_fwd(h1, kv, negcum, tb=TB_SEQ, name="fox_prep_fwd")
    y1, lse2 = _fox_fwd2(qa, ka, va, tq=TQ_FOX_FWD, name="fox_fwd")
    mkv1, ymem1, yg1, o1 = _mixer_fwd_tail(y1, ones, h1, mem, wmkv[1], wout[1], "l1")
    dz1, dg1, db1, sq = _ln_loss(x1, o1, g1, b1, target, tb=TB_ROWS, name="l1_ln_loss")

    dwout1 = _mm(yg1, dz1, mode="tn", out_dtype=F32, tm=D_MIX, tn=D_MODEL, tk=512, name="l1_dwout")
    dyg1 = _mm(dz1, wout[1], mode="nt", out_dtype=F32, tm=512, tn=D_MIX, tk=D_MODEL, name="l1_dyg")
    dy1, dymem1, dh1 = _gate_bwd(dyg1, y1, ones, h1, ymem1, tb=TB_ROWS, name="l1_gate_bwd")
    qb, dya = _fox_prep_bwd(qa, lse2, dy1, y1, tb=TB_SEQ, name="fox_prep_bwd")
    dh1, dh2, dnp, drowp = _fox_bwd2(qb, ka, va, dya, dh1, tq=TQ_FOX_BWD, name="fox_bwd")
    dnck = (dnp.reshape(S, FOX_HEADS, FOX_HEAD_DIM)[:, :, 0]
            - drowp.reshape(S, FOX_HEADS, FOX_HEAD_DIM)[:, :, 3])
    dnck = jnp.pad(dnck, ((0, 0), (0, LANES - FOX_HEADS)))
    dh2, dbias = _forget_bwd(dnck, fl, bias, dh2, tb=TB_SEQ, name="forget_bwd")
    dh1, dmkv1 = _memattn_bwd(h1, mkv1, dymem1, dh1, tb=TB_SEQ, name="l1_mem_bwd")
    dwmkv1 = _mm(mem, dmkv1, mode="tn", out_dtype=F32, tm=512, tn=1024, tk=N_MEM, name="l1_dwmkv")
    dwin1 = _mm(x1, dh1, mode="tn", out_dtype=F32, tm=D_MODEL, tn=D_IN // 2, tk=512, name="l1_dwin")
    dwkvf = _mm(x1, dh2, mode="tn", out_dtype=F32, tm=D_MODEL, tn=2 * D_MAIN + LANES, tk=512, name="dwkv")
    dx1 = _mm(dh1, win[1], mode="nt", out_dtype=F32, tm=256, tn=D_MODEL, tk=D_IN, add=dz1, add_scale=ALPHA,
              name="l1_dx")
    wkvf = jnp.concatenate([wkvp, wf], axis=1)
    dx1 = _mm(dh2, wkvf, mode="nt", out_dtype=F32, tm=256, tn=D_MODEL, tk=2 * D_MAIN + LANES, add=dx1, name="kv_dx")

    dz0, dg0, db0 = _ln_bwd(dx1, xhat0, rstd0, g0, tb=TB_ROWS, name="l0_ln_bwd")
    dwout0 = _mm(yg0, dz0, mode="tn", out_dtype=F32, tm=D_MIX, tn=D_MODEL, tk=512, name="l0_dwout")
    dyg0 = _mm(dz0, wout[0], mode="nt", out_dtype=F32, tm=512, tn=D_MIX, tk=D_MODEL, name="l0_dyg")
    dy0, dymem0, dh0 = _gate_bwd(dyg0, mixed, pscale, h0, ymem0, tb=TB_ROWS, name="l0_gate_bwd")
    dh0, dpw, dpscale = _pool_bwd(dy0, pm, mixed, pw, pscale, dh0, tb=TB_SEQ, name="l0_pool_bwd")
    dh0, dmkv0 = _memattn_bwd(h0, mkv0, dymem0, dh0, tb=TB_SEQ, name="l0_mem_bwd")
    dwmkv0 = _mm(mem, dmkv0, mode="tn", out_dtype=F32, tm=512, tn=1024, tk=N_MEM, name="l0_dwmkv")
    dwin0 = _mm(x, dh0, mode="tn", out_dtype=F32, tm=D_MODEL, tn=D_IN // 2, tk=512, name="l0_dwin")
    gx = _mm(dh0, win[0], mode="nt", out_dtype=F32, tm=256, tn=D_MODEL, tk=D_IN, add=dz0, add_scale=ALPHA,
             name="l0_dx")

    dkv = dwkvf[:, :2 * D_MAIN].reshape(D_MODEL, FOX_PAIRS, 2, LANES)
    dwkv = jnp.concatenate([dkv[:, :, 0, :].reshape(D_MODEL, D_MAIN), dkv[:, :, 1, :].reshape(D_MODEL, D_MAIN),
                            dwkvf[:, 2 * D_MAIN:2 * D_MAIN + FOX_HEADS]], axis=1)
    grads = dict(w_in=jnp.stack([dwin0, dwin1]), w_mem_kv=jnp.stack([dwmkv0, dwmkv1]),
                 w_out=jnp.stack([dwout0, dwout1]), pool_w=dpw, pool_scale=dpscale, w_kv_shared=dwkv,
                 ln_g=jnp.concatenate([dg0, dg1]), ln_b=jnp.concatenate([db0, db1]),
                 b_forget=dbias[0, :FOX_HEADS])
    return sq, gx, grads


MESH_ID = pl.DeviceIdType.MESH
HBM = pl.BlockSpec(memory_space=pl.ANY)


def _place():
    return lax.axis_index("x"), lax.axis_index("y"), lax.axis_index("c")


def _slot(p):
    return 4 * p[0] + 2 * p[1] + p[2]


def _all_gather(big, small, *, name):
    def body(big_ref, small_ref, obig, osmall, send_sems, recv_sems, local_sems):
        x, y, c = _place()
        me, sibling = (x, y, c), (x, y, 1 - c)
        chips = [(1 - x, y), (x, 1 - y), (1 - x, 1 - y)]

        def copies(k, block, to, from_input=False):
            s = _slot(block)
            mk = lambda src, dst, kk: pltpu.make_async_remote_copy(
                src_ref=src, dst_ref=dst, send_sem=send_sems.at[kk], recv_sem=recv_sems.at[kk],
                device_id=to, device_id_type=MESH_ID)
            return (mk(big_ref if from_input else obig.at[s], obig.at[s], 2 * k),
                    mk(small_ref if from_input else osmall.at[s], osmall.at[s], 2 * k + 1))

        mine = (pltpu.make_async_copy(big_ref, obig.at[_slot(me)], local_sems.at[0]),
                pltpu.make_async_copy(small_ref, osmall.at[_slot(me)], local_sems.at[1]))
        for cp in mine:
            cp.start()
        first = [copies(0, me, sibling, True)] + [copies(1 + j, me, (*chip, c), True) for j, chip in enumerate(chips)]
        for pair in first:
            for cp in pair:
                cp.start()
        passed = [copies(4 + j, (*chip, c), sibling) for j, chip in enumerate(chips)]
        for j, chip in enumerate(chips):
            for cp in copies(1 + j, (*chip, c), me):
                cp.wait_recv()
            for cp in passed[j]:
                cp.start()
        for cp in copies(0, sibling, me):
            cp.wait_recv()
        for j, chip in enumerate(chips):
            for cp in copies(4 + j, (*chip, 1 - c), me):
                cp.wait_recv()
        for pair in first + passed:
            for cp in pair:
                cp.wait_send()
        for cp in mine:
            cp.wait()

    return pl.pallas_call(
        body, name=name, in_specs=[HBM, HBM], out_specs=[HBM, HBM],
        out_shape=[jax.ShapeDtypeStruct((N_DEV,) + big.shape, big.dtype),
                   jax.ShapeDtypeStruct((N_DEV,) + small.shape, small.dtype)],
        scratch_shapes=[pltpu.SemaphoreType.DMA((14,)), pltpu.SemaphoreType.DMA((14,)), pltpu.SemaphoreType.DMA((2,))],
    )(big, small)


def _exchange(big, small, *, name):
    def body(big_ref, small_ref, obig, osmall, send_sems, recv_sems, local_sems):
        x, y, c = _place()
        me = _slot((x, y, c))
        flip = lambda v, bit: 1 - v if bit else v
        started = []
        for k in range(1, N_DEV):
            peer = (flip(x, k & 4), flip(y, k & 2), flip(c, k & 1))
            ps = _slot(peer)
            for r, (src, dst) in enumerate(((big_ref, obig), (small_ref, osmall))):
                kk = 2 * (k - 1) + r
                cp = pltpu.make_async_remote_copy(
                    src_ref=src.at[ps], dst_ref=dst.at[me], send_sem=send_sems.at[kk], recv_sem=recv_sems.at[kk],
                    device_id=peer, device_id_type=MESH_ID)
                cp.start()
                started.append((cp, pltpu.make_async_remote_copy(
                    src_ref=src.at[ps], dst_ref=dst.at[ps], send_sem=send_sems.at[kk], recv_sem=recv_sems.at[kk],
                    device_id=peer, device_id_type=MESH_ID)))
        mine = (pltpu.make_async_copy(big_ref.at[me], obig.at[me], local_sems.at[0]),
                pltpu.make_async_copy(small_ref.at[me], osmall.at[me], local_sems.at[1]))
        for cp in mine:
            cp.start()
        for sent, landed in started:
            landed.wait_recv()
            sent.wait_send()
        for cp in mine:
            cp.wait()

    return pl.pallas_call(
        body, name=name, in_specs=[HBM, HBM], out_specs=[HBM, HBM],
        out_shape=[jax.ShapeDtypeStruct(big.shape, big.dtype), jax.ShapeDtypeStruct(small.shape, small.dtype)],
        scratch_shapes=[pltpu.SemaphoreType.DMA((14,)), pltpu.SemaphoreType.DMA((14,)), pltpu.SemaphoreType.DMA((2,))],
    )(big, small)


def _adamw(recv, w, m, v, *, tb, name):
    R = w.shape[0]
    tb = min(tb, R)

    def body(r_ref, w_ref, m_ref, v_ref, g_ref, d_ref, nm_ref, nv_ref):
        g = r_ref[0].astype(F32)
        for j in range(1, N_DEV):
            g = g + r_ref[j].astype(F32)
        nm = ADAM_B1 * m_ref[...] + (1.0 - ADAM_B1) * g
        nv = ADAM_B2 * v_ref[...] + (1.0 - ADAM_B2) * (g * g)
        m_hat = nm / (1.0 - ADAM_B1 ** ADAM_STEP)
        v_hat = nv / (1.0 - ADAM_B2 ** ADAM_STEP)
        g_ref[...] = g
        nm_ref[...] = nm
        nv_ref[...] = nv
        d_ref[...] = -ADAM_LR * (m_hat / (jnp.sqrt(v_hat) + ADAM_EPS) + ADAM_WD * w_ref[...])

    row = pl.BlockSpec((tb, LANES), lambda i: (i, 0))
    shp = jax.ShapeDtypeStruct((R, LANES), F32)
    return pl.pallas_call(
        body, name=name, grid=(R // tb,),
        in_specs=[pl.BlockSpec((N_DEV, tb, LANES), lambda i: (0, i, 0)), row, row, row],
        out_specs=[row, row, row, row], out_shape=[shp, shp, shp, shp],
        compiler_params=_cparams("parallel"),
    )(recv, w, m, v)


BIG = ("w_in", "w_mem_kv", "w_out", "pool_w", "w_kv_shared")
SMALL = ("pool_scale", "ln_g", "ln_b", "b_forget")
BIG_ROWS = 13824
SMALL_ROWS = 40
ADAM_TB = 512


def _flat(parts, rows):
    v = jnp.concatenate([p.reshape(-1) for p in parts])
    return jnp.pad(v, (0, rows * LANES - v.shape[0])).reshape(rows, LANES)


def _unflat(flat, shapes):
    v, out, off = flat.reshape(-1), [], 0
    for s in shapes:
        n = math.prod(s)
        out.append(v[off:off + n].reshape(s))
        off += n
    return out


def _by_dest(name, g):
    if name == "w_in":
        return g.reshape(2, D_MODEL, N_DEV, D_IN // N_DEV).transpose(2, 0, 1, 3).reshape(N_DEV, -1)
    if name in ("w_mem_kv", "w_out"):
        return g.reshape(2, N_DEV, g.shape[1] // N_DEV, g.shape[2]).transpose(1, 0, 2, 3).reshape(N_DEV, -1)
    if name == "pool_w":
        return g.reshape(4, N_DEV, POOL_GROUP // N_DEV, POOL_GROUP).transpose(1, 0, 2, 3).reshape(N_DEV, -1)
    if name == "w_kv_shared":
        return g.reshape(D_MODEL, N_DEV, -1).transpose(1, 0, 2).reshape(N_DEV, -1)
    if name == "pool_scale":
        return g.reshape(N_DEV, -1)
    return jnp.broadcast_to(g.reshape(1, -1), (N_DEV, g.size))


def _from_gathered(name, rows, shard_shape):
    a = rows.reshape((N_DEV,) + shard_shape)
    if name == "w_in":
        return a.transpose(1, 2, 0, 3).reshape(2, D_MODEL, D_IN)
    if name in ("w_mem_kv", "w_out"):
        return a.transpose(1, 0, 2, 3).reshape(2, -1, D_MODEL)
    if name == "pool_w":
        return a[:, 0].transpose(1, 0, 2, 3).reshape(4, POOL_GROUP, POOL_GROUP)
    if name == "w_kv_shared":
        return a.transpose(1, 0, 2).reshape(D_MODEL, -1)
    raise ValueError(name)


def kernel(x, mem, w_in, w_mem_kv, w_out, ln_g, ln_b, pool_w, pool_scale, w_kv_shared, b_forget, loss_target, m_w_in, m_w_mem_kv, m_w_out, m_ln_g, m_ln_b, m_pool_w, m_pool_scale, m_w_kv_shared, m_b_forget, v_w_in, v_w_mem_kv, v_w_out, v_ln_g, v_ln_b, v_pool_w, v_pool_scale, v_w_kv_shared, v_b_forget):
    w = dict(w_in=w_in, w_mem_kv=w_mem_kv, w_out=w_out, ln_g=ln_g, ln_b=ln_b, pool_w=pool_w,
             pool_scale=pool_scale, w_kv_shared=w_kv_shared, b_forget=b_forget)
    m = dict(w_in=m_w_in, w_mem_kv=m_w_mem_kv, w_out=m_w_out, ln_g=m_ln_g, ln_b=m_ln_b, pool_w=m_pool_w,
             pool_scale=m_pool_scale, w_kv_shared=m_w_kv_shared, b_forget=m_b_forget)
    v = dict(w_in=v_w_in, w_mem_kv=v_w_mem_kv, w_out=v_w_out, ln_g=v_ln_g, ln_b=v_ln_b, pool_w=v_pool_w,
             pool_scale=v_pool_scale, w_kv_shared=v_w_kv_shared, b_forget=v_b_forget)
    big_shapes = [w[n].shape for n in BIG]
    small_shapes = [w[n].shape for n in SMALL]

    gathered, gscale = _all_gather(_flat([w[n].astype(BF16) for n in BIG], BIG_ROWS),
                                   _flat([pool_scale], 8), name="gather_weights")
    full, off = {}, 0
    flat_g = gathered.reshape(N_DEV, -1)
    for n, s in zip(BIG, big_shapes):
        k = math.prod(s)
        full[n] = _from_gathered(n, flat_g[:, off:off + k], s)
        off += k
    pscale = gscale.reshape(N_DEV, -1)[:, :pool_scale.size].reshape(1, D_MAIN)
    wkv = full["w_kv_shared"]
    wkvp = jnp.stack([wkv[:, :D_MAIN].reshape(D_MODEL, FOX_PAIRS, LANES),
                      wkv[:, D_MAIN:2 * D_MAIN].reshape(D_MODEL, FOX_PAIRS, LANES)], axis=2).reshape(D_MODEL, 2 * D_MAIN)
    wf = jnp.pad(wkv[:, 2 * D_MAIN:], ((0, 0), (0, LANES - FOX_HEADS)))
    bias = jnp.pad(b_forget, (0, LANES - FOX_HEADS)).reshape(1, LANES)

    sq, gx, grads = _local_step(x[0], mem[0], loss_target[0], full["w_in"], full["w_mem_kv"], full["w_out"],
                                full["pool_w"], pscale, wkvp, wf, ln_g, ln_b, bias)
    loss = lax.psum((0.5 / D_MODEL) * jnp.sum(sq), ("x", "y", "c"))

    send_big = jnp.concatenate([_by_dest(n, grads[n]) for n in BIG], axis=1)
    send_big = jnp.pad(send_big, ((0, 0), (0, BIG_ROWS * LANES - send_big.shape[1]))).astype(BF16)
    send_small = jnp.concatenate([_by_dest(n, grads[n]) for n in SMALL], axis=1)
    send_small = jnp.pad(send_small, ((0, 0), (0, SMALL_ROWS * LANES - send_small.shape[1])))
    recv_big, recv_small = _exchange(send_big.reshape(N_DEV, BIG_ROWS, LANES),
                                     send_small.reshape(N_DEV, SMALL_ROWS, LANES), name="exchange_grads")

    outs = {}
    for names, shapes, recv, rows, tag in ((BIG, big_shapes, recv_big, BIG_ROWS, "adamw_big"),
                                           (SMALL, small_shapes, recv_small, SMALL_ROWS, "adamw_small")):
        res = _adamw(recv, _flat([w[n] for n in names], rows), _flat([m[n] for n in names], rows),
                     _flat([v[n] for n in names], rows), tb=ADAM_TB, name=tag)
        for kind, flat in zip(("grad", "delta", "new_m", "new_v"), res):
            for n, a in zip(names, _unflat(flat, shapes)):
                outs[kind, n] = a
    order = ("w_in", "w_mem_kv", "w_out", "ln_g", "ln_b", "pool_w", "pool_scale", "w_kv_shared", "b_forget")
    return (loss, gx[None], *[outs[kind, n] for kind in ("grad", "delta", "new_m", "new_v") for n in order])
```

```python
import math

import numpy as np
import jax
import jax.numpy as jnp
from jax import lax
from jax.experimental import pallas as pl
from jax.experimental.pallas import tpu as pltpu

F32 = jnp.float32
BF16 = jnp.bfloat16

D_MODEL = 1024
D_MAIN = 1024
D_MEM = 512
D_MIX = D_MAIN + D_MEM
D_IN = 2 * D_MIX
N_MEM = 256
MEM_HEADS = 4
MEM_HEAD_DIM = 128
FOX_HEADS = 16
FOX_HEAD_DIM = 64
FOX_PAIRS = FOX_HEADS // 2
POOL_WINDOWS = (2, 4, 8, 16)
POOL_GROUP = 256
POOL_HALO = 16
ALPHA = 4.0 ** 0.25
LN_EPS = 1e-5
NEG = -1e30
LANES = 128
N_DEV = 8

ADAM_LR = 0.001
ADAM_B1 = 0.9
ADAM_B2 = 0.999
ADAM_EPS = 1e-08
ADAM_WD = 0.01
ADAM_STEP = 10

VMEM_LIMIT = 56 * 1024 * 1024

NN = (((1,), (0,)), ((), ()))
NT = (((1,), (1,)), ((), ()))
TN = (((0,), (0,)), ((), ()))


def _cparams(*sem):
    return pltpu.CompilerParams(dimension_semantics=sem, vmem_limit_bytes=VMEM_LIMIT)


def _sigmoid(z):
    return 1.0 / (1.0 + jnp.exp(-z))


def _mm(a, b, *, mode, out_dtype, tm, tn, tk, name, add=None, add_scale=1.0):
    if mode == "nn":
        (M, K), (K2, N) = a.shape, b.shape
    elif mode == "nt":
        (M, K), (N, K2) = a.shape, b.shape
    else:
        (K, M), (K2, N) = a.shape, b.shape
    assert K == K2, (a.shape, b.shape, mode)
    tm, tn, tk = min(tm, M), min(tn, N), min(tk, K)
    assert M % tm == 0 and N % tn == 0 and K % tk == 0, (M, N, K, tm, tn, tk)
    gm, gn, gk = M // tm, N // tn, K // tk
    dims = {"nn": NN, "nt": NT, "tn": TN}[mode]
    if mode == "tn":
        a_spec = pl.BlockSpec((tk, tm), lambda i, j, k: (k, i))
    else:
        a_spec = pl.BlockSpec((tm, tk), lambda i, j, k: (i, k))
    if mode == "nt":
        b_spec = pl.BlockSpec((tn, tk), lambda i, j, k: (j, k))
    else:
        b_spec = pl.BlockSpec((tk, tn), lambda i, j, k: (k, j))
    o_spec = pl.BlockSpec((tm, tn), lambda i, j, k: (i, j))
    has_add = add is not None
    acc_in_out = out_dtype == F32

    def body(*refs):
        a_ref, b_ref = refs[0], refs[1]
        add_ref = refs[2] if has_add else None
        o_ref = refs[3] if has_add else refs[2]
        prod = lax.dot_general(a_ref[...].astype(BF16), b_ref[...].astype(BF16), dims,
                               preferred_element_type=F32)

        def finish(r):
            if has_add:
                r = r + add_scale * add_ref[...]
            o_ref[...] = r.astype(out_dtype)

        if gk == 1:
            finish(prod)
        else:
            acc_ref = o_ref if acc_in_out else refs[-1]
            k = pl.program_id(2)

            @pl.when(k == 0)
            def _():
                acc_ref[...] = prod

            @pl.when(k > 0)
            def _():
                acc_ref[...] += prod

            if has_add or not acc_in_out:
                @pl.when(k == gk - 1)
                def _():
                    finish(acc_ref[...])

    in_specs = [a_spec, b_spec] + ([o_spec] if has_add else [])
    args = (a, b) + ((add,) if has_add else ())
    return pl.pallas_call(
        body, name=name, grid=(gm, gn, gk), in_specs=in_specs, out_specs=o_spec,
        out_shape=jax.ShapeDtypeStruct((M, N), out_dtype),
        scratch_shapes=[pltpu.VMEM((tm, tn), F32)] if gk > 1 and not acc_in_out else [],
        compiler_params=_cparams("parallel", "parallel", "arbitrary"),
    )(*args)


def _ln_stats(z):
    mu = jnp.mean(z, axis=1, keepdims=True)
    zc = z - mu
    var = jnp.mean(zc * zc, axis=1, keepdims=True)
    rstd = lax.rsqrt(var + LN_EPS)
    return zc * rstd, rstd


def _ln_bwd_math(dy, xhat, rstd, g):
    dxh = dy * g
    m1 = jnp.mean(dxh, axis=1, keepdims=True)
    m2 = jnp.mean(dxh * xhat, axis=1, keepdims=True)
    return rstd * (dxh - m1 - xhat * m2)


def _out_ln(yg, wout, x, g, b, *, tb, name):
    S = x.shape[0]
    tb = min(tb, S)

    def body(yg_ref, w_ref, x_ref, g_ref, b_ref, y_ref, xhat_ref, rstd_ref):
        o = jnp.dot(yg_ref[...], w_ref[...], preferred_element_type=F32)
        xhat, rstd = _ln_stats(ALPHA * x_ref[...] + o)
        xhat_ref[...] = xhat
        rstd_ref[...] = rstd
        y_ref[...] = xhat * g_ref[...] + b_ref[...]

    row = pl.BlockSpec((tb, D_MODEL), lambda i: (i, 0))
    vec = pl.BlockSpec((1, D_MODEL), lambda i: (0, 0))
    return pl.pallas_call(
        body, name=name, grid=(S // tb,),
        in_specs=[pl.BlockSpec((tb, D_MIX), lambda i: (i, 0)), pl.BlockSpec((D_MIX, D_MODEL), lambda i: (0, 0)),
                  row, vec, vec],
        out_specs=[row, row, pl.BlockSpec((tb, 1), lambda i: (i, 0))],
        out_shape=[jax.ShapeDtypeStruct((S, D_MODEL), F32), jax.ShapeDtypeStruct((S, D_MODEL), F32),
                   jax.ShapeDtypeStruct((S, 1), F32)],
        compiler_params=_cparams("parallel"),
    )(yg, wout, x, g, b)


def _out_ln_loss(yg, wout, x, g, b, target, *, tb, name):
    S = x.shape[0]
    tb = min(tb, S)

    def body(yg_ref, w_ref, x_ref, g_ref, b_ref, t_ref, dz_ref, dg_ref, db_ref, sq_ref):
        @pl.when(pl.program_id(0) == 0)
        def _():
            dg_ref[...] = jnp.zeros_like(dg_ref)
            db_ref[...] = jnp.zeros_like(db_ref)
            sq_ref[...] = jnp.zeros_like(sq_ref)

        o = jnp.dot(yg_ref[...], w_ref[...], preferred_element_type=F32)
        xhat, rstd = _ln_stats(ALPHA * x_ref[...] + o)
        err = xhat * g_ref[...] + b_ref[...] - t_ref[...]
        sq_ref[...] += jnp.sum(err * err, axis=0, keepdims=True)
        dy = err * (1.0 / D_MODEL)
        dz_ref[...] = _ln_bwd_math(dy, xhat, rstd, g_ref[...])
        dg_ref[...] += jnp.sum(dy * xhat, axis=0, keepdims=True)
        db_ref[...] += jnp.sum(dy, axis=0, keepdims=True)

    row = pl.BlockSpec((tb, D_MODEL), lambda i: (i, 0))
    vec = pl.BlockSpec((1, D_MODEL), lambda i: (0, 0))
    vshape = jax.ShapeDtypeStruct((1, D_MODEL), F32)
    return pl.pallas_call(
        body, name=name, grid=(S // tb,),
        in_specs=[pl.BlockSpec((tb, D_MIX), lambda i: (i, 0)), pl.BlockSpec((D_MIX, D_MODEL), lambda i: (0, 0)),
                  row, vec, vec, row],
        out_specs=[row, vec, vec, vec],
        out_shape=[jax.ShapeDtypeStruct((S, D_MODEL), F32), vshape, vshape, vshape],
        compiler_params=_cparams("arbitrary"),
    )(yg, wout, x, g, b, target)


def _ln_bwd(dy, xhat, rstd, g, *, tb, name):
    S = dy.shape[0]
    tb = min(tb, S)

    def body(dy_ref, xhat_ref, rstd_ref, g_ref, dz_ref, dg_ref, db_ref):
        @pl.when(pl.program_id(0) == 0)
        def _():
            dg_ref[...] = jnp.zeros_like(dg_ref)
            db_ref[...] = jnp.zeros_like(db_ref)

        dy_, xhat_ = dy_ref[...], xhat_ref[...]
        dz_ref[...] = _ln_bwd_math(dy_, xhat_, rstd_ref[...], g_ref[...])
        dg_ref[...] += jnp.sum(dy_ * xhat_, axis=0, keepdims=True)
        db_ref[...] += jnp.sum(dy_, axis=0, keepdims=True)

    row = pl.BlockSpec((tb, D_MODEL), lambda i: (i, 0))
    vec = pl.BlockSpec((1, D_MODEL), lambda i: (0, 0))
    return pl.pallas_call(
        body, name=name, grid=(S // tb,),
        in_specs=[row, row, pl.BlockSpec((tb, 1), lambda i: (i, 0)), vec],
        out_specs=[row, vec, vec],
        out_shape=[jax.ShapeDtypeStruct((S, D_MODEL), F32), jax.ShapeDtypeStruct((1, D_MODEL), F32),
                   jax.ShapeDtypeStruct((1, D_MODEL), F32)],
        compiler_params=_cparams("arbitrary"),
    )(dy, xhat, rstd, g)


def _gate_fwd(ysrc, scale, h, ymem, *, tb, name):
    S = ysrc.shape[0]
    tb = min(tb, S)

    def body(ys_ref, sc_ref, ga_ref, gb_ref, gc_ref, ym_ref, yg_ref):
        ymain = ys_ref[...] * sc_ref[...]
        for k, g_ref in enumerate((ga_ref, gb_ref)):
            gv = g_ref[...].astype(F32)
            yg_ref[:, 512 * k:512 * (k + 1)] = (ymain[:, 512 * k:512 * (k + 1)] * gv * _sigmoid(gv)).astype(BF16)
        gv = gc_ref[...].astype(F32)
        yg_ref[:, 1024:1536] = (ym_ref[...] * gv * _sigmoid(gv)).astype(BF16)

    slab = lambda c: pl.BlockSpec((tb, 512), lambda i, c=c: (i, c))
    return pl.pallas_call(
        body, name=name, grid=(S // tb,),
        in_specs=[pl.BlockSpec((tb, D_MAIN), lambda i: (i, 0)), pl.BlockSpec((1, D_MAIN), lambda i: (0, 0)),
                  slab(3), slab(4), slab(5), pl.BlockSpec((tb, D_MEM), lambda i: (i, 0))],
        out_specs=pl.BlockSpec((tb, D_MIX), lambda i: (i, 0)),
        out_shape=jax.ShapeDtypeStruct((S, D_MIX), BF16),
        compiler_params=_cparams("parallel"),
    )(ysrc, scale, h, h, h, ymem)


def _gate_bwd(dyg, ysrc, scale, h, ymem, *, tb, name):
    S = ysrc.shape[0]
    tb = min(tb, S)

    def dsilu(gv):
        sg = _sigmoid(gv)
        return sg, sg * (1.0 + gv * (1.0 - sg))

    def body(da_ref, db_ref, dc_ref, ys_ref, sc_ref, ga_ref, gb_ref, gc_ref, ym_ref, dym_ref, dymem_ref, dh_ref):
        ymain = ys_ref[...] * sc_ref[...]
        for k, (d_ref, g_ref) in enumerate(((da_ref, ga_ref), (db_ref, gb_ref))):
            gv, d = g_ref[...].astype(F32), d_ref[...].astype(F32)
            sg, ds = dsilu(gv)
            dym_ref[:, 512 * k:512 * (k + 1)] = d * gv * sg
            dh_ref[:, 512 * k:512 * (k + 1)] = (d * ymain[:, 512 * k:512 * (k + 1)] * ds).astype(BF16)
        gv, d = gc_ref[...].astype(F32), dc_ref[...].astype(F32)
        sg, ds = dsilu(gv)
        dymem_ref[...] = d * gv * sg
        dh_ref[:, 1024:1536] = (d * ym_ref[...] * ds).astype(BF16)

    slab = lambda c: pl.BlockSpec((tb, 512), lambda i, c=c: (i, c))
    return pl.pallas_call(
        body, name=name, grid=(S // tb,),
        in_specs=[slab(0), slab(1), slab(2),
                  pl.BlockSpec((tb, D_MAIN), lambda i: (i, 0)), pl.BlockSpec((1, D_MAIN), lambda i: (0, 0)),
                  slab(3), slab(4), slab(5), pl.BlockSpec((tb, D_MEM), lambda i: (i, 0))],
        out_specs=[pl.BlockSpec((tb, D_MAIN), lambda i: (i, 0)), pl.BlockSpec((tb, D_MEM), lambda i: (i, 0)),
                   pl.BlockSpec((tb, D_MIX), lambda i: (i, 1))],
        out_shape=[jax.ShapeDtypeStruct((S, D_MAIN), F32), jax.ShapeDtypeStruct((S, D_MEM), F32),
                   jax.ShapeDtypeStruct((S, D_IN), BF16)],
        compiler_params=_cparams("parallel"),
    )(dyg, dyg, dyg, ysrc, scale, h, h, h, ymem)


def _window_count(t0, rows, w):
    t = t0 + lax.broadcasted_iota(jnp.int32, (rows, POOL_GROUP), 0)
    return jnp.minimum(t + 1, w).astype(F32)


def _pool_fwd(h, pw, *, tb, name):
    S = h.shape[0]
    tb = min(tb, S)

    def body(u_ref, pw_ref, pm_ref, mixed_ref, tail_ref):
        i = pl.program_id(0)

        @pl.when(i == 0)
        def _():
            tail_ref[...] = jnp.zeros_like(tail_ref)

        u = u_ref[...].astype(F32)
        xfull = jnp.concatenate([tail_ref[...], u], axis=0)
        for gi, w in enumerate(POOL_WINDOWS):
            cols = slice(POOL_GROUP * gi, POOL_GROUP * (gi + 1))
            s = xfull[:, cols]
            sh = 1
            while sh < w:
                s = s + pltpu.roll(s, sh, 0)
                sh *= 2
            pm = s[POOL_HALO:, :] / _window_count(i * tb, tb, w) - u[:, cols]
            pmb = pm.astype(BF16)
            pm_ref[:, cols] = pmb
            mixed_ref[:, cols] = jnp.dot(pmb, pw_ref[gi], preferred_element_type=F32)
        tail_ref[...] = u[tb - POOL_HALO:, :]

    return pl.pallas_call(
        body, name=name, grid=(S // tb,),
        in_specs=[pl.BlockSpec((tb, D_MAIN), lambda i: (i, 0)),
                  pl.BlockSpec((4, POOL_GROUP, POOL_GROUP), lambda i: (0, 0, 0))],
        out_specs=[pl.BlockSpec((tb, D_MAIN), lambda i: (i, 0)), pl.BlockSpec((tb, D_MAIN), lambda i: (i, 0))],
        out_shape=[jax.ShapeDtypeStruct((S, D_MAIN), BF16), jax.ShapeDtypeStruct((S, D_MAIN), F32)],
        scratch_shapes=[pltpu.VMEM((POOL_HALO, D_MAIN), F32)],
        compiler_params=_cparams("arbitrary"),
    )(h, pw)


def _pool_bwd(dymain, pm, mixed, pw, scale, dh, *, tb, name):
    S = dymain.shape[0]
    tb = min(tb, S)
    nb = S // tb
    n = tb + POOL_HALO

    def body(dy_ref, pm_ref, mixed_ref, pw_ref, sc_ref, dh_in, dh_ref, dpw_ref, dsc_ref, head_ref, dpw_acc):
        del dh_in
        i = pl.program_id(0)

        @pl.when(i == 0)
        def _():
            head_ref[...] = jnp.zeros_like(head_ref)
            dpw_acc[...] = jnp.zeros_like(dpw_acc)
            dsc_ref[...] = jnp.zeros_like(dsc_ref)

        dy = dy_ref[...]
        dsc_ref[...] += jnp.sum(dy * mixed_ref[...], axis=0, keepdims=True)
        dmixed = dy * sc_ref[...]
        t0 = (nb - 1 - i) * tb
        for gi, w in enumerate(POOL_WINDOWS):
            cols = slice(POOL_GROUP * gi, POOL_GROUP * (gi + 1))
            dm = dmixed[:, cols].astype(BF16)
            dpw_acc[gi] += lax.dot_general(pm_ref[:, cols], dm, TN, preferred_element_type=F32)
            dpm = lax.dot_general(dm, pw_ref[gi], NT, preferred_element_type=F32)
            e = dpm / _window_count(t0, tb, w)
            s = jnp.concatenate([e, head_ref[:, cols]], axis=0)
            sh = 1
            while sh < w:
                s = s + pltpu.roll(s, n - sh, 0)
                sh *= 2
            dh_ref[:, cols] = (s[:tb, :] - dpm).astype(BF16)
            head_ref[:, cols] = e[:POOL_HALO, :]

        @pl.when(i == nb - 1)
        def _():
            dpw_ref[...] = dpw_acc[...].astype(BF16)

    rev = lambda i: (nb - 1 - i, 0)
    return pl.pallas_call(
        body, name=name, grid=(nb,),
        in_specs=[pl.BlockSpec((tb, D_MAIN), rev), pl.BlockSpec((tb, D_MAIN), rev), pl.BlockSpec((tb, D_MAIN), rev),
                  pl.BlockSpec((4, POOL_GROUP, POOL_GROUP), lambda i: (0, 0, 0)),
                  pl.BlockSpec((1, D_MAIN), lambda i: (0, 0)), pl.BlockSpec(memory_space=pl.ANY)],
        out_specs=[pl.BlockSpec((tb, D_MAIN), rev),
                   pl.BlockSpec((4, POOL_GROUP, POOL_GROUP), lambda i: (0, 0, 0)),
                   pl.BlockSpec((1, D_MAIN), lambda i: (0, 0))],
        out_shape=[jax.ShapeDtypeStruct(dh.shape, dh.dtype),
                   jax.ShapeDtypeStruct((4, POOL_GROUP, POOL_GROUP), BF16), jax.ShapeDtypeStruct((1, D_MAIN), F32)],
        scratch_shapes=[pltpu.VMEM((POOL_HALO, D_MAIN), F32), pltpu.VMEM((4, POOL_GROUP, POOL_GROUP), F32)],
        input_output_aliases={5: 0},
        compiler_params=_cparams("arbitrary"),
    )(dymain, pm, mixed, pw, scale, dh)


MEM_SCALE = MEM_HEAD_DIM ** -0.5


def _mem_probs(q_ref, mkv_ref, hd):
    cols = slice(MEM_HEAD_DIM * hd, MEM_HEAD_DIM * (hd + 1))
    q = (q_ref[:, cols].astype(F32) * MEM_SCALE).astype(BF16)
    mk = mkv_ref[:, cols].astype(BF16)
    mv = mkv_ref[:, D_MEM + MEM_HEAD_DIM * hd:D_MEM + MEM_HEAD_DIM * (hd + 1)].astype(BF16)
    s = lax.dot_general(q, mk, NT, preferred_element_type=F32)
    e = jnp.exp(s - jnp.max(s, axis=1, keepdims=True))
    return cols, q, mk, mv, e, jnp.sum(e, axis=1, keepdims=True)


def _memattn_fwd(h, mkv, *, tb, name):
    S = h.shape[0]
    tb = min(tb, S)

    def body(q_ref, mkv_ref, y_ref):
        for hd in range(MEM_HEADS):
            cols, _, _, mv, e, l = _mem_probs(q_ref, mkv_ref, hd)
            y_ref[:, cols] = jnp.dot(e.astype(BF16), mv, preferred_element_type=F32) / l

    return pl.pallas_call(
        body, name=name, grid=(S // tb,),
        in_specs=[pl.BlockSpec((tb, D_MEM), lambda i: (i, 2)), pl.BlockSpec((N_MEM, 2 * D_MEM), lambda i: (0, 0))],
        out_specs=pl.BlockSpec((tb, D_MEM), lambda i: (i, 0)),
        out_shape=jax.ShapeDtypeStruct((S, D_MEM), F32),
        compiler_params=_cparams("parallel"),
    )(h, mkv)


def _memattn_bwd(h, mkv, dy, dh, *, tb, name):
    S = h.shape[0]
    tb = min(tb, S)

    def body(q_ref, mkv_ref, dy_ref, dh_in, dh_ref, dmkv_ref):
        del dh_in

        @pl.when(pl.program_id(0) == 0)
        def _():
            dmkv_ref[...] = jnp.zeros_like(dmkv_ref)

        for hd in range(MEM_HEADS):
            cols, q, mk, mv, e, l = _mem_probs(q_ref, mkv_ref, hd)
            p = e / l
            dyh = dy_ref[:, cols].astype(BF16)
            dp = lax.dot_general(dyh, mv, NT, preferred_element_type=F32)
            ds = p * (dp - jnp.sum(dp * p, axis=1, keepdims=True))
            dsb = ds.astype(BF16)
            dh_ref[:, cols] = (jnp.dot(dsb, mk, preferred_element_type=F32) * MEM_SCALE).astype(BF16)
            dmkv_ref[:, cols] += lax.dot_general(dsb, q, TN, preferred_element_type=F32)
            vcols = slice(D_MEM + MEM_HEAD_DIM * hd, D_MEM + MEM_HEAD_DIM * (hd + 1))
            dmkv_ref[:, vcols] += lax.dot_general(p.astype(BF16), dyh, TN, preferred_element_type=F32)

    return pl.pallas_call(
        body, name=name, grid=(S // tb,),
        in_specs=[pl.BlockSpec((tb, D_MEM), lambda i: (i, 2)), pl.BlockSpec((N_MEM, 2 * D_MEM), lambda i: (0, 0)),
                  pl.BlockSpec((tb, D_MEM), lambda i: (i, 0)), pl.BlockSpec(memory_space=pl.ANY)],
        out_specs=[pl.BlockSpec((tb, D_MEM), lambda i: (i, 2)), pl.BlockSpec((N_MEM, 2 * D_MEM), lambda i: (0, 0))],
        out_shape=[jax.ShapeDtypeStruct(dh.shape, dh.dtype), jax.ShapeDtypeStruct((N_MEM, 2 * D_MEM), F32)],
        input_output_aliases={3: 0},
        compiler_params=_cparams("arbitrary"),
    )(h, mkv, dy, dh)


def _forget_fwd(fl, bias, *, tb, name):
    S = fl.shape[0]
    tb = min(tb, S)

    def body(fl_ref, b_ref, o_ref, carry_ref):
        @pl.when(pl.program_id(0) == 0)
        def _():
            carry_ref[...] = jnp.zeros_like(carry_ref)

        z = fl_ref[...] + b_ref[...]
        lf = jnp.minimum(z, 0.0) - jnp.log(1.0 + jnp.exp(-jnp.abs(z)))
        row = lax.broadcasted_iota(jnp.int32, (tb, LANES), 0)
        c = lf
        sh = 1
        while sh < tb:
            c = c + jnp.where(row >= sh, pltpu.roll(c, sh, 0), 0.0)
            sh *= 2
        o_ref[...] = -(carry_ref[...] + c)
        carry_ref[...] += jnp.sum(lf, axis=0, keepdims=True)

    return pl.pallas_call(
        body, name=name, grid=(S // tb,),
        in_specs=[pl.BlockSpec((tb, LANES), lambda i: (i, 0)), pl.BlockSpec((1, LANES), lambda i: (0, 0))],
        out_specs=pl.BlockSpec((tb, LANES), lambda i: (i, 0)),
        out_shape=jax.ShapeDtypeStruct((S, LANES), F32),
        scratch_shapes=[pltpu.VMEM((1, LANES), F32)],
        compiler_params=_cparams("arbitrary"),
    )(fl, bias)


def _forget_bwd(dnck, fl, bias, dh2, *, tb, name):
    S = fl.shape[0]
    tb = min(tb, S)
    nb = S // tb

    def body(d_ref, fl_ref, b_ref, dh_in, dh_ref, db_ref, carry_ref):
        del dh_in

        @pl.when(pl.program_id(0) == 0)
        def _():
            carry_ref[...] = jnp.zeros_like(carry_ref)
            db_ref[...] = jnp.zeros_like(db_ref)

        dcum = -d_ref[...]
        row = lax.broadcasted_iota(jnp.int32, (tb, LANES), 0)
        c = dcum
        sh = 1
        while sh < tb:
            c = c + jnp.where(row < tb - sh, pltpu.roll(c, tb - sh, 0), 0.0)
            sh *= 2
        dlf = carry_ref[...] + c
        carry_ref[...] += jnp.sum(dcum, axis=0, keepdims=True)
        z = fl_ref[...] + b_ref[...]
        lane = lax.broadcasted_iota(jnp.int32, (tb, LANES), 1)
        dfl = jnp.where(lane < FOX_HEADS, dlf / (1.0 + jnp.exp(z)), 0.0)
        db_ref[...] += jnp.sum(dfl, axis=0, keepdims=True)
        dh_ref[...] = dfl.astype(BF16)

    rev = lambda i: (nb - 1 - i, 0)
    return pl.pallas_call(
        body, name=name, grid=(nb,),
        in_specs=[pl.BlockSpec((tb, LANES), rev), pl.BlockSpec((tb, LANES), rev),
                  pl.BlockSpec((1, LANES), lambda i: (0, 0)), pl.BlockSpec(memory_space=pl.ANY)],
        out_specs=[pl.BlockSpec((tb, LANES), lambda i: (nb - 1 - i, 2 * D_MAIN // LANES)),
                   pl.BlockSpec((1, LANES), lambda i: (0, 0))],
        out_shape=[jax.ShapeDtypeStruct(dh2.shape, dh2.dtype), jax.ShapeDtypeStruct((1, LANES), F32)],
        scratch_shapes=[pltpu.VMEM((1, LANES), F32)],
        input_output_aliases={3: 0},
        compiler_params=_cparams("arbitrary"),
    )(dnck, fl, bias, dh2)


FOX_SCALE = FOX_HEAD_DIM ** -0.5
LOG2E = 1.4426950408889634
LN2 = 0.6931471805599453
AUX = FOX_HEAD_DIM


def _split3(x):
    hi = x.astype(BF16).astype(F32)
    r = x - hi
    mid = r.astype(BF16).astype(F32)
    return hi, mid, (r - mid).astype(BF16).astype(F32)


def _lanes3(lane, base, parts, rest):
    return jnp.where(lane == base, parts[0], jnp.where(lane == base + 1, parts[1],
                                                       jnp.where(lane == base + 2, parts[2], rest)))


def _swap_halves(x):
    return pltpu.roll(x, FOX_HEAD_DIM, 1)


def _causal_steps(nq, keys_outer):
    if keys_outer:
        pairs = [(i, j) for j in range(nq) for i in range(j, nq)]
    else:
        pairs = [(i, j) for i in range(nq) for j in range(i + 1)]
    it, jt = zip(*pairs)
    return jnp.asarray(np.array(it, np.int32)), jnp.asarray(np.array(jt, np.int32))


def _fox_prep_fwd(h, kv, negcum, *, tb, name):
    S = h.shape[0]
    tb = min(tb, S)

    def body(q_ref, kv_ref, nc_ref, qa_ref, ka_ref, va_ref):
        g = pl.program_id(0)
        lane = lax.broadcasted_iota(jnp.int32, (tb, LANES), 1)
        first = lane < FOX_HEAD_DIM
        q = q_ref[...].astype(F32) * (FOX_SCALE * LOG2E)
        k = kv_ref[:, :LANES].astype(F32)
        v = kv_ref[:, LANES:].astype(F32)
        nc = nc_ref[...]
        ones_q = jnp.where((lane >= AUX) & (lane < AUX + 3), 1.0, 0.0)
        ones_k = jnp.where((lane >= AUX + 3) & (lane < AUX + 6), 1.0, 0.0)
        for hh in range(2):
            sl = slice(LANES * hh, LANES * (hh + 1))
            qh, kh, vh = (q, k, v) if hh == 0 else (_swap_halves(q), _swap_halves(k), _swap_halves(v))
            ncol = jnp.sum(jnp.where(lane == 2 * g + hh, nc, 0.0), axis=1, keepdims=True) * LOG2E
            qa_ref[:, sl] = jnp.where(first, qh, ones_q).astype(BF16)
            ka_ref[:, sl] = jnp.where(first, kh, _lanes3(lane, AUX, _split3(ncol), ones_k)).astype(BF16)
            va_ref[:, sl] = jnp.where(first, vh, 1.0).astype(BF16)

    pair = pl.BlockSpec((tb, 2 * LANES), lambda g, i: (i, g))
    shp = jax.ShapeDtypeStruct((S, 2 * D_MAIN), BF16)
    return pl.pallas_call(
        body, name=name, grid=(FOX_PAIRS, S // tb),
        in_specs=[pl.BlockSpec((tb, LANES), lambda g, i: (i, g)), pair, pl.BlockSpec((tb, LANES), lambda g, i: (i, 0))],
        out_specs=[pair, pair, pair], out_shape=[shp, shp, shp],
        compiler_params=_cparams("parallel", "parallel"),
    )(h, kv, negcum)


def _fox_fwd(qa, ka, va, *, tq, name):
    S = qa.shape[0]
    tq = min(tq, S)
    nq = S // tq
    rep = tq // LANES
    it, jt = _causal_steps(nq, keys_outer=False)

    def body(it_ref, jt_ref, qa_ref, ka_ref, va_ref, y_ref, lse_ref, m_ref, acc_ref):
        n = pl.program_id(1)
        i, j = it_ref[n], jt_ref[n]
        first = lax.broadcasted_iota(jnp.int32, (tq, LANES), 1) < FOX_HEAD_DIM

        @pl.when(j == 0)
        def _():
            m_ref[...] = jnp.full_like(m_ref, NEG)
            acc_ref[...] = jnp.zeros_like(acc_ref)

        def step(masked):
            for hh in range(2):
                sl = slice(LANES * hh, LANES * (hh + 1))
                s = lax.dot_general(qa_ref[:, sl], ka_ref[:, sl], NT, preferred_element_type=F32)
                if masked:
                    r = lax.broadcasted_iota(jnp.int32, (tq, tq), 0)
                    c = lax.broadcasted_iota(jnp.int32, (tq, tq), 1)
                    s = jnp.where(c <= r, s, NEG)
                m_prev = m_ref[hh]
                m_new = jnp.maximum(m_prev, jnp.max(s, axis=1, keepdims=True))
                p = jnp.exp2(s - jnp.tile(m_new, (1, rep))).astype(BF16)
                acc_ref[hh] = jnp.exp2(m_prev - m_new) * acc_ref[hh] + jnp.dot(p, va_ref[:, sl],
                                                                               preferred_element_type=F32)
                m_ref[hh] = m_new

        @pl.when(j < i)
        def _():
            step(False)

        @pl.when(j == i)
        def _():
            step(True)
            ys, lses = [], []
            for hh in range(2):
                a = acc_ref[hh]
                denom = _swap_halves(a)
                ys.append(a / denom)
                lses.append(m_ref[hh] + jnp.log(jnp.where(first, denom, a)) * LOG2E)
            y_ref[...] = jnp.where(first, ys[0], _swap_halves(ys[1]))
            lse_ref[...] = jnp.where(first, lses[0], lses[1])

    qblock = pl.BlockSpec((tq, 2 * LANES), lambda g, n, it, jt: (it[n], g))
    kblock = pl.BlockSpec((tq, 2 * LANES), lambda g, n, it, jt: (jt[n], g))
    out = pl.BlockSpec((tq, LANES), lambda g, n, it, jt: (it[n], g))
    return pl.pallas_call(
        body, name=name,
        grid_spec=pltpu.PrefetchScalarGridSpec(
            num_scalar_prefetch=2, grid=(FOX_PAIRS, it.shape[0]),
            in_specs=[qblock, kblock, kblock], out_specs=[out, out],
            scratch_shapes=[pltpu.VMEM((2, tq, LANES), F32), pltpu.VMEM((2, tq, LANES), F32)]),
        out_shape=[jax.ShapeDtypeStruct((S, D_MAIN), F32), jax.ShapeDtypeStruct((S, D_MAIN), F32)],
        compiler_params=_cparams("parallel", "arbitrary"),
    )(it, jt, qa, ka, va)


def _fox_prep_bwd(qa, lse2, dy, y, *, tb, name):
    S = qa.shape[0]
    tb = min(tb, S)

    def body(qa_ref, lse_ref, dy_ref, y_ref, qb_ref, dya_ref):
        lane = lax.broadcasted_iota(jnp.int32, (tb, LANES), 1)
        first = lane < FOX_HEAD_DIM
        lse = lse_ref[...]
        lse_sw = _swap_halves(lse)
        dy = dy_ref[...]
        prod = dy * y_ref[...]
        for hh in range(2):
            sl = slice(LANES * hh, LANES * (hh + 1))
            lse_h = jnp.where(first, lse, lse_sw) if hh == 0 else jnp.where(first, lse_sw, lse)
            qb_ref[:, sl] = _lanes3(lane, AUX + 3, _split3(-lse_h), qa_ref[:, sl].astype(F32)).astype(BF16)
            delta = jnp.sum(jnp.where(first == (hh == 0), prod, 0.0), axis=1, keepdims=True)
            dyh = dy if hh == 0 else _swap_halves(dy)
            dya_ref[:, sl] = jnp.where(first, dyh, _lanes3(lane, AUX, _split3(-delta), 0.0)).astype(BF16)

    pair = pl.BlockSpec((tb, 2 * LANES), lambda g, i: (i, g))
    one = pl.BlockSpec((tb, LANES), lambda g, i: (i, g))
    shp = jax.ShapeDtypeStruct((S, 2 * D_MAIN), BF16)
    return pl.pallas_call(
        body, name=name, grid=(FOX_PAIRS, S // tb),
        in_specs=[pair, one, one, one], out_specs=[pair, pair], out_shape=[shp, shp],
        compiler_params=_cparams("parallel", "parallel"),
    )(qa, lse2, dy, y)


def _fox_bwd(qb, ka, va, dya, dh, *, tq, name):
    S = qb.shape[0]
    tq = min(tq, S)
    nq = S // tq
    it, jt = _causal_steps(nq, keys_outer=True)
    nsteps = it.shape[0]

    def body(it_ref, jt_ref, qb_ref, ka_ref, va_ref, dya_ref, dh_in, dq_ref, dkv_ref, dn_ref, drow_ref,
             dq_acc, dk_acc, dv_acc):
        del dh_in
        n = pl.program_id(1)
        i, j = it_ref[n], jt_ref[n]
        first = lax.broadcasted_iota(jnp.int32, (tq, LANES), 1) < FOX_HEAD_DIM

        @pl.when(n == 0)
        def _():
            dq_acc[...] = jnp.zeros_like(dq_acc)

        @pl.when(i == j)
        def _():
            dk_acc[...] = jnp.zeros_like(dk_acc)
            dv_acc[...] = jnp.zeros_like(dv_acc)

        def step(masked):
            rows = pl.ds(pl.multiple_of(i * tq, tq), tq)
            for hh in range(2):
                sl = slice(LANES * hh, LANES * (hh + 1))
                qbh, kah, dyah = qb_ref[:, sl], ka_ref[:, sl], dya_ref[:, sl]
                eT = lax.dot_general(kah, qbh, NT, preferred_element_type=F32)
                if masked:
                    r = lax.broadcasted_iota(jnp.int32, (tq, tq), 0)
                    c = lax.broadcasted_iota(jnp.int32, (tq, tq), 1)
                    eT = jnp.where(r <= c, eT, NEG)
                pT = jnp.exp2(eT)
                dsT = pT * lax.dot_general(va_ref[:, sl], dyah, NT, preferred_element_type=F32)
                dsb = dsT.astype(BF16)
                dv_acc[hh] += jnp.dot(pT.astype(BF16), dyah, preferred_element_type=F32)
                dk_acc[hh] += jnp.dot(dsb, qbh, preferred_element_type=F32)
                dq_acc[hh, rows, :] += lax.dot_general(dsb, kah, TN, preferred_element_type=F32)

        @pl.when(i > j)
        def _():
            step(False)

        @pl.when(i == j)
        def _():
            step(True)

        @pl.when(i == nq - 1)
        def _():
            dkv_ref[:, :LANES] = (jnp.where(first, dk_acc[0], _swap_halves(dk_acc[1])) * LN2).astype(BF16)
            dkv_ref[:, LANES:] = jnp.where(first, dv_acc[0], _swap_halves(dv_acc[1])).astype(BF16)
            dn_ref[...] = jnp.where(first, _swap_halves(dk_acc[0]), dk_acc[1])

        @pl.when(n == nsteps - 1)
        def _():
            first_s = lax.broadcasted_iota(jnp.int32, (S, LANES), 1) < FOX_HEAD_DIM
            dq_ref[...] = (jnp.where(first_s, dq_acc[0], _swap_halves(dq_acc[1])) * FOX_SCALE).astype(BF16)
            drow_ref[...] = jnp.where(first_s, _swap_halves(dq_acc[0]), dq_acc[1])

    qblock = pl.BlockSpec((tq, 2 * LANES), lambda g, n, it, jt: (it[n], g))
    kblock = pl.BlockSpec((tq, 2 * LANES), lambda g, n, it, jt: (jt[n], g))
    whole = pl.BlockSpec((S, LANES), lambda g, n, it, jt: (0, g))
    return pl.pallas_call(
        body, name=name,
        grid_spec=pltpu.PrefetchScalarGridSpec(
            num_scalar_prefetch=2, grid=(FOX_PAIRS, nsteps),
            in_specs=[qblock, kblock, kblock, qblock, pl.BlockSpec(memory_space=pl.ANY)],
            out_specs=[whole, kblock, pl.BlockSpec((tq, LANES), lambda g, n, it, jt: (jt[n], g)), whole],
            scratch_shapes=[pltpu.VMEM((2, S, LANES), F32), pltpu.VMEM((2, tq, LANES), F32),
                            pltpu.VMEM((2, tq, LANES), F32)]),
        out_shape=[jax.ShapeDtypeStruct(dh.shape, dh.dtype), jax.ShapeDtypeStruct((S, 2 * D_MAIN + LANES), BF16),
                   jax.ShapeDtypeStruct((S, D_MAIN), F32), jax.ShapeDtypeStruct((S, D_MAIN), F32)],
        input_output_aliases={6: 0},
        compiler_params=_cparams("parallel", "arbitrary"),
    )(it, jt, qb, ka, va, dya, dh)


TB_ROWS = 256
TB_SEQ = 512
TB_PREP = 2048
TQ_FOX_FWD = 1024
TQ_FOX_BWD = 1024


def _local_step(x, mem, target, win, wmkv, wout, pw, pscale, wkvp, wf, ln_g, ln_b, bias):
    S = x.shape[0]
    ones = jnp.ones((1, D_MAIN), F32)
    g0, b0, g1, b1 = ln_g[0:1], ln_b[0:1], ln_g[1:2], ln_b[1:2]
    mm = lambda a, b, mode, dt, tm, tn, tk, name, **kw: _mm(a, b, mode=mode, out_dtype=dt, tm=tm, tn=tn, tk=tk,
                                                            name=name, **kw)

    h0 = mm(x, win[0], "nn", BF16, 256, D_IN, D_MODEL, "l0_in")
    pm, mixed = _pool_fwd(h0, pw, tb=TB_SEQ, name="l0_pool_fwd")
    mkv0 = mm(mem, wmkv[0], "nn", F32, 256, 1024, 1024, "l0_mkv")
    ymem0 = _memattn_fwd(h0, mkv0, tb=TB_SEQ, name="l0_mem_fwd")
    yg0 = _gate_fwd(mixed, pscale, h0, ymem0, tb=TB_ROWS, name="l0_gate_fwd")
    x1, xhat0, rstd0 = _out_ln(yg0, wout[0], x, g0, b0, tb=TB_SEQ, name="l0_out_ln")

    kv = mm(x1, wkvp, "nn", BF16, 512, 2 * D_MAIN, D_MODEL, "kv_proj")
    fl = mm(x1, wf, "nn", F32, 512, LANES, D_MODEL, "f_proj")
    negcum = _forget_fwd(fl, bias, tb=TB_SEQ, name="forget_fwd")

    h1 = mm(x1, win[1], "nn", BF16, 256, D_IN, D_MODEL, "l1_in")
    qa, ka, va = _fox_prep_fwd(h1, kv, negcum, tb=TB_PREP, name="fox_prep_fwd")
    y1, lse2 = _fox_fwd(qa, ka, va, tq=TQ_FOX_FWD, name="fox_fwd")
    mkv1 = mm(mem, wmkv[1], "nn", F32, 256, 1024, 1024, "l1_mkv")
    ymem1 = _memattn_fwd(h1, mkv1, tb=TB_SEQ, name="l1_mem_fwd")
    yg1 = _gate_fwd(y1, ones, h1, ymem1, tb=TB_ROWS, name="l1_gate_fwd")
    dz1, dg1, db1, sq = _out_ln_loss(yg1, wout[1], x1, g1, b1, target, tb=TB_SEQ, name="l1_out_ln_loss")

    dwout1 = mm(yg1, dz1, "tn", BF16, D_MIX, D_MODEL, 512, "l1_dwout")
    dyg1 = mm(dz1, wout[1], "nt", BF16, 512, D_MIX, D_MODEL, "l1_dyg")
    dy1, dymem1, dh1 = _gate_bwd(dyg1, y1, ones, h1, ymem1, tb=TB_ROWS, name="l1_gate_bwd")
    qb, dya = _fox_prep_bwd(qa, lse2, dy1, y1, tb=TB_PREP, name="fox_prep_bwd")
    dh1, dh2, dnp, drowp = _fox_bwd(qb, ka, va, dya, dh1, tq=TQ_FOX_BWD, name="fox_bwd")
    dnck = (dnp.reshape(S, FOX_HEADS, FOX_HEAD_DIM)[:, :, 0]
            - drowp.reshape(S, FOX_HEADS, FOX_HEAD_DIM)[:, :, 3])
    dnck = jnp.pad(dnck, ((0, 0), (0, LANES - FOX_HEADS)))
    dh2, dbias = _forget_bwd(dnck, fl, bias, dh2, tb=TB_SEQ, name="forget_bwd")
    dh1, dmkv1 = _memattn_bwd(h1, mkv1, dymem1, dh1, tb=TB_SEQ, name="l1_mem_bwd")
    dwmkv1 = mm(mem, dmkv1, "tn", BF16, D_MODEL, 1024, N_MEM, "l1_dwmkv")
    dwin1 = mm(x1, dh1, "tn", BF16, D_MODEL, D_IN // 2, 512, "l1_dwin")
    dwkvf = mm(x1, dh2, "tn", F32, D_MODEL, 2 * D_MAIN + LANES, 512, "dwkv")
    dx1 = mm(dh1, win[1], "nt", F32, 256, D_MODEL, D_IN, "l1_dx", add=dz1, add_scale=ALPHA)
    wkvf = jnp.concatenate([wkvp, wf], axis=1)
    dx1 = mm(dh2, wkvf, "nt", F32, 256, D_MODEL, 2 * D_MAIN + LANES, "kv_dx", add=dx1)

    dz0, dg0, db0 = _ln_bwd(dx1, xhat0, rstd0, g0, tb=TB_ROWS, name="l0_ln_bwd")
    dwout0 = mm(yg0, dz0, "tn", BF16, D_MIX, D_MODEL, 512, "l0_dwout")
    dyg0 = mm(dz0, wout[0], "nt", BF16, 512, D_MIX, D_MODEL, "l0_dyg")
    dy0, dymem0, dh0 = _gate_bwd(dyg0, mixed, pscale, h0, ymem0, tb=TB_ROWS, name="l0_gate_bwd")
    dh0, dpw, dpscale = _pool_bwd(dy0, pm, mixed, pw, pscale, dh0, tb=TB_SEQ, name="l0_pool_bwd")
    dh0, dmkv0 = _memattn_bwd(h0, mkv0, dymem0, dh0, tb=TB_SEQ, name="l0_mem_bwd")
    dwmkv0 = mm(mem, dmkv0, "tn", BF16, D_MODEL, 1024, N_MEM, "l0_dwmkv")
    dwin0 = mm(x, dh0, "tn", BF16, D_MODEL, D_IN // 2, 512, "l0_dwin")
    gx = mm(dh0, win[0], "nt", F32, 256, D_MODEL, D_IN, "l0_dx", add=dz0, add_scale=ALPHA)

    dkv = dwkvf[:, :2 * D_MAIN].reshape(D_MODEL, FOX_PAIRS, 2, LANES)
    dwkv = jnp.concatenate([dkv[:, :, 0, :].reshape(D_MODEL, D_MAIN), dkv[:, :, 1, :].reshape(D_MODEL, D_MAIN),
                            dwkvf[:, 2 * D_MAIN:2 * D_MAIN + FOX_HEADS]], axis=1)
    dwkv = dwkv.reshape(D_MODEL, N_DEV, -1).transpose(1, 0, 2).astype(BF16)
    grads = dict(w_in=(dwin0, dwin1), w_mem_kv=(dwmkv0, dwmkv1), w_out=(dwout0, dwout1), pool_w=(dpw,),
                 w_kv_shared=dwkv, pool_scale=dpscale, ln_g=jnp.concatenate([dg0, dg1]),
                 ln_b=jnp.concatenate([db0, db1]), b_forget=dbias[0, :FOX_HEADS])
    return sq, gx, grads


MESH_ID = pl.DeviceIdType.MESH
HBM = pl.BlockSpec(memory_space=pl.ANY)
SLICED = {"w_in": (2, D_IN // N_DEV), "w_mem_kv": (1, D_MODEL // N_DEV), "w_out": (1, D_MIX // N_DEV),
          "pool_w": (1, POOL_GROUP // N_DEV)}


def _place():
    return lax.axis_index("x"), lax.axis_index("y"), lax.axis_index("c")


def _slot(p):
    return 4 * p[0] + 2 * p[1] + p[2]


def _cut(ref, axis, width, s):
    idx = [slice(None)] * len(ref.shape)
    idx[axis] = pl.ds(s * width, width)
    return ref.at[tuple(idx)]


def _all_gather(shards, cuts, *, name):
    nt = len(shards)

    def full_shape(a, cut):
        if cut is None:
            return (N_DEV,) + a.shape
        return a.shape[:cut[0]] + (a.shape[cut[0]] * N_DEV,) + a.shape[cut[0] + 1:]

    def body(*refs):
        ins, outs = refs[:nt], refs[nt:2 * nt]
        send_sems, recv_sems, local_sems = refs[2 * nt:]
        x, y, c = _place()
        me, sibling = (x, y, c), (x, y, 1 - c)
        chips = [(1 - x, y), (x, 1 - y), (1 - x, 1 - y)]

        def place(t, s):
            return outs[t].at[s] if cuts[t] is None else _cut(outs[t], cuts[t][0], cuts[t][1], s)

        def copies(k, block, to, from_input=False):
            s = _slot(block)
            return [pltpu.make_async_remote_copy(
                src_ref=ins[t] if from_input else place(t, s), dst_ref=place(t, s),
                send_sem=send_sems.at[nt * k + t], recv_sem=recv_sems.at[nt * k + t],
                device_id=to, device_id_type=MESH_ID) for t in range(nt)]

        mine = [pltpu.make_async_copy(ins[t], place(t, _slot(me)), local_sems.at[t]) for t in range(nt)]
        for cp in mine:
            cp.start()
        first = [copies(0, me, sibling, True)] + [copies(1 + j, me, (*chip, c), True) for j, chip in enumerate(chips)]
        for group in first:
            for cp in group:
                cp.start()
        passed = [copies(4 + j, (*chip, c), sibling) for j, chip in enumerate(chips)]
        for j, chip in enumerate(chips):
            for cp in copies(1 + j, (*chip, c), me):
                cp.wait_recv()
            for cp in passed[j]:
                cp.start()
        for cp in copies(0, sibling, me):
            cp.wait_recv()
        for j, chip in enumerate(chips):
            for cp in copies(4 + j, (*chip, 1 - c), me):
                cp.wait_recv()
        for group in first + passed:
            for cp in group:
                cp.wait_send()
        for cp in mine:
            cp.wait()

    return pl.pallas_call(
        body, name=name, in_specs=[HBM] * nt, out_specs=[HBM] * nt,
        out_shape=[jax.ShapeDtypeStruct(full_shape(a, cut), a.dtype) for a, cut in zip(shards, cuts)],
        scratch_shapes=[pltpu.SemaphoreType.DMA((7 * nt,)), pltpu.SemaphoreType.DMA((7 * nt,)),
                        pltpu.SemaphoreType.DMA((nt,))],
    )(*shards)


def _exchange(srcs, cuts, layers, *, name):
    nt = len(srcs)
    groups = []
    dest_of = []
    for t in range(nt):
        if cuts[t] is not None and layers[t] is not None and layers[t] > 0:
            dest_of.append(dest_of[-1])
            groups[-1][1] += 1
        else:
            dest_of.append(len(groups))
            groups.append([t, 1])
    nd = len(groups)

    def dest_shape(t0, nl):
        a, cut = srcs[t0], cuts[t0]
        if cut is None:
            return a.shape
        shard = a.shape[:cut[0]] + (a.shape[cut[0]] // N_DEV,) + a.shape[cut[0] + 1:]
        return (N_DEV,) + ((nl,) if layers[t0] is not None else ()) + shard

    def body(*refs):
        ins, outs = refs[:nt], refs[nt:nt + nd]
        send_sems, recv_sems, local_sems = refs[nt + nd:]
        x, y, c = _place()
        me = _slot((x, y, c))
        flip = lambda v, bit: 1 - v if bit else v

        def src(t, s):
            return ins[t].at[s] if cuts[t] is None else _cut(ins[t], cuts[t][0], cuts[t][1], s)

        def dst(t, s):
            d = outs[dest_of[t]].at[s]
            return d if cuts[t] is None or layers[t] is None else d.at[layers[t]]

        started = []
        for k in range(1, N_DEV):
            peer = (flip(x, k & 4), flip(y, k & 2), flip(c, k & 1))
            ps = _slot(peer)
            for t in range(nt):
                kk = nt * (k - 1) + t
                cp = pltpu.make_async_remote_copy(
                    src_ref=src(t, ps), dst_ref=dst(t, me), send_sem=send_sems.at[kk], recv_sem=recv_sems.at[kk],
                    device_id=peer, device_id_type=MESH_ID)
                cp.start()
                started.append((cp, pltpu.make_async_remote_copy(
                    src_ref=src(t, ps), dst_ref=dst(t, ps), send_sem=send_sems.at[kk], recv_sem=recv_sems.at[kk],
                    device_id=peer, device_id_type=MESH_ID)))
        mine = [pltpu.make_async_copy(src(t, me), dst(t, me), local_sems.at[t]) for t in range(nt)]
        for cp in mine:
            cp.start()
        for sent, landed in started:
            landed.wait_recv()
            sent.wait_send()
        for cp in mine:
            cp.wait()

    return pl.pallas_call(
        body, name=name, in_specs=[HBM] * nt, out_specs=[HBM] * nd,
        out_shape=[jax.ShapeDtypeStruct(dest_shape(t0, nl), srcs[t0].dtype) for t0, nl in groups],
        scratch_shapes=[pltpu.SemaphoreType.DMA((7 * nt,)), pltpu.SemaphoreType.DMA((7 * nt,)),
                        pltpu.SemaphoreType.DMA((nt,))],
    )(*srcs)


def _adamw(recv, w, m, v, *, split, name):
    shape = w.shape
    axis, parts = split
    block = shape[:axis] + (shape[axis] // parts,) + shape[axis + 1:]
    nd = len(shape)

    def body(r_ref, w_ref, m_ref, v_ref, g_ref, d_ref, nm_ref, nv_ref):
        g = r_ref[0].astype(F32)
        for j in range(1, N_DEV):
            g = g + r_ref[j].astype(F32)
        nm = ADAM_B1 * m_ref[...] + (1.0 - ADAM_B1) * g
        nv = ADAM_B2 * v_ref[...] + (1.0 - ADAM_B2) * (g * g)
        m_hat = nm / (1.0 - ADAM_B1 ** ADAM_STEP)
        v_hat = nv / (1.0 - ADAM_B2 ** ADAM_STEP)
        g_ref[...] = g
        nm_ref[...] = nm
        nv_ref[...] = nv
        d_ref[...] = -ADAM_LR * (m_hat / (jnp.sqrt(v_hat) + ADAM_EPS) + ADAM_WD * w_ref[...])

    at = lambda i: tuple(i if a == axis else 0 for a in range(nd))
    one = pl.BlockSpec(block, at)
    shp = jax.ShapeDtypeStruct(shape, F32)
    return pl.pallas_call(
        body, name=name, grid=(parts,),
        in_specs=[pl.BlockSpec((N_DEV,) + block, lambda i: (0,) + at(i)), one, one, one],
        out_specs=[one, one, one, one], out_shape=[shp, shp, shp, shp],
        compiler_params=_cparams("parallel"),
    )(recv, w, m, v)


BIG = ("w_in", "w_mem_kv", "w_out", "pool_w", "w_kv_shared")
SMALL = ("pool_scale", "ln_g", "ln_b", "b_forget")
SMALL_ROWS = 40
ADAM_SPLIT = {"w_in": (1, 4), "w_mem_kv": (0, 2), "w_out": (0, 2), "pool_w": (0, 1), "w_kv_shared": (0, 4)}


def _flat(parts, rows):
    v = jnp.concatenate([p.reshape(-1) for p in parts])
    return jnp.pad(v, (0, rows * LANES - v.shape[0])).reshape(rows, LANES)


def _unflat(flat, shapes):
    v, out, off = flat.reshape(-1), [], 0
    for s in shapes:
        n = math.prod(s)
        out.append(v[off:off + n].reshape(s))
        off += n
    return out


def kernel(x, mem, w_in, w_mem_kv, w_out, ln_g, ln_b, pool_w, pool_scale, w_kv_shared, b_forget, loss_target, m_w_in, m_w_mem_kv, m_w_out, m_ln_g, m_ln_b, m_pool_w, m_pool_scale, m_w_kv_shared, m_b_forget, v_w_in, v_w_mem_kv, v_w_out, v_ln_g, v_ln_b, v_pool_w, v_pool_scale, v_w_kv_shared, v_b_forget):
    w = dict(w_in=w_in, w_mem_kv=w_mem_kv, w_out=w_out, ln_g=ln_g, ln_b=ln_b, pool_w=pool_w[0],
             pool_scale=pool_scale, w_kv_shared=w_kv_shared, b_forget=b_forget)
    m = dict(w_in=m_w_in, w_mem_kv=m_w_mem_kv, w_out=m_w_out, ln_g=m_ln_g, ln_b=m_ln_b, pool_w=m_pool_w[0],
             pool_scale=m_pool_scale, w_kv_shared=m_w_kv_shared, b_forget=m_b_forget)
    v = dict(w_in=v_w_in, w_mem_kv=v_w_mem_kv, w_out=v_w_out, ln_g=v_ln_g, ln_b=v_ln_b, pool_w=v_pool_w[0],
             pool_scale=v_pool_scale, w_kv_shared=v_w_kv_shared, b_forget=v_b_forget)

    names = BIG + ("pool_scale",)
    full = dict(zip(names, _all_gather(
        [w[n].astype(BF16) for n in BIG] + [jnp.pad(pool_scale, ((0, 7), (0, 0)))],
        [SLICED.get(n) for n in names], name="gather_weights")))
    pscale = full["pool_scale"][:, 0, :].reshape(1, D_MAIN)
    wkv = full["w_kv_shared"].transpose(1, 0, 2).reshape(D_MODEL, -1)
    wkvp = jnp.stack([wkv[:, :D_MAIN].reshape(D_MODEL, FOX_PAIRS, LANES),
                      wkv[:, D_MAIN:2 * D_MAIN].reshape(D_MODEL, FOX_PAIRS, LANES)], axis=2).reshape(D_MODEL, 2 * D_MAIN)
    wf = jnp.pad(wkv[:, 2 * D_MAIN:], ((0, 0), (0, LANES - FOX_HEADS)))
    bias = jnp.pad(b_forget, (0, LANES - FOX_HEADS)).reshape(1, LANES)

    sq, gx, grads = _local_step(x[0], mem[0], loss_target[0], full["w_in"], full["w_mem_kv"], full["w_out"],
                                full["pool_w"], pscale, wkvp, wf, ln_g, ln_b, bias)
    loss = lax.psum((0.5 / D_MODEL) * jnp.sum(sq), ("x", "y", "c"))

    small = jnp.concatenate([grads["pool_scale"].reshape(N_DEV, -1)]
                            + [jnp.broadcast_to(grads[n].reshape(1, -1), (N_DEV, grads[n].size)) for n in SMALL[1:]], axis=1)
    small = jnp.pad(small, ((0, 0), (0, SMALL_ROWS * LANES - small.shape[1]))).reshape(N_DEV, SMALL_ROWS, LANES)
    srcs, cuts, layers = [], [], []
    for n in BIG:
        if n in SLICED:
            parts = grads[n]
            for l, a in enumerate(parts):
                srcs.append(a)
                cuts.append((SLICED[n][0] - (1 if n != "pool_w" else 0), SLICED[n][1]))
                layers.append(l if n != "pool_w" else None)
        else:
            srcs.append(grads[n])
            cuts.append(None)
            layers.append(None)
    recv = _exchange(srcs + [small], cuts + [None], layers + [None], name="exchange_grads")
    recv = dict(zip(BIG + ("small",), recv))

    outs = {}
    for n in BIG:
        res = _adamw(recv[n], w[n], m[n], v[n], split=ADAM_SPLIT[n], name="adamw_" + n)
        for kind, a in zip(("grad", "delta", "new_m", "new_v"), res):
            outs[kind, n] = a[None] if n == "pool_w" else a
    small_shapes = [w[n].shape for n in SMALL]
    res = _adamw(recv["small"], _flat([w[n] for n in SMALL], SMALL_ROWS), _flat([m[n] for n in SMALL], SMALL_ROWS),
                 _flat([v[n] for n in SMALL], SMALL_ROWS), split=(0, 1), name="adamw_small")
    for kind, flat in zip(("grad", "delta", "new_m", "new_v"), res):
        for n, a in zip(SMALL, _unflat(flat, small_shapes)):
            outs[kind, n] = a
    order = ("w_in", "w_mem_kv", "w_out", "ln_g", "ln_b", "pool_w", "pool_scale", "w_kv_shared", "b_forget")
    return (loss, gx[None], *[outs[kind, n] for kind in ("grad", "delta", "new_m", "new_v") for n in order])
```

```python
import math

import numpy as np
import jax
import jax.numpy as jnp
from jax import lax
from jax.experimental import pallas as pl
from jax.experimental.pallas import tpu as pltpu

F32 = jnp.float32
BF16 = jnp.bfloat16

D_MODEL = 1024
D_MAIN = 1024
D_MEM = 512
D_MIX = D_MAIN + D_MEM
D_IN = 2 * D_MIX
N_MEM = 256
MEM_HEADS = 4
MEM_HEAD_DIM = 128
FOX_HEADS = 16
FOX_HEAD_DIM = 64
FOX_PAIRS = FOX_HEADS // 2
POOL_WINDOWS = (2, 4, 8, 16)
POOL_GROUP = 256
POOL_HALO = 16
ALPHA = 4.0 ** 0.25
LN_EPS = 1e-5
NEG = -1e30
LANES = 128
N_DEV = 8

ADAM_LR = 0.001
ADAM_B1 = 0.9
ADAM_B2 = 0.999
ADAM_EPS = 1e-08
ADAM_WD = 0.01
ADAM_STEP = 10

VMEM_LIMIT = 56 * 1024 * 1024

NN = (((1,), (0,)), ((), ()))
NT = (((1,), (1,)), ((), ()))
TN = (((0,), (0,)), ((), ()))


def _cparams(*sem):
    return pltpu.CompilerParams(dimension_semantics=sem, vmem_limit_bytes=VMEM_LIMIT)


def _sigmoid(z):
    return 1.0 / (1.0 + jnp.exp(-z))


def _mm(a, b, *, mode, out_dtype, tm, tn, tk, name, add=None, add_scale=1.0):
    if mode == "nn":
        (M, K), (K2, N) = a.shape, b.shape
    elif mode == "nt":
        (M, K), (N, K2) = a.shape, b.shape
    else:
        (K, M), (K2, N) = a.shape, b.shape
    assert K == K2, (a.shape, b.shape, mode)
    tm, tn, tk = min(tm, M), min(tn, N), min(tk, K)
    assert M % tm == 0 and N % tn == 0 and K % tk == 0, (M, N, K, tm, tn, tk)
    gm, gn, gk = M // tm, N // tn, K // tk
    dims = {"nn": NN, "nt": NT, "tn": TN}[mode]
    if mode == "tn":
        a_spec = pl.BlockSpec((tk, tm), lambda i, j, k: (k, i))
    else:
        a_spec = pl.BlockSpec((tm, tk), lambda i, j, k: (i, k))
    if mode == "nt":
        b_spec = pl.BlockSpec((tn, tk), lambda i, j, k: (j, k))
    else:
        b_spec = pl.BlockSpec((tk, tn), lambda i, j, k: (k, j))
    o_spec = pl.BlockSpec((tm, tn), lambda i, j, k: (i, j))
    has_add = add is not None
    acc_in_out = out_dtype == F32

    def body(*refs):
        a_ref, b_ref = refs[0], refs[1]
        add_ref = refs[2] if has_add else None
        o_ref = refs[3] if has_add else refs[2]
        prod = lax.dot_general(a_ref[...].astype(BF16), b_ref[...].astype(BF16), dims,
                               preferred_element_type=F32)

        def finish(r):
            if has_add:
                r = r + add_scale * add_ref[...]
            o_ref[...] = r.astype(out_dtype)

        if gk == 1:
            finish(prod)
        else:
            acc_ref = o_ref if acc_in_out else refs[-1]
            k = pl.program_id(2)

            @pl.when(k == 0)
            def _():
                acc_ref[...] = prod

            @pl.when(k > 0)
            def _():
                acc_ref[...] += prod

            if has_add or not acc_in_out:
                @pl.when(k == gk - 1)
                def _():
                    finish(acc_ref[...])

    in_specs = [a_spec, b_spec] + ([o_spec] if has_add else [])
    args = (a, b) + ((add,) if has_add else ())
    return pl.pallas_call(
        body, name=name, grid=(gm, gn, gk), in_specs=in_specs, out_specs=o_spec,
        out_shape=jax.ShapeDtypeStruct((M, N), out_dtype),
        scratch_shapes=[pltpu.VMEM((tm, tn), F32)] if gk > 1 and not acc_in_out else [],
        compiler_params=_cparams("parallel", "parallel", "arbitrary"),
    )(*args)


def _ln_stats(z):
    mu = jnp.mean(z, axis=1, keepdims=True)
    zc = z - mu
    var = jnp.mean(zc * zc, axis=1, keepdims=True)
    rstd = lax.rsqrt(var + LN_EPS)
    return zc * rstd, rstd


def _ln_bwd_math(dy, xhat, rstd, g):
    dxh = dy * g
    m1 = jnp.mean(dxh, axis=1, keepdims=True)
    m2 = jnp.mean(dxh * xhat, axis=1, keepdims=True)
    return rstd * (dxh - m1 - xhat * m2)


def _out_ln(yg, wout, x, g, b, *, tb, name):
    S = x.shape[0]
    tb = min(tb, S)

    def body(yg_ref, w_ref, x_ref, g_ref, b_ref, y_ref, xhat_ref, rstd_ref):
        o = jnp.dot(yg_ref[...], w_ref[...], preferred_element_type=F32)
        xhat, rstd = _ln_stats(ALPHA * x_ref[...] + o)
        xhat_ref[...] = xhat
        rstd_ref[...] = rstd
        y_ref[...] = xhat * g_ref[...] + b_ref[...]

    row = pl.BlockSpec((tb, D_MODEL), lambda i: (i, 0))
    vec = pl.BlockSpec((1, D_MODEL), lambda i: (0, 0))
    return pl.pallas_call(
        body, name=name, grid=(S // tb,),
        in_specs=[pl.BlockSpec((tb, D_MIX), lambda i: (i, 0)), pl.BlockSpec((D_MIX, D_MODEL), lambda i: (0, 0)),
                  row, vec, vec],
        out_specs=[row, row, pl.BlockSpec((tb, 1), lambda i: (i, 0))],
        out_shape=[jax.ShapeDtypeStruct((S, D_MODEL), F32), jax.ShapeDtypeStruct((S, D_MODEL), F32),
                   jax.ShapeDtypeStruct((S, 1), F32)],
        compiler_params=_cparams("parallel"),
    )(yg, wout, x, g, b)


def _out_ln_loss(yg, wout, x, g, b, target, *, tb, name):
    S = x.shape[0]
    tb = min(tb, S)

    def body(yg_ref, w_ref, x_ref, g_ref, b_ref, t_ref, dz_ref, dg_ref, db_ref, sq_ref):
        @pl.when(pl.program_id(0) == 0)
        def _():
            dg_ref[...] = jnp.zeros_like(dg_ref)
            db_ref[...] = jnp.zeros_like(db_ref)
            sq_ref[...] = jnp.zeros_like(sq_ref)

        o = jnp.dot(yg_ref[...], w_ref[...], preferred_element_type=F32)
        xhat, rstd = _ln_stats(ALPHA * x_ref[...] + o)
        err = xhat * g_ref[...] + b_ref[...] - t_ref[...]
        sq_ref[...] += jnp.sum(err * err, axis=0, keepdims=True)
        dy = err * (1.0 / D_MODEL)
        dz_ref[...] = _ln_bwd_math(dy, xhat, rstd, g_ref[...])
        dg_ref[...] += jnp.sum(dy * xhat, axis=0, keepdims=True)
        db_ref[...] += jnp.sum(dy, axis=0, keepdims=True)

    row = pl.BlockSpec((tb, D_MODEL), lambda i: (i, 0))
    vec = pl.BlockSpec((1, D_MODEL), lambda i: (0, 0))
    vshape = jax.ShapeDtypeStruct((1, D_MODEL), F32)
    return pl.pallas_call(
        body, name=name, grid=(S // tb,),
        in_specs=[pl.BlockSpec((tb, D_MIX), lambda i: (i, 0)), pl.BlockSpec((D_MIX, D_MODEL), lambda i: (0, 0)),
                  row, vec, vec, row],
        out_specs=[row, vec, vec, vec],
        out_shape=[jax.ShapeDtypeStruct((S, D_MODEL), F32), vshape, vshape, vshape],
        compiler_params=_cparams("arbitrary"),
    )(yg, wout, x, g, b, target)


def _ln_bwd(dy, xhat, rstd, g, *, tb, name):
    S = dy.shape[0]
    tb = min(tb, S)

    def body(dy_ref, xhat_ref, rstd_ref, g_ref, dz_ref, dg_ref, db_ref):
        @pl.when(pl.program_id(0) == 0)
        def _():
            dg_ref[...] = jnp.zeros_like(dg_ref)
            db_ref[...] = jnp.zeros_like(db_ref)

        dy_, xhat_ = dy_ref[...], xhat_ref[...]
        dz_ref[...] = _ln_bwd_math(dy_, xhat_, rstd_ref[...], g_ref[...])
        dg_ref[...] += jnp.sum(dy_ * xhat_, axis=0, keepdims=True)
        db_ref[...] += jnp.sum(dy_, axis=0, keepdims=True)

    row = pl.BlockSpec((tb, D_MODEL), lambda i: (i, 0))
    vec = pl.BlockSpec((1, D_MODEL), lambda i: (0, 0))
    return pl.pallas_call(
        body, name=name, grid=(S // tb,),
        in_specs=[row, row, pl.BlockSpec((tb, 1), lambda i: (i, 0)), vec],
        out_specs=[row, vec, vec],
        out_shape=[jax.ShapeDtypeStruct((S, D_MODEL), F32), jax.ShapeDtypeStruct((1, D_MODEL), F32),
                   jax.ShapeDtypeStruct((1, D_MODEL), F32)],
        compiler_params=_cparams("arbitrary"),
    )(dy, xhat, rstd, g)


def _gate_fwd(ysrc, scale, h, ymem, *, tb, name):
    S = ysrc.shape[0]
    tb = min(tb, S)

    def body(ys_ref, sc_ref, ga_ref, gb_ref, gc_ref, ym_ref, yg_ref):
        ymain = ys_ref[...] * sc_ref[...]
        for k, g_ref in enumerate((ga_ref, gb_ref)):
            gv = g_ref[...].astype(F32)
            yg_ref[:, 512 * k:512 * (k + 1)] = (ymain[:, 512 * k:512 * (k + 1)] * gv * _sigmoid(gv)).astype(BF16)
        gv = gc_ref[...].astype(F32)
        yg_ref[:, 1024:1536] = (ym_ref[...] * gv * _sigmoid(gv)).astype(BF16)

    slab = lambda c: pl.BlockSpec((tb, 512), lambda i, c=c: (i, c))
    return pl.pallas_call(
        body, name=name, grid=(S // tb,),
        in_specs=[pl.BlockSpec((tb, D_MAIN), lambda i: (i, 0)), pl.BlockSpec((1, D_MAIN), lambda i: (0, 0)),
                  slab(3), slab(4), slab(5), pl.BlockSpec((tb, D_MEM), lambda i: (i, 0))],
        out_specs=pl.BlockSpec((tb, D_MIX), lambda i: (i, 0)),
        out_shape=jax.ShapeDtypeStruct((S, D_MIX), BF16),
        compiler_params=_cparams("parallel"),
    )(ysrc, scale, h, h, h, ymem)


def _gate_bwd(dyg, ysrc, scale, h, ymem, *, tb, name):
    S = ysrc.shape[0]
    tb = min(tb, S)

    def dsilu(gv):
        sg = _sigmoid(gv)
        return sg, sg * (1.0 + gv * (1.0 - sg))

    def body(da_ref, db_ref, dc_ref, ys_ref, sc_ref, ga_ref, gb_ref, gc_ref, ym_ref, dym_ref, dymem_ref, dh_ref):
        ymain = ys_ref[...] * sc_ref[...]
        for k, (d_ref, g_ref) in enumerate(((da_ref, ga_ref), (db_ref, gb_ref))):
            gv, d = g_ref[...].astype(F32), d_ref[...].astype(F32)
            sg, ds = dsilu(gv)
            dym_ref[:, 512 * k:512 * (k + 1)] = d * gv * sg
            dh_ref[:, 512 * k:512 * (k + 1)] = (d * ymain[:, 512 * k:512 * (k + 1)] * ds).astype(BF16)
        gv, d = gc_ref[...].astype(F32), dc_ref[...].astype(F32)
        sg, ds = dsilu(gv)
        dymem_ref[...] = d * gv * sg
        dh_ref[:, 1024:1536] = (d * ym_ref[...] * ds).astype(BF16)

    slab = lambda c: pl.BlockSpec((tb, 512), lambda i, c=c: (i, c))
    return pl.pallas_call(
        body, name=name, grid=(S // tb,),
        in_specs=[slab(0), slab(1), slab(2),
                  pl.BlockSpec((tb, D_MAIN), lambda i: (i, 0)), pl.BlockSpec((1, D_MAIN), lambda i: (0, 0)),
                  slab(3), slab(4), slab(5), pl.BlockSpec((tb, D_MEM), lambda i: (i, 0))],
        out_specs=[pl.BlockSpec((tb, D_MAIN), lambda i: (i, 0)), pl.BlockSpec((tb, D_MEM), lambda i: (i, 0)),
                   pl.BlockSpec((tb, D_MIX), lambda i: (i, 1))],
        out_shape=[jax.ShapeDtypeStruct((S, D_MAIN), F32), jax.ShapeDtypeStruct((S, D_MEM), F32),
                   jax.ShapeDtypeStruct((S, D_IN), BF16)],
        compiler_params=_cparams("parallel"),
    )(dyg, dyg, dyg, ysrc, scale, h, h, h, ymem)


def _window_count(t0, rows, w):
    t = t0 + lax.broadcasted_iota(jnp.int32, (rows, POOL_GROUP), 0)
    return jnp.minimum(t + 1, w).astype(F32)


def _pool_fwd(h, pw, *, tb, name):
    S = h.shape[0]
    tb = min(tb, S)

    def body(u_ref, pw_ref, pm_ref, mixed_ref, tail_ref):
        i = pl.program_id(0)

        @pl.when(i == 0)
        def _():
            tail_ref[...] = jnp.zeros_like(tail_ref)

        u = u_ref[...].astype(F32)
        xfull = jnp.concatenate([tail_ref[...], u], axis=0)
        for gi, w in enumerate(POOL_WINDOWS):
            cols = slice(POOL_GROUP * gi, POOL_GROUP * (gi + 1))
            s = xfull[:, cols]
            sh = 1
            while sh < w:
                s = s + pltpu.roll(s, sh, 0)
                sh *= 2
            pm = s[POOL_HALO:, :] / _window_count(i * tb, tb, w) - u[:, cols]
            pmb = pm.astype(BF16)
            pm_ref[:, cols] = pmb
            mixed_ref[:, cols] = jnp.dot(pmb, pw_ref[gi], preferred_element_type=F32)
        tail_ref[...] = u[tb - POOL_HALO:, :]

    return pl.pallas_call(
        body, name=name, grid=(S // tb,),
        in_specs=[pl.BlockSpec((tb, D_MAIN), lambda i: (i, 0)),
                  pl.BlockSpec((4, POOL_GROUP, POOL_GROUP), lambda i: (0, 0, 0))],
        out_specs=[pl.BlockSpec((tb, D_MAIN), lambda i: (i, 0)), pl.BlockSpec((tb, D_MAIN), lambda i: (i, 0))],
        out_shape=[jax.ShapeDtypeStruct((S, D_MAIN), BF16), jax.ShapeDtypeStruct((S, D_MAIN), F32)],
        scratch_shapes=[pltpu.VMEM((POOL_HALO, D_MAIN), F32)],
        compiler_params=_cparams("arbitrary"),
    )(h, pw)


def _pool_bwd(dymain, pm, mixed, pw, scale, dh, *, tb, name):
    S = dymain.shape[0]
    tb = min(tb, S)
    nb = S // tb
    n = tb + POOL_HALO

    def body(dy_ref, pm_ref, mixed_ref, pw_ref, sc_ref, dh_in, dh_ref, dpw_ref, dsc_ref, head_ref, dpw_acc):
        del dh_in
        i = pl.program_id(0)

        @pl.when(i == 0)
        def _():
            head_ref[...] = jnp.zeros_like(head_ref)
            dpw_acc[...] = jnp.zeros_like(dpw_acc)
            dsc_ref[...] = jnp.zeros_like(dsc_ref)

        dy = dy_ref[...]
        dsc_ref[...] += jnp.sum(dy * mixed_ref[...], axis=0, keepdims=True)
        dmixed = dy * sc_ref[...]
        t0 = (nb - 1 - i) * tb
        for gi, w in enumerate(POOL_WINDOWS):
            cols = slice(POOL_GROUP * gi, POOL_GROUP * (gi + 1))
            dm = dmixed[:, cols].astype(BF16)
            dpw_acc[gi] += lax.dot_general(pm_ref[:, cols], dm, TN, preferred_element_type=F32)
            dpm = lax.dot_general(dm, pw_ref[gi], NT, preferred_element_type=F32)
            e = dpm / _window_count(t0, tb, w)
            s = jnp.concatenate([e, head_ref[:, cols]], axis=0)
            sh = 1
            while sh < w:
                s = s + pltpu.roll(s, n - sh, 0)
                sh *= 2
            dh_ref[:, cols] = (s[:tb, :] - dpm).astype(BF16)
            head_ref[:, cols] = e[:POOL_HALO, :]

        @pl.when(i == nb - 1)
        def _():
            dpw_ref[...] = dpw_acc[...].astype(BF16)

    rev = lambda i: (nb - 1 - i, 0)
    return pl.pallas_call(
        body, name=name, grid=(nb,),
        in_specs=[pl.BlockSpec((tb, D_MAIN), rev), pl.BlockSpec((tb, D_MAIN), rev), pl.BlockSpec((tb, D_MAIN), rev),
                  pl.BlockSpec((4, POOL_GROUP, POOL_GROUP), lambda i: (0, 0, 0)),
                  pl.BlockSpec((1, D_MAIN), lambda i: (0, 0)), pl.BlockSpec(memory_space=pl.ANY)],
        out_specs=[pl.BlockSpec((tb, D_MAIN), rev),
                   pl.BlockSpec((4, POOL_GROUP, POOL_GROUP), lambda i: (0, 0, 0)),
                   pl.BlockSpec((1, D_MAIN), lambda i: (0, 0))],
        out_shape=[jax.ShapeDtypeStruct(dh.shape, dh.dtype),
                   jax.ShapeDtypeStruct((4, POOL_GROUP, POOL_GROUP), BF16), jax.ShapeDtypeStruct((1, D_MAIN), F32)],
        scratch_shapes=[pltpu.VMEM((POOL_HALO, D_MAIN), F32), pltpu.VMEM((4, POOL_GROUP, POOL_GROUP), F32)],
        input_output_aliases={5: 0},
        compiler_params=_cparams("arbitrary"),
    )(dymain, pm, mixed, pw, scale, dh)


MEM_SCALE = MEM_HEAD_DIM ** -0.5


def _mem_probs(q_ref, mkv_ref, hd):
    cols = slice(MEM_HEAD_DIM * hd, MEM_HEAD_DIM * (hd + 1))
    q = (q_ref[:, cols].astype(F32) * MEM_SCALE).astype(BF16)
    mk = mkv_ref[:, cols].astype(BF16)
    mv = mkv_ref[:, D_MEM + MEM_HEAD_DIM * hd:D_MEM + MEM_HEAD_DIM * (hd + 1)].astype(BF16)
    s = lax.dot_general(q, mk, NT, preferred_element_type=F32)
    e = jnp.exp(s - jnp.max(s, axis=1, keepdims=True))
    return cols, q, mk, mv, e, jnp.sum(e, axis=1, keepdims=True)


def _memattn_fwd(h, mkv, *, tb, name):
    S = h.shape[0]
    tb = min(tb, S)

    def body(q_ref, mkv_ref, y_ref):
        for hd in range(MEM_HEADS):
            cols, _, _, mv, e, l = _mem_probs(q_ref, mkv_ref, hd)
            y_ref[:, cols] = jnp.dot(e.astype(BF16), mv, preferred_element_type=F32) / l

    return pl.pallas_call(
        body, name=name, grid=(S // tb,),
        in_specs=[pl.BlockSpec((tb, D_MEM), lambda i: (i, 2)), pl.BlockSpec((N_MEM, 2 * D_MEM), lambda i: (0, 0))],
        out_specs=pl.BlockSpec((tb, D_MEM), lambda i: (i, 0)),
        out_shape=jax.ShapeDtypeStruct((S, D_MEM), F32),
        compiler_params=_cparams("parallel"),
    )(h, mkv)


def _memattn_bwd(h, mkv, dy, dh, *, tb, name):
    S = h.shape[0]
    tb = min(tb, S)

    def body(q_ref, mkv_ref, dy_ref, dh_in, dh_ref, dmkv_ref):
        del dh_in

        @pl.when(pl.program_id(0) == 0)
        def _():
            dmkv_ref[...] = jnp.zeros_like(dmkv_ref)

        for hd in range(MEM_HEADS):
            cols, q, mk, mv, e, l = _mem_probs(q_ref, mkv_ref, hd)
            p = e / l
            dyh = dy_ref[:, cols].astype(BF16)
            dp = lax.dot_general(dyh, mv, NT, preferred_element_type=F32)
            ds = p * (dp - jnp.sum(dp * p, axis=1, keepdims=True))
            dsb = ds.astype(BF16)
            dh_ref[:, cols] = (jnp.dot(dsb, mk, preferred_element_type=F32) * MEM_SCALE).astype(BF16)
            dmkv_ref[:, cols] += lax.dot_general(dsb, q, TN, preferred_element_type=F32)
            vcols = slice(D_MEM + MEM_HEAD_DIM * hd, D_MEM + MEM_HEAD_DIM * (hd + 1))
            dmkv_ref[:, vcols] += lax.dot_general(p.astype(BF16), dyh, TN, preferred_element_type=F32)

    return pl.pallas_call(
        body, name=name, grid=(S // tb,),
        in_specs=[pl.BlockSpec((tb, D_MEM), lambda i: (i, 2)), pl.BlockSpec((N_MEM, 2 * D_MEM), lambda i: (0, 0)),
                  pl.BlockSpec((tb, D_MEM), lambda i: (i, 0)), pl.BlockSpec(memory_space=pl.ANY)],
        out_specs=[pl.BlockSpec((tb, D_MEM), lambda i: (i, 2)), pl.BlockSpec((N_MEM, 2 * D_MEM), lambda i: (0, 0))],
        out_shape=[jax.ShapeDtypeStruct(dh.shape, dh.dtype), jax.ShapeDtypeStruct((N_MEM, 2 * D_MEM), F32)],
        input_output_aliases={3: 0},
        compiler_params=_cparams("arbitrary"),
    )(h, mkv, dy, dh)


def _forget_fwd(fl, bias, *, tb, name):
    S = fl.shape[0]
    tb = min(tb, S)

    def body(fl_ref, b_ref, o_ref, carry_ref):
        @pl.when(pl.program_id(0) == 0)
        def _():
            carry_ref[...] = jnp.zeros_like(carry_ref)

        z = fl_ref[...] + b_ref[...]
        lf = jnp.minimum(z, 0.0) - jnp.log(1.0 + jnp.exp(-jnp.abs(z)))
        row = lax.broadcasted_iota(jnp.int32, (tb, LANES), 0)
        c = lf
        sh = 1
        while sh < tb:
            c = c + jnp.where(row >= sh, pltpu.roll(c, sh, 0), 0.0)
            sh *= 2
        o_ref[...] = -(carry_ref[...] + c)
        carry_ref[...] += jnp.sum(lf, axis=0, keepdims=True)

    return pl.pallas_call(
        body, name=name, grid=(S // tb,),
        in_specs=[pl.BlockSpec((tb, LANES), lambda i: (i, 0)), pl.BlockSpec((1, LANES), lambda i: (0, 0))],
        out_specs=pl.BlockSpec((tb, LANES), lambda i: (i, 0)),
        out_shape=jax.ShapeDtypeStruct((S, LANES), F32),
        scratch_shapes=[pltpu.VMEM((1, LANES), F32)],
        compiler_params=_cparams("arbitrary"),
    )(fl, bias)


def _forget_bwd(dn, drow, fl, bias, dh2, *, tb, name):
    S = fl.shape[0]
    tb = min(tb, S)
    nb = S // tb

    def body(dn_ref, dr_ref, fl_ref, b_ref, dh_in, dh_ref, db_ref, carry_ref):
        del dh_in

        @pl.when(pl.program_id(0) == 0)
        def _():
            carry_ref[...] = jnp.zeros_like(carry_ref)
            db_ref[...] = jnp.zeros_like(db_ref)

        src = lax.broadcasted_iota(jnp.int32, (D_MAIN, LANES), 0)
        head = lax.broadcasted_iota(jnp.int32, (D_MAIN, LANES), 1)
        pick = lambda off: jnp.where((src == FOX_HEAD_DIM * head + off) & (head < FOX_HEADS), 1.0, 0.0)
        hdot = lambda a, sel: jnp.dot(a, sel, precision=lax.Precision.HIGHEST, preferred_element_type=F32)
        dcum = hdot(dr_ref[...], pick(3)) - hdot(dn_ref[...], pick(0))
        row = lax.broadcasted_iota(jnp.int32, (tb, LANES), 0)
        c = dcum
        sh = 1
        while sh < tb:
            c = c + jnp.where(row < tb - sh, pltpu.roll(c, tb - sh, 0), 0.0)
            sh *= 2
        dlf = carry_ref[...] + c
        carry_ref[...] += jnp.sum(dcum, axis=0, keepdims=True)
        z = fl_ref[...] + b_ref[...]
        lane = lax.broadcasted_iota(jnp.int32, (tb, LANES), 1)
        dfl = jnp.where(lane < FOX_HEADS, dlf / (1.0 + jnp.exp(z)), 0.0)
        db_ref[...] += jnp.sum(dfl, axis=0, keepdims=True)
        dh_ref[...] = dfl.astype(BF16)

    rev = lambda i: (nb - 1 - i, 0)
    return pl.pallas_call(
        body, name=name, grid=(nb,),
        in_specs=[pl.BlockSpec((tb, D_MAIN), rev), pl.BlockSpec((tb, D_MAIN), rev), pl.BlockSpec((tb, LANES), rev),
                  pl.BlockSpec((1, LANES), lambda i: (0, 0)), pl.BlockSpec(memory_space=pl.ANY)],
        out_specs=[pl.BlockSpec((tb, LANES), lambda i: (nb - 1 - i, 2 * D_MAIN // LANES)),
                   pl.BlockSpec((1, LANES), lambda i: (0, 0))],
        out_shape=[jax.ShapeDtypeStruct(dh2.shape, dh2.dtype), jax.ShapeDtypeStruct((1, LANES), F32)],
        scratch_shapes=[pltpu.VMEM((1, LANES), F32)],
        input_output_aliases={4: 0},
        compiler_params=_cparams("arbitrary"),
    )(dn, drow, fl, bias, dh2)


FOX_SCALE = FOX_HEAD_DIM ** -0.5
LOG2E = 1.4426950408889634
LN2 = 0.6931471805599453
AUX = FOX_HEAD_DIM


def _split3(x):
    hi = x.astype(BF16).astype(F32)
    r = x - hi
    mid = r.astype(BF16).astype(F32)
    return hi, mid, (r - mid).astype(BF16).astype(F32)


def _lanes3(lane, base, parts, rest):
    return jnp.where(lane == base, parts[0], jnp.where(lane == base + 1, parts[1],
                                                       jnp.where(lane == base + 2, parts[2], rest)))


def _swap_halves(x):
    return pltpu.roll(x, FOX_HEAD_DIM, 1)


def _causal_steps(nq, keys_outer):
    if keys_outer:
        pairs = [(i, j) for j in range(nq) for i in range(j, nq)]
    else:
        pairs = [(i, j) for i in range(nq) for j in range(i + 1)]
    it, jt = zip(*pairs)
    return jnp.asarray(np.array(it, np.int32)), jnp.asarray(np.array(jt, np.int32))


def _fox_prep_fwd(h, kv, negcum, *, tb, name):
    S = h.shape[0]
    tb = min(tb, S)

    def body(q_ref, kv_ref, nc_ref, qa_ref, ka_ref, va_ref):
        g = pl.program_id(0)
        lane = lax.broadcasted_iota(jnp.int32, (tb, LANES), 1)
        first = lane < FOX_HEAD_DIM
        q = q_ref[...].astype(F32) * (FOX_SCALE * LOG2E)
        k = kv_ref[:, :LANES].astype(F32)
        v = kv_ref[:, LANES:].astype(F32)
        nc = nc_ref[...]
        ones_q = jnp.where((lane >= AUX) & (lane < AUX + 3), 1.0, 0.0)
        ones_k = jnp.where((lane >= AUX + 3) & (lane < AUX + 6), 1.0, 0.0)
        for hh in range(2):
            sl = slice(LANES * hh, LANES * (hh + 1))
            qh, kh, vh = (q, k, v) if hh == 0 else (_swap_halves(q), _swap_halves(k), _swap_halves(v))
            ncol = jnp.sum(jnp.where(lane == 2 * g + hh, nc, 0.0), axis=1, keepdims=True) * LOG2E
            qa_ref[:, sl] = jnp.where(first, qh, ones_q).astype(BF16)
            ka_ref[:, sl] = jnp.where(first, kh, _lanes3(lane, AUX, _split3(ncol), ones_k)).astype(BF16)
            va_ref[:, sl] = jnp.where(first, vh, 1.0).astype(BF16)

    pair = pl.BlockSpec((tb, 2 * LANES), lambda g, i: (i, g))
    shp = jax.ShapeDtypeStruct((S, 2 * D_MAIN), BF16)
    return pl.pallas_call(
        body, name=name, grid=(FOX_PAIRS, S // tb),
        in_specs=[pl.BlockSpec((tb, LANES), lambda g, i: (i, g)), pair, pl.BlockSpec((tb, LANES), lambda g, i: (i, 0))],
        out_specs=[pair, pair, pair], out_shape=[shp, shp, shp],
        compiler_params=_cparams("parallel", "parallel"),
    )(h, kv, negcum)


def _fox_fwd(qa, ka, va, *, tq, name):
    S = qa.shape[0]
    tq = min(tq, S)
    nq = S // tq
    rep = tq // LANES
    it, jt = _causal_steps(nq, keys_outer=False)

    def body(it_ref, jt_ref, qa_ref, ka_ref, va_ref, y_ref, lse_ref, m_ref, acc_ref):
        n = pl.program_id(1)
        i, j = it_ref[n], jt_ref[n]
        first = lax.broadcasted_iota(jnp.int32, (tq, LANES), 1) < FOX_HEAD_DIM

        @pl.when(j == 0)
        def _():
            m_ref[...] = jnp.full_like(m_ref, NEG)
            acc_ref[...] = jnp.zeros_like(acc_ref)

        def step(masked):
            for hh in range(2):
                sl = slice(LANES * hh, LANES * (hh + 1))
                s = lax.dot_general(qa_ref[:, sl], ka_ref[:, sl], NT, preferred_element_type=F32)
                if masked:
                    r = lax.broadcasted_iota(jnp.int32, (tq, tq), 0)
                    c = lax.broadcasted_iota(jnp.int32, (tq, tq), 1)
                    s = jnp.where(c <= r, s, NEG)
                m_prev = m_ref[hh]
                m_new = jnp.maximum(m_prev, jnp.max(s, axis=1, keepdims=True))
                p = jnp.exp2(s - jnp.tile(m_new, (1, rep))).astype(BF16)
                acc_ref[hh] = jnp.exp2(m_prev - m_new) * acc_ref[hh] + jnp.dot(p, va_ref[:, sl],
                                                                               preferred_element_type=F32)
                m_ref[hh] = m_new

        @pl.when(j < i)
        def _():
            step(False)

        @pl.when(j == i)
        def _():
            step(True)
            ys, lses = [], []
            for hh in range(2):
                a = acc_ref[hh]
                denom = _swap_halves(a)
                ys.append(a / denom)
                lses.append(m_ref[hh] + jnp.log(jnp.where(first, denom, a)) * LOG2E)
            y_ref[...] = jnp.where(first, ys[0], _swap_halves(ys[1]))
            lse_ref[...] = jnp.where(first, lses[0], lses[1])

    qblock = pl.BlockSpec((tq, 2 * LANES), lambda g, n, it, jt: (it[n], g))
    kblock = pl.BlockSpec((tq, 2 * LANES), lambda g, n, it, jt: (jt[n], g))
    out = pl.BlockSpec((tq, LANES), lambda g, n, it, jt: (it[n], g))
    return pl.pallas_call(
        body, name=name,
        grid_spec=pltpu.PrefetchScalarGridSpec(
            num_scalar_prefetch=2, grid=(FOX_PAIRS, it.shape[0]),
            in_specs=[qblock, kblock, kblock], out_specs=[out, out],
            scratch_shapes=[pltpu.VMEM((2, tq, LANES), F32), pltpu.VMEM((2, tq, LANES), F32)]),
        out_shape=[jax.ShapeDtypeStruct((S, D_MAIN), F32), jax.ShapeDtypeStruct((S, D_MAIN), F32)],
        compiler_params=_cparams("parallel", "arbitrary"),
    )(it, jt, qa, ka, va)


def _fox_prep_bwd(qa, lse2, dy, y, *, tb, name):
    S = qa.shape[0]
    tb = min(tb, S)

    def body(qa_ref, lse_ref, dy_ref, y_ref, qb_ref, dya_ref):
        lane = lax.broadcasted_iota(jnp.int32, (tb, LANES), 1)
        first = lane < FOX_HEAD_DIM
        lse = lse_ref[...]
        lse_sw = _swap_halves(lse)
        dy = dy_ref[...]
        prod = dy * y_ref[...]
        for hh in range(2):
            sl = slice(LANES * hh, LANES * (hh + 1))
            lse_h = jnp.where(first, lse, lse_sw) if hh == 0 else jnp.where(first, lse_sw, lse)
            qb_ref[:, sl] = _lanes3(lane, AUX + 3, _split3(-lse_h), qa_ref[:, sl].astype(F32)).astype(BF16)
            delta = jnp.sum(jnp.where(first == (hh == 0), prod, 0.0), axis=1, keepdims=True)
            dyh = dy if hh == 0 else _swap_halves(dy)
            dya_ref[:, sl] = jnp.where(first, dyh, _lanes3(lane, AUX, _split3(-delta), 0.0)).astype(BF16)

    pair = pl.BlockSpec((tb, 2 * LANES), lambda g, i: (i, g))
    one = pl.BlockSpec((tb, LANES), lambda g, i: (i, g))
    shp = jax.ShapeDtypeStruct((S, 2 * D_MAIN), BF16)
    return pl.pallas_call(
        body, name=name, grid=(FOX_PAIRS, S // tb),
        in_specs=[pair, one, one, one], out_specs=[pair, pair], out_shape=[shp, shp],
        compiler_params=_cparams("parallel", "parallel"),
    )(qa, lse2, dy, y)


def _fox_bwd(qb, ka, va, dya, dh, *, tq, name):
    S = qb.shape[0]
    tq = min(tq, S)
    nq = S // tq
    it, jt = _causal_steps(nq, keys_outer=True)
    nsteps = it.shape[0]

    def body(it_ref, jt_ref, qb_ref, ka_ref, va_ref, dya_ref, dh_in, dq_ref, dkv_ref, dn_ref, drow_ref,
             dq_acc, dk_acc, dv_acc):
        del dh_in
        n = pl.program_id(1)
        i, j = it_ref[n], jt_ref[n]
        first = lax.broadcasted_iota(jnp.int32, (tq, LANES), 1) < FOX_HEAD_DIM

        @pl.when(n == 0)
        def _():
            dq_acc[...] = jnp.zeros_like(dq_acc)

        @pl.when(i == j)
        def _():
            dk_acc[...] = jnp.zeros_like(dk_acc)
            dv_acc[...] = jnp.zeros_like(dv_acc)

        def step(masked):
            rows = pl.ds(pl.multiple_of(i * tq, tq), tq)
            for hh in range(2):
                sl = slice(LANES * hh, LANES * (hh + 1))
                qbh, kah, dyah = qb_ref[:, sl], ka_ref[:, sl], dya_ref[:, sl]
                eT = lax.dot_general(kah, qbh, NT, preferred_element_type=F32)
                if masked:
                    r = lax.broadcasted_iota(jnp.int32, (tq, tq), 0)
                    c = lax.broadcasted_iota(jnp.int32, (tq, tq), 1)
                    eT = jnp.where(r <= c, eT, NEG)
                pT = jnp.exp2(eT)
                dsT = pT * lax.dot_general(va_ref[:, sl], dyah, NT, preferred_element_type=F32)
                dsb = dsT.astype(BF16)
                dv_acc[hh] += jnp.dot(pT.astype(BF16), dyah, preferred_element_type=F32)
                dk_acc[hh] += jnp.dot(dsb, qbh, preferred_element_type=F32)
                dq_acc[hh, rows, :] += lax.dot_general(dsb, kah, TN, preferred_element_type=F32)

        @pl.when(i > j)
        def _():
            step(False)

        @pl.when(i == j)
        def _():
            step(True)

        @pl.when(i == nq - 1)
        def _():
            dkv_ref[:, :LANES] = (jnp.where(first, dk_acc[0], _swap_halves(dk_acc[1])) * LN2).astype(BF16)
            dkv_ref[:, LANES:] = jnp.where(first, dv_acc[0], _swap_halves(dv_acc[1])).astype(BF16)
            dn_ref[...] = jnp.where(first, _swap_halves(dk_acc[0]), dk_acc[1])

        @pl.when(n == nsteps - 1)
        def _():
            first_s = lax.broadcasted_iota(jnp.int32, (S, LANES), 1) < FOX_HEAD_DIM
            dq_ref[...] = (jnp.where(first_s, dq_acc[0], _swap_halves(dq_acc[1])) * FOX_SCALE).astype(BF16)
            drow_ref[...] = jnp.where(first_s, _swap_halves(dq_acc[0]), dq_acc[1])

    qblock = pl.BlockSpec((tq, 2 * LANES), lambda g, n, it, jt: (it[n], g))
    kblock = pl.BlockSpec((tq, 2 * LANES), lambda g, n, it, jt: (jt[n], g))
    whole = pl.BlockSpec((S, LANES), lambda g, n, it, jt: (0, g))
    return pl.pallas_call(
        body, name=name,
        grid_spec=pltpu.PrefetchScalarGridSpec(
            num_scalar_prefetch=2, grid=(FOX_PAIRS, nsteps),
            in_specs=[qblock, kblock, kblock, qblock, pl.BlockSpec(memory_space=pl.ANY)],
            out_specs=[whole, kblock, pl.BlockSpec((tq, LANES), lambda g, n, it, jt: (jt[n], g)), whole],
            scratch_shapes=[pltpu.VMEM((2, S, LANES), F32), pltpu.VMEM((2, tq, LANES), F32),
                            pltpu.VMEM((2, tq, LANES), F32)]),
        out_shape=[jax.ShapeDtypeStruct(dh.shape, dh.dtype), jax.ShapeDtypeStruct((S, 2 * D_MAIN + LANES), BF16),
                   jax.ShapeDtypeStruct((S, D_MAIN), F32), jax.ShapeDtypeStruct((S, D_MAIN), F32)],
        input_output_aliases={6: 0},
        compiler_params=_cparams("parallel", "arbitrary"),
    )(it, jt, qb, ka, va, dya, dh)


TB_ROWS = 256
TB_SEQ = 512
TB_PREP = 2048
TQ_FOX_FWD = 1024
TQ_FOX_BWD = 1024


def _local_step(x, mem, target, win, wmkv, wout, pw, pscale, wkvp, wf, ln_g, ln_b, bias, send_early=None):
    S = x.shape[0]
    ones = jnp.ones((1, D_MAIN), F32)
    g0, b0, g1, b1 = ln_g[0:1], ln_b[0:1], ln_g[1:2], ln_b[1:2]
    mm = lambda a, b, mode, dt, tm, tn, tk, name, **kw: _mm(a, b, mode=mode, out_dtype=dt, tm=tm, tn=tn, tk=tk,
                                                            name=name, **kw)

    h0 = mm(x, win[0], "nn", BF16, 256, D_IN, D_MODEL, "l0_in")
    pm, mixed = _pool_fwd(h0, pw, tb=TB_SEQ, name="l0_pool_fwd")
    mkv0 = mm(mem, wmkv[0], "nn", F32, 256, 1024, 1024, "l0_mkv")
    ymem0 = _memattn_fwd(h0, mkv0, tb=TB_SEQ, name="l0_mem_fwd")
    yg0 = _gate_fwd(mixed, pscale, h0, ymem0, tb=TB_ROWS, name="l0_gate_fwd")
    x1, xhat0, rstd0 = _out_ln(yg0, wout[0], x, g0, b0, tb=TB_SEQ, name="l0_out_ln")

    kv = mm(x1, wkvp, "nn", BF16, 512, 2 * D_MAIN, D_MODEL, "kv_proj")
    fl = mm(x1, wf, "nn", F32, 512, LANES, D_MODEL, "f_proj")
    negcum = _forget_fwd(fl, bias, tb=TB_SEQ, name="forget_fwd")

    h1 = mm(x1, win[1], "nn", BF16, 256, D_IN, D_MODEL, "l1_in")
    qa, ka, va = _fox_prep_fwd(h1, kv, negcum, tb=TB_PREP, name="fox_prep_fwd")
    y1, lse2 = _fox_fwd(qa, ka, va, tq=TQ_FOX_FWD, name="fox_fwd")
    mkv1 = mm(mem, wmkv[1], "nn", F32, 256, 1024, 1024, "l1_mkv")
    ymem1 = _memattn_fwd(h1, mkv1, tb=TB_SEQ, name="l1_mem_fwd")
    yg1 = _gate_fwd(y1, ones, h1, ymem1, tb=TB_ROWS, name="l1_gate_fwd")
    dz1, dg1, db1, sq = _out_ln_loss(yg1, wout[1], x1, g1, b1, target, tb=TB_SEQ, name="l1_out_ln_loss")

    dwout1 = mm(yg1, dz1, "tn", BF16, D_MIX, D_MODEL, 512, "l1_dwout")
    dyg1 = mm(dz1, wout[1], "nt", BF16, 512, D_MIX, D_MODEL, "l1_dyg")
    dy1, dymem1, dh1 = _gate_bwd(dyg1, y1, ones, h1, ymem1, tb=TB_ROWS, name="l1_gate_bwd")
    qb, dya = _fox_prep_bwd(qa, lse2, dy1, y1, tb=TB_PREP, name="fox_prep_bwd")
    dh1, dh2, dnp, drowp = _fox_bwd(qb, ka, va, dya, dh1, tq=TQ_FOX_BWD, name="fox_bwd")
    dh2, dbias = _forget_bwd(dnp, drowp, fl, bias, dh2, tb=TB_SEQ, name="forget_bwd")
    dh1, dmkv1 = _memattn_bwd(h1, mkv1, dymem1, dh1, tb=TB_SEQ, name="l1_mem_bwd")
    dwmkv1 = mm(mem, dmkv1, "tn", BF16, D_MODEL, 1024, N_MEM, "l1_dwmkv")
    dwin1 = mm(x1, dh1, "tn", BF16, D_MODEL, D_IN // 2, 512, "l1_dwin")
    dwkvf = mm(x1, dh2, "tn", F32, D_MODEL, 2 * D_MAIN + LANES, 512, "dwkv")
    dkv = dwkvf[:, :2 * D_MAIN].reshape(D_MODEL, FOX_PAIRS, 2, LANES)
    dwkv = jnp.concatenate([dkv[:, :, 0, :].reshape(D_MODEL, D_MAIN), dkv[:, :, 1, :].reshape(D_MODEL, D_MAIN),
                            dwkvf[:, 2 * D_MAIN:2 * D_MAIN + FOX_HEADS]], axis=1)
    dwkv = dwkv.reshape(D_MODEL, N_DEV, -1).transpose(1, 0, 2).astype(BF16)
    anchor = send_early(dict(w_out=dwout1, w_mem_kv=dwmkv1, w_in=dwin1, w_kv_shared=dwkv)) if send_early else 0.0
    dx1 = mm(dh1, win[1], "nt", F32, 256, D_MODEL, D_IN, "l1_dx", add=dz1, add_scale=ALPHA)
    wkvf = jnp.concatenate([wkvp, wf], axis=1)
    dx1 = mm(dh2, wkvf, "nt", F32, 256, D_MODEL, 2 * D_MAIN + LANES, "kv_dx", add=dx1)

    dz0, dg0, db0 = _ln_bwd(dx1, xhat0, rstd0, g0 + anchor, tb=TB_ROWS, name="l0_ln_bwd")
    dwout0 = mm(yg0, dz0, "tn", BF16, D_MIX, D_MODEL, 512, "l0_dwout")
    dyg0 = mm(dz0, wout[0], "nt", BF16, 512, D_MIX, D_MODEL, "l0_dyg")
    dy0, dymem0, dh0 = _gate_bwd(dyg0, mixed, pscale, h0, ymem0, tb=TB_ROWS, name="l0_gate_bwd")
    dh0, dpw, dpscale = _pool_bwd(dy0, pm, mixed, pw, pscale, dh0, tb=TB_SEQ, name="l0_pool_bwd")
    dh0, dmkv0 = _memattn_bwd(h0, mkv0, dymem0, dh0, tb=TB_SEQ, name="l0_mem_bwd")
    dwmkv0 = mm(mem, dmkv0, "tn", BF16, D_MODEL, 1024, N_MEM, "l0_dwmkv")
    dwin0 = mm(x, dh0, "tn", BF16, D_MODEL, D_IN // 2, 512, "l0_dwin")
    gx = mm(dh0, win[0], "nt", F32, 256, D_MODEL, D_IN, "l0_dx", add=dz0, add_scale=ALPHA)

    grads = dict(w_in=(dwin0, dwin1), w_mem_kv=(dwmkv0, dwmkv1), w_out=(dwout0, dwout1), pool_w=(dpw,),
                 w_kv_shared=dwkv, pool_scale=dpscale, ln_g=jnp.concatenate([dg0, dg1]),
                 ln_b=jnp.concatenate([db0, db1]), b_forget=dbias[0, :FOX_HEADS])
    return sq, gx, grads


MESH_ID = pl.DeviceIdType.MESH
HBM = pl.BlockSpec(memory_space=pl.ANY)
SLICED = {"w_in": (2, D_IN // N_DEV), "w_mem_kv": (1, D_MODEL // N_DEV), "w_out": (1, D_MIX // N_DEV),
          "pool_w": (1, POOL_GROUP // N_DEV)}


def _place():
    return lax.axis_index("x"), lax.axis_index("y"), lax.axis_index("c")


def _slot(p):
    return 4 * p[0] + 2 * p[1] + p[2]


def _cut(ref, axis, width, s):
    idx = [slice(None)] * len(ref.shape)
    idx[axis] = pl.ds(s * width, width)
    return ref.at[tuple(idx)]


def _all_gather(shards, cuts, *, name):
    nt = len(shards)

    def full_shape(a, cut):
        if cut is None:
            return (N_DEV,) + a.shape
        return a.shape[:cut[0]] + (a.shape[cut[0]] * N_DEV,) + a.shape[cut[0] + 1:]

    def body(*refs):
        ins, outs = refs[:nt], refs[nt:2 * nt]
        send_sems, recv_sems, local_sems = refs[2 * nt:]
        x, y, c = _place()
        me, sibling = (x, y, c), (x, y, 1 - c)
        chips = [(1 - x, y), (x, 1 - y), (1 - x, 1 - y)]

        def place(t, s):
            return outs[t].at[s] if cuts[t] is None else _cut(outs[t], cuts[t][0], cuts[t][1], s)

        def copies(k, block, to, from_input=False):
            s = _slot(block)
            return [pltpu.make_async_remote_copy(
                src_ref=ins[t] if from_input else place(t, s), dst_ref=place(t, s),
                send_sem=send_sems.at[nt * k + t], recv_sem=recv_sems.at[nt * k + t],
                device_id=to, device_id_type=MESH_ID) for t in range(nt)]

        mine = [pltpu.make_async_copy(ins[t], place(t, _slot(me)), local_sems.at[t]) for t in range(nt)]
        for cp in mine:
            cp.start()
        first = [copies(0, me, sibling, True)] + [copies(1 + j, me, (*chip, c), True) for j, chip in enumerate(chips)]
        for group in first:
            for cp in group:
                cp.start()
        passed = [copies(4 + j, (*chip, c), sibling) for j, chip in enumerate(chips)]
        for j, chip in enumerate(chips):
            for cp in copies(1 + j, (*chip, c), me):
                cp.wait_recv()
            for cp in passed[j]:
                cp.start()
        for cp in copies(0, sibling, me):
            cp.wait_recv()
        for j, chip in enumerate(chips):
            for cp in copies(4 + j, (*chip, 1 - c), me):
                cp.wait_recv()
        for group in first + passed:
            for cp in group:
                cp.wait_send()
        for cp in mine:
            cp.wait()

    return pl.pallas_call(
        body, name=name, in_specs=[HBM] * nt, out_specs=[HBM] * nt,
        out_shape=[jax.ShapeDtypeStruct(full_shape(a, cut), a.dtype) for a, cut in zip(shards, cuts)],
        scratch_shapes=[pltpu.SemaphoreType.DMA((7 * nt,)), pltpu.SemaphoreType.DMA((7 * nt,)),
                        pltpu.SemaphoreType.DMA((nt,))],
    )(*shards)


def _exchange_copies(items, ins, outs, send_sems, recv_sems, local_sems):
    nt = len(items)
    x, y, c = _place()
    me = _slot((x, y, c))
    flip = lambda v, bit: 1 - v if bit else v

    def src(t, s):
        cut = items[t][0]
        return ins[t].at[s] if cut is None else _cut(ins[t], cut[0], cut[1], s)

    def dst(t, s):
        d = outs[items[t][1]].at[s]
        return d if items[t][2] is None else d.at[items[t][2]]

    sends, arrivals = [], []
    for k in range(1, N_DEV):
        peer = (flip(x, k & 4), flip(y, k & 2), flip(c, k & 1))
        ps = _slot(peer)
        for t in range(nt):
            sems = dict(send_sem=send_sems.at[nt * (k - 1) + t], recv_sem=recv_sems.at[nt * (k - 1) + t],
                        device_id=peer, device_id_type=MESH_ID)
            sends.append(pltpu.make_async_remote_copy(src_ref=src(t, ps), dst_ref=dst(t, me), **sems))
            arrivals.append(pltpu.make_async_remote_copy(src_ref=src(t, ps), dst_ref=dst(t, ps), **sems))
    mine = [pltpu.make_async_copy(src(t, me), dst(t, me), local_sems.at[t]) for t in range(nt)]
    return sends, arrivals, mine


def _landing(spec):
    return spec if isinstance(spec, jax.ShapeDtypeStruct) else jax.ShapeDtypeStruct(spec.shape, spec.dtype)


def _exchange(srcs, items, landings, *, name):
    nt, nl = len(srcs), len(landings)
    given = [i for i, l in enumerate(landings) if not isinstance(l, jax.ShapeDtypeStruct)]

    def body(*refs):
        ins, outs = refs[:nt], refs[nt + len(given):nt + len(given) + nl]
        sends, arrivals, mine = _exchange_copies(items, ins, outs, *refs[nt + len(given) + nl:])
        for cp in sends + mine:
            cp.start()
        for sent, landed in zip(sends, arrivals):
            landed.wait_recv()
            sent.wait_send()
        for cp in mine:
            cp.wait()

    return pl.pallas_call(
        body, name=name, in_specs=[HBM] * (nt + len(given)), out_specs=[HBM] * nl,
        out_shape=[_landing(l) for l in landings],
        input_output_aliases={nt + n: i for n, i in enumerate(given)},
        scratch_shapes=[pltpu.SemaphoreType.DMA((7 * nt,)), pltpu.SemaphoreType.DMA((7 * nt,)),
                        pltpu.SemaphoreType.DMA((nt,))],
    )(*srcs, *[landings[i] for i in given])


SEMS = pl.BlockSpec(memory_space=pltpu.SEMAPHORE)
SIDE_EFFECT = pltpu.SideEffectType.DATAFLOW_SIDE_EFFECTING


def _exchange_start(srcs, items, landings, *, name):
    nt, nl = len(srcs), len(landings)

    def body(*refs):
        ins, lands = refs[:nt], refs[nt:nt + nl]
        send_sems, recv_sems, local_sems = refs[nt + nl:nt + nl + 3]
        token = refs[-1]
        sends, _, mine = _exchange_copies(items, ins, lands, send_sems, recv_sems, local_sems)
        for cp in sends + mine:
            cp.start()
        token[...] = jnp.zeros_like(token)

    hbm = lambda a: pltpu.HBM(a.shape, a.dtype)
    fresh = [pltpu.with_memory_space_constraint(lax.empty(l.shape, l.dtype), pltpu.HBM) for l in landings]
    res = pl.pallas_call(
        body, name=name, in_specs=[HBM] * (nt + nl),
        out_specs=[SEMS, SEMS, SEMS] + [HBM] * (nt + nl) + [pl.BlockSpec(memory_space=pltpu.VMEM)],
        out_shape=[pltpu.SemaphoreType.DMA((7 * nt,)), pltpu.SemaphoreType.DMA((7 * nt,)), pltpu.SemaphoreType.DMA((nt,))]
        + [hbm(a) for a in srcs] + [hbm(l) for l in landings] + [jax.ShapeDtypeStruct((8, LANES), F32)],
        input_output_aliases={i: 3 + i for i in range(nt + nl)},
        compiler_params=pltpu.CompilerParams(has_side_effects=SIDE_EFFECT),
    )(*[pltpu.with_memory_space_constraint(a, pltpu.HBM) for a in srcs], *fresh)
    return res[:-1], res[-1][0:1, 0:1]


def _exchange_wait(state, items, nt, after, *, name):
    sems, bufs = state[:3], state[3:]
    nl = len(bufs) - nt

    def body(*refs):
        ins, lands = refs[:nt], refs[nt:nt + nl]
        send_sems, recv_sems, local_sems = refs[nt + nl:nt + nl + 3]
        sends, arrivals, mine = _exchange_copies(items, ins, lands, send_sems, recv_sems, local_sems)
        for sent, landed in zip(sends, arrivals):
            landed.wait_recv()
            sent.wait_send()
        for cp in mine:
            cp.wait()

    hbm = lambda a: pltpu.HBM(a.shape, a.dtype)
    res = pl.pallas_call(
        body, name=name, in_specs=[HBM] * (nt + nl) + [SEMS, SEMS, SEMS, HBM], out_specs=[HBM] * (nt + nl),
        out_shape=[hbm(a) for a in bufs],
        input_output_aliases={i: i for i in range(nt + nl)},
        compiler_params=pltpu.CompilerParams(has_side_effects=SIDE_EFFECT),
    )(*bufs, *sems, after)
    return res[nt:]


def _adamw(recv, w, m, v, *, split, name):
    shape = w.shape
    axis, parts = split
    block = shape[:axis] + (shape[axis] // parts,) + shape[axis + 1:]
    nd = len(shape)

    def body(r_ref, w_ref, m_ref, v_ref, g_ref, d_ref, nm_ref, nv_ref):
        g = r_ref[0].astype(F32)
        for j in range(1, N_DEV):
            g = g + r_ref[j].astype(F32)
        nm = ADAM_B1 * m_ref[...] + (1.0 - ADAM_B1) * g
        nv = ADAM_B2 * v_ref[...] + (1.0 - ADAM_B2) * (g * g)
        m_hat = nm / (1.0 - ADAM_B1 ** ADAM_STEP)
        v_hat = nv / (1.0 - ADAM_B2 ** ADAM_STEP)
        g_ref[...] = g
        nm_ref[...] = nm
        nv_ref[...] = nv
        d_ref[...] = -ADAM_LR * (m_hat / (jnp.sqrt(v_hat) + ADAM_EPS) + ADAM_WD * w_ref[...])

    at = lambda i: tuple(i if a == axis else 0 for a in range(nd))
    one = pl.BlockSpec(block, at)
    shp = jax.ShapeDtypeStruct(shape, F32)
    return pl.pallas_call(
        body, name=name, grid=(parts,),
        in_specs=[pl.BlockSpec((N_DEV,) + block, lambda i: (0,) + at(i)), one, one, one],
        out_specs=[one, one, one, one], out_shape=[shp, shp, shp, shp],
        compiler_params=_cparams("parallel"),
    )(recv, w, m, v)


BIG = ("w_in", "w_mem_kv", "w_out", "pool_w", "w_kv_shared")
SMALL = ("pool_scale", "ln_g", "ln_b", "b_forget")
SMALL_ROWS = 40
ADAM_SPLIT = {"w_in": (1, 4), "w_mem_kv": (0, 2), "w_out": (0, 2), "pool_w": (0, 1), "w_kv_shared": (0, 4)}
PER_LAYER_CUT = {"w_out": (0, D_MIX // N_DEV), "w_mem_kv": (0, D_MODEL // N_DEV), "w_in": (1, D_IN // N_DEV),
                 "pool_w": (1, POOL_GROUP // N_DEV)}
EARLY = ("w_out", "w_mem_kv", "w_in", "w_kv_shared")
EARLY_ITEMS = [(PER_LAYER_CUT[n], i, 1) for i, n in enumerate(EARLY[:3])] + [(None, 3, None)]


def _flat(parts, rows):
    v = jnp.concatenate([p.reshape(-1) for p in parts])
    return jnp.pad(v, (0, rows * LANES - v.shape[0])).reshape(rows, LANES)


def _unflat(flat, shapes):
    v, out, off = flat.reshape(-1), [], 0
    for s in shapes:
        n = math.prod(s)
        out.append(v[off:off + n].reshape(s))
        off += n
    return out


def kernel(x, mem, w_in, w_mem_kv, w_out, ln_g, ln_b, pool_w, pool_scale, w_kv_shared, b_forget, loss_target, m_w_in, m_w_mem_kv, m_w_out, m_ln_g, m_ln_b, m_pool_w, m_pool_scale, m_w_kv_shared, m_b_forget, v_w_in, v_w_mem_kv, v_w_out, v_ln_g, v_ln_b, v_pool_w, v_pool_scale, v_w_kv_shared, v_b_forget):
    w = dict(w_in=w_in, w_mem_kv=w_mem_kv, w_out=w_out, ln_g=ln_g, ln_b=ln_b, pool_w=pool_w[0],
             pool_scale=pool_scale, w_kv_shared=w_kv_shared, b_forget=b_forget)
    m = dict(w_in=m_w_in, w_mem_kv=m_w_mem_kv, w_out=m_w_out, ln_g=m_ln_g, ln_b=m_ln_b, pool_w=m_pool_w[0],
             pool_scale=m_pool_scale, w_kv_shared=m_w_kv_shared, b_forget=m_b_forget)
    v = dict(w_in=v_w_in, w_mem_kv=v_w_mem_kv, w_out=v_w_out, ln_g=v_ln_g, ln_b=v_ln_b, pool_w=v_pool_w[0],
             pool_scale=v_pool_scale, w_kv_shared=v_w_kv_shared, b_forget=v_b_forget)

    names = BIG + ("pool_scale",)
    full = dict(zip(names, _all_gather(
        [w[n].astype(BF16) for n in BIG] + [jnp.pad(pool_scale, ((0, 7), (0, 0)))],
        [SLICED.get(n) for n in names], name="gather_weights")))
    pscale = full["pool_scale"][:, 0, :].reshape(1, D_MAIN)
    wkv = full["w_kv_shared"].transpose(1, 0, 2).reshape(D_MODEL, -1)
    wkvp = jnp.stack([wkv[:, :D_MAIN].reshape(D_MODEL, FOX_PAIRS, LANES),
                      wkv[:, D_MAIN:2 * D_MAIN].reshape(D_MODEL, FOX_PAIRS, LANES)], axis=2).reshape(D_MODEL, 2 * D_MAIN)
    wf = jnp.pad(wkv[:, 2 * D_MAIN:], ((0, 0), (0, LANES - FOX_HEADS)))
    bias = jnp.pad(b_forget, (0, LANES - FOX_HEADS)).reshape(1, LANES)

    early = {}

    def send_early(g):
        srcs = [g[n] for n in EARLY]
        lands = [jax.ShapeDtypeStruct((N_DEV, 2) + w[n].shape[1:], g[n].dtype) for n in EARLY[:3]]
        lands.append(jax.ShapeDtypeStruct(g["w_kv_shared"].shape, g["w_kv_shared"].dtype))
        early["state"], anchor = _exchange_start(srcs, EARLY_ITEMS, lands, name="exchange_early_start")
        return anchor

    sq, gx, grads = _local_step(x[0], mem[0], loss_target[0], full["w_in"], full["w_mem_kv"], full["w_out"],
                                full["pool_w"], pscale, wkvp, wf, ln_g, ln_b, bias, send_early)
    loss = lax.psum((0.5 / D_MODEL) * jnp.sum(sq), ("x", "y", "c"))

    small = jnp.concatenate([grads["pool_scale"].reshape(N_DEV, -1)]
                            + [jnp.broadcast_to(grads[n].reshape(1, -1), (N_DEV, grads[n].size)) for n in SMALL[1:]], axis=1)
    small = jnp.pad(small, ((0, 0), (0, SMALL_ROWS * LANES - small.shape[1]))).reshape(N_DEV, SMALL_ROWS, LANES)
    r_out, r_mkv, r_in, r_kv = _exchange_wait(early["state"], EARLY_ITEMS, len(EARLY), gx, name="exchange_early_wait")
    late = [grads[n][0] for n in EARLY[:3]] + [grads["pool_w"][0], small]
    late_items = [(cut, i, 0) for i, (cut, _, _) in enumerate(EARLY_ITEMS[:3])] + [(PER_LAYER_CUT["pool_w"], 3, None),
                                                                                  (None, 4, None)]
    r_out, r_mkv, r_in, r_pw, r_small = _exchange(
        late, late_items,
        [r_out, r_mkv, r_in, jax.ShapeDtypeStruct((N_DEV,) + w["pool_w"].shape, late[3].dtype),
         jax.ShapeDtypeStruct(small.shape, small.dtype)], name="exchange_grads")
    recv = dict(w_out=r_out, w_mem_kv=r_mkv, w_in=r_in, pool_w=r_pw, w_kv_shared=r_kv, small=r_small)

    outs = {}
    for n in BIG:
        res = _adamw(recv[n], w[n], m[n], v[n], split=ADAM_SPLIT[n], name="adamw_" + n)
        for kind, a in zip(("grad", "delta", "new_m", "new_v"), res):
            outs[kind, n] = a[None] if n == "pool_w" else a
    small_shapes = [w[n].shape for n in SMALL]
    res = _adamw(recv["small"], _flat([w[n] for n in SMALL], SMALL_ROWS), _flat([m[n] for n in SMALL], SMALL_ROWS),
                 _flat([v[n] for n in SMALL], SMALL_ROWS), split=(0, 1), name="adamw_small")
    for kind, flat in zip(("grad", "delta", "new_m", "new_v"), res):
        for n, a in zip(SMALL, _unflat(flat, small_shapes)):
            outs[kind, n] = a
    order = ("w_in", "w_mem_kv", "w_out", "ln_g", "ln_b", "pool_w", "pool_scale", "w_kv_shared", "b_forget")
    return (loss, gx[None], *[outs[kind, n] for kind in ("grad", "delta", "new_m", "new_v") for n in order])
```

```python
import math

import numpy as np
import jax
import jax.numpy as jnp
from jax import lax
from jax.experimental import pallas as pl
from jax.experimental.pallas import tpu as pltpu

F32 = jnp.float32
BF16 = jnp.bfloat16

D_MODEL = 1024
D_MAIN = 1024
D_MEM = 512
D_MIX = D_MAIN + D_MEM
D_IN = 2 * D_MIX
N_MEM = 256
MEM_HEADS = 4
MEM_HEAD_DIM = 128
FOX_HEADS = 16
FOX_HEAD_DIM = 64
FOX_PAIRS = FOX_HEADS // 2
POOL_WINDOWS = (2, 4, 8, 16)
POOL_GROUP = 256
POOL_HALO = 16
ALPHA = 4.0 ** 0.25
LN_EPS = 1e-5
NEG = -1e30
LANES = 128
N_DEV = 8

ADAM_LR = 0.001
ADAM_B1 = 0.9
ADAM_B2 = 0.999
ADAM_EPS = 1e-08
ADAM_WD = 0.01
ADAM_STEP = 10

VMEM_LIMIT = 56 * 1024 * 1024

NN = (((1,), (0,)), ((), ()))
NT = (((1,), (1,)), ((), ()))
TN = (((0,), (0,)), ((), ()))


def _cparams(*sem):
    return pltpu.CompilerParams(dimension_semantics=sem, vmem_limit_bytes=VMEM_LIMIT)


def _sigmoid(z):
    return 1.0 / (1.0 + jnp.exp(-z))


def _mm(a, b, *, mode, out_dtype, tm, tn, tk, name, add=None, add_scale=1.0):
    if mode == "nn":
        (M, K), (K2, N) = a.shape, b.shape
    elif mode == "nt":
        (M, K), (N, K2) = a.shape, b.shape
    else:
        (K, M), (K2, N) = a.shape, b.shape
    assert K == K2, (a.shape, b.shape, mode)
    tm, tn, tk = min(tm, M), min(tn, N), min(tk, K)
    assert M % tm == 0 and N % tn == 0 and K % tk == 0, (M, N, K, tm, tn, tk)
    gm, gn, gk = M // tm, N // tn, K // tk
    dims = {"nn": NN, "nt": NT, "tn": TN}[mode]
    if mode == "tn":
        a_spec = pl.BlockSpec((tk, tm), lambda i, j, k: (k, i))
    else:
        a_spec = pl.BlockSpec((tm, tk), lambda i, j, k: (i, k))
    if mode == "nt":
        b_spec = pl.BlockSpec((tn, tk), lambda i, j, k: (j, k))
    else:
        b_spec = pl.BlockSpec((tk, tn), lambda i, j, k: (k, j))
    o_spec = pl.BlockSpec((tm, tn), lambda i, j, k: (i, j))
    has_add = add is not None
    acc_in_out = out_dtype == F32

    def body(*refs):
        a_ref, b_ref = refs[0], refs[1]
        add_ref = refs[2] if has_add else None
        o_ref = refs[3] if has_add else refs[2]
        prod = lax.dot_general(a_ref[...].astype(BF16), b_ref[...].astype(BF16), dims,
                               preferred_element_type=F32)

        def finish(r):
            if has_add:
                r = r + add_scale * add_ref[...]
            o_ref[...] = r.astype(out_dtype)

        if gk == 1:
            finish(prod)
        else:
            acc_ref = o_ref if acc_in_out else refs[-1]
            k = pl.program_id(2)

            @pl.when(k == 0)
            def _():
                acc_ref[...] = prod

            @pl.when(k > 0)
            def _():
                acc_ref[...] += prod

            if has_add or not acc_in_out:
                @pl.when(k == gk - 1)
                def _():
                    finish(acc_ref[...])

    in_specs = [a_spec, b_spec] + ([o_spec] if has_add else [])
    args = (a, b) + ((add,) if has_add else ())
    return pl.pallas_call(
        body, name=name, grid=(gm, gn, gk), in_specs=in_specs, out_specs=o_spec,
        out_shape=jax.ShapeDtypeStruct((M, N), out_dtype),
        scratch_shapes=[pltpu.VMEM((tm, tn), F32)] if gk > 1 and not acc_in_out else [],
        compiler_params=_cparams("parallel", "parallel", "arbitrary"),
    )(*args)


def _ln_stats(z):
    mu = jnp.mean(z, axis=1, keepdims=True)
    zc = z - mu
    var = jnp.mean(zc * zc, axis=1, keepdims=True)
    rstd = lax.rsqrt(var + LN_EPS)
    return zc * rstd, rstd


def _ln_bwd_math(dy, xhat, rstd, g):
    dxh = dy * g
    m1 = jnp.mean(dxh, axis=1, keepdims=True)
    m2 = jnp.mean(dxh * xhat, axis=1, keepdims=True)
    return rstd * (dxh - m1 - xhat * m2)


def _out_ln(yg, wout, x, g, b, *, tb, name):
    S = x.shape[0]
    tb = min(tb, S)

    def body(yg_ref, w_ref, x_ref, g_ref, b_ref, y_ref, xhat_ref, rstd_ref):
        o = jnp.dot(yg_ref[...], w_ref[...], preferred_element_type=F32)
        xhat, rstd = _ln_stats(ALPHA * x_ref[...] + o)
        xhat_ref[...] = xhat
        rstd_ref[...] = rstd
        y_ref[...] = xhat * g_ref[...] + b_ref[...]

    row = pl.BlockSpec((tb, D_MODEL), lambda i: (i, 0))
    vec = pl.BlockSpec((1, D_MODEL), lambda i: (0, 0))
    return pl.pallas_call(
        body, name=name, grid=(S // tb,),
        in_specs=[pl.BlockSpec((tb, D_MIX), lambda i: (i, 0)), pl.BlockSpec((D_MIX, D_MODEL), lambda i: (0, 0)),
                  row, vec, vec],
        out_specs=[row, row, pl.BlockSpec((tb, 1), lambda i: (i, 0))],
        out_shape=[jax.ShapeDtypeStruct((S, D_MODEL), F32), jax.ShapeDtypeStruct((S, D_MODEL), F32),
                   jax.ShapeDtypeStruct((S, 1), F32)],
        compiler_params=_cparams("parallel"),
    )(yg, wout, x, g, b)


def _out_ln_loss(yg, wout, x, g, b, target, *, tb, name):
    S = x.shape[0]
    tb = min(tb, S)

    def body(yg_ref, w_ref, x_ref, g_ref, b_ref, t_ref, dz_ref, dg_ref, db_ref, sq_ref):
        @pl.when(pl.program_id(0) == 0)
        def _():
            dg_ref[...] = jnp.zeros_like(dg_ref)
            db_ref[...] = jnp.zeros_like(db_ref)
            sq_ref[...] = jnp.zeros_like(sq_ref)

        o = jnp.dot(yg_ref[...], w_ref[...], preferred_element_type=F32)
        xhat, rstd = _ln_stats(ALPHA * x_ref[...] + o)
        err = xhat * g_ref[...] + b_ref[...] - t_ref[...]
        sq_ref[...] += jnp.sum(err * err, axis=0, keepdims=True)
        dy = err * (1.0 / D_MODEL)
        dz_ref[...] = _ln_bwd_math(dy, xhat, rstd, g_ref[...])
        dg_ref[...] += jnp.sum(dy * xhat, axis=0, keepdims=True)
        db_ref[...] += jnp.sum(dy, axis=0, keepdims=True)

    row = pl.BlockSpec((tb, D_MODEL), lambda i: (i, 0))
    vec = pl.BlockSpec((1, D_MODEL), lambda i: (0, 0))
    vshape = jax.ShapeDtypeStruct((1, D_MODEL), F32)
    return pl.pallas_call(
        body, name=name, grid=(S // tb,),
        in_specs=[pl.BlockSpec((tb, D_MIX), lambda i: (i, 0)), pl.BlockSpec((D_MIX, D_MODEL), lambda i: (0, 0)),
                  row, vec, vec, row],
        out_specs=[row, vec, vec, vec],
        out_shape=[jax.ShapeDtypeStruct((S, D_MODEL), F32), vshape, vshape, vshape],
        compiler_params=_cparams("arbitrary"),
    )(yg, wout, x, g, b, target)


def _ln_bwd(dy, xhat, rstd, g, *, tb, name):
    S = dy.shape[0]
    tb = min(tb, S)

    def body(dy_ref, xhat_ref, rstd_ref, g_ref, dz_ref, dg_ref, db_ref):
        @pl.when(pl.program_id(0) == 0)
        def _():
            dg_ref[...] = jnp.zeros_like(dg_ref)
            db_ref[...] = jnp.zeros_like(db_ref)

        dy_, xhat_ = dy_ref[...], xhat_ref[...]
        dz_ref[...] = _ln_bwd_math(dy_, xhat_, rstd_ref[...], g_ref[...])
        dg_ref[...] += jnp.sum(dy_ * xhat_, axis=0, keepdims=True)
        db_ref[...] += jnp.sum(dy_, axis=0, keepdims=True)

    row = pl.BlockSpec((tb, D_MODEL), lambda i: (i, 0))
    vec = pl.BlockSpec((1, D_MODEL), lambda i: (0, 0))
    return pl.pallas_call(
        body, name=name, grid=(S // tb,),
        in_specs=[row, row, pl.BlockSpec((tb, 1), lambda i: (i, 0)), vec],
        out_specs=[row, vec, vec],
        out_shape=[jax.ShapeDtypeStruct((S, D_MODEL), F32), jax.ShapeDtypeStruct((1, D_MODEL), F32),
                   jax.ShapeDtypeStruct((1, D_MODEL), F32)],
        compiler_params=_cparams("arbitrary"),
    )(dy, xhat, rstd, g)


def _gate_fwd(ysrc, scale, h, ymem, *, tb, name):
    S = ysrc.shape[0]
    tb = min(tb, S)

    def body(ys_ref, sc_ref, ga_ref, gb_ref, gc_ref, ym_ref, yg_ref):
        ymain = ys_ref[...] * sc_ref[...]
        for k, g_ref in enumerate((ga_ref, gb_ref)):
            gv = g_ref[...].astype(F32)
            yg_ref[:, 512 * k:512 * (k + 1)] = (ymain[:, 512 * k:512 * (k + 1)] * gv * _sigmoid(gv)).astype(BF16)
        gv = gc_ref[...].astype(F32)
        yg_ref[:, 1024:1536] = (ym_ref[...] * gv * _sigmoid(gv)).astype(BF16)

    slab = lambda c: pl.BlockSpec((tb, 512), lambda i, c=c: (i, c))
    return pl.pallas_call(
        body, name=name, grid=(S // tb,),
        in_specs=[pl.BlockSpec((tb, D_MAIN), lambda i: (i, 0)), pl.BlockSpec((1, D_MAIN), lambda i: (0, 0)),
                  slab(3), slab(4), slab(5), pl.BlockSpec((tb, D_MEM), lambda i: (i, 0))],
        out_specs=pl.BlockSpec((tb, D_MIX), lambda i: (i, 0)),
        out_shape=jax.ShapeDtypeStruct((S, D_MIX), BF16),
        compiler_params=_cparams("parallel"),
    )(ysrc, scale, h, h, h, ymem)


def _gate_bwd(dyg, ysrc, scale, h, ymem, *, tb, name):
    S = ysrc.shape[0]
    tb = min(tb, S)

    def dsilu(gv):
        sg = _sigmoid(gv)
        return sg, sg * (1.0 + gv * (1.0 - sg))

    def body(da_ref, db_ref, dc_ref, ys_ref, sc_ref, ga_ref, gb_ref, gc_ref, ym_ref, dym_ref, dymem_ref, dh_ref):
        ymain = ys_ref[...] * sc_ref[...]
        for k, (d_ref, g_ref) in enumerate(((da_ref, ga_ref), (db_ref, gb_ref))):
            gv, d = g_ref[...].astype(F32), d_ref[...].astype(F32)
            sg, ds = dsilu(gv)
            dym_ref[:, 512 * k:512 * (k + 1)] = d * gv * sg
            dh_ref[:, 512 * k:512 * (k + 1)] = (d * ymain[:, 512 * k:512 * (k + 1)] * ds).astype(BF16)
        gv, d = gc_ref[...].astype(F32), dc_ref[...].astype(F32)
        sg, ds = dsilu(gv)
        dymem_ref[...] = d * gv * sg
        dh_ref[:, 1024:1536] = (d * ym_ref[...] * ds).astype(BF16)

    slab = lambda c: pl.BlockSpec((tb, 512), lambda i, c=c: (i, c))
    return pl.pallas_call(
        body, name=name, grid=(S // tb,),
        in_specs=[slab(0), slab(1), slab(2),
                  pl.BlockSpec((tb, D_MAIN), lambda i: (i, 0)), pl.BlockSpec((1, D_MAIN), lambda i: (0, 0)),
                  slab(3), slab(4), slab(5), pl.BlockSpec((tb, D_MEM), lambda i: (i, 0))],
        out_specs=[pl.BlockSpec((tb, D_MAIN), lambda i: (i, 0)), pl.BlockSpec((tb, D_MEM), lambda i: (i, 0)),
                   pl.BlockSpec((tb, D_MIX), lambda i: (i, 1))],
        out_shape=[jax.ShapeDtypeStruct((S, D_MAIN), F32), jax.ShapeDtypeStruct((S, D_MEM), F32),
                   jax.ShapeDtypeStruct((S, D_IN), BF16)],
        compiler_params=_cparams("parallel"),
    )(dyg, dyg, dyg, ysrc, scale, h, h, h, ymem)


def _window_count(t0, rows, w):
    t = t0 + lax.broadcasted_iota(jnp.int32, (rows, POOL_GROUP), 0)
    return jnp.minimum(t + 1, w).astype(F32)


def _pool_fwd(h, pw, *, tb, name):
    S = h.shape[0]
    tb = min(tb, S)

    def body(u_ref, pw_ref, pm_ref, mixed_ref, tail_ref):
        i = pl.program_id(0)

        @pl.when(i == 0)
        def _():
            tail_ref[...] = jnp.zeros_like(tail_ref)

        u = u_ref[...].astype(F32)
        xfull = jnp.concatenate([tail_ref[...], u], axis=0)
        for gi, w in enumerate(POOL_WINDOWS):
            cols = slice(POOL_GROUP * gi, POOL_GROUP * (gi + 1))
            s = xfull[:, cols]
            sh = 1
            while sh < w:
                s = s + pltpu.roll(s, sh, 0)
                sh *= 2
            pm = s[POOL_HALO:, :] / _window_count(i * tb, tb, w) - u[:, cols]
            pmb = pm.astype(BF16)
            pm_ref[:, cols] = pmb
            mixed_ref[:, cols] = jnp.dot(pmb, pw_ref[gi], preferred_element_type=F32)
        tail_ref[...] = u[tb - POOL_HALO:, :]

    return pl.pallas_call(
        body, name=name, grid=(S // tb,),
        in_specs=[pl.BlockSpec((tb, D_MAIN), lambda i: (i, 0)),
                  pl.BlockSpec((4, POOL_GROUP, POOL_GROUP), lambda i: (0, 0, 0))],
        out_specs=[pl.BlockSpec((tb, D_MAIN), lambda i: (i, 0)), pl.BlockSpec((tb, D_MAIN), lambda i: (i, 0))],
        out_shape=[jax.ShapeDtypeStruct((S, D_MAIN), BF16), jax.ShapeDtypeStruct((S, D_MAIN), F32)],
        scratch_shapes=[pltpu.VMEM((POOL_HALO, D_MAIN), F32)],
        compiler_params=_cparams("arbitrary"),
    )(h, pw)


def _pool_bwd(dymain, pm, mixed, pw, scale, dh, *, tb, name):
    S = dymain.shape[0]
    tb = min(tb, S)
    nb = S // tb
    n = tb + POOL_HALO

    def body(dy_ref, pm_ref, mixed_ref, pw_ref, sc_ref, dh_in, dh_ref, dpw_ref, dsc_ref, head_ref, dpw_acc):
        del dh_in
        i = pl.program_id(0)

        @pl.when(i == 0)
        def _():
            head_ref[...] = jnp.zeros_like(head_ref)
            dpw_acc[...] = jnp.zeros_like(dpw_acc)
            dsc_ref[...] = jnp.zeros_like(dsc_ref)

        dy = dy_ref[...]
        dsc_ref[...] += jnp.sum(dy * mixed_ref[...], axis=0, keepdims=True)
        dmixed = dy * sc_ref[...]
        t0 = (nb - 1 - i) * tb
        for gi, w in enumerate(POOL_WINDOWS):
            cols = slice(POOL_GROUP * gi, POOL_GROUP * (gi + 1))
            dm = dmixed[:, cols].astype(BF16)
            dpw_acc[gi] += lax.dot_general(pm_ref[:, cols], dm, TN, preferred_element_type=F32)
            dpm = lax.dot_general(dm, pw_ref[gi], NT, preferred_element_type=F32)
            e = dpm / _window_count(t0, tb, w)
            s = jnp.concatenate([e, head_ref[:, cols]], axis=0)
            sh = 1
            while sh < w:
                s = s + pltpu.roll(s, n - sh, 0)
                sh *= 2
            dh_ref[:, cols] = (s[:tb, :] - dpm).astype(BF16)
            head_ref[:, cols] = e[:POOL_HALO, :]

        @pl.when(i == nb - 1)
        def _():
            dpw_ref[...] = dpw_acc[...].astype(BF16)

    rev = lambda i: (nb - 1 - i, 0)
    return pl.pallas_call(
        body, name=name, grid=(nb,),
        in_specs=[pl.BlockSpec((tb, D_MAIN), rev), pl.BlockSpec((tb, D_MAIN), rev), pl.BlockSpec((tb, D_MAIN), rev),
                  pl.BlockSpec((4, POOL_GROUP, POOL_GROUP), lambda i: (0, 0, 0)),
                  pl.BlockSpec((1, D_MAIN), lambda i: (0, 0)), pl.BlockSpec(memory_space=pl.ANY)],
        out_specs=[pl.BlockSpec((tb, D_MAIN), rev),
                   pl.BlockSpec((4, POOL_GROUP, POOL_GROUP), lambda i: (0, 0, 0)),
                   pl.BlockSpec((1, D_MAIN), lambda i: (0, 0))],
        out_shape=[jax.ShapeDtypeStruct(dh.shape, dh.dtype),
                   jax.ShapeDtypeStruct((4, POOL_GROUP, POOL_GROUP), BF16), jax.ShapeDtypeStruct((1, D_MAIN), F32)],
        scratch_shapes=[pltpu.VMEM((POOL_HALO, D_MAIN), F32), pltpu.VMEM((4, POOL_GROUP, POOL_GROUP), F32)],
        input_output_aliases={5: 0},
        compiler_params=_cparams("arbitrary"),
    )(dymain, pm, mixed, pw, scale, dh)


MEM_SCALE = MEM_HEAD_DIM ** -0.5


def _mem_probs(q_ref, mkv_ref, hd):
    cols = slice(MEM_HEAD_DIM * hd, MEM_HEAD_DIM * (hd + 1))
    q = (q_ref[:, cols].astype(F32) * MEM_SCALE).astype(BF16)
    mk = mkv_ref[:, cols].astype(BF16)
    mv = mkv_ref[:, D_MEM + MEM_HEAD_DIM * hd:D_MEM + MEM_HEAD_DIM * (hd + 1)].astype(BF16)
    s = lax.dot_general(q, mk, NT, preferred_element_type=F32)
    e = jnp.exp(s - jnp.max(s, axis=1, keepdims=True))
    return cols, q, mk, mv, e, jnp.sum(e, axis=1, keepdims=True)


def _memattn_fwd(h, mkv, *, tb, name):
    S = h.shape[0]
    tb = min(tb, S)

    def body(q_ref, mkv_ref, y_ref):
        for hd in range(MEM_HEADS):
            cols, _, _, mv, e, l = _mem_probs(q_ref, mkv_ref, hd)
            y_ref[:, cols] = jnp.dot(e.astype(BF16), mv, preferred_element_type=F32) / l

    return pl.pallas_call(
        body, name=name, grid=(S // tb,),
        in_specs=[pl.BlockSpec((tb, D_MEM), lambda i: (i, 2)), pl.BlockSpec((N_MEM, 2 * D_MEM), lambda i: (0, 0))],
        out_specs=pl.BlockSpec((tb, D_MEM), lambda i: (i, 0)),
        out_shape=jax.ShapeDtypeStruct((S, D_MEM), F32),
        compiler_params=_cparams("parallel"),
    )(h, mkv)


def _memattn_bwd(h, mkv, dy, dh, *, tb, name):
    S = h.shape[0]
    tb = min(tb, S)

    def body(q_ref, mkv_ref, dy_ref, dh_in, dh_ref, dmkv_ref):
        del dh_in

        @pl.when(pl.program_id(0) == 0)
        def _():
            dmkv_ref[...] = jnp.zeros_like(dmkv_ref)

        for hd in range(MEM_HEADS):
            cols, q, mk, mv, e, l = _mem_probs(q_ref, mkv_ref, hd)
            p = e / l
            dyh = dy_ref[:, cols].astype(BF16)
            dp = lax.dot_general(dyh, mv, NT, preferred_element_type=F32)
            ds = p * (dp - jnp.sum(dp * p, axis=1, keepdims=True))
            dsb = ds.astype(BF16)
            dh_ref[:, cols] = (jnp.dot(dsb, mk, preferred_element_type=F32) * MEM_SCALE).astype(BF16)
            dmkv_ref[:, cols] += lax.dot_general(dsb, q, TN, preferred_element_type=F32)
            vcols = slice(D_MEM + MEM_HEAD_DIM * hd, D_MEM + MEM_HEAD_DIM * (hd + 1))
            dmkv_ref[:, vcols] += lax.dot_general(p.astype(BF16), dyh, TN, preferred_element_type=F32)

    return pl.pallas_call(
        body, name=name, grid=(S // tb,),
        in_specs=[pl.BlockSpec((tb, D_MEM), lambda i: (i, 2)), pl.BlockSpec((N_MEM, 2 * D_MEM), lambda i: (0, 0)),
                  pl.BlockSpec((tb, D_MEM), lambda i: (i, 0)), pl.BlockSpec(memory_space=pl.ANY)],
        out_specs=[pl.BlockSpec((tb, D_MEM), lambda i: (i, 2)), pl.BlockSpec((N_MEM, 2 * D_MEM), lambda i: (0, 0))],
        out_shape=[jax.ShapeDtypeStruct(dh.shape, dh.dtype), jax.ShapeDtypeStruct((N_MEM, 2 * D_MEM), F32)],
        input_output_aliases={3: 0},
        compiler_params=_cparams("arbitrary"),
    )(h, mkv, dy, dh)


def _forget_fwd(fl, bias, *, tb, name):
    S = fl.shape[0]
    tb = min(tb, S)

    def body(fl_ref, b_ref, o_ref, carry_ref):
        @pl.when(pl.program_id(0) == 0)
        def _():
            carry_ref[...] = jnp.zeros_like(carry_ref)

        z = fl_ref[...] + b_ref[...]
        lf = jnp.minimum(z, 0.0) - jnp.log(1.0 + jnp.exp(-jnp.abs(z)))
        row = lax.broadcasted_iota(jnp.int32, (tb, LANES), 0)
        c = lf
        sh = 1
        while sh < tb:
            c = c + jnp.where(row >= sh, pltpu.roll(c, sh, 0), 0.0)
            sh *= 2
        o_ref[...] = -(carry_ref[...] + c)
        carry_ref[...] += jnp.sum(lf, axis=0, keepdims=True)

    return pl.pallas_call(
        body, name=name, grid=(S // tb,),
        in_specs=[pl.BlockSpec((tb, LANES), lambda i: (i, 0)), pl.BlockSpec((1, LANES), lambda i: (0, 0))],
        out_specs=pl.BlockSpec((tb, LANES), lambda i: (i, 0)),
        out_shape=jax.ShapeDtypeStruct((S, LANES), F32),
        scratch_shapes=[pltpu.VMEM((1, LANES), F32)],
        compiler_params=_cparams("arbitrary"),
    )(fl, bias)


def _forget_bwd(dn, drow, fl, bias, dh2, *, tb, name):
    S = fl.shape[0]
    tb = min(tb, S)
    nb = S // tb

    def body(dn_ref, dr_ref, fl_ref, b_ref, dh_in, dh_ref, db_ref, carry_ref):
        del dh_in

        @pl.when(pl.program_id(0) == 0)
        def _():
            carry_ref[...] = jnp.zeros_like(carry_ref)
            db_ref[...] = jnp.zeros_like(db_ref)

        src = lax.broadcasted_iota(jnp.int32, (D_MAIN, LANES), 0)
        head = lax.broadcasted_iota(jnp.int32, (D_MAIN, LANES), 1)
        pick = lambda off: jnp.where((src == FOX_HEAD_DIM * head + off) & (head < FOX_HEADS), 1.0, 0.0).astype(BF16)
        hdot = lambda a, sel: sum(jnp.dot(part.astype(BF16), sel, preferred_element_type=F32) for part in _split3(a))
        dcum = hdot(dr_ref[...], pick(3)) - hdot(dn_ref[...], pick(0))
        row = lax.broadcasted_iota(jnp.int32, (tb, LANES), 0)
        c = dcum
        sh = 1
        while sh < tb:
            c = c + jnp.where(row < tb - sh, pltpu.roll(c, tb - sh, 0), 0.0)
            sh *= 2
        dlf = carry_ref[...] + c
        carry_ref[...] += jnp.sum(dcum, axis=0, keepdims=True)
        z = fl_ref[...] + b_ref[...]
        lane = lax.broadcasted_iota(jnp.int32, (tb, LANES), 1)
        dfl = jnp.where(lane < FOX_HEADS, dlf / (1.0 + jnp.exp(z)), 0.0)
        db_ref[...] += jnp.sum(dfl, axis=0, keepdims=True)
        dh_ref[...] = dfl.astype(BF16)

    rev = lambda i: (nb - 1 - i, 0)
    return pl.pallas_call(
        body, name=name, grid=(nb,),
        in_specs=[pl.BlockSpec((tb, D_MAIN), rev), pl.BlockSpec((tb, D_MAIN), rev), pl.BlockSpec((tb, LANES), rev),
                  pl.BlockSpec((1, LANES), lambda i: (0, 0)), pl.BlockSpec(memory_space=pl.ANY)],
        out_specs=[pl.BlockSpec((tb, LANES), lambda i: (nb - 1 - i, 2 * D_MAIN // LANES)),
                   pl.BlockSpec((1, LANES), lambda i: (0, 0))],
        out_shape=[jax.ShapeDtypeStruct(dh2.shape, dh2.dtype), jax.ShapeDtypeStruct((1, LANES), F32)],
        scratch_shapes=[pltpu.VMEM((1, LANES), F32)],
        input_output_aliases={4: 0},
        compiler_params=_cparams("arbitrary"),
    )(dn, drow, fl, bias, dh2)


FOX_SCALE = FOX_HEAD_DIM ** -0.5
LOG2E = 1.4426950408889634
LN2 = 0.6931471805599453
AUX = FOX_HEAD_DIM


def _split3(x):
    hi = x.astype(BF16).astype(F32)
    r = x - hi
    mid = r.astype(BF16).astype(F32)
    return hi, mid, (r - mid).astype(BF16).astype(F32)


def _lanes3(lane, base, parts, rest):
    return jnp.where(lane == base, parts[0], jnp.where(lane == base + 1, parts[1],
                                                       jnp.where(lane == base + 2, parts[2], rest)))


def _swap_halves(x):
    return pltpu.roll(x, FOX_HEAD_DIM, 1)


def _causal_steps(nq, keys_outer):
    if keys_outer:
        pairs = [(i, j) for j in range(nq) for i in range(j, nq)]
    else:
        pairs = [(i, j) for i in range(nq) for j in range(i + 1)]
    it, jt = zip(*pairs)
    return jnp.asarray(np.array(it, np.int32)), jnp.asarray(np.array(jt, np.int32))


def _fox_prep_fwd(h, kv, negcum, *, tb, name):
    S = h.shape[0]
    tb = min(tb, S)

    def body(q_ref, kv_ref, nc_ref, qa_ref, ka_ref, va_ref):
        g = pl.program_id(0)
        lane = lax.broadcasted_iota(jnp.int32, (tb, LANES), 1)
        first = lane < FOX_HEAD_DIM
        q = q_ref[...].astype(F32) * (FOX_SCALE * LOG2E)
        k = kv_ref[:, :LANES].astype(F32)
        v = kv_ref[:, LANES:].astype(F32)
        nc = nc_ref[...]
        ones_q = jnp.where((lane >= AUX) & (lane < AUX + 3), 1.0, 0.0)
        ones_k = jnp.where((lane >= AUX + 3) & (lane < AUX + 6), 1.0, 0.0)
        for hh in range(2):
            sl = slice(LANES * hh, LANES * (hh + 1))
            qh, kh, vh = (q, k, v) if hh == 0 else (_swap_halves(q), _swap_halves(k), _swap_halves(v))
            ncol = jnp.sum(jnp.where(lane == 2 * g + hh, nc, 0.0), axis=1, keepdims=True) * LOG2E
            qa_ref[:, sl] = jnp.where(first, qh, ones_q).astype(BF16)
            ka_ref[:, sl] = jnp.where(first, kh, _lanes3(lane, AUX, _split3(ncol), ones_k)).astype(BF16)
            va_ref[:, sl] = jnp.where(first, vh, 1.0).astype(BF16)

    pair = pl.BlockSpec((tb, 2 * LANES), lambda g, i: (i, g))
    shp = jax.ShapeDtypeStruct((S, 2 * D_MAIN), BF16)
    return pl.pallas_call(
        body, name=name, grid=(FOX_PAIRS, S // tb),
        in_specs=[pl.BlockSpec((tb, LANES), lambda g, i: (i, g)), pair, pl.BlockSpec((tb, LANES), lambda g, i: (i, 0))],
        out_specs=[pair, pair, pair], out_shape=[shp, shp, shp],
        compiler_params=_cparams("parallel", "parallel"),
    )(h, kv, negcum)


def _fox_fwd(qa, ka, va, *, tq, name):
    S = qa.shape[0]
    tq = min(tq, S)
    nq = S // tq
    rep = tq // LANES
    it, jt = _causal_steps(nq, keys_outer=False)

    def body(it_ref, jt_ref, qa_ref, ka_ref, va_ref, y_ref, lse_ref, m_ref, acc_ref):
        n = pl.program_id(1)
        i, j = it_ref[n], jt_ref[n]
        first = lax.broadcasted_iota(jnp.int32, (tq, LANES), 1) < FOX_HEAD_DIM

        @pl.when(j == 0)
        def _():
            m_ref[...] = jnp.full_like(m_ref, NEG)
            acc_ref[...] = jnp.zeros_like(acc_ref)

        def step(masked):
            for hh in range(2):
                sl = slice(LANES * hh, LANES * (hh + 1))
                s = lax.dot_general(qa_ref[:, sl], ka_ref[:, sl], NT, preferred_element_type=F32)
                if masked:
                    r = lax.broadcasted_iota(jnp.int32, (tq, tq), 0)
                    c = lax.broadcasted_iota(jnp.int32, (tq, tq), 1)
                    s = jnp.where(c <= r, s, NEG)
                m_prev = m_ref[hh]
                m_new = jnp.maximum(m_prev, jnp.max(s, axis=1, keepdims=True))
                p = jnp.exp2(s - jnp.tile(m_new, (1, rep))).astype(BF16)
                acc_ref[hh] = jnp.exp2(m_prev - m_new) * acc_ref[hh] + jnp.dot(p, va_ref[:, sl],
                                                                               preferred_element_type=F32)
                m_ref[hh] = m_new

        @pl.when(j < i)
        def _():
            step(False)

        @pl.when(j == i)
        def _():
            step(True)
            ys, lses = [], []
            for hh in range(2):
                a = acc_ref[hh]
                denom = _swap_halves(a)
                ys.append(a / denom)
                lses.append(m_ref[hh] + jnp.log(jnp.where(first, denom, a)) * LOG2E)
            y_ref[...] = jnp.where(first, ys[0], _swap_halves(ys[1]))
            lse_ref[...] = jnp.where(first, lses[0], lses[1])

    qblock = pl.BlockSpec((tq, 2 * LANES), lambda g, n, it, jt: (it[n], g))
    kblock = pl.BlockSpec((tq, 2 * LANES), lambda g, n, it, jt: (jt[n], g))
    out = pl.BlockSpec((tq, LANES), lambda g, n, it, jt: (it[n], g))
    return pl.pallas_call(
        body, name=name,
        grid_spec=pltpu.PrefetchScalarGridSpec(
            num_scalar_prefetch=2, grid=(FOX_PAIRS, it.shape[0]),
            in_specs=[qblock, kblock, kblock], out_specs=[out, out],
            scratch_shapes=[pltpu.VMEM((2, tq, LANES), F32), pltpu.VMEM((2, tq, LANES), F32)]),
        out_shape=[jax.ShapeDtypeStruct((S, D_MAIN), F32), jax.ShapeDtypeStruct((S, D_MAIN), F32)],
        compiler_params=_cparams("parallel", "arbitrary"),
    )(it, jt, qa, ka, va)


def _fox_prep_bwd(qa, lse2, dy, y, *, tb, name):
    S = qa.shape[0]
    tb = min(tb, S)

    def body(qa_ref, lse_ref, dy_ref, y_ref, qb_ref, dya_ref):
        lane = lax.broadcasted_iota(jnp.int32, (tb, LANES), 1)
        first = lane < FOX_HEAD_DIM
        lse = lse_ref[...]
        lse_sw = _swap_halves(lse)
        dy = dy_ref[...]
        prod = dy * y_ref[...]
        for hh in range(2):
            sl = slice(LANES * hh, LANES * (hh + 1))
            lse_h = jnp.where(first, lse, lse_sw) if hh == 0 else jnp.where(first, lse_sw, lse)
            qb_ref[:, sl] = _lanes3(lane, AUX + 3, _split3(-lse_h), qa_ref[:, sl].astype(F32)).astype(BF16)
            delta = jnp.sum(jnp.where(first == (hh == 0), prod, 0.0), axis=1, keepdims=True)
            dyh = dy if hh == 0 else _swap_halves(dy)
            dya_ref[:, sl] = jnp.where(first, dyh, _lanes3(lane, AUX, _split3(-delta), 0.0)).astype(BF16)

    pair = pl.BlockSpec((tb, 2 * LANES), lambda g, i: (i, g))
    one = pl.BlockSpec((tb, LANES), lambda g, i: (i, g))
    shp = jax.ShapeDtypeStruct((S, 2 * D_MAIN), BF16)
    return pl.pallas_call(
        body, name=name, grid=(FOX_PAIRS, S // tb),
        in_specs=[pair, one, one, one], out_specs=[pair, pair], out_shape=[shp, shp],
        compiler_params=_cparams("parallel", "parallel"),
    )(qa, lse2, dy, y)


def _fox_bwd(qb, ka, va, dya, dh, *, tq, name):
    S = qb.shape[0]
    tq = min(tq, S)
    nq = S // tq
    it, jt = _causal_steps(nq, keys_outer=True)
    nsteps = it.shape[0]

    def body(it_ref, jt_ref, qb_ref, ka_ref, va_ref, dya_ref, dh_in, dq_ref, dkv_ref, dn_ref, drow_ref,
             dq_acc, dk_acc, dv_acc):
        del dh_in
        n = pl.program_id(1)
        i, j = it_ref[n], jt_ref[n]
        first = lax.broadcasted_iota(jnp.int32, (tq, LANES), 1) < FOX_HEAD_DIM

        @pl.when(n == 0)
        def _():
            dq_acc[...] = jnp.zeros_like(dq_acc)

        @pl.when(i == j)
        def _():
            dk_acc[...] = jnp.zeros_like(dk_acc)
            dv_acc[...] = jnp.zeros_like(dv_acc)

        def step(masked):
            rows = pl.ds(pl.multiple_of(i * tq, tq), tq)
            for hh in range(2):
                sl = slice(LANES * hh, LANES * (hh + 1))
                qbh, kah, dyah = qb_ref[:, sl], ka_ref[:, sl], dya_ref[:, sl]
                eT = lax.dot_general(kah, qbh, NT, preferred_element_type=F32)
                if masked:
                    r = lax.broadcasted_iota(jnp.int32, (tq, tq), 0)
                    c = lax.broadcasted_iota(jnp.int32, (tq, tq), 1)
                    eT = jnp.where(r <= c, eT, NEG)
                pT = jnp.exp2(eT)
                dsT = pT * lax.dot_general(va_ref[:, sl], dyah, NT, preferred_element_type=F32)
                dsb = dsT.astype(BF16)
                dv_acc[hh] += jnp.dot(pT.astype(BF16), dyah, preferred_element_type=F32)
                dk_acc[hh] += jnp.dot(dsb, qbh, preferred_element_type=F32)
                dq_acc[hh, rows, :] += lax.dot_general(dsb, kah, TN, preferred_element_type=F32)

        @pl.when(i > j)
        def _():
            step(False)

        @pl.when(i == j)
        def _():
            step(True)

        @pl.when(i == nq - 1)
        def _():
            dkv_ref[:, :LANES] = (jnp.where(first, dk_acc[0], _swap_halves(dk_acc[1])) * LN2).astype(BF16)
            dkv_ref[:, LANES:] = jnp.where(first, dv_acc[0], _swap_halves(dv_acc[1])).astype(BF16)
            dn_ref[...] = jnp.where(first, _swap_halves(dk_acc[0]), dk_acc[1])

        @pl.when(n == nsteps - 1)
        def _():
            first_s = lax.broadcasted_iota(jnp.int32, (S, LANES), 1) < FOX_HEAD_DIM
            dq_ref[...] = (jnp.where(first_s, dq_acc[0], _swap_halves(dq_acc[1])) * FOX_SCALE).astype(BF16)
            drow_ref[...] = jnp.where(first_s, _swap_halves(dq_acc[0]), dq_acc[1])

    qblock = pl.BlockSpec((tq, 2 * LANES), lambda g, n, it, jt: (it[n], g))
    kblock = pl.BlockSpec((tq, 2 * LANES), lambda g, n, it, jt: (jt[n], g))
    whole = pl.BlockSpec((S, LANES), lambda g, n, it, jt: (0, g))
    return pl.pallas_call(
        body, name=name,
        grid_spec=pltpu.PrefetchScalarGridSpec(
            num_scalar_prefetch=2, grid=(FOX_PAIRS, nsteps),
            in_specs=[qblock, kblock, kblock, qblock, pl.BlockSpec(memory_space=pl.ANY)],
            out_specs=[whole, kblock, pl.BlockSpec((tq, LANES), lambda g, n, it, jt: (jt[n], g)), whole],
            scratch_shapes=[pltpu.VMEM((2, S, LANES), F32), pltpu.VMEM((2, tq, LANES), F32),
                            pltpu.VMEM((2, tq, LANES), F32)]),
        out_shape=[jax.ShapeDtypeStruct(dh.shape, dh.dtype), jax.ShapeDtypeStruct((S, 2 * D_MAIN + LANES), BF16),
                   jax.ShapeDtypeStruct((S, D_MAIN), F32), jax.ShapeDtypeStruct((S, D_MAIN), F32)],
        input_output_aliases={6: 0},
        compiler_params=_cparams("parallel", "arbitrary"),
    )(it, jt, qb, ka, va, dya, dh)


TB_ROWS = 256
TB_SEQ = 512
TB_PREP = 2048
TQ_FOX_FWD = 1024
TQ_FOX_BWD = 1024


def _local_step(x, mem, target, win0, wmkv, wout0, pw, pscale, late_weights, ln_g, ln_b, bias, send_early=None):
    ones = jnp.ones((1, D_MAIN), F32)
    g0, b0, g1, b1 = ln_g[0:1], ln_b[0:1], ln_g[1:2], ln_b[1:2]
    mm = lambda a, b, mode, dt, tm, tn, tk, name, **kw: _mm(a, b, mode=mode, out_dtype=dt, tm=tm, tn=tn, tk=tk,
                                                            name=name, **kw)

    h0 = mm(x, win0, "nn", BF16, 256, D_IN, D_MODEL, "l0_in")
    pm, mixed = _pool_fwd(h0, pw, tb=TB_SEQ, name="l0_pool_fwd")
    mkv0 = mm(mem, wmkv[0], "nn", F32, 256, 1024, 1024, "l0_mkv")
    ymem0 = _memattn_fwd(h0, mkv0, tb=TB_SEQ, name="l0_mem_fwd")
    yg0 = _gate_fwd(mixed, pscale, h0, ymem0, tb=TB_ROWS, name="l0_gate_fwd")
    x1, xhat0, rstd0 = _out_ln(yg0, wout0, x, g0, b0, tb=TB_SEQ, name="l0_out_ln")
    win1, wout1, wkvp, wf = late_weights(x1)

    kv = mm(x1, wkvp, "nn", BF16, 512, 2 * D_MAIN, D_MODEL, "kv_proj")
    fl = mm(x1, wf, "nn", F32, 512, LANES, D_MODEL, "f_proj")
    negcum = _forget_fwd(fl, bias, tb=TB_SEQ, name="forget_fwd")

    h1 = mm(x1, win1, "nn", BF16, 256, D_IN, D_MODEL, "l1_in")
    qa, ka, va = _fox_prep_fwd(h1, kv, negcum, tb=TB_PREP, name="fox_prep_fwd")
    y1, lse2 = _fox_fwd(qa, ka, va, tq=TQ_FOX_FWD, name="fox_fwd")
    mkv1 = mm(mem, wmkv[1], "nn", F32, 256, 1024, 1024, "l1_mkv")
    ymem1 = _memattn_fwd(h1, mkv1, tb=TB_SEQ, name="l1_mem_fwd")
    yg1 = _gate_fwd(y1, ones, h1, ymem1, tb=TB_ROWS, name="l1_gate_fwd")
    dz1, dg1, db1, sq = _out_ln_loss(yg1, wout1, x1, g1, b1, target, tb=TB_SEQ, name="l1_out_ln_loss")

    dwout1 = mm(yg1, dz1, "tn", BF16, D_MIX, D_MODEL, 512, "l1_dwout")
    dyg1 = mm(dz1, wout1, "nt", BF16, 512, D_MIX, D_MODEL, "l1_dyg")
    dy1, dymem1, dh1 = _gate_bwd(dyg1, y1, ones, h1, ymem1, tb=TB_ROWS, name="l1_gate_bwd")
    qb, dya = _fox_prep_bwd(qa, lse2, dy1, y1, tb=TB_PREP, name="fox_prep_bwd")
    dh1, dh2, dnp, drowp = _fox_bwd(qb, ka, va, dya, dh1, tq=TQ_FOX_BWD, name="fox_bwd")
    dh2, dbias = _forget_bwd(dnp, drowp, fl, bias, dh2, tb=TB_SEQ, name="forget_bwd")
    dh1, dmkv1 = _memattn_bwd(h1, mkv1, dymem1, dh1, tb=TB_SEQ, name="l1_mem_bwd")
    dwmkv1 = mm(mem, dmkv1, "tn", BF16, D_MODEL, 1024, N_MEM, "l1_dwmkv")
    dwin1 = mm(x1, dh1, "tn", BF16, D_MODEL, D_IN // 2, 512, "l1_dwin")
    dwkvf = mm(x1, dh2, "tn", F32, D_MODEL, 2 * D_MAIN + LANES, 512, "dwkv")
    dkv = dwkvf[:, :2 * D_MAIN].reshape(D_MODEL, FOX_PAIRS, 2, LANES)
    dwkv = jnp.concatenate([dkv[:, :, 0, :].reshape(D_MODEL, D_MAIN), dkv[:, :, 1, :].reshape(D_MODEL, D_MAIN),
                            dwkvf[:, 2 * D_MAIN:2 * D_MAIN + FOX_HEADS]], axis=1)
    dwkv = dwkv.reshape(D_MODEL, N_DEV, -1).transpose(1, 0, 2).astype(BF16)
    anchor = send_early(dict(w_out=dwout1, w_mem_kv=dwmkv1, w_in=dwin1, w_kv_shared=dwkv)) if send_early else 0.0
    dx1 = mm(dh1, win1, "nt", F32, 256, D_MODEL, D_IN, "l1_dx", add=dz1, add_scale=ALPHA)
    wkvf = jnp.concatenate([wkvp, wf], axis=1)
    dx1 = mm(dh2, wkvf, "nt", F32, 256, D_MODEL, 2 * D_MAIN + LANES, "kv_dx", add=dx1)

    dz0, dg0, db0 = _ln_bwd(dx1, xhat0, rstd0, g0 + anchor, tb=TB_ROWS, name="l0_ln_bwd")
    dwout0 = mm(yg0, dz0, "tn", BF16, D_MIX, D_MODEL, 512, "l0_dwout")
    dyg0 = mm(dz0, wout0, "nt", BF16, 512, D_MIX, D_MODEL, "l0_dyg")
    dy0, dymem0, dh0 = _gate_bwd(dyg0, mixed, pscale, h0, ymem0, tb=TB_ROWS, name="l0_gate_bwd")
    dh0, dpw, dpscale = _pool_bwd(dy0, pm, mixed, pw, pscale, dh0, tb=TB_SEQ, name="l0_pool_bwd")
    dh0, dmkv0 = _memattn_bwd(h0, mkv0, dymem0, dh0, tb=TB_SEQ, name="l0_mem_bwd")
    dwmkv0 = mm(mem, dmkv0, "tn", BF16, D_MODEL, 1024, N_MEM, "l0_dwmkv")
    dwin0 = mm(x, dh0, "tn", BF16, D_MODEL, D_IN // 2, 512, "l0_dwin")
    gx = mm(dh0, win0, "nt", F32, 256, D_MODEL, D_IN, "l0_dx", add=dz0, add_scale=ALPHA)

    grads = dict(w_in=(dwin0, dwin1), w_mem_kv=(dwmkv0, dwmkv1), w_out=(dwout0, dwout1), pool_w=(dpw,),
                 w_kv_shared=dwkv, pool_scale=dpscale, ln_g=jnp.concatenate([dg0, dg1]),
                 ln_b=jnp.concatenate([db0, db1]), b_forget=dbias[0, :FOX_HEADS])
    return sq, gx, grads


MESH_ID = pl.DeviceIdType.MESH
HBM = pl.BlockSpec(memory_space=pl.ANY)
SLICED = {"w_in": (2, D_IN // N_DEV), "w_mem_kv": (1, D_MODEL // N_DEV), "w_out": (1, D_MIX // N_DEV),
          "pool_w": (1, POOL_GROUP // N_DEV)}


def _place():
    return lax.axis_index("x"), lax.axis_index("y"), lax.axis_index("c")


def _slot(p):
    return 4 * p[0] + 2 * p[1] + p[2]


def _cut(ref, axis, width, s):
    idx = [slice(None)] * len(ref.shape)
    idx[axis] = pl.ds(s * width, width)
    return ref.at[tuple(idx)]


def _all_gather(shards, cuts, *, name):
    nt = len(shards)

    def full_shape(a, cut):
        if cut is None:
            return (N_DEV,) + a.shape
        return a.shape[:cut[0]] + (a.shape[cut[0]] * N_DEV,) + a.shape[cut[0] + 1:]

    def body(*refs):
        ins, outs = refs[:nt], refs[nt:2 * nt]
        send_sems, recv_sems, local_sems = refs[2 * nt:]
        x, y, c = _place()
        me, sibling = (x, y, c), (x, y, 1 - c)
        chips = [(1 - x, y), (x, 1 - y), (1 - x, 1 - y)]

        def place(t, s):
            return outs[t].at[s] if cuts[t] is None else _cut(outs[t], cuts[t][0], cuts[t][1], s)

        def copies(k, block, to, from_input=False):
            s = _slot(block)
            return [pltpu.make_async_remote_copy(
                src_ref=ins[t] if from_input else place(t, s), dst_ref=place(t, s),
                send_sem=send_sems.at[nt * k + t], recv_sem=recv_sems.at[nt * k + t],
                device_id=to, device_id_type=MESH_ID) for t in range(nt)]

        mine = [pltpu.make_async_copy(ins[t], place(t, _slot(me)), local_sems.at[t]) for t in range(nt)]
        for cp in mine:
            cp.start()
        first = [copies(0, me, sibling, True)] + [copies(1 + j, me, (*chip, c), True) for j, chip in enumerate(chips)]
        for group in first:
            for cp in group:
                cp.start()
        passed = [copies(4 + j, (*chip, c), sibling) for j, chip in enumerate(chips)]
        for j, chip in enumerate(chips):
            for cp in copies(1 + j, (*chip, c), me):
                cp.wait_recv()
            for cp in passed[j]:
                cp.start()
        for cp in copies(0, sibling, me):
            cp.wait_recv()
        for j, chip in enumerate(chips):
            for cp in copies(4 + j, (*chip, 1 - c), me):
                cp.wait_recv()
        for group in first + passed:
            for cp in group:
                cp.wait_send()
        for cp in mine:
            cp.wait()

    return pl.pallas_call(
        body, name=name, in_specs=[HBM] * nt, out_specs=[HBM] * nt,
        out_shape=[jax.ShapeDtypeStruct(full_shape(a, cut), a.dtype) for a, cut in zip(shards, cuts)],
        scratch_shapes=[pltpu.SemaphoreType.DMA((7 * nt,)), pltpu.SemaphoreType.DMA((7 * nt,)),
                        pltpu.SemaphoreType.DMA((nt,))],
    )(*shards)


def _exchange_copies(items, ins, outs, send_sems, recv_sems, local_sems, gather=False):
    nt = len(items)
    x, y, c = _place()
    me = _slot((x, y, c))
    flip = lambda v, bit: 1 - v if bit else v

    def part(ref, cut, s):
        return ref.at[s] if cut is None else _cut(ref, cut[0], cut[1], s)

    def src(t, s):
        return ins[t] if gather else part(ins[t], items[t][0], s)

    def dst(t, s):
        if gather:
            return part(outs[items[t][1]], items[t][0], s)
        d = outs[items[t][1]].at[s]
        return d if items[t][2] is None else d.at[items[t][2]]

    sends, arrivals = [], []
    for k in range(1, N_DEV):
        peer = (flip(x, k & 4), flip(y, k & 2), flip(c, k & 1))
        ps = _slot(peer)
        for t in range(nt):
            sems = dict(send_sem=send_sems.at[nt * (k - 1) + t], recv_sem=recv_sems.at[nt * (k - 1) + t],
                        device_id=peer, device_id_type=MESH_ID)
            sends.append(pltpu.make_async_remote_copy(src_ref=src(t, ps), dst_ref=dst(t, me), **sems))
            arrivals.append(pltpu.make_async_remote_copy(src_ref=src(t, ps), dst_ref=dst(t, ps), **sems))
    mine = [pltpu.make_async_copy(src(t, me), dst(t, me), local_sems.at[t]) for t in range(nt)]
    return sends, arrivals, mine


def _landing(spec):
    return spec if isinstance(spec, jax.ShapeDtypeStruct) else jax.ShapeDtypeStruct(spec.shape, spec.dtype)


def _exchange(srcs, items, landings, *, name, gather=False):
    nt, nl = len(srcs), len(landings)
    given = [i for i, l in enumerate(landings) if not isinstance(l, jax.ShapeDtypeStruct)]

    def body(*refs):
        ins, outs = refs[:nt], refs[nt + len(given):nt + len(given) + nl]
        sends, arrivals, mine = _exchange_copies(items, ins, outs, *refs[nt + len(given) + nl:], gather=gather)
        for cp in sends + mine:
            cp.start()
        for sent, landed in zip(sends, arrivals):
            landed.wait_recv()
            sent.wait_send()
        for cp in mine:
            cp.wait()

    return pl.pallas_call(
        body, name=name, in_specs=[HBM] * (nt + len(given)), out_specs=[HBM] * nl,
        out_shape=[_landing(l) for l in landings],
        input_output_aliases={nt + n: i for n, i in enumerate(given)},
        scratch_shapes=[pltpu.SemaphoreType.DMA((7 * nt,)), pltpu.SemaphoreType.DMA((7 * nt,)),
                        pltpu.SemaphoreType.DMA((nt,))],
    )(*srcs, *[landings[i] for i in given])


SEMS = pl.BlockSpec(memory_space=pltpu.SEMAPHORE)
SIDE_EFFECT = pltpu.SideEffectType.DATAFLOW_SIDE_EFFECTING


def _exchange_start(srcs, items, landings, *, name, gather=False, carry=()):
    nt, nl, nc = len(srcs), len(landings), len(carry)

    def body(*refs):
        ins, lands = refs[:nt], refs[nt:nt + nl]
        send_sems, recv_sems, local_sems = refs[nt + nl + nc:nt + nl + nc + 3]
        token = refs[-1]
        sends, _, mine = _exchange_copies(items, ins, lands, send_sems, recv_sems, local_sems, gather)
        for cp in sends + mine:
            cp.start()
        token[...] = jnp.zeros_like(token)

    hbm = lambda a: pltpu.HBM(a.shape, a.dtype)
    fresh = [pltpu.with_memory_space_constraint(lax.empty(l.shape, l.dtype), pltpu.HBM) for l in landings]
    res = pl.pallas_call(
        body, name=name, in_specs=[HBM] * (nt + nl + nc),
        out_specs=[SEMS, SEMS, SEMS] + [HBM] * (nt + nl + nc) + [pl.BlockSpec(memory_space=pltpu.VMEM)],
        out_shape=[pltpu.SemaphoreType.DMA((7 * nt,)), pltpu.SemaphoreType.DMA((7 * nt,)), pltpu.SemaphoreType.DMA((nt,))]
        + [hbm(a) for a in srcs] + [hbm(l) for l in landings] + [hbm(a) for a in carry]
        + [jax.ShapeDtypeStruct((8, LANES), F32)],
        input_output_aliases={i: 3 + i for i in range(nt + nl + nc)},
        compiler_params=pltpu.CompilerParams(has_side_effects=SIDE_EFFECT),
    )(*[pltpu.with_memory_space_constraint(a, pltpu.HBM) for a in srcs], *fresh,
      *[pltpu.with_memory_space_constraint(a, pltpu.HBM) for a in carry])
    return res[:3 + nt + nl], res[-1][0:1, 0:1], res[3 + nt + nl:-1]


def _exchange_wait(state, items, nt, after, *, name, gather=False):
    sems, bufs = state[:3], state[3:]
    nl = len(bufs) - nt

    def body(*refs):
        ins, lands = refs[:nt], refs[nt:nt + nl]
        send_sems, recv_sems, local_sems = refs[nt + nl:nt + nl + 3]
        sends, arrivals, mine = _exchange_copies(items, ins, lands, send_sems, recv_sems, local_sems, gather)
        for sent, landed in zip(sends, arrivals):
            landed.wait_recv()
            sent.wait_send()
        for cp in mine:
            cp.wait()

    hbm = lambda a: pltpu.HBM(a.shape, a.dtype)
    res = pl.pallas_call(
        body, name=name, in_specs=[HBM] * (nt + nl) + [SEMS, SEMS, SEMS, HBM], out_specs=[HBM] * (nt + nl),
        out_shape=[hbm(a) for a in bufs],
        input_output_aliases={i: i for i in range(nt + nl)},
        compiler_params=pltpu.CompilerParams(has_side_effects=SIDE_EFFECT),
    )(*bufs, *sems, after)
    return res[nt:]


def _adamw(recv, w, m, v, *, split, name):
    shape = w.shape
    axis, parts = split
    block = shape[:axis] + (shape[axis] // parts,) + shape[axis + 1:]
    nd = len(shape)

    def body(r_ref, w_ref, m_ref, v_ref, g_ref, d_ref, nm_ref, nv_ref):
        g = r_ref[0].astype(F32)
        for j in range(1, N_DEV):
            g = g + r_ref[j].astype(F32)
        nm = ADAM_B1 * m_ref[...] + (1.0 - ADAM_B1) * g
        nv = ADAM_B2 * v_ref[...] + (1.0 - ADAM_B2) * (g * g)
        m_hat = nm / (1.0 - ADAM_B1 ** ADAM_STEP)
        v_hat = nv / (1.0 - ADAM_B2 ** ADAM_STEP)
        g_ref[...] = g
        nm_ref[...] = nm
        nv_ref[...] = nv
        d_ref[...] = -ADAM_LR * (m_hat / (jnp.sqrt(v_hat) + ADAM_EPS) + ADAM_WD * w_ref[...])

    at = lambda i: tuple(i if a == axis else 0 for a in range(nd))
    one = pl.BlockSpec(block, at)
    shp = jax.ShapeDtypeStruct(shape, F32)
    return pl.pallas_call(
        body, name=name, grid=(parts,),
        in_specs=[pl.BlockSpec((N_DEV,) + block, lambda i: (0,) + at(i)), one, one, one],
        out_specs=[one, one, one, one], out_shape=[shp, shp, shp, shp],
        compiler_params=_cparams("parallel"),
    )(recv, w, m, v)


BIG = ("w_in", "w_mem_kv", "w_out", "pool_w", "w_kv_shared")
SMALL = ("pool_scale", "ln_g", "ln_b", "b_forget")
SMALL_ROWS = 40
ADAM_SPLIT = {"w_in": (1, 4), "w_mem_kv": (0, 2), "w_out": (0, 2), "pool_w": (0, 1), "w_kv_shared": (0, 4)}
PER_LAYER_CUT = {"w_out": (0, D_MIX // N_DEV), "w_mem_kv": (0, D_MODEL // N_DEV), "w_in": (1, D_IN // N_DEV),
                 "pool_w": (1, POOL_GROUP // N_DEV)}
EARLY = ("w_out", "w_mem_kv", "w_in", "w_kv_shared")
EARLY_ITEMS = [(PER_LAYER_CUT[n], i, 1) for i, n in enumerate(EARLY[:3])] + [(None, 3, None)]


def _flat(parts, rows):
    v = jnp.concatenate([p.reshape(-1) for p in parts])
    return jnp.pad(v, (0, rows * LANES - v.shape[0])).reshape(rows, LANES)


def _unflat(flat, shapes):
    v, out, off = flat.reshape(-1), [], 0
    for s in shapes:
        n = math.prod(s)
        out.append(v[off:off + n].reshape(s))
        off += n
    return out


def kernel(x, mem, w_in, w_mem_kv, w_out, ln_g, ln_b, pool_w, pool_scale, w_kv_shared, b_forget, loss_target, m_w_in, m_w_mem_kv, m_w_out, m_ln_g, m_ln_b, m_pool_w, m_pool_scale, m_w_kv_shared, m_b_forget, v_w_in, v_w_mem_kv, v_w_out, v_ln_g, v_ln_b, v_pool_w, v_pool_scale, v_w_kv_shared, v_b_forget):
    w = dict(w_in=w_in, w_mem_kv=w_mem_kv, w_out=w_out, ln_g=ln_g, ln_b=ln_b, pool_w=pool_w[0],
             pool_scale=pool_scale, w_kv_shared=w_kv_shared, b_forget=b_forget)
    m = dict(w_in=m_w_in, w_mem_kv=m_w_mem_kv, w_out=m_w_out, ln_g=m_ln_g, ln_b=m_ln_b, pool_w=m_pool_w[0],
             pool_scale=m_pool_scale, w_kv_shared=m_w_kv_shared, b_forget=m_b_forget)
    v = dict(w_in=v_w_in, w_mem_kv=v_w_mem_kv, w_out=v_w_out, ln_g=v_ln_g, ln_b=v_ln_b, pool_w=v_pool_w[0],
             pool_scale=v_pool_scale, w_kv_shared=v_w_kv_shared, b_forget=v_b_forget)

    wb = {n: w[n].astype(BF16) for n in BIG}
    win0, wmkv, pw, wout0, pscale = _all_gather(
        [wb["w_in"][0], wb["w_mem_kv"], wb["pool_w"], wb["w_out"][0], jnp.pad(pool_scale, ((0, 7), (0, 0)))],
        [PER_LAYER_CUT["w_in"], SLICED["w_mem_kv"], SLICED["pool_w"], PER_LAYER_CUT["w_out"], None],
        name="gather_weights")
    pscale = pscale[:, 0, :].reshape(1, D_MAIN)
    late_srcs = [wb["w_in"][1], wb["w_out"][1], wb["w_kv_shared"]]
    late_items = [(PER_LAYER_CUT["w_in"], 0, None), (PER_LAYER_CUT["w_out"], 1, None), (None, 2, None)]
    late_state, _, (win0,) = _exchange_start(
        late_srcs, late_items,
        [jax.ShapeDtypeStruct((D_MODEL, D_IN), wb["w_in"].dtype), jax.ShapeDtypeStruct((D_MIX, D_MODEL), wb["w_in"].dtype),
         jax.ShapeDtypeStruct((N_DEV,) + w_kv_shared.shape, wb["w_in"].dtype)],
        name="gather_late_start", gather=True, carry=[win0])
    bias = jnp.pad(b_forget, (0, LANES - FOX_HEADS)).reshape(1, LANES)

    def late_weights(x1):
        win1, wout1, wkv = _exchange_wait(late_state, late_items, len(late_srcs), x1, name="gather_late_wait", gather=True)
        wkv = wkv.transpose(1, 0, 2).reshape(D_MODEL, -1)
        wkvp = jnp.stack([wkv[:, :D_MAIN].reshape(D_MODEL, FOX_PAIRS, LANES),
                          wkv[:, D_MAIN:2 * D_MAIN].reshape(D_MODEL, FOX_PAIRS, LANES)], axis=2).reshape(D_MODEL, 2 * D_MAIN)
        return win1, wout1, wkvp, jnp.pad(wkv[:, 2 * D_MAIN:], ((0, 0), (0, LANES - FOX_HEADS)))

    early = {}

    def send_early(g):
        srcs = [g[n] for n in EARLY]
        lands = [jax.ShapeDtypeStruct((N_DEV, 2) + w[n].shape[1:], g[n].dtype) for n in EARLY[:3]]
        lands.append(jax.ShapeDtypeStruct(g["w_kv_shared"].shape, g["w_kv_shared"].dtype))
        early["state"], anchor, _ = _exchange_start(srcs, EARLY_ITEMS, lands, name="exchange_early_start")
        return anchor

    sq, gx, grads = _local_step(x[0], mem[0], loss_target[0], win0, wmkv, wout0, pw, pscale, late_weights,
                                ln_g, ln_b, bias, send_early)
    loss = lax.psum((0.5 / D_MODEL) * jnp.sum(sq), ("x", "y", "c"))

    small = jnp.concatenate([grads["pool_scale"].reshape(N_DEV, -1)]
                            + [jnp.broadcast_to(grads[n].reshape(1, -1), (N_DEV, grads[n].size)) for n in SMALL[1:]], axis=1)
    small = jnp.pad(small, ((0, 0), (0, SMALL_ROWS * LANES - small.shape[1]))).reshape(N_DEV, SMALL_ROWS, LANES)
    r_out, r_mkv, r_in, r_kv = _exchange_wait(early["state"], EARLY_ITEMS, len(EARLY), gx, name="exchange_early_wait")
    late = [grads[n][0] for n in EARLY[:3]] + [grads["pool_w"][0], small]
    late_items = [(cut, i, 0) for i, (cut, _, _) in enumerate(EARLY_ITEMS[:3])] + [(PER_LAYER_CUT["pool_w"], 3, None),
                                                                                  (None, 4, None)]
    r_out, r_mkv, r_in, r_pw, r_small = _exchange(
        late, late_items,
        [r_out, r_mkv, r_in, jax.ShapeDtypeStruct((N_DEV,) + w["pool_w"].shape, late[3].dtype),
         jax.ShapeDtypeStruct(small.shape, small.dtype)], name="exchange_grads")
    recv = dict(w_out=r_out, w_mem_kv=r_mkv, w_in=r_in, pool_w=r_pw, w_kv_shared=r_kv, small=r_small)

    outs = {}
    for n in BIG:
        res = _adamw(recv[n], w[n], m[n], v[n], split=ADAM_SPLIT[n], name="adamw_" + n)
        for kind, a in zip(("grad", "delta", "new_m", "new_v"), res):
            outs[kind, n] = a[None] if n == "pool_w" else a
    small_shapes = [w[n].shape for n in SMALL]
    res = _adamw(recv["small"], _flat([w[n] for n in SMALL], SMALL_ROWS), _flat([m[n] for n in SMALL], SMALL_ROWS),
                 _flat([v[n] for n in SMALL], SMALL_ROWS), split=(0, 1), name="adamw_small")
    for kind, flat in zip(("grad", "delta", "new_m", "new_v"), res):
        for n, a in zip(SMALL, _unflat(flat, small_shapes)):
            outs[kind, n] = a
    order = ("w_in", "w_mem_kv", "w_out", "ln_g", "ln_b", "pool_w", "pool_scale", "w_kv_shared", "b_forget")
    return (loss, gx[None], *[outs[kind, n] for kind in ("grad", "delta", "new_m", "new_v") for n in order])
```

```python
import math

import numpy as np
import jax
import jax.numpy as jnp
from jax import lax
from jax.experimental import pallas as pl
from jax.experimental.pallas import tpu as pltpu

F32 = jnp.float32
BF16 = jnp.bfloat16

D_MODEL = 1024
D_MAIN = 1024
D_MEM = 512
D_MIX = D_MAIN + D_MEM
D_IN = 2 * D_MIX
N_MEM = 256
MEM_HEADS = 4
MEM_HEAD_DIM = 128
FOX_HEADS = 16
FOX_HEAD_DIM = 64
FOX_PAIRS = FOX_HEADS // 2
POOL_WINDOWS = (2, 4, 8, 16)
POOL_GROUP = 256
POOL_HALO = 16
ALPHA = 4.0 ** 0.25
LN_EPS = 1e-5
NEG = -1e30
LANES = 128
N_DEV = 8

ADAM_LR = 0.001
ADAM_B1 = 0.9
ADAM_B2 = 0.999
ADAM_EPS = 1e-08
ADAM_WD = 0.01
ADAM_STEP = 10

VMEM_LIMIT = 56 * 1024 * 1024

NN = (((1,), (0,)), ((), ()))
NT = (((1,), (1,)), ((), ()))
TN = (((0,), (0,)), ((), ()))


def _cparams(*sem):
    return pltpu.CompilerParams(dimension_semantics=sem, vmem_limit_bytes=VMEM_LIMIT)


def _sigmoid(z):
    return 1.0 / (1.0 + jnp.exp(-z))


def _mm(a, b, *, mode, out_dtype, tm, tn, tk, name, add=None, add_scale=1.0):
    if mode == "nn":
        (M, K), (K2, N) = a.shape, b.shape
    elif mode == "nt":
        (M, K), (N, K2) = a.shape, b.shape
    else:
        (K, M), (K2, N) = a.shape, b.shape
    assert K == K2, (a.shape, b.shape, mode)
    tm, tn, tk = min(tm, M), min(tn, N), min(tk, K)
    assert M % tm == 0 and N % tn == 0 and K % tk == 0, (M, N, K, tm, tn, tk)
    gm, gn, gk = M // tm, N // tn, K // tk
    dims = {"nn": NN, "nt": NT, "tn": TN}[mode]
    if mode == "tn":
        a_spec = pl.BlockSpec((tk, tm), lambda i, j, k: (k, i))
    else:
        a_spec = pl.BlockSpec((tm, tk), lambda i, j, k: (i, k))
    if mode == "nt":
        b_spec = pl.BlockSpec((tn, tk), lambda i, j, k: (j, k))
    else:
        b_spec = pl.BlockSpec((tk, tn), lambda i, j, k: (k, j))
    o_spec = pl.BlockSpec((tm, tn), lambda i, j, k: (i, j))
    has_add = add is not None
    acc_in_out = out_dtype == F32

    def body(*refs):
        a_ref, b_ref = refs[0], refs[1]
        add_ref = refs[2] if has_add else None
        o_ref = refs[3] if has_add else refs[2]
        prod = lax.dot_general(a_ref[...].astype(BF16), b_ref[...].astype(BF16), dims,
                               preferred_element_type=F32)

        def finish(r):
            if has_add:
                r = r + add_scale * add_ref[...]
            o_ref[...] = r.astype(out_dtype)

        if gk == 1:
            finish(prod)
        else:
            acc_ref = o_ref if acc_in_out else refs[-1]
            k = pl.program_id(2)

            @pl.when(k == 0)
            def _():
                acc_ref[...] = prod

            @pl.when(k > 0)
            def _():
                acc_ref[...] += prod

            if has_add or not acc_in_out:
                @pl.when(k == gk - 1)
                def _():
                    finish(acc_ref[...])

    in_specs = [a_spec, b_spec] + ([o_spec] if has_add else [])
    args = (a, b) + ((add,) if has_add else ())
    return pl.pallas_call(
        body, name=name, grid=(gm, gn, gk), in_specs=in_specs, out_specs=o_spec,
        out_shape=jax.ShapeDtypeStruct((M, N), out_dtype),
        scratch_shapes=[pltpu.VMEM((tm, tn), F32)] if gk > 1 and not acc_in_out else [],
        compiler_params=_cparams("parallel", "parallel", "arbitrary"),
    )(*args)


def _ln_stats(z):
    mu = jnp.mean(z, axis=1, keepdims=True)
    zc = z - mu
    var = jnp.mean(zc * zc, axis=1, keepdims=True)
    rstd = lax.rsqrt(var + LN_EPS)
    return zc * rstd, rstd


def _ln_bwd_math(dy, xhat, rstd, g):
    dxh = dy * g
    m1 = jnp.mean(dxh, axis=1, keepdims=True)
    m2 = jnp.mean(dxh * xhat, axis=1, keepdims=True)
    return rstd * (dxh - m1 - xhat * m2)


def _out_ln(yg, wout, x, g, b, *, tb, name):
    S = x.shape[0]
    tb = min(tb, S)

    def body(yg_ref, w_ref, x_ref, g_ref, b_ref, y_ref, xhat_ref, rstd_ref):
        o = jnp.dot(yg_ref[...], w_ref[...], preferred_element_type=F32)
        xhat, rstd = _ln_stats(ALPHA * x_ref[...] + o)
        xhat_ref[...] = xhat
        rstd_ref[...] = rstd
        y_ref[...] = xhat * g_ref[...] + b_ref[...]

    row = pl.BlockSpec((tb, D_MODEL), lambda i: (i, 0))
    vec = pl.BlockSpec((1, D_MODEL), lambda i: (0, 0))
    return pl.pallas_call(
        body, name=name, grid=(S // tb,),
        in_specs=[pl.BlockSpec((tb, D_MIX), lambda i: (i, 0)), pl.BlockSpec((D_MIX, D_MODEL), lambda i: (0, 0)),
                  row, vec, vec],
        out_specs=[row, row, pl.BlockSpec((tb, 1), lambda i: (i, 0))],
        out_shape=[jax.ShapeDtypeStruct((S, D_MODEL), F32), jax.ShapeDtypeStruct((S, D_MODEL), F32),
                   jax.ShapeDtypeStruct((S, 1), F32)],
        compiler_params=_cparams("parallel"),
    )(yg, wout, x, g, b)


def _out_ln_loss(yg, wout, x, g, b, target, *, tb, name):
    S = x.shape[0]
    tb = min(tb, S)

    def body(yg_ref, w_ref, x_ref, g_ref, b_ref, t_ref, dz_ref, dg_ref, db_ref, sq_ref):
        @pl.when(pl.program_id(0) == 0)
        def _():
            dg_ref[...] = jnp.zeros_like(dg_ref)
            db_ref[...] = jnp.zeros_like(db_ref)
            sq_ref[...] = jnp.zeros_like(sq_ref)

        o = jnp.dot(yg_ref[...], w_ref[...], preferred_element_type=F32)
        xhat, rstd = _ln_stats(ALPHA * x_ref[...] + o)
        err = xhat * g_ref[...] + b_ref[...] - t_ref[...]
        sq_ref[...] += jnp.sum(err * err, axis=0, keepdims=True)
        dy = err * (1.0 / D_MODEL)
        dz_ref[...] = _ln_bwd_math(dy, xhat, rstd, g_ref[...])
        dg_ref[...] += jnp.sum(dy * xhat, axis=0, keepdims=True)
        db_ref[...] += jnp.sum(dy, axis=0, keepdims=True)

    row = pl.BlockSpec((tb, D_MODEL), lambda i: (i, 0))
    vec = pl.BlockSpec((1, D_MODEL), lambda i: (0, 0))
    vshape = jax.ShapeDtypeStruct((1, D_MODEL), F32)
    return pl.pallas_call(
        body, name=name, grid=(S // tb,),
        in_specs=[pl.BlockSpec((tb, D_MIX), lambda i: (i, 0)), pl.BlockSpec((D_MIX, D_MODEL), lambda i: (0, 0)),
                  row, vec, vec, row],
        out_specs=[row, vec, vec, vec],
        out_shape=[jax.ShapeDtypeStruct((S, D_MODEL), F32), vshape, vshape, vshape],
        compiler_params=_cparams("arbitrary"),
    )(yg, wout, x, g, b, target)


def _ln_bwd(dy, xhat, rstd, g, *, tb, name):
    S = dy.shape[0]
    tb = min(tb, S)

    def body(dy_ref, xhat_ref, rstd_ref, g_ref, dz_ref, dg_ref, db_ref):
        @pl.when(pl.program_id(0) == 0)
        def _():
            dg_ref[...] = jnp.zeros_like(dg_ref)
            db_ref[...] = jnp.zeros_like(db_ref)

        dy_, xhat_ = dy_ref[...], xhat_ref[...]
        dz_ref[...] = _ln_bwd_math(dy_, xhat_, rstd_ref[...], g_ref[...])
        dg_ref[...] += jnp.sum(dy_ * xhat_, axis=0, keepdims=True)
        db_ref[...] += jnp.sum(dy_, axis=0, keepdims=True)

    row = pl.BlockSpec((tb, D_MODEL), lambda i: (i, 0))
    vec = pl.BlockSpec((1, D_MODEL), lambda i: (0, 0))
    return pl.pallas_call(
        body, name=name, grid=(S // tb,),
        in_specs=[row, row, pl.BlockSpec((tb, 1), lambda i: (i, 0)), vec],
        out_specs=[row, vec, vec],
        out_shape=[jax.ShapeDtypeStruct((S, D_MODEL), F32), jax.ShapeDtypeStruct((1, D_MODEL), F32),
                   jax.ShapeDtypeStruct((1, D_MODEL), F32)],
        compiler_params=_cparams("arbitrary"),
    )(dy, xhat, rstd, g)


def _gate_fwd(ysrc, scale, h, ymem, *, tb, name):
    S = ysrc.shape[0]
    tb = min(tb, S)

    def body(ys_ref, sc_ref, ga_ref, gb_ref, gc_ref, ym_ref, yg_ref):
        ymain = ys_ref[...] * sc_ref[...]
        for k, g_ref in enumerate((ga_ref, gb_ref)):
            gv = g_ref[...].astype(F32)
            yg_ref[:, 512 * k:512 * (k + 1)] = (ymain[:, 512 * k:512 * (k + 1)] * gv * _sigmoid(gv)).astype(BF16)
        gv = gc_ref[...].astype(F32)
        yg_ref[:, 1024:1536] = (ym_ref[...] * gv * _sigmoid(gv)).astype(BF16)

    slab = lambda c: pl.BlockSpec((tb, 512), lambda i, c=c: (i, c))
    return pl.pallas_call(
        body, name=name, grid=(S // tb,),
        in_specs=[pl.BlockSpec((tb, D_MAIN), lambda i: (i, 0)), pl.BlockSpec((1, D_MAIN), lambda i: (0, 0)),
                  slab(3), slab(4), slab(5), pl.BlockSpec((tb, D_MEM), lambda i: (i, 0))],
        out_specs=pl.BlockSpec((tb, D_MIX), lambda i: (i, 0)),
        out_shape=jax.ShapeDtypeStruct((S, D_MIX), BF16),
        compiler_params=_cparams("parallel"),
    )(ysrc, scale, h, h, h, ymem)


def _gate_bwd(dyg, ysrc, scale, h, ymem, *, tb, name):
    S = ysrc.shape[0]
    tb = min(tb, S)

    def dsilu(gv):
        sg = _sigmoid(gv)
        return sg, sg * (1.0 + gv * (1.0 - sg))

    def body(da_ref, db_ref, dc_ref, ys_ref, sc_ref, ga_ref, gb_ref, gc_ref, ym_ref, dym_ref, dymem_ref, dh_ref):
        ymain = ys_ref[...] * sc_ref[...]
        for k, (d_ref, g_ref) in enumerate(((da_ref, ga_ref), (db_ref, gb_ref))):
            gv, d = g_ref[...].astype(F32), d_ref[...].astype(F32)
            sg, ds = dsilu(gv)
            dym_ref[:, 512 * k:512 * (k + 1)] = d * gv * sg
            dh_ref[:, 512 * k:512 * (k + 1)] = (d * ymain[:, 512 * k:512 * (k + 1)] * ds).astype(BF16)
        gv, d = gc_ref[...].astype(F32), dc_ref[...].astype(F32)
        sg, ds = dsilu(gv)
        dymem_ref[...] = d * gv * sg
        dh_ref[:, 1024:1536] = (d * ym_ref[...] * ds).astype(BF16)

    slab = lambda c: pl.BlockSpec((tb, 512), lambda i, c=c: (i, c))
    return pl.pallas_call(
        body, name=name, grid=(S // tb,),
        in_specs=[slab(0), slab(1), slab(2),
                  pl.BlockSpec((tb, D_MAIN), lambda i: (i, 0)), pl.BlockSpec((1, D_MAIN), lambda i: (0, 0)),
                  slab(3), slab(4), slab(5), pl.BlockSpec((tb, D_MEM), lambda i: (i, 0))],
        out_specs=[pl.BlockSpec((tb, D_MAIN), lambda i: (i, 0)), pl.BlockSpec((tb, D_MEM), lambda i: (i, 0)),
                   pl.BlockSpec((tb, D_MIX), lambda i: (i, 1))],
        out_shape=[jax.ShapeDtypeStruct((S, D_MAIN), F32), jax.ShapeDtypeStruct((S, D_MEM), F32),
                   jax.ShapeDtypeStruct((S, D_IN), BF16)],
        compiler_params=_cparams("parallel"),
    )(dyg, dyg, dyg, ysrc, scale, h, h, h, ymem)


def _window_count(t0, rows, w):
    t = t0 + lax.broadcasted_iota(jnp.int32, (rows, POOL_GROUP), 0)
    return jnp.minimum(t + 1, w).astype(F32)


def _pool_fwd(h, pw, *, tb, name):
    S = h.shape[0]
    tb = min(tb, S)

    def body(u_ref, pw_ref, pm_ref, mixed_ref, tail_ref):
        i = pl.program_id(0)

        @pl.when(i == 0)
        def _():
            tail_ref[...] = jnp.zeros_like(tail_ref)

        u = u_ref[...].astype(F32)
        xfull = jnp.concatenate([tail_ref[...], u], axis=0)
        for gi, w in enumerate(POOL_WINDOWS):
            cols = slice(POOL_GROUP * gi, POOL_GROUP * (gi + 1))
            s = xfull[:, cols]
            sh = 1
            while sh < w:
                s = s + pltpu.roll(s, sh, 0)
                sh *= 2
            pm = s[POOL_HALO:, :] / _window_count(i * tb, tb, w) - u[:, cols]
            pmb = pm.astype(BF16)
            pm_ref[:, cols] = pmb
            mixed_ref[:, cols] = jnp.dot(pmb, pw_ref[gi], preferred_element_type=F32)
        tail_ref[...] = u[tb - POOL_HALO:, :]

    return pl.pallas_call(
        body, name=name, grid=(S // tb,),
        in_specs=[pl.BlockSpec((tb, D_MAIN), lambda i: (i, 0)),
                  pl.BlockSpec((4, POOL_GROUP, POOL_GROUP), lambda i: (0, 0, 0))],
        out_specs=[pl.BlockSpec((tb, D_MAIN), lambda i: (i, 0)), pl.BlockSpec((tb, D_MAIN), lambda i: (i, 0))],
        out_shape=[jax.ShapeDtypeStruct((S, D_MAIN), BF16), jax.ShapeDtypeStruct((S, D_MAIN), F32)],
        scratch_shapes=[pltpu.VMEM((POOL_HALO, D_MAIN), F32)],
        compiler_params=_cparams("arbitrary"),
    )(h, pw)


def _pool_bwd(dymain, pm, mixed, pw, scale, dh, *, tb, name):
    S = dymain.shape[0]
    tb = min(tb, S)
    nb = S // tb
    n = tb + POOL_HALO

    def body(dy_ref, pm_ref, mixed_ref, pw_ref, sc_ref, dh_in, dh_ref, dpw_ref, dsc_ref, head_ref, dpw_acc):
        del dh_in
        i = pl.program_id(0)

        @pl.when(i == 0)
        def _():
            head_ref[...] = jnp.zeros_like(head_ref)
            dpw_acc[...] = jnp.zeros_like(dpw_acc)
            dsc_ref[...] = jnp.zeros_like(dsc_ref)

        dy = dy_ref[...]
        dsc_ref[...] += jnp.sum(dy * mixed_ref[...], axis=0, keepdims=True)
        dmixed = dy * sc_ref[...]
        t0 = (nb - 1 - i) * tb
        for gi, w in enumerate(POOL_WINDOWS):
            cols = slice(POOL_GROUP * gi, POOL_GROUP * (gi + 1))
            dm = dmixed[:, cols].astype(BF16)
            dpw_acc[gi] += lax.dot_general(pm_ref[:, cols], dm, TN, preferred_element_type=F32)
            dpm = lax.dot_general(dm, pw_ref[gi], NT, preferred_element_type=F32)
            e = dpm / _window_count(t0, tb, w)
            s = jnp.concatenate([e, head_ref[:, cols]], axis=0)
            sh = 1
            while sh < w:
                s = s + pltpu.roll(s, n - sh, 0)
                sh *= 2
            dh_ref[:, cols] = (s[:tb, :] - dpm).astype(BF16)
            head_ref[:, cols] = e[:POOL_HALO, :]

        @pl.when(i == nb - 1)
        def _():
            dpw_ref[...] = dpw_acc[...].astype(BF16)

    rev = lambda i: (nb - 1 - i, 0)
    return pl.pallas_call(
        body, name=name, grid=(nb,),
        in_specs=[pl.BlockSpec((tb, D_MAIN), rev), pl.BlockSpec((tb, D_MAIN), rev), pl.BlockSpec((tb, D_MAIN), rev),
                  pl.BlockSpec((4, POOL_GROUP, POOL_GROUP), lambda i: (0, 0, 0)),
                  pl.BlockSpec((1, D_MAIN), lambda i: (0, 0)), pl.BlockSpec(memory_space=pl.ANY)],
        out_specs=[pl.BlockSpec((tb, D_MAIN), rev),
                   pl.BlockSpec((4, POOL_GROUP, POOL_GROUP), lambda i: (0, 0, 0)),
                   pl.BlockSpec((1, D_MAIN), lambda i: (0, 0))],
        out_shape=[jax.ShapeDtypeStruct(dh.shape, dh.dtype),
                   jax.ShapeDtypeStruct((4, POOL_GROUP, POOL_GROUP), BF16), jax.ShapeDtypeStruct((1, D_MAIN), F32)],
        scratch_shapes=[pltpu.VMEM((POOL_HALO, D_MAIN), F32), pltpu.VMEM((4, POOL_GROUP, POOL_GROUP), F32)],
        input_output_aliases={5: 0},
        compiler_params=_cparams("arbitrary"),
    )(dymain, pm, mixed, pw, scale, dh)


MEM_SCALE = MEM_HEAD_DIM ** -0.5


def _mem_probs(q_ref, mkv_ref, hd):
    cols = slice(MEM_HEAD_DIM * hd, MEM_HEAD_DIM * (hd + 1))
    q = (q_ref[:, cols].astype(F32) * MEM_SCALE).astype(BF16)
    mk = mkv_ref[:, cols].astype(BF16)
    mv = mkv_ref[:, D_MEM + MEM_HEAD_DIM * hd:D_MEM + MEM_HEAD_DIM * (hd + 1)].astype(BF16)
    s = lax.dot_general(q, mk, NT, preferred_element_type=F32)
    e = jnp.exp(s - jnp.max(s, axis=1, keepdims=True))
    return cols, q, mk, mv, e, jnp.sum(e, axis=1, keepdims=True)


def _memattn_fwd(h, mkv, *, tb, name):
    S = h.shape[0]
    tb = min(tb, S)

    def body(q_ref, mkv_ref, y_ref):
        for hd in range(MEM_HEADS):
            cols, _, _, mv, e, l = _mem_probs(q_ref, mkv_ref, hd)
            y_ref[:, cols] = jnp.dot(e.astype(BF16), mv, preferred_element_type=F32) / l

    return pl.pallas_call(
        body, name=name, grid=(S // tb,),
        in_specs=[pl.BlockSpec((tb, D_MEM), lambda i: (i, 2)), pl.BlockSpec((N_MEM, 2 * D_MEM), lambda i: (0, 0))],
        out_specs=pl.BlockSpec((tb, D_MEM), lambda i: (i, 0)),
        out_shape=jax.ShapeDtypeStruct((S, D_MEM), F32),
        compiler_params=_cparams("parallel"),
    )(h, mkv)


def _memattn_bwd(h, mkv, dy, dh, *, tb, name):
    S = h.shape[0]
    tb = min(tb, S)

    def body(q_ref, mkv_ref, dy_ref, dh_in, dh_ref, dmkv_ref):
        del dh_in

        @pl.when(pl.program_id(0) == 0)
        def _():
            dmkv_ref[...] = jnp.zeros_like(dmkv_ref)

        for hd in range(MEM_HEADS):
            cols, q, mk, mv, e, l = _mem_probs(q_ref, mkv_ref, hd)
            p = e / l
            dyh = dy_ref[:, cols].astype(BF16)
            dp = lax.dot_general(dyh, mv, NT, preferred_element_type=F32)
            ds = p * (dp - jnp.sum(dp * p, axis=1, keepdims=True))
            dsb = ds.astype(BF16)
            dh_ref[:, cols] = (jnp.dot(dsb, mk, preferred_element_type=F32) * MEM_SCALE).astype(BF16)
            dmkv_ref[:, cols] += lax.dot_general(dsb, q, TN, preferred_element_type=F32)
            vcols = slice(D_MEM + MEM_HEAD_DIM * hd, D_MEM + MEM_HEAD_DIM * (hd + 1))
            dmkv_ref[:, vcols] += lax.dot_general(p.astype(BF16), dyh, TN, preferred_element_type=F32)

    return pl.pallas_call(
        body, name=name, grid=(S // tb,),
        in_specs=[pl.BlockSpec((tb, D_MEM), lambda i: (i, 2)), pl.BlockSpec((N_MEM, 2 * D_MEM), lambda i: (0, 0)),
                  pl.BlockSpec((tb, D_MEM), lambda i: (i, 0)), pl.BlockSpec(memory_space=pl.ANY)],
        out_specs=[pl.BlockSpec((tb, D_MEM), lambda i: (i, 2)), pl.BlockSpec((N_MEM, 2 * D_MEM), lambda i: (0, 0))],
        out_shape=[jax.ShapeDtypeStruct(dh.shape, dh.dtype), jax.ShapeDtypeStruct((N_MEM, 2 * D_MEM), F32)],
        input_output_aliases={3: 0},
        compiler_params=_cparams("arbitrary"),
    )(h, mkv, dy, dh)


def _forget_fwd(fl, bias, *, tb, name):
    S = fl.shape[0]
    tb = min(tb, S)

    def body(fl_ref, b_ref, o_ref, carry_ref):
        @pl.when(pl.program_id(0) == 0)
        def _():
            carry_ref[...] = jnp.zeros_like(carry_ref)

        z = fl_ref[...] + b_ref[...]
        lf = jnp.minimum(z, 0.0) - jnp.log(1.0 + jnp.exp(-jnp.abs(z)))
        row = lax.broadcasted_iota(jnp.int32, (tb, LANES), 0)
        c = lf
        sh = 1
        while sh < tb:
            c = c + jnp.where(row >= sh, pltpu.roll(c, sh, 0), 0.0)
            sh *= 2
        o_ref[...] = -(carry_ref[...] + c)
        carry_ref[...] += jnp.sum(lf, axis=0, keepdims=True)

    return pl.pallas_call(
        body, name=name, grid=(S // tb,),
        in_specs=[pl.BlockSpec((tb, LANES), lambda i: (i, 0)), pl.BlockSpec((1, LANES), lambda i: (0, 0))],
        out_specs=pl.BlockSpec((tb, LANES), lambda i: (i, 0)),
        out_shape=jax.ShapeDtypeStruct((S, LANES), F32),
        scratch_shapes=[pltpu.VMEM((1, LANES), F32)],
        compiler_params=_cparams("arbitrary"),
    )(fl, bias)


def _forget_bwd(dn, drow, fl, bias, dh2, *, tb, name):
    S = fl.shape[0]
    tb = min(tb, S)
    nb = S // tb

    def body(dn_ref, dr_ref, fl_ref, b_ref, dh_in, dh_ref, db_ref, carry_ref):
        del dh_in

        @pl.when(pl.program_id(0) == 0)
        def _():
            carry_ref[...] = jnp.zeros_like(carry_ref)
            db_ref[...] = jnp.zeros_like(db_ref)

        src = lax.broadcasted_iota(jnp.int32, (D_MAIN, LANES), 0)
        head = lax.broadcasted_iota(jnp.int32, (D_MAIN, LANES), 1)
        pick = lambda off: jnp.where((src == FOX_HEAD_DIM * head + off) & (head < FOX_HEADS), 1.0, 0.0).astype(BF16)
        hdot = lambda a, sel: sum(jnp.dot(part.astype(BF16), sel, preferred_element_type=F32) for part in _split3(a))
        dcum = hdot(dr_ref[...], pick(3)) - hdot(dn_ref[...], pick(0))
        row = lax.broadcasted_iota(jnp.int32, (tb, LANES), 0)
        c = dcum
        sh = 1
        while sh < tb:
            c = c + jnp.where(row < tb - sh, pltpu.roll(c, tb - sh, 0), 0.0)
            sh *= 2
        dlf = carry_ref[...] + c
        carry_ref[...] += jnp.sum(dcum, axis=0, keepdims=True)
        z = fl_ref[...] + b_ref[...]
        lane = lax.broadcasted_iota(jnp.int32, (tb, LANES), 1)
        dfl = jnp.where(lane < FOX_HEADS, dlf / (1.0 + jnp.exp(z)), 0.0)
        db_ref[...] += jnp.sum(dfl, axis=0, keepdims=True)
        dh_ref[...] = dfl.astype(BF16)

    rev = lambda i: (nb - 1 - i, 0)
    return pl.pallas_call(
        body, name=name, grid=(nb,),
        in_specs=[pl.BlockSpec((tb, D_MAIN), rev), pl.BlockSpec((tb, D_MAIN), rev), pl.BlockSpec((tb, LANES), rev),
                  pl.BlockSpec((1, LANES), lambda i: (0, 0)), pl.BlockSpec(memory_space=pl.ANY)],
        out_specs=[pl.BlockSpec((tb, LANES), lambda i: (nb - 1 - i, 2 * D_MAIN // LANES)),
                   pl.BlockSpec((1, LANES), lambda i: (0, 0))],
        out_shape=[jax.ShapeDtypeStruct(dh2.shape, dh2.dtype), jax.ShapeDtypeStruct((1, LANES), F32)],
        scratch_shapes=[pltpu.VMEM((1, LANES), F32)],
        input_output_aliases={4: 0},
        compiler_params=_cparams("arbitrary"),
    )(dn, drow, fl, bias, dh2)


FOX_SCALE = FOX_HEAD_DIM ** -0.5
LOG2E = 1.4426950408889634
LN2 = 0.6931471805599453
AUX = FOX_HEAD_DIM


def _split3(x):
    hi = x.astype(BF16).astype(F32)
    r = x - hi
    mid = r.astype(BF16).astype(F32)
    return hi, mid, (r - mid).astype(BF16).astype(F32)


def _lanes3(lane, base, parts, rest):
    return jnp.where(lane == base, parts[0], jnp.where(lane == base + 1, parts[1],
                                                       jnp.where(lane == base + 2, parts[2], rest)))


def _swap_halves(x):
    return pltpu.roll(x, FOX_HEAD_DIM, 1)


def _causal_steps(nq, keys_outer):
    if keys_outer:
        pairs = [(i, j) for j in range(nq) for i in range(j, nq)]
    else:
        pairs = [(i, j) for i in range(nq) for j in range(i + 1)]
    it, jt = zip(*pairs)
    return jnp.asarray(np.array(it, np.int32)), jnp.asarray(np.array(jt, np.int32))


def _fox_prep_fwd(h, kv, negcum, *, tb, name):
    S = h.shape[0]
    tb = min(tb, S)

    def body(q_ref, kv_ref, nc_ref, qa_ref, ka_ref, va_ref):
        g = pl.program_id(0)
        lane = lax.broadcasted_iota(jnp.int32, (tb, LANES), 1)
        first = lane < FOX_HEAD_DIM
        q = q_ref[...].astype(F32) * (FOX_SCALE * LOG2E)
        k = kv_ref[:, :LANES].astype(F32)
        v = kv_ref[:, LANES:].astype(F32)
        nc = nc_ref[...]
        ones_q = jnp.where((lane >= AUX) & (lane < AUX + 3), 1.0, 0.0)
        ones_k = jnp.where((lane >= AUX + 3) & (lane < AUX + 6), 1.0, 0.0)
        for hh in range(2):
            sl = slice(LANES * hh, LANES * (hh + 1))
            qh, kh, vh = (q, k, v) if hh == 0 else (_swap_halves(q), _swap_halves(k), _swap_halves(v))
            ncol = jnp.sum(jnp.where(lane == 2 * g + hh, nc, 0.0), axis=1, keepdims=True) * LOG2E
            qa_ref[:, sl] = jnp.where(first, qh, ones_q).astype(BF16)
            ka_ref[:, sl] = jnp.where(first, kh, _lanes3(lane, AUX, _split3(ncol), ones_k)).astype(BF16)
            va_ref[:, sl] = jnp.where(first, vh, 1.0).astype(BF16)

    pair = pl.BlockSpec((tb, 2 * LANES), lambda g, i: (i, g))
    shp = jax.ShapeDtypeStruct((S, 2 * D_MAIN), BF16)
    return pl.pallas_call(
        body, name=name, grid=(FOX_PAIRS, S // tb),
        in_specs=[pl.BlockSpec((tb, LANES), lambda g, i: (i, g)), pair, pl.BlockSpec((tb, LANES), lambda g, i: (i, 0))],
        out_specs=[pair, pair, pair], out_shape=[shp, shp, shp],
        compiler_params=_cparams("parallel", "parallel"),
    )(h, kv, negcum)


def _fox_fwd(qa, ka, va, *, tq, name):
    S = qa.shape[0]
    tq = min(tq, S)
    nq = S // tq
    half = tq // 2
    it, jt = _causal_steps(nq, keys_outer=False)

    def body(it_ref, jt_ref, qa_ref, ka_ref, va_ref, y_ref, lse_ref, m_ref, acc_ref):
        n = pl.program_id(1)
        i, j = it_ref[n], jt_ref[n]
        first = lax.broadcasted_iota(jnp.int32, (tq, LANES), 1) < FOX_HEAD_DIM

        @pl.when(j == 0)
        def _():
            m_ref[...] = jnp.full_like(m_ref, NEG)
            acc_ref[...] = jnp.zeros_like(acc_ref)

        def update(hh, rows, nk, masked):
            sl = slice(LANES * hh, LANES * (hh + 1))
            s = lax.dot_general(qa_ref[rows, sl], ka_ref[0:nk, sl], NT, preferred_element_type=F32)
            if masked:
                r = lax.broadcasted_iota(jnp.int32, s.shape, 0) + rows.start
                c = lax.broadcasted_iota(jnp.int32, s.shape, 1)
                s = jnp.where(c <= r, s, NEG)
            m_prev = m_ref[hh, rows]
            m_new = jnp.maximum(m_prev, jnp.max(s, axis=1, keepdims=True))
            p = jnp.exp2(s - jnp.tile(m_new, (1, nk // LANES))).astype(BF16)
            acc_ref[hh, rows] = jnp.exp2(m_prev - m_new) * acc_ref[hh, rows] + jnp.dot(
                p, va_ref[0:nk, sl], preferred_element_type=F32)
            m_ref[hh, rows] = m_new

        @pl.when(j < i)
        def _():
            for hh in range(2):
                update(hh, slice(0, tq), tq, False)

        @pl.when(j == i)
        def _():
            for hh in range(2):
                for r0 in range(0, tq, half):
                    update(hh, slice(r0, r0 + half), r0 + half, True)
            ys, lses = [], []
            for hh in range(2):
                a = acc_ref[hh]
                denom = _swap_halves(a)
                ys.append(a / denom)
                lses.append(m_ref[hh] + jnp.log(jnp.where(first, denom, a)) * LOG2E)
            y_ref[...] = jnp.where(first, ys[0], _swap_halves(ys[1]))
            lse_ref[...] = jnp.where(first, lses[0], lses[1])

    qblock = pl.BlockSpec((tq, 2 * LANES), lambda g, n, it, jt: (it[n], g))
    kblock = pl.BlockSpec((tq, 2 * LANES), lambda g, n, it, jt: (jt[n], g))
    out = pl.BlockSpec((tq, LANES), lambda g, n, it, jt: (it[n], g))
    return pl.pallas_call(
        body, name=name,
        grid_spec=pltpu.PrefetchScalarGridSpec(
            num_scalar_prefetch=2, grid=(FOX_PAIRS, it.shape[0]),
            in_specs=[qblock, kblock, kblock], out_specs=[out, out],
            scratch_shapes=[pltpu.VMEM((2, tq, LANES), F32), pltpu.VMEM((2, tq, LANES), F32)]),
        out_shape=[jax.ShapeDtypeStruct((S, D_MAIN), F32), jax.ShapeDtypeStruct((S, D_MAIN), F32)],
        compiler_params=_cparams("parallel", "arbitrary"),
    )(it, jt, qa, ka, va)


def _fox_prep_bwd(qa, lse2, dy, y, *, tb, name):
    S = qa.shape[0]
    tb = min(tb, S)

    def body(qa_ref, lse_ref, dy_ref, y_ref, qb_ref, dya_ref):
        lane = lax.broadcasted_iota(jnp.int32, (tb, LANES), 1)
        first = lane < FOX_HEAD_DIM
        lse = lse_ref[...]
        lse_sw = _swap_halves(lse)
        dy = dy_ref[...]
        prod = dy * y_ref[...]
        for hh in range(2):
            sl = slice(LANES * hh, LANES * (hh + 1))
            lse_h = jnp.where(first, lse, lse_sw) if hh == 0 else jnp.where(first, lse_sw, lse)
            qb_ref[:, sl] = _lanes3(lane, AUX + 3, _split3(-lse_h), qa_ref[:, sl].astype(F32)).astype(BF16)
            delta = jnp.sum(jnp.where(first == (hh == 0), prod, 0.0), axis=1, keepdims=True)
            dyh = dy if hh == 0 else _swap_halves(dy)
            dya_ref[:, sl] = jnp.where(first, dyh, _lanes3(lane, AUX, _split3(-delta), 0.0)).astype(BF16)

    pair = pl.BlockSpec((tb, 2 * LANES), lambda g, i: (i, g))
    one = pl.BlockSpec((tb, LANES), lambda g, i: (i, g))
    shp = jax.ShapeDtypeStruct((S, 2 * D_MAIN), BF16)
    return pl.pallas_call(
        body, name=name, grid=(FOX_PAIRS, S // tb),
        in_specs=[pair, one, one, one], out_specs=[pair, pair], out_shape=[shp, shp],
        compiler_params=_cparams("parallel", "parallel"),
    )(qa, lse2, dy, y)


def _fox_bwd(qb, ka, va, dya, dh, *, tq, name):
    S = qb.shape[0]
    tq = min(tq, S)
    nq = S // tq
    half = tq // 2
    it, jt = _causal_steps(nq, keys_outer=True)
    nsteps = it.shape[0]

    def body(it_ref, jt_ref, qb_ref, ka_ref, va_ref, dya_ref, dh_in, dq_ref, dkv_ref, dn_ref, drow_ref,
             dq_acc, dk_acc, dv_acc):
        del dh_in
        n = pl.program_id(1)
        i, j = it_ref[n], jt_ref[n]
        first = lax.broadcasted_iota(jnp.int32, (tq, LANES), 1) < FOX_HEAD_DIM

        @pl.when(n == 0)
        def _():
            dq_acc[...] = jnp.zeros_like(dq_acc)

        @pl.when(i == j)
        def _():
            dk_acc[...] = jnp.zeros_like(dk_acc)
            dv_acc[...] = jnp.zeros_like(dv_acc)

        def update(hh, keys, q0, masked):
            sl = slice(LANES * hh, LANES * (hh + 1))
            qbh, kah, dyah = qb_ref[q0:tq, sl], ka_ref[keys, sl], dya_ref[q0:tq, sl]
            eT = lax.dot_general(kah, qbh, NT, preferred_element_type=F32)
            if masked:
                r = lax.broadcasted_iota(jnp.int32, eT.shape, 0) + keys.start
                c = lax.broadcasted_iota(jnp.int32, eT.shape, 1) + q0
                eT = jnp.where(r <= c, eT, NEG)
            pT = jnp.exp2(eT)
            dsT = pT * lax.dot_general(va_ref[keys, sl], dyah, NT, preferred_element_type=F32)
            dsb = dsT.astype(BF16)
            dv_acc[hh, keys] += jnp.dot(pT.astype(BF16), dyah, preferred_element_type=F32)
            dk_acc[hh, keys] += jnp.dot(dsb, qbh, preferred_element_type=F32)
            rows = pl.ds(pl.multiple_of(i * tq + q0, half), tq - q0)
            dq_acc[hh, rows, :] += lax.dot_general(dsb, kah, TN, preferred_element_type=F32)

        @pl.when(i > j)
        def _():
            for hh in range(2):
                update(hh, slice(0, tq), 0, False)

        @pl.when(i == j)
        def _():
            for hh in range(2):
                for k0 in range(0, tq, half):
                    update(hh, slice(k0, k0 + half), k0, True)

        @pl.when(i == nq - 1)
        def _():
            dkv_ref[:, :LANES] = (jnp.where(first, dk_acc[0], _swap_halves(dk_acc[1])) * LN2).astype(BF16)
            dkv_ref[:, LANES:] = jnp.where(first, dv_acc[0], _swap_halves(dv_acc[1])).astype(BF16)
            dn_ref[...] = jnp.where(first, _swap_halves(dk_acc[0]), dk_acc[1])

        @pl.when(n == nsteps - 1)
        def _():
            first_s = lax.broadcasted_iota(jnp.int32, (S, LANES), 1) < FOX_HEAD_DIM
            dq_ref[...] = (jnp.where(first_s, dq_acc[0], _swap_halves(dq_acc[1])) * FOX_SCALE).astype(BF16)
            drow_ref[...] = jnp.where(first_s, _swap_halves(dq_acc[0]), dq_acc[1])

    qblock = pl.BlockSpec((tq, 2 * LANES), lambda g, n, it, jt: (it[n], g))
    kblock = pl.BlockSpec((tq, 2 * LANES), lambda g, n, it, jt: (jt[n], g))
    whole = pl.BlockSpec((S, LANES), lambda g, n, it, jt: (0, g))
    return pl.pallas_call(
        body, name=name,
        grid_spec=pltpu.PrefetchScalarGridSpec(
            num_scalar_prefetch=2, grid=(FOX_PAIRS, nsteps),
            in_specs=[qblock, kblock, kblock, qblock, pl.BlockSpec(memory_space=pl.ANY)],
            out_specs=[whole, kblock, pl.BlockSpec((tq, LANES), lambda g, n, it, jt: (jt[n], g)), whole],
            scratch_shapes=[pltpu.VMEM((2, S, LANES), F32), pltpu.VMEM((2, tq, LANES), F32),
                            pltpu.VMEM((2, tq, LANES), F32)]),
        out_shape=[jax.ShapeDtypeStruct(dh.shape, dh.dtype), jax.ShapeDtypeStruct((S, 2 * D_MAIN + LANES), BF16),
                   jax.ShapeDtypeStruct((S, D_MAIN), F32), jax.ShapeDtypeStruct((S, D_MAIN), F32)],
        input_output_aliases={6: 0},
        compiler_params=_cparams("parallel", "arbitrary"),
    )(it, jt, qb, ka, va, dya, dh)


TB_ROWS = 256
TB_SEQ = 512
TB_PREP = 2048
TQ_FOX_FWD = 1024
TQ_FOX_BWD = 1024


def _local_step(x, mem, target, win0, wmkv, wout0, pw, pscale, late_weights, ln_g, ln_b, bias, send_early=None,
                send_late=None):
    ones = jnp.ones((1, D_MAIN), F32)
    g0, b0, g1, b1 = ln_g[0:1], ln_b[0:1], ln_g[1:2], ln_b[1:2]
    mm = lambda a, b, mode, dt, tm, tn, tk, name, **kw: _mm(a, b, mode=mode, out_dtype=dt, tm=tm, tn=tn, tk=tk,
                                                            name=name, **kw)

    h0 = mm(x, win0, "nn", BF16, 256, D_IN, D_MODEL, "l0_in")
    pm, mixed = _pool_fwd(h0, pw, tb=TB_SEQ, name="l0_pool_fwd")
    mkv0 = mm(mem, wmkv[0], "nn", F32, 256, 1024, 1024, "l0_mkv")
    ymem0 = _memattn_fwd(h0, mkv0, tb=TB_SEQ, name="l0_mem_fwd")
    yg0 = _gate_fwd(mixed, pscale, h0, ymem0, tb=TB_ROWS, name="l0_gate_fwd")
    x1, xhat0, rstd0 = _out_ln(yg0, wout0, x, g0, b0, tb=TB_SEQ, name="l0_out_ln")
    win1, wout1, wkvp, wf = late_weights(x1)

    kv = mm(x1, wkvp, "nn", BF16, 512, 2 * D_MAIN, D_MODEL, "kv_proj")
    fl = mm(x1, wf, "nn", F32, 512, LANES, D_MODEL, "f_proj")
    negcum = _forget_fwd(fl, bias, tb=TB_SEQ, name="forget_fwd")

    h1 = mm(x1, win1, "nn", BF16, 256, D_IN, D_MODEL, "l1_in")
    qa, ka, va = _fox_prep_fwd(h1, kv, negcum, tb=TB_PREP, name="fox_prep_fwd")
    y1, lse2 = _fox_fwd(qa, ka, va, tq=TQ_FOX_FWD, name="fox_fwd")
    mkv1 = mm(mem, wmkv[1], "nn", F32, 256, 1024, 1024, "l1_mkv")
    ymem1 = _memattn_fwd(h1, mkv1, tb=TB_SEQ, name="l1_mem_fwd")
    yg1 = _gate_fwd(y1, ones, h1, ymem1, tb=TB_ROWS, name="l1_gate_fwd")
    dz1, dg1, db1, sq = _out_ln_loss(yg1, wout1, x1, g1, b1, target, tb=TB_SEQ, name="l1_out_ln_loss")

    dwout1 = mm(yg1, dz1, "tn", BF16, D_MIX, D_MODEL, 512, "l1_dwout")
    dyg1 = mm(dz1, wout1, "nt", BF16, 512, D_MIX, D_MODEL, "l1_dyg")
    dy1, dymem1, dh1 = _gate_bwd(dyg1, y1, ones, h1, ymem1, tb=TB_ROWS, name="l1_gate_bwd")
    qb, dya = _fox_prep_bwd(qa, lse2, dy1, y1, tb=TB_PREP, name="fox_prep_bwd")
    dh1, dh2, dnp, drowp = _fox_bwd(qb, ka, va, dya, dh1, tq=TQ_FOX_BWD, name="fox_bwd")
    dh2, dbias = _forget_bwd(dnp, drowp, fl, bias, dh2, tb=TB_SEQ, name="forget_bwd")
    dh1, dmkv1 = _memattn_bwd(h1, mkv1, dymem1, dh1, tb=TB_SEQ, name="l1_mem_bwd")
    dwmkv1 = mm(mem, dmkv1, "tn", BF16, D_MODEL, 1024, N_MEM, "l1_dwmkv")
    dwin1 = mm(x1, dh1, "tn", BF16, D_MODEL, D_IN // 2, 512, "l1_dwin")
    dwkvf = mm(x1, dh2, "tn", F32, D_MODEL, 2 * D_MAIN + LANES, 512, "dwkv")
    dkv = dwkvf[:, :2 * D_MAIN].reshape(D_MODEL, FOX_PAIRS, 2, LANES)
    dwkv = jnp.concatenate([dkv[:, :, 0, :].reshape(D_MODEL, D_MAIN), dkv[:, :, 1, :].reshape(D_MODEL, D_MAIN),
                            dwkvf[:, 2 * D_MAIN:2 * D_MAIN + FOX_HEADS]], axis=1)
    dwkv = dwkv.reshape(D_MODEL, N_DEV, -1).transpose(1, 0, 2).astype(BF16)
    anchor = send_early(dict(w_out=dwout1, w_mem_kv=dwmkv1, w_in=dwin1, w_kv_shared=dwkv)) if send_early else 0.0
    dx1 = mm(dh1, win1, "nt", F32, 256, D_MODEL, D_IN, "l1_dx", add=dz1, add_scale=ALPHA)
    wkvf = jnp.concatenate([wkvp, wf], axis=1)
    dx1 = mm(dh2, wkvf, "nt", F32, 256, D_MODEL, 2 * D_MAIN + LANES, "kv_dx", add=dx1)

    dz0, dg0, db0 = _ln_bwd(dx1, xhat0, rstd0, g0 + anchor, tb=TB_ROWS, name="l0_ln_bwd")
    dwout0 = mm(yg0, dz0, "tn", BF16, D_MIX, D_MODEL, 512, "l0_dwout")
    dyg0 = mm(dz0, wout0, "nt", BF16, 512, D_MIX, D_MODEL, "l0_dyg")
    dy0, dymem0, dh0 = _gate_bwd(dyg0, mixed, pscale, h0, ymem0, tb=TB_ROWS, name="l0_gate_bwd")
    dh0, dpw, dpscale = _pool_bwd(dy0, pm, mixed, pw, pscale, dh0, tb=TB_SEQ, name="l0_pool_bwd")
    dh0, dmkv0 = _memattn_bwd(h0, mkv0, dymem0, dh0, tb=TB_SEQ, name="l0_mem_bwd")
    dwmkv0 = mm(mem, dmkv0, "tn", BF16, D_MODEL, 1024, N_MEM, "l0_dwmkv")
    dwin0 = mm(x, dh0, "tn", BF16, D_MODEL, D_IN // 2, 512, "l0_dwin")
    grads = dict(w_in=(dwin0, dwin1), w_mem_kv=(dwmkv0, dwmkv1), w_out=(dwout0, dwout1), pool_w=(dpw,),
                 w_kv_shared=dwkv, pool_scale=dpscale, ln_g=jnp.concatenate([dg0, dg1]),
                 ln_b=jnp.concatenate([db0, db1]), b_forget=dbias[0, :FOX_HEADS])
    if send_late:
        dz0 = send_late(grads, dz0)
    gx = mm(dh0, win0, "nt", F32, 256, D_MODEL, D_IN, "l0_dx", add=dz0, add_scale=ALPHA)
    return sq, gx, grads


MESH_ID = pl.DeviceIdType.MESH
HBM = pl.BlockSpec(memory_space=pl.ANY)
SLICED = {"w_in": (2, D_IN // N_DEV), "w_mem_kv": (1, D_MODEL // N_DEV), "w_out": (1, D_MIX // N_DEV),
          "pool_w": (1, POOL_GROUP // N_DEV)}


def _place():
    return lax.axis_index("x"), lax.axis_index("y"), lax.axis_index("c")


def _slot(p):
    return 4 * p[0] + 2 * p[1] + p[2]


def _cut(ref, axis, width, s):
    idx = [slice(None)] * len(ref.shape)
    idx[axis] = pl.ds(s * width, width)
    return ref.at[tuple(idx)]


def _all_gather(shards, cuts, *, name):
    nt = len(shards)

    def full_shape(a, cut):
        if cut is None:
            return (N_DEV,) + a.shape
        return a.shape[:cut[0]] + (a.shape[cut[0]] * N_DEV,) + a.shape[cut[0] + 1:]

    def body(*refs):
        ins, outs = refs[:nt], refs[nt:2 * nt]
        send_sems, recv_sems, local_sems = refs[2 * nt:]
        x, y, c = _place()
        me, sibling = (x, y, c), (x, y, 1 - c)
        chips = [(1 - x, y), (x, 1 - y), (1 - x, 1 - y)]

        def place(t, s):
            return outs[t].at[s] if cuts[t] is None else _cut(outs[t], cuts[t][0], cuts[t][1], s)

        def copies(k, block, to, from_input=False):
            s = _slot(block)
            return [pltpu.make_async_remote_copy(
                src_ref=ins[t] if from_input else place(t, s), dst_ref=place(t, s),
                send_sem=send_sems.at[nt * k + t], recv_sem=recv_sems.at[nt * k + t],
                device_id=to, device_id_type=MESH_ID) for t in range(nt)]

        mine = [pltpu.make_async_copy(ins[t], place(t, _slot(me)), local_sems.at[t]) for t in range(nt)]
        for cp in mine:
            cp.start()
        first = [copies(0, me, sibling, True)] + [copies(1 + j, me, (*chip, c), True) for j, chip in enumerate(chips)]
        for group in first:
            for cp in group:
                cp.start()
        passed = [copies(4 + j, (*chip, c), sibling) for j, chip in enumerate(chips)]
        for j, chip in enumerate(chips):
            for cp in copies(1 + j, (*chip, c), me):
                cp.wait_recv()
            for cp in passed[j]:
                cp.start()
        for cp in copies(0, sibling, me):
            cp.wait_recv()
        for j, chip in enumerate(chips):
            for cp in copies(4 + j, (*chip, 1 - c), me):
                cp.wait_recv()
        for group in first + passed:
            for cp in group:
                cp.wait_send()
        for cp in mine:
            cp.wait()

    return pl.pallas_call(
        body, name=name, in_specs=[HBM] * nt, out_specs=[HBM] * nt,
        out_shape=[jax.ShapeDtypeStruct(full_shape(a, cut), a.dtype) for a, cut in zip(shards, cuts)],
        scratch_shapes=[pltpu.SemaphoreType.DMA((7 * nt,)), pltpu.SemaphoreType.DMA((7 * nt,)),
                        pltpu.SemaphoreType.DMA((nt,))],
    )(*shards)


def _exchange_copies(items, ins, outs, send_sems, recv_sems, local_sems, gather=False):
    nt = len(items)
    x, y, c = _place()
    me = _slot((x, y, c))
    flip = lambda v, bit: 1 - v if bit else v

    def part(ref, cut, s):
        return ref.at[s] if cut is None else _cut(ref, cut[0], cut[1], s)

    def src(t, s):
        return ins[t] if gather else part(ins[t], items[t][0], s)

    def dst(t, s):
        if gather:
            return part(outs[items[t][1]], items[t][0], s)
        d = outs[items[t][1]].at[s]
        return d if items[t][2] is None else d.at[items[t][2]]

    sends, arrivals = [], []
    for k in range(1, N_DEV):
        peer = (flip(x, k & 4), flip(y, k & 2), flip(c, k & 1))
        ps = _slot(peer)
        for t in range(nt):
            sems = dict(send_sem=send_sems.at[nt * (k - 1) + t], recv_sem=recv_sems.at[nt * (k - 1) + t],
                        device_id=peer, device_id_type=MESH_ID)
            sends.append(pltpu.make_async_remote_copy(src_ref=src(t, ps), dst_ref=dst(t, me), **sems))
            arrivals.append(pltpu.make_async_remote_copy(src_ref=src(t, ps), dst_ref=dst(t, ps), **sems))
    mine = [pltpu.make_async_copy(src(t, me), dst(t, me), local_sems.at[t]) for t in range(nt)]
    return sends, arrivals, mine


SEMS = pl.BlockSpec(memory_space=pltpu.SEMAPHORE)
SIDE_EFFECT = pltpu.SideEffectType.DATAFLOW_SIDE_EFFECTING


def _exchange_start(srcs, items, landings, *, name, gather=False, carry=()):
    nt, nl, nc = len(srcs), len(landings), len(carry)

    def body(*refs):
        ins, lands = refs[:nt], refs[nt:nt + nl]
        send_sems, recv_sems, local_sems = refs[nt + nl + nc:nt + nl + nc + 3]
        token = refs[-1]
        sends, _, mine = _exchange_copies(items, ins, lands, send_sems, recv_sems, local_sems, gather)
        for cp in sends + mine:
            cp.start()
        token[...] = jnp.zeros_like(token)

    hbm = lambda a: pltpu.HBM(a.shape, a.dtype)
    fresh = [pltpu.with_memory_space_constraint(
        lax.empty(l.shape, l.dtype) if isinstance(l, jax.ShapeDtypeStruct) else l, pltpu.HBM) for l in landings]
    res = pl.pallas_call(
        body, name=name, in_specs=[HBM] * (nt + nl + nc),
        out_specs=[SEMS, SEMS, SEMS] + [HBM] * (nt + nl + nc) + [pl.BlockSpec(memory_space=pltpu.VMEM)],
        out_shape=[pltpu.SemaphoreType.DMA((7 * nt,)), pltpu.SemaphoreType.DMA((7 * nt,)), pltpu.SemaphoreType.DMA((nt,))]
        + [hbm(a) for a in srcs] + [hbm(l) for l in landings] + [hbm(a) for a in carry]
        + [jax.ShapeDtypeStruct((8, LANES), F32)],
        input_output_aliases={i: 3 + i for i in range(nt + nl + nc)},
        compiler_params=pltpu.CompilerParams(has_side_effects=SIDE_EFFECT),
    )(*[pltpu.with_memory_space_constraint(a, pltpu.HBM) for a in srcs], *fresh,
      *[pltpu.with_memory_space_constraint(a, pltpu.HBM) for a in carry])
    return res[:3 + nt + nl], res[-1][0:1, 0:1], res[3 + nt + nl:-1]


def _exchange_wait(state, items, nt, after, *, name, gather=False):
    sems, bufs = state[:3], state[3:]
    nl = len(bufs) - nt

    def body(*refs):
        ins, lands = refs[:nt], refs[nt:nt + nl]
        send_sems, recv_sems, local_sems = refs[nt + nl:nt + nl + 3]
        sends, arrivals, mine = _exchange_copies(items, ins, lands, send_sems, recv_sems, local_sems, gather)
        for sent, landed in zip(sends, arrivals):
            landed.wait_recv()
            sent.wait_send()
        for cp in mine:
            cp.wait()

    hbm = lambda a: pltpu.HBM(a.shape, a.dtype)
    res = pl.pallas_call(
        body, name=name, in_specs=[HBM] * (nt + nl) + [SEMS, SEMS, SEMS, HBM], out_specs=[HBM] * (nt + nl),
        out_shape=[hbm(a) for a in bufs],
        input_output_aliases={i: i for i in range(nt + nl)},
        compiler_params=pltpu.CompilerParams(has_side_effects=SIDE_EFFECT),
    )(*bufs, *sems, after)
    return res[nt:]


def _adamw(recv, w, m, v, *, split, name):
    shape = w.shape
    axis, parts = split
    block = shape[:axis] + (shape[axis] // parts,) + shape[axis + 1:]
    nd = len(shape)

    def body(r_ref, w_ref, m_ref, v_ref, g_ref, d_ref, nm_ref, nv_ref):
        g = r_ref[0].astype(F32)
        for j in range(1, N_DEV):
            g = g + r_ref[j].astype(F32)
        nm = ADAM_B1 * m_ref[...] + (1.0 - ADAM_B1) * g
        nv = ADAM_B2 * v_ref[...] + (1.0 - ADAM_B2) * (g * g)
        m_hat = nm / (1.0 - ADAM_B1 ** ADAM_STEP)
        v_hat = nv / (1.0 - ADAM_B2 ** ADAM_STEP)
        g_ref[...] = g
        nm_ref[...] = nm
        nv_ref[...] = nv
        d_ref[...] = -ADAM_LR * (m_hat / (jnp.sqrt(v_hat) + ADAM_EPS) + ADAM_WD * w_ref[...])

    at = lambda i: tuple(i if a == axis else 0 for a in range(nd))
    one = pl.BlockSpec(block, at)
    shp = jax.ShapeDtypeStruct(shape, F32)
    return pl.pallas_call(
        body, name=name, grid=(parts,),
        in_specs=[pl.BlockSpec((N_DEV,) + block, lambda i: (0,) + at(i)), one, one, one],
        out_specs=[one, one, one, one], out_shape=[shp, shp, shp, shp],
        compiler_params=_cparams("parallel"),
    )(recv, w, m, v)


BIG = ("w_in", "w_mem_kv", "w_out", "pool_w", "w_kv_shared")
SMALL = ("pool_scale", "ln_g", "ln_b", "b_forget")
SMALL_ROWS = 40
ADAM_SPLIT = {"w_in": (1, 4), "w_mem_kv": (0, 2), "w_out": (0, 2), "pool_w": (0, 1), "w_kv_shared": (0, 4)}
PER_LAYER_CUT = {"w_out": (0, D_MIX // N_DEV), "w_mem_kv": (0, D_MODEL // N_DEV), "w_in": (1, D_IN // N_DEV),
                 "pool_w": (1, POOL_GROUP // N_DEV)}
EARLY = ("w_out", "w_mem_kv", "w_in", "w_kv_shared")
EARLY_ITEMS = [(PER_LAYER_CUT[n], i, 1) for i, n in enumerate(EARLY[:3])] + [(None, 3, None)]


def _flat(parts, rows):
    v = jnp.concatenate([p.reshape(-1) for p in parts])
    return jnp.pad(v, (0, rows * LANES - v.shape[0])).reshape(rows, LANES)


def _unflat(flat, shapes):
    v, out, off = flat.reshape(-1), [], 0
    for s in shapes:
        n = math.prod(s)
        out.append(v[off:off + n].reshape(s))
        off += n
    return out


def kernel(x, mem, w_in, w_mem_kv, w_out, ln_g, ln_b, pool_w, pool_scale, w_kv_shared, b_forget, loss_target, m_w_in, m_w_mem_kv, m_w_out, m_ln_g, m_ln_b, m_pool_w, m_pool_scale, m_w_kv_shared, m_b_forget, v_w_in, v_w_mem_kv, v_w_out, v_ln_g, v_ln_b, v_pool_w, v_pool_scale, v_w_kv_shared, v_b_forget):
    w = dict(w_in=w_in, w_mem_kv=w_mem_kv, w_out=w_out, ln_g=ln_g, ln_b=ln_b, pool_w=pool_w[0],
             pool_scale=pool_scale, w_kv_shared=w_kv_shared, b_forget=b_forget)
    m = dict(w_in=m_w_in, w_mem_kv=m_w_mem_kv, w_out=m_w_out, ln_g=m_ln_g, ln_b=m_ln_b, pool_w=m_pool_w[0],
             pool_scale=m_pool_scale, w_kv_shared=m_w_kv_shared, b_forget=m_b_forget)
    v = dict(w_in=v_w_in, w_mem_kv=v_w_mem_kv, w_out=v_w_out, ln_g=v_ln_g, ln_b=v_ln_b, pool_w=v_pool_w[0],
             pool_scale=v_pool_scale, w_kv_shared=v_w_kv_shared, b_forget=v_b_forget)

    wb = {n: w[n].astype(BF16) for n in BIG}
    win0, wmkv, pw, wout0, pscale = _all_gather(
        [wb["w_in"][0], wb["w_mem_kv"], wb["pool_w"], wb["w_out"][0], jnp.pad(pool_scale, ((0, 7), (0, 0)))],
        [PER_LAYER_CUT["w_in"], SLICED["w_mem_kv"], SLICED["pool_w"], PER_LAYER_CUT["w_out"], None],
        name="gather_weights")
    pscale = pscale[:, 0, :].reshape(1, D_MAIN)
    late_srcs = [wb["w_in"][1], wb["w_out"][1], wb["w_kv_shared"]]
    late_items = [(PER_LAYER_CUT["w_in"], 0, None), (PER_LAYER_CUT["w_out"], 1, None), (None, 2, None)]
    late_state, _, (win0,) = _exchange_start(
        late_srcs, late_items,
        [jax.ShapeDtypeStruct((D_MODEL, D_IN), wb["w_in"].dtype), jax.ShapeDtypeStruct((D_MIX, D_MODEL), wb["w_in"].dtype),
         jax.ShapeDtypeStruct((N_DEV,) + w_kv_shared.shape, wb["w_in"].dtype)],
        name="gather_late_start", gather=True, carry=[win0])
    bias = jnp.pad(b_forget, (0, LANES - FOX_HEADS)).reshape(1, LANES)

    def late_weights(x1):
        win1, wout1, wkv = _exchange_wait(late_state, late_items, len(late_srcs), x1, name="gather_late_wait", gather=True)
        wkv = wkv.transpose(1, 0, 2).reshape(D_MODEL, -1)
        wkvp = jnp.stack([wkv[:, :D_MAIN].reshape(D_MODEL, FOX_PAIRS, LANES),
                          wkv[:, D_MAIN:2 * D_MAIN].reshape(D_MODEL, FOX_PAIRS, LANES)], axis=2).reshape(D_MODEL, 2 * D_MAIN)
        return win1, wout1, wkvp, jnp.pad(wkv[:, 2 * D_MAIN:], ((0, 0), (0, LANES - FOX_HEADS)))

    early = {}

    def send_early(g):
        srcs = [g[n] for n in EARLY]
        lands = [jax.ShapeDtypeStruct((N_DEV, 2) + w[n].shape[1:], g[n].dtype) for n in EARLY[:3]]
        lands.append(jax.ShapeDtypeStruct(g["w_kv_shared"].shape, g["w_kv_shared"].dtype))
        early["state"], anchor, _ = _exchange_start(srcs, EARLY_ITEMS, lands, name="exchange_early_start")
        return anchor

    last_items = [(cut, i, 0) for i, (cut, _, _) in enumerate(EARLY_ITEMS[:3])] + [(PER_LAYER_CUT["pool_w"], 3, None),
                                                                                  (None, 4, None)]

    def send_late(g, dz0):
        small = jnp.concatenate([g["pool_scale"].reshape(N_DEV, -1)]
                                + [jnp.broadcast_to(g[n].reshape(1, -1), (N_DEV, g[n].size)) for n in SMALL[1:]], axis=1)
        small = jnp.pad(small, ((0, 0), (0, SMALL_ROWS * LANES - small.shape[1]))).reshape(N_DEV, SMALL_ROWS, LANES)
        r_out, r_mkv, r_in, early["r_kv"] = _exchange_wait(early["state"], EARLY_ITEMS, len(EARLY), g["w_in"][0],
                                                           name="exchange_early_wait")
        srcs = [g[n][0] for n in EARLY[:3]] + [g["pool_w"][0], small]
        early["last"], _, (dz0,) = _exchange_start(
            srcs, last_items,
            [r_out, r_mkv, r_in, jax.ShapeDtypeStruct((N_DEV,) + w["pool_w"].shape, srcs[3].dtype),
             jax.ShapeDtypeStruct(small.shape, small.dtype)], name="exchange_last_start", carry=[dz0])
        return dz0

    sq, gx, grads = _local_step(x[0], mem[0], loss_target[0], win0, wmkv, wout0, pw, pscale, late_weights,
                                ln_g, ln_b, bias, send_early, send_late)
    loss = lax.psum((0.5 / D_MODEL) * jnp.sum(sq), ("x", "y", "c"))

    r_out, r_mkv, r_in, r_pw, r_small = _exchange_wait(early["last"], last_items, len(last_items), gx,
                                                       name="exchange_last_wait")
    recv = dict(w_out=r_out, w_mem_kv=r_mkv, w_in=r_in, pool_w=r_pw, w_kv_shared=early["r_kv"], small=r_small)

    outs = {}
    for n in BIG:
        res = _adamw(recv[n], w[n], m[n], v[n], split=ADAM_SPLIT[n], name="adamw_" + n)
        for kind, a in zip(("grad", "delta", "new_m", "new_v"), res):
            outs[kind, n] = a[None] if n == "pool_w" else a
    small_shapes = [w[n].shape for n in SMALL]
    res = _adamw(recv["small"], _flat([w[n] for n in SMALL], SMALL_ROWS), _flat([m[n] for n in SMALL], SMALL_ROWS),
                 _flat([v[n] for n in SMALL], SMALL_ROWS), split=(0, 1), name="adamw_small")
    for kind, flat in zip(("grad", "delta", "new_m", "new_v"), res):
        for n, a in zip(SMALL, _unflat(flat, small_shapes)):
            outs[kind, n] = a
    order = ("w_in", "w_mem_kv", "w_out", "ln_g", "ln_b", "pool_w", "pool_scale", "w_kv_shared", "b_forget")
    return (loss, gx[None], *[outs[kind, n] for kind in ("grad", "delta", "new_m", "new_v") for n in order])
```

```python
import math

import numpy as np
import jax
import jax.numpy as jnp
from jax import lax
from jax.experimental import pallas as pl
from jax.experimental.pallas import tpu as pltpu

F32 = jnp.float32
BF16 = jnp.bfloat16

D_MODEL = 1024
D_MAIN = 1024
D_MEM = 512
D_MIX = D_MAIN + D_MEM
D_IN = 2 * D_MIX
N_MEM = 256
MEM_HEADS = 4
MEM_HEAD_DIM = 128
FOX_HEADS = 16
FOX_HEAD_DIM = 64
FOX_PAIRS = FOX_HEADS // 2
POOL_WINDOWS = (2, 4, 8, 16)
POOL_GROUP = 256
POOL_HALO = 16
ALPHA = 4.0 ** 0.25
LN_EPS = 1e-5
NEG = -1e30
LANES = 128
N_DEV = 8

ADAM_LR = 0.001
ADAM_B1 = 0.9
ADAM_B2 = 0.999
ADAM_EPS = 1e-08
ADAM_WD = 0.01
ADAM_STEP = 10

VMEM_LIMIT = 56 * 1024 * 1024

NN = (((1,), (0,)), ((), ()))
NT = (((1,), (1,)), ((), ()))
TN = (((0,), (0,)), ((), ()))


def _cparams(*sem):
    return pltpu.CompilerParams(dimension_semantics=sem, vmem_limit_bytes=VMEM_LIMIT)


def _sigmoid(z):
    return 1.0 / (1.0 + jnp.exp(-z))


def _mm(a, b, *, mode, out_dtype, tm, tn, tk, name, add=None, add_scale=1.0):
    if mode == "nn":
        (M, K), (K2, N) = a.shape, b.shape
    elif mode == "nt":
        (M, K), (N, K2) = a.shape, b.shape
    else:
        (K, M), (K2, N) = a.shape, b.shape
    assert K == K2, (a.shape, b.shape, mode)
    tm, tn, tk = min(tm, M), min(tn, N), min(tk, K)
    assert M % tm == 0 and N % tn == 0 and K % tk == 0, (M, N, K, tm, tn, tk)
    gm, gn, gk = M // tm, N // tn, K // tk
    dims = {"nn": NN, "nt": NT, "tn": TN}[mode]
    if mode == "tn":
        a_spec = pl.BlockSpec((tk, tm), lambda i, j, k: (k, i))
    else:
        a_spec = pl.BlockSpec((tm, tk), lambda i, j, k: (i, k))
    if mode == "nt":
        b_spec = pl.BlockSpec((tn, tk), lambda i, j, k: (j, k))
    else:
        b_spec = pl.BlockSpec((tk, tn), lambda i, j, k: (k, j))
    o_spec = pl.BlockSpec((tm, tn), lambda i, j, k: (i, j))
    has_add = add is not None
    acc_in_out = out_dtype == F32

    def body(*refs):
        a_ref, b_ref = refs[0], refs[1]
        add_ref = refs[2] if has_add else None
        o_ref = refs[3] if has_add else refs[2]
        prod = lax.dot_general(a_ref[...].astype(BF16), b_ref[...].astype(BF16), dims,
                               preferred_element_type=F32)

        def finish(r):
            if has_add:
                r = r + add_scale * add_ref[...]
            o_ref[...] = r.astype(out_dtype)

        if gk == 1:
            finish(prod)
        else:
            acc_ref = o_ref if acc_in_out else refs[-1]
            k = pl.program_id(2)

            @pl.when(k == 0)
            def _():
                acc_ref[...] = prod

            @pl.when(k > 0)
            def _():
                acc_ref[...] += prod

            if has_add or not acc_in_out:
                @pl.when(k == gk - 1)
                def _():
                    finish(acc_ref[...])

    in_specs = [a_spec, b_spec] + ([o_spec] if has_add else [])
    args = (a, b) + ((add,) if has_add else ())
    return pl.pallas_call(
        body, name=name, grid=(gm, gn, gk), in_specs=in_specs, out_specs=o_spec,
        out_shape=jax.ShapeDtypeStruct((M, N), out_dtype),
        scratch_shapes=[pltpu.VMEM((tm, tn), F32)] if gk > 1 and not acc_in_out else [],
        compiler_params=_cparams("parallel", "parallel", "arbitrary"),
    )(*args)


def _ln_stats(z):
    mu = jnp.mean(z, axis=1, keepdims=True)
    zc = z - mu
    var = jnp.mean(zc * zc, axis=1, keepdims=True)
    rstd = lax.rsqrt(var + LN_EPS)
    return zc * rstd, rstd


def _ln_bwd_math(dy, xhat, rstd, g):
    dxh = dy * g
    m1 = jnp.mean(dxh, axis=1, keepdims=True)
    m2 = jnp.mean(dxh * xhat, axis=1, keepdims=True)
    return rstd * (dxh - m1 - xhat * m2)


def _out_ln(yg, wout, x, g, b, *, tb, name):
    S = x.shape[0]
    tb = min(tb, S)

    def body(yg_ref, w_ref, x_ref, g_ref, b_ref, y_ref, xhat_ref, rstd_ref):
        o = jnp.dot(yg_ref[...], w_ref[...], preferred_element_type=F32)
        xhat, rstd = _ln_stats(ALPHA * x_ref[...] + o)
        xhat_ref[...] = xhat
        rstd_ref[...] = rstd
        y_ref[...] = xhat * g_ref[...] + b_ref[...]

    row = pl.BlockSpec((tb, D_MODEL), lambda i: (i, 0))
    vec = pl.BlockSpec((1, D_MODEL), lambda i: (0, 0))
    return pl.pallas_call(
        body, name=name, grid=(S // tb,),
        in_specs=[pl.BlockSpec((tb, D_MIX), lambda i: (i, 0)), pl.BlockSpec((D_MIX, D_MODEL), lambda i: (0, 0)),
                  row, vec, vec],
        out_specs=[row, row, pl.BlockSpec((tb, 1), lambda i: (i, 0))],
        out_shape=[jax.ShapeDtypeStruct((S, D_MODEL), F32), jax.ShapeDtypeStruct((S, D_MODEL), F32),
                   jax.ShapeDtypeStruct((S, 1), F32)],
        compiler_params=_cparams("parallel"),
    )(yg, wout, x, g, b)


def _out_ln_loss(yg, wout, x, g, b, target, *, tb, name):
    S = x.shape[0]
    tb = min(tb, S)

    def body(yg_ref, w_ref, x_ref, g_ref, b_ref, t_ref, dz_ref, dg_ref, db_ref, sq_ref):
        @pl.when(pl.program_id(0) == 0)
        def _():
            dg_ref[...] = jnp.zeros_like(dg_ref)
            db_ref[...] = jnp.zeros_like(db_ref)
            sq_ref[...] = jnp.zeros_like(sq_ref)

        o = jnp.dot(yg_ref[...], w_ref[...], preferred_element_type=F32)
        xhat, rstd = _ln_stats(ALPHA * x_ref[...] + o)
        err = xhat * g_ref[...] + b_ref[...] - t_ref[...]
        sq_ref[...] += jnp.sum(err * err, axis=0, keepdims=True)
        dy = err * (1.0 / D_MODEL)
        dz_ref[...] = _ln_bwd_math(dy, xhat, rstd, g_ref[...])
        dg_ref[...] += jnp.sum(dy * xhat, axis=0, keepdims=True)
        db_ref[...] += jnp.sum(dy, axis=0, keepdims=True)

    row = pl.BlockSpec((tb, D_MODEL), lambda i: (i, 0))
    vec = pl.BlockSpec((1, D_MODEL), lambda i: (0, 0))
    vshape = jax.ShapeDtypeStruct((1, D_MODEL), F32)
    return pl.pallas_call(
        body, name=name, grid=(S // tb,),
        in_specs=[pl.BlockSpec((tb, D_MIX), lambda i: (i, 0)), pl.BlockSpec((D_MIX, D_MODEL), lambda i: (0, 0)),
                  row, vec, vec, row],
        out_specs=[row, vec, vec, vec],
        out_shape=[jax.ShapeDtypeStruct((S, D_MODEL), F32), vshape, vshape, vshape],
        compiler_params=_cparams("arbitrary"),
    )(yg, wout, x, g, b, target)


def _ln_bwd(dy, xhat, rstd, g, *, tb, name):
    S = dy.shape[0]
    tb = min(tb, S)

    def body(dy_ref, xhat_ref, rstd_ref, g_ref, dz_ref, dg_ref, db_ref):
        @pl.when(pl.program_id(0) == 0)
        def _():
            dg_ref[...] = jnp.zeros_like(dg_ref)
            db_ref[...] = jnp.zeros_like(db_ref)

        dy_, xhat_ = dy_ref[...], xhat_ref[...]
        dz_ref[...] = _ln_bwd_math(dy_, xhat_, rstd_ref[...], g_ref[...])
        dg_ref[...] += jnp.sum(dy_ * xhat_, axis=0, keepdims=True)
        db_ref[...] += jnp.sum(dy_, axis=0, keepdims=True)

    row = pl.BlockSpec((tb, D_MODEL), lambda i: (i, 0))
    vec = pl.BlockSpec((1, D_MODEL), lambda i: (0, 0))
    return pl.pallas_call(
        body, name=name, grid=(S // tb,),
        in_specs=[row, row, pl.BlockSpec((tb, 1), lambda i: (i, 0)), vec],
        out_specs=[row, vec, vec],
        out_shape=[jax.ShapeDtypeStruct((S, D_MODEL), F32), jax.ShapeDtypeStruct((1, D_MODEL), F32),
                   jax.ShapeDtypeStruct((1, D_MODEL), F32)],
        compiler_params=_cparams("arbitrary"),
    )(dy, xhat, rstd, g)


def _gate_fwd(ysrc, scale, h, ymem, *, tb, name):
    S = ysrc.shape[0]
    tb = min(tb, S)

    def body(ys_ref, sc_ref, ga_ref, gb_ref, gc_ref, ym_ref, yg_ref):
        ymain = ys_ref[...] * sc_ref[...]
        for k, g_ref in enumerate((ga_ref, gb_ref)):
            gv = g_ref[...].astype(F32)
            yg_ref[:, 512 * k:512 * (k + 1)] = (ymain[:, 512 * k:512 * (k + 1)] * gv * _sigmoid(gv)).astype(BF16)
        gv = gc_ref[...].astype(F32)
        yg_ref[:, 1024:1536] = (ym_ref[...] * gv * _sigmoid(gv)).astype(BF16)

    slab = lambda c: pl.BlockSpec((tb, 512), lambda i, c=c: (i, c))
    return pl.pallas_call(
        body, name=name, grid=(S // tb,),
        in_specs=[pl.BlockSpec((tb, D_MAIN), lambda i: (i, 0)), pl.BlockSpec((1, D_MAIN), lambda i: (0, 0)),
                  slab(3), slab(4), slab(5), pl.BlockSpec((tb, D_MEM), lambda i: (i, 0))],
        out_specs=pl.BlockSpec((tb, D_MIX), lambda i: (i, 0)),
        out_shape=jax.ShapeDtypeStruct((S, D_MIX), BF16),
        compiler_params=_cparams("parallel"),
    )(ysrc, scale, h, h, h, ymem)


def _gate_bwd(dyg, ysrc, scale, h, ymem, *, tb, name, fox=False):
    S = ysrc.shape[0]
    tb = min(tb, S)

    def dsilu(gv):
        sg = _sigmoid(gv)
        return sg, sg * (1.0 + gv * (1.0 - sg))

    def body(da_ref, db_ref, dc_ref, ys_ref, sc_ref, ga_ref, gb_ref, gc_ref, ym_ref, dym_ref, dymem_ref, dh_ref):
        ymain = ys_ref[...] * sc_ref[...]
        lane = lax.broadcasted_iota(jnp.int32, (tb, LANES), 1)
        first = lane < FOX_HEAD_DIM
        for k, (d_ref, g_ref) in enumerate(((da_ref, ga_ref), (db_ref, gb_ref))):
            gv, d = g_ref[...].astype(F32), d_ref[...].astype(F32)
            sg, ds = dsilu(gv)
            dy = d * gv * sg
            dh_ref[:, 512 * k:512 * (k + 1)] = (d * ymain[:, 512 * k:512 * (k + 1)] * ds).astype(BF16)
            if not fox:
                dym_ref[:, 512 * k:512 * (k + 1)] = dy
                continue
            for q in range(512 // LANES):
                cols = slice(LANES * q, LANES * (q + 1))
                dy2 = dy[:, cols]
                prod = dy2 * ymain[:, 512 * k + LANES * q:512 * k + LANES * (q + 1)]
                for hh in range(2):
                    delta = jnp.sum(jnp.where(first == (hh == 0), prod, 0.0), axis=1, keepdims=True)
                    dyh = dy2 if hh == 0 else _swap_halves(dy2)
                    c0 = LANES * (2 * (4 * k + q) + hh)
                    dym_ref[:, c0:c0 + LANES] = jnp.where(
                        first, dyh, _lanes3(lane, AUX, _split3(-delta), 0.0)).astype(BF16)
        gv, d = gc_ref[...].astype(F32), dc_ref[...].astype(F32)
        sg, ds = dsilu(gv)
        dymem_ref[...] = d * gv * sg
        dh_ref[:, 1024:1536] = (d * ym_ref[...] * ds).astype(BF16)

    slab = lambda c: pl.BlockSpec((tb, 512), lambda i, c=c: (i, c))
    return pl.pallas_call(
        body, name=name, grid=(S // tb,),
        in_specs=[slab(0), slab(1), slab(2),
                  pl.BlockSpec((tb, D_MAIN), lambda i: (i, 0)), pl.BlockSpec((1, D_MAIN), lambda i: (0, 0)),
                  slab(3), slab(4), slab(5), pl.BlockSpec((tb, D_MEM), lambda i: (i, 0))],
        out_specs=[pl.BlockSpec((tb, 2 * D_MAIN if fox else D_MAIN), lambda i: (i, 0)),
                   pl.BlockSpec((tb, D_MEM), lambda i: (i, 0)), pl.BlockSpec((tb, D_MIX), lambda i: (i, 1))],
        out_shape=[jax.ShapeDtypeStruct((S, 2 * D_MAIN), BF16) if fox else jax.ShapeDtypeStruct((S, D_MAIN), F32),
                   jax.ShapeDtypeStruct((S, D_MEM), F32), jax.ShapeDtypeStruct((S, D_IN), BF16)],
        compiler_params=_cparams("parallel"),
    )(dyg, dyg, dyg, ysrc, scale, h, h, h, ymem)


def _window_count(t0, rows, w):
    t = t0 + lax.broadcasted_iota(jnp.int32, (rows, POOL_GROUP), 0)
    return jnp.minimum(t + 1, w).astype(F32)


def _pool_fwd(h, pw, *, tb, name):
    S = h.shape[0]
    tb = min(tb, S)

    def body(u_ref, pw_ref, pm_ref, mixed_ref, tail_ref):
        i = pl.program_id(0)

        @pl.when(i == 0)
        def _():
            tail_ref[...] = jnp.zeros_like(tail_ref)

        u = u_ref[...].astype(F32)
        xfull = jnp.concatenate([tail_ref[...], u], axis=0)
        for gi, w in enumerate(POOL_WINDOWS):
            cols = slice(POOL_GROUP * gi, POOL_GROUP * (gi + 1))
            s = xfull[:, cols]
            sh = 1
            while sh < w:
                s = s + pltpu.roll(s, sh, 0)
                sh *= 2
            pm = s[POOL_HALO:, :] / _window_count(i * tb, tb, w) - u[:, cols]
            pmb = pm.astype(BF16)
            pm_ref[:, cols] = pmb
            mixed_ref[:, cols] = jnp.dot(pmb, pw_ref[gi], preferred_element_type=F32)
        tail_ref[...] = u[tb - POOL_HALO:, :]

    return pl.pallas_call(
        body, name=name, grid=(S // tb,),
        in_specs=[pl.BlockSpec((tb, D_MAIN), lambda i: (i, 0)),
                  pl.BlockSpec((4, POOL_GROUP, POOL_GROUP), lambda i: (0, 0, 0))],
        out_specs=[pl.BlockSpec((tb, D_MAIN), lambda i: (i, 0)), pl.BlockSpec((tb, D_MAIN), lambda i: (i, 0))],
        out_shape=[jax.ShapeDtypeStruct((S, D_MAIN), BF16), jax.ShapeDtypeStruct((S, D_MAIN), F32)],
        scratch_shapes=[pltpu.VMEM((POOL_HALO, D_MAIN), F32)],
        compiler_params=_cparams("arbitrary"),
    )(h, pw)


def _pool_bwd(dymain, pm, mixed, pw, scale, dh, *, tb, name):
    S = dymain.shape[0]
    tb = min(tb, S)
    nb = S // tb
    n = tb + POOL_HALO

    def body(dy_ref, pm_ref, mixed_ref, pw_ref, sc_ref, dh_in, dh_ref, dpw_ref, dsc_ref, head_ref, dpw_acc):
        del dh_in
        i = pl.program_id(0)

        @pl.when(i == 0)
        def _():
            head_ref[...] = jnp.zeros_like(head_ref)
            dpw_acc[...] = jnp.zeros_like(dpw_acc)
            dsc_ref[...] = jnp.zeros_like(dsc_ref)

        dy = dy_ref[...]
        dsc_ref[...] += jnp.sum(dy * mixed_ref[...], axis=0, keepdims=True)
        dmixed = dy * sc_ref[...]
        t0 = (nb - 1 - i) * tb
        for gi, w in enumerate(POOL_WINDOWS):
            cols = slice(POOL_GROUP * gi, POOL_GROUP * (gi + 1))
            dm = dmixed[:, cols].astype(BF16)
            dpw_acc[gi] += lax.dot_general(pm_ref[:, cols], dm, TN, preferred_element_type=F32)
            dpm = lax.dot_general(dm, pw_ref[gi], NT, preferred_element_type=F32)
            e = dpm / _window_count(t0, tb, w)
            s = jnp.concatenate([e, head_ref[:, cols]], axis=0)
            sh = 1
            while sh < w:
                s = s + pltpu.roll(s, n - sh, 0)
                sh *= 2
            dh_ref[:, cols] = (s[:tb, :] - dpm).astype(BF16)
            head_ref[:, cols] = e[:POOL_HALO, :]

        @pl.when(i == nb - 1)
        def _():
            dpw_ref[...] = dpw_acc[...].astype(BF16)

    rev = lambda i: (nb - 1 - i, 0)
    return pl.pallas_call(
        body, name=name, grid=(nb,),
        in_specs=[pl.BlockSpec((tb, D_MAIN), rev), pl.BlockSpec((tb, D_MAIN), rev), pl.BlockSpec((tb, D_MAIN), rev),
                  pl.BlockSpec((4, POOL_GROUP, POOL_GROUP), lambda i: (0, 0, 0)),
                  pl.BlockSpec((1, D_MAIN), lambda i: (0, 0)), pl.BlockSpec(memory_space=pl.ANY)],
        out_specs=[pl.BlockSpec((tb, D_MAIN), rev),
                   pl.BlockSpec((4, POOL_GROUP, POOL_GROUP), lambda i: (0, 0, 0)),
                   pl.BlockSpec((1, D_MAIN), lambda i: (0, 0))],
        out_shape=[jax.ShapeDtypeStruct(dh.shape, dh.dtype),
                   jax.ShapeDtypeStruct((4, POOL_GROUP, POOL_GROUP), BF16), jax.ShapeDtypeStruct((1, D_MAIN), F32)],
        scratch_shapes=[pltpu.VMEM((POOL_HALO, D_MAIN), F32), pltpu.VMEM((4, POOL_GROUP, POOL_GROUP), F32)],
        input_output_aliases={5: 0},
        compiler_params=_cparams("arbitrary"),
    )(dymain, pm, mixed, pw, scale, dh)


MEM_SCALE = MEM_HEAD_DIM ** -0.5


def _mem_probs(q_ref, mkv_ref, hd):
    cols = slice(MEM_HEAD_DIM * hd, MEM_HEAD_DIM * (hd + 1))
    q = (q_ref[:, cols].astype(F32) * MEM_SCALE).astype(BF16)
    mk = mkv_ref[:, cols].astype(BF16)
    mv = mkv_ref[:, D_MEM + MEM_HEAD_DIM * hd:D_MEM + MEM_HEAD_DIM * (hd + 1)].astype(BF16)
    s = lax.dot_general(q, mk, NT, preferred_element_type=F32)
    e = jnp.exp(s - jnp.max(s, axis=1, keepdims=True))
    return cols, q, mk, mv, e, jnp.sum(e, axis=1, keepdims=True)


def _memattn_fwd(h, mkv, *, tb, name):
    S = h.shape[0]
    tb = min(tb, S)

    def body(q_ref, mkv_ref, y_ref):
        for hd in range(MEM_HEADS):
            cols, _, _, mv, e, l = _mem_probs(q_ref, mkv_ref, hd)
            y_ref[:, cols] = jnp.dot(e.astype(BF16), mv, preferred_element_type=F32) / l

    return pl.pallas_call(
        body, name=name, grid=(S // tb,),
        in_specs=[pl.BlockSpec((tb, D_MEM), lambda i: (i, 2)), pl.BlockSpec((N_MEM, 2 * D_MEM), lambda i: (0, 0))],
        out_specs=pl.BlockSpec((tb, D_MEM), lambda i: (i, 0)),
        out_shape=jax.ShapeDtypeStruct((S, D_MEM), F32),
        compiler_params=_cparams("parallel"),
    )(h, mkv)


def _memattn_bwd(h, mkv, dy, dh, *, tb, name):
    S = h.shape[0]
    tb = min(tb, S)

    def body(q_ref, mkv_ref, dy_ref, dh_in, dh_ref, dmkv_ref):
        del dh_in

        @pl.when(pl.program_id(0) == 0)
        def _():
            dmkv_ref[...] = jnp.zeros_like(dmkv_ref)

        for hd in range(MEM_HEADS):
            cols, q, mk, mv, e, l = _mem_probs(q_ref, mkv_ref, hd)
            p = e / l
            dyh = dy_ref[:, cols].astype(BF16)
            dp = lax.dot_general(dyh, mv, NT, preferred_element_type=F32)
            ds = p * (dp - jnp.sum(dp * p, axis=1, keepdims=True))
            dsb = ds.astype(BF16)
            dh_ref[:, cols] = (jnp.dot(dsb, mk, preferred_element_type=F32) * MEM_SCALE).astype(BF16)
            dmkv_ref[:, cols] += lax.dot_general(dsb, q, TN, preferred_element_type=F32)
            vcols = slice(D_MEM + MEM_HEAD_DIM * hd, D_MEM + MEM_HEAD_DIM * (hd + 1))
            dmkv_ref[:, vcols] += lax.dot_general(p.astype(BF16), dyh, TN, preferred_element_type=F32)

    return pl.pallas_call(
        body, name=name, grid=(S // tb,),
        in_specs=[pl.BlockSpec((tb, D_MEM), lambda i: (i, 2)), pl.BlockSpec((N_MEM, 2 * D_MEM), lambda i: (0, 0)),
                  pl.BlockSpec((tb, D_MEM), lambda i: (i, 0)), pl.BlockSpec(memory_space=pl.ANY)],
        out_specs=[pl.BlockSpec((tb, D_MEM), lambda i: (i, 2)), pl.BlockSpec((N_MEM, 2 * D_MEM), lambda i: (0, 0))],
        out_shape=[jax.ShapeDtypeStruct(dh.shape, dh.dtype), jax.ShapeDtypeStruct((N_MEM, 2 * D_MEM), F32)],
        input_output_aliases={3: 0},
        compiler_params=_cparams("arbitrary"),
    )(h, mkv, dy, dh)


def _forget_fwd(fl, bias, *, tb, name):
    S = fl.shape[0]
    tb = min(tb, S)

    def body(fl_ref, b_ref, o_ref, carry_ref):
        @pl.when(pl.program_id(0) == 0)
        def _():
            carry_ref[...] = jnp.zeros_like(carry_ref)

        z = fl_ref[...] + b_ref[...]
        lf = jnp.minimum(z, 0.0) - jnp.log(1.0 + jnp.exp(-jnp.abs(z)))
        row = lax.broadcasted_iota(jnp.int32, (tb, LANES), 0)
        c = lf
        sh = 1
        while sh < tb:
            c = c + jnp.where(row >= sh, pltpu.roll(c, sh, 0), 0.0)
            sh *= 2
        o_ref[...] = -(carry_ref[...] + c)
        carry_ref[...] += jnp.sum(lf, axis=0, keepdims=True)

    return pl.pallas_call(
        body, name=name, grid=(S // tb,),
        in_specs=[pl.BlockSpec((tb, LANES), lambda i: (i, 0)), pl.BlockSpec((1, LANES), lambda i: (0, 0))],
        out_specs=pl.BlockSpec((tb, LANES), lambda i: (i, 0)),
        out_shape=jax.ShapeDtypeStruct((S, LANES), F32),
        scratch_shapes=[pltpu.VMEM((1, LANES), F32)],
        compiler_params=_cparams("arbitrary"),
    )(fl, bias)


def _forget_bwd(dn, drow, fl, bias, dh2, *, tb, name):
    S = fl.shape[0]
    tb = min(tb, S)
    nb = S // tb

    def body(dn_ref, dr_ref, fl_ref, b_ref, dh_in, dh_ref, db_ref, carry_ref):
        del dh_in

        @pl.when(pl.program_id(0) == 0)
        def _():
            carry_ref[...] = jnp.zeros_like(carry_ref)
            db_ref[...] = jnp.zeros_like(db_ref)

        src = lax.broadcasted_iota(jnp.int32, (D_MAIN, LANES), 0)
        head = lax.broadcasted_iota(jnp.int32, (D_MAIN, LANES), 1)
        pick = lambda off: jnp.where((src == FOX_HEAD_DIM * head + off) & (head < FOX_HEADS), 1.0, 0.0).astype(BF16)
        hdot = lambda a, sel: sum(jnp.dot(part.astype(BF16), sel, preferred_element_type=F32) for part in _split3(a))
        dcum = hdot(dr_ref[...], pick(3)) - hdot(dn_ref[...], pick(0))
        row = lax.broadcasted_iota(jnp.int32, (tb, LANES), 0)
        c = dcum
        sh = 1
        while sh < tb:
            c = c + jnp.where(row < tb - sh, pltpu.roll(c, tb - sh, 0), 0.0)
            sh *= 2
        dlf = carry_ref[...] + c
        carry_ref[...] += jnp.sum(dcum, axis=0, keepdims=True)
        z = fl_ref[...] + b_ref[...]
        lane = lax.broadcasted_iota(jnp.int32, (tb, LANES), 1)
        dfl = jnp.where(lane < FOX_HEADS, dlf / (1.0 + jnp.exp(z)), 0.0)
        db_ref[...] += jnp.sum(dfl, axis=0, keepdims=True)
        dh_ref[...] = dfl.astype(BF16)

    rev = lambda i: (nb - 1 - i, 0)
    return pl.pallas_call(
        body, name=name, grid=(nb,),
        in_specs=[pl.BlockSpec((tb, D_MAIN), rev), pl.BlockSpec((tb, D_MAIN), rev), pl.BlockSpec((tb, LANES), rev),
                  pl.BlockSpec((1, LANES), lambda i: (0, 0)), pl.BlockSpec(memory_space=pl.ANY)],
        out_specs=[pl.BlockSpec((tb, LANES), lambda i: (nb - 1 - i, 2 * D_MAIN // LANES)),
                   pl.BlockSpec((1, LANES), lambda i: (0, 0))],
        out_shape=[jax.ShapeDtypeStruct(dh2.shape, dh2.dtype), jax.ShapeDtypeStruct((1, LANES), F32)],
        scratch_shapes=[pltpu.VMEM((1, LANES), F32)],
        input_output_aliases={4: 0},
        compiler_params=_cparams("arbitrary"),
    )(dn, drow, fl, bias, dh2)


FOX_SCALE = FOX_HEAD_DIM ** -0.5
LOG2E = 1.4426950408889634
LN2 = 0.6931471805599453
AUX = FOX_HEAD_DIM


def _split3(x):
    hi = x.astype(BF16).astype(F32)
    r = x - hi
    mid = r.astype(BF16).astype(F32)
    return hi, mid, (r - mid).astype(BF16).astype(F32)


def _lanes3(lane, base, parts, rest):
    return jnp.where(lane == base, parts[0], jnp.where(lane == base + 1, parts[1],
                                                       jnp.where(lane == base + 2, parts[2], rest)))


def _swap_halves(x):
    return pltpu.roll(x, FOX_HEAD_DIM, 1)


def _causal_steps(nq, keys_outer):
    if keys_outer:
        pairs = [(i, j) for j in range(nq) for i in range(j, nq)]
    else:
        pairs = [(i, j) for i in range(nq) for j in range(i + 1)]
    it, jt = zip(*pairs)
    return jnp.asarray(np.array(it, np.int32)), jnp.asarray(np.array(jt, np.int32))


def _in_proj_fox(x, w, *, tm, name):
    S = x.shape[0]
    tm = min(tm, S)

    def body(x_ref, w_ref, h_ref, qa_ref):
        acc = jnp.dot(x_ref[...].astype(BF16), w_ref[...], preferred_element_type=F32)
        h_ref[...] = acc.astype(BF16)
        lane = lax.broadcasted_iota(jnp.int32, (tm, LANES), 1)
        first = lane < FOX_HEAD_DIM
        ones_q = jnp.where((lane >= AUX) & (lane < AUX + 3), 1.0, 0.0)
        for g in range(FOX_PAIRS):
            q = acc[:, LANES * g:LANES * (g + 1)] * (FOX_SCALE * LOG2E)
            qa_ref[:, 2 * LANES * g:2 * LANES * g + LANES] = jnp.where(first, q, ones_q).astype(BF16)
            qa_ref[:, 2 * LANES * g + LANES:2 * LANES * (g + 1)] = jnp.where(first, _swap_halves(q), ones_q).astype(BF16)

    return pl.pallas_call(
        body, name=name, grid=(S // tm,),
        in_specs=[pl.BlockSpec((tm, D_MODEL), lambda i: (i, 0)), pl.BlockSpec((D_MODEL, D_IN), lambda i: (0, 0))],
        out_specs=[pl.BlockSpec((tm, D_IN), lambda i: (i, 0)), pl.BlockSpec((tm, 2 * D_MAIN), lambda i: (i, 0))],
        out_shape=[jax.ShapeDtypeStruct((S, D_IN), BF16), jax.ShapeDtypeStruct((S, 2 * D_MAIN), BF16)],
        compiler_params=_cparams("parallel"),
    )(x, w)


def _kv_proj_fox(x, wkvp, negcum, *, tm, name):
    S = x.shape[0]
    tm = min(tm, S)

    def body(x_ref, w_ref, nc_ref, ka_ref, va_ref):
        acc = jnp.dot(x_ref[...].astype(BF16), w_ref[...], preferred_element_type=F32)
        nc = nc_ref[...]
        lane = lax.broadcasted_iota(jnp.int32, (tm, LANES), 1)
        first = lane < FOX_HEAD_DIM
        ones_k = jnp.where((lane >= AUX + 3) & (lane < AUX + 6), 1.0, 0.0)
        for g in range(FOX_PAIRS):
            k = acc[:, 2 * LANES * g:2 * LANES * g + LANES]
            v = acc[:, 2 * LANES * g + LANES:2 * LANES * (g + 1)]
            for hh in range(2):
                sl = slice(LANES * (2 * g + hh), LANES * (2 * g + hh + 1))
                kh, vh = (k, v) if hh == 0 else (_swap_halves(k), _swap_halves(v))
                ncol = jnp.sum(jnp.where(lane == 2 * g + hh, nc, 0.0), axis=1, keepdims=True) * LOG2E
                ka_ref[:, sl] = jnp.where(first, kh, _lanes3(lane, AUX, _split3(ncol), ones_k)).astype(BF16)
                va_ref[:, sl] = jnp.where(first, vh, 1.0).astype(BF16)

    out = pl.BlockSpec((tm, 2 * D_MAIN), lambda i: (i, 0))
    shp = jax.ShapeDtypeStruct((S, 2 * D_MAIN), BF16)
    return pl.pallas_call(
        body, name=name, grid=(S // tm,),
        in_specs=[pl.BlockSpec((tm, D_MODEL), lambda i: (i, 0)), pl.BlockSpec((D_MODEL, 2 * D_MAIN), lambda i: (0, 0)),
                  pl.BlockSpec((tm, LANES), lambda i: (i, 0))],
        out_specs=[out, out], out_shape=[shp, shp],
        compiler_params=_cparams("parallel"),
    )(x, wkvp, negcum)


def _fox_fwd(qa, ka, va, *, tq, name):
    S = qa.shape[0]
    tq = min(tq, S)
    nq = S // tq
    half = tq // 2
    it, jt = _causal_steps(nq, keys_outer=False)

    def body(it_ref, jt_ref, qa_ref, ka_ref, va_ref, y_ref, qb_ref, m_ref, acc_ref):
        n = pl.program_id(1)
        i, j = it_ref[n], jt_ref[n]
        first = lax.broadcasted_iota(jnp.int32, (tq, LANES), 1) < FOX_HEAD_DIM

        @pl.when(j == 0)
        def _():
            m_ref[...] = jnp.full_like(m_ref, NEG)
            acc_ref[...] = jnp.zeros_like(acc_ref)

        def update(hh, rows, nk, masked):
            sl = slice(LANES * hh, LANES * (hh + 1))
            s = lax.dot_general(qa_ref[rows, sl], ka_ref[0:nk, sl], NT, preferred_element_type=F32)
            if masked:
                r = lax.broadcasted_iota(jnp.int32, s.shape, 0) + rows.start
                c = lax.broadcasted_iota(jnp.int32, s.shape, 1)
                s = jnp.where(c <= r, s, NEG)
            m_prev = m_ref[hh, rows]
            m_new = jnp.maximum(m_prev, jnp.max(s, axis=1, keepdims=True))
            p = jnp.exp2(s - jnp.tile(m_new, (1, nk // LANES))).astype(BF16)
            acc_ref[hh, rows] = jnp.exp2(m_prev - m_new) * acc_ref[hh, rows] + jnp.dot(
                p, va_ref[0:nk, sl], preferred_element_type=F32)
            m_ref[hh, rows] = m_new

        @pl.when(j < i)
        def _():
            for hh in range(2):
                update(hh, slice(0, tq), tq, False)

        @pl.when(j == i)
        def _():
            for hh in range(2):
                for r0 in range(0, tq, half):
                    update(hh, slice(r0, r0 + half), r0 + half, True)
            lane = lax.broadcasted_iota(jnp.int32, (tq, LANES), 1)
            ys = []
            for hh in range(2):
                sl = slice(LANES * hh, LANES * (hh + 1))
                a = acc_ref[hh]
                denom = _swap_halves(a)
                ys.append(a / denom)
                lse2 = m_ref[hh] + jnp.log(jnp.where(first, denom, a)) * LOG2E
                qb_ref[:, sl] = _lanes3(lane, AUX + 3, _split3(-lse2), qa_ref[:, sl].astype(F32)).astype(BF16)
            y_ref[...] = jnp.where(first, ys[0], _swap_halves(ys[1]))

    qblock = pl.BlockSpec((tq, 2 * LANES), lambda g, n, it, jt: (it[n], g))
    kblock = pl.BlockSpec((tq, 2 * LANES), lambda g, n, it, jt: (jt[n], g))
    return pl.pallas_call(
        body, name=name,
        grid_spec=pltpu.PrefetchScalarGridSpec(
            num_scalar_prefetch=2, grid=(FOX_PAIRS, it.shape[0]),
            in_specs=[qblock, kblock, kblock],
            out_specs=[pl.BlockSpec((tq, LANES), lambda g, n, it, jt: (it[n], g)), qblock],
            scratch_shapes=[pltpu.VMEM((2, tq, LANES), F32), pltpu.VMEM((2, tq, LANES), F32)]),
        out_shape=[jax.ShapeDtypeStruct((S, D_MAIN), F32), jax.ShapeDtypeStruct((S, 2 * D_MAIN), BF16)],
        compiler_params=_cparams("parallel", "arbitrary"),
    )(it, jt, qa, ka, va)


def _fox_bwd(qb, ka, va, dya, dh, *, tq, name):
    S = qb.shape[0]
    tq = min(tq, S)
    nq = S // tq
    half = tq // 2
    it, jt = _causal_steps(nq, keys_outer=True)
    nsteps = it.shape[0]

    def body(it_ref, jt_ref, qb_ref, ka_ref, va_ref, dya_ref, dh_in, dq_ref, dkv_ref, dn_ref, drow_ref,
             dq_acc, dk_acc, dv_acc):
        del dh_in
        n = pl.program_id(1)
        i, j = it_ref[n], jt_ref[n]
        first = lax.broadcasted_iota(jnp.int32, (tq, LANES), 1) < FOX_HEAD_DIM

        @pl.when(n == 0)
        def _():
            dq_acc[...] = jnp.zeros_like(dq_acc)

        @pl.when(i == j)
        def _():
            dk_acc[...] = jnp.zeros_like(dk_acc)
            dv_acc[...] = jnp.zeros_like(dv_acc)

        def update(hh, keys, q0, masked):
            sl = slice(LANES * hh, LANES * (hh + 1))
            qbh, kah, dyah = qb_ref[q0:tq, sl], ka_ref[keys, sl], dya_ref[q0:tq, sl]
            eT = lax.dot_general(kah, qbh, NT, preferred_element_type=F32)
            if masked:
                r = lax.broadcasted_iota(jnp.int32, eT.shape, 0) + keys.start
                c = lax.broadcasted_iota(jnp.int32, eT.shape, 1) + q0
                eT = jnp.where(r <= c, eT, NEG)
            pT = jnp.exp2(eT)
            dsT = pT * lax.dot_general(va_ref[keys, sl], dyah, NT, preferred_element_type=F32)
            dsb = dsT.astype(BF16)
            dv_acc[hh, keys] += jnp.dot(pT.astype(BF16), dyah, preferred_element_type=F32)
            dk_acc[hh, keys] += jnp.dot(dsb, qbh, preferred_element_type=F32)
            rows = pl.ds(pl.multiple_of(i * tq + q0, half), tq - q0)
            dq_acc[hh, rows, :] += lax.dot_general(dsb, kah, TN, preferred_element_type=F32)

        @pl.when(i > j)
        def _():
            for hh in range(2):
                update(hh, slice(0, tq), 0, False)

        @pl.when(i == j)
        def _():
            for hh in range(2):
                for k0 in range(0, tq, half):
                    update(hh, slice(k0, k0 + half), k0, True)

        @pl.when(i == nq - 1)
        def _():
            dkv_ref[:, :LANES] = (jnp.where(first, dk_acc[0], _swap_halves(dk_acc[1])) * LN2).astype(BF16)
            dkv_ref[:, LANES:] = jnp.where(first, dv_acc[0], _swap_halves(dv_acc[1])).astype(BF16)
            dn_ref[...] = jnp.where(first, _swap_halves(dk_acc[0]), dk_acc[1])

        @pl.when(n == nsteps - 1)
        def _():
            first_s = lax.broadcasted_iota(jnp.int32, (S, LANES), 1) < FOX_HEAD_DIM
            dq_ref[...] = (jnp.where(first_s, dq_acc[0], _swap_halves(dq_acc[1])) * FOX_SCALE).astype(BF16)
            drow_ref[...] = jnp.where(first_s, _swap_halves(dq_acc[0]), dq_acc[1])

    qblock = pl.BlockSpec((tq, 2 * LANES), lambda g, n, it, jt: (it[n], g))
    kblock = pl.BlockSpec((tq, 2 * LANES), lambda g, n, it, jt: (jt[n], g))
    whole = pl.BlockSpec((S, LANES), lambda g, n, it, jt: (0, g))
    return pl.pallas_call(
        body, name=name,
        grid_spec=pltpu.PrefetchScalarGridSpec(
            num_scalar_prefetch=2, grid=(FOX_PAIRS, nsteps),
            in_specs=[qblock, kblock, kblock, qblock, pl.BlockSpec(memory_space=pl.ANY)],
            out_specs=[whole, kblock, pl.BlockSpec((tq, LANES), lambda g, n, it, jt: (jt[n], g)), whole],
            scratch_shapes=[pltpu.VMEM((2, S, LANES), F32), pltpu.VMEM((2, tq, LANES), F32),
                            pltpu.VMEM((2, tq, LANES), F32)]),
        out_shape=[jax.ShapeDtypeStruct(dh.shape, dh.dtype), jax.ShapeDtypeStruct((S, 2 * D_MAIN + LANES), BF16),
                   jax.ShapeDtypeStruct((S, D_MAIN), F32), jax.ShapeDtypeStruct((S, D_MAIN), F32)],
        input_output_aliases={6: 0},
        compiler_params=_cparams("parallel", "arbitrary"),
    )(it, jt, qb, ka, va, dya, dh)


TB_ROWS = 256
TB_SEQ = 512
TQ_FOX_FWD = 1024
TQ_FOX_BWD = 1024


def _local_step(x, mem, target, win0, pscale, next_weights, late_weights, ln_g, ln_b, bias, send_early=None,
                send_late=None):
    ones = jnp.ones((1, D_MAIN), F32)
    g0, b0, g1, b1 = ln_g[0:1], ln_b[0:1], ln_g[1:2], ln_b[1:2]
    mm = lambda a, b, mode, dt, tm, tn, tk, name, **kw: _mm(a, b, mode=mode, out_dtype=dt, tm=tm, tn=tn, tk=tk,
                                                            name=name, **kw)

    h0 = mm(x, win0, "nn", BF16, 256, D_IN, D_MODEL, "l0_in")
    wmkv, pw, wout0 = next_weights(h0)
    pm, mixed = _pool_fwd(h0, pw, tb=TB_SEQ, name="l0_pool_fwd")
    mkv0 = mm(mem, wmkv[0], "nn", F32, 256, 1024, 1024, "l0_mkv")
    ymem0 = _memattn_fwd(h0, mkv0, tb=TB_SEQ, name="l0_mem_fwd")
    yg0 = _gate_fwd(mixed, pscale, h0, ymem0, tb=TB_ROWS, name="l0_gate_fwd")
    x1, xhat0, rstd0 = _out_ln(yg0, wout0, x, g0, b0, tb=TB_SEQ, name="l0_out_ln")
    win1, wout1, wkvp, wf = late_weights(x1)

    fl = mm(x1, wf, "nn", F32, 512, LANES, D_MODEL, "f_proj")
    negcum = _forget_fwd(fl, bias, tb=TB_SEQ, name="forget_fwd")
    ka, va = _kv_proj_fox(x1, wkvp, negcum, tm=512, name="kv_proj")

    h1, qa = _in_proj_fox(x1, win1, tm=256, name="l1_in")
    y1, qb = _fox_fwd(qa, ka, va, tq=TQ_FOX_FWD, name="fox_fwd")
    mkv1 = mm(mem, wmkv[1], "nn", F32, 256, 1024, 1024, "l1_mkv")
    ymem1 = _memattn_fwd(h1, mkv1, tb=TB_SEQ, name="l1_mem_fwd")
    yg1 = _gate_fwd(y1, ones, h1, ymem1, tb=TB_ROWS, name="l1_gate_fwd")
    dz1, dg1, db1, sq = _out_ln_loss(yg1, wout1, x1, g1, b1, target, tb=TB_SEQ, name="l1_out_ln_loss")

    dwout1 = mm(yg1, dz1, "tn", BF16, D_MIX, D_MODEL, 512, "l1_dwout")
    dyg1 = mm(dz1, wout1, "nt", BF16, 512, D_MIX, D_MODEL, "l1_dyg")
    dya, dymem1, dh1 = _gate_bwd(dyg1, y1, ones, h1, ymem1, tb=TB_ROWS, name="l1_gate_bwd", fox=True)
    dh1, dh2, dnp, drowp = _fox_bwd(qb, ka, va, dya, dh1, tq=TQ_FOX_BWD, name="fox_bwd")
    dh2, dbias = _forget_bwd(dnp, drowp, fl, bias, dh2, tb=TB_SEQ, name="forget_bwd")
    dh1, dmkv1 = _memattn_bwd(h1, mkv1, dymem1, dh1, tb=TB_SEQ, name="l1_mem_bwd")
    dwmkv1 = mm(mem, dmkv1, "tn", BF16, D_MODEL, 1024, N_MEM, "l1_dwmkv")
    dwin1 = mm(x1, dh1, "tn", BF16, D_MODEL, D_IN // 2, 512, "l1_dwin")
    dwkvf = mm(x1, dh2, "tn", F32, D_MODEL, 2 * D_MAIN + LANES, 512, "dwkv")
    dkv = dwkvf[:, :2 * D_MAIN].reshape(D_MODEL, FOX_PAIRS, 2, LANES)
    dwkv = jnp.concatenate([dkv[:, :, 0, :].reshape(D_MODEL, D_MAIN), dkv[:, :, 1, :].reshape(D_MODEL, D_MAIN),
                            dwkvf[:, 2 * D_MAIN:2 * D_MAIN + FOX_HEADS]], axis=1)
    dwkv = dwkv.reshape(D_MODEL, N_DEV, -1).transpose(1, 0, 2).astype(BF16)
    anchor = send_early(dict(w_out=dwout1, w_mem_kv=dwmkv1, w_in=dwin1, w_kv_shared=dwkv)) if send_early else 0.0
    dx1 = mm(dh1, win1, "nt", F32, 256, D_MODEL, D_IN, "l1_dx", add=dz1, add_scale=ALPHA)
    wkvf = jnp.concatenate([wkvp, wf], axis=1)
    dx1 = mm(dh2, wkvf, "nt", F32, 256, D_MODEL, 2 * D_MAIN + LANES, "kv_dx", add=dx1)

    dz0, dg0, db0 = _ln_bwd(dx1, xhat0, rstd0, g0 + anchor, tb=TB_ROWS, name="l0_ln_bwd")
    dwout0 = mm(yg0, dz0, "tn", BF16, D_MIX, D_MODEL, 512, "l0_dwout")
    dyg0 = mm(dz0, wout0, "nt", BF16, 512, D_MIX, D_MODEL, "l0_dyg")
    dy0, dymem0, dh0 = _gate_bwd(dyg0, mixed, pscale, h0, ymem0, tb=TB_ROWS, name="l0_gate_bwd")
    dh0, dpw, dpscale = _pool_bwd(dy0, pm, mixed, pw, pscale, dh0, tb=TB_SEQ, name="l0_pool_bwd")
    dh0, dmkv0 = _memattn_bwd(h0, mkv0, dymem0, dh0, tb=TB_SEQ, name="l0_mem_bwd")
    dwmkv0 = mm(mem, dmkv0, "tn", BF16, D_MODEL, 1024, N_MEM, "l0_dwmkv")
    dwin0 = mm(x, dh0, "tn", BF16, D_MODEL, D_IN // 2, 512, "l0_dwin")
    grads = dict(w_in=(dwin0, dwin1), w_mem_kv=(dwmkv0, dwmkv1), w_out=(dwout0, dwout1), pool_w=(dpw,),
                 w_kv_shared=dwkv, pool_scale=dpscale, ln_g=jnp.concatenate([dg0, dg1]),
                 ln_b=jnp.concatenate([db0, db1]), b_forget=dbias[0, :FOX_HEADS])
    if send_late:
        dz0 = send_late(grads, dz0)
    gx = mm(dh0, win0, "nt", F32, 256, D_MODEL, D_IN, "l0_dx", add=dz0, add_scale=ALPHA)
    return sq, gx, grads


MESH_ID = pl.DeviceIdType.MESH
HBM = pl.BlockSpec(memory_space=pl.ANY)
SLICED = {"w_in": (2, D_IN // N_DEV), "w_mem_kv": (1, D_MODEL // N_DEV), "w_out": (1, D_MIX // N_DEV),
          "pool_w": (1, POOL_GROUP // N_DEV)}


def _place():
    return lax.axis_index("x"), lax.axis_index("y"), lax.axis_index("c")


def _slot(p):
    return 4 * p[0] + 2 * p[1] + p[2]


def _cut(ref, axis, width, s):
    idx = [slice(None)] * len(ref.shape)
    idx[axis] = pl.ds(s * width, width)
    return ref.at[tuple(idx)]


def _all_gather(shards, cuts, *, name):
    nt = len(shards)

    def full_shape(a, cut):
        if cut is None:
            return (N_DEV,) + a.shape
        return a.shape[:cut[0]] + (a.shape[cut[0]] * N_DEV,) + a.shape[cut[0] + 1:]

    def body(*refs):
        ins, outs = refs[:nt], refs[nt:2 * nt]
        send_sems, recv_sems, local_sems = refs[2 * nt:]
        x, y, c = _place()
        me, sibling = (x, y, c), (x, y, 1 - c)
        chips = [(1 - x, y), (x, 1 - y), (1 - x, 1 - y)]

        def place(t, s):
            return outs[t].at[s] if cuts[t] is None else _cut(outs[t], cuts[t][0], cuts[t][1], s)

        def copies(k, block, to, from_input=False):
            s = _slot(block)
            return [pltpu.make_async_remote_copy(
                src_ref=ins[t] if from_input else place(t, s), dst_ref=place(t, s),
                send_sem=send_sems.at[nt * k + t], recv_sem=recv_sems.at[nt * k + t],
                device_id=to, device_id_type=MESH_ID) for t in range(nt)]

        mine = [pltpu.make_async_copy(ins[t], place(t, _slot(me)), local_sems.at[t]) for t in range(nt)]
        for cp in mine:
            cp.start()
        first = [copies(0, me, sibling, True)] + [copies(1 + j, me, (*chip, c), True) for j, chip in enumerate(chips)]
        for group in first:
            for cp in group:
                cp.start()
        passed = [copies(4 + j, (*chip, c), sibling) for j, chip in enumerate(chips)]
        for j, chip in enumerate(chips):
            for cp in copies(1 + j, (*chip, c), me):
                cp.wait_recv()
            for cp in passed[j]:
                cp.start()
        for cp in copies(0, sibling, me):
            cp.wait_recv()
        for j, chip in enumerate(chips):
            for cp in copies(4 + j, (*chip, 1 - c), me):
                cp.wait_recv()
        for group in first + passed:
            for cp in group:
                cp.wait_send()
        for cp in mine:
            cp.wait()

    return pl.pallas_call(
        body, name=name, in_specs=[HBM] * nt, out_specs=[HBM] * nt,
        out_shape=[jax.ShapeDtypeStruct(full_shape(a, cut), a.dtype) for a, cut in zip(shards, cuts)],
        scratch_shapes=[pltpu.SemaphoreType.DMA((7 * nt,)), pltpu.SemaphoreType.DMA((7 * nt,)),
                        pltpu.SemaphoreType.DMA((nt,))],
    )(*shards)


def _exchange_copies(items, ins, outs, send_sems, recv_sems, local_sems, gather=False):
    nt = len(items)
    x, y, c = _place()
    me = _slot((x, y, c))
    flip = lambda v, bit: 1 - v if bit else v

    def part(ref, cut, s):
        return ref.at[s] if cut is None else _cut(ref, cut[0], cut[1], s)

    def src(t, s):
        return ins[t] if gather else part(ins[t], items[t][0], s)

    def dst(t, s):
        if gather:
            return part(outs[items[t][1]], items[t][0], s)
        d = outs[items[t][1]].at[s]
        return d if items[t][2] is None else d.at[items[t][2]]

    sends, arrivals = [], []
    for k in range(1, N_DEV):
        peer = (flip(x, k & 4), flip(y, k & 2), flip(c, k & 1))
        ps = _slot(peer)
        for t in range(nt):
            sems = dict(send_sem=send_sems.at[nt * (k - 1) + t], recv_sem=recv_sems.at[nt * (k - 1) + t],
                        device_id=peer, device_id_type=MESH_ID)
            sends.append(pltpu.make_async_remote_copy(src_ref=src(t, ps), dst_ref=dst(t, me), **sems))
            arrivals.append(pltpu.make_async_remote_copy(src_ref=src(t, ps), dst_ref=dst(t, ps), **sems))
    mine = [pltpu.make_async_copy(src(t, me), dst(t, me), local_sems.at[t]) for t in range(nt)]
    return sends, arrivals, mine


SEMS = pl.BlockSpec(memory_space=pltpu.SEMAPHORE)
SIDE_EFFECT = pltpu.SideEffectType.DATAFLOW_SIDE_EFFECTING


def _exchange_start(srcs, items, landings, *, name, gather=False, carry=()):
    nt, nl, nc = len(srcs), len(landings), len(carry)

    def body(*refs):
        ins, lands = refs[:nt], refs[nt:nt + nl]
        send_sems, recv_sems, local_sems = refs[nt + nl + nc:nt + nl + nc + 3]
        token = refs[-1]
        sends, _, mine = _exchange_copies(items, ins, lands, send_sems, recv_sems, local_sems, gather)
        for cp in sends + mine:
            cp.start()
        token[...] = jnp.zeros_like(token)

    hbm = lambda a: pltpu.HBM(a.shape, a.dtype)
    fresh = [pltpu.with_memory_space_constraint(
        lax.empty(l.shape, l.dtype) if isinstance(l, jax.ShapeDtypeStruct) else l, pltpu.HBM) for l in landings]
    res = pl.pallas_call(
        body, name=name, in_specs=[HBM] * (nt + nl + nc),
        out_specs=[SEMS, SEMS, SEMS] + [HBM] * (nt + nl + nc) + [pl.BlockSpec(memory_space=pltpu.VMEM)],
        out_shape=[pltpu.SemaphoreType.DMA((7 * nt,)), pltpu.SemaphoreType.DMA((7 * nt,)), pltpu.SemaphoreType.DMA((nt,))]
        + [hbm(a) for a in srcs] + [hbm(l) for l in landings] + [hbm(a) for a in carry]
        + [jax.ShapeDtypeStruct((8, LANES), F32)],
        input_output_aliases={i: 3 + i for i in range(nt + nl + nc)},
        compiler_params=pltpu.CompilerParams(has_side_effects=SIDE_EFFECT),
    )(*[pltpu.with_memory_space_constraint(a, pltpu.HBM) for a in srcs], *fresh,
      *[pltpu.with_memory_space_constraint(a, pltpu.HBM) for a in carry])
    return res[:3 + nt + nl], res[-1][0:1, 0:1], res[3 + nt + nl:-1]


def _exchange_wait(state, items, nt, after, *, name, gather=False):
    sems, bufs = state[:3], state[3:]
    nl = len(bufs) - nt

    def body(*refs):
        ins, lands = refs[:nt], refs[nt:nt + nl]
        send_sems, recv_sems, local_sems = refs[nt + nl:nt + nl + 3]
        sends, arrivals, mine = _exchange_copies(items, ins, lands, send_sems, recv_sems, local_sems, gather)
        for sent, landed in zip(sends, arrivals):
            landed.wait_recv()
            sent.wait_send()
        for cp in mine:
            cp.wait()

    hbm = lambda a: pltpu.HBM(a.shape, a.dtype)
    res = pl.pallas_call(
        body, name=name, in_specs=[HBM] * (nt + nl) + [SEMS, SEMS, SEMS, HBM], out_specs=[HBM] * (nt + nl),
        out_shape=[hbm(a) for a in bufs],
        input_output_aliases={i: i for i in range(nt + nl)},
        compiler_params=pltpu.CompilerParams(has_side_effects=SIDE_EFFECT),
    )(*bufs, *sems, after)
    return res[nt:]


def _adamw(recv, w, m, v, *, split, name):
    shape = w.shape
    axis, parts = split
    block = shape[:axis] + (shape[axis] // parts,) + shape[axis + 1:]
    nd = len(shape)

    def body(r_ref, w_ref, m_ref, v_ref, g_ref, d_ref, nm_ref, nv_ref):
        g = r_ref[0].astype(F32)
        for j in range(1, N_DEV):
            g = g + r_ref[j].astype(F32)
        nm = ADAM_B1 * m_ref[...] + (1.0 - ADAM_B1) * g
        nv = ADAM_B2 * v_ref[...] + (1.0 - ADAM_B2) * (g * g)
        m_hat = nm / (1.0 - ADAM_B1 ** ADAM_STEP)
        v_hat = nv / (1.0 - ADAM_B2 ** ADAM_STEP)
        g_ref[...] = g
        nm_ref[...] = nm
        nv_ref[...] = nv
        d_ref[...] = -ADAM_LR * (m_hat / (jnp.sqrt(v_hat) + ADAM_EPS) + ADAM_WD * w_ref[...])

    at = lambda i: tuple(i if a == axis else 0 for a in range(nd))
    one = pl.BlockSpec(block, at)
    shp = jax.ShapeDtypeStruct(shape, F32)
    return pl.pallas_call(
        body, name=name, grid=(parts,),
        in_specs=[pl.BlockSpec((N_DEV,) + block, lambda i: (0,) + at(i)), one, one, one],
        out_specs=[one, one, one, one], out_shape=[shp, shp, shp, shp],
        compiler_params=_cparams("parallel"),
    )(recv, w, m, v)


BIG = ("w_in", "w_mem_kv", "w_out", "pool_w", "w_kv_shared")
SMALL = ("pool_scale", "ln_g", "ln_b", "b_forget")
SMALL_ROWS = 40
ADAM_SPLIT = {"w_in": (1, 4), "w_mem_kv": (0, 2), "w_out": (0, 2), "pool_w": (0, 1), "w_kv_shared": (0, 4)}
PER_LAYER_CUT = {"w_out": (0, D_MIX // N_DEV), "w_mem_kv": (0, D_MODEL // N_DEV), "w_in": (1, D_IN // N_DEV),
                 "pool_w": (1, POOL_GROUP // N_DEV)}
EARLY = ("w_out", "w_mem_kv", "w_in", "w_kv_shared")
EARLY_ITEMS = [(PER_LAYER_CUT[n], i, 1) for i, n in enumerate(EARLY[:3])] + [(None, 3, None)]


def _flat(parts, rows):
    v = jnp.concatenate([p.reshape(-1) for p in parts])
    return jnp.pad(v, (0, rows * LANES - v.shape[0])).reshape(rows, LANES)


def _unflat(flat, shapes):
    v, out, off = flat.reshape(-1), [], 0
    for s in shapes:
        n = math.prod(s)
        out.append(v[off:off + n].reshape(s))
        off += n
    return out


def kernel(x, mem, w_in, w_mem_kv, w_out, ln_g, ln_b, pool_w, pool_scale, w_kv_shared, b_forget, loss_target, m_w_in, m_w_mem_kv, m_w_out, m_ln_g, m_ln_b, m_pool_w, m_pool_scale, m_w_kv_shared, m_b_forget, v_w_in, v_w_mem_kv, v_w_out, v_ln_g, v_ln_b, v_pool_w, v_pool_scale, v_w_kv_shared, v_b_forget):
    w = dict(w_in=w_in, w_mem_kv=w_mem_kv, w_out=w_out, ln_g=ln_g, ln_b=ln_b, pool_w=pool_w[0],
             pool_scale=pool_scale, w_kv_shared=w_kv_shared, b_forget=b_forget)
    m = dict(w_in=m_w_in, w_mem_kv=m_w_mem_kv, w_out=m_w_out, ln_g=m_ln_g, ln_b=m_ln_b, pool_w=m_pool_w[0],
             pool_scale=m_pool_scale, w_kv_shared=m_w_kv_shared, b_forget=m_b_forget)
    v = dict(w_in=v_w_in, w_mem_kv=v_w_mem_kv, w_out=v_w_out, ln_g=v_ln_g, ln_b=v_ln_b, pool_w=v_pool_w[0],
             pool_scale=v_pool_scale, w_kv_shared=v_w_kv_shared, b_forget=v_b_forget)

    wb = {n: w[n].astype(BF16) for n in BIG}
    bdt = wb["w_in"].dtype
    win0, pscale = _all_gather([wb["w_in"][0], jnp.pad(pool_scale, ((0, 7), (0, 0)))], [PER_LAYER_CUT["w_in"], None],
                               name="gather_weights")
    pscale = pscale[:, 0, :].reshape(1, D_MAIN)
    next_srcs = [wb["w_mem_kv"], wb["pool_w"], wb["w_out"][0]]
    next_items = [(SLICED["w_mem_kv"], 0, None), (SLICED["pool_w"], 1, None), (PER_LAYER_CUT["w_out"], 2, None)]
    next_state, _, (win0,) = _exchange_start(
        next_srcs, next_items,
        [jax.ShapeDtypeStruct((2, D_MODEL, D_MODEL), bdt), jax.ShapeDtypeStruct((4, POOL_GROUP, POOL_GROUP), bdt),
         jax.ShapeDtypeStruct((D_MIX, D_MODEL), bdt)], name="gather_next_start", gather=True, carry=[win0])
    late_srcs = [wb["w_in"][1], wb["w_out"][1], wb["w_kv_shared"]]
    late_items = [(PER_LAYER_CUT["w_in"], 0, None), (PER_LAYER_CUT["w_out"], 1, None), (None, 2, None)]
    late = {}
    bias = jnp.pad(b_forget, (0, LANES - FOX_HEADS)).reshape(1, LANES)

    def next_weights(h0):
        wmkv, pw, wout0 = _exchange_wait(next_state, next_items, len(next_srcs), h0, name="gather_next_wait", gather=True)
        late["state"], _, (pw,) = _exchange_start(
            late_srcs, late_items,
            [jax.ShapeDtypeStruct((D_MODEL, D_IN), bdt), jax.ShapeDtypeStruct((D_MIX, D_MODEL), bdt),
             jax.ShapeDtypeStruct((N_DEV,) + w_kv_shared.shape, bdt)], name="gather_late_start", gather=True, carry=[pw])
        return wmkv, pw, wout0

    def late_weights(x1):
        win1, wout1, wkv = _exchange_wait(late["state"], late_items, len(late_srcs), x1, name="gather_late_wait",
                                          gather=True)
        wkv = wkv.transpose(1, 0, 2).reshape(D_MODEL, -1)
        wkvp = jnp.stack([wkv[:, :D_MAIN].reshape(D_MODEL, FOX_PAIRS, LANES),
                          wkv[:, D_MAIN:2 * D_MAIN].reshape(D_MODEL, FOX_PAIRS, LANES)], axis=2).reshape(D_MODEL, 2 * D_MAIN)
        return win1, wout1, wkvp, jnp.pad(wkv[:, 2 * D_MAIN:], ((0, 0), (0, LANES - FOX_HEADS)))

    early = {}

    def send_early(g):
        srcs = [g[n] for n in EARLY]
        lands = [jax.ShapeDtypeStruct((N_DEV, 2) + w[n].shape[1:], g[n].dtype) for n in EARLY[:3]]
        lands.append(jax.ShapeDtypeStruct(g["w_kv_shared"].shape, g["w_kv_shared"].dtype))
        early["state"], anchor, _ = _exchange_start(srcs, EARLY_ITEMS, lands, name="exchange_early_start")
        return anchor

    last_items = [(cut, i, 0) for i, (cut, _, _) in enumerate(EARLY_ITEMS[:3])] + [(PER_LAYER_CUT["pool_w"], 3, None),
                                                                                  (None, 4, None)]

    def send_late(g, dz0):
        small = jnp.concatenate([g["pool_scale"].reshape(N_DEV, -1)]
                                + [jnp.broadcast_to(g[n].reshape(1, -1), (N_DEV, g[n].size)) for n in SMALL[1:]], axis=1)
        small = jnp.pad(small, ((0, 0), (0, SMALL_ROWS * LANES - small.shape[1]))).reshape(N_DEV, SMALL_ROWS, LANES)
        r_out, r_mkv, r_in, early["r_kv"] = _exchange_wait(early["state"], EARLY_ITEMS, len(EARLY), g["w_in"][0],
                                                           name="exchange_early_wait")
        srcs = [g[n][0] for n in EARLY[:3]] + [g["pool_w"][0], small]
        early["last"], _, (dz0,) = _exchange_start(
            srcs, last_items,
            [r_out, r_mkv, r_in, jax.ShapeDtypeStruct((N_DEV,) + w["pool_w"].shape, srcs[3].dtype),
             jax.ShapeDtypeStruct(small.shape, small.dtype)], name="exchange_last_start", carry=[dz0])
        return dz0

    sq, gx, grads = _local_step(x[0], mem[0], loss_target[0], win0, pscale, next_weights, late_weights,
                                ln_g, ln_b, bias, send_early, send_late)
    loss = lax.psum((0.5 / D_MODEL) * jnp.sum(sq), ("x", "y", "c"))

    r_out, r_mkv, r_in, r_pw, r_small = _exchange_wait(early["last"], last_items, len(last_items), gx,
                                                       name="exchange_last_wait")
    recv = dict(w_out=r_out, w_mem_kv=r_mkv, w_in=r_in, pool_w=r_pw, w_kv_shared=early["r_kv"], small=r_small)

    outs = {}
    for n in BIG:
        res = _adamw(recv[n], w[n], m[n], v[n], split=ADAM_SPLIT[n], name="adamw_" + n)
        for kind, a in zip(("grad", "delta", "new_m", "new_v"), res):
            outs[kind, n] = a[None] if n == "pool_w" else a
    small_shapes = [w[n].shape for n in SMALL]
    res = _adamw(recv["small"], _flat([w[n] for n in SMALL], SMALL_ROWS), _flat([m[n] for n in SMALL], SMALL_ROWS),
                 _flat([v[n] for n in SMALL], SMALL_ROWS), split=(0, 1), name="adamw_small")
    for kind, flat in zip(("grad", "delta", "new_m", "new_v"), res):
        for n, a in zip(SMALL, _unflat(flat, small_shapes)):
            outs[kind, n] = a
    order = ("w_in", "w_mem_kv", "w_out", "ln_g", "ln_b", "pool_w", "pool_scale", "w_kv_shared", "b_forget")
    return (loss, gx[None], *[outs[kind, n] for kind in ("grad", "delta", "new_m", "new_v") for n in order])
```

```python
import math

import numpy as np
import jax
import jax.numpy as jnp
from jax import lax
from jax.experimental import pallas as pl
from jax.experimental.pallas import tpu as pltpu

F32 = jnp.float32
BF16 = jnp.bfloat16

D_MODEL = 1024
D_MAIN = 1024
D_MEM = 512
D_MIX = D_MAIN + D_MEM
D_IN = 2 * D_MIX
N_MEM = 256
MEM_HEADS = 4
MEM_HEAD_DIM = 128
FOX_HEADS = 16
FOX_HEAD_DIM = 64
FOX_PAIRS = FOX_HEADS // 2
POOL_WINDOWS = (2, 4, 8, 16)
POOL_GROUP = 256
POOL_HALO = 16
ALPHA = 4.0 ** 0.25
LN_EPS = 1e-5
NEG = -1e30
LANES = 128
N_DEV = 8

ADAM_LR = 0.001
ADAM_B1 = 0.9
ADAM_B2 = 0.999
ADAM_EPS = 1e-08
ADAM_WD = 0.01
ADAM_STEP = 10

VMEM_LIMIT = 56 * 1024 * 1024

NN = (((1,), (0,)), ((), ()))
NT = (((1,), (1,)), ((), ()))
TN = (((0,), (0,)), ((), ()))


def _cparams(*sem):
    return pltpu.CompilerParams(dimension_semantics=sem, vmem_limit_bytes=VMEM_LIMIT)


def _sigmoid(z):
    return 1.0 / (1.0 + jnp.exp(-z))


def _mm(a, b, *, mode, out_dtype, tm, tn, tk, name, add=None, add_scale=1.0):
    if mode == "nn":
        (M, K), (K2, N) = a.shape, b.shape
    elif mode == "nt":
        (M, K), (N, K2) = a.shape, b.shape
    else:
        (K, M), (K2, N) = a.shape, b.shape
    assert K == K2, (a.shape, b.shape, mode)
    tm, tn, tk = min(tm, M), min(tn, N), min(tk, K)
    assert M % tm == 0 and N % tn == 0 and K % tk == 0, (M, N, K, tm, tn, tk)
    gm, gn, gk = M // tm, N // tn, K // tk
    dims = {"nn": NN, "nt": NT, "tn": TN}[mode]
    if mode == "tn":
        a_spec = pl.BlockSpec((tk, tm), lambda i, j, k: (k, i))
    else:
        a_spec = pl.BlockSpec((tm, tk), lambda i, j, k: (i, k))
    if mode == "nt":
        b_spec = pl.BlockSpec((tn, tk), lambda i, j, k: (j, k))
    else:
        b_spec = pl.BlockSpec((tk, tn), lambda i, j, k: (k, j))
    o_spec = pl.BlockSpec((tm, tn), lambda i, j, k: (i, j))
    has_add = add is not None
    acc_in_out = out_dtype == F32

    def body(*refs):
        a_ref, b_ref = refs[0], refs[1]
        add_ref = refs[2] if has_add else None
        o_ref = refs[3] if has_add else refs[2]
        prod = lax.dot_general(a_ref[...].astype(BF16), b_ref[...].astype(BF16), dims,
                               preferred_element_type=F32)

        def finish(r):
            if has_add:
                r = r + add_scale * add_ref[...]
            o_ref[...] = r.astype(out_dtype)

        if gk == 1:
            finish(prod)
        else:
            acc_ref = o_ref if acc_in_out else refs[-1]
            k = pl.program_id(2)

            @pl.when(k == 0)
            def _():
                acc_ref[...] = prod

            @pl.when(k > 0)
            def _():
                acc_ref[...] += prod

            if has_add or not acc_in_out:
                @pl.when(k == gk - 1)
                def _():
                    finish(acc_ref[...])

    in_specs = [a_spec, b_spec] + ([o_spec] if has_add else [])
    args = (a, b) + ((add,) if has_add else ())
    return pl.pallas_call(
        body, name=name, grid=(gm, gn, gk), in_specs=in_specs, out_specs=o_spec,
        out_shape=jax.ShapeDtypeStruct((M, N), out_dtype),
        scratch_shapes=[pltpu.VMEM((tm, tn), F32)] if gk > 1 and not acc_in_out else [],
        compiler_params=_cparams("parallel", "parallel", "arbitrary"),
    )(*args)


def _ln_stats(z):
    mu = jnp.mean(z, axis=1, keepdims=True)
    zc = z - mu
    var = jnp.mean(zc * zc, axis=1, keepdims=True)
    rstd = lax.rsqrt(var + LN_EPS)
    return zc * rstd, rstd


def _ln_bwd_math(dy, xhat, rstd, g):
    dxh = dy * g
    m1 = jnp.mean(dxh, axis=1, keepdims=True)
    m2 = jnp.mean(dxh * xhat, axis=1, keepdims=True)
    return rstd * (dxh - m1 - xhat * m2)


def _out_ln(yg, wout, x, g, b, *, tb, name):
    S = x.shape[0]
    tb = min(tb, S)

    def body(yg_ref, w_ref, x_ref, g_ref, b_ref, y_ref, xhat_ref, rstd_ref):
        o = jnp.dot(yg_ref[...], w_ref[...], preferred_element_type=F32)
        xhat, rstd = _ln_stats(ALPHA * x_ref[...] + o)
        xhat_ref[...] = xhat
        rstd_ref[...] = rstd
        y_ref[...] = xhat * g_ref[...] + b_ref[...]

    row = pl.BlockSpec((tb, D_MODEL), lambda i: (i, 0))
    vec = pl.BlockSpec((1, D_MODEL), lambda i: (0, 0))
    return pl.pallas_call(
        body, name=name, grid=(S // tb,),
        in_specs=[pl.BlockSpec((tb, D_MIX), lambda i: (i, 0)), pl.BlockSpec((D_MIX, D_MODEL), lambda i: (0, 0)),
                  row, vec, vec],
        out_specs=[row, row, pl.BlockSpec((tb, 1), lambda i: (i, 0))],
        out_shape=[jax.ShapeDtypeStruct((S, D_MODEL), F32), jax.ShapeDtypeStruct((S, D_MODEL), F32),
                   jax.ShapeDtypeStruct((S, 1), F32)],
        compiler_params=_cparams("parallel"),
    )(yg, wout, x, g, b)


def _out_ln_loss(yg, wout, x, g, b, target, *, tb, name):
    S = x.shape[0]
    tb = min(tb, S)

    def body(yg_ref, w_ref, x_ref, g_ref, b_ref, t_ref, dz_ref, dg_ref, db_ref, sq_ref):
        @pl.when(pl.program_id(0) == 0)
        def _():
            dg_ref[...] = jnp.zeros_like(dg_ref)
            db_ref[...] = jnp.zeros_like(db_ref)
            sq_ref[...] = jnp.zeros_like(sq_ref)

        o = jnp.dot(yg_ref[...], w_ref[...], preferred_element_type=F32)
        xhat, rstd = _ln_stats(ALPHA * x_ref[...] + o)
        err = xhat * g_ref[...] + b_ref[...] - t_ref[...]
        sq_ref[...] += jnp.sum(err * err, axis=0, keepdims=True)
        dy = err * (1.0 / D_MODEL)
        dz_ref[...] = _ln_bwd_math(dy, xhat, rstd, g_ref[...])
        dg_ref[...] += jnp.sum(dy * xhat, axis=0, keepdims=True)
        db_ref[...] += jnp.sum(dy, axis=0, keepdims=True)

    row = pl.BlockSpec((tb, D_MODEL), lambda i: (i, 0))
    vec = pl.BlockSpec((1, D_MODEL), lambda i: (0, 0))
    vshape = jax.ShapeDtypeStruct((1, D_MODEL), F32)
    return pl.pallas_call(
        body, name=name, grid=(S // tb,),
        in_specs=[pl.BlockSpec((tb, D_MIX), lambda i: (i, 0)), pl.BlockSpec((D_MIX, D_MODEL), lambda i: (0, 0)),
                  row, vec, vec, row],
        out_specs=[row, vec, vec, vec],
        out_shape=[jax.ShapeDtypeStruct((S, D_MODEL), F32), vshape, vshape, vshape],
        compiler_params=_cparams("arbitrary"),
    )(yg, wout, x, g, b, target)


def _ln_bwd(dy, xhat, rstd, g, *, tb, name):
    S = dy.shape[0]
    tb = min(tb, S)

    def body(dy_ref, xhat_ref, rstd_ref, g_ref, dz_ref, dg_ref, db_ref):
        @pl.when(pl.program_id(0) == 0)
        def _():
            dg_ref[...] = jnp.zeros_like(dg_ref)
            db_ref[...] = jnp.zeros_like(db_ref)

        dy_, xhat_ = dy_ref[...], xhat_ref[...]
        dz_ref[...] = _ln_bwd_math(dy_, xhat_, rstd_ref[...], g_ref[...])
        dg_ref[...] += jnp.sum(dy_ * xhat_, axis=0, keepdims=True)
        db_ref[...] += jnp.sum(dy_, axis=0, keepdims=True)

    row = pl.BlockSpec((tb, D_MODEL), lambda i: (i, 0))
    vec = pl.BlockSpec((1, D_MODEL), lambda i: (0, 0))
    return pl.pallas_call(
        body, name=name, grid=(S // tb,),
        in_specs=[row, row, pl.BlockSpec((tb, 1), lambda i: (i, 0)), vec],
        out_specs=[row, vec, vec],
        out_shape=[jax.ShapeDtypeStruct((S, D_MODEL), F32), jax.ShapeDtypeStruct((1, D_MODEL), F32),
                   jax.ShapeDtypeStruct((1, D_MODEL), F32)],
        compiler_params=_cparams("arbitrary"),
    )(dy, xhat, rstd, g)


def _gate_fwd(ysrc, scale, h, ymem, *, tb, name):
    S = ysrc.shape[0]
    tb = min(tb, S)

    def body(ys_ref, sc_ref, ga_ref, gb_ref, gc_ref, ym_ref, yg_ref):
        ymain = ys_ref[...] * sc_ref[...]
        for k, g_ref in enumerate((ga_ref, gb_ref)):
            gv = g_ref[...].astype(F32)
            yg_ref[:, 512 * k:512 * (k + 1)] = (ymain[:, 512 * k:512 * (k + 1)] * gv * _sigmoid(gv)).astype(BF16)
        gv = gc_ref[...].astype(F32)
        yg_ref[:, 1024:1536] = (ym_ref[...] * gv * _sigmoid(gv)).astype(BF16)

    slab = lambda c: pl.BlockSpec((tb, 512), lambda i, c=c: (i, c))
    return pl.pallas_call(
        body, name=name, grid=(S // tb,),
        in_specs=[pl.BlockSpec((tb, D_MAIN), lambda i: (i, 0)), pl.BlockSpec((1, D_MAIN), lambda i: (0, 0)),
                  slab(3), slab(4), slab(5), pl.BlockSpec((tb, D_MEM), lambda i: (i, 0))],
        out_specs=pl.BlockSpec((tb, D_MIX), lambda i: (i, 0)),
        out_shape=jax.ShapeDtypeStruct((S, D_MIX), BF16),
        compiler_params=_cparams("parallel"),
    )(ysrc, scale, h, h, h, ymem)


def _gate_bwd(dyg, ysrc, scale, h, ymem, *, tb, name, fox=False):
    S = ysrc.shape[0]
    tb = min(tb, S)

    def dsilu(gv):
        sg = _sigmoid(gv)
        return sg, sg * (1.0 + gv * (1.0 - sg))

    def body(da_ref, db_ref, dc_ref, ys_ref, sc_ref, ga_ref, gb_ref, gc_ref, ym_ref, dym_ref, dymem_ref, dh_ref):
        ymain = ys_ref[...] * sc_ref[...]
        lane = lax.broadcasted_iota(jnp.int32, (tb, LANES), 1)
        first = lane < FOX_HEAD_DIM
        for k, (d_ref, g_ref) in enumerate(((da_ref, ga_ref), (db_ref, gb_ref))):
            gv, d = g_ref[...].astype(F32), d_ref[...].astype(F32)
            sg, ds = dsilu(gv)
            dy = d * gv * sg
            dh_ref[:, 512 * k:512 * (k + 1)] = (d * ymain[:, 512 * k:512 * (k + 1)] * ds).astype(BF16)
            if not fox:
                dym_ref[:, 512 * k:512 * (k + 1)] = dy
                continue
            for q in range(512 // LANES):
                cols = slice(LANES * q, LANES * (q + 1))
                dy2 = dy[:, cols]
                prod = dy2 * ymain[:, 512 * k + LANES * q:512 * k + LANES * (q + 1)]
                for hh in range(2):
                    delta = jnp.sum(jnp.where(first == (hh == 0), prod, 0.0), axis=1, keepdims=True)
                    dyh = dy2 if hh == 0 else _swap_halves(dy2)
                    c0 = LANES * (2 * (4 * k + q) + hh)
                    dym_ref[:, c0:c0 + LANES] = jnp.where(
                        first, dyh, _lanes3(lane, AUX, _split3(-delta), 0.0)).astype(BF16)
        gv, d = gc_ref[...].astype(F32), dc_ref[...].astype(F32)
        sg, ds = dsilu(gv)
        dymem_ref[...] = d * gv * sg
        dh_ref[:, 1024:1536] = (d * ym_ref[...] * ds).astype(BF16)

    slab = lambda c: pl.BlockSpec((tb, 512), lambda i, c=c: (i, c))
    return pl.pallas_call(
        body, name=name, grid=(S // tb,),
        in_specs=[slab(0), slab(1), slab(2),
                  pl.BlockSpec((tb, D_MAIN), lambda i: (i, 0)), pl.BlockSpec((1, D_MAIN), lambda i: (0, 0)),
                  slab(3), slab(4), slab(5), pl.BlockSpec((tb, D_MEM), lambda i: (i, 0))],
        out_specs=[pl.BlockSpec((tb, 2 * D_MAIN if fox else D_MAIN), lambda i: (i, 0)),
                   pl.BlockSpec((tb, D_MEM), lambda i: (i, 0)), pl.BlockSpec((tb, D_MIX), lambda i: (i, 1))],
        out_shape=[jax.ShapeDtypeStruct((S, 2 * D_MAIN), BF16) if fox else jax.ShapeDtypeStruct((S, D_MAIN), F32),
                   jax.ShapeDtypeStruct((S, D_MEM), F32), jax.ShapeDtypeStruct((S, D_IN), BF16)],
        compiler_params=_cparams("parallel"),
    )(dyg, dyg, dyg, ysrc, scale, h, h, h, ymem)


def _window_count(t0, rows, w):
    t = t0 + lax.broadcasted_iota(jnp.int32, (rows, POOL_GROUP), 0)
    return jnp.minimum(t + 1, w).astype(F32)


def _pool_fwd(h, pw, *, tb, name):
    S = h.shape[0]
    tb = min(tb, S)

    def body(u_ref, pw_ref, pm_ref, mixed_ref, tail_ref):
        i = pl.program_id(0)

        @pl.when(i == 0)
        def _():
            tail_ref[...] = jnp.zeros_like(tail_ref)

        u = u_ref[...].astype(F32)
        xfull = jnp.concatenate([tail_ref[...], u], axis=0)
        for gi, w in enumerate(POOL_WINDOWS):
            cols = slice(POOL_GROUP * gi, POOL_GROUP * (gi + 1))
            s = xfull[:, cols]
            sh = 1
            while sh < w:
                s = s + pltpu.roll(s, sh, 0)
                sh *= 2
            pm = s[POOL_HALO:, :] / _window_count(i * tb, tb, w) - u[:, cols]
            pmb = pm.astype(BF16)
            pm_ref[:, cols] = pmb
            mixed_ref[:, cols] = jnp.dot(pmb, pw_ref[gi], preferred_element_type=F32)
        tail_ref[...] = u[tb - POOL_HALO:, :]

    return pl.pallas_call(
        body, name=name, grid=(S // tb,),
        in_specs=[pl.BlockSpec((tb, D_MAIN), lambda i: (i, 0)),
                  pl.BlockSpec((4, POOL_GROUP, POOL_GROUP), lambda i: (0, 0, 0))],
        out_specs=[pl.BlockSpec((tb, D_MAIN), lambda i: (i, 0)), pl.BlockSpec((tb, D_MAIN), lambda i: (i, 0))],
        out_shape=[jax.ShapeDtypeStruct((S, D_MAIN), BF16), jax.ShapeDtypeStruct((S, D_MAIN), F32)],
        scratch_shapes=[pltpu.VMEM((POOL_HALO, D_MAIN), F32)],
        compiler_params=_cparams("arbitrary"),
    )(h, pw)


def _pool_bwd(dymain, pm, mixed, pw, scale, dh, *, tb, name):
    S = dymain.shape[0]
    tb = min(tb, S)
    nb = S // tb
    n = tb + POOL_HALO

    def body(dy_ref, pm_ref, mixed_ref, pw_ref, sc_ref, dh_in, dh_ref, dpw_ref, dsc_ref, head_ref, dpw_acc):
        del dh_in
        i = pl.program_id(0)

        @pl.when(i == 0)
        def _():
            head_ref[...] = jnp.zeros_like(head_ref)
            dpw_acc[...] = jnp.zeros_like(dpw_acc)
            dsc_ref[...] = jnp.zeros_like(dsc_ref)

        dy = dy_ref[...]
        dsc_ref[...] += jnp.sum(dy * mixed_ref[...], axis=0, keepdims=True)
        dmixed = dy * sc_ref[...]
        t0 = (nb - 1 - i) * tb
        for gi, w in enumerate(POOL_WINDOWS):
            cols = slice(POOL_GROUP * gi, POOL_GROUP * (gi + 1))
            dm = dmixed[:, cols].astype(BF16)
            dpw_acc[gi] += lax.dot_general(pm_ref[:, cols], dm, TN, preferred_element_type=F32)
            dpm = lax.dot_general(dm, pw_ref[gi], NT, preferred_element_type=F32)
            e = dpm / _window_count(t0, tb, w)
            s = jnp.concatenate([e, head_ref[:, cols]], axis=0)
            sh = 1
            while sh < w:
                s = s + pltpu.roll(s, n - sh, 0)
                sh *= 2
            dh_ref[:, cols] = (s[:tb, :] - dpm).astype(BF16)
            head_ref[:, cols] = e[:POOL_HALO, :]

        @pl.when(i == nb - 1)
        def _():
            dpw_ref[...] = dpw_acc[...].astype(BF16)

    rev = lambda i: (nb - 1 - i, 0)
    return pl.pallas_call(
        body, name=name, grid=(nb,),
        in_specs=[pl.BlockSpec((tb, D_MAIN), rev), pl.BlockSpec((tb, D_MAIN), rev), pl.BlockSpec((tb, D_MAIN), rev),
                  pl.BlockSpec((4, POOL_GROUP, POOL_GROUP), lambda i: (0, 0, 0)),
                  pl.BlockSpec((1, D_MAIN), lambda i: (0, 0)), pl.BlockSpec(memory_space=pl.ANY)],
        out_specs=[pl.BlockSpec((tb, D_MAIN), rev),
                   pl.BlockSpec((4, POOL_GROUP, POOL_GROUP), lambda i: (0, 0, 0)),
                   pl.BlockSpec((1, D_MAIN), lambda i: (0, 0))],
        out_shape=[jax.ShapeDtypeStruct(dh.shape, dh.dtype),
                   jax.ShapeDtypeStruct((4, POOL_GROUP, POOL_GROUP), BF16), jax.ShapeDtypeStruct((1, D_MAIN), F32)],
        scratch_shapes=[pltpu.VMEM((POOL_HALO, D_MAIN), F32), pltpu.VMEM((4, POOL_GROUP, POOL_GROUP), F32)],
        input_output_aliases={5: 0},
        compiler_params=_cparams("arbitrary"),
    )(dymain, pm, mixed, pw, scale, dh)


MEM_SCALE = MEM_HEAD_DIM ** -0.5


def _mem_probs(q_ref, mkv_ref, hd):
    cols = slice(MEM_HEAD_DIM * hd, MEM_HEAD_DIM * (hd + 1))
    q = (q_ref[:, cols].astype(F32) * MEM_SCALE).astype(BF16)
    mk = mkv_ref[:, cols].astype(BF16)
    mv = mkv_ref[:, D_MEM + MEM_HEAD_DIM * hd:D_MEM + MEM_HEAD_DIM * (hd + 1)].astype(BF16)
    s = lax.dot_general(q, mk, NT, preferred_element_type=F32)
    e = jnp.exp(s - jnp.max(s, axis=1, keepdims=True))
    return cols, q, mk, mv, e, jnp.sum(e, axis=1, keepdims=True)


def _memattn_fwd(h, mkv, *, tb, name):
    S = h.shape[0]
    tb = min(tb, S)

    def body(q_ref, mkv_ref, y_ref):
        for hd in range(MEM_HEADS):
            cols, _, _, mv, e, l = _mem_probs(q_ref, mkv_ref, hd)
            y_ref[:, cols] = jnp.dot(e.astype(BF16), mv, preferred_element_type=F32) / l

    return pl.pallas_call(
        body, name=name, grid=(S // tb,),
        in_specs=[pl.BlockSpec((tb, D_MEM), lambda i: (i, 2)), pl.BlockSpec((N_MEM, 2 * D_MEM), lambda i: (0, 0))],
        out_specs=pl.BlockSpec((tb, D_MEM), lambda i: (i, 0)),
        out_shape=jax.ShapeDtypeStruct((S, D_MEM), F32),
        compiler_params=_cparams("parallel"),
    )(h, mkv)


def _memattn_bwd(h, mkv, dy, dh, *, tb, name):
    S = h.shape[0]
    tb = min(tb, S)

    def body(q_ref, mkv_ref, dy_ref, dh_in, dh_ref, dmkv_ref):
        del dh_in

        @pl.when(pl.program_id(0) == 0)
        def _():
            dmkv_ref[...] = jnp.zeros_like(dmkv_ref)

        for hd in range(MEM_HEADS):
            cols, q, mk, mv, e, l = _mem_probs(q_ref, mkv_ref, hd)
            p = e / l
            dyh = dy_ref[:, cols].astype(BF16)
            dp = lax.dot_general(dyh, mv, NT, preferred_element_type=F32)
            ds = p * (dp - jnp.sum(dp * p, axis=1, keepdims=True))
            dsb = ds.astype(BF16)
            dh_ref[:, cols] = (jnp.dot(dsb, mk, preferred_element_type=F32) * MEM_SCALE).astype(BF16)
            dmkv_ref[:, cols] += lax.dot_general(dsb, q, TN, preferred_element_type=F32)
            vcols = slice(D_MEM + MEM_HEAD_DIM * hd, D_MEM + MEM_HEAD_DIM * (hd + 1))
            dmkv_ref[:, vcols] += lax.dot_general(p.astype(BF16), dyh, TN, preferred_element_type=F32)

    return pl.pallas_call(
        body, name=name, grid=(S // tb,),
        in_specs=[pl.BlockSpec((tb, D_MEM), lambda i: (i, 2)), pl.BlockSpec((N_MEM, 2 * D_MEM), lambda i: (0, 0)),
                  pl.BlockSpec((tb, D_MEM), lambda i: (i, 0)), pl.BlockSpec(memory_space=pl.ANY)],
        out_specs=[pl.BlockSpec((tb, D_MEM), lambda i: (i, 2)), pl.BlockSpec((N_MEM, 2 * D_MEM), lambda i: (0, 0))],
        out_shape=[jax.ShapeDtypeStruct(dh.shape, dh.dtype), jax.ShapeDtypeStruct((N_MEM, 2 * D_MEM), F32)],
        input_output_aliases={3: 0},
        compiler_params=_cparams("arbitrary"),
    )(h, mkv, dy, dh)


def _forget_fwd(fl, bias, *, tb, name):
    S = fl.shape[0]
    tb = min(tb, S)

    def body(fl_ref, b_ref, o_ref, carry_ref):
        @pl.when(pl.program_id(0) == 0)
        def _():
            carry_ref[...] = jnp.zeros_like(carry_ref)

        z = fl_ref[...] + b_ref[...]
        lf = jnp.minimum(z, 0.0) - jnp.log(1.0 + jnp.exp(-jnp.abs(z)))
        row = lax.broadcasted_iota(jnp.int32, (tb, LANES), 0)
        c = lf
        sh = 1
        while sh < tb:
            c = c + jnp.where(row >= sh, pltpu.roll(c, sh, 0), 0.0)
            sh *= 2
        o_ref[...] = -(carry_ref[...] + c)
        carry_ref[...] += jnp.sum(lf, axis=0, keepdims=True)

    return pl.pallas_call(
        body, name=name, grid=(S // tb,),
        in_specs=[pl.BlockSpec((tb, LANES), lambda i: (i, 0)), pl.BlockSpec((1, LANES), lambda i: (0, 0))],
        out_specs=pl.BlockSpec((tb, LANES), lambda i: (i, 0)),
        out_shape=jax.ShapeDtypeStruct((S, LANES), F32),
        scratch_shapes=[pltpu.VMEM((1, LANES), F32)],
        compiler_params=_cparams("arbitrary"),
    )(fl, bias)


def _forget_bwd(dn, drow, fl, bias, *, tb, name):
    S = fl.shape[0]
    tb = min(tb, S)
    nb = S // tb

    def body(dn_ref, dr_ref, fl_ref, b_ref, dh_ref, db_ref, carry_ref):
        @pl.when(pl.program_id(0) == 0)
        def _():
            carry_ref[...] = jnp.zeros_like(carry_ref)
            db_ref[...] = jnp.zeros_like(db_ref)

        src = lax.broadcasted_iota(jnp.int32, (D_MAIN, LANES), 0)
        head = lax.broadcasted_iota(jnp.int32, (D_MAIN, LANES), 1)
        pick = lambda off: jnp.where((src == FOX_HEAD_DIM * head + off) & (head < FOX_HEADS), 1.0, 0.0).astype(BF16)
        hdot = lambda a, sel: sum(jnp.dot(part.astype(BF16), sel, preferred_element_type=F32) for part in _split3(a))
        dcum = hdot(dr_ref[...], pick(3)) - hdot(dn_ref[...], pick(0))
        row = lax.broadcasted_iota(jnp.int32, (tb, LANES), 0)
        c = dcum
        sh = 1
        while sh < tb:
            c = c + jnp.where(row < tb - sh, pltpu.roll(c, tb - sh, 0), 0.0)
            sh *= 2
        dlf = carry_ref[...] + c
        carry_ref[...] += jnp.sum(dcum, axis=0, keepdims=True)
        z = fl_ref[...] + b_ref[...]
        lane = lax.broadcasted_iota(jnp.int32, (tb, LANES), 1)
        dfl = jnp.where(lane < FOX_HEADS, dlf / (1.0 + jnp.exp(z)), 0.0)
        db_ref[...] += jnp.sum(dfl, axis=0, keepdims=True)
        dh_ref[...] = dfl.astype(BF16)

    rev = lambda i: (nb - 1 - i, 0)
    return pl.pallas_call(
        body, name=name, grid=(nb,),
        in_specs=[pl.BlockSpec((tb, D_MAIN), rev), pl.BlockSpec((tb, D_MAIN), rev), pl.BlockSpec((tb, LANES), rev),
                  pl.BlockSpec((1, LANES), lambda i: (0, 0))],
        out_specs=[pl.BlockSpec((tb, LANES), rev), pl.BlockSpec((1, LANES), lambda i: (0, 0))],
        out_shape=[jax.ShapeDtypeStruct((S, LANES), BF16), jax.ShapeDtypeStruct((1, LANES), F32)],
        scratch_shapes=[pltpu.VMEM((1, LANES), F32)],
        compiler_params=_cparams("arbitrary"),
    )(dn, drow, fl, bias)


FOX_SCALE = FOX_HEAD_DIM ** -0.5
LOG2E = 1.4426950408889634
LN2 = 0.6931471805599453
AUX = FOX_HEAD_DIM


def _split3(x):
    hi = x.astype(BF16).astype(F32)
    r = x - hi
    mid = r.astype(BF16).astype(F32)
    return hi, mid, (r - mid).astype(BF16).astype(F32)


def _lanes3(lane, base, parts, rest):
    return jnp.where(lane == base, parts[0], jnp.where(lane == base + 1, parts[1],
                                                       jnp.where(lane == base + 2, parts[2], rest)))


def _swap_halves(x):
    return pltpu.roll(x, FOX_HEAD_DIM, 1)


def _causal_steps(nq, keys_outer):
    if keys_outer:
        pairs = [(i, j) for j in range(nq) for i in range(j, nq)]
    else:
        pairs = [(i, j) for i in range(nq) for j in range(i + 1)]
    it, jt = zip(*pairs)
    return jnp.asarray(np.array(it, np.int32)), jnp.asarray(np.array(jt, np.int32))


def _in_proj_fox(x, w, *, tm, name):
    S = x.shape[0]
    tm = min(tm, S)

    def body(x_ref, w_ref, h_ref, qa_ref):
        acc = jnp.dot(x_ref[...].astype(BF16), w_ref[...], preferred_element_type=F32)
        h_ref[...] = acc.astype(BF16)
        lane = lax.broadcasted_iota(jnp.int32, (tm, LANES), 1)
        first = lane < FOX_HEAD_DIM
        ones_q = jnp.where((lane >= AUX) & (lane < AUX + 3), 1.0, 0.0)
        for g in range(FOX_PAIRS):
            q = acc[:, LANES * g:LANES * (g + 1)] * (FOX_SCALE * LOG2E)
            qa_ref[:, 2 * LANES * g:2 * LANES * g + LANES] = jnp.where(first, q, ones_q).astype(BF16)
            qa_ref[:, 2 * LANES * g + LANES:2 * LANES * (g + 1)] = jnp.where(first, _swap_halves(q), ones_q).astype(BF16)

    return pl.pallas_call(
        body, name=name, grid=(S // tm,),
        in_specs=[pl.BlockSpec((tm, D_MODEL), lambda i: (i, 0)), pl.BlockSpec((D_MODEL, D_IN), lambda i: (0, 0))],
        out_specs=[pl.BlockSpec((tm, D_IN), lambda i: (i, 0)), pl.BlockSpec((tm, 2 * D_MAIN), lambda i: (i, 0))],
        out_shape=[jax.ShapeDtypeStruct((S, D_IN), BF16), jax.ShapeDtypeStruct((S, 2 * D_MAIN), BF16)],
        compiler_params=_cparams("parallel"),
    )(x, w)


def _kv_proj_fox(x, wkv, negcum, *, tm, name):
    S = x.shape[0]
    tm = min(tm, S)

    def body(x_ref, w_ref, nc_ref, ka_ref, va_ref):
        acc = jnp.dot(x_ref[...].astype(BF16), w_ref[...], preferred_element_type=F32)
        nc = nc_ref[...]
        lane = lax.broadcasted_iota(jnp.int32, (tm, LANES), 1)
        first = lane < FOX_HEAD_DIM
        ones_k = jnp.where((lane >= AUX + 3) & (lane < AUX + 6), 1.0, 0.0)
        for g in range(FOX_PAIRS):
            k = acc[:, LANES * g:LANES * (g + 1)]
            v = acc[:, D_MAIN + LANES * g:D_MAIN + LANES * (g + 1)]
            for hh in range(2):
                sl = slice(LANES * (2 * g + hh), LANES * (2 * g + hh + 1))
                kh, vh = (k, v) if hh == 0 else (_swap_halves(k), _swap_halves(v))
                ncol = jnp.sum(jnp.where(lane == 2 * g + hh, nc, 0.0), axis=1, keepdims=True) * LOG2E
                ka_ref[:, sl] = jnp.where(first, kh, _lanes3(lane, AUX, _split3(ncol), ones_k)).astype(BF16)
                va_ref[:, sl] = jnp.where(first, vh, 1.0).astype(BF16)

    out = pl.BlockSpec((tm, 2 * D_MAIN), lambda i: (i, 0))
    shp = jax.ShapeDtypeStruct((S, 2 * D_MAIN), BF16)
    return pl.pallas_call(
        body, name=name, grid=(S // tm,),
        in_specs=[pl.BlockSpec((tm, D_MODEL), lambda i: (i, 0)), pl.BlockSpec((D_MODEL, 2 * D_MAIN), lambda i: (0, 0)),
                  pl.BlockSpec((tm, LANES), lambda i: (i, 0))],
        out_specs=[out, out], out_shape=[shp, shp],
        compiler_params=_cparams("parallel"),
    )(x, wkv, negcum)


def _dwkv(x, dk, dv, dfl, *, tk, name):
    S = x.shape[0]
    tk = min(tk, S)

    def body(x_ref, dk_ref, dv_ref, df_ref, o_ref):
        @pl.when(pl.program_id(0) == 0)
        def _():
            o_ref[...] = jnp.zeros_like(o_ref)

        a = x_ref[...].astype(BF16)
        for b_ref, c0 in ((dk_ref, 0), (dv_ref, D_MAIN), (df_ref, 2 * D_MAIN)):
            o_ref[:, c0:c0 + b_ref.shape[1]] += lax.dot_general(a, b_ref[...], TN, preferred_element_type=F32)

    row = lambda n: pl.BlockSpec((tk, n), lambda k: (k, 0))
    return pl.pallas_call(
        body, name=name, grid=(S // tk,), in_specs=[row(D_MODEL), row(D_MAIN), row(D_MAIN), row(LANES)],
        out_specs=pl.BlockSpec((D_MODEL, 2 * D_MAIN + LANES), lambda k: (0, 0)),
        out_shape=jax.ShapeDtypeStruct((D_MODEL, 2 * D_MAIN + LANES), F32),
        compiler_params=_cparams("arbitrary"),
    )(x, dk, dv, dfl)


def _dx1(dh, win, dk, dv, dfl, wkv, dz, *, tm, name):
    S = dh.shape[0]
    tm = min(tm, S)

    def body(dh_ref, win_ref, dk_ref, dv_ref, df_ref, wk_ref, wv_ref, wf_ref, dz_ref, o_ref):
        acc = ALPHA * dz_ref[...]
        for a_ref, b_ref in ((dh_ref, win_ref), (dk_ref, wk_ref), (dv_ref, wv_ref), (df_ref, wf_ref)):
            acc = acc + lax.dot_general(a_ref[...], b_ref[...], NT, preferred_element_type=F32)
        o_ref[...] = acc

    row = lambda n: pl.BlockSpec((tm, n), lambda i: (i, 0))
    wcols = lambda n, c: pl.BlockSpec((D_MODEL, n), lambda i, c=c: (0, c))
    return pl.pallas_call(
        body, name=name, grid=(S // tm,),
        in_specs=[row(D_IN), wcols(D_IN, 0), row(D_MAIN), row(D_MAIN), row(LANES),
                  wcols(D_MAIN, 0), wcols(D_MAIN, 1), wcols(LANES, 2 * D_MAIN // LANES), row(D_MODEL)],
        out_specs=row(D_MODEL), out_shape=jax.ShapeDtypeStruct((S, D_MODEL), F32),
        compiler_params=_cparams("parallel"),
    )(dh, win, dk, dv, dfl, wkv, wkv, wkv, dz)


def _fox_fwd(qa, ka, va, *, tq, name):
    S = qa.shape[0]
    tq = min(tq, S)
    nq = S // tq
    half = tq // 2
    it, jt = _causal_steps(nq, keys_outer=False)

    def body(it_ref, jt_ref, qa_ref, ka_ref, va_ref, y_ref, qb_ref, m_ref, acc_ref):
        n = pl.program_id(1)
        i, j = it_ref[n], jt_ref[n]
        first = lax.broadcasted_iota(jnp.int32, (tq, LANES), 1) < FOX_HEAD_DIM

        @pl.when(j == 0)
        def _():
            m_ref[...] = jnp.full_like(m_ref, NEG)
            acc_ref[...] = jnp.zeros_like(acc_ref)

        def update(hh, rows, nk, masked):
            sl = slice(LANES * hh, LANES * (hh + 1))
            s = lax.dot_general(qa_ref[rows, sl], ka_ref[0:nk, sl], NT, preferred_element_type=F32)
            if masked:
                r = lax.broadcasted_iota(jnp.int32, s.shape, 0) + rows.start
                c = lax.broadcasted_iota(jnp.int32, s.shape, 1)
                s = jnp.where(c <= r, s, NEG)
            m_prev = m_ref[hh, rows]
            m_new = jnp.maximum(m_prev, jnp.max(s, axis=1, keepdims=True))
            p = jnp.exp2(s - jnp.tile(m_new, (1, nk // LANES))).astype(BF16)
            acc_ref[hh, rows] = jnp.exp2(m_prev - m_new) * acc_ref[hh, rows] + jnp.dot(
                p, va_ref[0:nk, sl], preferred_element_type=F32)
            m_ref[hh, rows] = m_new

        @pl.when(j < i)
        def _():
            for hh in range(2):
                update(hh, slice(0, tq), tq, False)

        @pl.when(j == i)
        def _():
            for hh in range(2):
                for r0 in range(0, tq, half):
                    update(hh, slice(r0, r0 + half), r0 + half, True)
            lane = lax.broadcasted_iota(jnp.int32, (tq, LANES), 1)
            ys = []
            for hh in range(2):
                sl = slice(LANES * hh, LANES * (hh + 1))
                a = acc_ref[hh]
                denom = _swap_halves(a)
                ys.append(a / denom)
                lse2 = m_ref[hh] + jnp.log(jnp.where(first, denom, a)) * LOG2E
                qb_ref[:, sl] = _lanes3(lane, AUX + 3, _split3(-lse2), qa_ref[:, sl].astype(F32)).astype(BF16)
            y_ref[...] = jnp.where(first, ys[0], _swap_halves(ys[1]))

    qblock = pl.BlockSpec((tq, 2 * LANES), lambda g, n, it, jt: (it[n], g))
    kblock = pl.BlockSpec((tq, 2 * LANES), lambda g, n, it, jt: (jt[n], g))
    return pl.pallas_call(
        body, name=name,
        grid_spec=pltpu.PrefetchScalarGridSpec(
            num_scalar_prefetch=2, grid=(FOX_PAIRS, it.shape[0]),
            in_specs=[qblock, kblock, kblock],
            out_specs=[pl.BlockSpec((tq, LANES), lambda g, n, it, jt: (it[n], g)), qblock],
            scratch_shapes=[pltpu.VMEM((2, tq, LANES), F32), pltpu.VMEM((2, tq, LANES), F32)]),
        out_shape=[jax.ShapeDtypeStruct((S, D_MAIN), F32), jax.ShapeDtypeStruct((S, 2 * D_MAIN), BF16)],
        compiler_params=_cparams("parallel", "arbitrary"),
    )(it, jt, qa, ka, va)


def _fox_bwd(qb, ka, va, dya, dh, *, tq, name):
    S = qb.shape[0]
    tq = min(tq, S)
    nq = S // tq
    half = tq // 2
    it, jt = _causal_steps(nq, keys_outer=True)
    nsteps = it.shape[0]

    def body(it_ref, jt_ref, qb_ref, ka_ref, va_ref, dya_ref, dh_in, dq_ref, dk_ref, dv_ref, dn_ref, drow_ref,
             dq_acc, dk_acc, dv_acc):
        del dh_in
        n = pl.program_id(1)
        i, j = it_ref[n], jt_ref[n]
        first = lax.broadcasted_iota(jnp.int32, (tq, LANES), 1) < FOX_HEAD_DIM

        @pl.when(n == 0)
        def _():
            dq_acc[...] = jnp.zeros_like(dq_acc)

        @pl.when(i == j)
        def _():
            dk_acc[...] = jnp.zeros_like(dk_acc)
            dv_acc[...] = jnp.zeros_like(dv_acc)

        def update(hh, keys, q0, masked):
            sl = slice(LANES * hh, LANES * (hh + 1))
            qbh, kah, dyah = qb_ref[q0:tq, sl], ka_ref[keys, sl], dya_ref[q0:tq, sl]
            eT = lax.dot_general(kah, qbh, NT, preferred_element_type=F32)
            if masked:
                r = lax.broadcasted_iota(jnp.int32, eT.shape, 0) + keys.start
                c = lax.broadcasted_iota(jnp.int32, eT.shape, 1) + q0
                eT = jnp.where(r <= c, eT, NEG)
            pT = jnp.exp2(eT)
            dsT = pT * lax.dot_general(va_ref[keys, sl], dyah, NT, preferred_element_type=F32)
            dsb = dsT.astype(BF16)
            dv_acc[hh, keys] += jnp.dot(pT.astype(BF16), dyah, preferred_element_type=F32)
            dk_acc[hh, keys] += jnp.dot(dsb, qbh, preferred_element_type=F32)
            rows = pl.ds(pl.multiple_of(i * tq + q0, half), tq - q0)
            dq_acc[hh, rows, :] += lax.dot_general(dsb, kah, TN, preferred_element_type=F32)

        @pl.when(i > j)
        def _():
            for hh in range(2):
                update(hh, slice(0, tq), 0, False)

        @pl.when(i == j)
        def _():
            for hh in range(2):
                for k0 in range(0, tq, half):
                    update(hh, slice(k0, k0 + half), k0, True)

        @pl.when(i == nq - 1)
        def _():
            dk_ref[...] = (jnp.where(first, dk_acc[0], _swap_halves(dk_acc[1])) * LN2).astype(BF16)
            dv_ref[...] = jnp.where(first, dv_acc[0], _swap_halves(dv_acc[1])).astype(BF16)
            dn_ref[...] = jnp.where(first, _swap_halves(dk_acc[0]), dk_acc[1])

        @pl.when(n == nsteps - 1)
        def _():
            first_s = lax.broadcasted_iota(jnp.int32, (S, LANES), 1) < FOX_HEAD_DIM
            dq_ref[...] = (jnp.where(first_s, dq_acc[0], _swap_halves(dq_acc[1])) * FOX_SCALE).astype(BF16)
            drow_ref[...] = jnp.where(first_s, _swap_halves(dq_acc[0]), dq_acc[1])

    qblock = pl.BlockSpec((tq, 2 * LANES), lambda g, n, it, jt: (it[n], g))
    kblock = pl.BlockSpec((tq, 2 * LANES), lambda g, n, it, jt: (jt[n], g))
    whole = pl.BlockSpec((S, LANES), lambda g, n, it, jt: (0, g))
    kout = pl.BlockSpec((tq, LANES), lambda g, n, it, jt: (jt[n], g))
    return pl.pallas_call(
        body, name=name,
        grid_spec=pltpu.PrefetchScalarGridSpec(
            num_scalar_prefetch=2, grid=(FOX_PAIRS, nsteps),
            in_specs=[qblock, kblock, kblock, qblock, pl.BlockSpec(memory_space=pl.ANY)],
            out_specs=[whole, kout, kout, kout, whole],
            scratch_shapes=[pltpu.VMEM((2, S, LANES), F32), pltpu.VMEM((2, tq, LANES), F32),
                            pltpu.VMEM((2, tq, LANES), F32)]),
        out_shape=[jax.ShapeDtypeStruct(dh.shape, dh.dtype), jax.ShapeDtypeStruct((S, D_MAIN), BF16),
                   jax.ShapeDtypeStruct((S, D_MAIN), BF16),
                   jax.ShapeDtypeStruct((S, D_MAIN), F32), jax.ShapeDtypeStruct((S, D_MAIN), F32)],
        input_output_aliases={6: 0},
        compiler_params=_cparams("parallel", "arbitrary"),
    )(it, jt, qb, ka, va, dya, dh)


TB_ROWS = 256
TB_SEQ = 512
TQ_FOX_FWD = 1024
TQ_FOX_BWD = 1024


def _local_step(x, mem, target, win0, pscale, next_weights, late_weights, ln_g, ln_b, bias, send_early=None,
                send_late=None):
    ones = jnp.ones((1, D_MAIN), F32)
    g0, b0, g1, b1 = ln_g[0:1], ln_b[0:1], ln_g[1:2], ln_b[1:2]
    mm = lambda a, b, mode, dt, tm, tn, tk, name, **kw: _mm(a, b, mode=mode, out_dtype=dt, tm=tm, tn=tn, tk=tk,
                                                            name=name, **kw)

    h0 = mm(x, win0, "nn", BF16, 256, D_IN, D_MODEL, "l0_in")
    wmkv, pw, wout0 = next_weights(h0)
    pm, mixed = _pool_fwd(h0, pw, tb=TB_SEQ, name="l0_pool_fwd")
    mkv0 = mm(mem, wmkv[0], "nn", F32, 256, 1024, 1024, "l0_mkv")
    ymem0 = _memattn_fwd(h0, mkv0, tb=TB_SEQ, name="l0_mem_fwd")
    yg0 = _gate_fwd(mixed, pscale, h0, ymem0, tb=TB_ROWS, name="l0_gate_fwd")
    x1, xhat0, rstd0 = _out_ln(yg0, wout0, x, g0, b0, tb=TB_SEQ, name="l0_out_ln")
    win1, wout1, wkv, wf = late_weights(x1)

    fl = mm(x1, wf, "nn", F32, 512, LANES, D_MODEL, "f_proj")
    negcum = _forget_fwd(fl, bias, tb=TB_SEQ, name="forget_fwd")
    ka, va = _kv_proj_fox(x1, wkv, negcum, tm=512, name="kv_proj")

    h1, qa = _in_proj_fox(x1, win1, tm=256, name="l1_in")
    y1, qb = _fox_fwd(qa, ka, va, tq=TQ_FOX_FWD, name="fox_fwd")
    mkv1 = mm(mem, wmkv[1], "nn", F32, 256, 1024, 1024, "l1_mkv")
    ymem1 = _memattn_fwd(h1, mkv1, tb=TB_SEQ, name="l1_mem_fwd")
    yg1 = _gate_fwd(y1, ones, h1, ymem1, tb=TB_ROWS, name="l1_gate_fwd")
    dz1, dg1, db1, sq = _out_ln_loss(yg1, wout1, x1, g1, b1, target, tb=TB_SEQ, name="l1_out_ln_loss")

    dwout1 = mm(yg1, dz1, "tn", BF16, D_MIX, D_MODEL, 512, "l1_dwout")
    dyg1 = mm(dz1, wout1, "nt", BF16, 512, D_MIX, D_MODEL, "l1_dyg")
    dya, dymem1, dh1 = _gate_bwd(dyg1, y1, ones, h1, ymem1, tb=TB_ROWS, name="l1_gate_bwd", fox=True)
    dh1, dk, dv, dnp, drowp = _fox_bwd(qb, ka, va, dya, dh1, tq=TQ_FOX_BWD, name="fox_bwd")
    dfl, dbias = _forget_bwd(dnp, drowp, fl, bias, tb=TB_SEQ, name="forget_bwd")
    dh1, dmkv1 = _memattn_bwd(h1, mkv1, dymem1, dh1, tb=TB_SEQ, name="l1_mem_bwd")
    dwmkv1 = mm(mem, dmkv1, "tn", BF16, D_MODEL, 1024, N_MEM, "l1_dwmkv")
    dwin1 = mm(x1, dh1, "tn", BF16, D_MODEL, D_IN // 2, 512, "l1_dwin")
    dwkv = _dwkv(x1, dk, dv, dfl, tk=512, name="dwkv")
    dwkv = dwkv[:, :2 * D_MAIN + FOX_HEADS].reshape(D_MODEL, N_DEV, -1).transpose(1, 0, 2).astype(BF16)
    anchor = send_early(dict(w_out=dwout1, w_mem_kv=dwmkv1, w_in=dwin1, w_kv_shared=dwkv)) if send_early else 0.0
    dx1 = _dx1(dh1, win1, dk, dv, dfl, wkv, dz1, tm=256, name="l1_dx")

    dz0, dg0, db0 = _ln_bwd(dx1, xhat0, rstd0, g0 + anchor, tb=TB_ROWS, name="l0_ln_bwd")
    dwout0 = mm(yg0, dz0, "tn", BF16, D_MIX, D_MODEL, 512, "l0_dwout")
    dyg0 = mm(dz0, wout0, "nt", BF16, 512, D_MIX, D_MODEL, "l0_dyg")
    dy0, dymem0, dh0 = _gate_bwd(dyg0, mixed, pscale, h0, ymem0, tb=TB_ROWS, name="l0_gate_bwd")
    dh0, dpw, dpscale = _pool_bwd(dy0, pm, mixed, pw, pscale, dh0, tb=TB_SEQ, name="l0_pool_bwd")
    dh0, dmkv0 = _memattn_bwd(h0, mkv0, dymem0, dh0, tb=TB_SEQ, name="l0_mem_bwd")
    dwmkv0 = mm(mem, dmkv0, "tn", BF16, D_MODEL, 1024, N_MEM, "l0_dwmkv")
    dwin0 = mm(x, dh0, "tn", BF16, D_MODEL, D_IN // 2, 512, "l0_dwin")
    grads = dict(w_in=(dwin0, dwin1), w_mem_kv=(dwmkv0, dwmkv1), w_out=(dwout0, dwout1), pool_w=(dpw,),
                 w_kv_shared=dwkv, pool_scale=dpscale, ln_g=jnp.concatenate([dg0, dg1]),
                 ln_b=jnp.concatenate([db0, db1]), b_forget=dbias[0, :FOX_HEADS])
    if send_late:
        dz0 = send_late(grads, dz0)
    gx = mm(dh0, win0, "nt", F32, 256, D_MODEL, D_IN, "l0_dx", add=dz0, add_scale=ALPHA)
    return sq, gx, grads


MESH_ID = pl.DeviceIdType.MESH
HBM = pl.BlockSpec(memory_space=pl.ANY)
SLICED = {"w_in": (2, D_IN // N_DEV), "w_mem_kv": (1, D_MODEL // N_DEV), "w_out": (1, D_MIX // N_DEV),
          "pool_w": (1, POOL_GROUP // N_DEV)}


def _place():
    return lax.axis_index("x"), lax.axis_index("y"), lax.axis_index("c")


def _slot(p):
    return 4 * p[0] + 2 * p[1] + p[2]


def _cut(ref, axis, width, s):
    idx = [slice(None)] * len(ref.shape)
    idx[axis] = pl.ds(s * width, width)
    return ref.at[tuple(idx)]


def _all_gather(shards, cuts, *, name):
    nt = len(shards)

    def full_shape(a, cut):
        if cut is None:
            return (N_DEV,) + a.shape
        return a.shape[:cut[0]] + (a.shape[cut[0]] * N_DEV,) + a.shape[cut[0] + 1:]

    def body(*refs):
        ins, outs = refs[:nt], refs[nt:2 * nt]
        send_sems, recv_sems, local_sems = refs[2 * nt:]
        x, y, c = _place()
        me, sibling = (x, y, c), (x, y, 1 - c)
        chips = [(1 - x, y), (x, 1 - y), (1 - x, 1 - y)]

        def place(t, s):
            return outs[t].at[s] if cuts[t] is None else _cut(outs[t], cuts[t][0], cuts[t][1], s)

        def copies(k, block, to, from_input=False):
            s = _slot(block)
            return [pltpu.make_async_remote_copy(
                src_ref=ins[t] if from_input else place(t, s), dst_ref=place(t, s),
                send_sem=send_sems.at[nt * k + t], recv_sem=recv_sems.at[nt * k + t],
                device_id=to, device_id_type=MESH_ID) for t in range(nt)]

        mine = [pltpu.make_async_copy(ins[t], place(t, _slot(me)), local_sems.at[t]) for t in range(nt)]
        for cp in mine:
            cp.start()
        first = [copies(0, me, sibling, True)] + [copies(1 + j, me, (*chip, c), True) for j, chip in enumerate(chips)]
        for group in first:
            for cp in group:
                cp.start()
        passed = [copies(4 + j, (*chip, c), sibling) for j, chip in enumerate(chips)]
        for j, chip in enumerate(chips):
            for cp in copies(1 + j, (*chip, c), me):
                cp.wait_recv()
            for cp in passed[j]:
                cp.start()
        for cp in copies(0, sibling, me):
            cp.wait_recv()
        for j, chip in enumerate(chips):
            for cp in copies(4 + j, (*chip, 1 - c), me):
                cp.wait_recv()
        for group in first + passed:
            for cp in group:
                cp.wait_send()
        for cp in mine:
            cp.wait()

    return pl.pallas_call(
        body, name=name, in_specs=[HBM] * nt, out_specs=[HBM] * nt,
        out_shape=[jax.ShapeDtypeStruct(full_shape(a, cut), a.dtype) for a, cut in zip(shards, cuts)],
        scratch_shapes=[pltpu.SemaphoreType.DMA((7 * nt,)), pltpu.SemaphoreType.DMA((7 * nt,)),
                        pltpu.SemaphoreType.DMA((nt,))],
    )(*shards)


def _exchange_copies(items, ins, outs, send_sems, recv_sems, local_sems, gather=False):
    nt = len(items)
    x, y, c = _place()
    me = _slot((x, y, c))
    flip = lambda v, bit: 1 - v if bit else v

    def part(ref, cut, s):
        return ref.at[s] if cut is None else _cut(ref, cut[0], cut[1], s)

    def src(t, s):
        return ins[t] if gather else part(ins[t], items[t][0], s)

    def dst(t, s):
        if gather:
            return part(outs[items[t][1]], items[t][0], s)
        d = outs[items[t][1]].at[s]
        return d if items[t][2] is None else d.at[items[t][2]]

    sends, arrivals = [], []
    for k in range(1, N_DEV):
        peer = (flip(x, k & 4), flip(y, k & 2), flip(c, k & 1))
        ps = _slot(peer)
        for t in range(nt):
            sems = dict(send_sem=send_sems.at[nt * (k - 1) + t], recv_sem=recv_sems.at[nt * (k - 1) + t],
                        device_id=peer, device_id_type=MESH_ID)
            sends.append(pltpu.make_async_remote_copy(src_ref=src(t, ps), dst_ref=dst(t, me), **sems))
            arrivals.append(pltpu.make_async_remote_copy(src_ref=src(t, ps), dst_ref=dst(t, ps), **sems))
    mine = [pltpu.make_async_copy(src(t, me), dst(t, me), local_sems.at[t]) for t in range(nt)]
    return sends, arrivals, mine


SEMS = pl.BlockSpec(memory_space=pltpu.SEMAPHORE)
SIDE_EFFECT = pltpu.SideEffectType.DATAFLOW_SIDE_EFFECTING


def _exchange_start(srcs, items, landings, *, name, gather=False, carry=()):
    nt, nl, nc = len(srcs), len(landings), len(carry)

    def body(*refs):
        ins, lands = refs[:nt], refs[nt:nt + nl]
        send_sems, recv_sems, local_sems = refs[nt + nl + nc:nt + nl + nc + 3]
        token = refs[-1]
        sends, _, mine = _exchange_copies(items, ins, lands, send_sems, recv_sems, local_sems, gather)
        for cp in sends + mine:
            cp.start()
        token[...] = jnp.zeros_like(token)

    hbm = lambda a: pltpu.HBM(a.shape, a.dtype)
    fresh = [pltpu.with_memory_space_constraint(
        lax.empty(l.shape, l.dtype) if isinstance(l, jax.ShapeDtypeStruct) else l, pltpu.HBM) for l in landings]
    res = pl.pallas_call(
        body, name=name, in_specs=[HBM] * (nt + nl + nc),
        out_specs=[SEMS, SEMS, SEMS] + [HBM] * (nt + nl + nc) + [pl.BlockSpec(memory_space=pltpu.VMEM)],
        out_shape=[pltpu.SemaphoreType.DMA((7 * nt,)), pltpu.SemaphoreType.DMA((7 * nt,)), pltpu.SemaphoreType.DMA((nt,))]
        + [hbm(a) for a in srcs] + [hbm(l) for l in landings] + [hbm(a) for a in carry]
        + [jax.ShapeDtypeStruct((8, LANES), F32)],
        input_output_aliases={i: 3 + i for i in range(nt + nl + nc)},
        compiler_params=pltpu.CompilerParams(has_side_effects=SIDE_EFFECT),
    )(*[pltpu.with_memory_space_constraint(a, pltpu.HBM) for a in srcs], *fresh,
      *[pltpu.with_memory_space_constraint(a, pltpu.HBM) for a in carry])
    return res[:3 + nt + nl], res[-1][0:1, 0:1], res[3 + nt + nl:-1]


def _exchange_wait(state, items, nt, after, *, name, gather=False):
    sems, bufs = state[:3], state[3:]
    nl = len(bufs) - nt

    def body(*refs):
        ins, lands = refs[:nt], refs[nt:nt + nl]
        send_sems, recv_sems, local_sems = refs[nt + nl:nt + nl + 3]
        sends, arrivals, mine = _exchange_copies(items, ins, lands, send_sems, recv_sems, local_sems, gather)
        for sent, landed in zip(sends, arrivals):
            landed.wait_recv()
            sent.wait_send()
        for cp in mine:
            cp.wait()

    hbm = lambda a: pltpu.HBM(a.shape, a.dtype)
    res = pl.pallas_call(
        body, name=name, in_specs=[HBM] * (nt + nl) + [SEMS, SEMS, SEMS, HBM], out_specs=[HBM] * (nt + nl),
        out_shape=[hbm(a) for a in bufs],
        input_output_aliases={i: i for i in range(nt + nl)},
        compiler_params=pltpu.CompilerParams(has_side_effects=SIDE_EFFECT),
    )(*bufs, *sems, after)
    return res[nt:]


def _adamw(recv, w, m, v, *, split, name):
    shape = w.shape
    axis, parts = split
    block = shape[:axis] + (shape[axis] // parts,) + shape[axis + 1:]
    nd = len(shape)

    def body(r_ref, w_ref, m_ref, v_ref, g_ref, d_ref, nm_ref, nv_ref):
        g = r_ref[0].astype(F32)
        for j in range(1, N_DEV):
            g = g + r_ref[j].astype(F32)
        nm = ADAM_B1 * m_ref[...] + (1.0 - ADAM_B1) * g
        nv = ADAM_B2 * v_ref[...] + (1.0 - ADAM_B2) * (g * g)
        m_hat = nm / (1.0 - ADAM_B1 ** ADAM_STEP)
        v_hat = nv / (1.0 - ADAM_B2 ** ADAM_STEP)
        g_ref[...] = g
        nm_ref[...] = nm
        nv_ref[...] = nv
        d_ref[...] = -ADAM_LR * (m_hat / (jnp.sqrt(v_hat) + ADAM_EPS) + ADAM_WD * w_ref[...])

    at = lambda i: tuple(i if a == axis else 0 for a in range(nd))
    one = pl.BlockSpec(block, at)
    shp = jax.ShapeDtypeStruct(shape, F32)
    return pl.pallas_call(
        body, name=name, grid=(parts,),
        in_specs=[pl.BlockSpec((N_DEV,) + block, lambda i: (0,) + at(i)), one, one, one],
        out_specs=[one, one, one, one], out_shape=[shp, shp, shp, shp],
        compiler_params=_cparams("parallel"),
    )(recv, w, m, v)


BIG = ("w_in", "w_mem_kv", "w_out", "pool_w", "w_kv_shared")
SMALL = ("pool_scale", "ln_g", "ln_b", "b_forget")
SMALL_ROWS = 40
ADAM_SPLIT = {"w_in": (1, 4), "w_mem_kv": (0, 2), "w_out": (0, 2), "pool_w": (0, 1), "w_kv_shared": (0, 4)}
PER_LAYER_CUT = {"w_out": (0, D_MIX // N_DEV), "w_mem_kv": (0, D_MODEL // N_DEV), "w_in": (1, D_IN // N_DEV),
                 "pool_w": (1, POOL_GROUP // N_DEV)}
EARLY = ("w_out", "w_mem_kv", "w_in", "w_kv_shared")
EARLY_ITEMS = [(PER_LAYER_CUT[n], i, 1) for i, n in enumerate(EARLY[:3])] + [(None, 3, None)]


def _flat(parts, rows):
    v = jnp.concatenate([p.reshape(-1) for p in parts])
    return jnp.pad(v, (0, rows * LANES - v.shape[0])).reshape(rows, LANES)


def _unflat(flat, shapes):
    v, out, off = flat.reshape(-1), [], 0
    for s in shapes:
        n = math.prod(s)
        out.append(v[off:off + n].reshape(s))
        off += n
    return out


def kernel(x, mem, w_in, w_mem_kv, w_out, ln_g, ln_b, pool_w, pool_scale, w_kv_shared, b_forget, loss_target, m_w_in, m_w_mem_kv, m_w_out, m_ln_g, m_ln_b, m_pool_w, m_pool_scale, m_w_kv_shared, m_b_forget, v_w_in, v_w_mem_kv, v_w_out, v_ln_g, v_ln_b, v_pool_w, v_pool_scale, v_w_kv_shared, v_b_forget):
    w = dict(w_in=w_in, w_mem_kv=w_mem_kv, w_out=w_out, ln_g=ln_g, ln_b=ln_b, pool_w=pool_w[0],
             pool_scale=pool_scale, w_kv_shared=w_kv_shared, b_forget=b_forget)
    m = dict(w_in=m_w_in, w_mem_kv=m_w_mem_kv, w_out=m_w_out, ln_g=m_ln_g, ln_b=m_ln_b, pool_w=m_pool_w[0],
             pool_scale=m_pool_scale, w_kv_shared=m_w_kv_shared, b_forget=m_b_forget)
    v = dict(w_in=v_w_in, w_mem_kv=v_w_mem_kv, w_out=v_w_out, ln_g=v_ln_g, ln_b=v_ln_b, pool_w=v_pool_w[0],
             pool_scale=v_pool_scale, w_kv_shared=v_w_kv_shared, b_forget=v_b_forget)

    wb = {n: w[n].astype(BF16) for n in BIG}
    bdt = wb["w_in"].dtype
    win0, pscale = _all_gather([wb["w_in"][0], jnp.pad(pool_scale, ((0, 7), (0, 0)))], [PER_LAYER_CUT["w_in"], None],
                               name="gather_weights")
    pscale = pscale[:, 0, :].reshape(1, D_MAIN)
    next_srcs = [wb["w_mem_kv"], wb["pool_w"], wb["w_out"][0]]
    next_items = [(SLICED["w_mem_kv"], 0, None), (SLICED["pool_w"], 1, None), (PER_LAYER_CUT["w_out"], 2, None)]
    next_state, _, (win0,) = _exchange_start(
        next_srcs, next_items,
        [jax.ShapeDtypeStruct((2, D_MODEL, D_MODEL), bdt), jax.ShapeDtypeStruct((4, POOL_GROUP, POOL_GROUP), bdt),
         jax.ShapeDtypeStruct((D_MIX, D_MODEL), bdt)], name="gather_next_start", gather=True, carry=[win0])
    late_srcs = [wb["w_in"][1], wb["w_out"][1], wb["w_kv_shared"]]
    late_items = [(PER_LAYER_CUT["w_in"], 0, None), (PER_LAYER_CUT["w_out"], 1, None), (None, 2, None)]
    late = {}
    bias = jnp.pad(b_forget, (0, LANES - FOX_HEADS)).reshape(1, LANES)

    def next_weights(h0):
        wmkv, pw, wout0 = _exchange_wait(next_state, next_items, len(next_srcs), h0, name="gather_next_wait", gather=True)
        late["state"], _, (pw,) = _exchange_start(
            late_srcs, late_items,
            [jax.ShapeDtypeStruct((D_MODEL, D_IN), bdt), jax.ShapeDtypeStruct((D_MIX, D_MODEL), bdt),
             jax.ShapeDtypeStruct((N_DEV,) + w_kv_shared.shape, bdt)], name="gather_late_start", gather=True, carry=[pw])
        return wmkv, pw, wout0

    def late_weights(x1):
        win1, wout1, wkv = _exchange_wait(late["state"], late_items, len(late_srcs), x1, name="gather_late_wait",
                                          gather=True)
        wkv = jnp.pad(wkv.transpose(1, 0, 2).reshape(D_MODEL, -1), ((0, 0), (0, LANES - FOX_HEADS)))
        return win1, wout1, wkv, wkv[:, 2 * D_MAIN:]

    early = {}

    def send_early(g):
        srcs = [g[n] for n in EARLY]
        lands = [jax.ShapeDtypeStruct((N_DEV, 2) + w[n].shape[1:], g[n].dtype) for n in EARLY[:3]]
        lands.append(jax.ShapeDtypeStruct(g["w_kv_shared"].shape, g["w_kv_shared"].dtype))
        early["state"], anchor, _ = _exchange_start(srcs, EARLY_ITEMS, lands, name="exchange_early_start")
        return anchor

    last_items = [(cut, i, 0) for i, (cut, _, _) in enumerate(EARLY_ITEMS[:3])] + [(PER_LAYER_CUT["pool_w"], 3, None),
                                                                                  (None, 4, None)]

    def send_late(g, dz0):
        small = jnp.concatenate([g["pool_scale"].reshape(N_DEV, -1)]
                                + [jnp.broadcast_to(g[n].reshape(1, -1), (N_DEV, g[n].size)) for n in SMALL[1:]], axis=1)
        small = jnp.pad(small, ((0, 0), (0, SMALL_ROWS * LANES - small.shape[1]))).reshape(N_DEV, SMALL_ROWS, LANES)
        r_out, r_mkv, r_in, early["r_kv"] = _exchange_wait(early["state"], EARLY_ITEMS, len(EARLY), g["w_in"][0],
                                                           name="exchange_early_wait")
        srcs = [g[n][0] for n in EARLY[:3]] + [g["pool_w"][0], small]
        early["last"], _, (dz0,) = _exchange_start(
            srcs, last_items,
            [r_out, r_mkv, r_in, jax.ShapeDtypeStruct((N_DEV,) + w["pool_w"].shape, srcs[3].dtype),
             jax.ShapeDtypeStruct(small.shape, small.dtype)], name="exchange_last_start", carry=[dz0])
        return dz0

    sq, gx, grads = _local_step(x[0], mem[0], loss_target[0], win0, pscale, next_weights, late_weights,
                                ln_g, ln_b, bias, send_early, send_late)
    loss = lax.psum((0.5 / D_MODEL) * jnp.sum(sq), ("x", "y", "c"))

    r_out, r_mkv, r_in, r_pw, r_small = _exchange_wait(early["last"], last_items, len(last_items), gx,
                                                       name="exchange_last_wait")
    recv = dict(w_out=r_out, w_mem_kv=r_mkv, w_in=r_in, pool_w=r_pw, w_kv_shared=early["r_kv"], small=r_small)

    outs = {}
    for n in BIG:
        res = _adamw(recv[n], w[n], m[n], v[n], split=ADAM_SPLIT[n], name="adamw_" + n)
        for kind, a in zip(("grad", "delta", "new_m", "new_v"), res):
            outs[kind, n] = a[None] if n == "pool_w" else a
    small_shapes = [w[n].shape for n in SMALL]
    res = _adamw(recv["small"], _flat([w[n] for n in SMALL], SMALL_ROWS), _flat([m[n] for n in SMALL], SMALL_ROWS),
                 _flat([v[n] for n in SMALL], SMALL_ROWS), split=(0, 1), name="adamw_small")
    for kind, flat in zip(("grad", "delta", "new_m", "new_v"), res):
        for n, a in zip(SMALL, _unflat(flat, small_shapes)):
            outs[kind, n] = a
    order = ("w_in", "w_mem_kv", "w_out", "ln_g", "ln_b", "pool_w", "pool_scale", "w_kv_shared", "b_forget")
    return (loss, gx[None], *[outs[kind, n] for kind in ("grad", "delta", "new_m", "new_v") for n in order])
```

```python
import math
import types

import numpy as np
import jax
import jax.numpy as jnp
from jax import lax
from jax.experimental import pallas as pl
from jax.experimental.pallas import tpu as pltpu

F32 = jnp.float32
BF16 = jnp.bfloat16

D_MODEL = 1024
D_MAIN = 1024
D_MEM = 512
D_MIX = D_MAIN + D_MEM
D_IN = 2 * D_MIX
N_MEM = 256
MEM_HEADS = 4
MEM_HEAD_DIM = 128
FOX_HEADS = 16
FOX_HEAD_DIM = 64
FOX_PAIRS = FOX_HEADS // 2
POOL_WINDOWS = (2, 4, 8, 16)
POOL_GROUP = 256
POOL_HALO = 16
ALPHA = 4.0 ** 0.25
LN_EPS = 1e-5
NEG = -1e30
LANES = 128
N_DEV = 8

ADAM_LR = 0.001
ADAM_B1 = 0.9
ADAM_B2 = 0.999
ADAM_EPS = 1e-08
ADAM_WD = 0.01
ADAM_STEP = 10

VMEM_LIMIT = 56 * 1024 * 1024

NN = (((1,), (0,)), ((), ()))
NT = (((1,), (1,)), ((), ()))
TN = (((0,), (0,)), ((), ()))


def _cparams(*sem):
    return pltpu.CompilerParams(dimension_semantics=sem, vmem_limit_bytes=VMEM_LIMIT)


def _sigmoid(z):
    return 1.0 / (1.0 + jnp.exp(-z))


def _mm(a, b, *, mode, out_dtype, tm, tn, tk, name, add=None, add_scale=1.0):
    if mode == "nn":
        (M, K), (K2, N) = a.shape, b.shape
    elif mode == "nt":
        (M, K), (N, K2) = a.shape, b.shape
    else:
        (K, M), (K2, N) = a.shape, b.shape
    assert K == K2, (a.shape, b.shape, mode)
    tm, tn, tk = min(tm, M), min(tn, N), min(tk, K)
    assert M % tm == 0 and N % tn == 0 and K % tk == 0, (M, N, K, tm, tn, tk)
    gm, gn, gk = M // tm, N // tn, K // tk
    dims = {"nn": NN, "nt": NT, "tn": TN}[mode]
    if mode == "tn":
        a_spec = pl.BlockSpec((tk, tm), lambda i, j, k: (k, i))
    else:
        a_spec = pl.BlockSpec((tm, tk), lambda i, j, k: (i, k))
    if mode == "nt":
        b_spec = pl.BlockSpec((tn, tk), lambda i, j, k: (j, k))
    else:
        b_spec = pl.BlockSpec((tk, tn), lambda i, j, k: (k, j))
    o_spec = pl.BlockSpec((tm, tn), lambda i, j, k: (i, j))
    has_add = add is not None
    acc_in_out = out_dtype == F32

    def body(*refs):
        a_ref, b_ref = refs[0], refs[1]
        add_ref = refs[2] if has_add else None
        o_ref = refs[3] if has_add else refs[2]
        prod = lax.dot_general(a_ref[...].astype(BF16), b_ref[...].astype(BF16), dims,
                               preferred_element_type=F32)

        def finish(r):
            if has_add:
                r = r + add_scale * add_ref[...]
            o_ref[...] = r.astype(out_dtype)

        if gk == 1:
            finish(prod)
        else:
            acc_ref = o_ref if acc_in_out else refs[-1]
            k = pl.program_id(2)

            @pl.when(k == 0)
            def _():
                acc_ref[...] = prod

            @pl.when(k > 0)
            def _():
                acc_ref[...] += prod

            if has_add or not acc_in_out:
                @pl.when(k == gk - 1)
                def _():
                    finish(acc_ref[...])

    in_specs = [a_spec, b_spec] + ([o_spec] if has_add else [])
    args = (a, b) + ((add,) if has_add else ())
    return pl.pallas_call(
        body, name=name, grid=(gm, gn, gk), in_specs=in_specs, out_specs=o_spec,
        out_shape=jax.ShapeDtypeStruct((M, N), out_dtype),
        scratch_shapes=[pltpu.VMEM((tm, tn), F32)] if gk > 1 and not acc_in_out else [],
        compiler_params=_cparams("parallel", "parallel", "arbitrary"),
    )(*args)


def _ln_stats(z):
    mu = jnp.mean(z, axis=1, keepdims=True)
    zc = z - mu
    var = jnp.mean(zc * zc, axis=1, keepdims=True)
    rstd = lax.rsqrt(var + LN_EPS)
    return zc * rstd, rstd


def _ln_bwd_math(dy, xhat, rstd, g):
    dxh = dy * g
    m1 = jnp.mean(dxh, axis=1, keepdims=True)
    m2 = jnp.mean(dxh * xhat, axis=1, keepdims=True)
    return rstd * (dxh - m1 - xhat * m2)


def _out_ln(yg, wout, x, g, b, *, tb, name):
    S = x.shape[0]
    tb = min(tb, S)

    def body(yg_ref, w_ref, x_ref, g_ref, b_ref, y_ref, xhat_ref, rstd_ref, yt_ref):
        o = jnp.dot(yg_ref[...], w_ref[...], preferred_element_type=F32)
        xhat, rstd = _ln_stats(ALPHA * x_ref[...] + o)
        xhat_ref[...] = xhat
        rstd_ref[...] = rstd
        y = xhat * g_ref[...] + b_ref[...]
        y_ref[...] = y
        yt_ref[...] = y.T.astype(BF16)

    row = pl.BlockSpec((tb, D_MODEL), lambda i: (i, 0))
    vec = pl.BlockSpec((1, D_MODEL), lambda i: (0, 0))
    return pl.pallas_call(
        body, name=name, grid=(S // tb,),
        in_specs=[pl.BlockSpec((tb, D_MIX), lambda i: (i, 0)), pl.BlockSpec((D_MIX, D_MODEL), lambda i: (0, 0)),
                  row, vec, vec],
        out_specs=[row, row, pl.BlockSpec((tb, 1), lambda i: (i, 0)), pl.BlockSpec((D_MODEL, tb), lambda i: (0, i))],
        out_shape=[jax.ShapeDtypeStruct((S, D_MODEL), F32), jax.ShapeDtypeStruct((S, D_MODEL), F32),
                   jax.ShapeDtypeStruct((S, 1), F32), jax.ShapeDtypeStruct((D_MODEL, S), BF16)],
        compiler_params=_cparams("parallel"),
    )(yg, wout, x, g, b)


def _in_proj_t(x, w, *, tm, name):
    S = x.shape[0]
    tm = min(tm, S)

    def body(x_ref, w_ref, h_ref, xt_ref):
        xv = x_ref[...]
        h_ref[...] = jnp.dot(xv.astype(BF16), w_ref[...], preferred_element_type=F32).astype(BF16)
        xt_ref[...] = xv.T.astype(BF16)

    return pl.pallas_call(
        body, name=name, grid=(S // tm,),
        in_specs=[pl.BlockSpec((tm, D_MODEL), lambda i: (i, 0)), pl.BlockSpec((D_MODEL, D_IN), lambda i: (0, 0))],
        out_specs=[pl.BlockSpec((tm, D_IN), lambda i: (i, 0)), pl.BlockSpec((D_MODEL, tm), lambda i: (0, i))],
        out_shape=[jax.ShapeDtypeStruct((S, D_IN), BF16), jax.ShapeDtypeStruct((D_MODEL, S), BF16)],
        compiler_params=_cparams("parallel"),
    )(x, w)


def _out_ln_loss(yg, wout, x, g, b, target, *, tb, name):
    S = x.shape[0]
    tb = min(tb, S)

    def body(yg_ref, w_ref, x_ref, g_ref, b_ref, t_ref, dz_ref, dg_ref, db_ref, sq_ref):
        @pl.when(pl.program_id(0) == 0)
        def _():
            dg_ref[...] = jnp.zeros_like(dg_ref)
            db_ref[...] = jnp.zeros_like(db_ref)
            sq_ref[...] = jnp.zeros_like(sq_ref)

        o = jnp.dot(yg_ref[...], w_ref[...], preferred_element_type=F32)
        xhat, rstd = _ln_stats(ALPHA * x_ref[...] + o)
        err = xhat * g_ref[...] + b_ref[...] - t_ref[...]
        sq_ref[...] += jnp.sum(err * err, axis=0, keepdims=True)
        dy = err * (1.0 / D_MODEL)
        dz_ref[...] = _ln_bwd_math(dy, xhat, rstd, g_ref[...])
        dg_ref[...] += jnp.sum(dy * xhat, axis=0, keepdims=True)
        db_ref[...] += jnp.sum(dy, axis=0, keepdims=True)

    row = pl.BlockSpec((tb, D_MODEL), lambda i: (i, 0))
    vec = pl.BlockSpec((1, D_MODEL), lambda i: (0, 0))
    vshape = jax.ShapeDtypeStruct((1, D_MODEL), F32)
    return pl.pallas_call(
        body, name=name, grid=(S // tb,),
        in_specs=[pl.BlockSpec((tb, D_MIX), lambda i: (i, 0)), pl.BlockSpec((D_MIX, D_MODEL), lambda i: (0, 0)),
                  row, vec, vec, row],
        out_specs=[row, vec, vec, vec],
        out_shape=[jax.ShapeDtypeStruct((S, D_MODEL), F32), vshape, vshape, vshape],
        compiler_params=_cparams("arbitrary"),
    )(yg, wout, x, g, b, target)


def _ln_bwd(dy, xhat, rstd, g, *, tb, name):
    S = dy.shape[0]
    tb = min(tb, S)

    def body(dy_ref, xhat_ref, rstd_ref, g_ref, dz_ref, dg_ref, db_ref):
        @pl.when(pl.program_id(0) == 0)
        def _():
            dg_ref[...] = jnp.zeros_like(dg_ref)
            db_ref[...] = jnp.zeros_like(db_ref)

        dy_, xhat_ = dy_ref[...], xhat_ref[...]
        dz_ref[...] = _ln_bwd_math(dy_, xhat_, rstd_ref[...], g_ref[...])
        dg_ref[...] += jnp.sum(dy_ * xhat_, axis=0, keepdims=True)
        db_ref[...] += jnp.sum(dy_, axis=0, keepdims=True)

    row = pl.BlockSpec((tb, D_MODEL), lambda i: (i, 0))
    vec = pl.BlockSpec((1, D_MODEL), lambda i: (0, 0))
    return pl.pallas_call(
        body, name=name, grid=(S // tb,),
        in_specs=[row, row, pl.BlockSpec((tb, 1), lambda i: (i, 0)), vec],
        out_specs=[row, vec, vec],
        out_shape=[jax.ShapeDtypeStruct((S, D_MODEL), F32), jax.ShapeDtypeStruct((1, D_MODEL), F32),
                   jax.ShapeDtypeStruct((1, D_MODEL), F32)],
        compiler_params=_cparams("arbitrary"),
    )(dy, xhat, rstd, g)


def _gate_fwd(ysrc, scale, h, ymem, *, tb, name):
    S = ysrc.shape[0]
    tb = min(tb, S)

    def body(ys_ref, sc_ref, ga_ref, gb_ref, gc_ref, ym_ref, yg_ref):
        ymain = ys_ref[...] * sc_ref[...]
        for k, g_ref in enumerate((ga_ref, gb_ref)):
            gv = g_ref[...].astype(F32)
            yg_ref[:, 512 * k:512 * (k + 1)] = (ymain[:, 512 * k:512 * (k + 1)] * gv * _sigmoid(gv)).astype(BF16)
        gv = gc_ref[...].astype(F32)
        yg_ref[:, 1024:1536] = (ym_ref[...] * gv * _sigmoid(gv)).astype(BF16)

    slab = lambda c: pl.BlockSpec((tb, 512), lambda i, c=c: (i, c))
    return pl.pallas_call(
        body, name=name, grid=(S // tb,),
        in_specs=[pl.BlockSpec((tb, D_MAIN), lambda i: (i, 0)), pl.BlockSpec((1, D_MAIN), lambda i: (0, 0)),
                  slab(3), slab(4), slab(5), pl.BlockSpec((tb, D_MEM), lambda i: (i, 0))],
        out_specs=pl.BlockSpec((tb, D_MIX), lambda i: (i, 0)),
        out_shape=jax.ShapeDtypeStruct((S, D_MIX), BF16),
        compiler_params=_cparams("parallel"),
    )(ysrc, scale, h, h, h, ymem)


def _gate_bwd(dyg, ysrc, scale, h, ymem, *, tb, name, fox=False):
    S = ysrc.shape[0]
    tb = min(tb, S)

    def dsilu(gv):
        sg = _sigmoid(gv)
        return sg, sg * (1.0 + gv * (1.0 - sg))

    def body(da_ref, db_ref, dc_ref, ys_ref, sc_ref, ga_ref, gb_ref, gc_ref, ym_ref, dym_ref, dymem_ref, dh_ref):
        ymain = ys_ref[...] * sc_ref[...]
        lane = lax.broadcasted_iota(jnp.int32, (tb, LANES), 1)
        first = lane < FOX_HEAD_DIM
        for k, (d_ref, g_ref) in enumerate(((da_ref, ga_ref), (db_ref, gb_ref))):
            gv, d = g_ref[...].astype(F32), d_ref[...].astype(F32)
            sg, ds = dsilu(gv)
            dy = d * gv * sg
            dh_ref[:, 512 * k:512 * (k + 1)] = (d * ymain[:, 512 * k:512 * (k + 1)] * ds).astype(BF16)
            if not fox:
                dym_ref[:, 512 * k:512 * (k + 1)] = dy
                continue
            for q in range(512 // LANES):
                cols = slice(LANES * q, LANES * (q + 1))
                dy2 = dy[:, cols]
                prod = dy2 * ymain[:, 512 * k + LANES * q:512 * k + LANES * (q + 1)]
                for hh in range(2):
                    delta = jnp.sum(jnp.where(first == (hh == 0), prod, 0.0), axis=1, keepdims=True)
                    dyh = dy2 if hh == 0 else _swap_halves(dy2)
                    c0 = LANES * (2 * (4 * k + q) + hh)
                    dym_ref[:, c0:c0 + LANES] = jnp.where(
                        first, dyh, _lanes3(lane, AUX, _split3(-delta), 0.0)).astype(BF16)
        gv, d = gc_ref[...].astype(F32), dc_ref[...].astype(F32)
        sg, ds = dsilu(gv)
        dymem_ref[...] = d * gv * sg
        dh_ref[:, 1024:1536] = (d * ym_ref[...] * ds).astype(BF16)

    slab = lambda c: pl.BlockSpec((tb, 512), lambda i, c=c: (i, c))
    return pl.pallas_call(
        body, name=name, grid=(S // tb,),
        in_specs=[slab(0), slab(1), slab(2),
                  pl.BlockSpec((tb, D_MAIN), lambda i: (i, 0)), pl.BlockSpec((1, D_MAIN), lambda i: (0, 0)),
                  slab(3), slab(4), slab(5), pl.BlockSpec((tb, D_MEM), lambda i: (i, 0))],
        out_specs=[pl.BlockSpec((tb, 2 * D_MAIN if fox else D_MAIN), lambda i: (i, 0)),
                   pl.BlockSpec((tb, D_MEM), lambda i: (i, 0)), pl.BlockSpec((tb, D_MIX), lambda i: (i, 1))],
        out_shape=[jax.ShapeDtypeStruct((S, 2 * D_MAIN), BF16) if fox else jax.ShapeDtypeStruct((S, D_MAIN), F32),
                   jax.ShapeDtypeStruct((S, D_MEM), F32), jax.ShapeDtypeStruct((S, D_IN), BF16)],
        compiler_params=_cparams("parallel"),
    )(dyg, dyg, dyg, ysrc, scale, h, h, h, ymem)


def _window_count(t0, rows, w):
    t = t0 + lax.broadcasted_iota(jnp.int32, (rows, POOL_GROUP), 0)
    return jnp.minimum(t + 1, w).astype(F32)


def _pool_fwd(h, pw, *, tb, name):
    S = h.shape[0]
    tb = min(tb, S)

    def body(u_ref, pw_ref, pm_ref, mixed_ref, tail_ref):
        i = pl.program_id(0)

        @pl.when(i == 0)
        def _():
            tail_ref[...] = jnp.zeros_like(tail_ref)

        u = u_ref[...].astype(F32)
        xfull = jnp.concatenate([tail_ref[...], u], axis=0)
        for gi, w in enumerate(POOL_WINDOWS):
            cols = slice(POOL_GROUP * gi, POOL_GROUP * (gi + 1))
            s = xfull[:, cols]
            sh = 1
            while sh < w:
                s = s + pltpu.roll(s, sh, 0)
                sh *= 2
            pm = s[POOL_HALO:, :] / _window_count(i * tb, tb, w) - u[:, cols]
            pmb = pm.astype(BF16)
            pm_ref[:, cols] = pmb
            mixed_ref[:, cols] = jnp.dot(pmb, pw_ref[gi], preferred_element_type=F32)
        tail_ref[...] = u[tb - POOL_HALO:, :]

    return pl.pallas_call(
        body, name=name, grid=(S // tb,),
        in_specs=[pl.BlockSpec((tb, D_MAIN), lambda i: (i, 0)),
                  pl.BlockSpec((4, POOL_GROUP, POOL_GROUP), lambda i: (0, 0, 0))],
        out_specs=[pl.BlockSpec((tb, D_MAIN), lambda i: (i, 0)), pl.BlockSpec((tb, D_MAIN), lambda i: (i, 0))],
        out_shape=[jax.ShapeDtypeStruct((S, D_MAIN), BF16), jax.ShapeDtypeStruct((S, D_MAIN), F32)],
        scratch_shapes=[pltpu.VMEM((POOL_HALO, D_MAIN), F32)],
        compiler_params=_cparams("arbitrary"),
    )(h, pw)


def _pool_bwd(dymain, pm, mixed, pw, scale, dh, *, tb, name):
    S = dymain.shape[0]
    tb = min(tb, S)
    nb = S // tb
    n = tb + POOL_HALO

    def body(dy_ref, pm_ref, mixed_ref, pw_ref, sc_ref, dh_in, dh_ref, dpw_ref, dsc_ref, head_ref, dpw_acc):
        del dh_in
        i = pl.program_id(0)

        @pl.when(i == 0)
        def _():
            head_ref[...] = jnp.zeros_like(head_ref)
            dpw_acc[...] = jnp.zeros_like(dpw_acc)
            dsc_ref[...] = jnp.zeros_like(dsc_ref)

        dy = dy_ref[...]
        dsc_ref[...] += jnp.sum(dy * mixed_ref[...], axis=0, keepdims=True)
        dmixed = dy * sc_ref[...]
        t0 = (nb - 1 - i) * tb
        for gi, w in enumerate(POOL_WINDOWS):
            cols = slice(POOL_GROUP * gi, POOL_GROUP * (gi + 1))
            dm = dmixed[:, cols].astype(BF16)
            dpw_acc[gi] += lax.dot_general(pm_ref[:, cols], dm, TN, preferred_element_type=F32)
            dpm = lax.dot_general(dm, pw_ref[gi], NT, preferred_element_type=F32)
            e = dpm / _window_count(t0, tb, w)
            s = jnp.concatenate([e, head_ref[:, cols]], axis=0)
            sh = 1
            while sh < w:
                s = s + pltpu.roll(s, n - sh, 0)
                sh *= 2
            dh_ref[:, cols] = (s[:tb, :] - dpm).astype(BF16)
            head_ref[:, cols] = e[:POOL_HALO, :]

        @pl.when(i == nb - 1)
        def _():
            dpw_ref[...] = dpw_acc[...].astype(BF16)

    rev = lambda i: (nb - 1 - i, 0)
    return pl.pallas_call(
        body, name=name, grid=(nb,),
        in_specs=[pl.BlockSpec((tb, D_MAIN), rev), pl.BlockSpec((tb, D_MAIN), rev), pl.BlockSpec((tb, D_MAIN), rev),
                  pl.BlockSpec((4, POOL_GROUP, POOL_GROUP), lambda i: (0, 0, 0)),
                  pl.BlockSpec((1, D_MAIN), lambda i: (0, 0)), pl.BlockSpec(memory_space=pl.ANY)],
        out_specs=[pl.BlockSpec((tb, D_MAIN), rev),
                   pl.BlockSpec((4, POOL_GROUP, POOL_GROUP), lambda i: (0, 0, 0)),
                   pl.BlockSpec((1, D_MAIN), lambda i: (0, 0))],
        out_shape=[jax.ShapeDtypeStruct(dh.shape, dh.dtype),
                   jax.ShapeDtypeStruct((4, POOL_GROUP, POOL_GROUP), BF16), jax.ShapeDtypeStruct((1, D_MAIN), F32)],
        scratch_shapes=[pltpu.VMEM((POOL_HALO, D_MAIN), F32), pltpu.VMEM((4, POOL_GROUP, POOL_GROUP), F32)],
        input_output_aliases={5: 0},
        compiler_params=_cparams("arbitrary"),
    )(dymain, pm, mixed, pw, scale, dh)


MEM_SCALE = MEM_HEAD_DIM ** -0.5


def _mem_probs(q_ref, mkv_ref, hd):
    cols = slice(MEM_HEAD_DIM * hd, MEM_HEAD_DIM * (hd + 1))
    q = (q_ref[:, cols].astype(F32) * MEM_SCALE).astype(BF16)
    mk = mkv_ref[:, cols].astype(BF16)
    mv = mkv_ref[:, D_MEM + MEM_HEAD_DIM * hd:D_MEM + MEM_HEAD_DIM * (hd + 1)].astype(BF16)
    s = lax.dot_general(q, mk, NT, preferred_element_type=F32)
    e = jnp.exp(s - jnp.max(s, axis=1, keepdims=True))
    return cols, q, mk, mv, e, jnp.sum(e, axis=1, keepdims=True)


def _memattn_fwd(h, mkv, *, tb, name):
    S = h.shape[0]
    tb = min(tb, S)

    def body(q_ref, mkv_ref, y_ref):
        for hd in range(MEM_HEADS):
            cols, _, _, mv, e, l = _mem_probs(q_ref, mkv_ref, hd)
            y_ref[:, cols] = jnp.dot(e.astype(BF16), mv, preferred_element_type=F32) / l

    return pl.pallas_call(
        body, name=name, grid=(S // tb,),
        in_specs=[pl.BlockSpec((tb, D_MEM), lambda i: (i, 2)), pl.BlockSpec((N_MEM, 2 * D_MEM), lambda i: (0, 0))],
        out_specs=pl.BlockSpec((tb, D_MEM), lambda i: (i, 0)),
        out_shape=jax.ShapeDtypeStruct((S, D_MEM), F32),
        compiler_params=_cparams("parallel"),
    )(h, mkv)


def _memattn_bwd(h, mkv, dy, dh, *, tb, name):
    S = h.shape[0]
    tb = min(tb, S)

    def body(q_ref, mkv_ref, dy_ref, dh_in, dh_ref, dmkv_ref):
        del dh_in

        @pl.when(pl.program_id(0) == 0)
        def _():
            dmkv_ref[...] = jnp.zeros_like(dmkv_ref)

        for hd in range(MEM_HEADS):
            cols, q, mk, mv, e, l = _mem_probs(q_ref, mkv_ref, hd)
            p = e / l
            dyh = dy_ref[:, cols].astype(BF16)
            dp = lax.dot_general(dyh, mv, NT, preferred_element_type=F32)
            ds = p * (dp - jnp.sum(dp * p, axis=1, keepdims=True))
            dsb = ds.astype(BF16)
            dh_ref[:, cols] = (jnp.dot(dsb, mk, preferred_element_type=F32) * MEM_SCALE).astype(BF16)
            dmkv_ref[:, cols] += lax.dot_general(dsb, q, TN, preferred_element_type=F32)
            vcols = slice(D_MEM + MEM_HEAD_DIM * hd, D_MEM + MEM_HEAD_DIM * (hd + 1))
            dmkv_ref[:, vcols] += lax.dot_general(p.astype(BF16), dyh, TN, preferred_element_type=F32)

    return pl.pallas_call(
        body, name=name, grid=(S // tb,),
        in_specs=[pl.BlockSpec((tb, D_MEM), lambda i: (i, 2)), pl.BlockSpec((N_MEM, 2 * D_MEM), lambda i: (0, 0)),
                  pl.BlockSpec((tb, D_MEM), lambda i: (i, 0)), pl.BlockSpec(memory_space=pl.ANY)],
        out_specs=[pl.BlockSpec((tb, D_MEM), lambda i: (i, 2)), pl.BlockSpec((N_MEM, 2 * D_MEM), lambda i: (0, 0))],
        out_shape=[jax.ShapeDtypeStruct(dh.shape, dh.dtype), jax.ShapeDtypeStruct((N_MEM, 2 * D_MEM), F32)],
        input_output_aliases={3: 0},
        compiler_params=_cparams("arbitrary"),
    )(h, mkv, dy, dh)


def _forget_fwd(fl, bias, *, tb, name):
    S = fl.shape[0]
    tb = min(tb, S)

    def body(fl_ref, b_ref, o_ref, carry_ref):
        @pl.when(pl.program_id(0) == 0)
        def _():
            carry_ref[...] = jnp.zeros_like(carry_ref)

        z = fl_ref[...] + b_ref[...]
        lf = jnp.minimum(z, 0.0) - jnp.log(1.0 + jnp.exp(-jnp.abs(z)))
        row = lax.broadcasted_iota(jnp.int32, (tb, LANES), 0)
        c = lf
        sh = 1
        while sh < tb:
            c = c + jnp.where(row >= sh, pltpu.roll(c, sh, 0), 0.0)
            sh *= 2
        o_ref[...] = -(carry_ref[...] + c)
        carry_ref[...] += jnp.sum(lf, axis=0, keepdims=True)

    return pl.pallas_call(
        body, name=name, grid=(S // tb,),
        in_specs=[pl.BlockSpec((tb, LANES), lambda i: (i, 0)), pl.BlockSpec((1, LANES), lambda i: (0, 0))],
        out_specs=pl.BlockSpec((tb, LANES), lambda i: (i, 0)),
        out_shape=jax.ShapeDtypeStruct((S, LANES), F32),
        scratch_shapes=[pltpu.VMEM((1, LANES), F32)],
        compiler_params=_cparams("arbitrary"),
    )(fl, bias)


def _forget_bwd(dn, drow, fl, bias, *, tb, name):
    S = fl.shape[0]
    tb = min(tb, S)
    nb = S // tb

    def body(dn_ref, dr_ref, fl_ref, b_ref, dh_ref, db_ref, carry_ref):
        @pl.when(pl.program_id(0) == 0)
        def _():
            carry_ref[...] = jnp.zeros_like(carry_ref)
            db_ref[...] = jnp.zeros_like(db_ref)

        src = lax.broadcasted_iota(jnp.int32, (D_MAIN, LANES), 0)
        head = lax.broadcasted_iota(jnp.int32, (D_MAIN, LANES), 1)
        pick = lambda off: jnp.where((src == FOX_HEAD_DIM * head + off) & (head < FOX_HEADS), 1.0, 0.0).astype(BF16)
        hdot = lambda a, sel: sum(jnp.dot(part.astype(BF16), sel, preferred_element_type=F32) for part in _split3(a))
        dcum = hdot(dr_ref[...], pick(3)) - hdot(dn_ref[...], pick(0))
        row = lax.broadcasted_iota(jnp.int32, (tb, LANES), 0)
        c = dcum
        sh = 1
        while sh < tb:
            c = c + jnp.where(row < tb - sh, pltpu.roll(c, tb - sh, 0), 0.0)
            sh *= 2
        dlf = carry_ref[...] + c
        carry_ref[...] += jnp.sum(dcum, axis=0, keepdims=True)
        z = fl_ref[...] + b_ref[...]
        lane = lax.broadcasted_iota(jnp.int32, (tb, LANES), 1)
        dfl = jnp.where(lane < FOX_HEADS, dlf / (1.0 + jnp.exp(z)), 0.0)
        db_ref[...] += jnp.sum(dfl, axis=0, keepdims=True)
        dh_ref[...] = dfl.astype(BF16)

    rev = lambda i: (nb - 1 - i, 0)
    return pl.pallas_call(
        body, name=name, grid=(nb,),
        in_specs=[pl.BlockSpec((tb, D_MAIN), rev), pl.BlockSpec((tb, D_MAIN), rev), pl.BlockSpec((tb, LANES), rev),
                  pl.BlockSpec((1, LANES), lambda i: (0, 0))],
        out_specs=[pl.BlockSpec((tb, LANES), rev), pl.BlockSpec((1, LANES), lambda i: (0, 0))],
        out_shape=[jax.ShapeDtypeStruct((S, LANES), BF16), jax.ShapeDtypeStruct((1, LANES), F32)],
        scratch_shapes=[pltpu.VMEM((1, LANES), F32)],
        compiler_params=_cparams("arbitrary"),
    )(dn, drow, fl, bias)


FOX_SCALE = FOX_HEAD_DIM ** -0.5
LOG2E = 1.4426950408889634
LN2 = 0.6931471805599453
AUX = FOX_HEAD_DIM


def _split3(x):
    hi = x.astype(BF16).astype(F32)
    r = x - hi
    mid = r.astype(BF16).astype(F32)
    return hi, mid, (r - mid).astype(BF16).astype(F32)


def _lanes3(lane, base, parts, rest):
    return jnp.where(lane == base, parts[0], jnp.where(lane == base + 1, parts[1],
                                                       jnp.where(lane == base + 2, parts[2], rest)))


def _swap_halves(x):
    return pltpu.roll(x, FOX_HEAD_DIM, 1)


def _causal_steps(nq, keys_outer):
    if keys_outer:
        pairs = [(i, j) for j in range(nq) for i in range(j, nq)]
    else:
        pairs = [(i, j) for i in range(nq) for j in range(i + 1)]
    it, jt = zip(*pairs)
    return jnp.asarray(np.array(it, np.int32)), jnp.asarray(np.array(jt, np.int32))


def _in_proj_fox(x, w, *, tm, name):
    S = x.shape[0]
    tm = min(tm, S)

    def body(x_ref, w_ref, h_ref, qa_ref):
        acc = jnp.dot(x_ref[...].astype(BF16), w_ref[...], preferred_element_type=F32)
        h_ref[...] = acc.astype(BF16)
        lane = lax.broadcasted_iota(jnp.int32, (tm, LANES), 1)
        first = lane < FOX_HEAD_DIM
        ones_q = jnp.where((lane >= AUX) & (lane < AUX + 3), 1.0, 0.0)
        for g in range(FOX_PAIRS):
            q = acc[:, LANES * g:LANES * (g + 1)] * (FOX_SCALE * LOG2E)
            qa_ref[:, 2 * LANES * g:2 * LANES * g + LANES] = jnp.where(first, q, ones_q).astype(BF16)
            qa_ref[:, 2 * LANES * g + LANES:2 * LANES * (g + 1)] = jnp.where(first, _swap_halves(q), ones_q).astype(BF16)

    return pl.pallas_call(
        body, name=name, grid=(S // tm,),
        in_specs=[pl.BlockSpec((tm, D_MODEL), lambda i: (i, 0)), pl.BlockSpec((D_MODEL, D_IN), lambda i: (0, 0))],
        out_specs=[pl.BlockSpec((tm, D_IN), lambda i: (i, 0)), pl.BlockSpec((tm, 2 * D_MAIN), lambda i: (i, 0))],
        out_shape=[jax.ShapeDtypeStruct((S, D_IN), BF16), jax.ShapeDtypeStruct((S, 2 * D_MAIN), BF16)],
        compiler_params=_cparams("parallel"),
    )(x, w)


def _kv_proj_fox(x, wkv, negcum, *, tm, name):
    S = x.shape[0]
    tm = min(tm, S)

    def body(x_ref, w_ref, nc_ref, ka_ref, va_ref):
        acc = jnp.dot(x_ref[...].astype(BF16), w_ref[...], preferred_element_type=F32)
        nc = nc_ref[...]
        lane = lax.broadcasted_iota(jnp.int32, (tm, LANES), 1)
        first = lane < FOX_HEAD_DIM
        ones_k = jnp.where((lane >= AUX + 3) & (lane < AUX + 6), 1.0, 0.0)
        for g in range(FOX_PAIRS):
            k = acc[:, LANES * g:LANES * (g + 1)]
            v = acc[:, D_MAIN + LANES * g:D_MAIN + LANES * (g + 1)]
            for hh in range(2):
                sl = slice(LANES * (2 * g + hh), LANES * (2 * g + hh + 1))
                kh, vh = (k, v) if hh == 0 else (_swap_halves(k), _swap_halves(v))
                ncol = jnp.sum(jnp.where(lane == 2 * g + hh, nc, 0.0), axis=1, keepdims=True) * LOG2E
                ka_ref[:, sl] = jnp.where(first, kh, _lanes3(lane, AUX, _split3(ncol), ones_k)).astype(BF16)
                va_ref[:, sl] = jnp.where(first, vh, 1.0).astype(BF16)

    out = pl.BlockSpec((tm, 2 * D_MAIN), lambda i: (i, 0))
    shp = jax.ShapeDtypeStruct((S, 2 * D_MAIN), BF16)
    return pl.pallas_call(
        body, name=name, grid=(S // tm,),
        in_specs=[pl.BlockSpec((tm, D_MODEL), lambda i: (i, 0)), pl.BlockSpec((D_MODEL, 2 * D_MAIN), lambda i: (0, 0)),
                  pl.BlockSpec((tm, LANES), lambda i: (i, 0))],
        out_specs=[out, out], out_shape=[shp, shp],
        compiler_params=_cparams("parallel"),
    )(x, wkv, negcum)


def _dwkv(xt, dk, dv, dfl, *, tk, name):
    S = xt.shape[1]
    tk = min(tk, S)

    def body(x_ref, dk_ref, dv_ref, df_ref, o_ref):
        @pl.when(pl.program_id(0) == 0)
        def _():
            o_ref[...] = jnp.zeros_like(o_ref)

        for b_ref, c0 in ((dk_ref, 0), (dv_ref, D_MAIN), (df_ref, 2 * D_MAIN)):
            o_ref[:, c0:c0 + b_ref.shape[1]] += jnp.dot(x_ref[...], b_ref[...], preferred_element_type=F32)

    row = lambda n: pl.BlockSpec((tk, n), lambda k: (k, 0))
    return pl.pallas_call(
        body, name=name, grid=(S // tk,),
        in_specs=[pl.BlockSpec((D_MODEL, tk), lambda k: (0, k)), row(D_MAIN), row(D_MAIN), row(LANES)],
        out_specs=pl.BlockSpec((D_MODEL, 2 * D_MAIN + LANES), lambda k: (0, 0)),
        out_shape=jax.ShapeDtypeStruct((D_MODEL, 2 * D_MAIN + LANES), F32),
        compiler_params=_cparams("arbitrary"),
    )(xt, dk, dv, dfl)


def _dx1(dh, win, dk, dv, dfl, wkv, dz, *, tm, name):
    S = dh.shape[0]
    tm = min(tm, S)

    def body(dh_ref, win_ref, dk_ref, dv_ref, df_ref, wk_ref, wv_ref, wf_ref, dz_ref, o_ref):
        acc = ALPHA * dz_ref[...]
        for a_ref, b_ref in ((dh_ref, win_ref), (dk_ref, wk_ref), (dv_ref, wv_ref), (df_ref, wf_ref)):
            acc = acc + lax.dot_general(a_ref[...], b_ref[...], NT, preferred_element_type=F32)
        o_ref[...] = acc

    row = lambda n: pl.BlockSpec((tm, n), lambda i: (i, 0))
    wcols = lambda n, c: pl.BlockSpec((D_MODEL, n), lambda i, c=c: (0, c))
    return pl.pallas_call(
        body, name=name, grid=(S // tm,),
        in_specs=[row(D_IN), wcols(D_IN, 0), row(D_MAIN), row(D_MAIN), row(LANES),
                  wcols(D_MAIN, 0), wcols(D_MAIN, 1), wcols(LANES, 2 * D_MAIN // LANES), row(D_MODEL)],
        out_specs=row(D_MODEL), out_shape=jax.ShapeDtypeStruct((S, D_MODEL), F32),
        compiler_params=_cparams("parallel"),
    )(dh, win, dk, dv, dfl, wkv, wkv, wkv, dz)


def _fox_fwd(qa, ka, va, *, tq, name):
    S = qa.shape[0]
    tq = min(tq, S)
    nq = S // tq
    half = tq // 2
    it, jt = _causal_steps(nq, keys_outer=False)

    def body(it_ref, jt_ref, qa_ref, ka_ref, va_ref, y_ref, qb_ref, m_ref, acc_ref):
        n = pl.program_id(1)
        i, j = it_ref[n], jt_ref[n]
        first = lax.broadcasted_iota(jnp.int32, (tq, LANES), 1) < FOX_HEAD_DIM

        @pl.when(j == 0)
        def _():
            m_ref[...] = jnp.full_like(m_ref, NEG)
            acc_ref[...] = jnp.zeros_like(acc_ref)

        def update(hh, rows, nk, masked):
            sl = slice(LANES * hh, LANES * (hh + 1))
            s = lax.dot_general(qa_ref[rows, sl], ka_ref[0:nk, sl], NT, preferred_element_type=F32)
            if masked:
                r = lax.broadcasted_iota(jnp.int32, s.shape, 0) + rows.start
                c = lax.broadcasted_iota(jnp.int32, s.shape, 1)
                s = jnp.where(c <= r, s, NEG)
            m_prev = m_ref[hh, rows]
            m_new = jnp.maximum(m_prev, jnp.max(s, axis=1, keepdims=True))
            p = jnp.exp2(s - jnp.tile(m_new, (1, nk // LANES))).astype(BF16)
            acc_ref[hh, rows] = jnp.exp2(m_prev - m_new) * acc_ref[hh, rows] + jnp.dot(
                p, va_ref[0:nk, sl], preferred_element_type=F32)
            m_ref[hh, rows] = m_new

        @pl.when(j < i)
        def _():
            for hh in range(2):
                update(hh, slice(0, tq), tq, False)

        @pl.when(j == i)
        def _():
            for hh in range(2):
                for r0 in range(0, tq, half):
                    update(hh, slice(r0, r0 + half), r0 + half, True)
            lane = lax.broadcasted_iota(jnp.int32, (tq, LANES), 1)
            ys = []
            for hh in range(2):
                sl = slice(LANES * hh, LANES * (hh + 1))
                a = acc_ref[hh]
                denom = _swap_halves(a)
                ys.append(a / denom)
                lse2 = m_ref[hh] + jnp.log(jnp.where(first, denom, a)) * LOG2E
                qb_ref[:, sl] = _lanes3(lane, AUX + 3, _split3(-lse2), qa_ref[:, sl].astype(F32)).astype(BF16)
            y_ref[...] = jnp.where(first, ys[0], _swap_halves(ys[1]))

    qblock = pl.BlockSpec((tq, 2 * LANES), lambda g, n, it, jt: (it[n], g))
    kblock = pl.BlockSpec((tq, 2 * LANES), lambda g, n, it, jt: (jt[n], g))
    return pl.pallas_call(
        body, name=name,
        grid_spec=pltpu.PrefetchScalarGridSpec(
            num_scalar_prefetch=2, grid=(FOX_PAIRS, it.shape[0]),
            in_specs=[qblock, kblock, kblock],
            out_specs=[pl.BlockSpec((tq, LANES), lambda g, n, it, jt: (it[n], g)), qblock],
            scratch_shapes=[pltpu.VMEM((2, tq, LANES), F32), pltpu.VMEM((2, tq, LANES), F32)]),
        out_shape=[jax.ShapeDtypeStruct((S, D_MAIN), F32), jax.ShapeDtypeStruct((S, 2 * D_MAIN), BF16)],
        compiler_params=_cparams("parallel", "arbitrary"),
    )(it, jt, qa, ka, va)


def _fox_bwd(qb, ka, va, dya, dh, *, tq, name):
    S = qb.shape[0]
    tq = min(tq, S)
    nq = S // tq
    half = tq // 2
    it, jt = _causal_steps(nq, keys_outer=True)
    nsteps = it.shape[0]

    def body(it_ref, jt_ref, qb_ref, ka_ref, va_ref, dya_ref, dh_in, dq_ref, dk_ref, dv_ref, dn_ref, drow_ref,
             dq_acc, dk_acc, dv_acc):
        del dh_in
        n = pl.program_id(1)
        i, j = it_ref[n], jt_ref[n]
        first = lax.broadcasted_iota(jnp.int32, (tq, LANES), 1) < FOX_HEAD_DIM

        @pl.when(n == 0)
        def _():
            dq_acc[...] = jnp.zeros_like(dq_acc)

        @pl.when(i == j)
        def _():
            dk_acc[...] = jnp.zeros_like(dk_acc)
            dv_acc[...] = jnp.zeros_like(dv_acc)

        def update(hh, keys, q0, masked):
            sl = slice(LANES * hh, LANES * (hh + 1))
            qbh, kah, dyah = qb_ref[q0:tq, sl], ka_ref[keys, sl], dya_ref[q0:tq, sl]
            eT = lax.dot_general(kah, qbh, NT, preferred_element_type=F32)
            if masked:
                r = lax.broadcasted_iota(jnp.int32, eT.shape, 0) + keys.start
                c = lax.broadcasted_iota(jnp.int32, eT.shape, 1) + q0
                eT = jnp.where(r <= c, eT, NEG)
            pT = jnp.exp2(eT)
            dsT = pT * lax.dot_general(va_ref[keys, sl], dyah, NT, preferred_element_type=F32)
            dsb = dsT.astype(BF16)
            dv_acc[hh, keys] += jnp.dot(pT.astype(BF16), dyah, preferred_element_type=F32)
            dk_acc[hh, keys] += jnp.dot(dsb, qbh, preferred_element_type=F32)
            rows = pl.ds(pl.multiple_of(i * tq + q0, half), tq - q0)
            dq_acc[hh, rows, :] += lax.dot_general(dsb, kah, TN, preferred_element_type=F32)

        @pl.when(i > j)
        def _():
            for hh in range(2):
                update(hh, slice(0, tq), 0, False)

        @pl.when(i == j)
        def _():
            for hh in range(2):
                for k0 in range(0, tq, half):
                    update(hh, slice(k0, k0 + half), k0, True)

        @pl.when(i == nq - 1)
        def _():
            dk_ref[...] = (jnp.where(first, dk_acc[0], _swap_halves(dk_acc[1])) * LN2).astype(BF16)
            dv_ref[...] = jnp.where(first, dv_acc[0], _swap_halves(dv_acc[1])).astype(BF16)
            dn_ref[...] = jnp.where(first, _swap_halves(dk_acc[0]), dk_acc[1])

        @pl.when(n == nsteps - 1)
        def _():
            first_s = lax.broadcasted_iota(jnp.int32, (S, LANES), 1) < FOX_HEAD_DIM
            dq_ref[...] = (jnp.where(first_s, dq_acc[0], _swap_halves(dq_acc[1])) * FOX_SCALE).astype(BF16)
            drow_ref[...] = jnp.where(first_s, _swap_halves(dq_acc[0]), dq_acc[1])

    qblock = pl.BlockSpec((tq, 2 * LANES), lambda g, n, it, jt: (it[n], g))
    kblock = pl.BlockSpec((tq, 2 * LANES), lambda g, n, it, jt: (jt[n], g))
    whole = pl.BlockSpec((S, LANES), lambda g, n, it, jt: (0, g))
    kout = pl.BlockSpec((tq, LANES), lambda g, n, it, jt: (jt[n], g))
    return pl.pallas_call(
        body, name=name,
        grid_spec=pltpu.PrefetchScalarGridSpec(
            num_scalar_prefetch=2, grid=(FOX_PAIRS, nsteps),
            in_specs=[qblock, kblock, kblock, qblock, pl.BlockSpec(memory_space=pl.ANY)],
            out_specs=[whole, kout, kout, kout, whole],
            scratch_shapes=[pltpu.VMEM((2, S, LANES), F32), pltpu.VMEM((2, tq, LANES), F32),
                            pltpu.VMEM((2, tq, LANES), F32)]),
        out_shape=[jax.ShapeDtypeStruct(dh.shape, dh.dtype), jax.ShapeDtypeStruct((S, D_MAIN), BF16),
                   jax.ShapeDtypeStruct((S, D_MAIN), BF16),
                   jax.ShapeDtypeStruct((S, D_MAIN), F32), jax.ShapeDtypeStruct((S, D_MAIN), F32)],
        input_output_aliases={6: 0},
        compiler_params=_cparams("parallel", "arbitrary"),
    )(it, jt, qb, ka, va, dya, dh)


TB_ROWS = 256
TB_SEQ = 512
TQ_FOX_FWD = 1024
TQ_FOX_BWD = 1024


def _local_step(x, mem, target, win0, pscale, ln_g, ln_b, bias, comm):
    ones = jnp.ones((1, D_MAIN), F32)
    g0, b0, g1, b1 = ln_g[0:1], ln_b[0:1], ln_g[1:2], ln_b[1:2]
    mm = lambda a, b, mode, dt, tm, tn, tk, name, **kw: _mm(a, b, mode=mode, out_dtype=dt, tm=tm, tn=tn, tk=tk,
                                                            name=name, **kw)

    h0, xt = _in_proj_t(x, win0, tm=256, name="l0_in")
    wmkv, pw, wout0 = comm.next_weights(h0)
    pm, mixed = _pool_fwd(h0, pw, tb=TB_SEQ, name="l0_pool_fwd")
    mkv0 = mm(mem, wmkv[0], "nn", F32, 256, 1024, 1024, "l0_mkv")
    ymem0 = _memattn_fwd(h0, mkv0, tb=TB_SEQ, name="l0_mem_fwd")
    yg0 = _gate_fwd(mixed, pscale, h0, ymem0, tb=TB_ROWS, name="l0_gate_fwd")
    x1, xhat0, rstd0, x1t = _out_ln(yg0, wout0, x, g0, b0, tb=TB_SEQ, name="l0_out_ln")
    win1, wout1, wkv, wf = comm.late_weights(x1)

    fl = mm(x1, wf, "nn", F32, 512, LANES, D_MODEL, "f_proj")
    negcum = _forget_fwd(fl, bias, tb=TB_SEQ, name="forget_fwd")
    ka, va = _kv_proj_fox(x1, wkv, negcum, tm=512, name="kv_proj")

    h1, qa = _in_proj_fox(x1, win1, tm=256, name="l1_in")
    y1, qb = _fox_fwd(qa, ka, va, tq=TQ_FOX_FWD, name="fox_fwd")
    mkv1 = mm(mem, wmkv[1], "nn", F32, 256, 1024, 1024, "l1_mkv")
    ymem1 = _memattn_fwd(h1, mkv1, tb=TB_SEQ, name="l1_mem_fwd")
    yg1 = _gate_fwd(y1, ones, h1, ymem1, tb=TB_ROWS, name="l1_gate_fwd")
    dz1, dg1, db1, sq = _out_ln_loss(yg1, wout1, x1, g1, b1, target, tb=TB_SEQ, name="l1_out_ln_loss")

    dwout1 = mm(yg1, dz1, "tn", BF16, D_MIX, D_MODEL, 512, "l1_dwout")
    dyg1 = mm(dz1, wout1, "nt", BF16, 512, D_MIX, D_MODEL, "l1_dyg")
    dya, dymem1, dh1 = _gate_bwd(dyg1, y1, ones, h1, ymem1, tb=TB_ROWS, name="l1_gate_bwd", fox=True)
    dh1, dk, dv, dnp, drowp = _fox_bwd(qb, ka, va, dya, dh1, tq=TQ_FOX_BWD, name="fox_bwd")
    dfl, dbias = _forget_bwd(dnp, drowp, fl, bias, tb=TB_SEQ, name="forget_bwd")
    dh1, dmkv1 = _memattn_bwd(h1, mkv1, dymem1, dh1, tb=TB_SEQ, name="l1_mem_bwd")
    dwmkv1 = mm(mem, dmkv1, "tn", BF16, D_MODEL, 1024, N_MEM, "l1_dwmkv")
    dwin1 = mm(x1t, dh1, "nn", BF16, D_MODEL, D_IN // 2, 512, "l1_dwin")
    dwkv = _dwkv(x1t, dk, dv, dfl, tk=512, name="dwkv")
    dwkv = dwkv[:, :2 * D_MAIN + FOX_HEADS].reshape(D_MODEL, N_DEV, -1).transpose(1, 0, 2).astype(BF16)
    anchor = comm.send_layer1(dict(w_out=dwout1, w_mem_kv=dwmkv1, w_in=dwin1, w_kv_shared=dwkv))
    dx1 = _dx1(dh1, win1, dk, dv, dfl, wkv, dz1, tm=256, name="l1_dx")

    dz0, dg0, db0 = _ln_bwd(dx1, xhat0, rstd0, g0 + anchor, tb=TB_ROWS, name="l0_ln_bwd")
    dwout0 = mm(yg0, dz0, "tn", BF16, D_MIX, D_MODEL, 512, "l0_dwout")
    dyg0 = mm(dz0, wout0, "nt", BF16, 512, D_MIX, D_MODEL, "l0_dyg")
    dy0, dymem0, dh0 = _gate_bwd(dyg0, mixed, pscale, h0, ymem0, tb=TB_ROWS, name="l0_gate_bwd")
    comm.landed_layer1(dy0)
    dh0, dpw, dpscale = _pool_bwd(dy0, pm, mixed, pw, pscale, dh0, tb=TB_SEQ, name="l0_pool_bwd")
    dh0, dmkv0 = _memattn_bwd(h0, mkv0, dymem0, dh0, tb=TB_SEQ, name="l0_mem_bwd")
    dwmkv0 = mm(mem, dmkv0, "tn", BF16, D_MODEL, 1024, N_MEM, "l0_dwmkv")
    dh0 = comm.send_layer0_first(dict(w_out=dwout0, w_mem_kv=dwmkv0, pool_w=dpw), dh0)
    dwin0 = mm(xt, dh0, "nn", BF16, D_MODEL, D_IN // 2, 512, "l0_dwin")
    dz0 = comm.send_layer0_rest(dict(w_in=dwin0, pool_scale=dpscale, ln_g=jnp.concatenate([dg0, dg1]),
                                     ln_b=jnp.concatenate([db0, db1]), b_forget=dbias[0, :FOX_HEADS]), dz0)
    gx = mm(dh0, win0, "nt", F32, 256, D_MODEL, D_IN, "l0_dx", add=dz0, add_scale=ALPHA)
    return sq, gx


MESH_ID = pl.DeviceIdType.MESH
HBM = pl.BlockSpec(memory_space=pl.ANY)
SLICED = {"w_in": (2, D_IN // N_DEV), "w_mem_kv": (1, D_MODEL // N_DEV), "w_out": (1, D_MIX // N_DEV),
          "pool_w": (1, POOL_GROUP // N_DEV)}


def _place():
    return lax.axis_index("x"), lax.axis_index("y"), lax.axis_index("c")


def _slot(p):
    return 4 * p[0] + 2 * p[1] + p[2]


def _cut(ref, axis, width, s):
    idx = [slice(None)] * len(ref.shape)
    idx[axis] = pl.ds(s * width, width)
    return ref.at[tuple(idx)]


def _all_gather(shards, cuts, *, name):
    nt = len(shards)

    def full_shape(a, cut):
        if cut is None:
            return (N_DEV,) + a.shape
        return a.shape[:cut[0]] + (a.shape[cut[0]] * N_DEV,) + a.shape[cut[0] + 1:]

    def body(*refs):
        ins, outs = refs[:nt], refs[nt:2 * nt]
        send_sems, recv_sems, local_sems = refs[2 * nt:]
        x, y, c = _place()
        me, sibling = (x, y, c), (x, y, 1 - c)
        chips = [(1 - x, y), (x, 1 - y), (1 - x, 1 - y)]

        def place(t, s):
            return outs[t].at[s] if cuts[t] is None else _cut(outs[t], cuts[t][0], cuts[t][1], s)

        def copies(k, block, to, from_input=False):
            s = _slot(block)
            return [pltpu.make_async_remote_copy(
                src_ref=ins[t] if from_input else place(t, s), dst_ref=place(t, s),
                send_sem=send_sems.at[nt * k + t], recv_sem=recv_sems.at[nt * k + t],
                device_id=to, device_id_type=MESH_ID) for t in range(nt)]

        mine = [pltpu.make_async_copy(ins[t], place(t, _slot(me)), local_sems.at[t]) for t in range(nt)]
        for cp in mine:
            cp.start()
        first = [copies(0, me, sibling, True)] + [copies(1 + j, me, (*chip, c), True) for j, chip in enumerate(chips)]
        for group in first:
            for cp in group:
                cp.start()
        passed = [copies(4 + j, (*chip, c), sibling) for j, chip in enumerate(chips)]
        for j, chip in enumerate(chips):
            for cp in copies(1 + j, (*chip, c), me):
                cp.wait_recv()
            for cp in passed[j]:
                cp.start()
        for cp in copies(0, sibling, me):
            cp.wait_recv()
        for j, chip in enumerate(chips):
            for cp in copies(4 + j, (*chip, 1 - c), me):
                cp.wait_recv()
        for group in first + passed:
            for cp in group:
                cp.wait_send()
        for cp in mine:
            cp.wait()

    return pl.pallas_call(
        body, name=name, in_specs=[HBM] * nt, out_specs=[HBM] * nt,
        out_shape=[jax.ShapeDtypeStruct(full_shape(a, cut), a.dtype) for a, cut in zip(shards, cuts)],
        scratch_shapes=[pltpu.SemaphoreType.DMA((7 * nt,)), pltpu.SemaphoreType.DMA((7 * nt,)),
                        pltpu.SemaphoreType.DMA((nt,))],
    )(*shards)


def _exchange_copies(items, ins, outs, send_sems, recv_sems, local_sems, gather=False):
    nt = len(items)
    x, y, c = _place()
    me = _slot((x, y, c))
    flip = lambda v, bit: 1 - v if bit else v

    def part(ref, cut, s):
        return ref.at[s] if cut is None else _cut(ref, cut[0], cut[1], s)

    def src(t, s):
        return ins[t] if gather else part(ins[t], items[t][0], s)

    def dst(t, s):
        if gather:
            return part(outs[items[t][1]], items[t][0], s)
        d = outs[items[t][1]].at[s]
        return d if items[t][2] is None else d.at[items[t][2]]

    sends, arrivals = [], []
    for k in range(1, N_DEV):
        peer = (flip(x, k & 4), flip(y, k & 2), flip(c, k & 1))
        ps = _slot(peer)
        for t in range(nt):
            sems = dict(send_sem=send_sems.at[nt * (k - 1) + t], recv_sem=recv_sems.at[nt * (k - 1) + t],
                        device_id=peer, device_id_type=MESH_ID)
            sends.append(pltpu.make_async_remote_copy(src_ref=src(t, ps), dst_ref=dst(t, me), **sems))
            arrivals.append(pltpu.make_async_remote_copy(src_ref=src(t, ps), dst_ref=dst(t, ps), **sems))
    mine = [pltpu.make_async_copy(src(t, me), dst(t, me), local_sems.at[t]) for t in range(nt)]
    return sends, arrivals, mine


SEMS = pl.BlockSpec(memory_space=pltpu.SEMAPHORE)
SIDE_EFFECT = pltpu.SideEffectType.DATAFLOW_SIDE_EFFECTING


def _exchange_start(srcs, items, landings, *, name, gather=False, carry=()):
    nt, nl, nc = len(srcs), len(landings), len(carry)

    def body(*refs):
        ins, lands = refs[:nt], refs[nt:nt + nl]
        send_sems, recv_sems, local_sems = refs[nt + nl + nc:nt + nl + nc + 3]
        token = refs[-1]
        sends, _, mine = _exchange_copies(items, ins, lands, send_sems, recv_sems, local_sems, gather)
        for cp in sends + mine:
            cp.start()
        token[...] = jnp.zeros_like(token)

    hbm = lambda a: pltpu.HBM(a.shape, a.dtype)
    fresh = [pltpu.with_memory_space_constraint(
        lax.empty(l.shape, l.dtype) if isinstance(l, jax.ShapeDtypeStruct) else l, pltpu.HBM) for l in landings]
    res = pl.pallas_call(
        body, name=name, in_specs=[HBM] * (nt + nl + nc),
        out_specs=[SEMS, SEMS, SEMS] + [HBM] * (nt + nl + nc) + [pl.BlockSpec(memory_space=pltpu.VMEM)],
        out_shape=[pltpu.SemaphoreType.DMA((7 * nt,)), pltpu.SemaphoreType.DMA((7 * nt,)), pltpu.SemaphoreType.DMA((nt,))]
        + [hbm(a) for a in srcs] + [hbm(l) for l in landings] + [hbm(a) for a in carry]
        + [jax.ShapeDtypeStruct((8, LANES), F32)],
        input_output_aliases={i: 3 + i for i in range(nt + nl + nc)},
        compiler_params=pltpu.CompilerParams(has_side_effects=SIDE_EFFECT),
    )(*[pltpu.with_memory_space_constraint(a, pltpu.HBM) for a in srcs], *fresh,
      *[pltpu.with_memory_space_constraint(a, pltpu.HBM) for a in carry])
    return res[:3 + nt + nl], res[-1][0:1, 0:1], res[3 + nt + nl:-1]


def _exchange_wait(state, items, nt, after, *, name, gather=False):
    sems, bufs = state[:3], state[3:]
    nl = len(bufs) - nt

    def body(*refs):
        ins, lands = refs[:nt], refs[nt:nt + nl]
        send_sems, recv_sems, local_sems = refs[nt + nl:nt + nl + 3]
        sends, arrivals, mine = _exchange_copies(items, ins, lands, send_sems, recv_sems, local_sems, gather)
        for sent, landed in zip(sends, arrivals):
            landed.wait_recv()
            sent.wait_send()
        for cp in mine:
            cp.wait()

    hbm = lambda a: pltpu.HBM(a.shape, a.dtype)
    res = pl.pallas_call(
        body, name=name, in_specs=[HBM] * (nt + nl) + [SEMS, SEMS, SEMS, HBM], out_specs=[HBM] * (nt + nl),
        out_shape=[hbm(a) for a in bufs],
        input_output_aliases={i: i for i in range(nt + nl)},
        compiler_params=pltpu.CompilerParams(has_side_effects=SIDE_EFFECT),
    )(*bufs, *sems, after)
    return res[nt:]


def _adamw(recv, w, m, v, *, split, name):
    shape = w.shape
    axis, parts = split
    block = shape[:axis] + (shape[axis] // parts,) + shape[axis + 1:]
    nd = len(shape)

    def body(r_ref, w_ref, m_ref, v_ref, g_ref, d_ref, nm_ref, nv_ref):
        g = r_ref[0].astype(F32)
        for j in range(1, N_DEV):
            g = g + r_ref[j].astype(F32)
        nm = ADAM_B1 * m_ref[...] + (1.0 - ADAM_B1) * g
        nv = ADAM_B2 * v_ref[...] + (1.0 - ADAM_B2) * (g * g)
        m_hat = nm / (1.0 - ADAM_B1 ** ADAM_STEP)
        v_hat = nv / (1.0 - ADAM_B2 ** ADAM_STEP)
        g_ref[...] = g
        nm_ref[...] = nm
        nv_ref[...] = nv
        d_ref[...] = -ADAM_LR * (m_hat / (jnp.sqrt(v_hat) + ADAM_EPS) + ADAM_WD * w_ref[...])

    at = lambda i: tuple(i if a == axis else 0 for a in range(nd))
    one = pl.BlockSpec(block, at)
    shp = jax.ShapeDtypeStruct(shape, F32)
    return pl.pallas_call(
        body, name=name, grid=(parts,),
        in_specs=[pl.BlockSpec((N_DEV,) + block, lambda i: (0,) + at(i)), one, one, one],
        out_specs=[one, one, one, one], out_shape=[shp, shp, shp, shp],
        compiler_params=_cparams("parallel"),
    )(recv, w, m, v)


BIG = ("w_in", "w_mem_kv", "w_out", "pool_w", "w_kv_shared")
SMALL = ("pool_scale", "ln_g", "ln_b", "b_forget")
SMALL_ROWS = 40
ADAM_SPLIT = {"w_in": (1, 4), "w_mem_kv": (0, 2), "w_out": (0, 2), "pool_w": (0, 1), "w_kv_shared": (0, 4)}
PER_LAYER_CUT = {"w_out": (0, D_MIX // N_DEV), "w_mem_kv": (0, D_MODEL // N_DEV), "w_in": (1, D_IN // N_DEV),
                 "pool_w": (1, POOL_GROUP // N_DEV)}
EARLY = ("w_out", "w_mem_kv", "w_in", "w_kv_shared")
EARLY_ITEMS = [(PER_LAYER_CUT[n], i, 1) for i, n in enumerate(EARLY[:3])] + [(None, 3, None)]


def _flat(parts, rows):
    v = jnp.concatenate([p.reshape(-1) for p in parts])
    return jnp.pad(v, (0, rows * LANES - v.shape[0])).reshape(rows, LANES)


def _unflat(flat, shapes):
    v, out, off = flat.reshape(-1), [], 0
    for s in shapes:
        n = math.prod(s)
        out.append(v[off:off + n].reshape(s))
        off += n
    return out


def kernel(x, mem, w_in, w_mem_kv, w_out, ln_g, ln_b, pool_w, pool_scale, w_kv_shared, b_forget, loss_target, m_w_in, m_w_mem_kv, m_w_out, m_ln_g, m_ln_b, m_pool_w, m_pool_scale, m_w_kv_shared, m_b_forget, v_w_in, v_w_mem_kv, v_w_out, v_ln_g, v_ln_b, v_pool_w, v_pool_scale, v_w_kv_shared, v_b_forget):
    w = dict(w_in=w_in, w_mem_kv=w_mem_kv, w_out=w_out, ln_g=ln_g, ln_b=ln_b, pool_w=pool_w[0],
             pool_scale=pool_scale, w_kv_shared=w_kv_shared, b_forget=b_forget)
    m = dict(w_in=m_w_in, w_mem_kv=m_w_mem_kv, w_out=m_w_out, ln_g=m_ln_g, ln_b=m_ln_b, pool_w=m_pool_w[0],
             pool_scale=m_pool_scale, w_kv_shared=m_w_kv_shared, b_forget=m_b_forget)
    v = dict(w_in=v_w_in, w_mem_kv=v_w_mem_kv, w_out=v_w_out, ln_g=v_ln_g, ln_b=v_ln_b, pool_w=v_pool_w[0],
             pool_scale=v_pool_scale, w_kv_shared=v_w_kv_shared, b_forget=v_b_forget)

    wb = {n: w[n].astype(BF16) for n in BIG}
    bdt = wb["w_in"].dtype
    win0, pscale = _all_gather([wb["w_in"][0], jnp.pad(pool_scale, ((0, 7), (0, 0)))], [PER_LAYER_CUT["w_in"], None],
                               name="gather_weights")
    pscale = pscale[:, 0, :].reshape(1, D_MAIN)
    next_srcs = [wb["w_mem_kv"], wb["pool_w"], wb["w_out"][0]]
    next_items = [(SLICED["w_mem_kv"], 0, None), (SLICED["pool_w"], 1, None), (PER_LAYER_CUT["w_out"], 2, None)]
    next_state, _, (win0,) = _exchange_start(
        next_srcs, next_items,
        [jax.ShapeDtypeStruct((2, D_MODEL, D_MODEL), bdt), jax.ShapeDtypeStruct((4, POOL_GROUP, POOL_GROUP), bdt),
         jax.ShapeDtypeStruct((D_MIX, D_MODEL), bdt)], name="gather_next_start", gather=True, carry=[win0])
    late_srcs = [wb["w_in"][1], wb["w_out"][1], wb["w_kv_shared"]]
    late_items = [(PER_LAYER_CUT["w_in"], 0, None), (PER_LAYER_CUT["w_out"], 1, None), (None, 2, None)]
    late = {}
    bias = jnp.pad(b_forget, (0, LANES - FOX_HEADS)).reshape(1, LANES)

    def next_weights(h0):
        wmkv, pw, wout0 = _exchange_wait(next_state, next_items, len(next_srcs), h0, name="gather_next_wait", gather=True)
        late["state"], _, (pw,) = _exchange_start(
            late_srcs, late_items,
            [jax.ShapeDtypeStruct((D_MODEL, D_IN), bdt), jax.ShapeDtypeStruct((D_MIX, D_MODEL), bdt),
             jax.ShapeDtypeStruct((N_DEV,) + w_kv_shared.shape, bdt)], name="gather_late_start", gather=True, carry=[pw])
        return wmkv, pw, wout0

    def late_weights(x1):
        win1, wout1, wkv = _exchange_wait(late["state"], late_items, len(late_srcs), x1, name="gather_late_wait",
                                          gather=True)
        wkv = jnp.pad(wkv.transpose(1, 0, 2).reshape(D_MODEL, -1), ((0, 0), (0, LANES - FOX_HEADS)))
        return win1, wout1, wkv, wkv[:, 2 * D_MAIN:]

    sent, recv = {}, {}
    first_items = [(PER_LAYER_CUT["w_out"], 0, 0), (PER_LAYER_CUT["w_mem_kv"], 1, 0), (PER_LAYER_CUT["pool_w"], 2, None)]
    rest_items = [(PER_LAYER_CUT["w_in"], 0, 0), (None, 1, None)]

    def send_layer1(g):
        srcs = [g[n] for n in EARLY]
        lands = [jax.ShapeDtypeStruct((N_DEV, 2) + w[n].shape[1:], g[n].dtype) for n in EARLY[:3]]
        lands.append(jax.ShapeDtypeStruct(g["w_kv_shared"].shape, g["w_kv_shared"].dtype))
        sent["layer1"], anchor, _ = _exchange_start(srcs, EARLY_ITEMS, lands, name="exchange_early_start")
        return anchor

    def landed_layer1(after):
        recv["w_out"], recv["w_mem_kv"], recv["w_in"], recv["w_kv_shared"] = _exchange_wait(
            sent["layer1"], EARLY_ITEMS, len(EARLY), after, name="exchange_early_wait")

    def send_layer0_first(g, carry):
        srcs = [g["w_out"], g["w_mem_kv"], g["pool_w"]]
        sent["first"], _, (carry,) = _exchange_start(
            srcs, first_items,
            [recv["w_out"], recv["w_mem_kv"], jax.ShapeDtypeStruct((N_DEV,) + w["pool_w"].shape, srcs[2].dtype)],
            name="exchange_mid_start", carry=[carry])
        return carry

    def send_layer0_rest(g, carry):
        small = jnp.concatenate([g["pool_scale"].reshape(N_DEV, -1)]
                                + [jnp.broadcast_to(g[n].reshape(1, -1), (N_DEV, g[n].size)) for n in SMALL[1:]], axis=1)
        small = jnp.pad(small, ((0, 0), (0, SMALL_ROWS * LANES - small.shape[1]))).reshape(N_DEV, SMALL_ROWS, LANES)
        sent["rest"], _, (carry,) = _exchange_start(
            [g["w_in"], small], rest_items, [recv["w_in"], jax.ShapeDtypeStruct(small.shape, small.dtype)],
            name="exchange_last_start", carry=[carry])
        return carry

    comm = types.SimpleNamespace(next_weights=next_weights, late_weights=late_weights, send_layer1=send_layer1,
                                 landed_layer1=landed_layer1, send_layer0_first=send_layer0_first,
                                 send_layer0_rest=send_layer0_rest)
    sq, gx = _local_step(x[0], mem[0], loss_target[0], win0, pscale, ln_g, ln_b, bias, comm)
    loss = lax.psum((0.5 / D_MODEL) * jnp.sum(sq), ("x", "y", "c"))

    recv["w_out"], recv["w_mem_kv"], recv["pool_w"] = _exchange_wait(sent["first"], first_items, len(first_items), gx,
                                                                     name="exchange_mid_wait")
    recv["w_in"], recv["small"] = _exchange_wait(sent["rest"], rest_items, len(rest_items), gx,
                                                 name="exchange_last_wait")

    outs = {}
    for n in BIG:
        res = _adamw(recv[n], w[n], m[n], v[n], split=ADAM_SPLIT[n], name="adamw_" + n)
        for kind, a in zip(("grad", "delta", "new_m", "new_v"), res):
            outs[kind, n] = a[None] if n == "pool_w" else a
    small_shapes = [w[n].shape for n in SMALL]
    res = _adamw(recv["small"], _flat([w[n] for n in SMALL], SMALL_ROWS), _flat([m[n] for n in SMALL], SMALL_ROWS),
                 _flat([v[n] for n in SMALL], SMALL_ROWS), split=(0, 1), name="adamw_small")
    for kind, flat in zip(("grad", "delta", "new_m", "new_v"), res):
        for n, a in zip(SMALL, _unflat(flat, small_shapes)):
            outs[kind, n] = a
    order = ("w_in", "w_mem_kv", "w_out", "ln_g", "ln_b", "pool_w", "pool_scale", "w_kv_shared", "b_forget")
    return (loss, gx[None], *[outs[kind, n] for kind in ("grad", "delta", "new_m", "new_v") for n in order])
```

```python
import math
import types

import numpy as np
import jax
import jax.numpy as jnp
from jax import lax
from jax.experimental import pallas as pl
from jax.experimental.pallas import tpu as pltpu

F32 = jnp.float32
BF16 = jnp.bfloat16

D_MODEL = 1024
D_MAIN = 1024
D_MEM = 512
D_MIX = D_MAIN + D_MEM
D_IN = 2 * D_MIX
N_MEM = 256
MEM_HEADS = 4
MEM_HEAD_DIM = 128
FOX_HEADS = 16
FOX_HEAD_DIM = 64
FOX_PAIRS = FOX_HEADS // 2
POOL_WINDOWS = (2, 4, 8, 16)
POOL_GROUP = 256
POOL_HALO = 16
ALPHA = 4.0 ** 0.25
LN_EPS = 1e-5
NEG = -1e30
LANES = 128
N_DEV = 8

ADAM_LR = 0.001
ADAM_B1 = 0.9
ADAM_B2 = 0.999
ADAM_EPS = 1e-08
ADAM_WD = 0.01
ADAM_STEP = 10

VMEM_LIMIT = 56 * 1024 * 1024

NN = (((1,), (0,)), ((), ()))
NT = (((1,), (1,)), ((), ()))
TN = (((0,), (0,)), ((), ()))


def _cparams(*sem):
    return pltpu.CompilerParams(dimension_semantics=sem, vmem_limit_bytes=VMEM_LIMIT)


def _sigmoid(z):
    return 1.0 / (1.0 + jnp.exp(-z))


def _mm(a, b, *, mode, out_dtype, tm, tn, tk, name, add=None, add_scale=1.0):
    if mode == "nn":
        (M, K), (K2, N) = a.shape, b.shape
    elif mode == "nt":
        (M, K), (N, K2) = a.shape, b.shape
    else:
        (K, M), (K2, N) = a.shape, b.shape
    assert K == K2, (a.shape, b.shape, mode)
    tm, tn, tk = min(tm, M), min(tn, N), min(tk, K)
    assert M % tm == 0 and N % tn == 0 and K % tk == 0, (M, N, K, tm, tn, tk)
    gm, gn, gk = M // tm, N // tn, K // tk
    dims = {"nn": NN, "nt": NT, "tn": TN}[mode]
    if mode == "tn":
        a_spec = pl.BlockSpec((tk, tm), lambda i, j, k: (k, i))
    else:
        a_spec = pl.BlockSpec((tm, tk), lambda i, j, k: (i, k))
    if mode == "nt":
        b_spec = pl.BlockSpec((tn, tk), lambda i, j, k: (j, k))
    else:
        b_spec = pl.BlockSpec((tk, tn), lambda i, j, k: (k, j))
    o_spec = pl.BlockSpec((tm, tn), lambda i, j, k: (i, j))
    has_add = add is not None
    acc_in_out = out_dtype == F32

    def body(*refs):
        a_ref, b_ref = refs[0], refs[1]
        add_ref = refs[2] if has_add else None
        o_ref = refs[3] if has_add else refs[2]
        prod = lax.dot_general(a_ref[...].astype(BF16), b_ref[...].astype(BF16), dims,
                               preferred_element_type=F32)

        def finish(r):
            if has_add:
                r = r + add_scale * add_ref[...]
            o_ref[...] = r.astype(out_dtype)

        if gk == 1:
            finish(prod)
        else:
            acc_ref = o_ref if acc_in_out else refs[-1]
            k = pl.program_id(2)

            @pl.when(k == 0)
            def _():
                acc_ref[...] = prod

            @pl.when(k > 0)
            def _():
                acc_ref[...] += prod

            if has_add or not acc_in_out:
                @pl.when(k == gk - 1)
                def _():
                    finish(acc_ref[...])

    in_specs = [a_spec, b_spec] + ([o_spec] if has_add else [])
    args = (a, b) + ((add,) if has_add else ())
    return pl.pallas_call(
        body, name=name, grid=(gm, gn, gk), in_specs=in_specs, out_specs=o_spec,
        out_shape=jax.ShapeDtypeStruct((M, N), out_dtype),
        scratch_shapes=[pltpu.VMEM((tm, tn), F32)] if gk > 1 and not acc_in_out else [],
        compiler_params=_cparams("parallel", "parallel", "arbitrary"),
    )(*args)


def _ln_stats(z):
    mu = jnp.mean(z, axis=1, keepdims=True)
    zc = z - mu
    var = jnp.mean(zc * zc, axis=1, keepdims=True)
    rstd = lax.rsqrt(var + LN_EPS)
    return zc * rstd, rstd


def _ln_bwd_math(dy, xhat, rstd, g):
    dxh = dy * g
    m1 = jnp.mean(dxh, axis=1, keepdims=True)
    m2 = jnp.mean(dxh * xhat, axis=1, keepdims=True)
    return rstd * (dxh - m1 - xhat * m2)


def _out_ln(yg, wout, x, g, b, *, tb, name):
    S = x.shape[0]
    tb = min(tb, S)

    def body(yg_ref, w_ref, x_ref, g_ref, b_ref, y_ref, xhat_ref, rstd_ref, yt_ref):
        o = jnp.dot(yg_ref[...], w_ref[...], preferred_element_type=F32)
        xhat, rstd = _ln_stats(ALPHA * x_ref[...] + o)
        xhat_ref[...] = xhat
        rstd_ref[...] = rstd
        y = xhat * g_ref[...] + b_ref[...]
        y_ref[...] = y
        yt_ref[...] = y.T.astype(BF16)

    row = pl.BlockSpec((tb, D_MODEL), lambda i: (i, 0))
    vec = pl.BlockSpec((1, D_MODEL), lambda i: (0, 0))
    return pl.pallas_call(
        body, name=name, grid=(S // tb,),
        in_specs=[pl.BlockSpec((tb, D_MIX), lambda i: (i, 0)), pl.BlockSpec((D_MIX, D_MODEL), lambda i: (0, 0)),
                  row, vec, vec],
        out_specs=[row, row, pl.BlockSpec((tb, 1), lambda i: (i, 0)), pl.BlockSpec((D_MODEL, tb), lambda i: (0, i))],
        out_shape=[jax.ShapeDtypeStruct((S, D_MODEL), F32), jax.ShapeDtypeStruct((S, D_MODEL), F32),
                   jax.ShapeDtypeStruct((S, 1), F32), jax.ShapeDtypeStruct((D_MODEL, S), BF16)],
        compiler_params=_cparams("parallel"),
    )(yg, wout, x, g, b)


def _in_proj_t(x, w, *, tm, name):
    S = x.shape[0]
    tm = min(tm, S)

    def body(x_ref, w_ref, h_ref, xt_ref):
        xv = x_ref[...]
        h_ref[...] = jnp.dot(xv.astype(BF16), w_ref[...], preferred_element_type=F32).astype(BF16)
        xt_ref[...] = xv.T.astype(BF16)

    return pl.pallas_call(
        body, name=name, grid=(S // tm,),
        in_specs=[pl.BlockSpec((tm, D_MODEL), lambda i: (i, 0)), pl.BlockSpec((D_MODEL, D_IN), lambda i: (0, 0))],
        out_specs=[pl.BlockSpec((tm, D_IN), lambda i: (i, 0)), pl.BlockSpec((D_MODEL, tm), lambda i: (0, i))],
        out_shape=[jax.ShapeDtypeStruct((S, D_IN), BF16), jax.ShapeDtypeStruct((D_MODEL, S), BF16)],
        compiler_params=_cparams("parallel"),
    )(x, w)


def _out_ln_loss(yg, wout, x, g, b, target, *, tb, name):
    S = x.shape[0]
    tb = min(tb, S)

    def body(yg_ref, w_ref, x_ref, g_ref, b_ref, t_ref, dz_ref, dg_ref, db_ref, sq_ref):
        @pl.when(pl.program_id(0) == 0)
        def _():
            dg_ref[...] = jnp.zeros_like(dg_ref)
            db_ref[...] = jnp.zeros_like(db_ref)
            sq_ref[...] = jnp.zeros_like(sq_ref)

        o = jnp.dot(yg_ref[...], w_ref[...], preferred_element_type=F32)
        xhat, rstd = _ln_stats(ALPHA * x_ref[...] + o)
        err = xhat * g_ref[...] + b_ref[...] - t_ref[...]
        sq_ref[...] += jnp.sum(err * err, axis=0, keepdims=True)
        dy = err * (1.0 / D_MODEL)
        dz_ref[...] = _ln_bwd_math(dy, xhat, rstd, g_ref[...])
        dg_ref[...] += jnp.sum(dy * xhat, axis=0, keepdims=True)
        db_ref[...] += jnp.sum(dy, axis=0, keepdims=True)

    row = pl.BlockSpec((tb, D_MODEL), lambda i: (i, 0))
    vec = pl.BlockSpec((1, D_MODEL), lambda i: (0, 0))
    vshape = jax.ShapeDtypeStruct((1, D_MODEL), F32)
    return pl.pallas_call(
        body, name=name, grid=(S // tb,),
        in_specs=[pl.BlockSpec((tb, D_MIX), lambda i: (i, 0)), pl.BlockSpec((D_MIX, D_MODEL), lambda i: (0, 0)),
                  row, vec, vec, row],
        out_specs=[row, vec, vec, vec],
        out_shape=[jax.ShapeDtypeStruct((S, D_MODEL), F32), vshape, vshape, vshape],
        compiler_params=_cparams("arbitrary"),
    )(yg, wout, x, g, b, target)


def _ln_bwd(dy, xhat, rstd, g, *, tb, name):
    S = dy.shape[0]
    tb = min(tb, S)

    def body(dy_ref, xhat_ref, rstd_ref, g_ref, dz_ref, dg_ref, db_ref):
        @pl.when(pl.program_id(0) == 0)
        def _():
            dg_ref[...] = jnp.zeros_like(dg_ref)
            db_ref[...] = jnp.zeros_like(db_ref)

        dy_, xhat_ = dy_ref[...], xhat_ref[...]
        dz_ref[...] = _ln_bwd_math(dy_, xhat_, rstd_ref[...], g_ref[...])
        dg_ref[...] += jnp.sum(dy_ * xhat_, axis=0, keepdims=True)
        db_ref[...] += jnp.sum(dy_, axis=0, keepdims=True)

    row = pl.BlockSpec((tb, D_MODEL), lambda i: (i, 0))
    vec = pl.BlockSpec((1, D_MODEL), lambda i: (0, 0))
    return pl.pallas_call(
        body, name=name, grid=(S // tb,),
        in_specs=[row, row, pl.BlockSpec((tb, 1), lambda i: (i, 0)), vec],
        out_specs=[row, vec, vec],
        out_shape=[jax.ShapeDtypeStruct((S, D_MODEL), F32), jax.ShapeDtypeStruct((1, D_MODEL), F32),
                   jax.ShapeDtypeStruct((1, D_MODEL), F32)],
        compiler_params=_cparams("arbitrary"),
    )(dy, xhat, rstd, g)


def _gate_fwd(ysrc, scale, h, ymem, *, tb, name):
    S = ysrc.shape[0]
    tb = min(tb, S)

    def body(ys_ref, sc_ref, ga_ref, gb_ref, gc_ref, ym_ref, yg_ref):
        ymain = ys_ref[...] * sc_ref[...]
        for k, g_ref in enumerate((ga_ref, gb_ref)):
            gv = g_ref[...].astype(F32)
            yg_ref[:, 512 * k:512 * (k + 1)] = (ymain[:, 512 * k:512 * (k + 1)] * gv * _sigmoid(gv)).astype(BF16)
        gv = gc_ref[...].astype(F32)
        yg_ref[:, 1024:1536] = (ym_ref[...] * gv * _sigmoid(gv)).astype(BF16)

    slab = lambda c: pl.BlockSpec((tb, 512), lambda i, c=c: (i, c))
    return pl.pallas_call(
        body, name=name, grid=(S // tb,),
        in_specs=[pl.BlockSpec((tb, D_MAIN), lambda i: (i, 0)), pl.BlockSpec((1, D_MAIN), lambda i: (0, 0)),
                  slab(3), slab(4), slab(5), pl.BlockSpec((tb, D_MEM), lambda i: (i, 0))],
        out_specs=pl.BlockSpec((tb, D_MIX), lambda i: (i, 0)),
        out_shape=jax.ShapeDtypeStruct((S, D_MIX), BF16),
        compiler_params=_cparams("parallel"),
    )(ysrc, scale, h, h, h, ymem)


def _gate_bwd(dyg, ysrc, scale, h, ymem, *, tb, name, fox=False):
    S = ysrc.shape[0]
    tb = min(tb, S)

    def dsilu(gv):
        sg = _sigmoid(gv)
        return sg, sg * (1.0 + gv * (1.0 - sg))

    def body(da_ref, db_ref, dc_ref, ys_ref, sc_ref, ga_ref, gb_ref, gc_ref, ym_ref, dym_ref, dymem_ref, dh_ref):
        ymain = ys_ref[...] * sc_ref[...]
        lane = lax.broadcasted_iota(jnp.int32, (tb, LANES), 1)
        first = lane < FOX_HEAD_DIM
        for k, (d_ref, g_ref) in enumerate(((da_ref, ga_ref), (db_ref, gb_ref))):
            gv, d = g_ref[...].astype(F32), d_ref[...].astype(F32)
            sg, ds = dsilu(gv)
            dy = d * gv * sg
            dh_ref[:, 512 * k:512 * (k + 1)] = (d * ymain[:, 512 * k:512 * (k + 1)] * ds).astype(BF16)
            if not fox:
                dym_ref[:, 512 * k:512 * (k + 1)] = dy
                continue
            for q in range(512 // LANES):
                cols = slice(LANES * q, LANES * (q + 1))
                dy2 = dy[:, cols]
                prod = dy2 * ymain[:, 512 * k + LANES * q:512 * k + LANES * (q + 1)]
                for hh in range(2):
                    delta = jnp.sum(jnp.where(first == (hh == 0), prod, 0.0), axis=1, keepdims=True)
                    dyh = dy2 if hh == 0 else _swap_halves(dy2)
                    c0 = LANES * (2 * (4 * k + q) + hh)
                    dym_ref[:, c0:c0 + LANES] = jnp.where(
                        first, dyh, _lanes3(lane, AUX, _split3(-delta), 0.0)).astype(BF16)
        gv, d = gc_ref[...].astype(F32), dc_ref[...].astype(F32)
        sg, ds = dsilu(gv)
        dymem_ref[...] = d * gv * sg
        dh_ref[:, 1024:1536] = (d * ym_ref[...] * ds).astype(BF16)

    slab = lambda c: pl.BlockSpec((tb, 512), lambda i, c=c: (i, c))
    return pl.pallas_call(
        body, name=name, grid=(S // tb,),
        in_specs=[slab(0), slab(1), slab(2),
                  pl.BlockSpec((tb, D_MAIN), lambda i: (i, 0)), pl.BlockSpec((1, D_MAIN), lambda i: (0, 0)),
                  slab(3), slab(4), slab(5), pl.BlockSpec((tb, D_MEM), lambda i: (i, 0))],
        out_specs=[pl.BlockSpec((tb, 2 * D_MAIN if fox else D_MAIN), lambda i: (i, 0)),
                   pl.BlockSpec((tb, D_MEM), lambda i: (i, 0)), pl.BlockSpec((tb, D_MIX), lambda i: (i, 1))],
        out_shape=[jax.ShapeDtypeStruct((S, 2 * D_MAIN), BF16) if fox else jax.ShapeDtypeStruct((S, D_MAIN), F32),
                   jax.ShapeDtypeStruct((S, D_MEM), F32), jax.ShapeDtypeStruct((S, D_IN), BF16)],
        compiler_params=_cparams("parallel"),
    )(dyg, dyg, dyg, ysrc, scale, h, h, h, ymem)


def _window_count(t0, rows, w):
    t = t0 + lax.broadcasted_iota(jnp.int32, (rows, POOL_GROUP), 0)
    return jnp.minimum(t + 1, w).astype(F32)


def _pool_fwd(h, pw, *, tb, name):
    S = h.shape[0]
    tb = min(tb, S)

    def body(u_ref, pw_ref, pm_ref, mixed_ref, tail_ref):
        i = pl.program_id(0)

        @pl.when(i == 0)
        def _():
            tail_ref[...] = jnp.zeros_like(tail_ref)

        u = u_ref[...].astype(F32)
        xfull = jnp.concatenate([tail_ref[...], u], axis=0)
        for gi, w in enumerate(POOL_WINDOWS):
            cols = slice(POOL_GROUP * gi, POOL_GROUP * (gi + 1))
            s = xfull[:, cols]
            sh = 1
            while sh < w:
                s = s + pltpu.roll(s, sh, 0)
                sh *= 2
            pm = s[POOL_HALO:, :] / _window_count(i * tb, tb, w) - u[:, cols]
            pmb = pm.astype(BF16)
            pm_ref[:, cols] = pmb
            mixed_ref[:, cols] = jnp.dot(pmb, pw_ref[gi], preferred_element_type=F32)
        tail_ref[...] = u[tb - POOL_HALO:, :]

    return pl.pallas_call(
        body, name=name, grid=(S // tb,),
        in_specs=[pl.BlockSpec((tb, D_MAIN), lambda i: (i, 0)),
                  pl.BlockSpec((4, POOL_GROUP, POOL_GROUP), lambda i: (0, 0, 0))],
        out_specs=[pl.BlockSpec((tb, D_MAIN), lambda i: (i, 0)), pl.BlockSpec((tb, D_MAIN), lambda i: (i, 0))],
        out_shape=[jax.ShapeDtypeStruct((S, D_MAIN), BF16), jax.ShapeDtypeStruct((S, D_MAIN), F32)],
        scratch_shapes=[pltpu.VMEM((POOL_HALO, D_MAIN), F32)],
        compiler_params=_cparams("arbitrary"),
    )(h, pw)


def _pool_bwd(dymain, pm, mixed, pw, scale, dh, *, tb, name):
    S = dymain.shape[0]
    tb = min(tb, S)
    nb = S // tb
    n = tb + POOL_HALO

    def body(dy_ref, pm_ref, mixed_ref, pw_ref, sc_ref, dh_in, dh_ref, dpw_ref, dsc_ref, head_ref, dpw_acc):
        del dh_in
        i = pl.program_id(0)

        @pl.when(i == 0)
        def _():
            head_ref[...] = jnp.zeros_like(head_ref)
            dpw_acc[...] = jnp.zeros_like(dpw_acc)
            dsc_ref[...] = jnp.zeros_like(dsc_ref)

        dy = dy_ref[...]
        dsc_ref[...] += jnp.sum(dy * mixed_ref[...], axis=0, keepdims=True)
        dmixed = dy * sc_ref[...]
        t0 = (nb - 1 - i) * tb
        for gi, w in enumerate(POOL_WINDOWS):
            cols = slice(POOL_GROUP * gi, POOL_GROUP * (gi + 1))
            dm = dmixed[:, cols].astype(BF16)
            dpw_acc[gi] += lax.dot_general(pm_ref[:, cols], dm, TN, preferred_element_type=F32)
            dpm = lax.dot_general(dm, pw_ref[gi], NT, preferred_element_type=F32)
            e = dpm / _window_count(t0, tb, w)
            s = jnp.concatenate([e, head_ref[:, cols]], axis=0)
            sh = 1
            while sh < w:
                s = s + pltpu.roll(s, n - sh, 0)
                sh *= 2
            dh_ref[:, cols] = (s[:tb, :] - dpm).astype(BF16)
            head_ref[:, cols] = e[:POOL_HALO, :]

        @pl.when(i == nb - 1)
        def _():
            dpw_ref[...] = dpw_acc[...].astype(BF16)

    rev = lambda i: (nb - 1 - i, 0)
    return pl.pallas_call(
        body, name=name, grid=(nb,),
        in_specs=[pl.BlockSpec((tb, D_MAIN), rev), pl.BlockSpec((tb, D_MAIN), rev), pl.BlockSpec((tb, D_MAIN), rev),
                  pl.BlockSpec((4, POOL_GROUP, POOL_GROUP), lambda i: (0, 0, 0)),
                  pl.BlockSpec((1, D_MAIN), lambda i: (0, 0)), pl.BlockSpec(memory_space=pl.ANY)],
        out_specs=[pl.BlockSpec((tb, D_MAIN), rev),
                   pl.BlockSpec((4, POOL_GROUP, POOL_GROUP), lambda i: (0, 0, 0)),
                   pl.BlockSpec((1, D_MAIN), lambda i: (0, 0))],
        out_shape=[jax.ShapeDtypeStruct(dh.shape, dh.dtype),
                   jax.ShapeDtypeStruct((4, POOL_GROUP, POOL_GROUP), BF16), jax.ShapeDtypeStruct((1, D_MAIN), F32)],
        scratch_shapes=[pltpu.VMEM((POOL_HALO, D_MAIN), F32), pltpu.VMEM((4, POOL_GROUP, POOL_GROUP), F32)],
        input_output_aliases={5: 0},
        compiler_params=_cparams("arbitrary"),
    )(dymain, pm, mixed, pw, scale, dh)


MEM_SCALE = MEM_HEAD_DIM ** -0.5


def _mem_probs(q_ref, mkv_ref, hd):
    cols = slice(MEM_HEAD_DIM * hd, MEM_HEAD_DIM * (hd + 1))
    q = (q_ref[:, cols].astype(F32) * MEM_SCALE).astype(BF16)
    mk = mkv_ref[:, cols].astype(BF16)
    mv = mkv_ref[:, D_MEM + MEM_HEAD_DIM * hd:D_MEM + MEM_HEAD_DIM * (hd + 1)].astype(BF16)
    s = lax.dot_general(q, mk, NT, preferred_element_type=F32)
    e = jnp.exp(s - jnp.max(s, axis=1, keepdims=True))
    return cols, q, mk, mv, e, jnp.sum(e, axis=1, keepdims=True)


def _memattn_fwd(h, mkv, *, tb, name):
    S = h.shape[0]
    tb = min(tb, S)

    def body(q_ref, mkv_ref, y_ref):
        for hd in range(MEM_HEADS):
            cols, _, _, mv, e, l = _mem_probs(q_ref, mkv_ref, hd)
            y_ref[:, cols] = jnp.dot(e.astype(BF16), mv, preferred_element_type=F32) / l

    return pl.pallas_call(
        body, name=name, grid=(S // tb,),
        in_specs=[pl.BlockSpec((tb, D_MEM), lambda i: (i, 2)), pl.BlockSpec((N_MEM, 2 * D_MEM), lambda i: (0, 0))],
        out_specs=pl.BlockSpec((tb, D_MEM), lambda i: (i, 0)),
        out_shape=jax.ShapeDtypeStruct((S, D_MEM), F32),
        compiler_params=_cparams("parallel"),
    )(h, mkv)


def _memattn_bwd(h, mkv, dy, dh, *, tb, name):
    S = h.shape[0]
    tb = min(tb, S)

    def body(q_ref, mkv_ref, dy_ref, dh_in, dh_ref, dmkv_ref):
        del dh_in

        @pl.when(pl.program_id(0) == 0)
        def _():
            dmkv_ref[...] = jnp.zeros_like(dmkv_ref)

        for hd in range(MEM_HEADS):
            cols, q, mk, mv, e, l = _mem_probs(q_ref, mkv_ref, hd)
            p = e / l
            dyh = dy_ref[:, cols].astype(BF16)
            dp = lax.dot_general(dyh, mv, NT, preferred_element_type=F32)
            ds = p * (dp - jnp.sum(dp * p, axis=1, keepdims=True))
            dsb = ds.astype(BF16)
            dh_ref[:, cols] = (jnp.dot(dsb, mk, preferred_element_type=F32) * MEM_SCALE).astype(BF16)
            dmkv_ref[:, cols] += lax.dot_general(dsb, q, TN, preferred_element_type=F32)
            vcols = slice(D_MEM + MEM_HEAD_DIM * hd, D_MEM + MEM_HEAD_DIM * (hd + 1))
            dmkv_ref[:, vcols] += lax.dot_general(p.astype(BF16), dyh, TN, preferred_element_type=F32)

    return pl.pallas_call(
        body, name=name, grid=(S // tb,),
        in_specs=[pl.BlockSpec((tb, D_MEM), lambda i: (i, 2)), pl.BlockSpec((N_MEM, 2 * D_MEM), lambda i: (0, 0)),
                  pl.BlockSpec((tb, D_MEM), lambda i: (i, 0)), pl.BlockSpec(memory_space=pl.ANY)],
        out_specs=[pl.BlockSpec((tb, D_MEM), lambda i: (i, 2)), pl.BlockSpec((N_MEM, 2 * D_MEM), lambda i: (0, 0))],
        out_shape=[jax.ShapeDtypeStruct(dh.shape, dh.dtype), jax.ShapeDtypeStruct((N_MEM, 2 * D_MEM), F32)],
        input_output_aliases={3: 0},
        compiler_params=_cparams("arbitrary"),
    )(h, mkv, dy, dh)


def _forget_fwd(fl, bias, *, tb, name):
    S = fl.shape[0]
    tb = min(tb, S)

    def body(fl_ref, b_ref, o_ref, carry_ref):
        @pl.when(pl.program_id(0) == 0)
        def _():
            carry_ref[...] = jnp.zeros_like(carry_ref)

        z = fl_ref[...] + b_ref[...]
        lf = jnp.minimum(z, 0.0) - jnp.log(1.0 + jnp.exp(-jnp.abs(z)))
        row = lax.broadcasted_iota(jnp.int32, (tb, LANES), 0)
        c = lf
        sh = 1
        while sh < tb:
            c = c + jnp.where(row >= sh, pltpu.roll(c, sh, 0), 0.0)
            sh *= 2
        o_ref[...] = -(carry_ref[...] + c)
        carry_ref[...] += jnp.sum(lf, axis=0, keepdims=True)

    return pl.pallas_call(
        body, name=name, grid=(S // tb,),
        in_specs=[pl.BlockSpec((tb, LANES), lambda i: (i, 0)), pl.BlockSpec((1, LANES), lambda i: (0, 0))],
        out_specs=pl.BlockSpec((tb, LANES), lambda i: (i, 0)),
        out_shape=jax.ShapeDtypeStruct((S, LANES), F32),
        scratch_shapes=[pltpu.VMEM((1, LANES), F32)],
        compiler_params=_cparams("arbitrary"),
    )(fl, bias)


def _forget_bwd(dn, drow, fl, bias, *, tb, name):
    S = fl.shape[0]
    tb = min(tb, S)
    nb = S // tb

    def body(dn_ref, dr_ref, fl_ref, b_ref, dh_ref, db_ref, carry_ref):
        @pl.when(pl.program_id(0) == 0)
        def _():
            carry_ref[...] = jnp.zeros_like(carry_ref)
            db_ref[...] = jnp.zeros_like(db_ref)

        src = lax.broadcasted_iota(jnp.int32, (D_MAIN, LANES), 0)
        head = lax.broadcasted_iota(jnp.int32, (D_MAIN, LANES), 1)
        pick = lambda off: jnp.where((src == FOX_HEAD_DIM * head + off) & (head < FOX_HEADS), 1.0, 0.0).astype(BF16)
        hdot = lambda a, sel: sum(jnp.dot(part.astype(BF16), sel, preferred_element_type=F32) for part in _split3(a))
        dcum = hdot(dr_ref[...], pick(3)) - hdot(dn_ref[...], pick(0))
        row = lax.broadcasted_iota(jnp.int32, (tb, LANES), 0)
        c = dcum
        sh = 1
        while sh < tb:
            c = c + jnp.where(row < tb - sh, pltpu.roll(c, tb - sh, 0), 0.0)
            sh *= 2
        dlf = carry_ref[...] + c
        carry_ref[...] += jnp.sum(dcum, axis=0, keepdims=True)
        z = fl_ref[...] + b_ref[...]
        lane = lax.broadcasted_iota(jnp.int32, (tb, LANES), 1)
        dfl = jnp.where(lane < FOX_HEADS, dlf / (1.0 + jnp.exp(z)), 0.0)
        db_ref[...] += jnp.sum(dfl, axis=0, keepdims=True)
        dh_ref[...] = dfl.astype(BF16)

    rev = lambda i: (nb - 1 - i, 0)
    return pl.pallas_call(
        body, name=name, grid=(nb,),
        in_specs=[pl.BlockSpec((tb, D_MAIN), rev), pl.BlockSpec((tb, D_MAIN), rev), pl.BlockSpec((tb, LANES), rev),
                  pl.BlockSpec((1, LANES), lambda i: (0, 0))],
        out_specs=[pl.BlockSpec((tb, LANES), rev), pl.BlockSpec((1, LANES), lambda i: (0, 0))],
        out_shape=[jax.ShapeDtypeStruct((S, LANES), BF16), jax.ShapeDtypeStruct((1, LANES), F32)],
        scratch_shapes=[pltpu.VMEM((1, LANES), F32)],
        compiler_params=_cparams("arbitrary"),
    )(dn, drow, fl, bias)


FOX_SCALE = FOX_HEAD_DIM ** -0.5
LOG2E = 1.4426950408889634
LN2 = 0.6931471805599453
AUX = FOX_HEAD_DIM


def _split3(x):
    hi = x.astype(BF16).astype(F32)
    r = x - hi
    mid = r.astype(BF16).astype(F32)
    return hi, mid, (r - mid).astype(BF16).astype(F32)


def _lanes3(lane, base, parts, rest):
    return jnp.where(lane == base, parts[0], jnp.where(lane == base + 1, parts[1],
                                                       jnp.where(lane == base + 2, parts[2], rest)))


def _swap_halves(x):
    return pltpu.roll(x, FOX_HEAD_DIM, 1)


def _causal_steps(nq, keys_outer):
    if keys_outer:
        pairs = [(i, j) for j in range(nq) for i in range(j, nq)]
    else:
        pairs = [(i, j) for i in range(nq) for j in range(i + 1)]
    it, jt = zip(*pairs)
    return jnp.asarray(np.array(it, np.int32)), jnp.asarray(np.array(jt, np.int32))


def _in_proj_fox(x, w, *, tm, name):
    S = x.shape[0]
    tm = min(tm, S)

    def body(x_ref, w_ref, h_ref, qa_ref):
        acc = jnp.dot(x_ref[...].astype(BF16), w_ref[...], preferred_element_type=F32)
        h_ref[...] = acc.astype(BF16)
        lane = lax.broadcasted_iota(jnp.int32, (tm, LANES), 1)
        first = lane < FOX_HEAD_DIM
        ones_q = jnp.where((lane >= AUX) & (lane < AUX + 3), 1.0, 0.0)
        for g in range(FOX_PAIRS):
            q = acc[:, LANES * g:LANES * (g + 1)] * (FOX_SCALE * LOG2E)
            qa_ref[:, 2 * LANES * g:2 * LANES * g + LANES] = jnp.where(first, q, ones_q).astype(BF16)
            qa_ref[:, 2 * LANES * g + LANES:2 * LANES * (g + 1)] = jnp.where(first, _swap_halves(q), ones_q).astype(BF16)

    return pl.pallas_call(
        body, name=name, grid=(S // tm,),
        in_specs=[pl.BlockSpec((tm, D_MODEL), lambda i: (i, 0)), pl.BlockSpec((D_MODEL, D_IN), lambda i: (0, 0))],
        out_specs=[pl.BlockSpec((tm, D_IN), lambda i: (i, 0)), pl.BlockSpec((tm, 2 * D_MAIN), lambda i: (i, 0))],
        out_shape=[jax.ShapeDtypeStruct((S, D_IN), BF16), jax.ShapeDtypeStruct((S, 2 * D_MAIN), BF16)],
        compiler_params=_cparams("parallel"),
    )(x, w)


def _kv_proj_fox(x, wkv, negcum, *, tm, name):
    S = x.shape[0]
    tm = min(tm, S)

    def body(x_ref, w_ref, nc_ref, ka_ref, va_ref):
        acc = jnp.dot(x_ref[...].astype(BF16), w_ref[...], preferred_element_type=F32)
        nc = nc_ref[...]
        lane = lax.broadcasted_iota(jnp.int32, (tm, LANES), 1)
        first = lane < FOX_HEAD_DIM
        ones_k = jnp.where((lane >= AUX + 3) & (lane < AUX + 6), 1.0, 0.0)
        for g in range(FOX_PAIRS):
            k = acc[:, LANES * g:LANES * (g + 1)]
            v = acc[:, D_MAIN + LANES * g:D_MAIN + LANES * (g + 1)]
            for hh in range(2):
                sl = slice(LANES * (2 * g + hh), LANES * (2 * g + hh + 1))
                kh, vh = (k, v) if hh == 0 else (_swap_halves(k), _swap_halves(v))
                ncol = jnp.sum(jnp.where(lane == 2 * g + hh, nc, 0.0), axis=1, keepdims=True) * LOG2E
                ka_ref[:, sl] = jnp.where(first, kh, _lanes3(lane, AUX, _split3(ncol), ones_k)).astype(BF16)
                va_ref[:, sl] = jnp.where(first, vh, 1.0).astype(BF16)

    out = pl.BlockSpec((tm, 2 * D_MAIN), lambda i: (i, 0))
    shp = jax.ShapeDtypeStruct((S, 2 * D_MAIN), BF16)
    return pl.pallas_call(
        body, name=name, grid=(S // tm,),
        in_specs=[pl.BlockSpec((tm, D_MODEL), lambda i: (i, 0)), pl.BlockSpec((D_MODEL, 2 * D_MAIN), lambda i: (0, 0)),
                  pl.BlockSpec((tm, LANES), lambda i: (i, 0))],
        out_specs=[out, out], out_shape=[shp, shp],
        compiler_params=_cparams("parallel"),
    )(x, wkv, negcum)


def _dwkv(xt, dk, dv, dfl, *, tk, name):
    S = xt.shape[1]
    tk = min(tk, S)

    def body(x_ref, dk_ref, dv_ref, df_ref, o_ref):
        @pl.when(pl.program_id(0) == 0)
        def _():
            o_ref[...] = jnp.zeros_like(o_ref)

        for b_ref, c0 in ((dk_ref, 0), (dv_ref, D_MAIN), (df_ref, 2 * D_MAIN)):
            o_ref[:, c0:c0 + b_ref.shape[1]] += jnp.dot(x_ref[...], b_ref[...], preferred_element_type=F32)

    row = lambda n: pl.BlockSpec((tk, n), lambda k: (k, 0))
    return pl.pallas_call(
        body, name=name, grid=(S // tk,),
        in_specs=[pl.BlockSpec((D_MODEL, tk), lambda k: (0, k)), row(D_MAIN), row(D_MAIN), row(LANES)],
        out_specs=pl.BlockSpec((D_MODEL, 2 * D_MAIN + LANES), lambda k: (0, 0)),
        out_shape=jax.ShapeDtypeStruct((D_MODEL, 2 * D_MAIN + LANES), F32),
        compiler_params=_cparams("arbitrary"),
    )(xt, dk, dv, dfl)


def _dx1(dh, win, dk, dv, dfl, wkv, dz, *, tm, name):
    S = dh.shape[0]
    tm = min(tm, S)

    def body(dh_ref, win_ref, dk_ref, dv_ref, df_ref, wk_ref, wv_ref, wf_ref, dz_ref, o_ref):
        acc = ALPHA * dz_ref[...]
        for a_ref, b_ref in ((dh_ref, win_ref), (dk_ref, wk_ref), (dv_ref, wv_ref), (df_ref, wf_ref)):
            acc = acc + lax.dot_general(a_ref[...], b_ref[...], NT, preferred_element_type=F32)
        o_ref[...] = acc

    row = lambda n: pl.BlockSpec((tm, n), lambda i: (i, 0))
    wcols = lambda n, c: pl.BlockSpec((D_MODEL, n), lambda i, c=c: (0, c))
    return pl.pallas_call(
        body, name=name, grid=(S // tm,),
        in_specs=[row(D_IN), wcols(D_IN, 0), row(D_MAIN), row(D_MAIN), row(LANES),
                  wcols(D_MAIN, 0), wcols(D_MAIN, 1), wcols(LANES, 2 * D_MAIN // LANES), row(D_MODEL)],
        out_specs=row(D_MODEL), out_shape=jax.ShapeDtypeStruct((S, D_MODEL), F32),
        compiler_params=_cparams("parallel"),
    )(dh, win, dk, dv, dfl, wkv, wkv, wkv, dz)


def _fox_fwd(qa, ka, va, *, tq, name):
    S = qa.shape[0]
    tq = min(tq, S)
    nq = S // tq
    half = tq // 2
    it, jt = _causal_steps(nq, keys_outer=False)

    def body(it_ref, jt_ref, qa_ref, ka_ref, va_ref, y_ref, qb_ref, m_ref, acc_ref):
        n = pl.program_id(1)
        i, j = it_ref[n], jt_ref[n]
        first = lax.broadcasted_iota(jnp.int32, (tq, LANES), 1) < FOX_HEAD_DIM

        @pl.when(j == 0)
        def _():
            m_ref[...] = jnp.full_like(m_ref, NEG)
            acc_ref[...] = jnp.zeros_like(acc_ref)

        def update(hh, rows, nk, masked):
            sl = slice(LANES * hh, LANES * (hh + 1))
            s = lax.dot_general(qa_ref[rows, sl], ka_ref[0:nk, sl], NT, preferred_element_type=F32)
            if masked:
                r = lax.broadcasted_iota(jnp.int32, s.shape, 0) + rows.start
                c = lax.broadcasted_iota(jnp.int32, s.shape, 1)
                s = jnp.where(c <= r, s, NEG)
            m_prev = m_ref[hh, rows]
            m_new = jnp.maximum(m_prev, jnp.max(s, axis=1, keepdims=True))
            p = jnp.exp2(s - jnp.tile(m_new, (1, nk // LANES))).astype(BF16)
            acc_ref[hh, rows] = jnp.exp2(m_prev - m_new) * acc_ref[hh, rows] + jnp.dot(
                p, va_ref[0:nk, sl], preferred_element_type=F32)
            m_ref[hh, rows] = m_new

        @pl.when(j < i)
        def _():
            for hh in range(2):
                update(hh, slice(0, tq), tq, False)

        @pl.when(j == i)
        def _():
            for hh in range(2):
                for r0 in range(0, tq, half):
                    update(hh, slice(r0, r0 + half), r0 + half, True)
            lane = lax.broadcasted_iota(jnp.int32, (tq, LANES), 1)
            ys = []
            for hh in range(2):
                sl = slice(LANES * hh, LANES * (hh + 1))
                a = acc_ref[hh]
                denom = _swap_halves(a)
                ys.append(a / denom)
                lse2 = m_ref[hh] + jnp.log(jnp.where(first, denom, a)) * LOG2E
                qb_ref[:, sl] = _lanes3(lane, AUX + 3, _split3(-lse2), qa_ref[:, sl].astype(F32)).astype(BF16)
            y_ref[...] = jnp.where(first, ys[0], _swap_halves(ys[1]))

    qblock = pl.BlockSpec((tq, 2 * LANES), lambda g, n, it, jt: (it[n], g))
    kblock = pl.BlockSpec((tq, 2 * LANES), lambda g, n, it, jt: (jt[n], g))
    return pl.pallas_call(
        body, name=name,
        grid_spec=pltpu.PrefetchScalarGridSpec(
            num_scalar_prefetch=2, grid=(FOX_PAIRS, it.shape[0]),
            in_specs=[qblock, kblock, kblock],
            out_specs=[pl.BlockSpec((tq, LANES), lambda g, n, it, jt: (it[n], g)), qblock],
            scratch_shapes=[pltpu.VMEM((2, tq, LANES), F32), pltpu.VMEM((2, tq, LANES), F32)]),
        out_shape=[jax.ShapeDtypeStruct((S, D_MAIN), F32), jax.ShapeDtypeStruct((S, 2 * D_MAIN), BF16)],
        compiler_params=_cparams("parallel", "arbitrary"),
    )(it, jt, qa, ka, va)


def _fox_bwd(qb, ka, va, dya, dh, *, tq, name):
    S = qb.shape[0]
    tq = min(tq, S)
    nq = S // tq
    half = tq // 2
    it, jt = _causal_steps(nq, keys_outer=True)
    nsteps = it.shape[0]

    def body(it_ref, jt_ref, qb_ref, ka_ref, va_ref, dya_ref, dh_in, dq_ref, dk_ref, dv_ref, dn_ref, drow_ref,
             dq_acc, dk_acc, dv_acc):
        del dh_in
        n = pl.program_id(1)
        i, j = it_ref[n], jt_ref[n]
        first = lax.broadcasted_iota(jnp.int32, (tq, LANES), 1) < FOX_HEAD_DIM

        @pl.when(n == 0)
        def _():
            dq_acc[...] = jnp.zeros_like(dq_acc)

        @pl.when(i == j)
        def _():
            dk_acc[...] = jnp.zeros_like(dk_acc)
            dv_acc[...] = jnp.zeros_like(dv_acc)

        def update(hh, keys, q0, masked):
            sl = slice(LANES * hh, LANES * (hh + 1))
            qbh, kah, dyah = qb_ref[q0:tq, sl], ka_ref[keys, sl], dya_ref[q0:tq, sl]
            eT = lax.dot_general(kah, qbh, NT, preferred_element_type=F32)
            if masked:
                r = lax.broadcasted_iota(jnp.int32, eT.shape, 0) + keys.start
                c = lax.broadcasted_iota(jnp.int32, eT.shape, 1) + q0
                eT = jnp.where(r <= c, eT, NEG)
            pT = jnp.exp2(eT)
            dsT = pT * lax.dot_general(va_ref[keys, sl], dyah, NT, preferred_element_type=F32)
            dsb = dsT.astype(BF16)
            dv_acc[hh, keys] += jnp.dot(pT.astype(BF16), dyah, preferred_element_type=F32)
            dk_acc[hh, keys] += jnp.dot(dsb, qbh, preferred_element_type=F32)
            rows = pl.ds(pl.multiple_of(i * tq + q0, half), tq - q0)
            dq_acc[hh, rows, :] += lax.dot_general(dsb, kah, TN, preferred_element_type=F32)

        @pl.when(i > j)
        def _():
            for hh in range(2):
                update(hh, slice(0, tq), 0, False)

        @pl.when(i == j)
        def _():
            for hh in range(2):
                for k0 in range(0, tq, half):
                    update(hh, slice(k0, k0 + half), k0, True)

        @pl.when(i == nq - 1)
        def _():
            dk_ref[...] = (jnp.where(first, dk_acc[0], _swap_halves(dk_acc[1])) * LN2).astype(BF16)
            dv_ref[...] = jnp.where(first, dv_acc[0], _swap_halves(dv_acc[1])).astype(BF16)
            dn_ref[...] = jnp.where(first, _swap_halves(dk_acc[0]), dk_acc[1])

        @pl.when(n == nsteps - 1)
        def _():
            first_s = lax.broadcasted_iota(jnp.int32, (S, LANES), 1) < FOX_HEAD_DIM
            dq_ref[...] = (jnp.where(first_s, dq_acc[0], _swap_halves(dq_acc[1])) * FOX_SCALE).astype(BF16)
            drow_ref[...] = jnp.where(first_s, _swap_halves(dq_acc[0]), dq_acc[1])

    qblock = pl.BlockSpec((tq, 2 * LANES), lambda g, n, it, jt: (it[n], g))
    kblock = pl.BlockSpec((tq, 2 * LANES), lambda g, n, it, jt: (jt[n], g))
    whole = pl.BlockSpec((S, LANES), lambda g, n, it, jt: (0, g))
    kout = pl.BlockSpec((tq, LANES), lambda g, n, it, jt: (jt[n], g))
    return pl.pallas_call(
        body, name=name,
        grid_spec=pltpu.PrefetchScalarGridSpec(
            num_scalar_prefetch=2, grid=(FOX_PAIRS, nsteps),
            in_specs=[qblock, kblock, kblock, qblock, pl.BlockSpec(memory_space=pl.ANY)],
            out_specs=[whole, kout, kout, kout, whole],
            scratch_shapes=[pltpu.VMEM((2, S, LANES), F32), pltpu.VMEM((2, tq, LANES), F32),
                            pltpu.VMEM((2, tq, LANES), F32)]),
        out_shape=[jax.ShapeDtypeStruct(dh.shape, dh.dtype), jax.ShapeDtypeStruct((S, D_MAIN), BF16),
                   jax.ShapeDtypeStruct((S, D_MAIN), BF16),
                   jax.ShapeDtypeStruct((S, D_MAIN), F32), jax.ShapeDtypeStruct((S, D_MAIN), F32)],
        input_output_aliases={6: 0},
        compiler_params=_cparams("parallel", "arbitrary"),
    )(it, jt, qb, ka, va, dya, dh)


TB_ROWS = 512
TB_SEQ = 512
TK_DW_IN = 2048
TK_DW_OUT = 1024
TQ_FOX_FWD = 1024
TQ_FOX_BWD = 1024


def _local_step(x, mem, target, win0, pscale, ln_g, ln_b, bias, comm):
    ones = jnp.ones((1, D_MAIN), F32)
    g0, b0, g1, b1 = ln_g[0:1], ln_b[0:1], ln_g[1:2], ln_b[1:2]
    mm = lambda a, b, mode, dt, tm, tn, tk, name, **kw: _mm(a, b, mode=mode, out_dtype=dt, tm=tm, tn=tn, tk=tk,
                                                            name=name, **kw)

    h0, xt = _in_proj_t(x, win0, tm=256, name="l0_in")
    wmkv, pw, wout0 = comm.next_weights(h0)
    pm, mixed = _pool_fwd(h0, pw, tb=TB_SEQ, name="l0_pool_fwd")
    mkv0 = mm(mem, wmkv[0], "nn", F32, 256, 1024, 1024, "l0_mkv")
    ymem0 = _memattn_fwd(h0, mkv0, tb=TB_SEQ, name="l0_mem_fwd")
    yg0 = _gate_fwd(mixed, pscale, h0, ymem0, tb=TB_ROWS, name="l0_gate_fwd")
    x1, xhat0, rstd0, x1t = _out_ln(yg0, wout0, x, g0, b0, tb=TB_SEQ, name="l0_out_ln")
    win1, wout1, wkv, wf = comm.late_weights(x1)

    fl = mm(x1, wf, "nn", F32, 512, LANES, D_MODEL, "f_proj")
    negcum = _forget_fwd(fl, bias, tb=TB_SEQ, name="forget_fwd")
    ka, va = _kv_proj_fox(x1, wkv, negcum, tm=512, name="kv_proj")

    h1, qa = _in_proj_fox(x1, win1, tm=256, name="l1_in")
    y1, qb = _fox_fwd(qa, ka, va, tq=TQ_FOX_FWD, name="fox_fwd")
    mkv1 = mm(mem, wmkv[1], "nn", F32, 256, 1024, 1024, "l1_mkv")
    ymem1 = _memattn_fwd(h1, mkv1, tb=TB_SEQ, name="l1_mem_fwd")
    yg1 = _gate_fwd(y1, ones, h1, ymem1, tb=TB_ROWS, name="l1_gate_fwd")
    dz1, dg1, db1, sq = _out_ln_loss(yg1, wout1, x1, g1, b1, target, tb=TB_SEQ, name="l1_out_ln_loss")

    dwout1 = mm(yg1, dz1, "tn", BF16, D_MIX, D_MODEL, TK_DW_OUT, "l1_dwout")
    dyg1 = mm(dz1, wout1, "nt", BF16, 512, D_MIX, D_MODEL, "l1_dyg")
    dya, dymem1, dh1 = _gate_bwd(dyg1, y1, ones, h1, ymem1, tb=TB_ROWS, name="l1_gate_bwd", fox=True)
    dh1, dk, dv, dnp, drowp = _fox_bwd(qb, ka, va, dya, dh1, tq=TQ_FOX_BWD, name="fox_bwd")
    dfl, dbias = _forget_bwd(dnp, drowp, fl, bias, tb=TB_SEQ, name="forget_bwd")
    dh1, dmkv1 = _memattn_bwd(h1, mkv1, dymem1, dh1, tb=TB_SEQ, name="l1_mem_bwd")
    dwmkv1 = mm(mem, dmkv1, "tn", BF16, D_MODEL, 1024, N_MEM, "l1_dwmkv")
    dwin1 = mm(x1t, dh1, "nn", BF16, D_MODEL, D_IN // 2, TK_DW_IN, "l1_dwin")
    dwkv = _dwkv(x1t, dk, dv, dfl, tk=TK_DW_OUT, name="dwkv")
    dwkv = dwkv[:, :2 * D_MAIN + FOX_HEADS].reshape(D_MODEL, N_DEV, -1).transpose(1, 0, 2).astype(BF16)
    anchor = comm.send_layer1(dict(w_out=dwout1, w_mem_kv=dwmkv1, w_in=dwin1, w_kv_shared=dwkv))
    dx1 = _dx1(dh1, win1, dk, dv, dfl, wkv, dz1, tm=256, name="l1_dx")

    dz0, dg0, db0 = _ln_bwd(dx1, xhat0, rstd0, g0 + anchor, tb=TB_ROWS, name="l0_ln_bwd")
    dwout0 = mm(yg0, dz0, "tn", BF16, D_MIX, D_MODEL, TK_DW_OUT, "l0_dwout")
    dyg0 = mm(dz0, wout0, "nt", BF16, 512, D_MIX, D_MODEL, "l0_dyg")
    dy0, dymem0, dh0 = _gate_bwd(dyg0, mixed, pscale, h0, ymem0, tb=TB_ROWS, name="l0_gate_bwd")
    comm.landed_layer1(dy0)
    dh0, dpw, dpscale = _pool_bwd(dy0, pm, mixed, pw, pscale, dh0, tb=TB_SEQ, name="l0_pool_bwd")
    dh0, dmkv0 = _memattn_bwd(h0, mkv0, dymem0, dh0, tb=TB_SEQ, name="l0_mem_bwd")
    dwmkv0 = mm(mem, dmkv0, "tn", BF16, D_MODEL, 1024, N_MEM, "l0_dwmkv")
    dh0 = comm.send_layer0_first(dict(w_out=dwout0, w_mem_kv=dwmkv0, pool_w=dpw), dh0)
    dwin0 = mm(xt, dh0, "nn", BF16, D_MODEL, D_IN // 2, TK_DW_IN, "l0_dwin")
    dz0 = comm.send_layer0_rest(dict(w_in=dwin0, pool_scale=dpscale, ln_g=jnp.concatenate([dg0, dg1]),
                                     ln_b=jnp.concatenate([db0, db1]), b_forget=dbias[0, :FOX_HEADS]), dz0)
    gx = mm(dh0, win0, "nt", F32, 256, D_MODEL, D_IN, "l0_dx", add=dz0, add_scale=ALPHA)
    return sq, gx


MESH_ID = pl.DeviceIdType.MESH
HBM = pl.BlockSpec(memory_space=pl.ANY)
SLICED = {"w_in": (2, D_IN // N_DEV), "w_mem_kv": (1, D_MODEL // N_DEV), "w_out": (1, D_MIX // N_DEV),
          "pool_w": (1, POOL_GROUP // N_DEV)}


def _place():
    return lax.axis_index("x"), lax.axis_index("y"), lax.axis_index("c")


def _slot(p):
    return 4 * p[0] + 2 * p[1] + p[2]


def _cut(ref, axis, width, s):
    idx = [slice(None)] * len(ref.shape)
    idx[axis] = pl.ds(s * width, width)
    return ref.at[tuple(idx)]


def _all_gather(shards, cuts, *, name):
    nt = len(shards)

    def full_shape(a, cut):
        if cut is None:
            return (N_DEV,) + a.shape
        return a.shape[:cut[0]] + (a.shape[cut[0]] * N_DEV,) + a.shape[cut[0] + 1:]

    def body(*refs):
        ins, outs = refs[:nt], refs[nt:2 * nt]
        send_sems, recv_sems, local_sems = refs[2 * nt:]
        x, y, c = _place()
        me, sibling = (x, y, c), (x, y, 1 - c)
        chips = [(1 - x, y), (x, 1 - y), (1 - x, 1 - y)]

        def place(t, s):
            return outs[t].at[s] if cuts[t] is None else _cut(outs[t], cuts[t][0], cuts[t][1], s)

        def copies(k, block, to, from_input=False):
            s = _slot(block)
            return [pltpu.make_async_remote_copy(
                src_ref=ins[t] if from_input else place(t, s), dst_ref=place(t, s),
                send_sem=send_sems.at[nt * k + t], recv_sem=recv_sems.at[nt * k + t],
                device_id=to, device_id_type=MESH_ID) for t in range(nt)]

        mine = [pltpu.make_async_copy(ins[t], place(t, _slot(me)), local_sems.at[t]) for t in range(nt)]
        for cp in mine:
            cp.start()
        first = [copies(0, me, sibling, True)] + [copies(1 + j, me, (*chip, c), True) for j, chip in enumerate(chips)]
        for group in first:
            for cp in group:
                cp.start()
        passed = [copies(4 + j, (*chip, c), sibling) for j, chip in enumerate(chips)]
        for j, chip in enumerate(chips):
            for cp in copies(1 + j, (*chip, c), me):
                cp.wait_recv()
            for cp in passed[j]:
                cp.start()
        for cp in copies(0, sibling, me):
            cp.wait_recv()
        for j, chip in enumerate(chips):
            for cp in copies(4 + j, (*chip, 1 - c), me):
                cp.wait_recv()
        for group in first + passed:
            for cp in group:
                cp.wait_send()
        for cp in mine:
            cp.wait()

    return pl.pallas_call(
        body, name=name, in_specs=[HBM] * nt, out_specs=[HBM] * nt,
        out_shape=[jax.ShapeDtypeStruct(full_shape(a, cut), a.dtype) for a, cut in zip(shards, cuts)],
        scratch_shapes=[pltpu.SemaphoreType.DMA((7 * nt,)), pltpu.SemaphoreType.DMA((7 * nt,)),
                        pltpu.SemaphoreType.DMA((nt,))],
    )(*shards)


def _exchange_copies(items, ins, outs, send_sems, recv_sems, local_sems, gather=False):
    nt = len(items)
    x, y, c = _place()
    me = _slot((x, y, c))
    flip = lambda v, bit: 1 - v if bit else v

    def part(ref, cut, s):
        return ref.at[s] if cut is None else _cut(ref, cut[0], cut[1], s)

    def src(t, s):
        return ins[t] if gather else part(ins[t], items[t][0], s)

    def dst(t, s):
        if gather:
            return part(outs[items[t][1]], items[t][0], s)
        d = outs[items[t][1]].at[s]
        return d if items[t][2] is None else d.at[items[t][2]]

    sends, arrivals = [], []
    for k in range(1, N_DEV):
        peer = (flip(x, k & 4), flip(y, k & 2), flip(c, k & 1))
        ps = _slot(peer)
        for t in range(nt):
            sems = dict(send_sem=send_sems.at[nt * (k - 1) + t], recv_sem=recv_sems.at[nt * (k - 1) + t],
                        device_id=peer, device_id_type=MESH_ID)
            sends.append(pltpu.make_async_remote_copy(src_ref=src(t, ps), dst_ref=dst(t, me), **sems))
            arrivals.append(pltpu.make_async_remote_copy(src_ref=src(t, ps), dst_ref=dst(t, ps), **sems))
    mine = [pltpu.make_async_copy(src(t, me), dst(t, me), local_sems.at[t]) for t in range(nt)]
    return sends, arrivals, mine


SEMS = pl.BlockSpec(memory_space=pltpu.SEMAPHORE)
SIDE_EFFECT = pltpu.SideEffectType.DATAFLOW_SIDE_EFFECTING


def _exchange_start(srcs, items, landings, *, name, gather=False, carry=()):
    nt, nl, nc = len(srcs), len(landings), len(carry)

    def body(*refs):
        ins, lands = refs[:nt], refs[nt:nt + nl]
        send_sems, recv_sems, local_sems = refs[nt + nl + nc:nt + nl + nc + 3]
        token = refs[-1]
        sends, _, mine = _exchange_copies(items, ins, lands, send_sems, recv_sems, local_sems, gather)
        for cp in sends + mine:
            cp.start()
        token[...] = jnp.zeros_like(token)

    hbm = lambda a: pltpu.HBM(a.shape, a.dtype)
    fresh = [pltpu.with_memory_space_constraint(
        lax.empty(l.shape, l.dtype) if isinstance(l, jax.ShapeDtypeStruct) else l, pltpu.HBM) for l in landings]
    res = pl.pallas_call(
        body, name=name, in_specs=[HBM] * (nt + nl + nc),
        out_specs=[SEMS, SEMS, SEMS] + [HBM] * (nt + nl + nc) + [pl.BlockSpec(memory_space=pltpu.VMEM)],
        out_shape=[pltpu.SemaphoreType.DMA((7 * nt,)), pltpu.SemaphoreType.DMA((7 * nt,)), pltpu.SemaphoreType.DMA((nt,))]
        + [hbm(a) for a in srcs] + [hbm(l) for l in landings] + [hbm(a) for a in carry]
        + [jax.ShapeDtypeStruct((8, LANES), F32)],
        input_output_aliases={i: 3 + i for i in range(nt + nl + nc)},
        compiler_params=pltpu.CompilerParams(has_side_effects=SIDE_EFFECT),
    )(*[pltpu.with_memory_space_constraint(a, pltpu.HBM) for a in srcs], *fresh,
      *[pltpu.with_memory_space_constraint(a, pltpu.HBM) for a in carry])
    return res[:3 + nt + nl], res[-1][0:1, 0:1], res[3 + nt + nl:-1]


def _exchange_wait(state, items, nt, after, *, name, gather=False):
    sems, bufs = state[:3], state[3:]
    nl = len(bufs) - nt

    def body(*refs):
        ins, lands = refs[:nt], refs[nt:nt + nl]
        send_sems, recv_sems, local_sems = refs[nt + nl:nt + nl + 3]
        sends, arrivals, mine = _exchange_copies(items, ins, lands, send_sems, recv_sems, local_sems, gather)
        for sent, landed in zip(sends, arrivals):
            landed.wait_recv()
            sent.wait_send()
        for cp in mine:
            cp.wait()

    hbm = lambda a: pltpu.HBM(a.shape, a.dtype)
    res = pl.pallas_call(
        body, name=name, in_specs=[HBM] * (nt + nl) + [SEMS, SEMS, SEMS, HBM], out_specs=[HBM] * (nt + nl),
        out_shape=[hbm(a) for a in bufs],
        input_output_aliases={i: i for i in range(nt + nl)},
        compiler_params=pltpu.CompilerParams(has_side_effects=SIDE_EFFECT),
    )(*bufs, *sems, after)
    return res[nt:]


def _adamw(recv, w, m, v, *, split, name):
    shape = w.shape
    axis, parts = split
    block = shape[:axis] + (shape[axis] // parts,) + shape[axis + 1:]
    nd = len(shape)

    def body(r_ref, w_ref, m_ref, v_ref, g_ref, d_ref, nm_ref, nv_ref):
        g = r_ref[0].astype(F32)
        for j in range(1, N_DEV):
            g = g + r_ref[j].astype(F32)
        nm = ADAM_B1 * m_ref[...] + (1.0 - ADAM_B1) * g
        nv = ADAM_B2 * v_ref[...] + (1.0 - ADAM_B2) * (g * g)
        m_hat = nm / (1.0 - ADAM_B1 ** ADAM_STEP)
        v_hat = nv / (1.0 - ADAM_B2 ** ADAM_STEP)
        g_ref[...] = g
        nm_ref[...] = nm
        nv_ref[...] = nv
        d_ref[...] = -ADAM_LR * (m_hat / (jnp.sqrt(v_hat) + ADAM_EPS) + ADAM_WD * w_ref[...])

    at = lambda i: tuple(i if a == axis else 0 for a in range(nd))
    one = pl.BlockSpec(block, at)
    shp = jax.ShapeDtypeStruct(shape, F32)
    return pl.pallas_call(
        body, name=name, grid=(parts,),
        in_specs=[pl.BlockSpec((N_DEV,) + block, lambda i: (0,) + at(i)), one, one, one],
        out_specs=[one, one, one, one], out_shape=[shp, shp, shp, shp],
        compiler_params=_cparams("parallel"),
    )(recv, w, m, v)


BIG = ("w_in", "w_mem_kv", "w_out", "pool_w", "w_kv_shared")
SMALL = ("pool_scale", "ln_g", "ln_b", "b_forget")
SMALL_ROWS = 40
ADAM_SPLIT = {"w_in": (1, 4), "w_mem_kv": (0, 2), "w_out": (0, 2), "pool_w": (0, 1), "w_kv_shared": (0, 4)}
PER_LAYER_CUT = {"w_out": (0, D_MIX // N_DEV), "w_mem_kv": (0, D_MODEL // N_DEV), "w_in": (1, D_IN // N_DEV),
                 "pool_w": (1, POOL_GROUP // N_DEV)}
EARLY = ("w_out", "w_mem_kv", "w_in", "w_kv_shared")
EARLY_ITEMS = [(PER_LAYER_CUT[n], i, 1) for i, n in enumerate(EARLY[:3])] + [(None, 3, None)]


def _flat(parts, rows):
    v = jnp.concatenate([p.reshape(-1) for p in parts])
    return jnp.pad(v, (0, rows * LANES - v.shape[0])).reshape(rows, LANES)


def _unflat(flat, shapes):
    v, out, off = flat.reshape(-1), [], 0
    for s in shapes:
        n = math.prod(s)
        out.append(v[off:off + n].reshape(s))
        off += n
    return out


def kernel(x, mem, w_in, w_mem_kv, w_out, ln_g, ln_b, pool_w, pool_scale, w_kv_shared, b_forget, loss_target, m_w_in, m_w_mem_kv, m_w_out, m_ln_g, m_ln_b, m_pool_w, m_pool_scale, m_w_kv_shared, m_b_forget, v_w_in, v_w_mem_kv, v_w_out, v_ln_g, v_ln_b, v_pool_w, v_pool_scale, v_w_kv_shared, v_b_forget):
    w = dict(w_in=w_in, w_mem_kv=w_mem_kv, w_out=w_out, ln_g=ln_g, ln_b=ln_b, pool_w=pool_w[0],
             pool_scale=pool_scale, w_kv_shared=w_kv_shared, b_forget=b_forget)
    m = dict(w_in=m_w_in, w_mem_kv=m_w_mem_kv, w_out=m_w_out, ln_g=m_ln_g, ln_b=m_ln_b, pool_w=m_pool_w[0],
             pool_scale=m_pool_scale, w_kv_shared=m_w_kv_shared, b_forget=m_b_forget)
    v = dict(w_in=v_w_in, w_mem_kv=v_w_mem_kv, w_out=v_w_out, ln_g=v_ln_g, ln_b=v_ln_b, pool_w=v_pool_w[0],
             pool_scale=v_pool_scale, w_kv_shared=v_w_kv_shared, b_forget=v_b_forget)

    wb = {n: w[n].astype(BF16) for n in BIG}
    bdt = wb["w_in"].dtype
    win0, pscale = _all_gather([wb["w_in"][0], jnp.pad(pool_scale, ((0, 7), (0, 0)))], [PER_LAYER_CUT["w_in"], None],
                               name="gather_weights")
    pscale = pscale[:, 0, :].reshape(1, D_MAIN)
    next_srcs = [wb["w_mem_kv"], wb["pool_w"], wb["w_out"][0]]
    next_items = [(SLICED["w_mem_kv"], 0, None), (SLICED["pool_w"], 1, None), (PER_LAYER_CUT["w_out"], 2, None)]
    next_state, _, (win0,) = _exchange_start(
        next_srcs, next_items,
        [jax.ShapeDtypeStruct((2, D_MODEL, D_MODEL), bdt), jax.ShapeDtypeStruct((4, POOL_GROUP, POOL_GROUP), bdt),
         jax.ShapeDtypeStruct((D_MIX, D_MODEL), bdt)], name="gather_next_start", gather=True, carry=[win0])
    late_srcs = [wb["w_in"][1], wb["w_out"][1], wb["w_kv_shared"]]
    late_items = [(PER_LAYER_CUT["w_in"], 0, None), (PER_LAYER_CUT["w_out"], 1, None), (None, 2, None)]
    late = {}
    bias = jnp.pad(b_forget, (0, LANES - FOX_HEADS)).reshape(1, LANES)

    def next_weights(h0):
        wmkv, pw, wout0 = _exchange_wait(next_state, next_items, len(next_srcs), h0, name="gather_next_wait", gather=True)
        late["state"], _, (pw,) = _exchange_start(
            late_srcs, late_items,
            [jax.ShapeDtypeStruct((D_MODEL, D_IN), bdt), jax.ShapeDtypeStruct((D_MIX, D_MODEL), bdt),
             jax.ShapeDtypeStruct((N_DEV,) + w_kv_shared.shape, bdt)], name="gather_late_start", gather=True, carry=[pw])
        return wmkv, pw, wout0

    def late_weights(x1):
        win1, wout1, wkv = _exchange_wait(late["state"], late_items, len(late_srcs), x1, name="gather_late_wait",
                                          gather=True)
        wkv = jnp.pad(wkv.transpose(1, 0, 2).reshape(D_MODEL, -1), ((0, 0), (0, LANES - FOX_HEADS)))
        return win1, wout1, wkv, wkv[:, 2 * D_MAIN:]

    sent, recv = {}, {}
    first_items = [(PER_LAYER_CUT["w_out"], 0, 0), (PER_LAYER_CUT["w_mem_kv"], 1, 0), (PER_LAYER_CUT["pool_w"], 2, None)]
    rest_items = [(PER_LAYER_CUT["w_in"], 0, 0), (None, 1, None)]

    def send_layer1(g):
        srcs = [g[n] for n in EARLY]
        lands = [jax.ShapeDtypeStruct((N_DEV, 2) + w[n].shape[1:], g[n].dtype) for n in EARLY[:3]]
        lands.append(jax.ShapeDtypeStruct(g["w_kv_shared"].shape, g["w_kv_shared"].dtype))
        sent["layer1"], anchor, _ = _exchange_start(srcs, EARLY_ITEMS, lands, name="exchange_early_start")
        return anchor

    def landed_layer1(after):
        recv["w_out"], recv["w_mem_kv"], recv["w_in"], recv["w_kv_shared"] = _exchange_wait(
            sent["layer1"], EARLY_ITEMS, len(EARLY), after, name="exchange_early_wait")

    def send_layer0_first(g, carry):
        srcs = [g["w_out"], g["w_mem_kv"], g["pool_w"]]
        sent["first"], _, (carry,) = _exchange_start(
            srcs, first_items,
            [recv["w_out"], recv["w_mem_kv"], jax.ShapeDtypeStruct((N_DEV,) + w["pool_w"].shape, srcs[2].dtype)],
            name="exchange_mid_start", carry=[carry])
        return carry

    def send_layer0_rest(g, carry):
        small = jnp.concatenate([g["pool_scale"].reshape(N_DEV, -1)]
                                + [jnp.broadcast_to(g[n].reshape(1, -1), (N_DEV, g[n].size)) for n in SMALL[1:]], axis=1)
        small = jnp.pad(small, ((0, 0), (0, SMALL_ROWS * LANES - small.shape[1]))).reshape(N_DEV, SMALL_ROWS, LANES)
        sent["rest"], _, (carry,) = _exchange_start(
            [g["w_in"], small], rest_items, [recv["w_in"], jax.ShapeDtypeStruct(small.shape, small.dtype)],
            name="exchange_last_start", carry=[carry])
        return carry

    comm = types.SimpleNamespace(next_weights=next_weights, late_weights=late_weights, send_layer1=send_layer1,
                                 landed_layer1=landed_layer1, send_layer0_first=send_layer0_first,
                                 send_layer0_rest=send_layer0_rest)
    sq, gx = _local_step(x[0], mem[0], loss_target[0], win0, pscale, ln_g, ln_b, bias, comm)
    loss = lax.psum((0.5 / D_MODEL) * jnp.sum(sq), ("x", "y", "c"))

    recv["w_out"], recv["w_mem_kv"], recv["pool_w"] = _exchange_wait(sent["first"], first_items, len(first_items), gx,
                                                                     name="exchange_mid_wait")
    recv["w_in"], recv["small"] = _exchange_wait(sent["rest"], rest_items, len(rest_items), gx,
                                                 name="exchange_last_wait")

    outs = {}
    for n in BIG:
        res = _adamw(recv[n], w[n], m[n], v[n], split=ADAM_SPLIT[n], name="adamw_" + n)
        for kind, a in zip(("grad", "delta", "new_m", "new_v"), res):
            outs[kind, n] = a[None] if n == "pool_w" else a
    small_shapes = [w[n].shape for n in SMALL]
    res = _adamw(recv["small"], _flat([w[n] for n in SMALL], SMALL_ROWS), _flat([m[n] for n in SMALL], SMALL_ROWS),
                 _flat([v[n] for n in SMALL], SMALL_ROWS), split=(0, 1), name="adamw_small")
    for kind, flat in zip(("grad", "delta", "new_m", "new_v"), res):
        for n, a in zip(SMALL, _unflat(flat, small_shapes)):
            outs[kind, n] = a
    order = ("w_in", "w_mem_kv", "w_out", "ln_g", "ln_b", "pool_w", "pool_scale", "w_kv_shared", "b_forget")
    return (loss, gx[None], *[outs[kind, n] for kind in ("grad", "delta", "new_m", "new_v") for n in order])
```

```python
import math
import types

import numpy as np
import jax
import jax.numpy as jnp
from jax import lax
from jax.experimental import pallas as pl
from jax.experimental.pallas import tpu as pltpu

F32 = jnp.float32
BF16 = jnp.bfloat16

D_MODEL = 1024
D_MAIN = 1024
D_MEM = 512
D_MIX = D_MAIN + D_MEM
D_IN = 2 * D_MIX
N_MEM = 256
MEM_HEADS = 4
MEM_HEAD_DIM = 128
FOX_HEADS = 16
FOX_HEAD_DIM = 64
FOX_PAIRS = FOX_HEADS // 2
POOL_WINDOWS = (2, 4, 8, 16)
POOL_GROUP = 256
POOL_HALO = 16
ALPHA = 4.0 ** 0.25
LN_EPS = 1e-5
NEG = -1e30
LANES = 128
N_DEV = 8

ADAM_LR = 0.001
ADAM_B1 = 0.9
ADAM_B2 = 0.999
ADAM_EPS = 1e-08
ADAM_WD = 0.01
ADAM_STEP = 10

VMEM_LIMIT = 56 * 1024 * 1024

NN = (((1,), (0,)), ((), ()))
NT = (((1,), (1,)), ((), ()))
TN = (((0,), (0,)), ((), ()))


def _cparams(*sem):
    return pltpu.CompilerParams(dimension_semantics=sem, vmem_limit_bytes=VMEM_LIMIT)


def _sigmoid(z):
    return 1.0 / (1.0 + jnp.exp(-z))


def _mm(a, b, *, mode, out_dtype, tm, tn, tk, name, add=None, add_scale=1.0):
    if mode == "nn":
        (M, K), (K2, N) = a.shape, b.shape
    elif mode == "nt":
        (M, K), (N, K2) = a.shape, b.shape
    else:
        (K, M), (K2, N) = a.shape, b.shape
    assert K == K2, (a.shape, b.shape, mode)
    tm, tn, tk = min(tm, M), min(tn, N), min(tk, K)
    assert M % tm == 0 and N % tn == 0 and K % tk == 0, (M, N, K, tm, tn, tk)
    gm, gn, gk = M // tm, N // tn, K // tk
    dims = {"nn": NN, "nt": NT, "tn": TN}[mode]
    if mode == "tn":
        a_spec = pl.BlockSpec((tk, tm), lambda i, j, k: (k, i))
    else:
        a_spec = pl.BlockSpec((tm, tk), lambda i, j, k: (i, k))
    if mode == "nt":
        b_spec = pl.BlockSpec((tn, tk), lambda i, j, k: (j, k))
    else:
        b_spec = pl.BlockSpec((tk, tn), lambda i, j, k: (k, j))
    o_spec = pl.BlockSpec((tm, tn), lambda i, j, k: (i, j))
    has_add = add is not None
    acc_in_out = out_dtype == F32

    def body(*refs):
        a_ref, b_ref = refs[0], refs[1]
        add_ref = refs[2] if has_add else None
        o_ref = refs[3] if has_add else refs[2]
        prod = lax.dot_general(a_ref[...].astype(BF16), b_ref[...].astype(BF16), dims,
                               preferred_element_type=F32)

        def finish(r):
            if has_add:
                r = r + add_scale * add_ref[...]
            o_ref[...] = r.astype(out_dtype)

        if gk == 1:
            finish(prod)
        else:
            acc_ref = o_ref if acc_in_out else refs[-1]
            k = pl.program_id(2)

            @pl.when(k == 0)
            def _():
                acc_ref[...] = prod

            @pl.when(k > 0)
            def _():
                acc_ref[...] += prod

            if has_add or not acc_in_out:
                @pl.when(k == gk - 1)
                def _():
                    finish(acc_ref[...])

    in_specs = [a_spec, b_spec] + ([o_spec] if has_add else [])
    args = (a, b) + ((add,) if has_add else ())
    return pl.pallas_call(
        body, name=name, grid=(gm, gn, gk), in_specs=in_specs, out_specs=o_spec,
        out_shape=jax.ShapeDtypeStruct((M, N), out_dtype),
        scratch_shapes=[pltpu.VMEM((tm, tn), F32)] if gk > 1 and not acc_in_out else [],
        compiler_params=_cparams("parallel", "parallel", "arbitrary"),
    )(*args)


def _ln_stats(z):
    mu = jnp.mean(z, axis=1, keepdims=True)
    zc = z - mu
    var = jnp.mean(zc * zc, axis=1, keepdims=True)
    rstd = lax.rsqrt(var + LN_EPS)
    return zc * rstd, rstd


def _ln_bwd_math(dy, xhat, rstd, g):
    dxh = dy * g
    m1 = jnp.mean(dxh, axis=1, keepdims=True)
    m2 = jnp.mean(dxh * xhat, axis=1, keepdims=True)
    return rstd * (dxh - m1 - xhat * m2)


def _out_ln(yg, wout, x, g, b, *, tb, name):
    S = x.shape[0]
    tb = min(tb, S)

    def body(yg_ref, w_ref, x_ref, g_ref, b_ref, y_ref, xhat_ref, rstd_ref, yt_ref):
        o = jnp.dot(yg_ref[...], w_ref[...], preferred_element_type=F32)
        xhat, rstd = _ln_stats(ALPHA * x_ref[...] + o)
        xhat_ref[...] = xhat
        rstd_ref[...] = rstd
        y = xhat * g_ref[...] + b_ref[...]
        y_ref[...] = y
        yt_ref[...] = y.T.astype(BF16)

    row = pl.BlockSpec((tb, D_MODEL), lambda i: (i, 0))
    vec = pl.BlockSpec((1, D_MODEL), lambda i: (0, 0))
    return pl.pallas_call(
        body, name=name, grid=(S // tb,),
        in_specs=[pl.BlockSpec((tb, D_MIX), lambda i: (i, 0)), pl.BlockSpec((D_MIX, D_MODEL), lambda i: (0, 0)),
                  row, vec, vec],
        out_specs=[row, row, pl.BlockSpec((tb, 1), lambda i: (i, 0)), pl.BlockSpec((D_MODEL, tb), lambda i: (0, i))],
        out_shape=[jax.ShapeDtypeStruct((S, D_MODEL), F32), jax.ShapeDtypeStruct((S, D_MODEL), F32),
                   jax.ShapeDtypeStruct((S, 1), F32), jax.ShapeDtypeStruct((D_MODEL, S), BF16)],
        compiler_params=_cparams("parallel"),
    )(yg, wout, x, g, b)


def _in_proj_t(x, w, *, tm, name):
    S = x.shape[0]
    tm = min(tm, S)

    def body(x_ref, w_ref, h_ref, xt_ref):
        xv = x_ref[...]
        h_ref[...] = jnp.dot(xv.astype(BF16), w_ref[...], preferred_element_type=F32).astype(BF16)
        xt_ref[...] = xv.T.astype(BF16)

    return pl.pallas_call(
        body, name=name, grid=(S // tm,),
        in_specs=[pl.BlockSpec((tm, D_MODEL), lambda i: (i, 0)), pl.BlockSpec((D_MODEL, D_IN), lambda i: (0, 0))],
        out_specs=[pl.BlockSpec((tm, D_IN), lambda i: (i, 0)), pl.BlockSpec((D_MODEL, tm), lambda i: (0, i))],
        out_shape=[jax.ShapeDtypeStruct((S, D_IN), BF16), jax.ShapeDtypeStruct((D_MODEL, S), BF16)],
        compiler_params=_cparams("parallel"),
    )(x, w)


def _out_ln_loss(yg, wout, x, g, b, target, *, tb, name):
    S = x.shape[0]
    tb = min(tb, S)

    def body(yg_ref, w_ref, x_ref, g_ref, b_ref, t_ref, dz_ref, dg_ref, db_ref, sq_ref):
        @pl.when(pl.program_id(0) == 0)
        def _():
            dg_ref[...] = jnp.zeros_like(dg_ref)
            db_ref[...] = jnp.zeros_like(db_ref)
            sq_ref[...] = jnp.zeros_like(sq_ref)

        o = jnp.dot(yg_ref[...], w_ref[...], preferred_element_type=F32)
        xhat, rstd = _ln_stats(ALPHA * x_ref[...] + o)
        err = xhat * g_ref[...] + b_ref[...] - t_ref[...]
        sq_ref[...] += jnp.sum(err * err, axis=0, keepdims=True)
        dy = err * (1.0 / D_MODEL)
        dz_ref[...] = _ln_bwd_math(dy, xhat, rstd, g_ref[...])
        dg_ref[...] += jnp.sum(dy * xhat, axis=0, keepdims=True)
        db_ref[...] += jnp.sum(dy, axis=0, keepdims=True)

    row = pl.BlockSpec((tb, D_MODEL), lambda i: (i, 0))
    vec = pl.BlockSpec((1, D_MODEL), lambda i: (0, 0))
    vshape = jax.ShapeDtypeStruct((1, D_MODEL), F32)
    return pl.pallas_call(
        body, name=name, grid=(S // tb,),
        in_specs=[pl.BlockSpec((tb, D_MIX), lambda i: (i, 0)), pl.BlockSpec((D_MIX, D_MODEL), lambda i: (0, 0)),
                  row, vec, vec, row],
        out_specs=[row, vec, vec, vec],
        out_shape=[jax.ShapeDtypeStruct((S, D_MODEL), F32), vshape, vshape, vshape],
        compiler_params=_cparams("arbitrary"),
    )(yg, wout, x, g, b, target)


def _ln_bwd(dy, xhat, rstd, g, *, tb, name):
    S = dy.shape[0]
    tb = min(tb, S)

    def body(dy_ref, xhat_ref, rstd_ref, g_ref, dz_ref, dg_ref, db_ref):
        @pl.when(pl.program_id(0) == 0)
        def _():
            dg_ref[...] = jnp.zeros_like(dg_ref)
            db_ref[...] = jnp.zeros_like(db_ref)

        dy_, xhat_ = dy_ref[...], xhat_ref[...]
        dz_ref[...] = _ln_bwd_math(dy_, xhat_, rstd_ref[...], g_ref[...])
        dg_ref[...] += jnp.sum(dy_ * xhat_, axis=0, keepdims=True)
        db_ref[...] += jnp.sum(dy_, axis=0, keepdims=True)

    row = pl.BlockSpec((tb, D_MODEL), lambda i: (i, 0))
    vec = pl.BlockSpec((1, D_MODEL), lambda i: (0, 0))
    return pl.pallas_call(
        body, name=name, grid=(S // tb,),
        in_specs=[row, row, pl.BlockSpec((tb, 1), lambda i: (i, 0)), vec],
        out_specs=[row, vec, vec],
        out_shape=[jax.ShapeDtypeStruct((S, D_MODEL), F32), jax.ShapeDtypeStruct((1, D_MODEL), F32),
                   jax.ShapeDtypeStruct((1, D_MODEL), F32)],
        compiler_params=_cparams("arbitrary"),
    )(dy, xhat, rstd, g)


def _gate_fwd(ysrc, scale, h, ymem, *, tb, name):
    S = ysrc.shape[0]
    tb = min(tb, S)

    def body(ys_ref, sc_ref, ga_ref, gb_ref, gc_ref, ym_ref, yg_ref):
        ymain = ys_ref[...] * sc_ref[...]
        for k, g_ref in enumerate((ga_ref, gb_ref)):
            gv = g_ref[...].astype(F32)
            yg_ref[:, 512 * k:512 * (k + 1)] = (ymain[:, 512 * k:512 * (k + 1)] * gv * _sigmoid(gv)).astype(BF16)
        gv = gc_ref[...].astype(F32)
        yg_ref[:, 1024:1536] = (ym_ref[...] * gv * _sigmoid(gv)).astype(BF16)

    slab = lambda c: pl.BlockSpec((tb, 512), lambda i, c=c: (i, c))
    return pl.pallas_call(
        body, name=name, grid=(S // tb,),
        in_specs=[pl.BlockSpec((tb, D_MAIN), lambda i: (i, 0)), pl.BlockSpec((1, D_MAIN), lambda i: (0, 0)),
                  slab(3), slab(4), slab(5), pl.BlockSpec((tb, D_MEM), lambda i: (i, 0))],
        out_specs=pl.BlockSpec((tb, D_MIX), lambda i: (i, 0)),
        out_shape=jax.ShapeDtypeStruct((S, D_MIX), BF16),
        compiler_params=_cparams("parallel"),
    )(ysrc, scale, h, h, h, ymem)


def _gate_bwd(dz, wout, ysrc, scale, h, ymem, *, tb, name, fox=False):
    S = ysrc.shape[0]
    tb = min(tb, S)

    def dsilu(gv):
        sg = _sigmoid(gv)
        return sg, sg * (1.0 + gv * (1.0 - sg))

    def body(dz_ref, w_ref, ys_ref, sc_ref, ga_ref, gb_ref, gc_ref, ym_ref, dym_ref, dymem_ref, dh_ref):
        dyg = lax.dot_general(dz_ref[...].astype(BF16), w_ref[...], NT, preferred_element_type=F32)
        ymain = ys_ref[...] * sc_ref[...]
        lane = lax.broadcasted_iota(jnp.int32, (tb, LANES), 1)
        first = lane < FOX_HEAD_DIM
        for k, g_ref in enumerate((ga_ref, gb_ref)):
            gv, d = g_ref[...].astype(F32), dyg[:, 512 * k:512 * (k + 1)]
            sg, ds = dsilu(gv)
            dy = d * gv * sg
            dh_ref[:, 512 * k:512 * (k + 1)] = (d * ymain[:, 512 * k:512 * (k + 1)] * ds).astype(BF16)
            if not fox:
                dym_ref[:, 512 * k:512 * (k + 1)] = dy
                continue
            for q in range(512 // LANES):
                cols = slice(LANES * q, LANES * (q + 1))
                dy2 = dy[:, cols]
                prod = dy2 * ymain[:, 512 * k + LANES * q:512 * k + LANES * (q + 1)]
                for hh in range(2):
                    delta = jnp.sum(jnp.where(first == (hh == 0), prod, 0.0), axis=1, keepdims=True)
                    dyh = dy2 if hh == 0 else _swap_halves(dy2)
                    c0 = LANES * (2 * (4 * k + q) + hh)
                    dym_ref[:, c0:c0 + LANES] = jnp.where(
                        first, dyh, _lanes3(lane, AUX, _split3(-delta), 0.0)).astype(BF16)
        gv, d = gc_ref[...].astype(F32), dyg[:, 1024:1536]
        sg, ds = dsilu(gv)
        dymem_ref[...] = d * gv * sg
        dh_ref[:, 1024:1536] = (d * ym_ref[...] * ds).astype(BF16)

    slab = lambda c: pl.BlockSpec((tb, 512), lambda i, c=c: (i, c))
    return pl.pallas_call(
        body, name=name, grid=(S // tb,),
        in_specs=[pl.BlockSpec((tb, D_MODEL), lambda i: (i, 0)), pl.BlockSpec((D_MIX, D_MODEL), lambda i: (0, 0)),
                  pl.BlockSpec((tb, D_MAIN), lambda i: (i, 0)), pl.BlockSpec((1, D_MAIN), lambda i: (0, 0)),
                  slab(3), slab(4), slab(5), pl.BlockSpec((tb, D_MEM), lambda i: (i, 0))],
        out_specs=[pl.BlockSpec((tb, 2 * D_MAIN if fox else D_MAIN), lambda i: (i, 0)),
                   pl.BlockSpec((tb, D_MEM), lambda i: (i, 0)), pl.BlockSpec((tb, D_MIX), lambda i: (i, 1))],
        out_shape=[jax.ShapeDtypeStruct((S, 2 * D_MAIN), BF16) if fox else jax.ShapeDtypeStruct((S, D_MAIN), F32),
                   jax.ShapeDtypeStruct((S, D_MEM), F32), jax.ShapeDtypeStruct((S, D_IN), BF16)],
        compiler_params=_cparams("parallel"),
    )(dz, wout, ysrc, scale, h, h, h, ymem)


def _window_count(t0, rows, w):
    t = t0 + lax.broadcasted_iota(jnp.int32, (rows, POOL_GROUP), 0)
    return jnp.minimum(t + 1, w).astype(F32)


def _pool_fwd(h, pw, *, tb, name):
    S = h.shape[0]
    tb = min(tb, S)

    def body(u_ref, pw_ref, pm_ref, mixed_ref, tail_ref):
        i = pl.program_id(0)

        @pl.when(i == 0)
        def _():
            tail_ref[...] = jnp.zeros_like(tail_ref)

        u = u_ref[...].astype(F32)
        xfull = jnp.concatenate([tail_ref[...], u], axis=0)
        for gi, w in enumerate(POOL_WINDOWS):
            cols = slice(POOL_GROUP * gi, POOL_GROUP * (gi + 1))
            s = xfull[:, cols]
            sh = 1
            while sh < w:
                s = s + pltpu.roll(s, sh, 0)
                sh *= 2
            pm = s[POOL_HALO:, :] / _window_count(i * tb, tb, w) - u[:, cols]
            pmb = pm.astype(BF16)
            pm_ref[:, cols] = pmb
            mixed_ref[:, cols] = jnp.dot(pmb, pw_ref[gi], preferred_element_type=F32)
        tail_ref[...] = u[tb - POOL_HALO:, :]

    return pl.pallas_call(
        body, name=name, grid=(S // tb,),
        in_specs=[pl.BlockSpec((tb, D_MAIN), lambda i: (i, 0)),
                  pl.BlockSpec((4, POOL_GROUP, POOL_GROUP), lambda i: (0, 0, 0))],
        out_specs=[pl.BlockSpec((tb, D_MAIN), lambda i: (i, 0)), pl.BlockSpec((tb, D_MAIN), lambda i: (i, 0))],
        out_shape=[jax.ShapeDtypeStruct((S, D_MAIN), BF16), jax.ShapeDtypeStruct((S, D_MAIN), F32)],
        scratch_shapes=[pltpu.VMEM((POOL_HALO, D_MAIN), F32)],
        compiler_params=_cparams("arbitrary"),
    )(h, pw)


def _pool_bwd(dymain, pm, mixed, pw, scale, dh, *, tb, name):
    S = dymain.shape[0]
    tb = min(tb, S)
    nb = S // tb
    n = tb + POOL_HALO

    def body(dy_ref, pm_ref, mixed_ref, pw_ref, sc_ref, dh_in, dh_ref, dpw_ref, dsc_ref, head_ref, dpw_acc):
        del dh_in
        i = pl.program_id(0)

        @pl.when(i == 0)
        def _():
            head_ref[...] = jnp.zeros_like(head_ref)
            dpw_acc[...] = jnp.zeros_like(dpw_acc)
            dsc_ref[...] = jnp.zeros_like(dsc_ref)

        dy = dy_ref[...]
        dsc_ref[...] += jnp.sum(dy * mixed_ref[...], axis=0, keepdims=True)
        dmixed = dy * sc_ref[...]
        t0 = (nb - 1 - i) * tb
        for gi, w in enumerate(POOL_WINDOWS):
            cols = slice(POOL_GROUP * gi, POOL_GROUP * (gi + 1))
            dm = dmixed[:, cols].astype(BF16)
            dpw_acc[gi] += lax.dot_general(pm_ref[:, cols], dm, TN, preferred_element_type=F32)
            dpm = lax.dot_general(dm, pw_ref[gi], NT, preferred_element_type=F32)
            e = dpm / _window_count(t0, tb, w)
            s = jnp.concatenate([e, head_ref[:, cols]], axis=0)
            sh = 1
            while sh < w:
                s = s + pltpu.roll(s, n - sh, 0)
                sh *= 2
            dh_ref[:, cols] = (s[:tb, :] - dpm).astype(BF16)
            head_ref[:, cols] = e[:POOL_HALO, :]

        @pl.when(i == nb - 1)
        def _():
            dpw_ref[...] = dpw_acc[...].astype(BF16)

    rev = lambda i: (nb - 1 - i, 0)
    return pl.pallas_call(
        body, name=name, grid=(nb,),
        in_specs=[pl.BlockSpec((tb, D_MAIN), rev), pl.BlockSpec((tb, D_MAIN), rev), pl.BlockSpec((tb, D_MAIN), rev),
                  pl.BlockSpec((4, POOL_GROUP, POOL_GROUP), lambda i: (0, 0, 0)),
                  pl.BlockSpec((1, D_MAIN), lambda i: (0, 0)), pl.BlockSpec(memory_space=pl.ANY)],
        out_specs=[pl.BlockSpec((tb, D_MAIN), rev),
                   pl.BlockSpec((4, POOL_GROUP, POOL_GROUP), lambda i: (0, 0, 0)),
                   pl.BlockSpec((1, D_MAIN), lambda i: (0, 0))],
        out_shape=[jax.ShapeDtypeStruct(dh.shape, dh.dtype),
                   jax.ShapeDtypeStruct((4, POOL_GROUP, POOL_GROUP), BF16), jax.ShapeDtypeStruct((1, D_MAIN), F32)],
        scratch_shapes=[pltpu.VMEM((POOL_HALO, D_MAIN), F32), pltpu.VMEM((4, POOL_GROUP, POOL_GROUP), F32)],
        input_output_aliases={5: 0},
        compiler_params=_cparams("arbitrary"),
    )(dymain, pm, mixed, pw, scale, dh)


MEM_SCALE = MEM_HEAD_DIM ** -0.5


def _mem_probs(q_ref, mkv_ref, hd):
    cols = slice(MEM_HEAD_DIM * hd, MEM_HEAD_DIM * (hd + 1))
    q = (q_ref[:, cols].astype(F32) * MEM_SCALE).astype(BF16)
    mk = mkv_ref[:, cols].astype(BF16)
    mv = mkv_ref[:, D_MEM + MEM_HEAD_DIM * hd:D_MEM + MEM_HEAD_DIM * (hd + 1)].astype(BF16)
    s = lax.dot_general(q, mk, NT, preferred_element_type=F32)
    e = jnp.exp(s - jnp.max(s, axis=1, keepdims=True))
    return cols, q, mk, mv, e, jnp.sum(e, axis=1, keepdims=True)


def _memattn_fwd(h, mkv, *, tb, name):
    S = h.shape[0]
    tb = min(tb, S)

    def body(q_ref, mkv_ref, y_ref):
        for hd in range(MEM_HEADS):
            cols, _, _, mv, e, l = _mem_probs(q_ref, mkv_ref, hd)
            y_ref[:, cols] = jnp.dot(e.astype(BF16), mv, preferred_element_type=F32) / l

    return pl.pallas_call(
        body, name=name, grid=(S // tb,),
        in_specs=[pl.BlockSpec((tb, D_MEM), lambda i: (i, 2)), pl.BlockSpec((N_MEM, 2 * D_MEM), lambda i: (0, 0))],
        out_specs=pl.BlockSpec((tb, D_MEM), lambda i: (i, 0)),
        out_shape=jax.ShapeDtypeStruct((S, D_MEM), F32),
        compiler_params=_cparams("parallel"),
    )(h, mkv)


def _memattn_bwd(h, mkv, dy, dh, *, tb, name):
    S = h.shape[0]
    tb = min(tb, S)

    def body(q_ref, mkv_ref, dy_ref, dh_in, dh_ref, dmkv_ref):
        del dh_in

        @pl.when(pl.program_id(0) == 0)
        def _():
            dmkv_ref[...] = jnp.zeros_like(dmkv_ref)

        for hd in range(MEM_HEADS):
            cols, q, mk, mv, e, l = _mem_probs(q_ref, mkv_ref, hd)
            p = e / l
            dyh = dy_ref[:, cols].astype(BF16)
            dp = lax.dot_general(dyh, mv, NT, preferred_element_type=F32)
            ds = p * (dp - jnp.sum(dp * p, axis=1, keepdims=True))
            dsb = ds.astype(BF16)
            dh_ref[:, cols] = (jnp.dot(dsb, mk, preferred_element_type=F32) * MEM_SCALE).astype(BF16)
            dmkv_ref[:, cols] += lax.dot_general(dsb, q, TN, preferred_element_type=F32)
            vcols = slice(D_MEM + MEM_HEAD_DIM * hd, D_MEM + MEM_HEAD_DIM * (hd + 1))
            dmkv_ref[:, vcols] += lax.dot_general(p.astype(BF16), dyh, TN, preferred_element_type=F32)

    return pl.pallas_call(
        body, name=name, grid=(S // tb,),
        in_specs=[pl.BlockSpec((tb, D_MEM), lambda i: (i, 2)), pl.BlockSpec((N_MEM, 2 * D_MEM), lambda i: (0, 0)),
                  pl.BlockSpec((tb, D_MEM), lambda i: (i, 0)), pl.BlockSpec(memory_space=pl.ANY)],
        out_specs=[pl.BlockSpec((tb, D_MEM), lambda i: (i, 2)), pl.BlockSpec((N_MEM, 2 * D_MEM), lambda i: (0, 0))],
        out_shape=[jax.ShapeDtypeStruct(dh.shape, dh.dtype), jax.ShapeDtypeStruct((N_MEM, 2 * D_MEM), F32)],
        input_output_aliases={3: 0},
        compiler_params=_cparams("arbitrary"),
    )(h, mkv, dy, dh)


def _forget_fwd(fl, bias, *, tb, name):
    S = fl.shape[0]
    tb = min(tb, S)

    def body(fl_ref, b_ref, o_ref, carry_ref):
        @pl.when(pl.program_id(0) == 0)
        def _():
            carry_ref[...] = jnp.zeros_like(carry_ref)

        z = fl_ref[...] + b_ref[...]
        lf = jnp.minimum(z, 0.0) - jnp.log(1.0 + jnp.exp(-jnp.abs(z)))
        row = lax.broadcasted_iota(jnp.int32, (tb, LANES), 0)
        c = lf
        sh = 1
        while sh < tb:
            c = c + jnp.where(row >= sh, pltpu.roll(c, sh, 0), 0.0)
            sh *= 2
        o_ref[...] = -(carry_ref[...] + c)
        carry_ref[...] += jnp.sum(lf, axis=0, keepdims=True)

    return pl.pallas_call(
        body, name=name, grid=(S // tb,),
        in_specs=[pl.BlockSpec((tb, LANES), lambda i: (i, 0)), pl.BlockSpec((1, LANES), lambda i: (0, 0))],
        out_specs=pl.BlockSpec((tb, LANES), lambda i: (i, 0)),
        out_shape=jax.ShapeDtypeStruct((S, LANES), F32),
        scratch_shapes=[pltpu.VMEM((1, LANES), F32)],
        compiler_params=_cparams("arbitrary"),
    )(fl, bias)


def _forget_bwd(dn, drow, fl, bias, *, tb, name):
    S = fl.shape[0]
    tb = min(tb, S)
    nb = S // tb

    def body(dn_ref, dr_ref, fl_ref, b_ref, dh_ref, db_ref, carry_ref):
        @pl.when(pl.program_id(0) == 0)
        def _():
            carry_ref[...] = jnp.zeros_like(carry_ref)
            db_ref[...] = jnp.zeros_like(db_ref)

        src = lax.broadcasted_iota(jnp.int32, (D_MAIN, LANES), 0)
        head = lax.broadcasted_iota(jnp.int32, (D_MAIN, LANES), 1)
        pick = lambda off: jnp.where((src == FOX_HEAD_DIM * head + off) & (head < FOX_HEADS), 1.0, 0.0).astype(BF16)
        hdot = lambda a, sel: sum(jnp.dot(part.astype(BF16), sel, preferred_element_type=F32) for part in _split3(a))
        dcum = hdot(dr_ref[...], pick(3)) - hdot(dn_ref[...], pick(0))
        row = lax.broadcasted_iota(jnp.int32, (tb, LANES), 0)
        c = dcum
        sh = 1
        while sh < tb:
            c = c + jnp.where(row < tb - sh, pltpu.roll(c, tb - sh, 0), 0.0)
            sh *= 2
        dlf = carry_ref[...] + c
        carry_ref[...] += jnp.sum(dcum, axis=0, keepdims=True)
        z = fl_ref[...] + b_ref[...]
        lane = lax.broadcasted_iota(jnp.int32, (tb, LANES), 1)
        dfl = jnp.where(lane < FOX_HEADS, dlf / (1.0 + jnp.exp(z)), 0.0)
        db_ref[...] += jnp.sum(dfl, axis=0, keepdims=True)
        dh_ref[...] = dfl.astype(BF16)

    rev = lambda i: (nb - 1 - i, 0)
    return pl.pallas_call(
        body, name=name, grid=(nb,),
        in_specs=[pl.BlockSpec((tb, D_MAIN), rev), pl.BlockSpec((tb, D_MAIN), rev), pl.BlockSpec((tb, LANES), rev),
                  pl.BlockSpec((1, LANES), lambda i: (0, 0))],
        out_specs=[pl.BlockSpec((tb, LANES), rev), pl.BlockSpec((1, LANES), lambda i: (0, 0))],
        out_shape=[jax.ShapeDtypeStruct((S, LANES), BF16), jax.ShapeDtypeStruct((1, LANES), F32)],
        scratch_shapes=[pltpu.VMEM((1, LANES), F32)],
        compiler_params=_cparams("arbitrary"),
    )(dn, drow, fl, bias)


FOX_SCALE = FOX_HEAD_DIM ** -0.5
LOG2E = 1.4426950408889634
LN2 = 0.6931471805599453
AUX = FOX_HEAD_DIM


def _split3(x):
    hi = x.astype(BF16).astype(F32)
    r = x - hi
    mid = r.astype(BF16).astype(F32)
    return hi, mid, (r - mid).astype(BF16).astype(F32)


def _lanes3(lane, base, parts, rest):
    return jnp.where(lane == base, parts[0], jnp.where(lane == base + 1, parts[1],
                                                       jnp.where(lane == base + 2, parts[2], rest)))


def _swap_halves(x):
    return pltpu.roll(x, FOX_HEAD_DIM, 1)


def _causal_steps(nq, keys_outer):
    if keys_outer:
        pairs = [(i, j) for j in range(nq) for i in range(j, nq)]
    else:
        pairs = [(i, j) for i in range(nq) for j in range(i + 1)]
    it, jt = zip(*pairs)
    return jnp.asarray(np.array(it, np.int32)), jnp.asarray(np.array(jt, np.int32))


def _in_proj_fox(x, w, *, tm, name):
    S = x.shape[0]
    tm = min(tm, S)

    def body(x_ref, w_ref, h_ref, qa_ref):
        acc = jnp.dot(x_ref[...].astype(BF16), w_ref[...], preferred_element_type=F32)
        h_ref[...] = acc.astype(BF16)
        lane = lax.broadcasted_iota(jnp.int32, (tm, LANES), 1)
        first = lane < FOX_HEAD_DIM
        ones_q = jnp.where((lane >= AUX) & (lane < AUX + 3), 1.0, 0.0)
        for g in range(FOX_PAIRS):
            q = acc[:, LANES * g:LANES * (g + 1)] * (FOX_SCALE * LOG2E)
            qa_ref[:, 2 * LANES * g:2 * LANES * g + LANES] = jnp.where(first, q, ones_q).astype(BF16)
            qa_ref[:, 2 * LANES * g + LANES:2 * LANES * (g + 1)] = jnp.where(first, _swap_halves(q), ones_q).astype(BF16)

    return pl.pallas_call(
        body, name=name, grid=(S // tm,),
        in_specs=[pl.BlockSpec((tm, D_MODEL), lambda i: (i, 0)), pl.BlockSpec((D_MODEL, D_IN), lambda i: (0, 0))],
        out_specs=[pl.BlockSpec((tm, D_IN), lambda i: (i, 0)), pl.BlockSpec((tm, 2 * D_MAIN), lambda i: (i, 0))],
        out_shape=[jax.ShapeDtypeStruct((S, D_IN), BF16), jax.ShapeDtypeStruct((S, 2 * D_MAIN), BF16)],
        compiler_params=_cparams("parallel"),
    )(x, w)


def _kv_proj_fox(x, wkv, negcum, *, tm, name):
    S = x.shape[0]
    tm = min(tm, S)

    def body(x_ref, w_ref, nc_ref, ka_ref, va_ref):
        acc = jnp.dot(x_ref[...].astype(BF16), w_ref[...], preferred_element_type=F32)
        nc = nc_ref[...]
        lane = lax.broadcasted_iota(jnp.int32, (tm, LANES), 1)
        first = lane < FOX_HEAD_DIM
        ones_k = jnp.where((lane >= AUX + 3) & (lane < AUX + 6), 1.0, 0.0)
        for g in range(FOX_PAIRS):
            k = acc[:, LANES * g:LANES * (g + 1)]
            v = acc[:, D_MAIN + LANES * g:D_MAIN + LANES * (g + 1)]
            for hh in range(2):
                sl = slice(LANES * (2 * g + hh), LANES * (2 * g + hh + 1))
                kh, vh = (k, v) if hh == 0 else (_swap_halves(k), _swap_halves(v))
                ncol = jnp.sum(jnp.where(lane == 2 * g + hh, nc, 0.0), axis=1, keepdims=True) * LOG2E
                ka_ref[:, sl] = jnp.where(first, kh, _lanes3(lane, AUX, _split3(ncol), ones_k)).astype(BF16)
                va_ref[:, sl] = jnp.where(first, vh, 1.0).astype(BF16)

    out = pl.BlockSpec((tm, 2 * D_MAIN), lambda i: (i, 0))
    shp = jax.ShapeDtypeStruct((S, 2 * D_MAIN), BF16)
    return pl.pallas_call(
        body, name=name, grid=(S // tm,),
        in_specs=[pl.BlockSpec((tm, D_MODEL), lambda i: (i, 0)), pl.BlockSpec((D_MODEL, 2 * D_MAIN), lambda i: (0, 0)),
                  pl.BlockSpec((tm, LANES), lambda i: (i, 0))],
        out_specs=[out, out], out_shape=[shp, shp],
        compiler_params=_cparams("parallel"),
    )(x, wkv, negcum)


def _dwkv(xt, dk, dv, dfl, *, tk, name):
    S = xt.shape[1]
    tk = min(tk, S)

    def body(x_ref, dk_ref, dv_ref, df_ref, o_ref):
        @pl.when(pl.program_id(0) == 0)
        def _():
            o_ref[...] = jnp.zeros_like(o_ref)

        for b_ref, c0 in ((dk_ref, 0), (dv_ref, D_MAIN), (df_ref, 2 * D_MAIN)):
            o_ref[:, c0:c0 + b_ref.shape[1]] += jnp.dot(x_ref[...], b_ref[...], preferred_element_type=F32)

    row = lambda n: pl.BlockSpec((tk, n), lambda k: (k, 0))
    return pl.pallas_call(
        body, name=name, grid=(S // tk,),
        in_specs=[pl.BlockSpec((D_MODEL, tk), lambda k: (0, k)), row(D_MAIN), row(D_MAIN), row(LANES)],
        out_specs=pl.BlockSpec((D_MODEL, 2 * D_MAIN + LANES), lambda k: (0, 0)),
        out_shape=jax.ShapeDtypeStruct((D_MODEL, 2 * D_MAIN + LANES), F32),
        compiler_params=_cparams("arbitrary"),
    )(xt, dk, dv, dfl)


def _dx1(dh, win, dk, dv, dfl, wkv, dz, *, tm, name):
    S = dh.shape[0]
    tm = min(tm, S)

    def body(dh_ref, win_ref, dk_ref, dv_ref, df_ref, wk_ref, wv_ref, wf_ref, dz_ref, o_ref):
        acc = ALPHA * dz_ref[...]
        for a_ref, b_ref in ((dh_ref, win_ref), (dk_ref, wk_ref), (dv_ref, wv_ref), (df_ref, wf_ref)):
            acc = acc + lax.dot_general(a_ref[...], b_ref[...], NT, preferred_element_type=F32)
        o_ref[...] = acc

    row = lambda n: pl.BlockSpec((tm, n), lambda i: (i, 0))
    wcols = lambda n, c: pl.BlockSpec((D_MODEL, n), lambda i, c=c: (0, c))
    return pl.pallas_call(
        body, name=name, grid=(S // tm,),
        in_specs=[row(D_IN), wcols(D_IN, 0), row(D_MAIN), row(D_MAIN), row(LANES),
                  wcols(D_MAIN, 0), wcols(D_MAIN, 1), wcols(LANES, 2 * D_MAIN // LANES), row(D_MODEL)],
        out_specs=row(D_MODEL), out_shape=jax.ShapeDtypeStruct((S, D_MODEL), F32),
        compiler_params=_cparams("parallel"),
    )(dh, win, dk, dv, dfl, wkv, wkv, wkv, dz)


def _fox_fwd(qa, ka, va, *, tq, name):
    S = qa.shape[0]
    tq = min(tq, S)
    nq = S // tq
    half = tq // 2
    it, jt = _causal_steps(nq, keys_outer=False)

    def body(it_ref, jt_ref, qa_ref, ka_ref, va_ref, y_ref, qb_ref, m_ref, acc_ref):
        n = pl.program_id(1)
        i, j = it_ref[n], jt_ref[n]
        first = lax.broadcasted_iota(jnp.int32, (tq, LANES), 1) < FOX_HEAD_DIM

        @pl.when(j == 0)
        def _():
            m_ref[...] = jnp.full_like(m_ref, NEG)
            acc_ref[...] = jnp.zeros_like(acc_ref)

        def update(hh, rows, nk, masked):
            sl = slice(LANES * hh, LANES * (hh + 1))
            s = lax.dot_general(qa_ref[rows, sl], ka_ref[0:nk, sl], NT, preferred_element_type=F32)
            if masked:
                r = lax.broadcasted_iota(jnp.int32, s.shape, 0) + rows.start
                c = lax.broadcasted_iota(jnp.int32, s.shape, 1)
                s = jnp.where(c <= r, s, NEG)
            m_prev = m_ref[hh, rows]
            m_new = jnp.maximum(m_prev, jnp.max(s, axis=1, keepdims=True))
            p = jnp.exp2(s - jnp.tile(m_new, (1, nk // LANES))).astype(BF16)
            acc_ref[hh, rows] = jnp.exp2(m_prev - m_new) * acc_ref[hh, rows] + jnp.dot(
                p, va_ref[0:nk, sl], preferred_element_type=F32)
            m_ref[hh, rows] = m_new

        @pl.when(j < i)
        def _():
            for hh in range(2):
                update(hh, slice(0, tq), tq, False)

        @pl.when(j == i)
        def _():
            for hh in range(2):
                for r0 in range(0, tq, half):
                    update(hh, slice(r0, r0 + half), r0 + half, True)
            lane = lax.broadcasted_iota(jnp.int32, (tq, LANES), 1)
            ys = []
            for hh in range(2):
                sl = slice(LANES * hh, LANES * (hh + 1))
                a = acc_ref[hh]
                denom = _swap_halves(a)
                ys.append(a / denom)
                lse2 = m_ref[hh] + jnp.log(jnp.where(first, denom, a)) * LOG2E
                qb_ref[:, sl] = _lanes3(lane, AUX + 3, _split3(-lse2), qa_ref[:, sl].astype(F32)).astype(BF16)
            y_ref[...] = jnp.where(first, ys[0], _swap_halves(ys[1]))

    qblock = pl.BlockSpec((tq, 2 * LANES), lambda g, n, it, jt: (it[n], g))
    kblock = pl.BlockSpec((tq, 2 * LANES), lambda g, n, it, jt: (jt[n], g))
    return pl.pallas_call(
        body, name=name,
        grid_spec=pltpu.PrefetchScalarGridSpec(
            num_scalar_prefetch=2, grid=(FOX_PAIRS, it.shape[0]),
            in_specs=[qblock, kblock, kblock],
            out_specs=[pl.BlockSpec((tq, LANES), lambda g, n, it, jt: (it[n], g)), qblock],
            scratch_shapes=[pltpu.VMEM((2, tq, LANES), F32), pltpu.VMEM((2, tq, LANES), F32)]),
        out_shape=[jax.ShapeDtypeStruct((S, D_MAIN), F32), jax.ShapeDtypeStruct((S, 2 * D_MAIN), BF16)],
        compiler_params=_cparams("parallel", "arbitrary"),
    )(it, jt, qa, ka, va)


def _fox_bwd(qb, ka, va, dya, dh, *, tq, name):
    S = qb.shape[0]
    tq = min(tq, S)
    nq = S // tq
    half = tq // 2
    it, jt = _causal_steps(nq, keys_outer=True)
    nsteps = it.shape[0]

    def body(it_ref, jt_ref, qb_ref, ka_ref, va_ref, dya_ref, dh_in, dq_ref, dk_ref, dv_ref, dn_ref, drow_ref,
             dq_acc, dk_acc, dv_acc):
        del dh_in
        n = pl.program_id(1)
        i, j = it_ref[n], jt_ref[n]
        first = lax.broadcasted_iota(jnp.int32, (tq, LANES), 1) < FOX_HEAD_DIM

        @pl.when(n == 0)
        def _():
            dq_acc[...] = jnp.zeros_like(dq_acc)

        @pl.when(i == j)
        def _():
            dk_acc[...] = jnp.zeros_like(dk_acc)
            dv_acc[...] = jnp.zeros_like(dv_acc)

        def update(hh, keys, q0, masked):
            sl = slice(LANES * hh, LANES * (hh + 1))
            qbh, kah, dyah = qb_ref[q0:tq, sl], ka_ref[keys, sl], dya_ref[q0:tq, sl]
            eT = lax.dot_general(kah, qbh, NT, preferred_element_type=F32)
            if masked:
                r = lax.broadcasted_iota(jnp.int32, eT.shape, 0) + keys.start
                c = lax.broadcasted_iota(jnp.int32, eT.shape, 1) + q0
                eT = jnp.where(r <= c, eT, NEG)
            pT = jnp.exp2(eT)
            dsT = pT * lax.dot_general(va_ref[keys, sl], dyah, NT, preferred_element_type=F32)
            dsb = dsT.astype(BF16)
            dv_acc[hh, keys] += jnp.dot(pT.astype(BF16), dyah, preferred_element_type=F32)
            dk_acc[hh, keys] += jnp.dot(dsb, qbh, preferred_element_type=F32)
            rows = pl.ds(pl.multiple_of(i * tq + q0, half), tq - q0)
            dq_acc[hh, rows, :] += lax.dot_general(dsb, kah, TN, preferred_element_type=F32)

        @pl.when(i > j)
        def _():
            for hh in range(2):
                update(hh, slice(0, tq), 0, False)

        @pl.when(i == j)
        def _():
            for hh in range(2):
                for k0 in range(0, tq, half):
                    update(hh, slice(k0, k0 + half), k0, True)

        @pl.when(i == nq - 1)
        def _():
            dk_ref[...] = (jnp.where(first, dk_acc[0], _swap_halves(dk_acc[1])) * LN2).astype(BF16)
            dv_ref[...] = jnp.where(first, dv_acc[0], _swap_halves(dv_acc[1])).astype(BF16)
            dn_ref[...] = jnp.where(first, _swap_halves(dk_acc[0]), dk_acc[1])

        @pl.when(n == nsteps - 1)
        def _():
            first_s = lax.broadcasted_iota(jnp.int32, (S, LANES), 1) < FOX_HEAD_DIM
            dq_ref[...] = (jnp.where(first_s, dq_acc[0], _swap_halves(dq_acc[1])) * FOX_SCALE).astype(BF16)
            drow_ref[...] = jnp.where(first_s, _swap_halves(dq_acc[0]), dq_acc[1])

    qblock = pl.BlockSpec((tq, 2 * LANES), lambda g, n, it, jt: (it[n], g))
    kblock = pl.BlockSpec((tq, 2 * LANES), lambda g, n, it, jt: (jt[n], g))
    whole = pl.BlockSpec((S, LANES), lambda g, n, it, jt: (0, g))
    kout = pl.BlockSpec((tq, LANES), lambda g, n, it, jt: (jt[n], g))
    return pl.pallas_call(
        body, name=name,
        grid_spec=pltpu.PrefetchScalarGridSpec(
            num_scalar_prefetch=2, grid=(FOX_PAIRS, nsteps),
            in_specs=[qblock, kblock, kblock, qblock, pl.BlockSpec(memory_space=pl.ANY)],
            out_specs=[whole, kout, kout, kout, whole],
            scratch_shapes=[pltpu.VMEM((2, S, LANES), F32), pltpu.VMEM((2, tq, LANES), F32),
                            pltpu.VMEM((2, tq, LANES), F32)]),
        out_shape=[jax.ShapeDtypeStruct(dh.shape, dh.dtype), jax.ShapeDtypeStruct((S, D_MAIN), BF16),
                   jax.ShapeDtypeStruct((S, D_MAIN), BF16),
                   jax.ShapeDtypeStruct((S, D_MAIN), F32), jax.ShapeDtypeStruct((S, D_MAIN), F32)],
        input_output_aliases={6: 0},
        compiler_params=_cparams("parallel", "arbitrary"),
    )(it, jt, qb, ka, va, dya, dh)


TB_ROWS = 512
TB_SEQ = 512
TK_DW_IN = 2048
TK_DW_OUT = 1024
TQ_FOX_FWD = 1024
TQ_FOX_BWD = 1024


def _local_step(x, mem, target, win0, pscale, ln_g, ln_b, bias, comm):
    ones = jnp.ones((1, D_MAIN), F32)
    g0, b0, g1, b1 = ln_g[0:1], ln_b[0:1], ln_g[1:2], ln_b[1:2]
    mm = lambda a, b, mode, dt, tm, tn, tk, name, **kw: _mm(a, b, mode=mode, out_dtype=dt, tm=tm, tn=tn, tk=tk,
                                                            name=name, **kw)

    h0, xt = _in_proj_t(x, win0, tm=256, name="l0_in")
    wmkv, pw, wout0 = comm.next_weights(h0)
    pm, mixed = _pool_fwd(h0, pw, tb=TB_SEQ, name="l0_pool_fwd")
    mkv0 = mm(mem, wmkv[0], "nn", F32, 256, 1024, 1024, "l0_mkv")
    ymem0 = _memattn_fwd(h0, mkv0, tb=TB_SEQ, name="l0_mem_fwd")
    yg0 = _gate_fwd(mixed, pscale, h0, ymem0, tb=TB_ROWS, name="l0_gate_fwd")
    x1, xhat0, rstd0, x1t = _out_ln(yg0, wout0, x, g0, b0, tb=TB_SEQ, name="l0_out_ln")
    win1, wout1, wkv, wf = comm.late_weights(x1)

    fl = mm(x1, wf, "nn", F32, 512, LANES, D_MODEL, "f_proj")
    negcum = _forget_fwd(fl, bias, tb=TB_SEQ, name="forget_fwd")
    ka, va = _kv_proj_fox(x1, wkv, negcum, tm=512, name="kv_proj")

    h1, qa = _in_proj_fox(x1, win1, tm=256, name="l1_in")
    y1, qb = _fox_fwd(qa, ka, va, tq=TQ_FOX_FWD, name="fox_fwd")
    mkv1 = mm(mem, wmkv[1], "nn", F32, 256, 1024, 1024, "l1_mkv")
    ymem1 = _memattn_fwd(h1, mkv1, tb=TB_SEQ, name="l1_mem_fwd")
    yg1 = _gate_fwd(y1, ones, h1, ymem1, tb=TB_ROWS, name="l1_gate_fwd")
    dz1, dg1, db1, sq = _out_ln_loss(yg1, wout1, x1, g1, b1, target, tb=TB_SEQ, name="l1_out_ln_loss")

    dwout1 = mm(yg1, dz1, "tn", BF16, D_MIX, D_MODEL, TK_DW_OUT, "l1_dwout")
    dya, dymem1, dh1 = _gate_bwd(dz1, wout1, y1, ones, h1, ymem1, tb=TB_ROWS, name="l1_gate_bwd", fox=True)
    dh1, dk, dv, dnp, drowp = _fox_bwd(qb, ka, va, dya, dh1, tq=TQ_FOX_BWD, name="fox_bwd")
    dfl, dbias = _forget_bwd(dnp, drowp, fl, bias, tb=TB_SEQ, name="forget_bwd")
    dh1, dmkv1 = _memattn_bwd(h1, mkv1, dymem1, dh1, tb=TB_SEQ, name="l1_mem_bwd")
    dwmkv1 = mm(mem, dmkv1, "tn", BF16, D_MODEL, 1024, N_MEM, "l1_dwmkv")
    dwin1 = mm(x1t, dh1, "nn", BF16, D_MODEL, D_IN // 2, TK_DW_IN, "l1_dwin")
    dwkv = _dwkv(x1t, dk, dv, dfl, tk=TK_DW_OUT, name="dwkv")
    dwkv = dwkv[:, :2 * D_MAIN + FOX_HEADS].reshape(D_MODEL, N_DEV, -1).transpose(1, 0, 2).astype(BF16)
    anchor = comm.send_layer1(dict(w_out=dwout1, w_mem_kv=dwmkv1, w_in=dwin1, w_kv_shared=dwkv))
    dx1 = _dx1(dh1, win1, dk, dv, dfl, wkv, dz1, tm=256, name="l1_dx")

    dz0, dg0, db0 = _ln_bwd(dx1, xhat0, rstd0, g0 + anchor, tb=TB_ROWS, name="l0_ln_bwd")
    dwout0 = mm(yg0, dz0, "tn", BF16, D_MIX, D_MODEL, TK_DW_OUT, "l0_dwout")
    dy0, dymem0, dh0 = _gate_bwd(dz0, wout0, mixed, pscale, h0, ymem0, tb=TB_ROWS, name="l0_gate_bwd")
    comm.landed_layer1(dy0)
    dh0, dpw, dpscale = _pool_bwd(dy0, pm, mixed, pw, pscale, dh0, tb=TB_SEQ, name="l0_pool_bwd")
    dh0, dmkv0 = _memattn_bwd(h0, mkv0, dymem0, dh0, tb=TB_SEQ, name="l0_mem_bwd")
    dwmkv0 = mm(mem, dmkv0, "tn", BF16, D_MODEL, 1024, N_MEM, "l0_dwmkv")
    dh0 = comm.send_layer0_first(dict(w_out=dwout0, w_mem_kv=dwmkv0, pool_w=dpw), dh0)
    dwin0 = mm(xt, dh0, "nn", BF16, D_MODEL, D_IN // 2, TK_DW_IN, "l0_dwin")
    dz0 = comm.send_layer0_rest(dict(w_in=dwin0, pool_scale=dpscale, ln_g=jnp.concatenate([dg0, dg1]),
                                     ln_b=jnp.concatenate([db0, db1]), b_forget=dbias[0, :FOX_HEADS]), dz0)
    gx = mm(dh0, win0, "nt", F32, 256, D_MODEL, D_IN, "l0_dx", add=dz0, add_scale=ALPHA)
    return sq, gx


MESH_ID = pl.DeviceIdType.MESH
HBM = pl.BlockSpec(memory_space=pl.ANY)
SLICED = {"w_in": (2, D_IN // N_DEV), "w_mem_kv": (1, D_MODEL // N_DEV), "w_out": (1, D_MIX // N_DEV),
          "pool_w": (1, POOL_GROUP // N_DEV)}


def _place():
    return lax.axis_index("x"), lax.axis_index("y"), lax.axis_index("c")


def _slot(p):
    return 4 * p[0] + 2 * p[1] + p[2]


def _cut(ref, axis, width, s):
    idx = [slice(None)] * len(ref.shape)
    idx[axis] = pl.ds(s * width, width)
    return ref.at[tuple(idx)]


def _all_gather(shards, cuts, *, name):
    nt = len(shards)

    def full_shape(a, cut):
        if cut is None:
            return (N_DEV,) + a.shape
        return a.shape[:cut[0]] + (a.shape[cut[0]] * N_DEV,) + a.shape[cut[0] + 1:]

    def body(*refs):
        ins, outs = refs[:nt], refs[nt:2 * nt]
        send_sems, recv_sems, local_sems = refs[2 * nt:]
        x, y, c = _place()
        me, sibling = (x, y, c), (x, y, 1 - c)
        chips = [(1 - x, y), (x, 1 - y), (1 - x, 1 - y)]

        def place(t, s):
            return outs[t].at[s] if cuts[t] is None else _cut(outs[t], cuts[t][0], cuts[t][1], s)

        def copies(k, block, to, from_input=False):
            s = _slot(block)
            return [pltpu.make_async_remote_copy(
                src_ref=ins[t] if from_input else place(t, s), dst_ref=place(t, s),
                send_sem=send_sems.at[nt * k + t], recv_sem=recv_sems.at[nt * k + t],
                device_id=to, device_id_type=MESH_ID) for t in range(nt)]

        mine = [pltpu.make_async_copy(ins[t], place(t, _slot(me)), local_sems.at[t]) for t in range(nt)]
        for cp in mine:
            cp.start()
        first = [copies(0, me, sibling, True)] + [copies(1 + j, me, (*chip, c), True) for j, chip in enumerate(chips)]
        for group in first:
            for cp in group:
                cp.start()
        passed = [copies(4 + j, (*chip, c), sibling) for j, chip in enumerate(chips)]
        for j, chip in enumerate(chips):
            for cp in copies(1 + j, (*chip, c), me):
                cp.wait_recv()
            for cp in passed[j]:
                cp.start()
        for cp in copies(0, sibling, me):
            cp.wait_recv()
        for j, chip in enumerate(chips):
            for cp in copies(4 + j, (*chip, 1 - c), me):
                cp.wait_recv()
        for group in first + passed:
            for cp in group:
                cp.wait_send()
        for cp in mine:
            cp.wait()

    return pl.pallas_call(
        body, name=name, in_specs=[HBM] * nt, out_specs=[HBM] * nt,
        out_shape=[jax.ShapeDtypeStruct(full_shape(a, cut), a.dtype) for a, cut in zip(shards, cuts)],
        scratch_shapes=[pltpu.SemaphoreType.DMA((7 * nt,)), pltpu.SemaphoreType.DMA((7 * nt,)),
                        pltpu.SemaphoreType.DMA((nt,))],
    )(*shards)


def _exchange_copies(items, ins, outs, send_sems, recv_sems, local_sems, gather=False):
    nt = len(items)
    x, y, c = _place()
    me = _slot((x, y, c))
    flip = lambda v, bit: 1 - v if bit else v

    def part(ref, cut, s):
        return ref.at[s] if cut is None else _cut(ref, cut[0], cut[1], s)

    def src(t, s):
        return ins[t] if gather else part(ins[t], items[t][0], s)

    def dst(t, s):
        if gather:
            return part(outs[items[t][1]], items[t][0], s)
        d = outs[items[t][1]].at[s]
        return d if items[t][2] is None else d.at[items[t][2]]

    sends, arrivals = [], []
    for k in range(1, N_DEV):
        peer = (flip(x, k & 4), flip(y, k & 2), flip(c, k & 1))
        ps = _slot(peer)
        for t in range(nt):
            sems = dict(send_sem=send_sems.at[nt * (k - 1) + t], recv_sem=recv_sems.at[nt * (k - 1) + t],
                        device_id=peer, device_id_type=MESH_ID)
            sends.append(pltpu.make_async_remote_copy(src_ref=src(t, ps), dst_ref=dst(t, me), **sems))
            arrivals.append(pltpu.make_async_remote_copy(src_ref=src(t, ps), dst_ref=dst(t, ps), **sems))
    mine = [pltpu.make_async_copy(src(t, me), dst(t, me), local_sems.at[t]) for t in range(nt)]
    return sends, arrivals, mine


SEMS = pl.BlockSpec(memory_space=pltpu.SEMAPHORE)
SIDE_EFFECT = pltpu.SideEffectType.DATAFLOW_SIDE_EFFECTING


def _exchange_start(srcs, items, landings, *, name, gather=False, carry=()):
    nt, nl, nc = len(srcs), len(landings), len(carry)

    def body(*refs):
        ins, lands = refs[:nt], refs[nt:nt + nl]
        send_sems, recv_sems, local_sems = refs[nt + nl + nc:nt + nl + nc + 3]
        token = refs[-1]
        sends, _, mine = _exchange_copies(items, ins, lands, send_sems, recv_sems, local_sems, gather)
        for cp in sends + mine:
            cp.start()
        token[...] = jnp.zeros_like(token)

    hbm = lambda a: pltpu.HBM(a.shape, a.dtype)
    fresh = [pltpu.with_memory_space_constraint(
        lax.empty(l.shape, l.dtype) if isinstance(l, jax.ShapeDtypeStruct) else l, pltpu.HBM) for l in landings]
    res = pl.pallas_call(
        body, name=name, in_specs=[HBM] * (nt + nl + nc),
        out_specs=[SEMS, SEMS, SEMS] + [HBM] * (nt + nl + nc) + [pl.BlockSpec(memory_space=pltpu.VMEM)],
        out_shape=[pltpu.SemaphoreType.DMA((7 * nt,)), pltpu.SemaphoreType.DMA((7 * nt,)), pltpu.SemaphoreType.DMA((nt,))]
        + [hbm(a) for a in srcs] + [hbm(l) for l in landings] + [hbm(a) for a in carry]
        + [jax.ShapeDtypeStruct((8, LANES), F32)],
        input_output_aliases={i: 3 + i for i in range(nt + nl + nc)},
        compiler_params=pltpu.CompilerParams(has_side_effects=SIDE_EFFECT),
    )(*[pltpu.with_memory_space_constraint(a, pltpu.HBM) for a in srcs], *fresh,
      *[pltpu.with_memory_space_constraint(a, pltpu.HBM) for a in carry])
    return res[:3 + nt + nl], res[-1][0:1, 0:1], res[3 + nt + nl:-1]


def _exchange_wait(state, items, nt, after, *, name, gather=False):
    sems, bufs = state[:3], state[3:]
    nl = len(bufs) - nt

    def body(*refs):
        ins, lands = refs[:nt], refs[nt:nt + nl]
        send_sems, recv_sems, local_sems = refs[nt + nl:nt + nl + 3]
        sends, arrivals, mine = _exchange_copies(items, ins, lands, send_sems, recv_sems, local_sems, gather)
        for sent, landed in zip(sends, arrivals):
            landed.wait_recv()
            sent.wait_send()
        for cp in mine:
            cp.wait()

    hbm = lambda a: pltpu.HBM(a.shape, a.dtype)
    res = pl.pallas_call(
        body, name=name, in_specs=[HBM] * (nt + nl) + [SEMS, SEMS, SEMS, HBM], out_specs=[HBM] * (nt + nl),
        out_shape=[hbm(a) for a in bufs],
        input_output_aliases={i: i for i in range(nt + nl)},
        compiler_params=pltpu.CompilerParams(has_side_effects=SIDE_EFFECT),
    )(*bufs, *sems, after)
    return res[nt:]


def _adamw(recv, w, m, v, *, split, name):
    shape = w.shape
    axis, parts = split
    block = shape[:axis] + (shape[axis] // parts,) + shape[axis + 1:]
    nd = len(shape)

    def body(r_ref, w_ref, m_ref, v_ref, g_ref, d_ref, nm_ref, nv_ref):
        g = r_ref[0].astype(F32)
        for j in range(1, N_DEV):
            g = g + r_ref[j].astype(F32)
        nm = ADAM_B1 * m_ref[...] + (1.0 - ADAM_B1) * g
        nv = ADAM_B2 * v_ref[...] + (1.0 - ADAM_B2) * (g * g)
        m_hat = nm / (1.0 - ADAM_B1 ** ADAM_STEP)
        v_hat = nv / (1.0 - ADAM_B2 ** ADAM_STEP)
        g_ref[...] = g
        nm_ref[...] = nm
        nv_ref[...] = nv
        d_ref[...] = -ADAM_LR * (m_hat / (jnp.sqrt(v_hat) + ADAM_EPS) + ADAM_WD * w_ref[...])

    at = lambda i: tuple(i if a == axis else 0 for a in range(nd))
    one = pl.BlockSpec(block, at)
    shp = jax.ShapeDtypeStruct(shape, F32)
    return pl.pallas_call(
        body, name=name, grid=(parts,),
        in_specs=[pl.BlockSpec((N_DEV,) + block, lambda i: (0,) + at(i)), one, one, one],
        out_specs=[one, one, one, one], out_shape=[shp, shp, shp, shp],
        compiler_params=_cparams("parallel"),
    )(recv, w, m, v)


BIG = ("w_in", "w_mem_kv", "w_out", "pool_w", "w_kv_shared")
SMALL = ("pool_scale", "ln_g", "ln_b", "b_forget")
SMALL_ROWS = 40
ADAM_SPLIT = {"w_in": (1, 4), "w_mem_kv": (0, 2), "w_out": (0, 2), "pool_w": (0, 1), "w_kv_shared": (0, 4)}
PER_LAYER_CUT = {"w_out": (0, D_MIX // N_DEV), "w_mem_kv": (0, D_MODEL // N_DEV), "w_in": (1, D_IN // N_DEV),
                 "pool_w": (1, POOL_GROUP // N_DEV)}
EARLY = ("w_out", "w_mem_kv", "w_in", "w_kv_shared")
EARLY_ITEMS = [(PER_LAYER_CUT[n], i, 1) for i, n in enumerate(EARLY[:3])] + [(None, 3, None)]


def _flat(parts, rows):
    v = jnp.concatenate([p.reshape(-1) for p in parts])
    return jnp.pad(v, (0, rows * LANES - v.shape[0])).reshape(rows, LANES)


def _unflat(flat, shapes):
    v, out, off = flat.reshape(-1), [], 0
    for s in shapes:
        n = math.prod(s)
        out.append(v[off:off + n].reshape(s))
        off += n
    return out


def kernel(x, mem, w_in, w_mem_kv, w_out, ln_g, ln_b, pool_w, pool_scale, w_kv_shared, b_forget, loss_target, m_w_in, m_w_mem_kv, m_w_out, m_ln_g, m_ln_b, m_pool_w, m_pool_scale, m_w_kv_shared, m_b_forget, v_w_in, v_w_mem_kv, v_w_out, v_ln_g, v_ln_b, v_pool_w, v_pool_scale, v_w_kv_shared, v_b_forget):
    w = dict(w_in=w_in, w_mem_kv=w_mem_kv, w_out=w_out, ln_g=ln_g, ln_b=ln_b, pool_w=pool_w[0],
             pool_scale=pool_scale, w_kv_shared=w_kv_shared, b_forget=b_forget)
    m = dict(w_in=m_w_in, w_mem_kv=m_w_mem_kv, w_out=m_w_out, ln_g=m_ln_g, ln_b=m_ln_b, pool_w=m_pool_w[0],
             pool_scale=m_pool_scale, w_kv_shared=m_w_kv_shared, b_forget=m_b_forget)
    v = dict(w_in=v_w_in, w_mem_kv=v_w_mem_kv, w_out=v_w_out, ln_g=v_ln_g, ln_b=v_ln_b, pool_w=v_pool_w[0],
             pool_scale=v_pool_scale, w_kv_shared=v_w_kv_shared, b_forget=v_b_forget)

    wb = {n: w[n].astype(BF16) for n in BIG}
    bdt = wb["w_in"].dtype
    win0, pscale = _all_gather([wb["w_in"][0], jnp.pad(pool_scale, ((0, 7), (0, 0)))], [PER_LAYER_CUT["w_in"], None],
                               name="gather_weights")
    pscale = pscale[:, 0, :].reshape(1, D_MAIN)
    next_srcs = [wb["w_mem_kv"], wb["pool_w"], wb["w_out"][0]]
    next_items = [(SLICED["w_mem_kv"], 0, None), (SLICED["pool_w"], 1, None), (PER_LAYER_CUT["w_out"], 2, None)]
    next_state, _, (win0,) = _exchange_start(
        next_srcs, next_items,
        [jax.ShapeDtypeStruct((2, D_MODEL, D_MODEL), bdt), jax.ShapeDtypeStruct((4, POOL_GROUP, POOL_GROUP), bdt),
         jax.ShapeDtypeStruct((D_MIX, D_MODEL), bdt)], name="gather_next_start", gather=True, carry=[win0])
    late_srcs = [wb["w_in"][1], wb["w_out"][1], wb["w_kv_shared"]]
    late_items = [(PER_LAYER_CUT["w_in"], 0, None), (PER_LAYER_CUT["w_out"], 1, None), (None, 2, None)]
    late = {}
    bias = jnp.pad(b_forget, (0, LANES - FOX_HEADS)).reshape(1, LANES)

    def next_weights(h0):
        wmkv, pw, wout0 = _exchange_wait(next_state, next_items, len(next_srcs), h0, name="gather_next_wait", gather=True)
        late["state"], _, (pw,) = _exchange_start(
            late_srcs, late_items,
            [jax.ShapeDtypeStruct((D_MODEL, D_IN), bdt), jax.ShapeDtypeStruct((D_MIX, D_MODEL), bdt),
             jax.ShapeDtypeStruct((N_DEV,) + w_kv_shared.shape, bdt)], name="gather_late_start", gather=True, carry=[pw])
        return wmkv, pw, wout0

    def late_weights(x1):
        win1, wout1, wkv = _exchange_wait(late["state"], late_items, len(late_srcs), x1, name="gather_late_wait",
                                          gather=True)
        wkv = jnp.pad(wkv.transpose(1, 0, 2).reshape(D_MODEL, -1), ((0, 0), (0, LANES - FOX_HEADS)))
        return win1, wout1, wkv, wkv[:, 2 * D_MAIN:]

    sent, recv = {}, {}
    first_items = [(PER_LAYER_CUT["w_out"], 0, 0), (PER_LAYER_CUT["w_mem_kv"], 1, 0), (PER_LAYER_CUT["pool_w"], 2, None)]
    rest_items = [(PER_LAYER_CUT["w_in"], 0, 0), (None, 1, None)]

    def send_layer1(g):
        srcs = [g[n] for n in EARLY]
        lands = [jax.ShapeDtypeStruct((N_DEV, 2) + w[n].shape[1:], g[n].dtype) for n in EARLY[:3]]
        lands.append(jax.ShapeDtypeStruct(g["w_kv_shared"].shape, g["w_kv_shared"].dtype))
        sent["layer1"], anchor, _ = _exchange_start(srcs, EARLY_ITEMS, lands, name="exchange_early_start")
        return anchor

    def landed_layer1(after):
        recv["w_out"], recv["w_mem_kv"], recv["w_in"], recv["w_kv_shared"] = _exchange_wait(
            sent["layer1"], EARLY_ITEMS, len(EARLY), after, name="exchange_early_wait")

    def send_layer0_first(g, carry):
        srcs = [g["w_out"], g["w_mem_kv"], g["pool_w"]]
        sent["first"], _, (carry,) = _exchange_start(
            srcs, first_items,
            [recv["w_out"], recv["w_mem_kv"], jax.ShapeDtypeStruct((N_DEV,) + w["pool_w"].shape, srcs[2].dtype)],
            name="exchange_mid_start", carry=[carry])
        return carry

    def send_layer0_rest(g, carry):
        small = jnp.concatenate([g["pool_scale"].reshape(N_DEV, -1)]
                                + [jnp.broadcast_to(g[n].reshape(1, -1), (N_DEV, g[n].size)) for n in SMALL[1:]], axis=1)
        small = jnp.pad(small, ((0, 0), (0, SMALL_ROWS * LANES - small.shape[1]))).reshape(N_DEV, SMALL_ROWS, LANES)
        sent["rest"], _, (carry,) = _exchange_start(
            [g["w_in"], small], rest_items, [recv["w_in"], jax.ShapeDtypeStruct(small.shape, small.dtype)],
            name="exchange_last_start", carry=[carry])
        return carry

    comm = types.SimpleNamespace(next_weights=next_weights, late_weights=late_weights, send_layer1=send_layer1,
                                 landed_layer1=landed_layer1, send_layer0_first=send_layer0_first,
                                 send_layer0_rest=send_layer0_rest)
    sq, gx = _local_step(x[0], mem[0], loss_target[0], win0, pscale, ln_g, ln_b, bias, comm)
    loss = lax.psum((0.5 / D_MODEL) * jnp.sum(sq), ("x", "y", "c"))

    recv["w_out"], recv["w_mem_kv"], recv["pool_w"] = _exchange_wait(sent["first"], first_items, len(first_items), gx,
                                                                     name="exchange_mid_wait")
    recv["w_in"], recv["small"] = _exchange_wait(sent["rest"], rest_items, len(rest_items), gx,
                                                 name="exchange_last_wait")

    outs = {}
    for n in BIG:
        res = _adamw(recv[n], w[n], m[n], v[n], split=ADAM_SPLIT[n], name="adamw_" + n)
        for kind, a in zip(("grad", "delta", "new_m", "new_v"), res):
            outs[kind, n] = a[None] if n == "pool_w" else a
    small_shapes = [w[n].shape for n in SMALL]
    res = _adamw(recv["small"], _flat([w[n] for n in SMALL], SMALL_ROWS), _flat([m[n] for n in SMALL], SMALL_ROWS),
                 _flat([v[n] for n in SMALL], SMALL_ROWS), split=(0, 1), name="adamw_small")
    for kind, flat in zip(("grad", "delta", "new_m", "new_v"), res):
        for n, a in zip(SMALL, _unflat(flat, small_shapes)):
            outs[kind, n] = a
    order = ("w_in", "w_mem_kv", "w_out", "ln_g", "ln_b", "pool_w", "pool_scale", "w_kv_shared", "b_forget")
    return (loss, gx[None], *[outs[kind, n] for kind in ("grad", "delta", "new_m", "new_v") for n in order])
```

```python
import math
import types

import numpy as np
import jax
import jax.numpy as jnp
from jax import lax
from jax.experimental import pallas as pl
from jax.experimental.pallas import tpu as pltpu

F32 = jnp.float32
BF16 = jnp.bfloat16

D_MODEL = 1024
D_MAIN = 1024
D_MEM = 512
D_MIX = D_MAIN + D_MEM
D_IN = 2 * D_MIX
N_MEM = 256
MEM_HEADS = 4
MEM_HEAD_DIM = 128
FOX_HEADS = 16
FOX_HEAD_DIM = 64
FOX_PAIRS = FOX_HEADS // 2
POOL_WINDOWS = (2, 4, 8, 16)
POOL_GROUP = 256
POOL_HALO = 16
ALPHA = 4.0 ** 0.25
LN_EPS = 1e-5
NEG = -1e30
LANES = 128
N_DEV = 8

ADAM_LR = 0.001
ADAM_B1 = 0.9
ADAM_B2 = 0.999
ADAM_EPS = 1e-08
ADAM_WD = 0.01
ADAM_STEP = 10

VMEM_LIMIT = 56 * 1024 * 1024

NN = (((1,), (0,)), ((), ()))
NT = (((1,), (1,)), ((), ()))
TN = (((0,), (0,)), ((), ()))


def _cparams(*sem):
    return pltpu.CompilerParams(dimension_semantics=sem, vmem_limit_bytes=VMEM_LIMIT)


def _sigmoid(z):
    return 1.0 / (1.0 + jnp.exp(-z))


def _mm(a, b, *, mode, out_dtype, tm, tn, tk, name, add=None, add_scale=1.0):
    if mode == "nn":
        (M, K), (K2, N) = a.shape, b.shape
    elif mode == "nt":
        (M, K), (N, K2) = a.shape, b.shape
    else:
        (K, M), (K2, N) = a.shape, b.shape
    assert K == K2, (a.shape, b.shape, mode)
    tm, tn, tk = min(tm, M), min(tn, N), min(tk, K)
    assert M % tm == 0 and N % tn == 0 and K % tk == 0, (M, N, K, tm, tn, tk)
    gm, gn, gk = M // tm, N // tn, K // tk
    dims = {"nn": NN, "nt": NT, "tn": TN}[mode]
    if mode == "tn":
        a_spec = pl.BlockSpec((tk, tm), lambda i, j, k: (k, i))
    else:
        a_spec = pl.BlockSpec((tm, tk), lambda i, j, k: (i, k))
    if mode == "nt":
        b_spec = pl.BlockSpec((tn, tk), lambda i, j, k: (j, k))
    else:
        b_spec = pl.BlockSpec((tk, tn), lambda i, j, k: (k, j))
    o_spec = pl.BlockSpec((tm, tn), lambda i, j, k: (i, j))
    has_add = add is not None
    acc_in_out = out_dtype == F32

    def body(*refs):
        a_ref, b_ref = refs[0], refs[1]
        add_ref = refs[2] if has_add else None
        o_ref = refs[3] if has_add else refs[2]
        prod = lax.dot_general(a_ref[...].astype(BF16), b_ref[...].astype(BF16), dims,
                               preferred_element_type=F32)

        def finish(r):
            if has_add:
                r = r + add_scale * add_ref[...]
            o_ref[...] = r.astype(out_dtype)

        if gk == 1:
            finish(prod)
        else:
            acc_ref = o_ref if acc_in_out else refs[-1]
            k = pl.program_id(2)

            @pl.when(k == 0)
            def _():
                acc_ref[...] = prod

            @pl.when(k > 0)
            def _():
                acc_ref[...] += prod

            if has_add or not acc_in_out:
                @pl.when(k == gk - 1)
                def _():
                    finish(acc_ref[...])

    in_specs = [a_spec, b_spec] + ([o_spec] if has_add else [])
    args = (a, b) + ((add,) if has_add else ())
    return pl.pallas_call(
        body, name=name, grid=(gm, gn, gk), in_specs=in_specs, out_specs=o_spec,
        out_shape=jax.ShapeDtypeStruct((M, N), out_dtype),
        scratch_shapes=[pltpu.VMEM((tm, tn), F32)] if gk > 1 and not acc_in_out else [],
        compiler_params=_cparams("parallel", "parallel", "arbitrary"),
    )(*args)


def _ln_stats(z):
    mu = jnp.mean(z, axis=1, keepdims=True)
    zc = z - mu
    var = jnp.mean(zc * zc, axis=1, keepdims=True)
    rstd = lax.rsqrt(var + LN_EPS)
    return zc * rstd, rstd


def _ln_bwd_math(dy, xhat, rstd, g):
    dxh = dy * g
    m1 = jnp.mean(dxh, axis=1, keepdims=True)
    m2 = jnp.mean(dxh * xhat, axis=1, keepdims=True)
    return rstd * (dxh - m1 - xhat * m2)


def _out_ln(yg, wout, x, g, b, *, tb, name):
    S = x.shape[0]
    tb = min(tb, S)

    def body(yg_ref, w_ref, x_ref, g_ref, b_ref, y_ref, xhat_ref, rstd_ref, yt_ref):
        o = jnp.dot(yg_ref[...], w_ref[...], preferred_element_type=F32)
        xhat, rstd = _ln_stats(ALPHA * x_ref[...] + o)
        xhat_ref[...] = xhat
        rstd_ref[...] = rstd
        y = xhat * g_ref[...] + b_ref[...]
        y_ref[...] = y
        yt_ref[...] = y.T.astype(BF16)

    row = pl.BlockSpec((tb, D_MODEL), lambda i: (i, 0))
    vec = pl.BlockSpec((1, D_MODEL), lambda i: (0, 0))
    return pl.pallas_call(
        body, name=name, grid=(S // tb,),
        in_specs=[pl.BlockSpec((tb, D_MIX), lambda i: (i, 0)), pl.BlockSpec((D_MIX, D_MODEL), lambda i: (0, 0)),
                  row, vec, vec],
        out_specs=[row, row, pl.BlockSpec((tb, 1), lambda i: (i, 0)), pl.BlockSpec((D_MODEL, tb), lambda i: (0, i))],
        out_shape=[jax.ShapeDtypeStruct((S, D_MODEL), F32), jax.ShapeDtypeStruct((S, D_MODEL), F32),
                   jax.ShapeDtypeStruct((S, 1), F32), jax.ShapeDtypeStruct((D_MODEL, S), BF16)],
        compiler_params=_cparams("parallel"),
    )(yg, wout, x, g, b)


def _in_proj_t(x, w, *, tm, name):
    S = x.shape[0]
    tm = min(tm, S)

    def body(x_ref, w_ref, h_ref, xt_ref):
        xv = x_ref[...]
        h_ref[...] = jnp.dot(xv.astype(BF16), w_ref[...], preferred_element_type=F32).astype(BF16)
        xt_ref[...] = xv.T.astype(BF16)

    return pl.pallas_call(
        body, name=name, grid=(S // tm,),
        in_specs=[pl.BlockSpec((tm, D_MODEL), lambda i: (i, 0)), pl.BlockSpec((D_MODEL, D_IN), lambda i: (0, 0))],
        out_specs=[pl.BlockSpec((tm, D_IN), lambda i: (i, 0)), pl.BlockSpec((D_MODEL, tm), lambda i: (0, i))],
        out_shape=[jax.ShapeDtypeStruct((S, D_IN), BF16), jax.ShapeDtypeStruct((D_MODEL, S), BF16)],
        compiler_params=_cparams("parallel"),
    )(x, w)


def _out_ln_loss(yg, wout, x, g, b, target, *, tb, name):
    S = x.shape[0]
    tb = min(tb, S)

    def body(yg_ref, w_ref, x_ref, g_ref, b_ref, t_ref, dz_ref, dg_ref, db_ref, sq_ref):
        @pl.when(pl.program_id(0) == 0)
        def _():
            dg_ref[...] = jnp.zeros_like(dg_ref)
            db_ref[...] = jnp.zeros_like(db_ref)
            sq_ref[...] = jnp.zeros_like(sq_ref)

        o = jnp.dot(yg_ref[...], w_ref[...], preferred_element_type=F32)
        xhat, rstd = _ln_stats(ALPHA * x_ref[...] + o)
        err = xhat * g_ref[...] + b_ref[...] - t_ref[...]
        sq_ref[...] += jnp.sum(err * err, axis=0, keepdims=True)
        dy = err * (1.0 / D_MODEL)
        dz_ref[...] = _ln_bwd_math(dy, xhat, rstd, g_ref[...])
        dg_ref[...] += jnp.sum(dy * xhat, axis=0, keepdims=True)
        db_ref[...] += jnp.sum(dy, axis=0, keepdims=True)

    row = pl.BlockSpec((tb, D_MODEL), lambda i: (i, 0))
    vec = pl.BlockSpec((1, D_MODEL), lambda i: (0, 0))
    vshape = jax.ShapeDtypeStruct((1, D_MODEL), F32)
    return pl.pallas_call(
        body, name=name, grid=(S // tb,),
        in_specs=[pl.BlockSpec((tb, D_MIX), lambda i: (i, 0)), pl.BlockSpec((D_MIX, D_MODEL), lambda i: (0, 0)),
                  row, vec, vec, row],
        out_specs=[row, vec, vec, vec],
        out_shape=[jax.ShapeDtypeStruct((S, D_MODEL), F32), vshape, vshape, vshape],
        compiler_params=_cparams("arbitrary"),
    )(yg, wout, x, g, b, target)


def _ln_bwd(dy, xhat, rstd, g, *, tb, name):
    S = dy.shape[0]
    tb = min(tb, S)

    def body(dy_ref, xhat_ref, rstd_ref, g_ref, dz_ref, dg_ref, db_ref):
        @pl.when(pl.program_id(0) == 0)
        def _():
            dg_ref[...] = jnp.zeros_like(dg_ref)
            db_ref[...] = jnp.zeros_like(db_ref)

        dy_, xhat_ = dy_ref[...], xhat_ref[...]
        dz_ref[...] = _ln_bwd_math(dy_, xhat_, rstd_ref[...], g_ref[...])
        dg_ref[...] += jnp.sum(dy_ * xhat_, axis=0, keepdims=True)
        db_ref[...] += jnp.sum(dy_, axis=0, keepdims=True)

    row = pl.BlockSpec((tb, D_MODEL), lambda i: (i, 0))
    vec = pl.BlockSpec((1, D_MODEL), lambda i: (0, 0))
    return pl.pallas_call(
        body, name=name, grid=(S // tb,),
        in_specs=[row, row, pl.BlockSpec((tb, 1), lambda i: (i, 0)), vec],
        out_specs=[row, vec, vec],
        out_shape=[jax.ShapeDtypeStruct((S, D_MODEL), F32), jax.ShapeDtypeStruct((1, D_MODEL), F32),
                   jax.ShapeDtypeStruct((1, D_MODEL), F32)],
        compiler_params=_cparams("arbitrary"),
    )(dy, xhat, rstd, g)


def _gate_fwd(ysrc, scale, h, ymem, *, tb, name):
    S = ysrc.shape[0]
    tb = min(tb, S)

    def body(ys_ref, sc_ref, ga_ref, gb_ref, gc_ref, ym_ref, yg_ref):
        ymain = ys_ref[...] * sc_ref[...]
        for k, g_ref in enumerate((ga_ref, gb_ref)):
            gv = g_ref[...].astype(F32)
            yg_ref[:, 512 * k:512 * (k + 1)] = (ymain[:, 512 * k:512 * (k + 1)] * gv * _sigmoid(gv)).astype(BF16)
        gv = gc_ref[...].astype(F32)
        yg_ref[:, 1024:1536] = (ym_ref[...] * gv * _sigmoid(gv)).astype(BF16)

    slab = lambda c: pl.BlockSpec((tb, 512), lambda i, c=c: (i, c))
    return pl.pallas_call(
        body, name=name, grid=(S // tb,),
        in_specs=[pl.BlockSpec((tb, D_MAIN), lambda i: (i, 0)), pl.BlockSpec((1, D_MAIN), lambda i: (0, 0)),
                  slab(3), slab(4), slab(5), pl.BlockSpec((tb, D_MEM), lambda i: (i, 0))],
        out_specs=pl.BlockSpec((tb, D_MIX), lambda i: (i, 0)),
        out_shape=jax.ShapeDtypeStruct((S, D_MIX), BF16),
        compiler_params=_cparams("parallel"),
    )(ysrc, scale, h, h, h, ymem)


def _gate_bwd(dz, wout, ysrc, scale, h, ymem, *, tb, name, fox=False):
    S = ysrc.shape[0]
    tb = min(tb, S)

    def dsilu(gv):
        sg = _sigmoid(gv)
        return sg, sg * (1.0 + gv * (1.0 - sg))

    def body(dz_ref, w_ref, ys_ref, sc_ref, ga_ref, gb_ref, gc_ref, ym_ref, *rest):
        heads_ref, place_ref = rest[:2] if fox else (None, None)
        dym_ref, dymem_ref, dh_ref = rest[-3:]
        dyg = lax.dot_general(dz_ref[...].astype(BF16), w_ref[...], NT, preferred_element_type=F32)
        ymain = ys_ref[...] * sc_ref[...]
        dys = []
        for k, g_ref in enumerate((ga_ref, gb_ref)):
            gv, d = g_ref[...].astype(F32), dyg[:, 512 * k:512 * (k + 1)]
            sg, ds = dsilu(gv)
            dys.append(d * gv * sg)
            dh_ref[:, 512 * k:512 * (k + 1)] = (d * ymain[:, 512 * k:512 * (k + 1)] * ds).astype(BF16)
        dy = jnp.concatenate(dys, axis=1)
        if fox:
            delta = sum(jnp.dot(p.astype(BF16), heads_ref[...], preferred_element_type=F32) for p in _split3(dy * ymain))
            aux = _aux_lanes(-delta, place_ref)
            first = lax.broadcasted_iota(jnp.int32, (tb, LANES), 1) < FOX_HEAD_DIM
            for g in range(FOX_PAIRS):
                dy2 = dy[:, LANES * g:LANES * (g + 1)]
                for hh in range(2):
                    sl = slice(LANES * (2 * g + hh), LANES * (2 * g + hh + 1))
                    dym_ref[:, sl] = jnp.where(first, dy2 if hh == 0 else _swap_halves(dy2), aux[:, sl]).astype(BF16)
        else:
            dym_ref[...] = dy
        gv, d = gc_ref[...].astype(F32), dyg[:, 1024:1536]
        sg, ds = dsilu(gv)
        dymem_ref[...] = d * gv * sg
        dh_ref[:, 1024:1536] = (d * ym_ref[...] * ds).astype(BF16)

    slab = lambda c: pl.BlockSpec((tb, 512), lambda i, c=c: (i, c))
    whole = lambda a: pl.BlockSpec(a.shape, lambda i: (0, 0))
    heads = jnp.asarray(np.arange(D_MAIN)[:, None] // FOX_HEAD_DIM == np.arange(LANES)[None, :], BF16)
    consts = (heads, _aux_placement()) if fox else ()
    return pl.pallas_call(
        body, name=name, grid=(S // tb,),
        in_specs=[pl.BlockSpec((tb, D_MODEL), lambda i: (i, 0)), pl.BlockSpec((D_MIX, D_MODEL), lambda i: (0, 0)),
                  pl.BlockSpec((tb, D_MAIN), lambda i: (i, 0)), pl.BlockSpec((1, D_MAIN), lambda i: (0, 0)),
                  slab(3), slab(4), slab(5), pl.BlockSpec((tb, D_MEM), lambda i: (i, 0))] + [whole(a) for a in consts],
        out_specs=[pl.BlockSpec((tb, 2 * D_MAIN if fox else D_MAIN), lambda i: (i, 0)),
                   pl.BlockSpec((tb, D_MEM), lambda i: (i, 0)), pl.BlockSpec((tb, D_MIX), lambda i: (i, 1))],
        out_shape=[jax.ShapeDtypeStruct((S, 2 * D_MAIN), BF16) if fox else jax.ShapeDtypeStruct((S, D_MAIN), F32),
                   jax.ShapeDtypeStruct((S, D_MEM), F32), jax.ShapeDtypeStruct((S, D_IN), BF16)],
        compiler_params=_cparams("parallel"),
    )(dz, wout, ysrc, scale, h, h, h, ymem, *consts)


def _window_count(t0, rows, w):
    t = t0 + lax.broadcasted_iota(jnp.int32, (rows, POOL_GROUP), 0)
    return jnp.minimum(t + 1, w).astype(F32)


def _pool_fwd(h, pw, *, tb, name):
    S = h.shape[0]
    tb = min(tb, S)

    def body(u_ref, pw_ref, pm_ref, mixed_ref, tail_ref):
        i = pl.program_id(0)

        @pl.when(i == 0)
        def _():
            tail_ref[...] = jnp.zeros_like(tail_ref)

        u = u_ref[...].astype(F32)
        xfull = jnp.concatenate([tail_ref[...], u], axis=0)
        for gi, w in enumerate(POOL_WINDOWS):
            cols = slice(POOL_GROUP * gi, POOL_GROUP * (gi + 1))
            s = xfull[:, cols]
            sh = 1
            while sh < w:
                s = s + pltpu.roll(s, sh, 0)
                sh *= 2
            pm = s[POOL_HALO:, :] / _window_count(i * tb, tb, w) - u[:, cols]
            pmb = pm.astype(BF16)
            pm_ref[:, cols] = pmb
            mixed_ref[:, cols] = jnp.dot(pmb, pw_ref[gi], preferred_element_type=F32)
        tail_ref[...] = u[tb - POOL_HALO:, :]

    return pl.pallas_call(
        body, name=name, grid=(S // tb,),
        in_specs=[pl.BlockSpec((tb, D_MAIN), lambda i: (i, 0)),
                  pl.BlockSpec((4, POOL_GROUP, POOL_GROUP), lambda i: (0, 0, 0))],
        out_specs=[pl.BlockSpec((tb, D_MAIN), lambda i: (i, 0)), pl.BlockSpec((tb, D_MAIN), lambda i: (i, 0))],
        out_shape=[jax.ShapeDtypeStruct((S, D_MAIN), BF16), jax.ShapeDtypeStruct((S, D_MAIN), F32)],
        scratch_shapes=[pltpu.VMEM((POOL_HALO, D_MAIN), F32)],
        compiler_params=_cparams("arbitrary"),
    )(h, pw)


def _pool_bwd(dymain, pm, mixed, pw, scale, dh, *, tb, name):
    S = dymain.shape[0]
    tb = min(tb, S)
    nb = S // tb
    n = tb + POOL_HALO

    def body(dy_ref, pm_ref, mixed_ref, pw_ref, sc_ref, dh_in, dh_ref, dpw_ref, dsc_ref, head_ref, dpw_acc):
        del dh_in
        i = pl.program_id(0)

        @pl.when(i == 0)
        def _():
            head_ref[...] = jnp.zeros_like(head_ref)
            dpw_acc[...] = jnp.zeros_like(dpw_acc)
            dsc_ref[...] = jnp.zeros_like(dsc_ref)

        dy = dy_ref[...]
        dsc_ref[...] += jnp.sum(dy * mixed_ref[...], axis=0, keepdims=True)
        dmixed = dy * sc_ref[...]
        t0 = (nb - 1 - i) * tb
        for gi, w in enumerate(POOL_WINDOWS):
            cols = slice(POOL_GROUP * gi, POOL_GROUP * (gi + 1))
            dm = dmixed[:, cols].astype(BF16)
            dpw_acc[gi] += lax.dot_general(pm_ref[:, cols], dm, TN, preferred_element_type=F32)
            dpm = lax.dot_general(dm, pw_ref[gi], NT, preferred_element_type=F32)
            e = dpm / _window_count(t0, tb, w)
            s = jnp.concatenate([e, head_ref[:, cols]], axis=0)
            sh = 1
            while sh < w:
                s = s + pltpu.roll(s, n - sh, 0)
                sh *= 2
            dh_ref[:, cols] = (s[:tb, :] - dpm).astype(BF16)
            head_ref[:, cols] = e[:POOL_HALO, :]

        @pl.when(i == nb - 1)
        def _():
            dpw_ref[...] = dpw_acc[...].astype(BF16)

    rev = lambda i: (nb - 1 - i, 0)
    return pl.pallas_call(
        body, name=name, grid=(nb,),
        in_specs=[pl.BlockSpec((tb, D_MAIN), rev), pl.BlockSpec((tb, D_MAIN), rev), pl.BlockSpec((tb, D_MAIN), rev),
                  pl.BlockSpec((4, POOL_GROUP, POOL_GROUP), lambda i: (0, 0, 0)),
                  pl.BlockSpec((1, D_MAIN), lambda i: (0, 0)), pl.BlockSpec(memory_space=pl.ANY)],
        out_specs=[pl.BlockSpec((tb, D_MAIN), rev),
                   pl.BlockSpec((4, POOL_GROUP, POOL_GROUP), lambda i: (0, 0, 0)),
                   pl.BlockSpec((1, D_MAIN), lambda i: (0, 0))],
        out_shape=[jax.ShapeDtypeStruct(dh.shape, dh.dtype),
                   jax.ShapeDtypeStruct((4, POOL_GROUP, POOL_GROUP), BF16), jax.ShapeDtypeStruct((1, D_MAIN), F32)],
        scratch_shapes=[pltpu.VMEM((POOL_HALO, D_MAIN), F32), pltpu.VMEM((4, POOL_GROUP, POOL_GROUP), F32)],
        input_output_aliases={5: 0},
        compiler_params=_cparams("arbitrary"),
    )(dymain, pm, mixed, pw, scale, dh)


MEM_SCALE = MEM_HEAD_DIM ** -0.5


def _mem_probs(q_ref, mkv_ref, hd):
    cols = slice(MEM_HEAD_DIM * hd, MEM_HEAD_DIM * (hd + 1))
    q = (q_ref[:, cols].astype(F32) * MEM_SCALE).astype(BF16)
    mk = mkv_ref[:, cols].astype(BF16)
    mv = mkv_ref[:, D_MEM + MEM_HEAD_DIM * hd:D_MEM + MEM_HEAD_DIM * (hd + 1)].astype(BF16)
    s = lax.dot_general(q, mk, NT, preferred_element_type=F32)
    e = jnp.exp(s - jnp.max(s, axis=1, keepdims=True))
    return cols, q, mk, mv, e, jnp.sum(e, axis=1, keepdims=True)


def _memattn_fwd(h, mkv, *, tb, name):
    S = h.shape[0]
    tb = min(tb, S)

    def body(q_ref, mkv_ref, y_ref):
        for hd in range(MEM_HEADS):
            cols, _, _, mv, e, l = _mem_probs(q_ref, mkv_ref, hd)
            y_ref[:, cols] = jnp.dot(e.astype(BF16), mv, preferred_element_type=F32) / l

    return pl.pallas_call(
        body, name=name, grid=(S // tb,),
        in_specs=[pl.BlockSpec((tb, D_MEM), lambda i: (i, 2)), pl.BlockSpec((N_MEM, 2 * D_MEM), lambda i: (0, 0))],
        out_specs=pl.BlockSpec((tb, D_MEM), lambda i: (i, 0)),
        out_shape=jax.ShapeDtypeStruct((S, D_MEM), F32),
        compiler_params=_cparams("parallel"),
    )(h, mkv)


def _memattn_bwd(h, mkv, dy, dh, *, tb, name):
    S = h.shape[0]
    tb = min(tb, S)

    def body(q_ref, mkv_ref, dy_ref, dh_in, dh_ref, dmkv_ref):
        del dh_in

        @pl.when(pl.program_id(0) == 0)
        def _():
            dmkv_ref[...] = jnp.zeros_like(dmkv_ref)

        for hd in range(MEM_HEADS):
            cols, q, mk, mv, e, l = _mem_probs(q_ref, mkv_ref, hd)
            p = e / l
            dyh = dy_ref[:, cols].astype(BF16)
            dp = lax.dot_general(dyh, mv, NT, preferred_element_type=F32)
            ds = p * (dp - jnp.sum(dp * p, axis=1, keepdims=True))
            dsb = ds.astype(BF16)
            dh_ref[:, cols] = (jnp.dot(dsb, mk, preferred_element_type=F32) * MEM_SCALE).astype(BF16)
            dmkv_ref[:, cols] += lax.dot_general(dsb, q, TN, preferred_element_type=F32)
            vcols = slice(D_MEM + MEM_HEAD_DIM * hd, D_MEM + MEM_HEAD_DIM * (hd + 1))
            dmkv_ref[:, vcols] += lax.dot_general(p.astype(BF16), dyh, TN, preferred_element_type=F32)

    return pl.pallas_call(
        body, name=name, grid=(S // tb,),
        in_specs=[pl.BlockSpec((tb, D_MEM), lambda i: (i, 2)), pl.BlockSpec((N_MEM, 2 * D_MEM), lambda i: (0, 0)),
                  pl.BlockSpec((tb, D_MEM), lambda i: (i, 0)), pl.BlockSpec(memory_space=pl.ANY)],
        out_specs=[pl.BlockSpec((tb, D_MEM), lambda i: (i, 2)), pl.BlockSpec((N_MEM, 2 * D_MEM), lambda i: (0, 0))],
        out_shape=[jax.ShapeDtypeStruct(dh.shape, dh.dtype), jax.ShapeDtypeStruct((N_MEM, 2 * D_MEM), F32)],
        input_output_aliases={3: 0},
        compiler_params=_cparams("arbitrary"),
    )(h, mkv, dy, dh)


def _forget_fwd(fl, bias, *, tb, name):
    S = fl.shape[0]
    tb = min(tb, S)

    def body(fl_ref, b_ref, o_ref, carry_ref):
        @pl.when(pl.program_id(0) == 0)
        def _():
            carry_ref[...] = jnp.zeros_like(carry_ref)

        z = fl_ref[...] + b_ref[...]
        lf = jnp.minimum(z, 0.0) - jnp.log(1.0 + jnp.exp(-jnp.abs(z)))
        row = lax.broadcasted_iota(jnp.int32, (tb, LANES), 0)
        c = lf
        sh = 1
        while sh < tb:
            c = c + jnp.where(row >= sh, pltpu.roll(c, sh, 0), 0.0)
            sh *= 2
        o_ref[...] = -(carry_ref[...] + c)
        carry_ref[...] += jnp.sum(lf, axis=0, keepdims=True)

    return pl.pallas_call(
        body, name=name, grid=(S // tb,),
        in_specs=[pl.BlockSpec((tb, LANES), lambda i: (i, 0)), pl.BlockSpec((1, LANES), lambda i: (0, 0))],
        out_specs=pl.BlockSpec((tb, LANES), lambda i: (i, 0)),
        out_shape=jax.ShapeDtypeStruct((S, LANES), F32),
        scratch_shapes=[pltpu.VMEM((1, LANES), F32)],
        compiler_params=_cparams("arbitrary"),
    )(fl, bias)


def _forget_bwd(dn, drow, fl, bias, *, tb, name):
    S = fl.shape[0]
    tb = min(tb, S)
    nb = S // tb

    def body(dn_ref, dr_ref, fl_ref, b_ref, dh_ref, db_ref, carry_ref):
        @pl.when(pl.program_id(0) == 0)
        def _():
            carry_ref[...] = jnp.zeros_like(carry_ref)
            db_ref[...] = jnp.zeros_like(db_ref)

        src = lax.broadcasted_iota(jnp.int32, (D_MAIN, LANES), 0)
        head = lax.broadcasted_iota(jnp.int32, (D_MAIN, LANES), 1)
        pick = lambda off: jnp.where((src == FOX_HEAD_DIM * head + off) & (head < FOX_HEADS), 1.0, 0.0).astype(BF16)
        hdot = lambda a, sel: sum(jnp.dot(part.astype(BF16), sel, preferred_element_type=F32) for part in _split3(a))
        dcum = hdot(dr_ref[...], pick(3)) - hdot(dn_ref[...], pick(0))
        row = lax.broadcasted_iota(jnp.int32, (tb, LANES), 0)
        c = dcum
        sh = 1
        while sh < tb:
            c = c + jnp.where(row < tb - sh, pltpu.roll(c, tb - sh, 0), 0.0)
            sh *= 2
        dlf = carry_ref[...] + c
        carry_ref[...] += jnp.sum(dcum, axis=0, keepdims=True)
        z = fl_ref[...] + b_ref[...]
        lane = lax.broadcasted_iota(jnp.int32, (tb, LANES), 1)
        dfl = jnp.where(lane < FOX_HEADS, dlf / (1.0 + jnp.exp(z)), 0.0)
        db_ref[...] += jnp.sum(dfl, axis=0, keepdims=True)
        dh_ref[...] = dfl.astype(BF16)

    rev = lambda i: (nb - 1 - i, 0)
    return pl.pallas_call(
        body, name=name, grid=(nb,),
        in_specs=[pl.BlockSpec((tb, D_MAIN), rev), pl.BlockSpec((tb, D_MAIN), rev), pl.BlockSpec((tb, LANES), rev),
                  pl.BlockSpec((1, LANES), lambda i: (0, 0))],
        out_specs=[pl.BlockSpec((tb, LANES), rev), pl.BlockSpec((1, LANES), lambda i: (0, 0))],
        out_shape=[jax.ShapeDtypeStruct((S, LANES), BF16), jax.ShapeDtypeStruct((1, LANES), F32)],
        scratch_shapes=[pltpu.VMEM((1, LANES), F32)],
        compiler_params=_cparams("arbitrary"),
    )(dn, drow, fl, bias)


FOX_SCALE = FOX_HEAD_DIM ** -0.5
LOG2E = 1.4426950408889634
LN2 = 0.6931471805599453
AUX = FOX_HEAD_DIM


def _split3(x):
    hi = x.astype(BF16).astype(F32)
    r = x - hi
    mid = r.astype(BF16).astype(F32)
    return hi, mid, (r - mid).astype(BF16).astype(F32)


def _aux_placement():
    r = np.arange(3 * LANES)[:, None]
    c = np.arange(FOX_HEADS * LANES)[None, :]
    return jnp.asarray((r % LANES == c // LANES) & (c % LANES == AUX + r // LANES), BF16)


def _aux_lanes(vals, place_ref):
    parts = jnp.concatenate(_split3(vals), axis=1).astype(BF16)
    return jnp.dot(parts, place_ref[...], preferred_element_type=F32)


def _lanes3(lane, base, parts, rest):
    return jnp.where(lane == base, parts[0], jnp.where(lane == base + 1, parts[1],
                                                       jnp.where(lane == base + 2, parts[2], rest)))


def _swap_halves(x):
    return pltpu.roll(x, FOX_HEAD_DIM, 1)


def _causal_steps(nq, keys_outer):
    if keys_outer:
        pairs = [(i, j) for j in range(nq) for i in range(j, nq)]
    else:
        pairs = [(i, j) for i in range(nq) for j in range(i + 1)]
    it, jt = zip(*pairs)
    return jnp.asarray(np.array(it, np.int32)), jnp.asarray(np.array(jt, np.int32))


def _in_proj_fox(x, w, *, tm, name):
    S = x.shape[0]
    tm = min(tm, S)

    def body(x_ref, w_ref, h_ref, qa_ref):
        acc = jnp.dot(x_ref[...].astype(BF16), w_ref[...], preferred_element_type=F32)
        h_ref[...] = acc.astype(BF16)
        lane = lax.broadcasted_iota(jnp.int32, (tm, LANES), 1)
        first = lane < FOX_HEAD_DIM
        ones_q = jnp.where((lane >= AUX) & (lane < AUX + 3), 1.0, 0.0)
        for g in range(FOX_PAIRS):
            q = acc[:, LANES * g:LANES * (g + 1)] * (FOX_SCALE * LOG2E)
            qa_ref[:, 2 * LANES * g:2 * LANES * g + LANES] = jnp.where(first, q, ones_q).astype(BF16)
            qa_ref[:, 2 * LANES * g + LANES:2 * LANES * (g + 1)] = jnp.where(first, _swap_halves(q), ones_q).astype(BF16)

    return pl.pallas_call(
        body, name=name, grid=(S // tm,),
        in_specs=[pl.BlockSpec((tm, D_MODEL), lambda i: (i, 0)), pl.BlockSpec((D_MODEL, D_IN), lambda i: (0, 0))],
        out_specs=[pl.BlockSpec((tm, D_IN), lambda i: (i, 0)), pl.BlockSpec((tm, 2 * D_MAIN), lambda i: (i, 0))],
        out_shape=[jax.ShapeDtypeStruct((S, D_IN), BF16), jax.ShapeDtypeStruct((S, 2 * D_MAIN), BF16)],
        compiler_params=_cparams("parallel"),
    )(x, w)


def _kv_proj_fox(x, wkv, negcum, *, tm, name):
    S = x.shape[0]
    tm = min(tm, S)

    def body(x_ref, w_ref, nc_ref, place_ref, ka_ref, va_ref):
        acc = jnp.dot(x_ref[...].astype(BF16), w_ref[...], preferred_element_type=F32)
        aux = _aux_lanes(nc_ref[...] * LOG2E, place_ref)
        lane = lax.broadcasted_iota(jnp.int32, (tm, LANES), 1)
        first = lane < FOX_HEAD_DIM
        ones_k = jnp.where((lane >= AUX + 3) & (lane < AUX + 6), 1.0, 0.0)
        for g in range(FOX_PAIRS):
            k = acc[:, LANES * g:LANES * (g + 1)]
            v = acc[:, D_MAIN + LANES * g:D_MAIN + LANES * (g + 1)]
            for hh in range(2):
                sl = slice(LANES * (2 * g + hh), LANES * (2 * g + hh + 1))
                kh, vh = (k, v) if hh == 0 else (_swap_halves(k), _swap_halves(v))
                ka_ref[:, sl] = jnp.where(first, kh, aux[:, sl] + ones_k).astype(BF16)
                va_ref[:, sl] = jnp.where(first, vh, 1.0).astype(BF16)

    out = pl.BlockSpec((tm, 2 * D_MAIN), lambda i: (i, 0))
    shp = jax.ShapeDtypeStruct((S, 2 * D_MAIN), BF16)
    return pl.pallas_call(
        body, name=name, grid=(S // tm,),
        in_specs=[pl.BlockSpec((tm, D_MODEL), lambda i: (i, 0)), pl.BlockSpec((D_MODEL, 2 * D_MAIN), lambda i: (0, 0)),
                  pl.BlockSpec((tm, LANES), lambda i: (i, 0)), pl.BlockSpec((3 * LANES, 2 * D_MAIN), lambda i: (0, 0))],
        out_specs=[out, out], out_shape=[shp, shp],
        compiler_params=_cparams("parallel"),
    )(x, wkv, negcum, _aux_placement())


def _dwkv(xt, dk, dv, dfl, *, tk, name):
    S = xt.shape[1]
    tk = min(tk, S)

    def body(x_ref, dk_ref, dv_ref, df_ref, o_ref):
        @pl.when(pl.program_id(0) == 0)
        def _():
            o_ref[...] = jnp.zeros_like(o_ref)

        for b_ref, c0 in ((dk_ref, 0), (dv_ref, D_MAIN), (df_ref, 2 * D_MAIN)):
            o_ref[:, c0:c0 + b_ref.shape[1]] += jnp.dot(x_ref[...], b_ref[...], preferred_element_type=F32)

    row = lambda n: pl.BlockSpec((tk, n), lambda k: (k, 0))
    return pl.pallas_call(
        body, name=name, grid=(S // tk,),
        in_specs=[pl.BlockSpec((D_MODEL, tk), lambda k: (0, k)), row(D_MAIN), row(D_MAIN), row(LANES)],
        out_specs=pl.BlockSpec((D_MODEL, 2 * D_MAIN + LANES), lambda k: (0, 0)),
        out_shape=jax.ShapeDtypeStruct((D_MODEL, 2 * D_MAIN + LANES), F32),
        compiler_params=_cparams("arbitrary"),
    )(xt, dk, dv, dfl)


def _dx1(dh, win, dk, dv, dfl, wkv, dz, *, tm, name):
    S = dh.shape[0]
    tm = min(tm, S)

    def body(dh_ref, win_ref, dk_ref, dv_ref, df_ref, wk_ref, wv_ref, wf_ref, dz_ref, o_ref):
        acc = ALPHA * dz_ref[...]
        for a_ref, b_ref in ((dh_ref, win_ref), (dk_ref, wk_ref), (dv_ref, wv_ref), (df_ref, wf_ref)):
            acc = acc + lax.dot_general(a_ref[...], b_ref[...], NT, preferred_element_type=F32)
        o_ref[...] = acc

    row = lambda n: pl.BlockSpec((tm, n), lambda i: (i, 0))
    wcols = lambda n, c: pl.BlockSpec((D_MODEL, n), lambda i, c=c: (0, c))
    return pl.pallas_call(
        body, name=name, grid=(S // tm,),
        in_specs=[row(D_IN), wcols(D_IN, 0), row(D_MAIN), row(D_MAIN), row(LANES),
                  wcols(D_MAIN, 0), wcols(D_MAIN, 1), wcols(LANES, 2 * D_MAIN // LANES), row(D_MODEL)],
        out_specs=row(D_MODEL), out_shape=jax.ShapeDtypeStruct((S, D_MODEL), F32),
        compiler_params=_cparams("parallel"),
    )(dh, win, dk, dv, dfl, wkv, wkv, wkv, dz)


def _fox_fwd(qa, ka, va, *, tq, name):
    S = qa.shape[0]
    tq = min(tq, S)
    nq = S // tq
    half = tq // 2
    it, jt = _causal_steps(nq, keys_outer=False)

    def body(it_ref, jt_ref, qa_ref, ka_ref, va_ref, y_ref, qb_ref, m_ref, acc_ref):
        n = pl.program_id(1)
        i, j = it_ref[n], jt_ref[n]
        first = lax.broadcasted_iota(jnp.int32, (tq, LANES), 1) < FOX_HEAD_DIM

        @pl.when(j == 0)
        def _():
            m_ref[...] = jnp.full_like(m_ref, NEG)
            acc_ref[...] = jnp.zeros_like(acc_ref)

        def update(hh, rows, nk, masked):
            sl = slice(LANES * hh, LANES * (hh + 1))
            s = lax.dot_general(qa_ref[rows, sl], ka_ref[0:nk, sl], NT, preferred_element_type=F32)
            if masked:
                r = lax.broadcasted_iota(jnp.int32, s.shape, 0) + rows.start
                c = lax.broadcasted_iota(jnp.int32, s.shape, 1)
                s = jnp.where(c <= r, s, NEG)
            m_prev = m_ref[hh, rows]
            m_new = jnp.maximum(m_prev, jnp.max(s, axis=1, keepdims=True))
            p = jnp.exp2(s - jnp.tile(m_new, (1, nk // LANES))).astype(BF16)
            acc_ref[hh, rows] = jnp.exp2(m_prev - m_new) * acc_ref[hh, rows] + jnp.dot(
                p, va_ref[0:nk, sl], preferred_element_type=F32)
            m_ref[hh, rows] = m_new

        @pl.when(j < i)
        def _():
            for hh in range(2):
                update(hh, slice(0, tq), tq, False)

        @pl.when(j == i)
        def _():
            for hh in range(2):
                for r0 in range(0, tq, half):
                    update(hh, slice(r0, r0 + half), r0 + half, True)
            lane = lax.broadcasted_iota(jnp.int32, (tq, LANES), 1)
            ys = []
            for hh in range(2):
                sl = slice(LANES * hh, LANES * (hh + 1))
                a = acc_ref[hh]
                denom = _swap_halves(a)
                ys.append(a / denom)
                lse2 = m_ref[hh] + jnp.log(jnp.where(first, denom, a)) * LOG2E
                qb_ref[:, sl] = _lanes3(lane, AUX + 3, _split3(-lse2), qa_ref[:, sl].astype(F32)).astype(BF16)
            y_ref[...] = jnp.where(first, ys[0], _swap_halves(ys[1]))

    qblock = pl.BlockSpec((tq, 2 * LANES), lambda g, n, it, jt: (it[n], g))
    kblock = pl.BlockSpec((tq, 2 * LANES), lambda g, n, it, jt: (jt[n], g))
    return pl.pallas_call(
        body, name=name,
        grid_spec=pltpu.PrefetchScalarGridSpec(
            num_scalar_prefetch=2, grid=(FOX_PAIRS, it.shape[0]),
            in_specs=[qblock, kblock, kblock],
            out_specs=[pl.BlockSpec((tq, LANES), lambda g, n, it, jt: (it[n], g)), qblock],
            scratch_shapes=[pltpu.VMEM((2, tq, LANES), F32), pltpu.VMEM((2, tq, LANES), F32)]),
        out_shape=[jax.ShapeDtypeStruct((S, D_MAIN), F32), jax.ShapeDtypeStruct((S, 2 * D_MAIN), BF16)],
        compiler_params=_cparams("parallel", "arbitrary"),
    )(it, jt, qa, ka, va)


def _fox_bwd(qb, ka, va, dya, dh, *, tq, name):
    S = qb.shape[0]
    tq = min(tq, S)
    nq = S // tq
    half = tq // 2
    it, jt = _causal_steps(nq, keys_outer=True)
    nsteps = it.shape[0]

    def body(it_ref, jt_ref, qb_ref, ka_ref, va_ref, dya_ref, dh_in, dq_ref, dk_ref, dv_ref, dn_ref, drow_ref,
             dq_acc, dk_acc, dv_acc):
        del dh_in
        n = pl.program_id(1)
        i, j = it_ref[n], jt_ref[n]
        first = lax.broadcasted_iota(jnp.int32, (tq, LANES), 1) < FOX_HEAD_DIM

        @pl.when(n == 0)
        def _():
            dq_acc[...] = jnp.zeros_like(dq_acc)

        @pl.when(i == j)
        def _():
            dk_acc[...] = jnp.zeros_like(dk_acc)
            dv_acc[...] = jnp.zeros_like(dv_acc)

        def update(hh, keys, q0, masked):
            sl = slice(LANES * hh, LANES * (hh + 1))
            qbh, kah, dyah = qb_ref[q0:tq, sl], ka_ref[keys, sl], dya_ref[q0:tq, sl]
            eT = lax.dot_general(kah, qbh, NT, preferred_element_type=F32)
            if masked:
                r = lax.broadcasted_iota(jnp.int32, eT.shape, 0) + keys.start
                c = lax.broadcasted_iota(jnp.int32, eT.shape, 1) + q0
                eT = jnp.where(r <= c, eT, NEG)
            pT = jnp.exp2(eT)
            dsT = pT * lax.dot_general(va_ref[keys, sl], dyah, NT, preferred_element_type=F32)
            dsb = dsT.astype(BF16)
            dv_acc[hh, keys] += jnp.dot(pT.astype(BF16), dyah, preferred_element_type=F32)
            dk_acc[hh, keys] += jnp.dot(dsb, qbh, preferred_element_type=F32)
            rows = pl.ds(pl.multiple_of(i * tq + q0, half), tq - q0)
            dq_acc[hh, rows, :] += lax.dot_general(dsb, kah, TN, preferred_element_type=F32)

        @pl.when(i > j)
        def _():
            for hh in range(2):
                update(hh, slice(0, tq), 0, False)

        @pl.when(i == j)
        def _():
            for hh in range(2):
                for k0 in range(0, tq, half):
                    update(hh, slice(k0, k0 + half), k0, True)

        @pl.when(i == nq - 1)
        def _():
            dk_ref[...] = (jnp.where(first, dk_acc[0], _swap_halves(dk_acc[1])) * LN2).astype(BF16)
            dv_ref[...] = jnp.where(first, dv_acc[0], _swap_halves(dv_acc[1])).astype(BF16)
            dn_ref[...] = jnp.where(first, _swap_halves(dk_acc[0]), dk_acc[1])

        @pl.when(n == nsteps - 1)
        def _():
            first_s = lax.broadcasted_iota(jnp.int32, (S, LANES), 1) < FOX_HEAD_DIM
            dq_ref[...] = (jnp.where(first_s, dq_acc[0], _swap_halves(dq_acc[1])) * FOX_SCALE).astype(BF16)
            drow_ref[...] = jnp.where(first_s, _swap_halves(dq_acc[0]), dq_acc[1])

    qblock = pl.BlockSpec((tq, 2 * LANES), lambda g, n, it, jt: (it[n], g))
    kblock = pl.BlockSpec((tq, 2 * LANES), lambda g, n, it, jt: (jt[n], g))
    whole = pl.BlockSpec((S, LANES), lambda g, n, it, jt: (0, g))
    kout = pl.BlockSpec((tq, LANES), lambda g, n, it, jt: (jt[n], g))
    return pl.pallas_call(
        body, name=name,
        grid_spec=pltpu.PrefetchScalarGridSpec(
            num_scalar_prefetch=2, grid=(FOX_PAIRS, nsteps),
            in_specs=[qblock, kblock, kblock, qblock, pl.BlockSpec(memory_space=pl.ANY)],
            out_specs=[whole, kout, kout, kout, whole],
            scratch_shapes=[pltpu.VMEM((2, S, LANES), F32), pltpu.VMEM((2, tq, LANES), F32),
                            pltpu.VMEM((2, tq, LANES), F32)]),
        out_shape=[jax.ShapeDtypeStruct(dh.shape, dh.dtype), jax.ShapeDtypeStruct((S, D_MAIN), BF16),
                   jax.ShapeDtypeStruct((S, D_MAIN), BF16),
                   jax.ShapeDtypeStruct((S, D_MAIN), F32), jax.ShapeDtypeStruct((S, D_MAIN), F32)],
        input_output_aliases={6: 0},
        compiler_params=_cparams("parallel", "arbitrary"),
    )(it, jt, qb, ka, va, dya, dh)


TB_ROWS = 512
TB_SEQ = 512
TK_DW_IN = 2048
TK_DW_OUT = 1024
TQ_FOX_FWD = 1024
TQ_FOX_BWD = 1024


def _local_step(x, mem, target, win0, pscale, ln_g, ln_b, bias, comm):
    ones = jnp.ones((1, D_MAIN), F32)
    g0, b0, g1, b1 = ln_g[0:1], ln_b[0:1], ln_g[1:2], ln_b[1:2]
    mm = lambda a, b, mode, dt, tm, tn, tk, name, **kw: _mm(a, b, mode=mode, out_dtype=dt, tm=tm, tn=tn, tk=tk,
                                                            name=name, **kw)

    h0, xt = _in_proj_t(x, win0, tm=256, name="l0_in")
    wmkv, pw, wout0 = comm.next_weights(h0)
    pm, mixed = _pool_fwd(h0, pw, tb=TB_SEQ, name="l0_pool_fwd")
    mkv0 = mm(mem, wmkv[0], "nn", F32, 256, 1024, 1024, "l0_mkv")
    ymem0 = _memattn_fwd(h0, mkv0, tb=TB_SEQ, name="l0_mem_fwd")
    yg0 = _gate_fwd(mixed, pscale, h0, ymem0, tb=TB_ROWS, name="l0_gate_fwd")
    x1, xhat0, rstd0, x1t = _out_ln(yg0, wout0, x, g0, b0, tb=TB_SEQ, name="l0_out_ln")
    win1, wout1, wkv, wf = comm.late_weights(x1)

    fl = mm(x1, wf, "nn", F32, 512, LANES, D_MODEL, "f_proj")
    negcum = _forget_fwd(fl, bias, tb=TB_SEQ, name="forget_fwd")
    ka, va = _kv_proj_fox(x1, wkv, negcum, tm=512, name="kv_proj")

    h1, qa = _in_proj_fox(x1, win1, tm=256, name="l1_in")
    y1, qb = _fox_fwd(qa, ka, va, tq=TQ_FOX_FWD, name="fox_fwd")
    mkv1 = mm(mem, wmkv[1], "nn", F32, 256, 1024, 1024, "l1_mkv")
    ymem1 = _memattn_fwd(h1, mkv1, tb=TB_SEQ, name="l1_mem_fwd")
    yg1 = _gate_fwd(y1, ones, h1, ymem1, tb=TB_ROWS, name="l1_gate_fwd")
    dz1, dg1, db1, sq = _out_ln_loss(yg1, wout1, x1, g1, b1, target, tb=TB_SEQ, name="l1_out_ln_loss")

    dwout1 = mm(yg1, dz1, "tn", BF16, D_MIX, D_MODEL, TK_DW_OUT, "l1_dwout")
    dya, dymem1, dh1 = _gate_bwd(dz1, wout1, y1, ones, h1, ymem1, tb=TB_ROWS, name="l1_gate_bwd", fox=True)
    dh1, dk, dv, dnp, drowp = _fox_bwd(qb, ka, va, dya, dh1, tq=TQ_FOX_BWD, name="fox_bwd")
    dfl, dbias = _forget_bwd(dnp, drowp, fl, bias, tb=TB_SEQ, name="forget_bwd")
    dh1, dmkv1 = _memattn_bwd(h1, mkv1, dymem1, dh1, tb=TB_SEQ, name="l1_mem_bwd")
    dwmkv1 = mm(mem, dmkv1, "tn", BF16, D_MODEL, 1024, N_MEM, "l1_dwmkv")
    dwin1 = mm(x1t, dh1, "nn", BF16, D_MODEL, D_IN // 2, TK_DW_IN, "l1_dwin")
    dwkv = _dwkv(x1t, dk, dv, dfl, tk=TK_DW_OUT, name="dwkv")
    dwkv = dwkv[:, :2 * D_MAIN + FOX_HEADS].reshape(D_MODEL, N_DEV, -1).transpose(1, 0, 2).astype(BF16)
    anchor = comm.send_layer1(dict(w_out=dwout1, w_mem_kv=dwmkv1, w_in=dwin1, w_kv_shared=dwkv))
    dx1 = _dx1(dh1, win1, dk, dv, dfl, wkv, dz1, tm=256, name="l1_dx")

    dz0, dg0, db0 = _ln_bwd(dx1, xhat0, rstd0, g0 + anchor, tb=TB_ROWS, name="l0_ln_bwd")
    dwout0 = mm(yg0, dz0, "tn", BF16, D_MIX, D_MODEL, TK_DW_OUT, "l0_dwout")
    dy0, dymem0, dh0 = _gate_bwd(dz0, wout0, mixed, pscale, h0, ymem0, tb=TB_ROWS, name="l0_gate_bwd")
    comm.landed_layer1(dy0)
    dh0, dpw, dpscale = _pool_bwd(dy0, pm, mixed, pw, pscale, dh0, tb=TB_SEQ, name="l0_pool_bwd")
    dh0, dmkv0 = _memattn_bwd(h0, mkv0, dymem0, dh0, tb=TB_SEQ, name="l0_mem_bwd")
    dwmkv0 = mm(mem, dmkv0, "tn", BF16, D_MODEL, 1024, N_MEM, "l0_dwmkv")
    dh0 = comm.send_layer0_first(dict(w_out=dwout0, w_mem_kv=dwmkv0, pool_w=dpw), dh0)
    dwin0 = mm(xt, dh0, "nn", BF16, D_MODEL, D_IN // 2, TK_DW_IN, "l0_dwin")
    dz0 = comm.send_layer0_rest(dict(w_in=dwin0, pool_scale=dpscale, ln_g=jnp.concatenate([dg0, dg1]),
                                     ln_b=jnp.concatenate([db0, db1]), b_forget=dbias[0, :FOX_HEADS]), dz0)
    gx = mm(dh0, win0, "nt", F32, 256, D_MODEL, D_IN, "l0_dx", add=dz0, add_scale=ALPHA)
    return sq, gx


MESH_ID = pl.DeviceIdType.MESH
HBM = pl.BlockSpec(memory_space=pl.ANY)
SLICED = {"w_in": (2, D_IN // N_DEV), "w_mem_kv": (1, D_MODEL // N_DEV), "w_out": (1, D_MIX // N_DEV),
          "pool_w": (1, POOL_GROUP // N_DEV)}


def _place():
    return lax.axis_index("x"), lax.axis_index("y"), lax.axis_index("c")


def _slot(p):
    return 4 * p[0] + 2 * p[1] + p[2]


def _cut(ref, axis, width, s):
    idx = [slice(None)] * len(ref.shape)
    idx[axis] = pl.ds(s * width, width)
    return ref.at[tuple(idx)]


def _all_gather(shards, cuts, *, name):
    nt = len(shards)

    def full_shape(a, cut):
        if cut is None:
            return (N_DEV,) + a.shape
        return a.shape[:cut[0]] + (a.shape[cut[0]] * N_DEV,) + a.shape[cut[0] + 1:]

    def body(*refs):
        ins, outs = refs[:nt], refs[nt:2 * nt]
        send_sems, recv_sems, local_sems = refs[2 * nt:]
        x, y, c = _place()
        me, sibling = (x, y, c), (x, y, 1 - c)
        chips = [(1 - x, y), (x, 1 - y), (1 - x, 1 - y)]

        def place(t, s):
            return outs[t].at[s] if cuts[t] is None else _cut(outs[t], cuts[t][0], cuts[t][1], s)

        def copies(k, block, to, from_input=False):
            s = _slot(block)
            return [pltpu.make_async_remote_copy(
                src_ref=ins[t] if from_input else place(t, s), dst_ref=place(t, s),
                send_sem=send_sems.at[nt * k + t], recv_sem=recv_sems.at[nt * k + t],
                device_id=to, device_id_type=MESH_ID) for t in range(nt)]

        mine = [pltpu.make_async_copy(ins[t], place(t, _slot(me)), local_sems.at[t]) for t in range(nt)]
        for cp in mine:
            cp.start()
        first = [copies(0, me, sibling, True)] + [copies(1 + j, me, (*chip, c), True) for j, chip in enumerate(chips)]
        for group in first:
            for cp in group:
                cp.start()
        passed = [copies(4 + j, (*chip, c), sibling) for j, chip in enumerate(chips)]
        for j, chip in enumerate(chips):
            for cp in copies(1 + j, (*chip, c), me):
                cp.wait_recv()
            for cp in passed[j]:
                cp.start()
        for cp in copies(0, sibling, me):
            cp.wait_recv()
        for j, chip in enumerate(chips):
            for cp in copies(4 + j, (*chip, 1 - c), me):
                cp.wait_recv()
        for group in first + passed:
            for cp in group:
                cp.wait_send()
        for cp in mine:
            cp.wait()

    return pl.pallas_call(
        body, name=name, in_specs=[HBM] * nt, out_specs=[HBM] * nt,
        out_shape=[jax.ShapeDtypeStruct(full_shape(a, cut), a.dtype) for a, cut in zip(shards, cuts)],
        scratch_shapes=[pltpu.SemaphoreType.DMA((7 * nt,)), pltpu.SemaphoreType.DMA((7 * nt,)),
                        pltpu.SemaphoreType.DMA((nt,))],
    )(*shards)


def _exchange_copies(items, ins, outs, send_sems, recv_sems, local_sems, gather=False):
    nt = len(items)
    x, y, c = _place()
    me = _slot((x, y, c))
    flip = lambda v, bit: 1 - v if bit else v

    def part(ref, cut, s):
        return ref.at[s] if cut is None else _cut(ref, cut[0], cut[1], s)

    def src(t, s):
        return ins[t] if gather else part(ins[t], items[t][0], s)

    def dst(t, s):
        if gather:
            return part(outs[items[t][1]], items[t][0], s)
        d = outs[items[t][1]].at[s]
        return d if items[t][2] is None else d.at[items[t][2]]

    sends, arrivals = [], []
    for k in range(1, N_DEV):
        peer = (flip(x, k & 4), flip(y, k & 2), flip(c, k & 1))
        ps = _slot(peer)
        for t in range(nt):
            sems = dict(send_sem=send_sems.at[nt * (k - 1) + t], recv_sem=recv_sems.at[nt * (k - 1) + t],
                        device_id=peer, device_id_type=MESH_ID)
            sends.append(pltpu.make_async_remote_copy(src_ref=src(t, ps), dst_ref=dst(t, me), **sems))
            arrivals.append(pltpu.make_async_remote_copy(src_ref=src(t, ps), dst_ref=dst(t, ps), **sems))
    mine = [pltpu.make_async_copy(src(t, me), dst(t, me), local_sems.at[t]) for t in range(nt)]
    return sends, arrivals, mine


SEMS = pl.BlockSpec(memory_space=pltpu.SEMAPHORE)
SIDE_EFFECT = pltpu.SideEffectType.DATAFLOW_SIDE_EFFECTING


def _exchange_start(srcs, items, landings, *, name, gather=False, carry=()):
    nt, nl, nc = len(srcs), len(landings), len(carry)

    def body(*refs):
        ins, lands = refs[:nt], refs[nt:nt + nl]
        send_sems, recv_sems, local_sems = refs[nt + nl + nc:nt + nl + nc + 3]
        token = refs[-1]
        sends, _, mine = _exchange_copies(items, ins, lands, send_sems, recv_sems, local_sems, gather)
        for cp in sends + mine:
            cp.start()
        token[...] = jnp.zeros_like(token)

    hbm = lambda a: pltpu.HBM(a.shape, a.dtype)
    fresh = [pltpu.with_memory_space_constraint(
        lax.empty(l.shape, l.dtype) if isinstance(l, jax.ShapeDtypeStruct) else l, pltpu.HBM) for l in landings]
    res = pl.pallas_call(
        body, name=name, in_specs=[HBM] * (nt + nl + nc),
        out_specs=[SEMS, SEMS, SEMS] + [HBM] * (nt + nl + nc) + [pl.BlockSpec(memory_space=pltpu.VMEM)],
        out_shape=[pltpu.SemaphoreType.DMA((7 * nt,)), pltpu.SemaphoreType.DMA((7 * nt,)), pltpu.SemaphoreType.DMA((nt,))]
        + [hbm(a) for a in srcs] + [hbm(l) for l in landings] + [hbm(a) for a in carry]
        + [jax.ShapeDtypeStruct((8, LANES), F32)],
        input_output_aliases={i: 3 + i for i in range(nt + nl + nc)},
        compiler_params=pltpu.CompilerParams(has_side_effects=SIDE_EFFECT),
    )(*[pltpu.with_memory_space_constraint(a, pltpu.HBM) for a in srcs], *fresh,
      *[pltpu.with_memory_space_constraint(a, pltpu.HBM) for a in carry])
    return res[:3 + nt + nl], res[-1][0:1, 0:1], res[3 + nt + nl:-1]


def _exchange_wait(state, items, nt, after, *, name, gather=False):
    sems, bufs = state[:3], state[3:]
    nl = len(bufs) - nt

    def body(*refs):
        ins, lands = refs[:nt], refs[nt:nt + nl]
        send_sems, recv_sems, local_sems = refs[nt + nl:nt + nl + 3]
        sends, arrivals, mine = _exchange_copies(items, ins, lands, send_sems, recv_sems, local_sems, gather)
        for sent, landed in zip(sends, arrivals):
            landed.wait_recv()
            sent.wait_send()
        for cp in mine:
            cp.wait()

    hbm = lambda a: pltpu.HBM(a.shape, a.dtype)
    res = pl.pallas_call(
        body, name=name, in_specs=[HBM] * (nt + nl) + [SEMS, SEMS, SEMS, HBM], out_specs=[HBM] * (nt + nl),
        out_shape=[hbm(a) for a in bufs],
        input_output_aliases={i: i for i in range(nt + nl)},
        compiler_params=pltpu.CompilerParams(has_side_effects=SIDE_EFFECT),
    )(*bufs, *sems, after)
    return res[nt:]


def _adamw(recv, w, m, v, *, split, name):
    shape = w.shape
    axis, parts = split
    block = shape[:axis] + (shape[axis] // parts,) + shape[axis + 1:]
    nd = len(shape)

    def body(r_ref, w_ref, m_ref, v_ref, g_ref, d_ref, nm_ref, nv_ref):
        g = r_ref[0].astype(F32)
        for j in range(1, N_DEV):
            g = g + r_ref[j].astype(F32)
        nm = ADAM_B1 * m_ref[...] + (1.0 - ADAM_B1) * g
        nv = ADAM_B2 * v_ref[...] + (1.0 - ADAM_B2) * (g * g)
        m_hat = nm / (1.0 - ADAM_B1 ** ADAM_STEP)
        v_hat = nv / (1.0 - ADAM_B2 ** ADAM_STEP)
        g_ref[...] = g
        nm_ref[...] = nm
        nv_ref[...] = nv
        d_ref[...] = -ADAM_LR * (m_hat / (jnp.sqrt(v_hat) + ADAM_EPS) + ADAM_WD * w_ref[...])

    at = lambda i: tuple(i if a == axis else 0 for a in range(nd))
    one = pl.BlockSpec(block, at)
    shp = jax.ShapeDtypeStruct(shape, F32)
    return pl.pallas_call(
        body, name=name, grid=(parts,),
        in_specs=[pl.BlockSpec((N_DEV,) + block, lambda i: (0,) + at(i)), one, one, one],
        out_specs=[one, one, one, one], out_shape=[shp, shp, shp, shp],
        compiler_params=_cparams("parallel"),
    )(recv, w, m, v)


BIG = ("w_in", "w_mem_kv", "w_out", "pool_w", "w_kv_shared")
SMALL = ("pool_scale", "ln_g", "ln_b", "b_forget")
SMALL_ROWS = 40
ADAM_SPLIT = {"w_in": (1, 4), "w_mem_kv": (0, 2), "w_out": (0, 2), "pool_w": (0, 1), "w_kv_shared": (0, 4)}
PER_LAYER_CUT = {"w_out": (0, D_MIX // N_DEV), "w_mem_kv": (0, D_MODEL // N_DEV), "w_in": (1, D_IN // N_DEV),
                 "pool_w": (1, POOL_GROUP // N_DEV)}
EARLY = ("w_out", "w_mem_kv", "w_in", "w_kv_shared")
EARLY_ITEMS = [(PER_LAYER_CUT[n], i, 1) for i, n in enumerate(EARLY[:3])] + [(None, 3, None)]


def _flat(parts, rows):
    v = jnp.concatenate([p.reshape(-1) for p in parts])
    return jnp.pad(v, (0, rows * LANES - v.shape[0])).reshape(rows, LANES)


def _unflat(flat, shapes):
    v, out, off = flat.reshape(-1), [], 0
    for s in shapes:
        n = math.prod(s)
        out.append(v[off:off + n].reshape(s))
        off += n
    return out


def kernel(x, mem, w_in, w_mem_kv, w_out, ln_g, ln_b, pool_w, pool_scale, w_kv_shared, b_forget, loss_target, m_w_in, m_w_mem_kv, m_w_out, m_ln_g, m_ln_b, m_pool_w, m_pool_scale, m_w_kv_shared, m_b_forget, v_w_in, v_w_mem_kv, v_w_out, v_ln_g, v_ln_b, v_pool_w, v_pool_scale, v_w_kv_shared, v_b_forget):
    w = dict(w_in=w_in, w_mem_kv=w_mem_kv, w_out=w_out, ln_g=ln_g, ln_b=ln_b, pool_w=pool_w[0],
             pool_scale=pool_scale, w_kv_shared=w_kv_shared, b_forget=b_forget)
    m = dict(w_in=m_w_in, w_mem_kv=m_w_mem_kv, w_out=m_w_out, ln_g=m_ln_g, ln_b=m_ln_b, pool_w=m_pool_w[0],
             pool_scale=m_pool_scale, w_kv_shared=m_w_kv_shared, b_forget=m_b_forget)
    v = dict(w_in=v_w_in, w_mem_kv=v_w_mem_kv, w_out=v_w_out, ln_g=v_ln_g, ln_b=v_ln_b, pool_w=v_pool_w[0],
             pool_scale=v_pool_scale, w_kv_shared=v_w_kv_shared, b_forget=v_b_forget)

    wb = {n: w[n].astype(BF16) for n in BIG}
    bdt = wb["w_in"].dtype
    win0, pscale = _all_gather([wb["w_in"][0], jnp.pad(pool_scale, ((0, 7), (0, 0)))], [PER_LAYER_CUT["w_in"], None],
                               name="gather_weights")
    pscale = pscale[:, 0, :].reshape(1, D_MAIN)
    next_srcs = [wb["w_mem_kv"], wb["pool_w"], wb["w_out"][0]]
    next_items = [(SLICED["w_mem_kv"], 0, None), (SLICED["pool_w"], 1, None), (PER_LAYER_CUT["w_out"], 2, None)]
    next_state, _, (win0,) = _exchange_start(
        next_srcs, next_items,
        [jax.ShapeDtypeStruct((2, D_MODEL, D_MODEL), bdt), jax.ShapeDtypeStruct((4, POOL_GROUP, POOL_GROUP), bdt),
         jax.ShapeDtypeStruct((D_MIX, D_MODEL), bdt)], name="gather_next_start", gather=True, carry=[win0])
    late_srcs = [wb["w_in"][1], wb["w_out"][1], wb["w_kv_shared"]]
    late_items = [(PER_LAYER_CUT["w_in"], 0, None), (PER_LAYER_CUT["w_out"], 1, None), (None, 2, None)]
    late = {}
    bias = jnp.pad(b_forget, (0, LANES - FOX_HEADS)).reshape(1, LANES)

    def next_weights(h0):
        wmkv, pw, wout0 = _exchange_wait(next_state, next_items, len(next_srcs), h0, name="gather_next_wait", gather=True)
        late["state"], _, (pw,) = _exchange_start(
            late_srcs, late_items,
            [jax.ShapeDtypeStruct((D_MODEL, D_IN), bdt), jax.ShapeDtypeStruct((D_MIX, D_MODEL), bdt),
             jax.ShapeDtypeStruct((N_DEV,) + w_kv_shared.shape, bdt)], name="gather_late_start", gather=True, carry=[pw])
        return wmkv, pw, wout0

    def late_weights(x1):
        win1, wout1, wkv = _exchange_wait(late["state"], late_items, len(late_srcs), x1, name="gather_late_wait",
                                          gather=True)
        wkv = jnp.pad(wkv.transpose(1, 0, 2).reshape(D_MODEL, -1), ((0, 0), (0, LANES - FOX_HEADS)))
        return win1, wout1, wkv, wkv[:, 2 * D_MAIN:]

    sent, recv = {}, {}
    first_items = [(PER_LAYER_CUT["w_out"], 0, 0), (PER_LAYER_CUT["w_mem_kv"], 1, 0), (PER_LAYER_CUT["pool_w"], 2, None)]
    rest_items = [(PER_LAYER_CUT["w_in"], 0, 0), (None, 1, None)]

    def send_layer1(g):
        srcs = [g[n] for n in EARLY]
        lands = [jax.ShapeDtypeStruct((N_DEV, 2) + w[n].shape[1:], g[n].dtype) for n in EARLY[:3]]
        lands.append(jax.ShapeDtypeStruct(g["w_kv_shared"].shape, g["w_kv_shared"].dtype))
        sent["layer1"], anchor, _ = _exchange_start(srcs, EARLY_ITEMS, lands, name="exchange_early_start")
        return anchor

    def landed_layer1(after):
        recv["w_out"], recv["w_mem_kv"], recv["w_in"], recv["w_kv_shared"] = _exchange_wait(
            sent["layer1"], EARLY_ITEMS, len(EARLY), after, name="exchange_early_wait")

    def send_layer0_first(g, carry):
        srcs = [g["w_out"], g["w_mem_kv"], g["pool_w"]]
        sent["first"], _, (carry,) = _exchange_start(
            srcs, first_items,
            [recv["w_out"], recv["w_mem_kv"], jax.ShapeDtypeStruct((N_DEV,) + w["pool_w"].shape, srcs[2].dtype)],
            name="exchange_mid_start", carry=[carry])
        return carry

    def send_layer0_rest(g, carry):
        small = jnp.concatenate([g["pool_scale"].reshape(N_DEV, -1)]
                                + [jnp.broadcast_to(g[n].reshape(1, -1), (N_DEV, g[n].size)) for n in SMALL[1:]], axis=1)
        small = jnp.pad(small, ((0, 0), (0, SMALL_ROWS * LANES - small.shape[1]))).reshape(N_DEV, SMALL_ROWS, LANES)
        sent["rest"], _, (carry,) = _exchange_start(
            [g["w_in"], small], rest_items, [recv["w_in"], jax.ShapeDtypeStruct(small.shape, small.dtype)],
            name="exchange_last_start", carry=[carry])
        return carry

    comm = types.SimpleNamespace(next_weights=next_weights, late_weights=late_weights, send_layer1=send_layer1,
                                 landed_layer1=landed_layer1, send_layer0_first=send_layer0_first,
                                 send_layer0_rest=send_layer0_rest)
    sq, gx = _local_step(x[0], mem[0], loss_target[0], win0, pscale, ln_g, ln_b, bias, comm)
    loss = lax.psum((0.5 / D_MODEL) * jnp.sum(sq), ("x", "y", "c"))

    recv["w_out"], recv["w_mem_kv"], recv["pool_w"] = _exchange_wait(sent["first"], first_items, len(first_items), gx,
                                                                     name="exchange_mid_wait")
    recv["w_in"], recv["small"] = _exchange_wait(sent["rest"], rest_items, len(rest_items), gx,
                                                 name="exchange_last_wait")

    outs = {}
    for n in BIG:
        res = _adamw(recv[n], w[n], m[n], v[n], split=ADAM_SPLIT[n], name="adamw_" + n)
        for kind, a in zip(("grad", "delta", "new_m", "new_v"), res):
            outs[kind, n] = a[None] if n == "pool_w" else a
    small_shapes = [w[n].shape for n in SMALL]
    res = _adamw(recv["small"], _flat([w[n] for n in SMALL], SMALL_ROWS), _flat([m[n] for n in SMALL], SMALL_ROWS),
                 _flat([v[n] for n in SMALL], SMALL_ROWS), split=(0, 1), name="adamw_small")
    for kind, flat in zip(("grad", "delta", "new_m", "new_v"), res):
        for n, a in zip(SMALL, _unflat(flat, small_shapes)):
            outs[kind, n] = a
    order = ("w_in", "w_mem_kv", "w_out", "ln_g", "ln_b", "pool_w", "pool_scale", "w_kv_shared", "b_forget")
    return (loss, gx[None], *[outs[kind, n] for kind in ("grad", "delta", "new_m", "new_v") for n in order])
```

```python
import math
import types

import numpy as np
import jax
import jax.numpy as jnp
from jax import lax
from jax.experimental import pallas as pl
from jax.experimental.pallas import tpu as pltpu

F32 = jnp.float32
BF16 = jnp.bfloat16

D_MODEL = 1024
D_MAIN = 1024
D_MEM = 512
D_MIX = D_MAIN + D_MEM
D_IN = 2 * D_MIX
N_MEM = 256
MEM_HEADS = 4
MEM_HEAD_DIM = 128
FOX_HEADS = 16
FOX_HEAD_DIM = 64
FOX_PAIRS = FOX_HEADS // 2
POOL_WINDOWS = (2, 4, 8, 16)
POOL_GROUP = 256
POOL_HALO = 16
ALPHA = 4.0 ** 0.25
LN_EPS = 1e-5
NEG = -1e30
LANES = 128
N_DEV = 8

ADAM_LR = 0.001
ADAM_B1 = 0.9
ADAM_B2 = 0.999
ADAM_EPS = 1e-08
ADAM_WD = 0.01
ADAM_STEP = 10

VMEM_LIMIT = 56 * 1024 * 1024

NN = (((1,), (0,)), ((), ()))
NT = (((1,), (1,)), ((), ()))
TN = (((0,), (0,)), ((), ()))


def _cparams(*sem):
    return pltpu.CompilerParams(dimension_semantics=sem, vmem_limit_bytes=VMEM_LIMIT)


def _sigmoid(z):
    return 1.0 / (1.0 + jnp.exp(-z))


def _mm(a, b, *, mode, out_dtype, tm, tn, tk, name, add=None, add_scale=1.0):
    if mode == "nn":
        (M, K), (K2, N) = a.shape, b.shape
    elif mode == "nt":
        (M, K), (N, K2) = a.shape, b.shape
    else:
        (K, M), (K2, N) = a.shape, b.shape
    assert K == K2, (a.shape, b.shape, mode)
    tm, tn, tk = min(tm, M), min(tn, N), min(tk, K)
    assert M % tm == 0 and N % tn == 0 and K % tk == 0, (M, N, K, tm, tn, tk)
    gm, gn, gk = M // tm, N // tn, K // tk
    dims = {"nn": NN, "nt": NT, "tn": TN}[mode]
    if mode == "tn":
        a_spec = pl.BlockSpec((tk, tm), lambda i, j, k: (k, i))
    else:
        a_spec = pl.BlockSpec((tm, tk), lambda i, j, k: (i, k))
    if mode == "nt":
        b_spec = pl.BlockSpec((tn, tk), lambda i, j, k: (j, k))
    else:
        b_spec = pl.BlockSpec((tk, tn), lambda i, j, k: (k, j))
    o_spec = pl.BlockSpec((tm, tn), lambda i, j, k: (i, j))
    has_add = add is not None
    acc_in_out = out_dtype == F32

    def body(*refs):
        a_ref, b_ref = refs[0], refs[1]
        add_ref = refs[2] if has_add else None
        o_ref = refs[3] if has_add else refs[2]
        prod = lax.dot_general(a_ref[...].astype(BF16), b_ref[...].astype(BF16), dims,
                               preferred_element_type=F32)

        def finish(r):
            if has_add:
                r = r + add_scale * add_ref[...]
            o_ref[...] = r.astype(out_dtype)

        if gk == 1:
            finish(prod)
        else:
            acc_ref = o_ref if acc_in_out else refs[-1]
            k = pl.program_id(2)

            @pl.when(k == 0)
            def _():
                acc_ref[...] = prod

            @pl.when(k > 0)
            def _():
                acc_ref[...] += prod

            if has_add or not acc_in_out:
                @pl.when(k == gk - 1)
                def _():
                    finish(acc_ref[...])

    in_specs = [a_spec, b_spec] + ([o_spec] if has_add else [])
    args = (a, b) + ((add,) if has_add else ())
    return pl.pallas_call(
        body, name=name, grid=(gm, gn, gk), in_specs=in_specs, out_specs=o_spec,
        out_shape=jax.ShapeDtypeStruct((M, N), out_dtype),
        scratch_shapes=[pltpu.VMEM((tm, tn), F32)] if gk > 1 and not acc_in_out else [],
        compiler_params=_cparams("parallel", "parallel", "arbitrary"),
    )(*args)


def _ln_stats(z):
    mu = jnp.mean(z, axis=1, keepdims=True)
    zc = z - mu
    var = jnp.mean(zc * zc, axis=1, keepdims=True)
    rstd = lax.rsqrt(var + LN_EPS)
    return zc * rstd, rstd


def _ln_bwd_math(dy, xhat, rstd, g):
    dxh = dy * g
    m1 = jnp.mean(dxh, axis=1, keepdims=True)
    m2 = jnp.mean(dxh * xhat, axis=1, keepdims=True)
    return rstd * (dxh - m1 - xhat * m2)


def _out_ln(yg, wout, x, g, b, *, tb, name):
    S = x.shape[0]
    tb = min(tb, S)

    def body(yg_ref, w_ref, x_ref, g_ref, b_ref, y_ref, xhat_ref, rstd_ref, yt_ref):
        o = jnp.dot(yg_ref[...], w_ref[...], preferred_element_type=F32)
        xhat, rstd = _ln_stats(ALPHA * x_ref[...] + o)
        xhat_ref[...] = xhat
        rstd_ref[...] = rstd
        y = xhat * g_ref[...] + b_ref[...]
        y_ref[...] = y
        yt_ref[...] = y.T.astype(BF16)

    row = pl.BlockSpec((tb, D_MODEL), lambda i: (i, 0))
    vec = pl.BlockSpec((1, D_MODEL), lambda i: (0, 0))
    return pl.pallas_call(
        body, name=name, grid=(S // tb,),
        in_specs=[pl.BlockSpec((tb, D_MIX), lambda i: (i, 0)), pl.BlockSpec((D_MIX, D_MODEL), lambda i: (0, 0)),
                  row, vec, vec],
        out_specs=[row, row, pl.BlockSpec((tb, 1), lambda i: (i, 0)), pl.BlockSpec((D_MODEL, tb), lambda i: (0, i))],
        out_shape=[jax.ShapeDtypeStruct((S, D_MODEL), F32), jax.ShapeDtypeStruct((S, D_MODEL), F32),
                   jax.ShapeDtypeStruct((S, 1), F32), jax.ShapeDtypeStruct((D_MODEL, S), BF16)],
        compiler_params=_cparams("parallel"),
    )(yg, wout, x, g, b)


def _in_proj_t(x, w, *, tm, name):
    S = x.shape[0]
    tm = min(tm, S)

    def body(x_ref, w_ref, h_ref, xt_ref):
        xv = x_ref[...]
        h_ref[...] = jnp.dot(xv.astype(BF16), w_ref[...], preferred_element_type=F32).astype(BF16)
        xt_ref[...] = xv.T.astype(BF16)

    return pl.pallas_call(
        body, name=name, grid=(S // tm,),
        in_specs=[pl.BlockSpec((tm, D_MODEL), lambda i: (i, 0)), pl.BlockSpec((D_MODEL, D_IN), lambda i: (0, 0))],
        out_specs=[pl.BlockSpec((tm, D_IN), lambda i: (i, 0)), pl.BlockSpec((D_MODEL, tm), lambda i: (0, i))],
        out_shape=[jax.ShapeDtypeStruct((S, D_IN), BF16), jax.ShapeDtypeStruct((D_MODEL, S), BF16)],
        compiler_params=_cparams("parallel"),
    )(x, w)


def _out_ln_loss(yg, wout, x, g, b, target, *, tb, name):
    S = x.shape[0]
    tb = min(tb, S)

    def body(yg_ref, w_ref, x_ref, g_ref, b_ref, t_ref, dz_ref, dg_ref, db_ref, sq_ref):
        @pl.when(pl.program_id(0) == 0)
        def _():
            dg_ref[...] = jnp.zeros_like(dg_ref)
            db_ref[...] = jnp.zeros_like(db_ref)
            sq_ref[...] = jnp.zeros_like(sq_ref)

        o = jnp.dot(yg_ref[...], w_ref[...], preferred_element_type=F32)
        xhat, rstd = _ln_stats(ALPHA * x_ref[...] + o)
        err = xhat * g_ref[...] + b_ref[...] - t_ref[...]
        sq_ref[...] += jnp.sum(err * err, axis=0, keepdims=True)
        dy = err * (1.0 / D_MODEL)
        dz_ref[...] = _ln_bwd_math(dy, xhat, rstd, g_ref[...])
        dg_ref[...] += jnp.sum(dy * xhat, axis=0, keepdims=True)
        db_ref[...] += jnp.sum(dy, axis=0, keepdims=True)

    row = pl.BlockSpec((tb, D_MODEL), lambda i: (i, 0))
    vec = pl.BlockSpec((1, D_MODEL), lambda i: (0, 0))
    vshape = jax.ShapeDtypeStruct((1, D_MODEL), F32)
    return pl.pallas_call(
        body, name=name, grid=(S // tb,),
        in_specs=[pl.BlockSpec((tb, D_MIX), lambda i: (i, 0)), pl.BlockSpec((D_MIX, D_MODEL), lambda i: (0, 0)),
                  row, vec, vec, row],
        out_specs=[row, vec, vec, vec],
        out_shape=[jax.ShapeDtypeStruct((S, D_MODEL), F32), vshape, vshape, vshape],
        compiler_params=_cparams("arbitrary"),
    )(yg, wout, x, g, b, target)


def _ln_bwd(dy, xhat, rstd, g, *, tb, name):
    S = dy.shape[0]
    tb = min(tb, S)

    def body(dy_ref, xhat_ref, rstd_ref, g_ref, dz_ref, dg_ref, db_ref):
        @pl.when(pl.program_id(0) == 0)
        def _():
            dg_ref[...] = jnp.zeros_like(dg_ref)
            db_ref[...] = jnp.zeros_like(db_ref)

        dy_, xhat_ = dy_ref[...], xhat_ref[...]
        dz_ref[...] = _ln_bwd_math(dy_, xhat_, rstd_ref[...], g_ref[...])
        dg_ref[...] += jnp.sum(dy_ * xhat_, axis=0, keepdims=True)
        db_ref[...] += jnp.sum(dy_, axis=0, keepdims=True)

    row = pl.BlockSpec((tb, D_MODEL), lambda i: (i, 0))
    vec = pl.BlockSpec((1, D_MODEL), lambda i: (0, 0))
    return pl.pallas_call(
        body, name=name, grid=(S // tb,),
        in_specs=[row, row, pl.BlockSpec((tb, 1), lambda i: (i, 0)), vec],
        out_specs=[row, vec, vec],
        out_shape=[jax.ShapeDtypeStruct((S, D_MODEL), F32), jax.ShapeDtypeStruct((1, D_MODEL), F32),
                   jax.ShapeDtypeStruct((1, D_MODEL), F32)],
        compiler_params=_cparams("arbitrary"),
    )(dy, xhat, rstd, g)


def _gate_fwd(ysrc, scale, h, ymem, *, tb, name):
    S = ysrc.shape[0]
    tb = min(tb, S)

    def body(ys_ref, sc_ref, ga_ref, gb_ref, gc_ref, ym_ref, yg_ref):
        ymain = ys_ref[...] * sc_ref[...]
        for k, g_ref in enumerate((ga_ref, gb_ref)):
            gv = g_ref[...].astype(F32)
            yg_ref[:, 512 * k:512 * (k + 1)] = (ymain[:, 512 * k:512 * (k + 1)] * gv * _sigmoid(gv)).astype(BF16)
        gv = gc_ref[...].astype(F32)
        yg_ref[:, 1024:1536] = (ym_ref[...] * gv * _sigmoid(gv)).astype(BF16)

    slab = lambda c: pl.BlockSpec((tb, 512), lambda i, c=c: (i, c))
    return pl.pallas_call(
        body, name=name, grid=(S // tb,),
        in_specs=[pl.BlockSpec((tb, D_MAIN), lambda i: (i, 0)), pl.BlockSpec((1, D_MAIN), lambda i: (0, 0)),
                  slab(3), slab(4), slab(5), pl.BlockSpec((tb, D_MEM), lambda i: (i, 0))],
        out_specs=pl.BlockSpec((tb, D_MIX), lambda i: (i, 0)),
        out_shape=jax.ShapeDtypeStruct((S, D_MIX), BF16),
        compiler_params=_cparams("parallel"),
    )(ysrc, scale, h, h, h, ymem)


def _gate_bwd(dz, wout, ysrc, scale, h, ymem, *, tb, name, fox=False):
    S = ysrc.shape[0]
    tb = min(tb, S)

    def dsilu(gv):
        sg = _sigmoid(gv)
        return sg, sg * (1.0 + gv * (1.0 - sg))

    def body(dz_ref, w_ref, ys_ref, sc_ref, ga_ref, gb_ref, gc_ref, ym_ref, dym_ref, dymem_ref, dh_ref):
        dyg = lax.dot_general(dz_ref[...].astype(BF16), w_ref[...], NT, preferred_element_type=F32)
        ymain = ys_ref[...] * sc_ref[...]
        lane = lax.broadcasted_iota(jnp.int32, (tb, LANES), 1)
        first = lane < FOX_HEAD_DIM
        for k, g_ref in enumerate((ga_ref, gb_ref)):
            gv, d = g_ref[...].astype(F32), dyg[:, 512 * k:512 * (k + 1)]
            sg, ds = dsilu(gv)
            dy = d * gv * sg
            dh_ref[:, 512 * k:512 * (k + 1)] = (d * ymain[:, 512 * k:512 * (k + 1)] * ds).astype(BF16)
            if not fox:
                dym_ref[:, 512 * k:512 * (k + 1)] = dy
                continue
            for q in range(512 // LANES):
                cols = slice(LANES * q, LANES * (q + 1))
                dy2 = dy[:, cols]
                prod = dy2 * ymain[:, 512 * k + LANES * q:512 * k + LANES * (q + 1)]
                for hh in range(2):
                    delta = jnp.sum(jnp.where(first == (hh == 0), prod, 0.0), axis=1, keepdims=True)
                    dyh = dy2 if hh == 0 else _swap_halves(dy2)
                    c0 = LANES * (2 * (4 * k + q) + hh)
                    dym_ref[:, c0:c0 + LANES] = jnp.where(
                        first, dyh, _lanes3(lane, AUX, _split3(-delta), 0.0)).astype(BF16)
        gv, d = gc_ref[...].astype(F32), dyg[:, 1024:1536]
        sg, ds = dsilu(gv)
        dymem_ref[...] = d * gv * sg
        dh_ref[:, 1024:1536] = (d * ym_ref[...] * ds).astype(BF16)

    slab = lambda c: pl.BlockSpec((tb, 512), lambda i, c=c: (i, c))
    return pl.pallas_call(
        body, name=name, grid=(S // tb,),
        in_specs=[pl.BlockSpec((tb, D_MODEL), lambda i: (i, 0)), pl.BlockSpec((D_MIX, D_MODEL), lambda i: (0, 0)),
                  pl.BlockSpec((tb, D_MAIN), lambda i: (i, 0)), pl.BlockSpec((1, D_MAIN), lambda i: (0, 0)),
                  slab(3), slab(4), slab(5), pl.BlockSpec((tb, D_MEM), lambda i: (i, 0))],
        out_specs=[pl.BlockSpec((tb, 2 * D_MAIN if fox else D_MAIN), lambda i: (i, 0)),
                   pl.BlockSpec((tb, D_MEM), lambda i: (i, 0)), pl.BlockSpec((tb, D_MIX), lambda i: (i, 1))],
        out_shape=[jax.ShapeDtypeStruct((S, 2 * D_MAIN), BF16) if fox else jax.ShapeDtypeStruct((S, D_MAIN), F32),
                   jax.ShapeDtypeStruct((S, D_MEM), F32), jax.ShapeDtypeStruct((S, D_IN), BF16)],
        compiler_params=_cparams("parallel"),
    )(dz, wout, ysrc, scale, h, h, h, ymem)


def _window_count(t0, rows, w):
    t = t0 + lax.broadcasted_iota(jnp.int32, (rows, POOL_GROUP), 0)
    return jnp.minimum(t + 1, w).astype(F32)


def _pool_fwd(h, pw, *, tb, name):
    S = h.shape[0]
    tb = min(tb, S)

    def body(u_ref, pw_ref, pm_ref, mixed_ref, tail_ref):
        i = pl.program_id(0)

        @pl.when(i == 0)
        def _():
            tail_ref[...] = jnp.zeros_like(tail_ref)

        u = u_ref[...].astype(F32)
        xfull = jnp.concatenate([tail_ref[...], u], axis=0)
        for gi, w in enumerate(POOL_WINDOWS):
            cols = slice(POOL_GROUP * gi, POOL_GROUP * (gi + 1))
            s = xfull[:, cols]
            sh = 1
            while sh < w:
                s = s + pltpu.roll(s, sh, 0)
                sh *= 2
            pm = s[POOL_HALO:, :] / _window_count(i * tb, tb, w) - u[:, cols]
            pmb = pm.astype(BF16)
            pm_ref[:, cols] = pmb
            mixed_ref[:, cols] = jnp.dot(pmb, pw_ref[gi], preferred_element_type=F32)
        tail_ref[...] = u[tb - POOL_HALO:, :]

    return pl.pallas_call(
        body, name=name, grid=(S // tb,),
        in_specs=[pl.BlockSpec((tb, D_MAIN), lambda i: (i, 0)),
                  pl.BlockSpec((4, POOL_GROUP, POOL_GROUP), lambda i: (0, 0, 0))],
        out_specs=[pl.BlockSpec((tb, D_MAIN), lambda i: (i, 0)), pl.BlockSpec((tb, D_MAIN), lambda i: (i, 0))],
        out_shape=[jax.ShapeDtypeStruct((S, D_MAIN), BF16), jax.ShapeDtypeStruct((S, D_MAIN), F32)],
        scratch_shapes=[pltpu.VMEM((POOL_HALO, D_MAIN), F32)],
        compiler_params=_cparams("arbitrary"),
    )(h, pw)


def _pool_bwd(dymain, pm, mixed, pw, scale, dh, *, tb, name):
    S = dymain.shape[0]
    tb = min(tb, S)
    nb = S // tb
    n = tb + POOL_HALO

    def body(dy_ref, pm_ref, mixed_ref, pw_ref, sc_ref, dh_in, dh_ref, dpw_ref, dsc_ref, head_ref, dpw_acc):
        del dh_in
        i = pl.program_id(0)

        @pl.when(i == 0)
        def _():
            head_ref[...] = jnp.zeros_like(head_ref)
            dpw_acc[...] = jnp.zeros_like(dpw_acc)
            dsc_ref[...] = jnp.zeros_like(dsc_ref)

        dy = dy_ref[...]
        dsc_ref[...] += jnp.sum(dy * mixed_ref[...], axis=0, keepdims=True)
        dmixed = dy * sc_ref[...]
        t0 = (nb - 1 - i) * tb
        for gi, w in enumerate(POOL_WINDOWS):
            cols = slice(POOL_GROUP * gi, POOL_GROUP * (gi + 1))
            dm = dmixed[:, cols].astype(BF16)
            dpw_acc[gi] += lax.dot_general(pm_ref[:, cols], dm, TN, preferred_element_type=F32)
            dpm = lax.dot_general(dm, pw_ref[gi], NT, preferred_element_type=F32)
            e = dpm / _window_count(t0, tb, w)
            s = jnp.concatenate([e, head_ref[:, cols]], axis=0)
            sh = 1
            while sh < w:
                s = s + pltpu.roll(s, n - sh, 0)
                sh *= 2
            dh_ref[:, cols] = (s[:tb, :] - dpm).astype(BF16)
            head_ref[:, cols] = e[:POOL_HALO, :]

        @pl.when(i == nb - 1)
        def _():
            dpw_ref[...] = dpw_acc[...].astype(BF16)

    rev = lambda i: (nb - 1 - i, 0)
    return pl.pallas_call(
        body, name=name, grid=(nb,),
        in_specs=[pl.BlockSpec((tb, D_MAIN), rev), pl.BlockSpec((tb, D_MAIN), rev), pl.BlockSpec((tb, D_MAIN), rev),
                  pl.BlockSpec((4, POOL_GROUP, POOL_GROUP), lambda i: (0, 0, 0)),
                  pl.BlockSpec((1, D_MAIN), lambda i: (0, 0)), pl.BlockSpec(memory_space=pl.ANY)],
        out_specs=[pl.BlockSpec((tb, D_MAIN), rev),
                   pl.BlockSpec((4, POOL_GROUP, POOL_GROUP), lambda i: (0, 0, 0)),
                   pl.BlockSpec((1, D_MAIN), lambda i: (0, 0))],
        out_shape=[jax.ShapeDtypeStruct(dh.shape, dh.dtype),
                   jax.ShapeDtypeStruct((4, POOL_GROUP, POOL_GROUP), BF16), jax.ShapeDtypeStruct((1, D_MAIN), F32)],
        scratch_shapes=[pltpu.VMEM((POOL_HALO, D_MAIN), F32), pltpu.VMEM((4, POOL_GROUP, POOL_GROUP), F32)],
        input_output_aliases={5: 0},
        compiler_params=_cparams("arbitrary"),
    )(dymain, pm, mixed, pw, scale, dh)


MEM_SCALE = MEM_HEAD_DIM ** -0.5


def _mem_probs(q_ref, mkv_ref, hd):
    cols = slice(MEM_HEAD_DIM * hd, MEM_HEAD_DIM * (hd + 1))
    q = (q_ref[:, cols].astype(F32) * MEM_SCALE).astype(BF16)
    mk = mkv_ref[:, cols].astype(BF16)
    mv = mkv_ref[:, D_MEM + MEM_HEAD_DIM * hd:D_MEM + MEM_HEAD_DIM * (hd + 1)].astype(BF16)
    s = lax.dot_general(q, mk, NT, preferred_element_type=F32)
    e = jnp.exp(s - jnp.max(s, axis=1, keepdims=True))
    return cols, q, mk, mv, e, jnp.sum(e, axis=1, keepdims=True)


def _memattn_fwd(h, mkv, *, tb, name):
    S = h.shape[0]
    tb = min(tb, S)

    def body(q_ref, mkv_ref, y_ref):
        for hd in range(MEM_HEADS):
            cols, _, _, mv, e, l = _mem_probs(q_ref, mkv_ref, hd)
            y_ref[:, cols] = jnp.dot(e.astype(BF16), mv, preferred_element_type=F32) / l

    return pl.pallas_call(
        body, name=name, grid=(S // tb,),
        in_specs=[pl.BlockSpec((tb, D_MEM), lambda i: (i, 2)), pl.BlockSpec((N_MEM, 2 * D_MEM), lambda i: (0, 0))],
        out_specs=pl.BlockSpec((tb, D_MEM), lambda i: (i, 0)),
        out_shape=jax.ShapeDtypeStruct((S, D_MEM), F32),
        compiler_params=_cparams("parallel"),
    )(h, mkv)


def _memattn_bwd(h, mkv, dy, dh, *, tb, name):
    S = h.shape[0]
    tb = min(tb, S)

    def body(q_ref, mkv_ref, dy_ref, dh_in, dh_ref, dmkv_ref):
        del dh_in

        @pl.when(pl.program_id(0) == 0)
        def _():
            dmkv_ref[...] = jnp.zeros_like(dmkv_ref)

        for hd in range(MEM_HEADS):
            cols, q, mk, mv, e, l = _mem_probs(q_ref, mkv_ref, hd)
            p = e / l
            dyh = dy_ref[:, cols].astype(BF16)
            dp = lax.dot_general(dyh, mv, NT, preferred_element_type=F32)
            ds = p * (dp - jnp.sum(dp * p, axis=1, keepdims=True))
            dsb = ds.astype(BF16)
            dh_ref[:, cols] = (jnp.dot(dsb, mk, preferred_element_type=F32) * MEM_SCALE).astype(BF16)
            dmkv_ref[:, cols] += lax.dot_general(dsb, q, TN, preferred_element_type=F32)
            vcols = slice(D_MEM + MEM_HEAD_DIM * hd, D_MEM + MEM_HEAD_DIM * (hd + 1))
            dmkv_ref[:, vcols] += lax.dot_general(p.astype(BF16), dyh, TN, preferred_element_type=F32)

    return pl.pallas_call(
        body, name=name, grid=(S // tb,),
        in_specs=[pl.BlockSpec((tb, D_MEM), lambda i: (i, 2)), pl.BlockSpec((N_MEM, 2 * D_MEM), lambda i: (0, 0)),
                  pl.BlockSpec((tb, D_MEM), lambda i: (i, 0)), pl.BlockSpec(memory_space=pl.ANY)],
        out_specs=[pl.BlockSpec((tb, D_MEM), lambda i: (i, 2)), pl.BlockSpec((N_MEM, 2 * D_MEM), lambda i: (0, 0))],
        out_shape=[jax.ShapeDtypeStruct(dh.shape, dh.dtype), jax.ShapeDtypeStruct((N_MEM, 2 * D_MEM), F32)],
        input_output_aliases={3: 0},
        compiler_params=_cparams("arbitrary"),
    )(h, mkv, dy, dh)


def _forget_fwd(fl, bias, *, tb, name):
    S = fl.shape[0]
    tb = min(tb, S)

    def body(fl_ref, b_ref, o_ref, carry_ref):
        @pl.when(pl.program_id(0) == 0)
        def _():
            carry_ref[...] = jnp.zeros_like(carry_ref)

        z = fl_ref[...] + b_ref[...]
        lf = jnp.minimum(z, 0.0) - jnp.log(1.0 + jnp.exp(-jnp.abs(z)))
        row = lax.broadcasted_iota(jnp.int32, (tb, LANES), 0)
        c = lf
        sh = 1
        while sh < tb:
            c = c + jnp.where(row >= sh, pltpu.roll(c, sh, 0), 0.0)
            sh *= 2
        o_ref[...] = -(carry_ref[...] + c)
        carry_ref[...] += jnp.sum(lf, axis=0, keepdims=True)

    return pl.pallas_call(
        body, name=name, grid=(S // tb,),
        in_specs=[pl.BlockSpec((tb, LANES), lambda i: (i, 0)), pl.BlockSpec((1, LANES), lambda i: (0, 0))],
        out_specs=pl.BlockSpec((tb, LANES), lambda i: (i, 0)),
        out_shape=jax.ShapeDtypeStruct((S, LANES), F32),
        scratch_shapes=[pltpu.VMEM((1, LANES), F32)],
        compiler_params=_cparams("arbitrary"),
    )(fl, bias)


def _forget_bwd(dn, drow, fl, bias, *, tb, name):
    S = fl.shape[0]
    tb = min(tb, S)
    nb = S // tb

    def body(dn_ref, dr_ref, fl_ref, b_ref, dh_ref, db_ref, carry_ref):
        @pl.when(pl.program_id(0) == 0)
        def _():
            carry_ref[...] = jnp.zeros_like(carry_ref)
            db_ref[...] = jnp.zeros_like(db_ref)

        src = lax.broadcasted_iota(jnp.int32, (D_MAIN, LANES), 0)
        head = lax.broadcasted_iota(jnp.int32, (D_MAIN, LANES), 1)
        pick = lambda off: jnp.where((src == FOX_HEAD_DIM * head + off) & (head < FOX_HEADS), 1.0, 0.0).astype(BF16)
        hdot = lambda a, sel: sum(jnp.dot(part.astype(BF16), sel, preferred_element_type=F32) for part in _split3(a))
        dcum = hdot(dr_ref[...], pick(3)) - hdot(dn_ref[...], pick(0))
        row = lax.broadcasted_iota(jnp.int32, (tb, LANES), 0)
        c = dcum
        sh = 1
        while sh < tb:
            c = c + jnp.where(row < tb - sh, pltpu.roll(c, tb - sh, 0), 0.0)
            sh *= 2
        dlf = carry_ref[...] + c
        carry_ref[...] += jnp.sum(dcum, axis=0, keepdims=True)
        z = fl_ref[...] + b_ref[...]
        lane = lax.broadcasted_iota(jnp.int32, (tb, LANES), 1)
        dfl = jnp.where(lane < FOX_HEADS, dlf / (1.0 + jnp.exp(z)), 0.0)
        db_ref[...] += jnp.sum(dfl, axis=0, keepdims=True)
        dh_ref[...] = dfl.astype(BF16)

    rev = lambda i: (nb - 1 - i, 0)
    return pl.pallas_call(
        body, name=name, grid=(nb,),
        in_specs=[pl.BlockSpec((tb, D_MAIN), rev), pl.BlockSpec((tb, D_MAIN), rev), pl.BlockSpec((tb, LANES), rev),
                  pl.BlockSpec((1, LANES), lambda i: (0, 0))],
        out_specs=[pl.BlockSpec((tb, LANES), rev), pl.BlockSpec((1, LANES), lambda i: (0, 0))],
        out_shape=[jax.ShapeDtypeStruct((S, LANES), BF16), jax.ShapeDtypeStruct((1, LANES), F32)],
        scratch_shapes=[pltpu.VMEM((1, LANES), F32)],
        compiler_params=_cparams("arbitrary"),
    )(dn, drow, fl, bias)


FOX_SCALE = FOX_HEAD_DIM ** -0.5
LOG2E = 1.4426950408889634
LN2 = 0.6931471805599453
AUX = FOX_HEAD_DIM


def _split3(x):
    hi = x.astype(BF16).astype(F32)
    r = x - hi
    mid = r.astype(BF16).astype(F32)
    return hi, mid, (r - mid).astype(BF16).astype(F32)


def _aux_placement():
    r = np.arange(3 * LANES)[:, None]
    c = np.arange(FOX_HEADS * LANES)[None, :]
    return jnp.asarray((r % LANES == c // LANES) & (c % LANES == AUX + r // LANES), BF16)


def _aux_lanes(vals, place_ref):
    parts = jnp.concatenate(_split3(vals), axis=1).astype(BF16)
    return jnp.dot(parts, place_ref[...], preferred_element_type=F32)


def _lanes3(lane, base, parts, rest):
    return jnp.where(lane == base, parts[0], jnp.where(lane == base + 1, parts[1],
                                                       jnp.where(lane == base + 2, parts[2], rest)))


def _swap_halves(x):
    return pltpu.roll(x, FOX_HEAD_DIM, 1)


def _causal_steps(nq, keys_outer):
    if keys_outer:
        pairs = [(i, j) for j in range(nq) for i in range(j, nq)]
    else:
        pairs = [(i, j) for i in range(nq) for j in range(i + 1)]
    it, jt = zip(*pairs)
    return jnp.asarray(np.array(it, np.int32)), jnp.asarray(np.array(jt, np.int32))


def _in_proj_fox(x, w, *, tm, name):
    S = x.shape[0]
    tm = min(tm, S)

    def body(x_ref, w_ref, h_ref, qa_ref):
        acc = jnp.dot(x_ref[...].astype(BF16), w_ref[...], preferred_element_type=F32)
        h_ref[...] = acc.astype(BF16)
        lane = lax.broadcasted_iota(jnp.int32, (tm, LANES), 1)
        first = lane < FOX_HEAD_DIM
        ones_q = jnp.where((lane >= AUX) & (lane < AUX + 3), 1.0, 0.0)
        for g in range(FOX_PAIRS):
            q = acc[:, LANES * g:LANES * (g + 1)] * (FOX_SCALE * LOG2E)
            qa_ref[:, 2 * LANES * g:2 * LANES * g + LANES] = jnp.where(first, q, ones_q).astype(BF16)
            qa_ref[:, 2 * LANES * g + LANES:2 * LANES * (g + 1)] = jnp.where(first, _swap_halves(q), ones_q).astype(BF16)

    return pl.pallas_call(
        body, name=name, grid=(S // tm,),
        in_specs=[pl.BlockSpec((tm, D_MODEL), lambda i: (i, 0)), pl.BlockSpec((D_MODEL, D_IN), lambda i: (0, 0))],
        out_specs=[pl.BlockSpec((tm, D_IN), lambda i: (i, 0)), pl.BlockSpec((tm, 2 * D_MAIN), lambda i: (i, 0))],
        out_shape=[jax.ShapeDtypeStruct((S, D_IN), BF16), jax.ShapeDtypeStruct((S, 2 * D_MAIN), BF16)],
        compiler_params=_cparams("parallel"),
    )(x, w)


def _kv_proj_fox(x, wkv, negcum, *, tm, name):
    S = x.shape[0]
    tm = min(tm, S)

    def body(x_ref, w_ref, nc_ref, place_ref, ka_ref, va_ref):
        acc = jnp.dot(x_ref[...].astype(BF16), w_ref[...], preferred_element_type=F32)
        aux = _aux_lanes(nc_ref[...] * LOG2E, place_ref)
        lane = lax.broadcasted_iota(jnp.int32, (tm, LANES), 1)
        first = lane < FOX_HEAD_DIM
        ones_k = jnp.where((lane >= AUX + 3) & (lane < AUX + 6), 1.0, 0.0)
        for g in range(FOX_PAIRS):
            k = acc[:, LANES * g:LANES * (g + 1)]
            v = acc[:, D_MAIN + LANES * g:D_MAIN + LANES * (g + 1)]
            for hh in range(2):
                sl = slice(LANES * (2 * g + hh), LANES * (2 * g + hh + 1))
                kh, vh = (k, v) if hh == 0 else (_swap_halves(k), _swap_halves(v))
                ka_ref[:, sl] = jnp.where(first, kh, aux[:, sl] + ones_k).astype(BF16)
                va_ref[:, sl] = jnp.where(first, vh, 1.0).astype(BF16)

    out = pl.BlockSpec((tm, 2 * D_MAIN), lambda i: (i, 0))
    shp = jax.ShapeDtypeStruct((S, 2 * D_MAIN), BF16)
    return pl.pallas_call(
        body, name=name, grid=(S // tm,),
        in_specs=[pl.BlockSpec((tm, D_MODEL), lambda i: (i, 0)), pl.BlockSpec((D_MODEL, 2 * D_MAIN), lambda i: (0, 0)),
                  pl.BlockSpec((tm, LANES), lambda i: (i, 0)), pl.BlockSpec((3 * LANES, 2 * D_MAIN), lambda i: (0, 0))],
        out_specs=[out, out], out_shape=[shp, shp],
        compiler_params=_cparams("parallel"),
    )(x, wkv, negcum, _aux_placement())


def _dwkv(xt, dk, dv, dfl, *, tk, name):
    S = xt.shape[1]
    tk = min(tk, S)

    def body(x_ref, dk_ref, dv_ref, df_ref, o_ref):
        @pl.when(pl.program_id(0) == 0)
        def _():
            o_ref[...] = jnp.zeros_like(o_ref)

        for b_ref, c0 in ((dk_ref, 0), (dv_ref, D_MAIN), (df_ref, 2 * D_MAIN)):
            o_ref[:, c0:c0 + b_ref.shape[1]] += jnp.dot(x_ref[...], b_ref[...], preferred_element_type=F32)

    row = lambda n: pl.BlockSpec((tk, n), lambda k: (k, 0))
    return pl.pallas_call(
        body, name=name, grid=(S // tk,),
        in_specs=[pl.BlockSpec((D_MODEL, tk), lambda k: (0, k)), row(D_MAIN), row(D_MAIN), row(LANES)],
        out_specs=pl.BlockSpec((D_MODEL, 2 * D_MAIN + LANES), lambda k: (0, 0)),
        out_shape=jax.ShapeDtypeStruct((D_MODEL, 2 * D_MAIN + LANES), F32),
        compiler_params=_cparams("arbitrary"),
    )(xt, dk, dv, dfl)


def _dx1(dh, win, dk, dv, dfl, wkv, dz, *, tm, name):
    S = dh.shape[0]
    tm = min(tm, S)

    def body(dh_ref, win_ref, dk_ref, dv_ref, df_ref, wk_ref, wv_ref, wf_ref, dz_ref, o_ref):
        acc = ALPHA * dz_ref[...]
        for a_ref, b_ref in ((dh_ref, win_ref), (dk_ref, wk_ref), (dv_ref, wv_ref), (df_ref, wf_ref)):
            acc = acc + lax.dot_general(a_ref[...], b_ref[...], NT, preferred_element_type=F32)
        o_ref[...] = acc

    row = lambda n: pl.BlockSpec((tm, n), lambda i: (i, 0))
    wcols = lambda n, c: pl.BlockSpec((D_MODEL, n), lambda i, c=c: (0, c))
    return pl.pallas_call(
        body, name=name, grid=(S // tm,),
        in_specs=[row(D_IN), wcols(D_IN, 0), row(D_MAIN), row(D_MAIN), row(LANES),
                  wcols(D_MAIN, 0), wcols(D_MAIN, 1), wcols(LANES, 2 * D_MAIN // LANES), row(D_MODEL)],
        out_specs=row(D_MODEL), out_shape=jax.ShapeDtypeStruct((S, D_MODEL), F32),
        compiler_params=_cparams("parallel"),
    )(dh, win, dk, dv, dfl, wkv, wkv, wkv, dz)


def _fox_fwd(qa, ka, va, *, tq, name):
    S = qa.shape[0]
    tq = min(tq, S)
    nq = S // tq
    half = tq // 2
    it, jt = _causal_steps(nq, keys_outer=False)

    def body(it_ref, jt_ref, qa_ref, ka_ref, va_ref, y_ref, qb_ref, m_ref, acc_ref):
        n = pl.program_id(1)
        i, j = it_ref[n], jt_ref[n]
        first = lax.broadcasted_iota(jnp.int32, (tq, LANES), 1) < FOX_HEAD_DIM

        @pl.when(j == 0)
        def _():
            m_ref[...] = jnp.full_like(m_ref, NEG)
            acc_ref[...] = jnp.zeros_like(acc_ref)

        def update(hh, rows, nk, masked):
            sl = slice(LANES * hh, LANES * (hh + 1))
            s = lax.dot_general(qa_ref[rows, sl], ka_ref[0:nk, sl], NT, preferred_element_type=F32)
            if masked:
                r = lax.broadcasted_iota(jnp.int32, s.shape, 0) + rows.start
                c = lax.broadcasted_iota(jnp.int32, s.shape, 1)
                s = jnp.where(c <= r, s, NEG)
            m_prev = m_ref[hh, rows]
            m_new = jnp.maximum(m_prev, jnp.max(s, axis=1, keepdims=True))
            p = jnp.exp2(s - jnp.tile(m_new, (1, nk // LANES))).astype(BF16)
            acc_ref[hh, rows] = jnp.exp2(m_prev - m_new) * acc_ref[hh, rows] + jnp.dot(
                p, va_ref[0:nk, sl], preferred_element_type=F32)
            m_ref[hh, rows] = m_new

        @pl.when(j < i)
        def _():
            for hh in range(2):
                update(hh, slice(0, tq), tq, False)

        @pl.when(j == i)
        def _():
            for hh in range(2):
                for r0 in range(0, tq, half):
                    update(hh, slice(r0, r0 + half), r0 + half, True)
            lane = lax.broadcasted_iota(jnp.int32, (tq, LANES), 1)
            ys = []
            for hh in range(2):
                sl = slice(LANES * hh, LANES * (hh + 1))
                a = acc_ref[hh]
                denom = _swap_halves(a)
                ys.append(a / denom)
                lse2 = m_ref[hh] + jnp.log(jnp.where(first, denom, a)) * LOG2E
                qb_ref[:, sl] = _lanes3(lane, AUX + 3, _split3(-lse2), qa_ref[:, sl].astype(F32)).astype(BF16)
            y_ref[...] = jnp.where(first, ys[0], _swap_halves(ys[1]))

    qblock = pl.BlockSpec((tq, 2 * LANES), lambda g, n, it, jt: (it[n], g))
    kblock = pl.BlockSpec((tq, 2 * LANES), lambda g, n, it, jt: (jt[n], g))
    return pl.pallas_call(
        body, name=name,
        grid_spec=pltpu.PrefetchScalarGridSpec(
            num_scalar_prefetch=2, grid=(FOX_PAIRS, it.shape[0]),
            in_specs=[qblock, kblock, kblock],
            out_specs=[pl.BlockSpec((tq, LANES), lambda g, n, it, jt: (it[n], g)), qblock],
            scratch_shapes=[pltpu.VMEM((2, tq, LANES), F32), pltpu.VMEM((2, tq, LANES), F32)]),
        out_shape=[jax.ShapeDtypeStruct((S, D_MAIN), F32), jax.ShapeDtypeStruct((S, 2 * D_MAIN), BF16)],
        compiler_params=_cparams("parallel", "arbitrary"),
    )(it, jt, qa, ka, va)


def _fox_bwd(qb, ka, va, dya, dh, *, tq, name):
    S = qb.shape[0]
    tq = min(tq, S)
    nq = S // tq
    half = tq // 2
    it, jt = _causal_steps(nq, keys_outer=True)
    nsteps = it.shape[0]

    def body(it_ref, jt_ref, qb_ref, ka_ref, va_ref, dya_ref, dh_in, dq_ref, dk_ref, dv_ref, dn_ref, drow_ref,
             dq_acc, dk_acc, dv_acc):
        del dh_in
        n = pl.program_id(1)
        i, j = it_ref[n], jt_ref[n]
        first = lax.broadcasted_iota(jnp.int32, (tq, LANES), 1) < FOX_HEAD_DIM

        @pl.when(n == 0)
        def _():
            dq_acc[...] = jnp.zeros_like(dq_acc)

        @pl.when(i == j)
        def _():
            dk_acc[...] = jnp.zeros_like(dk_acc)
            dv_acc[...] = jnp.zeros_like(dv_acc)

        def update(hh, keys, q0, masked):
            sl = slice(LANES * hh, LANES * (hh + 1))
            qbh, kah, dyah = qb_ref[q0:tq, sl], ka_ref[keys, sl], dya_ref[q0:tq, sl]
            eT = lax.dot_general(kah, qbh, NT, preferred_element_type=F32)
            if masked:
                r = lax.broadcasted_iota(jnp.int32, eT.shape, 0) + keys.start
                c = lax.broadcasted_iota(jnp.int32, eT.shape, 1) + q0
                eT = jnp.where(r <= c, eT, NEG)
            pT = jnp.exp2(eT)
            dsT = pT * lax.dot_general(va_ref[keys, sl], dyah, NT, preferred_element_type=F32)
            dsb = dsT.astype(BF16)
            dv_acc[hh, keys] += jnp.dot(pT.astype(BF16), dyah, preferred_element_type=F32)
            dk_acc[hh, keys] += jnp.dot(dsb, qbh, preferred_element_type=F32)
            rows = pl.ds(pl.multiple_of(i * tq + q0, half), tq - q0)
            dq_acc[hh, rows, :] += lax.dot_general(dsb, kah, TN, preferred_element_type=F32)

        @pl.when(i > j)
        def _():
            for hh in range(2):
                update(hh, slice(0, tq), 0, False)

        @pl.when(i == j)
        def _():
            for hh in range(2):
                for k0 in range(0, tq, half):
                    update(hh, slice(k0, k0 + half), k0, True)

        @pl.when(i == nq - 1)
        def _():
            dk_ref[...] = (jnp.where(first, dk_acc[0], _swap_halves(dk_acc[1])) * LN2).astype(BF16)
            dv_ref[...] = jnp.where(first, dv_acc[0], _swap_halves(dv_acc[1])).astype(BF16)
            dn_ref[...] = jnp.where(first, _swap_halves(dk_acc[0]), dk_acc[1])

        @pl.when(n == nsteps - 1)
        def _():
            first_s = lax.broadcasted_iota(jnp.int32, (S, LANES), 1) < FOX_HEAD_DIM
            dq_ref[...] = (jnp.where(first_s, dq_acc[0], _swap_halves(dq_acc[1])) * FOX_SCALE).astype(BF16)
            drow_ref[...] = jnp.where(first_s, _swap_halves(dq_acc[0]), dq_acc[1])

    qblock = pl.BlockSpec((tq, 2 * LANES), lambda g, n, it, jt: (it[n], g))
    kblock = pl.BlockSpec((tq, 2 * LANES), lambda g, n, it, jt: (jt[n], g))
    whole = pl.BlockSpec((S, LANES), lambda g, n, it, jt: (0, g))
    kout = pl.BlockSpec((tq, LANES), lambda g, n, it, jt: (jt[n], g))
    return pl.pallas_call(
        body, name=name,
        grid_spec=pltpu.PrefetchScalarGridSpec(
            num_scalar_prefetch=2, grid=(FOX_PAIRS, nsteps),
            in_specs=[qblock, kblock, kblock, qblock, pl.BlockSpec(memory_space=pl.ANY)],
            out_specs=[whole, kout, kout, kout, whole],
            scratch_shapes=[pltpu.VMEM((2, S, LANES), F32), pltpu.VMEM((2, tq, LANES), F32),
                            pltpu.VMEM((2, tq, LANES), F32)]),
        out_shape=[jax.ShapeDtypeStruct(dh.shape, dh.dtype), jax.ShapeDtypeStruct((S, D_MAIN), BF16),
                   jax.ShapeDtypeStruct((S, D_MAIN), BF16),
                   jax.ShapeDtypeStruct((S, D_MAIN), F32), jax.ShapeDtypeStruct((S, D_MAIN), F32)],
        input_output_aliases={6: 0},
        compiler_params=_cparams("parallel", "arbitrary"),
    )(it, jt, qb, ka, va, dya, dh)


TB_ROWS = 512
TB_SEQ = 512
TK_DW_IN = 2048
TK_DW_OUT = 1024
TQ_FOX_FWD = 1024
TQ_FOX_BWD = 1024


def _local_step(x, mem, target, win0, pscale, ln_g, ln_b, bias, comm):
    ones = jnp.ones((1, D_MAIN), F32)
    g0, b0, g1, b1 = ln_g[0:1], ln_b[0:1], ln_g[1:2], ln_b[1:2]
    mm = lambda a, b, mode, dt, tm, tn, tk, name, **kw: _mm(a, b, mode=mode, out_dtype=dt, tm=tm, tn=tn, tk=tk,
                                                            name=name, **kw)

    h0, xt = _in_proj_t(x, win0, tm=TB_SEQ, name="l0_in")
    wmkv, pw, wout0 = comm.next_weights(h0)
    pm, mixed = _pool_fwd(h0, pw, tb=TB_SEQ, name="l0_pool_fwd")
    mkv0 = mm(mem, wmkv[0], "nn", F32, 256, 1024, 1024, "l0_mkv")
    ymem0 = _memattn_fwd(h0, mkv0, tb=TB_SEQ, name="l0_mem_fwd")
    yg0 = _gate_fwd(mixed, pscale, h0, ymem0, tb=TB_ROWS, name="l0_gate_fwd")
    x1, xhat0, rstd0, x1t = _out_ln(yg0, wout0, x, g0, b0, tb=TB_SEQ, name="l0_out_ln")
    win1, wout1, wkv, wf = comm.late_weights(x1)

    fl = mm(x1, wf, "nn", F32, 512, LANES, D_MODEL, "f_proj")
    negcum = _forget_fwd(fl, bias, tb=TB_SEQ, name="forget_fwd")
    ka, va = _kv_proj_fox(x1, wkv, negcum, tm=512, name="kv_proj")

    h1, qa = _in_proj_fox(x1, win1, tm=TB_SEQ, name="l1_in")
    y1, qb = _fox_fwd(qa, ka, va, tq=TQ_FOX_FWD, name="fox_fwd")
    mkv1 = mm(mem, wmkv[1], "nn", F32, 256, 1024, 1024, "l1_mkv")
    ymem1 = _memattn_fwd(h1, mkv1, tb=TB_SEQ, name="l1_mem_fwd")
    yg1 = _gate_fwd(y1, ones, h1, ymem1, tb=TB_ROWS, name="l1_gate_fwd")
    dz1, dg1, db1, sq = _out_ln_loss(yg1, wout1, x1, g1, b1, target, tb=TB_SEQ, name="l1_out_ln_loss")

    dwout1 = mm(yg1, dz1, "tn", BF16, D_MIX, D_MODEL, TK_DW_OUT, "l1_dwout")
    dya, dymem1, dh1 = _gate_bwd(dz1, wout1, y1, ones, h1, ymem1, tb=TB_ROWS, name="l1_gate_bwd", fox=True)
    dh1, dk, dv, dnp, drowp = _fox_bwd(qb, ka, va, dya, dh1, tq=TQ_FOX_BWD, name="fox_bwd")
    dfl, dbias = _forget_bwd(dnp, drowp, fl, bias, tb=TB_SEQ, name="forget_bwd")
    dh1, dmkv1 = _memattn_bwd(h1, mkv1, dymem1, dh1, tb=TB_SEQ, name="l1_mem_bwd")
    dwmkv1 = mm(mem, dmkv1, "tn", BF16, D_MODEL, 1024, N_MEM, "l1_dwmkv")
    dwin1 = mm(x1t, dh1, "nn", BF16, D_MODEL, D_IN // 2, TK_DW_IN, "l1_dwin")
    dwkv = _dwkv(x1t, dk, dv, dfl, tk=TK_DW_OUT, name="dwkv")
    dwkv = dwkv[:, :2 * D_MAIN + FOX_HEADS].reshape(D_MODEL, N_DEV, -1).transpose(1, 0, 2).astype(BF16)
    anchor = comm.send_layer1(dict(w_out=dwout1, w_mem_kv=dwmkv1, w_in=dwin1, w_kv_shared=dwkv))
    dx1 = _dx1(dh1, win1, dk, dv, dfl, wkv, dz1, tm=TB_SEQ, name="l1_dx")

    dz0, dg0, db0 = _ln_bwd(dx1, xhat0, rstd0, g0 + anchor, tb=TB_ROWS, name="l0_ln_bwd")
    dwout0 = mm(yg0, dz0, "tn", BF16, D_MIX, D_MODEL, TK_DW_OUT, "l0_dwout")
    dy0, dymem0, dh0 = _gate_bwd(dz0, wout0, mixed, pscale, h0, ymem0, tb=TB_ROWS, name="l0_gate_bwd")
    comm.landed_layer1(dy0)
    dh0, dpw, dpscale = _pool_bwd(dy0, pm, mixed, pw, pscale, dh0, tb=TB_SEQ, name="l0_pool_bwd")
    dh0, dmkv0 = _memattn_bwd(h0, mkv0, dymem0, dh0, tb=TB_SEQ, name="l0_mem_bwd")
    dwmkv0 = mm(mem, dmkv0, "tn", BF16, D_MODEL, 1024, N_MEM, "l0_dwmkv")
    dh0 = comm.send_layer0_first(dict(w_out=dwout0, w_mem_kv=dwmkv0, pool_w=dpw), dh0)
    dwin0 = mm(xt, dh0, "nn", BF16, D_MODEL, D_IN // 2, TK_DW_IN, "l0_dwin")
    dz0 = comm.send_layer0_rest(dict(w_in=dwin0, pool_scale=dpscale, ln_g=jnp.concatenate([dg0, dg1]),
                                     ln_b=jnp.concatenate([db0, db1]), b_forget=dbias[0, :FOX_HEADS]), dz0)
    gx = mm(dh0, win0, "nt", F32, TB_SEQ, D_MODEL, D_IN, "l0_dx", add=dz0, add_scale=ALPHA)
    return sq, gx


MESH_ID = pl.DeviceIdType.MESH
HBM = pl.BlockSpec(memory_space=pl.ANY)
SLICED = {"w_in": (2, D_IN // N_DEV), "w_mem_kv": (1, D_MODEL // N_DEV), "w_out": (1, D_MIX // N_DEV),
          "pool_w": (1, POOL_GROUP // N_DEV)}


def _place():
    return lax.axis_index("x"), lax.axis_index("y"), lax.axis_index("c")


def _slot(p):
    return 4 * p[0] + 2 * p[1] + p[2]


def _cut(ref, axis, width, s):
    idx = [slice(None)] * len(ref.shape)
    idx[axis] = pl.ds(s * width, width)
    return ref.at[tuple(idx)]


def _all_gather(shards, cuts, *, name):
    nt = len(shards)

    def full_shape(a, cut):
        if cut is None:
            return (N_DEV,) + a.shape
        return a.shape[:cut[0]] + (a.shape[cut[0]] * N_DEV,) + a.shape[cut[0] + 1:]

    def body(*refs):
        ins, outs = refs[:nt], refs[nt:2 * nt]
        send_sems, recv_sems, local_sems = refs[2 * nt:]
        x, y, c = _place()
        me, sibling = (x, y, c), (x, y, 1 - c)
        chips = [(1 - x, y), (x, 1 - y), (1 - x, 1 - y)]

        def place(t, s):
            return outs[t].at[s] if cuts[t] is None else _cut(outs[t], cuts[t][0], cuts[t][1], s)

        def copies(k, block, to, from_input=False):
            s = _slot(block)
            return [pltpu.make_async_remote_copy(
                src_ref=ins[t] if from_input else place(t, s), dst_ref=place(t, s),
                send_sem=send_sems.at[nt * k + t], recv_sem=recv_sems.at[nt * k + t],
                device_id=to, device_id_type=MESH_ID) for t in range(nt)]

        mine = [pltpu.make_async_copy(ins[t], place(t, _slot(me)), local_sems.at[t]) for t in range(nt)]
        for cp in mine:
            cp.start()
        first = [copies(0, me, sibling, True)] + [copies(1 + j, me, (*chip, c), True) for j, chip in enumerate(chips)]
        for group in first:
            for cp in group:
                cp.start()
        passed = [copies(4 + j, (*chip, c), sibling) for j, chip in enumerate(chips)]
        for j, chip in enumerate(chips):
            for cp in copies(1 + j, (*chip, c), me):
                cp.wait_recv()
            for cp in passed[j]:
                cp.start()
        for cp in copies(0, sibling, me):
            cp.wait_recv()
        for j, chip in enumerate(chips):
            for cp in copies(4 + j, (*chip, 1 - c), me):
                cp.wait_recv()
        for group in first + passed:
            for cp in group:
                cp.wait_send()
        for cp in mine:
            cp.wait()

    return pl.pallas_call(
        body, name=name, in_specs=[HBM] * nt, out_specs=[HBM] * nt,
        out_shape=[jax.ShapeDtypeStruct(full_shape(a, cut), a.dtype) for a, cut in zip(shards, cuts)],
        scratch_shapes=[pltpu.SemaphoreType.DMA((7 * nt,)), pltpu.SemaphoreType.DMA((7 * nt,)),
                        pltpu.SemaphoreType.DMA((nt,))],
    )(*shards)


def _exchange_copies(items, ins, outs, send_sems, recv_sems, local_sems, gather=False):
    nt = len(items)
    x, y, c = _place()
    me = _slot((x, y, c))
    flip = lambda v, bit: 1 - v if bit else v

    def part(ref, cut, s):
        return ref.at[s] if cut is None else _cut(ref, cut[0], cut[1], s)

    def src(t, s):
        return ins[t] if gather else part(ins[t], items[t][0], s)

    def dst(t, s):
        if gather:
            return part(outs[items[t][1]], items[t][0], s)
        d = outs[items[t][1]].at[s]
        return d if items[t][2] is None else d.at[items[t][2]]

    sends, arrivals = [], []
    for k in range(1, N_DEV):
        peer = (flip(x, k & 4), flip(y, k & 2), flip(c, k & 1))
        ps = _slot(peer)
        for t in range(nt):
            sems = dict(send_sem=send_sems.at[nt * (k - 1) + t], recv_sem=recv_sems.at[nt * (k - 1) + t],
                        device_id=peer, device_id_type=MESH_ID)
            sends.append(pltpu.make_async_remote_copy(src_ref=src(t, ps), dst_ref=dst(t, me), **sems))
            arrivals.append(pltpu.make_async_remote_copy(src_ref=src(t, ps), dst_ref=dst(t, ps), **sems))
    mine = [pltpu.make_async_copy(src(t, me), dst(t, me), local_sems.at[t]) for t in range(nt)]
    return sends, arrivals, mine


SEMS = pl.BlockSpec(memory_space=pltpu.SEMAPHORE)
SIDE_EFFECT = pltpu.SideEffectType.DATAFLOW_SIDE_EFFECTING


def _exchange_start(srcs, items, landings, *, name, gather=False, carry=()):
    nt, nl, nc = len(srcs), len(landings), len(carry)

    def body(*refs):
        ins, lands = refs[:nt], refs[nt:nt + nl]
        send_sems, recv_sems, local_sems = refs[nt + nl + nc:nt + nl + nc + 3]
        token = refs[-1]
        sends, _, mine = _exchange_copies(items, ins, lands, send_sems, recv_sems, local_sems, gather)
        for cp in sends + mine:
            cp.start()
        token[...] = jnp.zeros_like(token)

    hbm = lambda a: pltpu.HBM(a.shape, a.dtype)
    fresh = [pltpu.with_memory_space_constraint(
        lax.empty(l.shape, l.dtype) if isinstance(l, jax.ShapeDtypeStruct) else l, pltpu.HBM) for l in landings]
    res = pl.pallas_call(
        body, name=name, in_specs=[HBM] * (nt + nl + nc),
        out_specs=[SEMS, SEMS, SEMS] + [HBM] * (nt + nl + nc) + [pl.BlockSpec(memory_space=pltpu.VMEM)],
        out_shape=[pltpu.SemaphoreType.DMA((7 * nt,)), pltpu.SemaphoreType.DMA((7 * nt,)), pltpu.SemaphoreType.DMA((nt,))]
        + [hbm(a) for a in srcs] + [hbm(l) for l in landings] + [hbm(a) for a in carry]
        + [jax.ShapeDtypeStruct((8, LANES), F32)],
        input_output_aliases={i: 3 + i for i in range(nt + nl + nc)},
        compiler_params=pltpu.CompilerParams(has_side_effects=SIDE_EFFECT),
    )(*[pltpu.with_memory_space_constraint(a, pltpu.HBM) for a in srcs], *fresh,
      *[pltpu.with_memory_space_constraint(a, pltpu.HBM) for a in carry])
    return res[:3 + nt + nl], res[-1][0:1, 0:1], res[3 + nt + nl:-1]


def _exchange_wait(state, items, nt, after, *, name, gather=False):
    sems, bufs = state[:3], state[3:]
    nl = len(bufs) - nt

    def body(*refs):
        ins, lands = refs[:nt], refs[nt:nt + nl]
        send_sems, recv_sems, local_sems = refs[nt + nl:nt + nl + 3]
        sends, arrivals, mine = _exchange_copies(items, ins, lands, send_sems, recv_sems, local_sems, gather)
        for sent, landed in zip(sends, arrivals):
            landed.wait_recv()
            sent.wait_send()
        for cp in mine:
            cp.wait()

    hbm = lambda a: pltpu.HBM(a.shape, a.dtype)
    res = pl.pallas_call(
        body, name=name, in_specs=[HBM] * (nt + nl) + [SEMS, SEMS, SEMS, HBM], out_specs=[HBM] * (nt + nl),
        out_shape=[hbm(a) for a in bufs],
        input_output_aliases={i: i for i in range(nt + nl)},
        compiler_params=pltpu.CompilerParams(has_side_effects=SIDE_EFFECT),
    )(*bufs, *sems, after)
    return res[nt:]


def _adamw(recv, w, m, v, *, split, name):
    shape = w.shape
    axis, parts = split
    block = shape[:axis] + (shape[axis] // parts,) + shape[axis + 1:]
    nd = len(shape)

    def body(r_ref, w_ref, m_ref, v_ref, g_ref, d_ref, nm_ref, nv_ref):
        g = r_ref[0].astype(F32)
        for j in range(1, N_DEV):
            g = g + r_ref[j].astype(F32)
        nm = ADAM_B1 * m_ref[...] + (1.0 - ADAM_B1) * g
        nv = ADAM_B2 * v_ref[...] + (1.0 - ADAM_B2) * (g * g)
        m_hat = nm / (1.0 - ADAM_B1 ** ADAM_STEP)
        v_hat = nv / (1.0 - ADAM_B2 ** ADAM_STEP)
        g_ref[...] = g
        nm_ref[...] = nm
        nv_ref[...] = nv
        d_ref[...] = -ADAM_LR * (m_hat / (jnp.sqrt(v_hat) + ADAM_EPS) + ADAM_WD * w_ref[...])

    at = lambda i: tuple(i if a == axis else 0 for a in range(nd))
    one = pl.BlockSpec(block, at)
    shp = jax.ShapeDtypeStruct(shape, F32)
    return pl.pallas_call(
        body, name=name, grid=(parts,),
        in_specs=[pl.BlockSpec((N_DEV,) + block, lambda i: (0,) + at(i)), one, one, one],
        out_specs=[one, one, one, one], out_shape=[shp, shp, shp, shp],
        compiler_params=_cparams("parallel"),
    )(recv, w, m, v)


BIG = ("w_in", "w_mem_kv", "w_out", "pool_w", "w_kv_shared")
SMALL = ("pool_scale", "ln_g", "ln_b", "b_forget")
SMALL_ROWS = 40
ADAM_SPLIT = {"w_in": (1, 4), "w_mem_kv": (0, 2), "w_out": (0, 2), "pool_w": (0, 1), "w_kv_shared": (0, 4)}
PER_LAYER_CUT = {"w_out": (0, D_MIX // N_DEV), "w_mem_kv": (0, D_MODEL // N_DEV), "w_in": (1, D_IN // N_DEV),
                 "pool_w": (1, POOL_GROUP // N_DEV)}
EARLY = ("w_out", "w_mem_kv", "w_in", "w_kv_shared")
EARLY_ITEMS = [(PER_LAYER_CUT[n], i, 1) for i, n in enumerate(EARLY[:3])] + [(None, 3, None)]


def _flat(parts, rows):
    v = jnp.concatenate([p.reshape(-1) for p in parts])
    return jnp.pad(v, (0, rows * LANES - v.shape[0])).reshape(rows, LANES)


def _unflat(flat, shapes):
    v, out, off = flat.reshape(-1), [], 0
    for s in shapes:
        n = math.prod(s)
        out.append(v[off:off + n].reshape(s))
        off += n
    return out


def kernel(x, mem, w_in, w_mem_kv, w_out, ln_g, ln_b, pool_w, pool_scale, w_kv_shared, b_forget, loss_target, m_w_in, m_w_mem_kv, m_w_out, m_ln_g, m_ln_b, m_pool_w, m_pool_scale, m_w_kv_shared, m_b_forget, v_w_in, v_w_mem_kv, v_w_out, v_ln_g, v_ln_b, v_pool_w, v_pool_scale, v_w_kv_shared, v_b_forget):
    w = dict(w_in=w_in, w_mem_kv=w_mem_kv, w_out=w_out, ln_g=ln_g, ln_b=ln_b, pool_w=pool_w[0],
             pool_scale=pool_scale, w_kv_shared=w_kv_shared, b_forget=b_forget)
    m = dict(w_in=m_w_in, w_mem_kv=m_w_mem_kv, w_out=m_w_out, ln_g=m_ln_g, ln_b=m_ln_b, pool_w=m_pool_w[0],
             pool_scale=m_pool_scale, w_kv_shared=m_w_kv_shared, b_forget=m_b_forget)
    v = dict(w_in=v_w_in, w_mem_kv=v_w_mem_kv, w_out=v_w_out, ln_g=v_ln_g, ln_b=v_ln_b, pool_w=v_pool_w[0],
             pool_scale=v_pool_scale, w_kv_shared=v_w_kv_shared, b_forget=v_b_forget)

    wb = {n: w[n].astype(BF16) for n in BIG}
    bdt = wb["w_in"].dtype
    win0, pscale = _all_gather([wb["w_in"][0], jnp.pad(pool_scale, ((0, 7), (0, 0)))], [PER_LAYER_CUT["w_in"], None],
                               name="gather_weights")
    pscale = pscale[:, 0, :].reshape(1, D_MAIN)
    next_srcs = [wb["w_mem_kv"], wb["pool_w"], wb["w_out"][0]]
    next_items = [(SLICED["w_mem_kv"], 0, None), (SLICED["pool_w"], 1, None), (PER_LAYER_CUT["w_out"], 2, None)]
    next_state, _, (win0,) = _exchange_start(
        next_srcs, next_items,
        [jax.ShapeDtypeStruct((2, D_MODEL, D_MODEL), bdt), jax.ShapeDtypeStruct((4, POOL_GROUP, POOL_GROUP), bdt),
         jax.ShapeDtypeStruct((D_MIX, D_MODEL), bdt)], name="gather_next_start", gather=True, carry=[win0])
    late_srcs = [wb["w_in"][1], wb["w_out"][1], wb["w_kv_shared"]]
    late_items = [(PER_LAYER_CUT["w_in"], 0, None), (PER_LAYER_CUT["w_out"], 1, None), (None, 2, None)]
    late = {}
    bias = jnp.pad(b_forget, (0, LANES - FOX_HEADS)).reshape(1, LANES)

    def next_weights(h0):
        wmkv, pw, wout0 = _exchange_wait(next_state, next_items, len(next_srcs), h0, name="gather_next_wait", gather=True)
        late["state"], _, (pw,) = _exchange_start(
            late_srcs, late_items,
            [jax.ShapeDtypeStruct((D_MODEL, D_IN), bdt), jax.ShapeDtypeStruct((D_MIX, D_MODEL), bdt),
             jax.ShapeDtypeStruct((N_DEV,) + w_kv_shared.shape, bdt)], name="gather_late_start", gather=True, carry=[pw])
        return wmkv, pw, wout0

    def late_weights(x1):
        win1, wout1, wkv = _exchange_wait(late["state"], late_items, len(late_srcs), x1, name="gather_late_wait",
                                          gather=True)
        wkv = jnp.pad(wkv.transpose(1, 0, 2).reshape(D_MODEL, -1), ((0, 0), (0, LANES - FOX_HEADS)))
        return win1, wout1, wkv, wkv[:, 2 * D_MAIN:]

    sent, recv = {}, {}
    first_items = [(PER_LAYER_CUT["w_out"], 0, 0), (PER_LAYER_CUT["w_mem_kv"], 1, 0), (PER_LAYER_CUT["pool_w"], 2, None)]
    rest_items = [(PER_LAYER_CUT["w_in"], 0, 0), (None, 1, None)]

    def send_layer1(g):
        srcs = [g[n] for n in EARLY]
        lands = [jax.ShapeDtypeStruct((N_DEV, 2) + w[n].shape[1:], g[n].dtype) for n in EARLY[:3]]
        lands.append(jax.ShapeDtypeStruct(g["w_kv_shared"].shape, g["w_kv_shared"].dtype))
        sent["layer1"], anchor, _ = _exchange_start(srcs, EARLY_ITEMS, lands, name="exchange_early_start")
        return anchor

    def landed_layer1(after):
        recv["w_out"], recv["w_mem_kv"], recv["w_in"], recv["w_kv_shared"] = _exchange_wait(
            sent["layer1"], EARLY_ITEMS, len(EARLY), after, name="exchange_early_wait")

    def send_layer0_first(g, carry):
        srcs = [g["w_out"], g["w_mem_kv"], g["pool_w"]]
        sent["first"], _, (carry,) = _exchange_start(
            srcs, first_items,
            [recv["w_out"], recv["w_mem_kv"], jax.ShapeDtypeStruct((N_DEV,) + w["pool_w"].shape, srcs[2].dtype)],
            name="exchange_mid_start", carry=[carry])
        return carry

    def send_layer0_rest(g, carry):
        small = jnp.concatenate([g["pool_scale"].reshape(N_DEV, -1)]
                                + [jnp.broadcast_to(g[n].reshape(1, -1), (N_DEV, g[n].size)) for n in SMALL[1:]], axis=1)
        small = jnp.pad(small, ((0, 0), (0, SMALL_ROWS * LANES - small.shape[1]))).reshape(N_DEV, SMALL_ROWS, LANES)
        sent["rest"], _, (carry,) = _exchange_start(
            [g["w_in"], small], rest_items, [recv["w_in"], jax.ShapeDtypeStruct(small.shape, small.dtype)],
            name="exchange_last_start", carry=[carry])
        return carry

    comm = types.SimpleNamespace(next_weights=next_weights, late_weights=late_weights, send_layer1=send_layer1,
                                 landed_layer1=landed_layer1, send_layer0_first=send_layer0_first,
                                 send_layer0_rest=send_layer0_rest)
    sq, gx = _local_step(x[0], mem[0], loss_target[0], win0, pscale, ln_g, ln_b, bias, comm)
    loss = lax.psum((0.5 / D_MODEL) * jnp.sum(sq), ("x", "y", "c"))

    recv["w_out"], recv["w_mem_kv"], recv["pool_w"] = _exchange_wait(sent["first"], first_items, len(first_items), gx,
                                                                     name="exchange_mid_wait")
    recv["w_in"], recv["small"] = _exchange_wait(sent["rest"], rest_items, len(rest_items), gx,
                                                 name="exchange_last_wait")

    outs = {}
    for n in BIG:
        res = _adamw(recv[n], w[n], m[n], v[n], split=ADAM_SPLIT[n], name="adamw_" + n)
        for kind, a in zip(("grad", "delta", "new_m", "new_v"), res):
            outs[kind, n] = a[None] if n == "pool_w" else a
    small_shapes = [w[n].shape for n in SMALL]
    res = _adamw(recv["small"], _flat([w[n] for n in SMALL], SMALL_ROWS), _flat([m[n] for n in SMALL], SMALL_ROWS),
                 _flat([v[n] for n in SMALL], SMALL_ROWS), split=(0, 1), name="adamw_small")
    for kind, flat in zip(("grad", "delta", "new_m", "new_v"), res):
        for n, a in zip(SMALL, _unflat(flat, small_shapes)):
            outs[kind, n] = a
    order = ("w_in", "w_mem_kv", "w_out", "ln_g", "ln_b", "pool_w", "pool_scale", "w_kv_shared", "b_forget")
    return (loss, gx[None], *[outs[kind, n] for kind in ("grad", "delta", "new_m", "new_v") for n in order])
```

```python
import math
import types

import numpy as np
import jax
import jax.numpy as jnp
from jax import lax
from jax.experimental import pallas as pl
from jax.experimental.pallas import tpu as pltpu

F32 = jnp.float32
BF16 = jnp.bfloat16

D_MODEL = 1024
D_MAIN = 1024
D_MEM = 512
D_MIX = D_MAIN + D_MEM
D_IN = 2 * D_MIX
N_MEM = 256
MEM_HEADS = 4
MEM_HEAD_DIM = 128
FOX_HEADS = 16
FOX_HEAD_DIM = 64
FOX_PAIRS = FOX_HEADS // 2
POOL_WINDOWS = (2, 4, 8, 16)
POOL_GROUP = 256
POOL_HALO = 16
ALPHA = 4.0 ** 0.25
LN_EPS = 1e-5
NEG = -1e30
LANES = 128
N_DEV = 8

ADAM_LR = 0.001
ADAM_B1 = 0.9
ADAM_B2 = 0.999
ADAM_EPS = 1e-08
ADAM_WD = 0.01
ADAM_STEP = 10

VMEM_LIMIT = 56 * 1024 * 1024

NN = (((1,), (0,)), ((), ()))
NT = (((1,), (1,)), ((), ()))
TN = (((0,), (0,)), ((), ()))


def _cparams(*sem):
    return pltpu.CompilerParams(dimension_semantics=sem, vmem_limit_bytes=VMEM_LIMIT)


def _sigmoid(z):
    return 1.0 / (1.0 + jnp.exp(-z))


def _mm(a, b, *, mode, out_dtype, tm, tn, tk, name, add=None, add_scale=1.0):
    if mode == "nn":
        (M, K), (K2, N) = a.shape, b.shape
    elif mode == "nt":
        (M, K), (N, K2) = a.shape, b.shape
    else:
        (K, M), (K2, N) = a.shape, b.shape
    assert K == K2, (a.shape, b.shape, mode)
    tm, tn, tk = min(tm, M), min(tn, N), min(tk, K)
    assert M % tm == 0 and N % tn == 0 and K % tk == 0, (M, N, K, tm, tn, tk)
    gm, gn, gk = M // tm, N // tn, K // tk
    dims = {"nn": NN, "nt": NT, "tn": TN}[mode]
    if mode == "tn":
        a_spec = pl.BlockSpec((tk, tm), lambda i, j, k: (k, i))
    else:
        a_spec = pl.BlockSpec((tm, tk), lambda i, j, k: (i, k))
    if mode == "nt":
        b_spec = pl.BlockSpec((tn, tk), lambda i, j, k: (j, k))
    else:
        b_spec = pl.BlockSpec((tk, tn), lambda i, j, k: (k, j))
    o_spec = pl.BlockSpec((tm, tn), lambda i, j, k: (i, j))
    has_add = add is not None
    acc_in_out = out_dtype == F32

    def body(*refs):
        a_ref, b_ref = refs[0], refs[1]
        add_ref = refs[2] if has_add else None
        o_ref = refs[3] if has_add else refs[2]
        prod = lax.dot_general(a_ref[...].astype(BF16), b_ref[...].astype(BF16), dims,
                               preferred_element_type=F32)

        def finish(r):
            if has_add:
                r = r + add_scale * add_ref[...]
            o_ref[...] = r.astype(out_dtype)

        if gk == 1:
            finish(prod)
        else:
            acc_ref = o_ref if acc_in_out else refs[-1]
            k = pl.program_id(2)

            @pl.when(k == 0)
            def _():
                acc_ref[...] = prod

            @pl.when(k > 0)
            def _():
                acc_ref[...] += prod

            if has_add or not acc_in_out:
                @pl.when(k == gk - 1)
                def _():
                    finish(acc_ref[...])

    in_specs = [a_spec, b_spec] + ([o_spec] if has_add else [])
    args = (a, b) + ((add,) if has_add else ())
    return pl.pallas_call(
        body, name=name, grid=(gm, gn, gk), in_specs=in_specs, out_specs=o_spec,
        out_shape=jax.ShapeDtypeStruct((M, N), out_dtype),
        scratch_shapes=[pltpu.VMEM((tm, tn), F32)] if gk > 1 and not acc_in_out else [],
        compiler_params=_cparams("parallel", "parallel", "arbitrary"),
    )(*args)


def _ln_stats(z):
    mu = jnp.mean(z, axis=1, keepdims=True)
    zc = z - mu
    var = jnp.mean(zc * zc, axis=1, keepdims=True)
    rstd = lax.rsqrt(var + LN_EPS)
    return zc * rstd, rstd


def _ln_bwd_math(dy, xhat, rstd, g):
    dxh = dy * g
    m1 = jnp.mean(dxh, axis=1, keepdims=True)
    m2 = jnp.mean(dxh * xhat, axis=1, keepdims=True)
    return rstd * (dxh - m1 - xhat * m2)


def _out_ln(yg, wout, x, g, b, *, tb, name):
    S = x.shape[0]
    tb = min(tb, S)

    def body(yg_ref, w_ref, x_ref, g_ref, b_ref, y_ref, xhat_ref, rstd_ref, yt_ref):
        o = jnp.dot(yg_ref[...], w_ref[...], preferred_element_type=F32)
        xhat, rstd = _ln_stats(ALPHA * x_ref[...] + o)
        xhat_ref[...] = xhat
        rstd_ref[...] = rstd
        y = xhat * g_ref[...] + b_ref[...]
        y_ref[...] = y
        yt_ref[...] = y.T.astype(BF16)

    row = pl.BlockSpec((tb, D_MODEL), lambda i: (i, 0))
    vec = pl.BlockSpec((1, D_MODEL), lambda i: (0, 0))
    return pl.pallas_call(
        body, name=name, grid=(S // tb,),
        in_specs=[pl.BlockSpec((tb, D_MIX), lambda i: (i, 0)), pl.BlockSpec((D_MIX, D_MODEL), lambda i: (0, 0)),
                  row, vec, vec],
        out_specs=[row, row, pl.BlockSpec((tb, 1), lambda i: (i, 0)), pl.BlockSpec((D_MODEL, tb), lambda i: (0, i))],
        out_shape=[jax.ShapeDtypeStruct((S, D_MODEL), F32), jax.ShapeDtypeStruct((S, D_MODEL), F32),
                   jax.ShapeDtypeStruct((S, 1), F32), jax.ShapeDtypeStruct((D_MODEL, S), BF16)],
        compiler_params=_cparams("parallel"),
    )(yg, wout, x, g, b)


def _in_proj_t(x, w, *, tm, name):
    S = x.shape[0]
    tm = min(tm, S)

    def body(x_ref, w_ref, h_ref, xt_ref):
        xv = x_ref[...]
        h_ref[...] = jnp.dot(xv.astype(BF16), w_ref[...], preferred_element_type=F32).astype(BF16)
        xt_ref[...] = xv.T.astype(BF16)

    return pl.pallas_call(
        body, name=name, grid=(S // tm,),
        in_specs=[pl.BlockSpec((tm, D_MODEL), lambda i: (i, 0)), pl.BlockSpec((D_MODEL, D_IN), lambda i: (0, 0))],
        out_specs=[pl.BlockSpec((tm, D_IN), lambda i: (i, 0)), pl.BlockSpec((D_MODEL, tm), lambda i: (0, i))],
        out_shape=[jax.ShapeDtypeStruct((S, D_IN), BF16), jax.ShapeDtypeStruct((D_MODEL, S), BF16)],
        compiler_params=_cparams("parallel"),
    )(x, w)


def _out_ln_loss(yg, wout, x, g, b, target, *, tb, name):
    S = x.shape[0]
    tb = min(tb, S)

    def body(yg_ref, w_ref, x_ref, g_ref, b_ref, t_ref, dz_ref, dg_ref, db_ref, sq_ref):
        @pl.when(pl.program_id(0) == 0)
        def _():
            dg_ref[...] = jnp.zeros_like(dg_ref)
            db_ref[...] = jnp.zeros_like(db_ref)
            sq_ref[...] = jnp.zeros_like(sq_ref)

        o = jnp.dot(yg_ref[...], w_ref[...], preferred_element_type=F32)
        xhat, rstd = _ln_stats(ALPHA * x_ref[...] + o)
        err = xhat * g_ref[...] + b_ref[...] - t_ref[...]
        sq_ref[...] += jnp.sum(err * err, axis=0, keepdims=True)
        dy = err * (1.0 / D_MODEL)
        dz_ref[...] = _ln_bwd_math(dy, xhat, rstd, g_ref[...])
        dg_ref[...] += jnp.sum(dy * xhat, axis=0, keepdims=True)
        db_ref[...] += jnp.sum(dy, axis=0, keepdims=True)

    row = pl.BlockSpec((tb, D_MODEL), lambda i: (i, 0))
    vec = pl.BlockSpec((1, D_MODEL), lambda i: (0, 0))
    vshape = jax.ShapeDtypeStruct((1, D_MODEL), F32)
    return pl.pallas_call(
        body, name=name, grid=(S // tb,),
        in_specs=[pl.BlockSpec((tb, D_MIX), lambda i: (i, 0)), pl.BlockSpec((D_MIX, D_MODEL), lambda i: (0, 0)),
                  row, vec, vec, row],
        out_specs=[row, vec, vec, vec],
        out_shape=[jax.ShapeDtypeStruct((S, D_MODEL), F32), vshape, vshape, vshape],
        compiler_params=_cparams("arbitrary"),
    )(yg, wout, x, g, b, target)


def _ln_bwd(dy, xhat, rstd, g, *, tb, name):
    S = dy.shape[0]
    tb = min(tb, S)

    def body(dy_ref, xhat_ref, rstd_ref, g_ref, dz_ref, dg_ref, db_ref):
        @pl.when(pl.program_id(0) == 0)
        def _():
            dg_ref[...] = jnp.zeros_like(dg_ref)
            db_ref[...] = jnp.zeros_like(db_ref)

        dy_, xhat_ = dy_ref[...], xhat_ref[...]
        dz_ref[...] = _ln_bwd_math(dy_, xhat_, rstd_ref[...], g_ref[...])
        dg_ref[...] += jnp.sum(dy_ * xhat_, axis=0, keepdims=True)
        db_ref[...] += jnp.sum(dy_, axis=0, keepdims=True)

    row = pl.BlockSpec((tb, D_MODEL), lambda i: (i, 0))
    vec = pl.BlockSpec((1, D_MODEL), lambda i: (0, 0))
    return pl.pallas_call(
        body, name=name, grid=(S // tb,),
        in_specs=[row, row, pl.BlockSpec((tb, 1), lambda i: (i, 0)), vec],
        out_specs=[row, vec, vec],
        out_shape=[jax.ShapeDtypeStruct((S, D_MODEL), F32), jax.ShapeDtypeStruct((1, D_MODEL), F32),
                   jax.ShapeDtypeStruct((1, D_MODEL), F32)],
        compiler_params=_cparams("arbitrary"),
    )(dy, xhat, rstd, g)


def _gate_fwd(ysrc, scale, h, ymem, *, tb, name):
    S = ysrc.shape[0]
    tb = min(tb, S)

    def body(ys_ref, sc_ref, ga_ref, gb_ref, gc_ref, ym_ref, yg_ref):
        ymain = ys_ref[...] * sc_ref[...]
        for k, g_ref in enumerate((ga_ref, gb_ref)):
            gv = g_ref[...].astype(F32)
            yg_ref[:, 512 * k:512 * (k + 1)] = (ymain[:, 512 * k:512 * (k + 1)] * gv * _sigmoid(gv)).astype(BF16)
        gv = gc_ref[...].astype(F32)
        yg_ref[:, 1024:1536] = (ym_ref[...] * gv * _sigmoid(gv)).astype(BF16)

    slab = lambda c: pl.BlockSpec((tb, 512), lambda i, c=c: (i, c))
    return pl.pallas_call(
        body, name=name, grid=(S // tb,),
        in_specs=[pl.BlockSpec((tb, D_MAIN), lambda i: (i, 0)), pl.BlockSpec((1, D_MAIN), lambda i: (0, 0)),
                  slab(3), slab(4), slab(5), pl.BlockSpec((tb, D_MEM), lambda i: (i, 0))],
        out_specs=pl.BlockSpec((tb, D_MIX), lambda i: (i, 0)),
        out_shape=jax.ShapeDtypeStruct((S, D_MIX), BF16),
        compiler_params=_cparams("parallel"),
    )(ysrc, scale, h, h, h, ymem)


def _gate_bwd(dz, wout, ysrc, scale, h, ymem, *, tb, name, fox=False):
    S = ysrc.shape[0]
    tb = min(tb, S)

    def dsilu(gv):
        sg = _sigmoid(gv)
        return sg, sg * (1.0 + gv * (1.0 - sg))

    def body(dz_ref, w_ref, ys_ref, sc_ref, ga_ref, gb_ref, gc_ref, ym_ref, dym_ref, dymem_ref, dh_ref):
        dyg = lax.dot_general(dz_ref[...].astype(BF16), w_ref[...], NT, preferred_element_type=F32)
        ymain = ys_ref[...] * sc_ref[...]
        lane = lax.broadcasted_iota(jnp.int32, (tb, LANES), 1)
        first = lane < FOX_HEAD_DIM
        for k, g_ref in enumerate((ga_ref, gb_ref)):
            gv, d = g_ref[...].astype(F32), dyg[:, 512 * k:512 * (k + 1)]
            sg, ds = dsilu(gv)
            dy = d * gv * sg
            dh_ref[:, 512 * k:512 * (k + 1)] = (d * ymain[:, 512 * k:512 * (k + 1)] * ds).astype(BF16)
            if not fox:
                dym_ref[:, 512 * k:512 * (k + 1)] = dy
                continue
            for q in range(512 // LANES):
                cols = slice(LANES * q, LANES * (q + 1))
                dy2 = dy[:, cols]
                prod = dy2 * ymain[:, 512 * k + LANES * q:512 * k + LANES * (q + 1)]
                for hh in range(2):
                    delta = jnp.sum(jnp.where(first == (hh == 0), prod, 0.0), axis=1, keepdims=True)
                    dyh = dy2 if hh == 0 else _swap_halves(dy2)
                    c0 = LANES * (2 * (4 * k + q) + hh)
                    dym_ref[:, c0:c0 + LANES] = jnp.where(
                        first, dyh, _lanes3(lane, AUX, _split3(-delta), 0.0)).astype(BF16)
        gv, d = gc_ref[...].astype(F32), dyg[:, 1024:1536]
        sg, ds = dsilu(gv)
        dymem_ref[...] = d * gv * sg
        dh_ref[:, 1024:1536] = (d * ym_ref[...] * ds).astype(BF16)

    slab = lambda c: pl.BlockSpec((tb, 512), lambda i, c=c: (i, c))
    return pl.pallas_call(
        body, name=name, grid=(S // tb,),
        in_specs=[pl.BlockSpec((tb, D_MODEL), lambda i: (i, 0)), pl.BlockSpec((D_MIX, D_MODEL), lambda i: (0, 0)),
                  pl.BlockSpec((tb, D_MAIN), lambda i: (i, 0)), pl.BlockSpec((1, D_MAIN), lambda i: (0, 0)),
                  slab(3), slab(4), slab(5), pl.BlockSpec((tb, D_MEM), lambda i: (i, 0))],
        out_specs=[pl.BlockSpec((tb, 2 * D_MAIN if fox else D_MAIN), lambda i: (i, 0)),
                   pl.BlockSpec((tb, D_MEM), lambda i: (i, 0)), pl.BlockSpec((tb, D_MIX), lambda i: (i, 1))],
        out_shape=[jax.ShapeDtypeStruct((S, 2 * D_MAIN), BF16) if fox else jax.ShapeDtypeStruct((S, D_MAIN), F32),
                   jax.ShapeDtypeStruct((S, D_MEM), F32), jax.ShapeDtypeStruct((S, D_IN), BF16)],
        compiler_params=_cparams("parallel"),
    )(dz, wout, ysrc, scale, h, h, h, ymem)


def _window_count(t0, rows, w):
    t = t0 + lax.broadcasted_iota(jnp.int32, (rows, POOL_GROUP), 0)
    return jnp.minimum(t + 1, w).astype(F32)


def _pool_fwd(h, pw, *, tb, name):
    S = h.shape[0]
    tb = min(tb, S)

    def body(u_ref, pw_ref, pm_ref, mixed_ref, tail_ref):
        i = pl.program_id(0)

        @pl.when(i == 0)
        def _():
            tail_ref[...] = jnp.zeros_like(tail_ref)

        u = u_ref[...].astype(F32)
        xfull = jnp.concatenate([tail_ref[...], u], axis=0)
        for gi, w in enumerate(POOL_WINDOWS):
            cols = slice(POOL_GROUP * gi, POOL_GROUP * (gi + 1))
            s = xfull[:, cols]
            sh = 1
            while sh < w:
                s = s + pltpu.roll(s, sh, 0)
                sh *= 2
            pm = s[POOL_HALO:, :] / _window_count(i * tb, tb, w) - u[:, cols]
            pmb = pm.astype(BF16)
            pm_ref[:, cols] = pmb
            mixed_ref[:, cols] = jnp.dot(pmb, pw_ref[gi], preferred_element_type=F32)
        tail_ref[...] = u[tb - POOL_HALO:, :]

    return pl.pallas_call(
        body, name=name, grid=(S // tb,),
        in_specs=[pl.BlockSpec((tb, D_MAIN), lambda i: (i, 0)),
                  pl.BlockSpec((4, POOL_GROUP, POOL_GROUP), lambda i: (0, 0, 0))],
        out_specs=[pl.BlockSpec((tb, D_MAIN), lambda i: (i, 0)), pl.BlockSpec((tb, D_MAIN), lambda i: (i, 0))],
        out_shape=[jax.ShapeDtypeStruct((S, D_MAIN), BF16), jax.ShapeDtypeStruct((S, D_MAIN), F32)],
        scratch_shapes=[pltpu.VMEM((POOL_HALO, D_MAIN), F32)],
        compiler_params=_cparams("arbitrary"),
    )(h, pw)


def _pool_bwd(dymain, pm, mixed, pw, scale, dh, *, tb, name):
    S = dymain.shape[0]
    tb = min(tb, S)
    nb = S // tb
    n = tb + POOL_HALO

    def body(dy_ref, pm_ref, mixed_ref, pw_ref, sc_ref, dh_in, dh_ref, dpw_ref, dsc_ref, head_ref, dpw_acc):
        del dh_in
        i = pl.program_id(0)

        @pl.when(i == 0)
        def _():
            head_ref[...] = jnp.zeros_like(head_ref)
            dpw_acc[...] = jnp.zeros_like(dpw_acc)
            dsc_ref[...] = jnp.zeros_like(dsc_ref)

        dy = dy_ref[...]
        dsc_ref[...] += jnp.sum(dy * mixed_ref[...], axis=0, keepdims=True)
        dmixed = dy * sc_ref[...]
        t0 = (nb - 1 - i) * tb
        for gi, w in enumerate(POOL_WINDOWS):
            cols = slice(POOL_GROUP * gi, POOL_GROUP * (gi + 1))
            dm = dmixed[:, cols].astype(BF16)
            dpw_acc[gi] += lax.dot_general(pm_ref[:, cols], dm, TN, preferred_element_type=F32)
            dpm = lax.dot_general(dm, pw_ref[gi], NT, preferred_element_type=F32)
            e = dpm / _window_count(t0, tb, w)
            s = jnp.concatenate([e, head_ref[:, cols]], axis=0)
            sh = 1
            while sh < w:
                s = s + pltpu.roll(s, n - sh, 0)
                sh *= 2
            dh_ref[:, cols] = (s[:tb, :] - dpm).astype(BF16)
            head_ref[:, cols] = e[:POOL_HALO, :]

        @pl.when(i == nb - 1)
        def _():
            dpw_ref[...] = dpw_acc[...].astype(BF16)

    rev = lambda i: (nb - 1 - i, 0)
    return pl.pallas_call(
        body, name=name, grid=(nb,),
        in_specs=[pl.BlockSpec((tb, D_MAIN), rev), pl.BlockSpec((tb, D_MAIN), rev), pl.BlockSpec((tb, D_MAIN), rev),
                  pl.BlockSpec((4, POOL_GROUP, POOL_GROUP), lambda i: (0, 0, 0)),
                  pl.BlockSpec((1, D_MAIN), lambda i: (0, 0)), pl.BlockSpec(memory_space=pl.ANY)],
        out_specs=[pl.BlockSpec((tb, D_MAIN), rev),
                   pl.BlockSpec((4, POOL_GROUP, POOL_GROUP), lambda i: (0, 0, 0)),
                   pl.BlockSpec((1, D_MAIN), lambda i: (0, 0))],
        out_shape=[jax.ShapeDtypeStruct(dh.shape, dh.dtype),
                   jax.ShapeDtypeStruct((4, POOL_GROUP, POOL_GROUP), BF16), jax.ShapeDtypeStruct((1, D_MAIN), F32)],
        scratch_shapes=[pltpu.VMEM((POOL_HALO, D_MAIN), F32), pltpu.VMEM((4, POOL_GROUP, POOL_GROUP), F32)],
        input_output_aliases={5: 0},
        compiler_params=_cparams("arbitrary"),
    )(dymain, pm, mixed, pw, scale, dh)


MEM_SCALE = MEM_HEAD_DIM ** -0.5


def _mem_probs(q_ref, mkv_ref, hd):
    cols = slice(MEM_HEAD_DIM * hd, MEM_HEAD_DIM * (hd + 1))
    q = (q_ref[:, cols].astype(F32) * MEM_SCALE).astype(BF16)
    mk = mkv_ref[:, cols].astype(BF16)
    mv = mkv_ref[:, D_MEM + MEM_HEAD_DIM * hd:D_MEM + MEM_HEAD_DIM * (hd + 1)].astype(BF16)
    s = lax.dot_general(q, mk, NT, preferred_element_type=F32)
    e = jnp.exp(s - jnp.max(s, axis=1, keepdims=True))
    return cols, q, mk, mv, e, jnp.sum(e, axis=1, keepdims=True)


def _memattn_fwd(h, mkv, *, tb, name):
    S = h.shape[0]
    tb = min(tb, S)

    def body(q_ref, mkv_ref, y_ref):
        for hd in range(MEM_HEADS):
            cols, _, _, mv, e, l = _mem_probs(q_ref, mkv_ref, hd)
            y_ref[:, cols] = jnp.dot(e.astype(BF16), mv, preferred_element_type=F32) / l

    return pl.pallas_call(
        body, name=name, grid=(S // tb,),
        in_specs=[pl.BlockSpec((tb, D_MEM), lambda i: (i, 2)), pl.BlockSpec((N_MEM, 2 * D_MEM), lambda i: (0, 0))],
        out_specs=pl.BlockSpec((tb, D_MEM), lambda i: (i, 0)),
        out_shape=jax.ShapeDtypeStruct((S, D_MEM), F32),
        compiler_params=_cparams("parallel"),
    )(h, mkv)


def _memattn_bwd(h, mkv, dy, dh, *, tb, name):
    S = h.shape[0]
    tb = min(tb, S)

    def body(q_ref, mkv_ref, dy_ref, dh_in, dh_ref, dmkv_ref):
        del dh_in

        @pl.when(pl.program_id(0) == 0)
        def _():
            dmkv_ref[...] = jnp.zeros_like(dmkv_ref)

        for hd in range(MEM_HEADS):
            cols, q, mk, mv, e, l = _mem_probs(q_ref, mkv_ref, hd)
            p = e / l
            dyh = dy_ref[:, cols].astype(BF16)
            dp = lax.dot_general(dyh, mv, NT, preferred_element_type=F32)
            ds = p * (dp - jnp.sum(dp * p, axis=1, keepdims=True))
            dsb = ds.astype(BF16)
            dh_ref[:, cols] = (jnp.dot(dsb, mk, preferred_element_type=F32) * MEM_SCALE).astype(BF16)
            dmkv_ref[:, cols] += lax.dot_general(dsb, q, TN, preferred_element_type=F32)
            vcols = slice(D_MEM + MEM_HEAD_DIM * hd, D_MEM + MEM_HEAD_DIM * (hd + 1))
            dmkv_ref[:, vcols] += lax.dot_general(p.astype(BF16), dyh, TN, preferred_element_type=F32)

    return pl.pallas_call(
        body, name=name, grid=(S // tb,),
        in_specs=[pl.BlockSpec((tb, D_MEM), lambda i: (i, 2)), pl.BlockSpec((N_MEM, 2 * D_MEM), lambda i: (0, 0)),
                  pl.BlockSpec((tb, D_MEM), lambda i: (i, 0)), pl.BlockSpec(memory_space=pl.ANY)],
        out_specs=[pl.BlockSpec((tb, D_MEM), lambda i: (i, 2)), pl.BlockSpec((N_MEM, 2 * D_MEM), lambda i: (0, 0))],
        out_shape=[jax.ShapeDtypeStruct(dh.shape, dh.dtype), jax.ShapeDtypeStruct((N_MEM, 2 * D_MEM), F32)],
        input_output_aliases={3: 0},
        compiler_params=_cparams("arbitrary"),
    )(h, mkv, dy, dh)


def _forget_fwd(fl, bias, *, tb, name):
    S = fl.shape[0]
    tb = min(tb, S)

    def body(fl_ref, b_ref, o_ref, carry_ref):
        @pl.when(pl.program_id(0) == 0)
        def _():
            carry_ref[...] = jnp.zeros_like(carry_ref)

        z = fl_ref[...] + b_ref[...]
        lf = jnp.minimum(z, 0.0) - jnp.log(1.0 + jnp.exp(-jnp.abs(z)))
        row = lax.broadcasted_iota(jnp.int32, (tb, LANES), 0)
        c = lf
        sh = 1
        while sh < tb:
            c = c + jnp.where(row >= sh, pltpu.roll(c, sh, 0), 0.0)
            sh *= 2
        o_ref[...] = -(carry_ref[...] + c)
        carry_ref[...] += jnp.sum(lf, axis=0, keepdims=True)

    return pl.pallas_call(
        body, name=name, grid=(S // tb,),
        in_specs=[pl.BlockSpec((tb, LANES), lambda i: (i, 0)), pl.BlockSpec((1, LANES), lambda i: (0, 0))],
        out_specs=pl.BlockSpec((tb, LANES), lambda i: (i, 0)),
        out_shape=jax.ShapeDtypeStruct((S, LANES), F32),
        scratch_shapes=[pltpu.VMEM((1, LANES), F32)],
        compiler_params=_cparams("arbitrary"),
    )(fl, bias)


def _forget_bwd(dn, drow, fl, bias, *, tb, name):
    S = fl.shape[0]
    tb = min(tb, S)
    nb = S // tb

    def body(dn_ref, dr_ref, fl_ref, b_ref, dh_ref, db_ref, carry_ref):
        @pl.when(pl.program_id(0) == 0)
        def _():
            carry_ref[...] = jnp.zeros_like(carry_ref)
            db_ref[...] = jnp.zeros_like(db_ref)

        src = lax.broadcasted_iota(jnp.int32, (D_MAIN, LANES), 0)
        head = lax.broadcasted_iota(jnp.int32, (D_MAIN, LANES), 1)
        pick = lambda off: jnp.where((src == FOX_HEAD_DIM * head + off) & (head < FOX_HEADS), 1.0, 0.0).astype(BF16)
        hdot = lambda a, sel: sum(jnp.dot(part.astype(BF16), sel, preferred_element_type=F32) for part in _split3(a))
        dcum = hdot(dr_ref[...], pick(3)) - hdot(dn_ref[...], pick(0))
        row = lax.broadcasted_iota(jnp.int32, (tb, LANES), 0)
        c = dcum
        sh = 1
        while sh < tb:
            c = c + jnp.where(row < tb - sh, pltpu.roll(c, tb - sh, 0), 0.0)
            sh *= 2
        dlf = carry_ref[...] + c
        carry_ref[...] += jnp.sum(dcum, axis=0, keepdims=True)
        z = fl_ref[...] + b_ref[...]
        lane = lax.broadcasted_iota(jnp.int32, (tb, LANES), 1)
        dfl = jnp.where(lane < FOX_HEADS, dlf / (1.0 + jnp.exp(z)), 0.0)
        db_ref[...] += jnp.sum(dfl, axis=0, keepdims=True)
        dh_ref[...] = dfl.astype(BF16)

    rev = lambda i: (nb - 1 - i, 0)
    return pl.pallas_call(
        body, name=name, grid=(nb,),
        in_specs=[pl.BlockSpec((tb, D_MAIN), rev), pl.BlockSpec((tb, D_MAIN), rev), pl.BlockSpec((tb, LANES), rev),
                  pl.BlockSpec((1, LANES), lambda i: (0, 0))],
        out_specs=[pl.BlockSpec((tb, LANES), rev), pl.BlockSpec((1, LANES), lambda i: (0, 0))],
        out_shape=[jax.ShapeDtypeStruct((S, LANES), BF16), jax.ShapeDtypeStruct((1, LANES), F32)],
        scratch_shapes=[pltpu.VMEM((1, LANES), F32)],
        compiler_params=_cparams("arbitrary"),
    )(dn, drow, fl, bias)


FOX_SCALE = FOX_HEAD_DIM ** -0.5
LOG2E = 1.4426950408889634
LN2 = 0.6931471805599453
AUX = FOX_HEAD_DIM


def _split3(x):
    hi = x.astype(BF16).astype(F32)
    r = x - hi
    mid = r.astype(BF16).astype(F32)
    return hi, mid, (r - mid).astype(BF16).astype(F32)


def _aux_placement():
    r = np.arange(3 * LANES)[:, None]
    c = np.arange(FOX_HEADS * LANES)[None, :]
    return jnp.asarray((r % LANES == c // LANES) & (c % LANES == AUX + r // LANES), BF16)


def _aux_lanes(vals, place_ref):
    parts = jnp.concatenate(_split3(vals), axis=1).astype(BF16)
    return jnp.dot(parts, place_ref[...], preferred_element_type=F32)


def _lanes3(lane, base, parts, rest):
    return jnp.where(lane == base, parts[0], jnp.where(lane == base + 1, parts[1],
                                                       jnp.where(lane == base + 2, parts[2], rest)))


def _swap_halves(x):
    return pltpu.roll(x, FOX_HEAD_DIM, 1)


def _causal_steps(nq, keys_outer):
    if keys_outer:
        pairs = [(i, j) for j in range(nq) for i in range(j, nq)]
    else:
        pairs = [(i, j) for i in range(nq) for j in range(i + 1)]
    it, jt = zip(*pairs)
    return jnp.asarray(np.array(it, np.int32)), jnp.asarray(np.array(jt, np.int32))


def _in_proj_fox(x, w, *, tm, name):
    S = x.shape[0]
    tm = min(tm, S)

    def body(x_ref, w_ref, h_ref, qa_ref):
        acc = jnp.dot(x_ref[...].astype(BF16), w_ref[...], preferred_element_type=F32)
        h_ref[...] = acc.astype(BF16)
        lane = lax.broadcasted_iota(jnp.int32, (tm, LANES), 1)
        first = lane < FOX_HEAD_DIM
        ones_q = jnp.where((lane >= AUX) & (lane < AUX + 3), 1.0, 0.0)
        for g in range(FOX_PAIRS):
            q = acc[:, LANES * g:LANES * (g + 1)] * (FOX_SCALE * LOG2E)
            qa_ref[:, 2 * LANES * g:2 * LANES * g + LANES] = jnp.where(first, q, ones_q).astype(BF16)
            qa_ref[:, 2 * LANES * g + LANES:2 * LANES * (g + 1)] = jnp.where(first, _swap_halves(q), ones_q).astype(BF16)

    return pl.pallas_call(
        body, name=name, grid=(S // tm,),
        in_specs=[pl.BlockSpec((tm, D_MODEL), lambda i: (i, 0)), pl.BlockSpec((D_MODEL, D_IN), lambda i: (0, 0))],
        out_specs=[pl.BlockSpec((tm, D_IN), lambda i: (i, 0)), pl.BlockSpec((tm, 2 * D_MAIN), lambda i: (i, 0))],
        out_shape=[jax.ShapeDtypeStruct((S, D_IN), BF16), jax.ShapeDtypeStruct((S, 2 * D_MAIN), BF16)],
        compiler_params=_cparams("parallel"),
    )(x, w)


def _kv_proj_fox(x, wkv, negcum, *, tm, name):
    S = x.shape[0]
    tm = min(tm, S)

    def body(x_ref, w_ref, nc_ref, place_ref, ka_ref, va_ref):
        acc = jnp.dot(x_ref[...].astype(BF16), w_ref[...], preferred_element_type=F32)
        aux = _aux_lanes(nc_ref[...] * LOG2E, place_ref)
        lane = lax.broadcasted_iota(jnp.int32, (tm, LANES), 1)
        first = lane < FOX_HEAD_DIM
        ones_k = jnp.where((lane >= AUX + 3) & (lane < AUX + 6), 1.0, 0.0)
        for g in range(FOX_PAIRS):
            k = acc[:, LANES * g:LANES * (g + 1)]
            v = acc[:, D_MAIN + LANES * g:D_MAIN + LANES * (g + 1)]
            for hh in range(2):
                sl = slice(LANES * (2 * g + hh), LANES * (2 * g + hh + 1))
                kh, vh = (k, v) if hh == 0 else (_swap_halves(k), _swap_halves(v))
                ka_ref[:, sl] = jnp.where(first, kh, aux[:, sl] + ones_k).astype(BF16)
                va_ref[:, sl] = jnp.where(first, vh, 1.0).astype(BF16)

    out = pl.BlockSpec((tm, 2 * D_MAIN), lambda i: (i, 0))
    shp = jax.ShapeDtypeStruct((S, 2 * D_MAIN), BF16)
    return pl.pallas_call(
        body, name=name, grid=(S // tm,),
        in_specs=[pl.BlockSpec((tm, D_MODEL), lambda i: (i, 0)), pl.BlockSpec((D_MODEL, 2 * D_MAIN), lambda i: (0, 0)),
                  pl.BlockSpec((tm, LANES), lambda i: (i, 0)), pl.BlockSpec((3 * LANES, 2 * D_MAIN), lambda i: (0, 0))],
        out_specs=[out, out], out_shape=[shp, shp],
        compiler_params=_cparams("parallel"),
    )(x, wkv, negcum, _aux_placement())


def _dwkv(xt, dk, dv, dfl, *, tk, name):
    S = xt.shape[1]
    tk = min(tk, S)

    def body(x_ref, dk_ref, dv_ref, df_ref, o_ref):
        @pl.when(pl.program_id(0) == 0)
        def _():
            o_ref[...] = jnp.zeros_like(o_ref)

        for b_ref, c0 in ((dk_ref, 0), (dv_ref, D_MAIN), (df_ref, 2 * D_MAIN)):
            o_ref[:, c0:c0 + b_ref.shape[1]] += jnp.dot(x_ref[...], b_ref[...], preferred_element_type=F32)

    row = lambda n: pl.BlockSpec((tk, n), lambda k: (k, 0))
    return pl.pallas_call(
        body, name=name, grid=(S // tk,),
        in_specs=[pl.BlockSpec((D_MODEL, tk), lambda k: (0, k)), row(D_MAIN), row(D_MAIN), row(LANES)],
        out_specs=pl.BlockSpec((D_MODEL, 2 * D_MAIN + LANES), lambda k: (0, 0)),
        out_shape=jax.ShapeDtypeStruct((D_MODEL, 2 * D_MAIN + LANES), F32),
        compiler_params=_cparams("arbitrary"),
    )(xt, dk, dv, dfl)


def _dx1(dh, win, dk, dv, dfl, wkv, dz, *, tm, name):
    S = dh.shape[0]
    tm = min(tm, S)

    def body(dh_ref, win_ref, dk_ref, dv_ref, df_ref, wk_ref, wv_ref, wf_ref, dz_ref, o_ref):
        acc = ALPHA * dz_ref[...]
        for a_ref, b_ref in ((dh_ref, win_ref), (dk_ref, wk_ref), (dv_ref, wv_ref), (df_ref, wf_ref)):
            acc = acc + lax.dot_general(a_ref[...], b_ref[...], NT, preferred_element_type=F32)
        o_ref[...] = acc

    row = lambda n: pl.BlockSpec((tm, n), lambda i: (i, 0))
    wcols = lambda n, c: pl.BlockSpec((D_MODEL, n), lambda i, c=c: (0, c))
    return pl.pallas_call(
        body, name=name, grid=(S // tm,),
        in_specs=[row(D_IN), wcols(D_IN, 0), row(D_MAIN), row(D_MAIN), row(LANES),
                  wcols(D_MAIN, 0), wcols(D_MAIN, 1), wcols(LANES, 2 * D_MAIN // LANES), row(D_MODEL)],
        out_specs=row(D_MODEL), out_shape=jax.ShapeDtypeStruct((S, D_MODEL), F32),
        compiler_params=_cparams("parallel"),
    )(dh, win, dk, dv, dfl, wkv, wkv, wkv, dz)


def _fox_fwd(qa, ka, va, *, tq, name):
    S = qa.shape[0]
    tq = min(tq, S)
    nq = S // tq
    half = tq // 2
    it, jt = _causal_steps(nq, keys_outer=False)

    def body(it_ref, jt_ref, qa_ref, ka_ref, va_ref, y_ref, qb_ref, m_ref, acc_ref):
        n = pl.program_id(1)
        i, j = it_ref[n], jt_ref[n]
        first = lax.broadcasted_iota(jnp.int32, (tq, LANES), 1) < FOX_HEAD_DIM

        @pl.when(j == 0)
        def _():
            m_ref[...] = jnp.full_like(m_ref, NEG)
            acc_ref[...] = jnp.zeros_like(acc_ref)

        def update(hh, rows, nk, masked):
            sl = slice(LANES * hh, LANES * (hh + 1))
            s = lax.dot_general(qa_ref[rows, sl], ka_ref[0:nk, sl], NT, preferred_element_type=F32)
            if masked:
                r = lax.broadcasted_iota(jnp.int32, s.shape, 0) + rows.start
                c = lax.broadcasted_iota(jnp.int32, s.shape, 1)
                s = jnp.where(c <= r, s, NEG)
            m_prev = m_ref[hh, rows]
            m_new = jnp.maximum(m_prev, jnp.max(s, axis=1, keepdims=True))
            p = jnp.exp2(s - jnp.tile(m_new, (1, nk // LANES))).astype(BF16)
            acc_ref[hh, rows] = jnp.exp2(m_prev - m_new) * acc_ref[hh, rows] + jnp.dot(
                p, va_ref[0:nk, sl], preferred_element_type=F32)
            m_ref[hh, rows] = m_new

        @pl.when(j < i)
        def _():
            for hh in range(2):
                update(hh, slice(0, tq), tq, False)

        @pl.when(j == i)
        def _():
            for hh in range(2):
                for r0 in range(0, tq, half):
                    update(hh, slice(r0, r0 + half), r0 + half, True)
            lane = lax.broadcasted_iota(jnp.int32, (tq, LANES), 1)
            ys = []
            for hh in range(2):
                sl = slice(LANES * hh, LANES * (hh + 1))
                a = acc_ref[hh]
                denom = _swap_halves(a)
                ys.append(a / denom)
                lse2 = m_ref[hh] + jnp.log(jnp.where(first, denom, a)) * LOG2E
                qb_ref[:, sl] = _lanes3(lane, AUX + 3, _split3(-lse2), qa_ref[:, sl].astype(F32)).astype(BF16)
            y_ref[...] = jnp.where(first, ys[0], _swap_halves(ys[1]))

    qblock = pl.BlockSpec((tq, 2 * LANES), lambda g, n, it, jt: (it[n], g))
    kblock = pl.BlockSpec((tq, 2 * LANES), lambda g, n, it, jt: (jt[n], g))
    return pl.pallas_call(
        body, name=name,
        grid_spec=pltpu.PrefetchScalarGridSpec(
            num_scalar_prefetch=2, grid=(FOX_PAIRS, it.shape[0]),
            in_specs=[qblock, kblock, kblock],
            out_specs=[pl.BlockSpec((tq, LANES), lambda g, n, it, jt: (it[n], g)), qblock],
            scratch_shapes=[pltpu.VMEM((2, tq, LANES), F32), pltpu.VMEM((2, tq, LANES), F32)]),
        out_shape=[jax.ShapeDtypeStruct((S, D_MAIN), F32), jax.ShapeDtypeStruct((S, 2 * D_MAIN), BF16)],
        compiler_params=_cparams("parallel", "arbitrary"),
    )(it, jt, qa, ka, va)


def _fox_bwd(qb, ka, va, dya, dh, *, tq, name):
    S = qb.shape[0]
    tq = min(tq, S)
    nq = S // tq
    half = tq // 2
    it, jt = _causal_steps(nq, keys_outer=True)
    nsteps = it.shape[0]

    def body(it_ref, jt_ref, qb_ref, ka_ref, va_ref, dya_ref, dh_in, dq_ref, dk_ref, dv_ref, dn_ref, drow_ref,
             dq_acc, dk_acc, dv_acc):
        del dh_in
        n = pl.program_id(1)
        i, j = it_ref[n], jt_ref[n]
        first = lax.broadcasted_iota(jnp.int32, (tq, LANES), 1) < FOX_HEAD_DIM

        @pl.when(n == 0)
        def _():
            dq_acc[...] = jnp.zeros_like(dq_acc)

        @pl.when(i == j)
        def _():
            dk_acc[...] = jnp.zeros_like(dk_acc)
            dv_acc[...] = jnp.zeros_like(dv_acc)

        def update(hh, keys, q0, masked):
            sl = slice(LANES * hh, LANES * (hh + 1))
            qbh, kah, dyah = qb_ref[q0:tq, sl], ka_ref[keys, sl], dya_ref[q0:tq, sl]
            eT = lax.dot_general(kah, qbh, NT, preferred_element_type=F32)
            if masked:
                r = lax.broadcasted_iota(jnp.int32, eT.shape, 0) + keys.start
                c = lax.broadcasted_iota(jnp.int32, eT.shape, 1) + q0
                eT = jnp.where(r <= c, eT, NEG)
            pT = jnp.exp2(eT)
            dsT = pT * lax.dot_general(va_ref[keys, sl], dyah, NT, preferred_element_type=F32)
            dsb = dsT.astype(BF16)
            dv_acc[hh, keys] += jnp.dot(pT.astype(BF16), dyah, preferred_element_type=F32)
            dk_acc[hh, keys] += jnp.dot(dsb, qbh, preferred_element_type=F32)
            rows = pl.ds(pl.multiple_of(i * tq + q0, half), tq - q0)
            dq_acc[hh, rows, :] += lax.dot_general(dsb, kah, TN, preferred_element_type=F32)

        @pl.when(i > j)
        def _():
            for hh in range(2):
                update(hh, slice(0, tq), 0, False)

        @pl.when(i == j)
        def _():
            for hh in range(2):
                for k0 in range(0, tq, half):
                    update(hh, slice(k0, k0 + half), k0, True)

        @pl.when(i == nq - 1)
        def _():
            dk_ref[...] = (jnp.where(first, dk_acc[0], _swap_halves(dk_acc[1])) * LN2).astype(BF16)
            dv_ref[...] = jnp.where(first, dv_acc[0], _swap_halves(dv_acc[1])).astype(BF16)
            dn_ref[...] = jnp.where(first, _swap_halves(dk_acc[0]), dk_acc[1])

        @pl.when(n == nsteps - 1)
        def _():
            first_s = lax.broadcasted_iota(jnp.int32, (S, LANES), 1) < FOX_HEAD_DIM
            dq_ref[...] = (jnp.where(first_s, dq_acc[0], _swap_halves(dq_acc[1])) * FOX_SCALE).astype(BF16)
            drow_ref[...] = jnp.where(first_s, _swap_halves(dq_acc[0]), dq_acc[1])

    qblock = pl.BlockSpec((tq, 2 * LANES), lambda g, n, it, jt: (it[n], g))
    kblock = pl.BlockSpec((tq, 2 * LANES), lambda g, n, it, jt: (jt[n], g))
    whole = pl.BlockSpec((S, LANES), lambda g, n, it, jt: (0, g))
    kout = pl.BlockSpec((tq, LANES), lambda g, n, it, jt: (jt[n], g))
    return pl.pallas_call(
        body, name=name,
        grid_spec=pltpu.PrefetchScalarGridSpec(
            num_scalar_prefetch=2, grid=(FOX_PAIRS, nsteps),
            in_specs=[qblock, kblock, kblock, qblock, pl.BlockSpec(memory_space=pl.ANY)],
            out_specs=[whole, kout, kout, kout, whole],
            scratch_shapes=[pltpu.VMEM((2, S, LANES), F32), pltpu.VMEM((2, tq, LANES), F32),
                            pltpu.VMEM((2, tq, LANES), F32)]),
        out_shape=[jax.ShapeDtypeStruct(dh.shape, dh.dtype), jax.ShapeDtypeStruct((S, D_MAIN), BF16),
                   jax.ShapeDtypeStruct((S, D_MAIN), BF16),
                   jax.ShapeDtypeStruct((S, D_MAIN), F32), jax.ShapeDtypeStruct((S, D_MAIN), F32)],
        input_output_aliases={6: 0},
        compiler_params=_cparams("parallel", "arbitrary"),
    )(it, jt, qb, ka, va, dya, dh)


TB_ROWS = 512
TB_SEQ = 512
TB_MEM = 1024
TK_DW_IN = 2048
TK_DW_OUT = 1024
TQ_FOX_FWD = 1024
TQ_FOX_BWD = 1024


def _local_step(x, mem, target, win0, pscale, ln_g, ln_b, bias, comm):
    ones = jnp.ones((1, D_MAIN), F32)
    g0, b0, g1, b1 = ln_g[0:1], ln_b[0:1], ln_g[1:2], ln_b[1:2]
    mm = lambda a, b, mode, dt, tm, tn, tk, name, **kw: _mm(a, b, mode=mode, out_dtype=dt, tm=tm, tn=tn, tk=tk,
                                                            name=name, **kw)

    h0, xt = _in_proj_t(x, win0, tm=TB_SEQ, name="l0_in")
    wmkv, pw, wout0 = comm.next_weights(h0)
    pm, mixed = _pool_fwd(h0, pw, tb=TB_SEQ, name="l0_pool_fwd")
    mkv0 = mm(mem, wmkv[0], "nn", F32, 256, 1024, 1024, "l0_mkv")
    ymem0 = _memattn_fwd(h0, mkv0, tb=TB_MEM, name="l0_mem_fwd")
    yg0 = _gate_fwd(mixed, pscale, h0, ymem0, tb=TB_ROWS, name="l0_gate_fwd")
    x1, xhat0, rstd0, x1t = _out_ln(yg0, wout0, x, g0, b0, tb=TB_SEQ, name="l0_out_ln")
    win1, wout1, wkv, wf = comm.late_weights(x1)

    fl = mm(x1, wf, "nn", F32, 512, LANES, D_MODEL, "f_proj")
    negcum = _forget_fwd(fl, bias, tb=TB_SEQ, name="forget_fwd")
    ka, va = _kv_proj_fox(x1, wkv, negcum, tm=512, name="kv_proj")

    h1, qa = _in_proj_fox(x1, win1, tm=TB_SEQ, name="l1_in")
    y1, qb = _fox_fwd(qa, ka, va, tq=TQ_FOX_FWD, name="fox_fwd")
    mkv1 = mm(mem, wmkv[1], "nn", F32, 256, 1024, 1024, "l1_mkv")
    ymem1 = _memattn_fwd(h1, mkv1, tb=TB_MEM, name="l1_mem_fwd")
    yg1 = _gate_fwd(y1, ones, h1, ymem1, tb=TB_ROWS, name="l1_gate_fwd")
    dz1, dg1, db1, sq = _out_ln_loss(yg1, wout1, x1, g1, b1, target, tb=TB_SEQ, name="l1_out_ln_loss")

    dwout1 = mm(yg1, dz1, "tn", BF16, D_MIX, D_MODEL, TK_DW_OUT, "l1_dwout")
    dya, dymem1, dh1 = _gate_bwd(dz1, wout1, y1, ones, h1, ymem1, tb=TB_ROWS, name="l1_gate_bwd", fox=True)
    dh1, dk, dv, dnp, drowp = _fox_bwd(qb, ka, va, dya, dh1, tq=TQ_FOX_BWD, name="fox_bwd")
    dfl, dbias = _forget_bwd(dnp, drowp, fl, bias, tb=TB_SEQ, name="forget_bwd")
    dh1, dmkv1 = _memattn_bwd(h1, mkv1, dymem1, dh1, tb=TB_MEM, name="l1_mem_bwd")
    dwmkv1 = mm(mem, dmkv1, "tn", BF16, D_MODEL, 1024, N_MEM, "l1_dwmkv")
    dwin1 = mm(x1t, dh1, "nn", BF16, D_MODEL, D_IN // 2, TK_DW_IN, "l1_dwin")
    dwkv = _dwkv(x1t, dk, dv, dfl, tk=TK_DW_OUT, name="dwkv")
    dwkv = dwkv[:, :2 * D_MAIN + FOX_HEADS].reshape(D_MODEL, N_DEV, -1).transpose(1, 0, 2).astype(BF16)
    anchor = comm.send_layer1(dict(w_out=dwout1, w_mem_kv=dwmkv1, w_in=dwin1, w_kv_shared=dwkv))
    dx1 = _dx1(dh1, win1, dk, dv, dfl, wkv, dz1, tm=TB_SEQ, name="l1_dx")

    dz0, dg0, db0 = _ln_bwd(dx1, xhat0, rstd0, g0 + anchor, tb=TB_ROWS, name="l0_ln_bwd")
    dwout0 = mm(yg0, dz0, "tn", BF16, D_MIX, D_MODEL, TK_DW_OUT, "l0_dwout")
    dy0, dymem0, dh0 = _gate_bwd(dz0, wout0, mixed, pscale, h0, ymem0, tb=TB_ROWS, name="l0_gate_bwd")
    comm.landed_layer1(dy0)
    dh0, dpw, dpscale = _pool_bwd(dy0, pm, mixed, pw, pscale, dh0, tb=TB_SEQ, name="l0_pool_bwd")
    dh0, dmkv0 = _memattn_bwd(h0, mkv0, dymem0, dh0, tb=TB_MEM, name="l0_mem_bwd")
    dwmkv0 = mm(mem, dmkv0, "tn", BF16, D_MODEL, 1024, N_MEM, "l0_dwmkv")
    dh0 = comm.send_layer0_first(dict(w_out=dwout0, w_mem_kv=dwmkv0, pool_w=dpw), dh0)
    dwin0 = mm(xt, dh0, "nn", BF16, D_MODEL, D_IN // 2, TK_DW_IN, "l0_dwin")
    dz0 = comm.send_layer0_rest(dict(w_in=dwin0, pool_scale=dpscale, ln_g=jnp.concatenate([dg0, dg1]),
                                     ln_b=jnp.concatenate([db0, db1]), b_forget=dbias[0, :FOX_HEADS]), dz0)
    gx = mm(dh0, win0, "nt", F32, TB_SEQ, D_MODEL, D_IN, "l0_dx", add=dz0, add_scale=ALPHA)
    return sq, gx


MESH_ID = pl.DeviceIdType.MESH
HBM = pl.BlockSpec(memory_space=pl.ANY)
SLICED = {"w_in": (2, D_IN // N_DEV), "w_mem_kv": (1, D_MODEL // N_DEV), "w_out": (1, D_MIX // N_DEV),
          "pool_w": (1, POOL_GROUP // N_DEV)}


def _place():
    return lax.axis_index("x"), lax.axis_index("y"), lax.axis_index("c")


def _slot(p):
    return 4 * p[0] + 2 * p[1] + p[2]


def _cut(ref, axis, width, s):
    idx = [slice(None)] * len(ref.shape)
    idx[axis] = pl.ds(s * width, width)
    return ref.at[tuple(idx)]


def _all_gather(shards, cuts, *, name):
    nt = len(shards)

    def full_shape(a, cut):
        if cut is None:
            return (N_DEV,) + a.shape
        return a.shape[:cut[0]] + (a.shape[cut[0]] * N_DEV,) + a.shape[cut[0] + 1:]

    def body(*refs):
        ins, outs = refs[:nt], refs[nt:2 * nt]
        send_sems, recv_sems, local_sems = refs[2 * nt:]
        x, y, c = _place()
        me, sibling = (x, y, c), (x, y, 1 - c)
        chips = [(1 - x, y), (x, 1 - y), (1 - x, 1 - y)]

        def place(t, s):
            return outs[t].at[s] if cuts[t] is None else _cut(outs[t], cuts[t][0], cuts[t][1], s)

        def copies(k, block, to, from_input=False):
            s = _slot(block)
            return [pltpu.make_async_remote_copy(
                src_ref=ins[t] if from_input else place(t, s), dst_ref=place(t, s),
                send_sem=send_sems.at[nt * k + t], recv_sem=recv_sems.at[nt * k + t],
                device_id=to, device_id_type=MESH_ID) for t in range(nt)]

        mine = [pltpu.make_async_copy(ins[t], place(t, _slot(me)), local_sems.at[t]) for t in range(nt)]
        for cp in mine:
            cp.start()
        first = [copies(0, me, sibling, True)] + [copies(1 + j, me, (*chip, c), True) for j, chip in enumerate(chips)]
        for group in first:
            for cp in group:
                cp.start()
        passed = [copies(4 + j, (*chip, c), sibling) for j, chip in enumerate(chips)]
        for j, chip in enumerate(chips):
            for cp in copies(1 + j, (*chip, c), me):
                cp.wait_recv()
            for cp in passed[j]:
                cp.start()
        for cp in copies(0, sibling, me):
            cp.wait_recv()
        for j, chip in enumerate(chips):
            for cp in copies(4 + j, (*chip, 1 - c), me):
                cp.wait_recv()
        for group in first + passed:
            for cp in group:
                cp.wait_send()
        for cp in mine:
            cp.wait()

    return pl.pallas_call(
        body, name=name, in_specs=[HBM] * nt, out_specs=[HBM] * nt,
        out_shape=[jax.ShapeDtypeStruct(full_shape(a, cut), a.dtype) for a, cut in zip(shards, cuts)],
        scratch_shapes=[pltpu.SemaphoreType.DMA((7 * nt,)), pltpu.SemaphoreType.DMA((7 * nt,)),
                        pltpu.SemaphoreType.DMA((nt,))],
    )(*shards)


def _exchange_copies(items, ins, outs, send_sems, recv_sems, local_sems, gather=False):
    nt = len(items)
    x, y, c = _place()
    me = _slot((x, y, c))
    flip = lambda v, bit: 1 - v if bit else v

    def part(ref, cut, s):
        return ref.at[s] if cut is None else _cut(ref, cut[0], cut[1], s)

    def src(t, s):
        return ins[t] if gather else part(ins[t], items[t][0], s)

    def dst(t, s):
        if gather:
            return part(outs[items[t][1]], items[t][0], s)
        d = outs[items[t][1]].at[s]
        return d if items[t][2] is None else d.at[items[t][2]]

    sends, arrivals = [], []
    for k in range(1, N_DEV):
        peer = (flip(x, k & 4), flip(y, k & 2), flip(c, k & 1))
        ps = _slot(peer)
        for t in range(nt):
            sems = dict(send_sem=send_sems.at[nt * (k - 1) + t], recv_sem=recv_sems.at[nt * (k - 1) + t],
                        device_id=peer, device_id_type=MESH_ID)
            sends.append(pltpu.make_async_remote_copy(src_ref=src(t, ps), dst_ref=dst(t, me), **sems))
            arrivals.append(pltpu.make_async_remote_copy(src_ref=src(t, ps), dst_ref=dst(t, ps), **sems))
    mine = [pltpu.make_async_copy(src(t, me), dst(t, me), local_sems.at[t]) for t in range(nt)]
    return sends, arrivals, mine


SEMS = pl.BlockSpec(memory_space=pltpu.SEMAPHORE)
SIDE_EFFECT = pltpu.SideEffectType.DATAFLOW_SIDE_EFFECTING


def _exchange_start(srcs, items, landings, *, name, gather=False, carry=()):
    nt, nl, nc = len(srcs), len(landings), len(carry)

    def body(*refs):
        ins, lands = refs[:nt], refs[nt:nt + nl]
        send_sems, recv_sems, local_sems = refs[nt + nl + nc:nt + nl + nc + 3]
        token = refs[-1]
        sends, _, mine = _exchange_copies(items, ins, lands, send_sems, recv_sems, local_sems, gather)
        for cp in sends + mine:
            cp.start()
        token[...] = jnp.zeros_like(token)

    hbm = lambda a: pltpu.HBM(a.shape, a.dtype)
    fresh = [pltpu.with_memory_space_constraint(
        lax.empty(l.shape, l.dtype) if isinstance(l, jax.ShapeDtypeStruct) else l, pltpu.HBM) for l in landings]
    res = pl.pallas_call(
        body, name=name, in_specs=[HBM] * (nt + nl + nc),
        out_specs=[SEMS, SEMS, SEMS] + [HBM] * (nt + nl + nc) + [pl.BlockSpec(memory_space=pltpu.VMEM)],
        out_shape=[pltpu.SemaphoreType.DMA((7 * nt,)), pltpu.SemaphoreType.DMA((7 * nt,)), pltpu.SemaphoreType.DMA((nt,))]
        + [hbm(a) for a in srcs] + [hbm(l) for l in landings] + [hbm(a) for a in carry]
        + [jax.ShapeDtypeStruct((8, LANES), F32)],
        input_output_aliases={i: 3 + i for i in range(nt + nl + nc)},
        compiler_params=pltpu.CompilerParams(has_side_effects=SIDE_EFFECT),
    )(*[pltpu.with_memory_space_constraint(a, pltpu.HBM) for a in srcs], *fresh,
      *[pltpu.with_memory_space_constraint(a, pltpu.HBM) for a in carry])
    return res[:3 + nt + nl], res[-1][0:1, 0:1], res[3 + nt + nl:-1]


def _exchange_wait(state, items, nt, after, *, name, gather=False):
    sems, bufs = state[:3], state[3:]
    nl = len(bufs) - nt

    def body(*refs):
        ins, lands = refs[:nt], refs[nt:nt + nl]
        send_sems, recv_sems, local_sems = refs[nt + nl:nt + nl + 3]
        sends, arrivals, mine = _exchange_copies(items, ins, lands, send_sems, recv_sems, local_sems, gather)
        for sent, landed in zip(sends, arrivals):
            landed.wait_recv()
            sent.wait_send()
        for cp in mine:
            cp.wait()

    hbm = lambda a: pltpu.HBM(a.shape, a.dtype)
    res = pl.pallas_call(
        body, name=name, in_specs=[HBM] * (nt + nl) + [SEMS, SEMS, SEMS, HBM], out_specs=[HBM] * (nt + nl),
        out_shape=[hbm(a) for a in bufs],
        input_output_aliases={i: i for i in range(nt + nl)},
        compiler_params=pltpu.CompilerParams(has_side_effects=SIDE_EFFECT),
    )(*bufs, *sems, after)
    return res[nt:]


def _adamw(recv, w, m, v, *, split, name):
    shape = w.shape
    axis, parts = split
    block = shape[:axis] + (shape[axis] // parts,) + shape[axis + 1:]
    nd = len(shape)

    def body(r_ref, w_ref, m_ref, v_ref, g_ref, d_ref, nm_ref, nv_ref):
        g = r_ref[0].astype(F32)
        for j in range(1, N_DEV):
            g = g + r_ref[j].astype(F32)
        nm = ADAM_B1 * m_ref[...] + (1.0 - ADAM_B1) * g
        nv = ADAM_B2 * v_ref[...] + (1.0 - ADAM_B2) * (g * g)
        m_hat = nm / (1.0 - ADAM_B1 ** ADAM_STEP)
        v_hat = nv / (1.0 - ADAM_B2 ** ADAM_STEP)
        g_ref[...] = g
        nm_ref[...] = nm
        nv_ref[...] = nv
        d_ref[...] = -ADAM_LR * (m_hat / (jnp.sqrt(v_hat) + ADAM_EPS) + ADAM_WD * w_ref[...])

    at = lambda i: tuple(i if a == axis else 0 for a in range(nd))
    one = pl.BlockSpec(block, at)
    shp = jax.ShapeDtypeStruct(shape, F32)
    return pl.pallas_call(
        body, name=name, grid=(parts,),
        in_specs=[pl.BlockSpec((N_DEV,) + block, lambda i: (0,) + at(i)), one, one, one],
        out_specs=[one, one, one, one], out_shape=[shp, shp, shp, shp],
        compiler_params=_cparams("parallel"),
    )(recv, w, m, v)


BIG = ("w_in", "w_mem_kv", "w_out", "pool_w", "w_kv_shared")
SMALL = ("pool_scale", "ln_g", "ln_b", "b_forget")
SMALL_ROWS = 40
ADAM_SPLIT = {"w_in": (1, 4), "w_mem_kv": (0, 2), "w_out": (0, 2), "pool_w": (0, 1), "w_kv_shared": (0, 4)}
PER_LAYER_CUT = {"w_out": (0, D_MIX // N_DEV), "w_mem_kv": (0, D_MODEL // N_DEV), "w_in": (1, D_IN // N_DEV),
                 "pool_w": (1, POOL_GROUP // N_DEV)}
EARLY = ("w_out", "w_mem_kv", "w_in", "w_kv_shared")
EARLY_ITEMS = [(PER_LAYER_CUT[n], i, 1) for i, n in enumerate(EARLY[:3])] + [(None, 3, None)]


def _flat(parts, rows):
    v = jnp.concatenate([p.reshape(-1) for p in parts])
    return jnp.pad(v, (0, rows * LANES - v.shape[0])).reshape(rows, LANES)


def _unflat(flat, shapes):
    v, out, off = flat.reshape(-1), [], 0
    for s in shapes:
        n = math.prod(s)
        out.append(v[off:off + n].reshape(s))
        off += n
    return out


def kernel(x, mem, w_in, w_mem_kv, w_out, ln_g, ln_b, pool_w, pool_scale, w_kv_shared, b_forget, loss_target, m_w_in, m_w_mem_kv, m_w_out, m_ln_g, m_ln_b, m_pool_w, m_pool_scale, m_w_kv_shared, m_b_forget, v_w_in, v_w_mem_kv, v_w_out, v_ln_g, v_ln_b, v_pool_w, v_pool_scale, v_w_kv_shared, v_b_forget):
    w = dict(w_in=w_in, w_mem_kv=w_mem_kv, w_out=w_out, ln_g=ln_g, ln_b=ln_b, pool_w=pool_w[0],
             pool_scale=pool_scale, w_kv_shared=w_kv_shared, b_forget=b_forget)
    m = dict(w_in=m_w_in, w_mem_kv=m_w_mem_kv, w_out=m_w_out, ln_g=m_ln_g, ln_b=m_ln_b, pool_w=m_pool_w[0],
             pool_scale=m_pool_scale, w_kv_shared=m_w_kv_shared, b_forget=m_b_forget)
    v = dict(w_in=v_w_in, w_mem_kv=v_w_mem_kv, w_out=v_w_out, ln_g=v_ln_g, ln_b=v_ln_b, pool_w=v_pool_w[0],
             pool_scale=v_pool_scale, w_kv_shared=v_w_kv_shared, b_forget=v_b_forget)

    wb = {n: w[n].astype(BF16) for n in BIG}
    bdt = wb["w_in"].dtype
    win0, pscale = _all_gather([wb["w_in"][0], jnp.pad(pool_scale, ((0, 7), (0, 0)))], [PER_LAYER_CUT["w_in"], None],
                               name="gather_weights")
    pscale = pscale[:, 0, :].reshape(1, D_MAIN)
    next_srcs = [wb["w_mem_kv"], wb["pool_w"], wb["w_out"][0]]
    next_items = [(SLICED["w_mem_kv"], 0, None), (SLICED["pool_w"], 1, None), (PER_LAYER_CUT["w_out"], 2, None)]
    next_state, _, (win0,) = _exchange_start(
        next_srcs, next_items,
        [jax.ShapeDtypeStruct((2, D_MODEL, D_MODEL), bdt), jax.ShapeDtypeStruct((4, POOL_GROUP, POOL_GROUP), bdt),
         jax.ShapeDtypeStruct((D_MIX, D_MODEL), bdt)], name="gather_next_start", gather=True, carry=[win0])
    late_srcs = [wb["w_in"][1], wb["w_out"][1], wb["w_kv_shared"]]
    late_items = [(PER_LAYER_CUT["w_in"], 0, None), (PER_LAYER_CUT["w_out"], 1, None), (None, 2, None)]
    late = {}
    bias = jnp.pad(b_forget, (0, LANES - FOX_HEADS)).reshape(1, LANES)

    def next_weights(h0):
        wmkv, pw, wout0 = _exchange_wait(next_state, next_items, len(next_srcs), h0, name="gather_next_wait", gather=True)
        late["state"], _, (pw,) = _exchange_start(
            late_srcs, late_items,
            [jax.ShapeDtypeStruct((D_MODEL, D_IN), bdt), jax.ShapeDtypeStruct((D_MIX, D_MODEL), bdt),
             jax.ShapeDtypeStruct((N_DEV,) + w_kv_shared.shape, bdt)], name="gather_late_start", gather=True, carry=[pw])
        return wmkv, pw, wout0

    def late_weights(x1):
        win1, wout1, wkv = _exchange_wait(late["state"], late_items, len(late_srcs), x1, name="gather_late_wait",
                                          gather=True)
        wkv = jnp.pad(wkv.transpose(1, 0, 2).reshape(D_MODEL, -1), ((0, 0), (0, LANES - FOX_HEADS)))
        return win1, wout1, wkv, wkv[:, 2 * D_MAIN:]

    sent, recv = {}, {}
    first_items = [(PER_LAYER_CUT["w_out"], 0, 0), (PER_LAYER_CUT["w_mem_kv"], 1, 0), (PER_LAYER_CUT["pool_w"], 2, None)]
    rest_items = [(PER_LAYER_CUT["w_in"], 0, 0), (None, 1, None)]

    def send_layer1(g):
        srcs = [g[n] for n in EARLY]
        lands = [jax.ShapeDtypeStruct((N_DEV, 2) + w[n].shape[1:], g[n].dtype) for n in EARLY[:3]]
        lands.append(jax.ShapeDtypeStruct(g["w_kv_shared"].shape, g["w_kv_shared"].dtype))
        sent["layer1"], anchor, _ = _exchange_start(srcs, EARLY_ITEMS, lands, name="exchange_early_start")
        return anchor

    def landed_layer1(after):
        recv["w_out"], recv["w_mem_kv"], recv["w_in"], recv["w_kv_shared"] = _exchange_wait(
            sent["layer1"], EARLY_ITEMS, len(EARLY), after, name="exchange_early_wait")

    def send_layer0_first(g, carry):
        srcs = [g["w_out"], g["w_mem_kv"], g["pool_w"]]
        sent["first"], _, (carry,) = _exchange_start(
            srcs, first_items,
            [recv["w_out"], recv["w_mem_kv"], jax.ShapeDtypeStruct((N_DEV,) + w["pool_w"].shape, srcs[2].dtype)],
            name="exchange_mid_start", carry=[carry])
        return carry

    def send_layer0_rest(g, carry):
        small = jnp.concatenate([g["pool_scale"].reshape(N_DEV, -1)]
                                + [jnp.broadcast_to(g[n].reshape(1, -1), (N_DEV, g[n].size)) for n in SMALL[1:]], axis=1)
        small = jnp.pad(small, ((0, 0), (0, SMALL_ROWS * LANES - small.shape[1]))).reshape(N_DEV, SMALL_ROWS, LANES)
        sent["rest"], _, (carry,) = _exchange_start(
            [g["w_in"], small], rest_items, [recv["w_in"], jax.ShapeDtypeStruct(small.shape, small.dtype)],
            name="exchange_last_start", carry=[carry])
        return carry

    comm = types.SimpleNamespace(next_weights=next_weights, late_weights=late_weights, send_layer1=send_layer1,
                                 landed_layer1=landed_layer1, send_layer0_first=send_layer0_first,
                                 send_layer0_rest=send_layer0_rest)
    sq, gx = _local_step(x[0], mem[0], loss_target[0], win0, pscale, ln_g, ln_b, bias, comm)
    loss = lax.psum((0.5 / D_MODEL) * jnp.sum(sq), ("x", "y", "c"))

    recv["w_out"], recv["w_mem_kv"], recv["pool_w"] = _exchange_wait(sent["first"], first_items, len(first_items), gx,
                                                                     name="exchange_mid_wait")
    recv["w_in"], recv["small"] = _exchange_wait(sent["rest"], rest_items, len(rest_items), gx,
                                                 name="exchange_last_wait")

    outs = {}
    for n in BIG:
        res = _adamw(recv[n], w[n], m[n], v[n], split=ADAM_SPLIT[n], name="adamw_" + n)
        for kind, a in zip(("grad", "delta", "new_m", "new_v"), res):
            outs[kind, n] = a[None] if n == "pool_w" else a
    small_shapes = [w[n].shape for n in SMALL]
    res = _adamw(recv["small"], _flat([w[n] for n in SMALL], SMALL_ROWS), _flat([m[n] for n in SMALL], SMALL_ROWS),
                 _flat([v[n] for n in SMALL], SMALL_ROWS), split=(0, 1), name="adamw_small")
    for kind, flat in zip(("grad", "delta", "new_m", "new_v"), res):
        for n, a in zip(SMALL, _unflat(flat, small_shapes)):
            outs[kind, n] = a
    order = ("w_in", "w_mem_kv", "w_out", "ln_g", "ln_b", "pool_w", "pool_scale", "w_kv_shared", "b_forget")
    return (loss, gx[None], *[outs[kind, n] for kind in ("grad", "delta", "new_m", "new_v") for n in order])
```

```python
import math
import types

import numpy as np
import jax
import jax.numpy as jnp
from jax import lax
from jax.experimental import pallas as pl
from jax.experimental.pallas import tpu as pltpu

F32 = jnp.float32
BF16 = jnp.bfloat16

D_MODEL = 1024
D_MAIN = 1024
D_MEM = 512
D_MIX = D_MAIN + D_MEM
D_IN = 2 * D_MIX
N_MEM = 256
MEM_HEADS = 4
MEM_HEAD_DIM = 128
FOX_HEADS = 16
FOX_HEAD_DIM = 64
FOX_PAIRS = FOX_HEADS // 2
POOL_WINDOWS = (2, 4, 8, 16)
POOL_GROUP = 256
POOL_HALO = 16
ALPHA = 4.0 ** 0.25
LN_EPS = 1e-5
NEG = -1e30
LANES = 128
N_DEV = 8

ADAM_LR = 0.001
ADAM_B1 = 0.9
ADAM_B2 = 0.999
ADAM_EPS = 1e-08
ADAM_WD = 0.01
ADAM_STEP = 10

VMEM_LIMIT = 56 * 1024 * 1024

NN = (((1,), (0,)), ((), ()))
NT = (((1,), (1,)), ((), ()))
TN = (((0,), (0,)), ((), ()))


def _cparams(*sem):
    return pltpu.CompilerParams(dimension_semantics=sem, vmem_limit_bytes=VMEM_LIMIT)


def _sigmoid(z):
    return 1.0 / (1.0 + jnp.exp(-z))


def _mm(a, b, *, mode, out_dtype, tm, tn, tk, name, add=None, add_scale=1.0):
    if mode == "nn":
        (M, K), (K2, N) = a.shape, b.shape
    elif mode == "nt":
        (M, K), (N, K2) = a.shape, b.shape
    else:
        (K, M), (K2, N) = a.shape, b.shape
    assert K == K2, (a.shape, b.shape, mode)
    tm, tn, tk = min(tm, M), min(tn, N), min(tk, K)
    assert M % tm == 0 and N % tn == 0 and K % tk == 0, (M, N, K, tm, tn, tk)
    gm, gn, gk = M // tm, N // tn, K // tk
    dims = {"nn": NN, "nt": NT, "tn": TN}[mode]
    if mode == "tn":
        a_spec = pl.BlockSpec((tk, tm), lambda i, j, k: (k, i))
    else:
        a_spec = pl.BlockSpec((tm, tk), lambda i, j, k: (i, k))
    if mode == "nt":
        b_spec = pl.BlockSpec((tn, tk), lambda i, j, k: (j, k))
    else:
        b_spec = pl.BlockSpec((tk, tn), lambda i, j, k: (k, j))
    o_spec = pl.BlockSpec((tm, tn), lambda i, j, k: (i, j))
    has_add = add is not None
    acc_in_out = out_dtype == F32

    def body(*refs):
        a_ref, b_ref = refs[0], refs[1]
        add_ref = refs[2] if has_add else None
        o_ref = refs[3] if has_add else refs[2]
        prod = lax.dot_general(a_ref[...].astype(BF16), b_ref[...].astype(BF16), dims,
                               preferred_element_type=F32)

        def finish(r):
            if has_add:
                r = r + add_scale * add_ref[...]
            o_ref[...] = r.astype(out_dtype)

        if gk == 1:
            finish(prod)
        else:
            acc_ref = o_ref if acc_in_out else refs[-1]
            k = pl.program_id(2)

            @pl.when(k == 0)
            def _():
                acc_ref[...] = prod

            @pl.when(k > 0)
            def _():
                acc_ref[...] += prod

            if has_add or not acc_in_out:
                @pl.when(k == gk - 1)
                def _():
                    finish(acc_ref[...])

    in_specs = [a_spec, b_spec] + ([o_spec] if has_add else [])
    args = (a, b) + ((add,) if has_add else ())
    return pl.pallas_call(
        body, name=name, grid=(gm, gn, gk), in_specs=in_specs, out_specs=o_spec,
        out_shape=jax.ShapeDtypeStruct((M, N), out_dtype),
        scratch_shapes=[pltpu.VMEM((tm, tn), F32)] if gk > 1 and not acc_in_out else [],
        compiler_params=_cparams("parallel", "parallel", "arbitrary"),
    )(*args)


def _ln_stats(z):
    mu = jnp.mean(z, axis=1, keepdims=True)
    zc = z - mu
    var = jnp.mean(zc * zc, axis=1, keepdims=True)
    rstd = lax.rsqrt(var + LN_EPS)
    return zc * rstd, rstd


def _ln_bwd_math(dy, xhat, rstd, g):
    dxh = dy * g
    m1 = jnp.mean(dxh, axis=1, keepdims=True)
    m2 = jnp.mean(dxh * xhat, axis=1, keepdims=True)
    return rstd * (dxh - m1 - xhat * m2)


def _out_ln(yg, wout, x, g, b, *, tb, name):
    S = x.shape[0]
    tb = min(tb, S)

    def body(yg_ref, w_ref, x_ref, g_ref, b_ref, y_ref, xhat_ref, rstd_ref, yt_ref):
        o = jnp.dot(yg_ref[...], w_ref[...], preferred_element_type=F32)
        xhat, rstd = _ln_stats(ALPHA * x_ref[...] + o)
        xhat_ref[...] = xhat
        rstd_ref[...] = rstd
        y = xhat * g_ref[...] + b_ref[...]
        y_ref[...] = y
        yt_ref[...] = y.T.astype(BF16)

    row = pl.BlockSpec((tb, D_MODEL), lambda i: (i, 0))
    vec = pl.BlockSpec((1, D_MODEL), lambda i: (0, 0))
    return pl.pallas_call(
        body, name=name, grid=(S // tb,),
        in_specs=[pl.BlockSpec((tb, D_MIX), lambda i: (i, 0)), pl.BlockSpec((D_MIX, D_MODEL), lambda i: (0, 0)),
                  row, vec, vec],
        out_specs=[row, row, pl.BlockSpec((tb, 1), lambda i: (i, 0)), pl.BlockSpec((D_MODEL, tb), lambda i: (0, i))],
        out_shape=[jax.ShapeDtypeStruct((S, D_MODEL), F32), jax.ShapeDtypeStruct((S, D_MODEL), F32),
                   jax.ShapeDtypeStruct((S, 1), F32), jax.ShapeDtypeStruct((D_MODEL, S), BF16)],
        compiler_params=_cparams("parallel"),
    )(yg, wout, x, g, b)


def _in_proj_t(x, w, *, tm, name):
    S = x.shape[0]
    tm = min(tm, S)

    def body(x_ref, w_ref, h_ref, xt_ref):
        xv = x_ref[...]
        h_ref[...] = jnp.dot(xv.astype(BF16), w_ref[...], preferred_element_type=F32).astype(BF16)
        xt_ref[...] = xv.T.astype(BF16)

    return pl.pallas_call(
        body, name=name, grid=(S // tm,),
        in_specs=[pl.BlockSpec((tm, D_MODEL), lambda i: (i, 0)), pl.BlockSpec((D_MODEL, D_IN), lambda i: (0, 0))],
        out_specs=[pl.BlockSpec((tm, D_IN), lambda i: (i, 0)), pl.BlockSpec((D_MODEL, tm), lambda i: (0, i))],
        out_shape=[jax.ShapeDtypeStruct((S, D_IN), BF16), jax.ShapeDtypeStruct((D_MODEL, S), BF16)],
        compiler_params=_cparams("parallel"),
    )(x, w)


def _out_ln_loss(yg, wout, x, g, b, target, *, tb, name):
    S = x.shape[0]
    tb = min(tb, S)

    def body(yg_ref, w_ref, x_ref, g_ref, b_ref, t_ref, dz_ref, dg_ref, db_ref, sq_ref):
        @pl.when(pl.program_id(0) == 0)
        def _():
            dg_ref[...] = jnp.zeros_like(dg_ref)
            db_ref[...] = jnp.zeros_like(db_ref)
            sq_ref[...] = jnp.zeros_like(sq_ref)

        o = jnp.dot(yg_ref[...], w_ref[...], preferred_element_type=F32)
        xhat, rstd = _ln_stats(ALPHA * x_ref[...] + o)
        err = xhat * g_ref[...] + b_ref[...] - t_ref[...]
        sq_ref[...] += jnp.sum(err * err, axis=0, keepdims=True)
        dy = err * (1.0 / D_MODEL)
        dz_ref[...] = _ln_bwd_math(dy, xhat, rstd, g_ref[...])
        dg_ref[...] += jnp.sum(dy * xhat, axis=0, keepdims=True)
        db_ref[...] += jnp.sum(dy, axis=0, keepdims=True)

    row = pl.BlockSpec((tb, D_MODEL), lambda i: (i, 0))
    vec = pl.BlockSpec((1, D_MODEL), lambda i: (0, 0))
    vshape = jax.ShapeDtypeStruct((1, D_MODEL), F32)
    return pl.pallas_call(
        body, name=name, grid=(S // tb,),
        in_specs=[pl.BlockSpec((tb, D_MIX), lambda i: (i, 0)), pl.BlockSpec((D_MIX, D_MODEL), lambda i: (0, 0)),
                  row, vec, vec, row],
        out_specs=[row, vec, vec, vec],
        out_shape=[jax.ShapeDtypeStruct((S, D_MODEL), F32), vshape, vshape, vshape],
        compiler_params=_cparams("arbitrary"),
    )(yg, wout, x, g, b, target)


def _ln_bwd(dy, xhat, rstd, g, *, tb, name):
    S = dy.shape[0]
    tb = min(tb, S)

    def body(dy_ref, xhat_ref, rstd_ref, g_ref, dz_ref, dg_ref, db_ref):
        @pl.when(pl.program_id(0) == 0)
        def _():
            dg_ref[...] = jnp.zeros_like(dg_ref)
            db_ref[...] = jnp.zeros_like(db_ref)

        dy_, xhat_ = dy_ref[...], xhat_ref[...]
        dz_ref[...] = _ln_bwd_math(dy_, xhat_, rstd_ref[...], g_ref[...])
        dg_ref[...] += jnp.sum(dy_ * xhat_, axis=0, keepdims=True)
        db_ref[...] += jnp.sum(dy_, axis=0, keepdims=True)

    row = pl.BlockSpec((tb, D_MODEL), lambda i: (i, 0))
    vec = pl.BlockSpec((1, D_MODEL), lambda i: (0, 0))
    return pl.pallas_call(
        body, name=name, grid=(S // tb,),
        in_specs=[row, row, pl.BlockSpec((tb, 1), lambda i: (i, 0)), vec],
        out_specs=[row, vec, vec],
        out_shape=[jax.ShapeDtypeStruct((S, D_MODEL), F32), jax.ShapeDtypeStruct((1, D_MODEL), F32),
                   jax.ShapeDtypeStruct((1, D_MODEL), F32)],
        compiler_params=_cparams("arbitrary"),
    )(dy, xhat, rstd, g)


def _gate_fwd(ysrc, scale, h, ymem, *, tb, name):
    S = ysrc.shape[0]
    tb = min(tb, S)

    def body(ys_ref, sc_ref, ga_ref, gb_ref, gc_ref, ym_ref, yg_ref):
        ymain = ys_ref[...] * sc_ref[...]
        for k, g_ref in enumerate((ga_ref, gb_ref)):
            gv = g_ref[...].astype(F32)
            yg_ref[:, 512 * k:512 * (k + 1)] = (ymain[:, 512 * k:512 * (k + 1)] * gv * _sigmoid(gv)).astype(BF16)
        gv = gc_ref[...].astype(F32)
        yg_ref[:, 1024:1536] = (ym_ref[...] * gv * _sigmoid(gv)).astype(BF16)

    slab = lambda c: pl.BlockSpec((tb, 512), lambda i, c=c: (i, c))
    return pl.pallas_call(
        body, name=name, grid=(S // tb,),
        in_specs=[pl.BlockSpec((tb, D_MAIN), lambda i: (i, 0)), pl.BlockSpec((1, D_MAIN), lambda i: (0, 0)),
                  slab(3), slab(4), slab(5), pl.BlockSpec((tb, D_MEM), lambda i: (i, 0))],
        out_specs=pl.BlockSpec((tb, D_MIX), lambda i: (i, 0)),
        out_shape=jax.ShapeDtypeStruct((S, D_MIX), BF16),
        compiler_params=_cparams("parallel"),
    )(ysrc, scale, h, h, h, ymem)


def _gate_bwd(dz, wout, ysrc, scale, h, ymem, *, tb, name, fox=False):
    S = ysrc.shape[0]
    tb = min(tb, S)

    def dsilu(gv):
        sg = _sigmoid(gv)
        return sg, sg * (1.0 + gv * (1.0 - sg))

    def body(dz_ref, w_ref, ys_ref, sc_ref, ga_ref, gb_ref, gc_ref, ym_ref, dym_ref, dymem_ref, dh_ref):
        dyg = lax.dot_general(dz_ref[...].astype(BF16), w_ref[...], NT, preferred_element_type=F32)
        ymain = ys_ref[...] * sc_ref[...]
        lane = lax.broadcasted_iota(jnp.int32, (tb, LANES), 1)
        first = lane < FOX_HEAD_DIM
        for k, g_ref in enumerate((ga_ref, gb_ref)):
            gv, d = g_ref[...].astype(F32), dyg[:, 512 * k:512 * (k + 1)]
            sg, ds = dsilu(gv)
            dy = d * gv * sg
            dh_ref[:, 512 * k:512 * (k + 1)] = (d * ymain[:, 512 * k:512 * (k + 1)] * ds).astype(BF16)
            if not fox:
                dym_ref[:, 512 * k:512 * (k + 1)] = dy
                continue
            for q in range(512 // LANES):
                cols = slice(LANES * q, LANES * (q + 1))
                dy2 = dy[:, cols]
                prod = dy2 * ymain[:, 512 * k + LANES * q:512 * k + LANES * (q + 1)]
                for hh in range(2):
                    delta = jnp.sum(jnp.where(first == (hh == 0), prod, 0.0), axis=1, keepdims=True)
                    dyh = dy2 if hh == 0 else _swap_halves(dy2)
                    c0 = LANES * (2 * (4 * k + q) + hh)
                    dym_ref[:, c0:c0 + LANES] = jnp.where(
                        first, dyh, _lanes3(lane, AUX, _split3(-delta), 0.0)).astype(BF16)
        gv, d = gc_ref[...].astype(F32), dyg[:, 1024:1536]
        sg, ds = dsilu(gv)
        dymem_ref[...] = d * gv * sg
        dh_ref[:, 1024:1536] = (d * ym_ref[...] * ds).astype(BF16)

    slab = lambda c: pl.BlockSpec((tb, 512), lambda i, c=c: (i, c))
    return pl.pallas_call(
        body, name=name, grid=(S // tb,),
        in_specs=[pl.BlockSpec((tb, D_MODEL), lambda i: (i, 0)), pl.BlockSpec((D_MIX, D_MODEL), lambda i: (0, 0)),
                  pl.BlockSpec((tb, D_MAIN), lambda i: (i, 0)), pl.BlockSpec((1, D_MAIN), lambda i: (0, 0)),
                  slab(3), slab(4), slab(5), pl.BlockSpec((tb, D_MEM), lambda i: (i, 0))],
        out_specs=[pl.BlockSpec((tb, 2 * D_MAIN if fox else D_MAIN), lambda i: (i, 0)),
                   pl.BlockSpec((tb, D_MEM), lambda i: (i, 0)), pl.BlockSpec((tb, D_MIX), lambda i: (i, 1))],
        out_shape=[jax.ShapeDtypeStruct((S, 2 * D_MAIN), BF16) if fox else jax.ShapeDtypeStruct((S, D_MAIN), F32),
                   jax.ShapeDtypeStruct((S, D_MEM), F32), jax.ShapeDtypeStruct((S, D_IN), BF16)],
        compiler_params=_cparams("parallel"),
    )(dz, wout, ysrc, scale, h, h, h, ymem)


def _window_count(t0, rows, w):
    t = t0 + lax.broadcasted_iota(jnp.int32, (rows, POOL_GROUP), 0)
    return jnp.minimum(t + 1, w).astype(F32)


def _pool_fwd(h, pw, *, tb, name):
    S = h.shape[0]
    tb = min(tb, S)

    def body(u_ref, pw_ref, pm_ref, mixed_ref, tail_ref):
        i = pl.program_id(0)

        @pl.when(i == 0)
        def _():
            tail_ref[...] = jnp.zeros_like(tail_ref)

        u = u_ref[...].astype(F32)
        xfull = jnp.concatenate([tail_ref[...], u], axis=0)
        for gi, w in enumerate(POOL_WINDOWS):
            cols = slice(POOL_GROUP * gi, POOL_GROUP * (gi + 1))
            s = xfull[:, cols]
            sh = 1
            while sh < w:
                s = s + pltpu.roll(s, sh, 0)
                sh *= 2
            pm = s[POOL_HALO:, :] / _window_count(i * tb, tb, w) - u[:, cols]
            pmb = pm.astype(BF16)
            pm_ref[:, cols] = pmb
            mixed_ref[:, cols] = jnp.dot(pmb, pw_ref[gi], preferred_element_type=F32)
        tail_ref[...] = u[tb - POOL_HALO:, :]

    return pl.pallas_call(
        body, name=name, grid=(S // tb,),
        in_specs=[pl.BlockSpec((tb, D_MAIN), lambda i: (i, 0)),
                  pl.BlockSpec((4, POOL_GROUP, POOL_GROUP), lambda i: (0, 0, 0))],
        out_specs=[pl.BlockSpec((tb, D_MAIN), lambda i: (i, 0)), pl.BlockSpec((tb, D_MAIN), lambda i: (i, 0))],
        out_shape=[jax.ShapeDtypeStruct((S, D_MAIN), BF16), jax.ShapeDtypeStruct((S, D_MAIN), F32)],
        scratch_shapes=[pltpu.VMEM((POOL_HALO, D_MAIN), F32)],
        compiler_params=_cparams("arbitrary"),
    )(h, pw)


def _pool_bwd(dymain, pm, mixed, pw, scale, dh, *, tb, name):
    S = dymain.shape[0]
    tb = min(tb, S)
    nb = S // tb
    n = tb + POOL_HALO

    def body(dy_ref, pm_ref, mixed_ref, pw_ref, sc_ref, dh_in, dh_ref, dpw_ref, dsc_ref, head_ref, dpw_acc):
        del dh_in
        i = pl.program_id(0)

        @pl.when(i == 0)
        def _():
            head_ref[...] = jnp.zeros_like(head_ref)
            dpw_acc[...] = jnp.zeros_like(dpw_acc)
            dsc_ref[...] = jnp.zeros_like(dsc_ref)

        dy = dy_ref[...]
        dsc_ref[...] += jnp.sum(dy * mixed_ref[...], axis=0, keepdims=True)
        dmixed = dy * sc_ref[...]
        t0 = (nb - 1 - i) * tb
        for gi, w in enumerate(POOL_WINDOWS):
            cols = slice(POOL_GROUP * gi, POOL_GROUP * (gi + 1))
            dm = dmixed[:, cols].astype(BF16)
            dpw_acc[gi] += lax.dot_general(pm_ref[:, cols], dm, TN, preferred_element_type=F32)
            dpm = lax.dot_general(dm, pw_ref[gi], NT, preferred_element_type=F32)
            e = dpm / _window_count(t0, tb, w)
            s = jnp.concatenate([e, head_ref[:, cols]], axis=0)
            sh = 1
            while sh < w:
                s = s + pltpu.roll(s, n - sh, 0)
                sh *= 2
            dh_ref[:, cols] = (s[:tb, :] - dpm).astype(BF16)
            head_ref[:, cols] = e[:POOL_HALO, :]

        @pl.when(i == nb - 1)
        def _():
            dpw_ref[...] = dpw_acc[...].astype(BF16)

    rev = lambda i: (nb - 1 - i, 0)
    return pl.pallas_call(
        body, name=name, grid=(nb,),
        in_specs=[pl.BlockSpec((tb, D_MAIN), rev), pl.BlockSpec((tb, D_MAIN), rev), pl.BlockSpec((tb, D_MAIN), rev),
                  pl.BlockSpec((4, POOL_GROUP, POOL_GROUP), lambda i: (0, 0, 0)),
                  pl.BlockSpec((1, D_MAIN), lambda i: (0, 0)), pl.BlockSpec(memory_space=pl.ANY)],
        out_specs=[pl.BlockSpec((tb, D_MAIN), rev),
                   pl.BlockSpec((4, POOL_GROUP, POOL_GROUP), lambda i: (0, 0, 0)),
                   pl.BlockSpec((1, D_MAIN), lambda i: (0, 0))],
        out_shape=[jax.ShapeDtypeStruct(dh.shape, dh.dtype),
                   jax.ShapeDtypeStruct((4, POOL_GROUP, POOL_GROUP), BF16), jax.ShapeDtypeStruct((1, D_MAIN), F32)],
        scratch_shapes=[pltpu.VMEM((POOL_HALO, D_MAIN), F32), pltpu.VMEM((4, POOL_GROUP, POOL_GROUP), F32)],
        input_output_aliases={5: 0},
        compiler_params=_cparams("arbitrary"),
    )(dymain, pm, mixed, pw, scale, dh)


MEM_SCALE = MEM_HEAD_DIM ** -0.5


def _mem_probs(q_ref, mkv_ref, hd):
    cols = slice(MEM_HEAD_DIM * hd, MEM_HEAD_DIM * (hd + 1))
    q = (q_ref[:, cols].astype(F32) * MEM_SCALE).astype(BF16)
    mk = mkv_ref[:, cols].astype(BF16)
    mv = mkv_ref[:, D_MEM + MEM_HEAD_DIM * hd:D_MEM + MEM_HEAD_DIM * (hd + 1)].astype(BF16)
    s = lax.dot_general(q, mk, NT, preferred_element_type=F32)
    e = jnp.exp(s - jnp.max(s, axis=1, keepdims=True))
    return cols, q, mk, mv, e, jnp.sum(e, axis=1, keepdims=True)


def _memattn_fwd(h, mkv, *, tb, name):
    S = h.shape[0]
    tb = min(tb, S)

    def body(q_ref, mkv_ref, y_ref):
        for hd in range(MEM_HEADS):
            cols, _, _, mv, e, l = _mem_probs(q_ref, mkv_ref, hd)
            y_ref[:, cols] = jnp.dot(e.astype(BF16), mv, preferred_element_type=F32) / l

    return pl.pallas_call(
        body, name=name, grid=(S // tb,),
        in_specs=[pl.BlockSpec((tb, D_MEM), lambda i: (i, 2)), pl.BlockSpec((N_MEM, 2 * D_MEM), lambda i: (0, 0))],
        out_specs=pl.BlockSpec((tb, D_MEM), lambda i: (i, 0)),
        out_shape=jax.ShapeDtypeStruct((S, D_MEM), F32),
        compiler_params=_cparams("parallel"),
    )(h, mkv)


def _memattn_bwd(h, mkv, dy, dh, *, tb, name):
    S = h.shape[0]
    tb = min(tb, S)

    def body(q_ref, mkv_ref, dy_ref, dh_in, dh_ref, dmkv_ref):
        del dh_in

        @pl.when(pl.program_id(0) == 0)
        def _():
            dmkv_ref[...] = jnp.zeros_like(dmkv_ref)

        for hd in range(MEM_HEADS):
            cols, q, mk, mv, e, l = _mem_probs(q_ref, mkv_ref, hd)
            p = e / l
            dyh = dy_ref[:, cols].astype(BF16)
            dp = lax.dot_general(dyh, mv, NT, preferred_element_type=F32)
            ds = p * (dp - jnp.sum(dp * p, axis=1, keepdims=True))
            dsb = ds.astype(BF16)
            dh_ref[:, cols] = (jnp.dot(dsb, mk, preferred_element_type=F32) * MEM_SCALE).astype(BF16)
            dmkv_ref[:, cols] += lax.dot_general(dsb, q, TN, preferred_element_type=F32)
            vcols = slice(D_MEM + MEM_HEAD_DIM * hd, D_MEM + MEM_HEAD_DIM * (hd + 1))
            dmkv_ref[:, vcols] += lax.dot_general(p.astype(BF16), dyh, TN, preferred_element_type=F32)

    return pl.pallas_call(
        body, name=name, grid=(S // tb,),
        in_specs=[pl.BlockSpec((tb, D_MEM), lambda i: (i, 2)), pl.BlockSpec((N_MEM, 2 * D_MEM), lambda i: (0, 0)),
                  pl.BlockSpec((tb, D_MEM), lambda i: (i, 0)), pl.BlockSpec(memory_space=pl.ANY)],
        out_specs=[pl.BlockSpec((tb, D_MEM), lambda i: (i, 2)), pl.BlockSpec((N_MEM, 2 * D_MEM), lambda i: (0, 0))],
        out_shape=[jax.ShapeDtypeStruct(dh.shape, dh.dtype), jax.ShapeDtypeStruct((N_MEM, 2 * D_MEM), F32)],
        input_output_aliases={3: 0},
        compiler_params=_cparams("arbitrary"),
    )(h, mkv, dy, dh)


def _forget_fwd(x, wf, bias, *, tb, name):
    S = x.shape[0]
    tb = min(tb, S)

    def body(x_ref, w_ref, b_ref, fl_ref, o_ref, carry_ref):
        @pl.when(pl.program_id(0) == 0)
        def _():
            carry_ref[...] = jnp.zeros_like(carry_ref)

        fl = jnp.dot(x_ref[...].astype(BF16), w_ref[...], preferred_element_type=F32)
        fl_ref[...] = fl
        z = fl + b_ref[...]
        lf = jnp.minimum(z, 0.0) - jnp.log(1.0 + jnp.exp(-jnp.abs(z)))
        row = lax.broadcasted_iota(jnp.int32, (tb, LANES), 0)
        c = lf
        sh = 1
        while sh < tb:
            c = c + jnp.where(row >= sh, pltpu.roll(c, sh, 0), 0.0)
            sh *= 2
        o_ref[...] = -(carry_ref[...] + c)
        carry_ref[...] += jnp.sum(lf, axis=0, keepdims=True)

    return pl.pallas_call(
        body, name=name, grid=(S // tb,),
        in_specs=[pl.BlockSpec((tb, D_MODEL), lambda i: (i, 0)), pl.BlockSpec((D_MODEL, LANES), lambda i: (0, 0)),
                  pl.BlockSpec((1, LANES), lambda i: (0, 0))],
        out_specs=[pl.BlockSpec((tb, LANES), lambda i: (i, 0)), pl.BlockSpec((tb, LANES), lambda i: (i, 0))],
        out_shape=[jax.ShapeDtypeStruct((S, LANES), F32), jax.ShapeDtypeStruct((S, LANES), F32)],
        scratch_shapes=[pltpu.VMEM((1, LANES), F32)],
        compiler_params=_cparams("arbitrary"),
    )(x, wf, bias)


def _forget_bwd(dn, drow, fl, bias, *, tb, name):
    S = fl.shape[0]
    tb = min(tb, S)
    nb = S // tb

    def body(dn_ref, dr_ref, fl_ref, b_ref, dh_ref, db_ref, carry_ref):
        @pl.when(pl.program_id(0) == 0)
        def _():
            carry_ref[...] = jnp.zeros_like(carry_ref)
            db_ref[...] = jnp.zeros_like(db_ref)

        src = lax.broadcasted_iota(jnp.int32, (D_MAIN, LANES), 0)
        head = lax.broadcasted_iota(jnp.int32, (D_MAIN, LANES), 1)
        pick = lambda off: jnp.where((src == FOX_HEAD_DIM * head + off) & (head < FOX_HEADS), 1.0, 0.0).astype(BF16)
        hdot = lambda a, sel: sum(jnp.dot(part.astype(BF16), sel, preferred_element_type=F32) for part in _split3(a))
        dcum = hdot(dr_ref[...], pick(3)) - hdot(dn_ref[...], pick(0))
        row = lax.broadcasted_iota(jnp.int32, (tb, LANES), 0)
        c = dcum
        sh = 1
        while sh < tb:
            c = c + jnp.where(row < tb - sh, pltpu.roll(c, tb - sh, 0), 0.0)
            sh *= 2
        dlf = carry_ref[...] + c
        carry_ref[...] += jnp.sum(dcum, axis=0, keepdims=True)
        z = fl_ref[...] + b_ref[...]
        lane = lax.broadcasted_iota(jnp.int32, (tb, LANES), 1)
        dfl = jnp.where(lane < FOX_HEADS, dlf / (1.0 + jnp.exp(z)), 0.0)
        db_ref[...] += jnp.sum(dfl, axis=0, keepdims=True)
        dh_ref[...] = dfl.astype(BF16)

    rev = lambda i: (nb - 1 - i, 0)
    return pl.pallas_call(
        body, name=name, grid=(nb,),
        in_specs=[pl.BlockSpec((tb, D_MAIN), rev), pl.BlockSpec((tb, D_MAIN), rev), pl.BlockSpec((tb, LANES), rev),
                  pl.BlockSpec((1, LANES), lambda i: (0, 0))],
        out_specs=[pl.BlockSpec((tb, LANES), rev), pl.BlockSpec((1, LANES), lambda i: (0, 0))],
        out_shape=[jax.ShapeDtypeStruct((S, LANES), BF16), jax.ShapeDtypeStruct((1, LANES), F32)],
        scratch_shapes=[pltpu.VMEM((1, LANES), F32)],
        compiler_params=_cparams("arbitrary"),
    )(dn, drow, fl, bias)


FOX_SCALE = FOX_HEAD_DIM ** -0.5
LOG2E = 1.4426950408889634
LN2 = 0.6931471805599453
AUX = FOX_HEAD_DIM


def _split3(x):
    hi = x.astype(BF16).astype(F32)
    r = x - hi
    mid = r.astype(BF16).astype(F32)
    return hi, mid, (r - mid).astype(BF16).astype(F32)


def _aux_placement():
    r = np.arange(3 * LANES)[:, None]
    c = np.arange(FOX_HEADS * LANES)[None, :]
    return jnp.asarray((r % LANES == c // LANES) & (c % LANES == AUX + r // LANES), BF16)


def _aux_lanes(vals, place_ref):
    parts = jnp.concatenate(_split3(vals), axis=1).astype(BF16)
    return jnp.dot(parts, place_ref[...], preferred_element_type=F32)


def _lanes3(lane, base, parts, rest):
    return jnp.where(lane == base, parts[0], jnp.where(lane == base + 1, parts[1],
                                                       jnp.where(lane == base + 2, parts[2], rest)))


def _swap_halves(x):
    return pltpu.roll(x, FOX_HEAD_DIM, 1)


def _causal_steps(nq, keys_outer):
    if keys_outer:
        pairs = [(i, j) for j in range(nq) for i in range(j, nq)]
    else:
        pairs = [(i, j) for i in range(nq) for j in range(i + 1)]
    it, jt = zip(*pairs)
    return jnp.asarray(np.array(it, np.int32)), jnp.asarray(np.array(jt, np.int32))


def _in_proj_fox(x, w, *, tm, name):
    S = x.shape[0]
    tm = min(tm, S)

    def body(x_ref, w_ref, h_ref, qa_ref):
        acc = jnp.dot(x_ref[...].astype(BF16), w_ref[...], preferred_element_type=F32)
        h_ref[...] = acc.astype(BF16)
        lane = lax.broadcasted_iota(jnp.int32, (tm, LANES), 1)
        first = lane < FOX_HEAD_DIM
        ones_q = jnp.where((lane >= AUX) & (lane < AUX + 3), 1.0, 0.0)
        for g in range(FOX_PAIRS):
            q = acc[:, LANES * g:LANES * (g + 1)] * (FOX_SCALE * LOG2E)
            qa_ref[:, 2 * LANES * g:2 * LANES * g + LANES] = jnp.where(first, q, ones_q).astype(BF16)
            qa_ref[:, 2 * LANES * g + LANES:2 * LANES * (g + 1)] = jnp.where(first, _swap_halves(q), ones_q).astype(BF16)

    return pl.pallas_call(
        body, name=name, grid=(S // tm,),
        in_specs=[pl.BlockSpec((tm, D_MODEL), lambda i: (i, 0)), pl.BlockSpec((D_MODEL, D_IN), lambda i: (0, 0))],
        out_specs=[pl.BlockSpec((tm, D_IN), lambda i: (i, 0)), pl.BlockSpec((tm, 2 * D_MAIN), lambda i: (i, 0))],
        out_shape=[jax.ShapeDtypeStruct((S, D_IN), BF16), jax.ShapeDtypeStruct((S, 2 * D_MAIN), BF16)],
        compiler_params=_cparams("parallel"),
    )(x, w)


def _kv_proj_fox(x, wkv, negcum, *, tm, name):
    S = x.shape[0]
    tm = min(tm, S)

    def body(x_ref, w_ref, nc_ref, place_ref, ka_ref, va_ref):
        acc = jnp.dot(x_ref[...].astype(BF16), w_ref[...], preferred_element_type=F32)
        aux = _aux_lanes(nc_ref[...] * LOG2E, place_ref)
        lane = lax.broadcasted_iota(jnp.int32, (tm, LANES), 1)
        first = lane < FOX_HEAD_DIM
        ones_k = jnp.where((lane >= AUX + 3) & (lane < AUX + 6), 1.0, 0.0)
        for g in range(FOX_PAIRS):
            k = acc[:, LANES * g:LANES * (g + 1)]
            v = acc[:, D_MAIN + LANES * g:D_MAIN + LANES * (g + 1)]
            for hh in range(2):
                sl = slice(LANES * (2 * g + hh), LANES * (2 * g + hh + 1))
                kh, vh = (k, v) if hh == 0 else (_swap_halves(k), _swap_halves(v))
                ka_ref[:, sl] = jnp.where(first, kh, aux[:, sl] + ones_k).astype(BF16)
                va_ref[:, sl] = jnp.where(first, vh, 1.0).astype(BF16)

    out = pl.BlockSpec((tm, 2 * D_MAIN), lambda i: (i, 0))
    shp = jax.ShapeDtypeStruct((S, 2 * D_MAIN), BF16)
    return pl.pallas_call(
        body, name=name, grid=(S // tm,),
        in_specs=[pl.BlockSpec((tm, D_MODEL), lambda i: (i, 0)), pl.BlockSpec((D_MODEL, 2 * D_MAIN), lambda i: (0, 0)),
                  pl.BlockSpec((tm, LANES), lambda i: (i, 0)), pl.BlockSpec((3 * LANES, 2 * D_MAIN), lambda i: (0, 0))],
        out_specs=[out, out], out_shape=[shp, shp],
        compiler_params=_cparams("parallel"),
    )(x, wkv, negcum, _aux_placement())


def _dwkv(xt, dk, dv, dfl, *, tk, name):
    S = xt.shape[1]
    tk = min(tk, S)

    def body(x_ref, dk_ref, dv_ref, df_ref, o_ref):
        @pl.when(pl.program_id(0) == 0)
        def _():
            o_ref[...] = jnp.zeros_like(o_ref)

        for b_ref, c0 in ((dk_ref, 0), (dv_ref, D_MAIN), (df_ref, 2 * D_MAIN)):
            o_ref[:, c0:c0 + b_ref.shape[1]] += jnp.dot(x_ref[...], b_ref[...], preferred_element_type=F32)

    row = lambda n: pl.BlockSpec((tk, n), lambda k: (k, 0))
    return pl.pallas_call(
        body, name=name, grid=(S // tk,),
        in_specs=[pl.BlockSpec((D_MODEL, tk), lambda k: (0, k)), row(D_MAIN), row(D_MAIN), row(LANES)],
        out_specs=pl.BlockSpec((D_MODEL, 2 * D_MAIN + LANES), lambda k: (0, 0)),
        out_shape=jax.ShapeDtypeStruct((D_MODEL, 2 * D_MAIN + LANES), F32),
        compiler_params=_cparams("arbitrary"),
    )(xt, dk, dv, dfl)


def _dx1(dh, win, dk, dv, dfl, wkv, dz, *, tm, name):
    S = dh.shape[0]
    tm = min(tm, S)

    def body(dh_ref, win_ref, dk_ref, dv_ref, df_ref, wk_ref, wv_ref, wf_ref, dz_ref, o_ref):
        acc = ALPHA * dz_ref[...]
        for a_ref, b_ref in ((dh_ref, win_ref), (dk_ref, wk_ref), (dv_ref, wv_ref), (df_ref, wf_ref)):
            acc = acc + lax.dot_general(a_ref[...], b_ref[...], NT, preferred_element_type=F32)
        o_ref[...] = acc

    row = lambda n: pl.BlockSpec((tm, n), lambda i: (i, 0))
    wcols = lambda n, c: pl.BlockSpec((D_MODEL, n), lambda i, c=c: (0, c))
    return pl.pallas_call(
        body, name=name, grid=(S // tm,),
        in_specs=[row(D_IN), wcols(D_IN, 0), row(D_MAIN), row(D_MAIN), row(LANES),
                  wcols(D_MAIN, 0), wcols(D_MAIN, 1), wcols(LANES, 2 * D_MAIN // LANES), row(D_MODEL)],
        out_specs=row(D_MODEL), out_shape=jax.ShapeDtypeStruct((S, D_MODEL), F32),
        compiler_params=_cparams("parallel"),
    )(dh, win, dk, dv, dfl, wkv, wkv, wkv, dz)


def _fox_fwd(qa, ka, va, *, tq, name):
    S = qa.shape[0]
    tq = min(tq, S)
    nq = S // tq
    half = tq // 2
    it, jt = _causal_steps(nq, keys_outer=False)

    def body(it_ref, jt_ref, qa_ref, ka_ref, va_ref, y_ref, qb_ref, m_ref, acc_ref):
        n = pl.program_id(1)
        i, j = it_ref[n], jt_ref[n]
        first = lax.broadcasted_iota(jnp.int32, (tq, LANES), 1) < FOX_HEAD_DIM

        @pl.when(j == 0)
        def _():
            m_ref[...] = jnp.full_like(m_ref, NEG)
            acc_ref[...] = jnp.zeros_like(acc_ref)

        def update(hh, rows, nk, masked):
            sl = slice(LANES * hh, LANES * (hh + 1))
            s = lax.dot_general(qa_ref[rows, sl], ka_ref[0:nk, sl], NT, preferred_element_type=F32)
            if masked:
                r = lax.broadcasted_iota(jnp.int32, s.shape, 0) + rows.start
                c = lax.broadcasted_iota(jnp.int32, s.shape, 1)
                s = jnp.where(c <= r, s, NEG)
            m_prev = m_ref[hh, rows]
            m_new = jnp.maximum(m_prev, jnp.max(s, axis=1, keepdims=True))
            p = jnp.exp2(s - jnp.tile(m_new, (1, nk // LANES))).astype(BF16)
            acc_ref[hh, rows] = jnp.exp2(m_prev - m_new) * acc_ref[hh, rows] + jnp.dot(
                p, va_ref[0:nk, sl], preferred_element_type=F32)
            m_ref[hh, rows] = m_new

        @pl.when(j < i)
        def _():
            for hh in range(2):
                update(hh, slice(0, tq), tq, False)

        @pl.when(j == i)
        def _():
            for hh in range(2):
                for r0 in range(0, tq, half):
                    update(hh, slice(r0, r0 + half), r0 + half, True)
            lane = lax.broadcasted_iota(jnp.int32, (tq, LANES), 1)
            ys = []
            for hh in range(2):
                sl = slice(LANES * hh, LANES * (hh + 1))
                a = acc_ref[hh]
                denom = _swap_halves(a)
                ys.append(a / denom)
                lse2 = m_ref[hh] + jnp.log(jnp.where(first, denom, a)) * LOG2E
                qb_ref[:, sl] = _lanes3(lane, AUX + 3, _split3(-lse2), qa_ref[:, sl].astype(F32)).astype(BF16)
            y_ref[...] = jnp.where(first, ys[0], _swap_halves(ys[1]))

    qblock = pl.BlockSpec((tq, 2 * LANES), lambda g, n, it, jt: (it[n], g))
    kblock = pl.BlockSpec((tq, 2 * LANES), lambda g, n, it, jt: (jt[n], g))
    return pl.pallas_call(
        body, name=name,
        grid_spec=pltpu.PrefetchScalarGridSpec(
            num_scalar_prefetch=2, grid=(FOX_PAIRS, it.shape[0]),
            in_specs=[qblock, kblock, kblock],
            out_specs=[pl.BlockSpec((tq, LANES), lambda g, n, it, jt: (it[n], g)), qblock],
            scratch_shapes=[pltpu.VMEM((2, tq, LANES), F32), pltpu.VMEM((2, tq, LANES), F32)]),
        out_shape=[jax.ShapeDtypeStruct((S, D_MAIN), F32), jax.ShapeDtypeStruct((S, 2 * D_MAIN), BF16)],
        compiler_params=_cparams("parallel", "arbitrary"),
    )(it, jt, qa, ka, va)


def _fox_bwd(qb, ka, va, dya, dh, *, tq, name):
    S = qb.shape[0]
    tq = min(tq, S)
    nq = S // tq
    half = tq // 2
    it, jt = _causal_steps(nq, keys_outer=True)
    nsteps = it.shape[0]

    def body(it_ref, jt_ref, qb_ref, ka_ref, va_ref, dya_ref, dh_in, dq_ref, dk_ref, dv_ref, dn_ref, drow_ref,
             dq_acc, dk_acc, dv_acc):
        del dh_in
        n = pl.program_id(1)
        i, j = it_ref[n], jt_ref[n]
        first = lax.broadcasted_iota(jnp.int32, (tq, LANES), 1) < FOX_HEAD_DIM

        @pl.when(n == 0)
        def _():
            dq_acc[...] = jnp.zeros_like(dq_acc)

        @pl.when(i == j)
        def _():
            dk_acc[...] = jnp.zeros_like(dk_acc)
            dv_acc[...] = jnp.zeros_like(dv_acc)

        def update(hh, keys, q0, masked):
            sl = slice(LANES * hh, LANES * (hh + 1))
            qbh, kah, dyah = qb_ref[q0:tq, sl], ka_ref[keys, sl], dya_ref[q0:tq, sl]
            eT = lax.dot_general(kah, qbh, NT, preferred_element_type=F32)
            if masked:
                r = lax.broadcasted_iota(jnp.int32, eT.shape, 0) + keys.start
                c = lax.broadcasted_iota(jnp.int32, eT.shape, 1) + q0
                eT = jnp.where(r <= c, eT, NEG)
            pT = jnp.exp2(eT)
            dsT = pT * lax.dot_general(va_ref[keys, sl], dyah, NT, preferred_element_type=F32)
            dsb = dsT.astype(BF16)
            dv_acc[hh, keys] += jnp.dot(pT.astype(BF16), dyah, preferred_element_type=F32)
            dk_acc[hh, keys] += jnp.dot(dsb, qbh, preferred_element_type=F32)
            rows = pl.ds(pl.multiple_of(i * tq + q0, half), tq - q0)
            dq_acc[hh, rows, :] += lax.dot_general(dsb, kah, TN, preferred_element_type=F32)

        @pl.when(i > j)
        def _():
            for hh in range(2):
                update(hh, slice(0, tq), 0, False)

        @pl.when(i == j)
        def _():
            for hh in range(2):
                for k0 in range(0, tq, half):
                    update(hh, slice(k0, k0 + half), k0, True)

        @pl.when(i == nq - 1)
        def _():
            dk_ref[...] = (jnp.where(first, dk_acc[0], _swap_halves(dk_acc[1])) * LN2).astype(BF16)
            dv_ref[...] = jnp.where(first, dv_acc[0], _swap_halves(dv_acc[1])).astype(BF16)
            dn_ref[...] = jnp.where(first, _swap_halves(dk_acc[0]), dk_acc[1])

        @pl.when(n == nsteps - 1)
        def _():
            first_s = lax.broadcasted_iota(jnp.int32, (S, LANES), 1) < FOX_HEAD_DIM
            dq_ref[...] = (jnp.where(first_s, dq_acc[0], _swap_halves(dq_acc[1])) * FOX_SCALE).astype(BF16)
            drow_ref[...] = jnp.where(first_s, _swap_halves(dq_acc[0]), dq_acc[1])

    qblock = pl.BlockSpec((tq, 2 * LANES), lambda g, n, it, jt: (it[n], g))
    kblock = pl.BlockSpec((tq, 2 * LANES), lambda g, n, it, jt: (jt[n], g))
    whole = pl.BlockSpec((S, LANES), lambda g, n, it, jt: (0, g))
    kout = pl.BlockSpec((tq, LANES), lambda g, n, it, jt: (jt[n], g))
    return pl.pallas_call(
        body, name=name,
        grid_spec=pltpu.PrefetchScalarGridSpec(
            num_scalar_prefetch=2, grid=(FOX_PAIRS, nsteps),
            in_specs=[qblock, kblock, kblock, qblock, pl.BlockSpec(memory_space=pl.ANY)],
            out_specs=[whole, kout, kout, kout, whole],
            scratch_shapes=[pltpu.VMEM((2, S, LANES), F32), pltpu.VMEM((2, tq, LANES), F32),
                            pltpu.VMEM((2, tq, LANES), F32)]),
        out_shape=[jax.ShapeDtypeStruct(dh.shape, dh.dtype), jax.ShapeDtypeStruct((S, D_MAIN), BF16),
                   jax.ShapeDtypeStruct((S, D_MAIN), BF16),
                   jax.ShapeDtypeStruct((S, D_MAIN), F32), jax.ShapeDtypeStruct((S, D_MAIN), F32)],
        input_output_aliases={6: 0},
        compiler_params=_cparams("parallel", "arbitrary"),
    )(it, jt, qb, ka, va, dya, dh)


TB_ROWS = 512
TB_SEQ = 512
TB_MEM = 1024
TK_DW_IN = 2048
TK_DW_OUT = 1024
TQ_FOX_FWD = 1024
TQ_FOX_BWD = 1024


def _local_step(x, mem, target, win0, pscale, ln_g, ln_b, bias, comm):
    ones = jnp.ones((1, D_MAIN), F32)
    g0, b0, g1, b1 = ln_g[0:1], ln_b[0:1], ln_g[1:2], ln_b[1:2]
    mm = lambda a, b, mode, dt, tm, tn, tk, name, **kw: _mm(a, b, mode=mode, out_dtype=dt, tm=tm, tn=tn, tk=tk,
                                                            name=name, **kw)

    h0, xt = _in_proj_t(x, win0, tm=TB_SEQ, name="l0_in")
    wmkv, pw, wout0 = comm.next_weights(h0)
    pm, mixed = _pool_fwd(h0, pw, tb=TB_SEQ, name="l0_pool_fwd")
    mkv0 = mm(mem, wmkv[0], "nn", F32, 256, 1024, 1024, "l0_mkv")
    ymem0 = _memattn_fwd(h0, mkv0, tb=TB_MEM, name="l0_mem_fwd")
    yg0 = _gate_fwd(mixed, pscale, h0, ymem0, tb=TB_ROWS, name="l0_gate_fwd")
    x1, xhat0, rstd0, x1t = _out_ln(yg0, wout0, x, g0, b0, tb=TB_SEQ, name="l0_out_ln")
    win1, wout1, wkv, wf = comm.late_weights(x1)

    fl, negcum = _forget_fwd(x1, wf, bias, tb=TB_SEQ, name="forget_fwd")
    ka, va = _kv_proj_fox(x1, wkv, negcum, tm=512, name="kv_proj")

    h1, qa = _in_proj_fox(x1, win1, tm=TB_SEQ, name="l1_in")
    y1, qb = _fox_fwd(qa, ka, va, tq=TQ_FOX_FWD, name="fox_fwd")
    mkv1 = mm(mem, wmkv[1], "nn", F32, 256, 1024, 1024, "l1_mkv")
    ymem1 = _memattn_fwd(h1, mkv1, tb=TB_MEM, name="l1_mem_fwd")
    yg1 = _gate_fwd(y1, ones, h1, ymem1, tb=TB_ROWS, name="l1_gate_fwd")
    dz1, dg1, db1, sq = _out_ln_loss(yg1, wout1, x1, g1, b1, target, tb=TB_SEQ, name="l1_out_ln_loss")

    dwout1 = mm(yg1, dz1, "tn", BF16, D_MIX, D_MODEL, TK_DW_OUT, "l1_dwout")
    dya, dymem1, dh1 = _gate_bwd(dz1, wout1, y1, ones, h1, ymem1, tb=TB_ROWS, name="l1_gate_bwd", fox=True)
    dh1, dk, dv, dnp, drowp = _fox_bwd(qb, ka, va, dya, dh1, tq=TQ_FOX_BWD, name="fox_bwd")
    dfl, dbias = _forget_bwd(dnp, drowp, fl, bias, tb=TB_SEQ, name="forget_bwd")
    dh1, dmkv1 = _memattn_bwd(h1, mkv1, dymem1, dh1, tb=TB_MEM, name="l1_mem_bwd")
    dwmkv1 = mm(mem, dmkv1, "tn", BF16, D_MODEL, 1024, N_MEM, "l1_dwmkv")
    dwin1 = mm(x1t, dh1, "nn", BF16, D_MODEL, D_IN // 2, TK_DW_IN, "l1_dwin")
    dwkv = _dwkv(x1t, dk, dv, dfl, tk=TK_DW_OUT, name="dwkv")
    dwkv = dwkv[:, :2 * D_MAIN + FOX_HEADS].reshape(D_MODEL, N_DEV, -1).transpose(1, 0, 2).astype(BF16)
    anchor = comm.send_layer1(dict(w_out=dwout1, w_mem_kv=dwmkv1, w_in=dwin1, w_kv_shared=dwkv))
    dx1 = _dx1(dh1, win1, dk, dv, dfl, wkv, dz1, tm=TB_SEQ, name="l1_dx")

    dz0, dg0, db0 = _ln_bwd(dx1, xhat0, rstd0, g0 + anchor, tb=TB_ROWS, name="l0_ln_bwd")
    dwout0 = mm(yg0, dz0, "tn", BF16, D_MIX, D_MODEL, TK_DW_OUT, "l0_dwout")
    dy0, dymem0, dh0 = _gate_bwd(dz0, wout0, mixed, pscale, h0, ymem0, tb=TB_ROWS, name="l0_gate_bwd")
    comm.landed_layer1(dy0)
    dh0, dpw, dpscale = _pool_bwd(dy0, pm, mixed, pw, pscale, dh0, tb=TB_SEQ, name="l0_pool_bwd")
    dh0, dmkv0 = _memattn_bwd(h0, mkv0, dymem0, dh0, tb=TB_MEM, name="l0_mem_bwd")
    dwmkv0 = mm(mem, dmkv0, "tn", BF16, D_MODEL, 1024, N_MEM, "l0_dwmkv")
    dh0 = comm.send_layer0_first(dict(w_out=dwout0, w_mem_kv=dwmkv0, pool_w=dpw), dh0)
    dwin0 = mm(xt, dh0, "nn", BF16, D_MODEL, D_IN // 2, TK_DW_IN, "l0_dwin")
    dz0 = comm.send_layer0_rest(dict(w_in=dwin0, pool_scale=dpscale, ln_g=jnp.concatenate([dg0, dg1]),
                                     ln_b=jnp.concatenate([db0, db1]), b_forget=dbias[0, :FOX_HEADS]), dz0)
    gx = mm(dh0, win0, "nt", F32, TB_SEQ, D_MODEL, D_IN, "l0_dx", add=dz0, add_scale=ALPHA)
    return sq, gx


MESH_ID = pl.DeviceIdType.MESH
HBM = pl.BlockSpec(memory_space=pl.ANY)
SLICED = {"w_in": (2, D_IN // N_DEV), "w_mem_kv": (1, D_MODEL // N_DEV), "w_out": (1, D_MIX // N_DEV),
          "pool_w": (1, POOL_GROUP // N_DEV)}


def _place():
    return lax.axis_index("x"), lax.axis_index("y"), lax.axis_index("c")


def _slot(p):
    return 4 * p[0] + 2 * p[1] + p[2]


def _cut(ref, axis, width, s):
    idx = [slice(None)] * len(ref.shape)
    idx[axis] = pl.ds(s * width, width)
    return ref.at[tuple(idx)]


def _all_gather(shards, cuts, *, name):
    nt = len(shards)

    def full_shape(a, cut):
        if cut is None:
            return (N_DEV,) + a.shape
        return a.shape[:cut[0]] + (a.shape[cut[0]] * N_DEV,) + a.shape[cut[0] + 1:]

    def body(*refs):
        ins, outs = refs[:nt], refs[nt:2 * nt]
        send_sems, recv_sems, local_sems = refs[2 * nt:]
        x, y, c = _place()
        me, sibling = (x, y, c), (x, y, 1 - c)
        chips = [(1 - x, y), (x, 1 - y), (1 - x, 1 - y)]

        def place(t, s):
            return outs[t].at[s] if cuts[t] is None else _cut(outs[t], cuts[t][0], cuts[t][1], s)

        def copies(k, block, to, from_input=False):
            s = _slot(block)
            return [pltpu.make_async_remote_copy(
                src_ref=ins[t] if from_input else place(t, s), dst_ref=place(t, s),
                send_sem=send_sems.at[nt * k + t], recv_sem=recv_sems.at[nt * k + t],
                device_id=to, device_id_type=MESH_ID) for t in range(nt)]

        mine = [pltpu.make_async_copy(ins[t], place(t, _slot(me)), local_sems.at[t]) for t in range(nt)]
        for cp in mine:
            cp.start()
        first = [copies(0, me, sibling, True)] + [copies(1 + j, me, (*chip, c), True) for j, chip in enumerate(chips)]
        for group in first:
            for cp in group:
                cp.start()
        passed = [copies(4 + j, (*chip, c), sibling) for j, chip in enumerate(chips)]
        for j, chip in enumerate(chips):
            for cp in copies(1 + j, (*chip, c), me):
                cp.wait_recv()
            for cp in passed[j]:
                cp.start()
        for cp in copies(0, sibling, me):
            cp.wait_recv()
        for j, chip in enumerate(chips):
            for cp in copies(4 + j, (*chip, 1 - c), me):
                cp.wait_recv()
        for group in first + passed:
            for cp in group:
                cp.wait_send()
        for cp in mine:
            cp.wait()

    return pl.pallas_call(
        body, name=name, in_specs=[HBM] * nt, out_specs=[HBM] * nt,
        out_shape=[jax.ShapeDtypeStruct(full_shape(a, cut), a.dtype) for a, cut in zip(shards, cuts)],
        scratch_shapes=[pltpu.SemaphoreType.DMA((7 * nt,)), pltpu.SemaphoreType.DMA((7 * nt,)),
                        pltpu.SemaphoreType.DMA((nt,))],
    )(*shards)


def _exchange_copies(items, ins, outs, send_sems, recv_sems, local_sems, gather=False):
    nt = len(items)
    x, y, c = _place()
    me = _slot((x, y, c))
    flip = lambda v, bit: 1 - v if bit else v

    def part(ref, cut, s):
        return ref.at[s] if cut is None else _cut(ref, cut[0], cut[1], s)

    def src(t, s):
        return ins[t] if gather else part(ins[t], items[t][0], s)

    def dst(t, s):
        if gather:
            return part(outs[items[t][1]], items[t][0], s)
        d = outs[items[t][1]].at[s]
        return d if items[t][2] is None else d.at[items[t][2]]

    sends, arrivals = [], []
    for k in range(1, N_DEV):
        peer = (flip(x, k & 4), flip(y, k & 2), flip(c, k & 1))
        ps = _slot(peer)
        for t in range(nt):
            sems = dict(send_sem=send_sems.at[nt * (k - 1) + t], recv_sem=recv_sems.at[nt * (k - 1) + t],
                        device_id=peer, device_id_type=MESH_ID)
            sends.append(pltpu.make_async_remote_copy(src_ref=src(t, ps), dst_ref=dst(t, me), **sems))
            arrivals.append(pltpu.make_async_remote_copy(src_ref=src(t, ps), dst_ref=dst(t, ps), **sems))
    mine = [pltpu.make_async_copy(src(t, me), dst(t, me), local_sems.at[t]) for t in range(nt)]
    return sends, arrivals, mine


SEMS = pl.BlockSpec(memory_space=pltpu.SEMAPHORE)
SIDE_EFFECT = pltpu.SideEffectType.DATAFLOW_SIDE_EFFECTING


def _exchange_start(srcs, items, landings, *, name, gather=False, carry=()):
    nt, nl, nc = len(srcs), len(landings), len(carry)

    def body(*refs):
        ins, lands = refs[:nt], refs[nt:nt + nl]
        send_sems, recv_sems, local_sems = refs[nt + nl + nc:nt + nl + nc + 3]
        token = refs[-1]
        sends, _, mine = _exchange_copies(items, ins, lands, send_sems, recv_sems, local_sems, gather)
        for cp in sends + mine:
            cp.start()
        token[...] = jnp.zeros_like(token)

    hbm = lambda a: pltpu.HBM(a.shape, a.dtype)
    fresh = [pltpu.with_memory_space_constraint(
        lax.empty(l.shape, l.dtype) if isinstance(l, jax.ShapeDtypeStruct) else l, pltpu.HBM) for l in landings]
    res = pl.pallas_call(
        body, name=name, in_specs=[HBM] * (nt + nl + nc),
        out_specs=[SEMS, SEMS, SEMS] + [HBM] * (nt + nl + nc) + [pl.BlockSpec(memory_space=pltpu.VMEM)],
        out_shape=[pltpu.SemaphoreType.DMA((7 * nt,)), pltpu.SemaphoreType.DMA((7 * nt,)), pltpu.SemaphoreType.DMA((nt,))]
        + [hbm(a) for a in srcs] + [hbm(l) for l in landings] + [hbm(a) for a in carry]
        + [jax.ShapeDtypeStruct((8, LANES), F32)],
        input_output_aliases={i: 3 + i for i in range(nt + nl + nc)},
        compiler_params=pltpu.CompilerParams(has_side_effects=SIDE_EFFECT),
    )(*[pltpu.with_memory_space_constraint(a, pltpu.HBM) for a in srcs], *fresh,
      *[pltpu.with_memory_space_constraint(a, pltpu.HBM) for a in carry])
    return res[:3 + nt + nl], res[-1][0:1, 0:1], res[3 + nt + nl:-1]


def _exchange_wait(state, items, nt, after, *, name, gather=False):
    sems, bufs = state[:3], state[3:]
    nl = len(bufs) - nt

    def body(*refs):
        ins, lands = refs[:nt], refs[nt:nt + nl]
        send_sems, recv_sems, local_sems = refs[nt + nl:nt + nl + 3]
        sends, arrivals, mine = _exchange_copies(items, ins, lands, send_sems, recv_sems, local_sems, gather)
        for sent, landed in zip(sends, arrivals):
            landed.wait_recv()
            sent.wait_send()
        for cp in mine:
            cp.wait()

    hbm = lambda a: pltpu.HBM(a.shape, a.dtype)
    res = pl.pallas_call(
        body, name=name, in_specs=[HBM] * (nt + nl) + [SEMS, SEMS, SEMS, HBM], out_specs=[HBM] * (nt + nl),
        out_shape=[hbm(a) for a in bufs],
        input_output_aliases={i: i for i in range(nt + nl)},
        compiler_params=pltpu.CompilerParams(has_side_effects=SIDE_EFFECT),
    )(*bufs, *sems, after)
    return res[nt:]


def _adamw(recv, w, m, v, *, split, name):
    shape = w.shape
    axis, parts = split
    block = shape[:axis] + (shape[axis] // parts,) + shape[axis + 1:]
    nd = len(shape)

    def body(r_ref, w_ref, m_ref, v_ref, g_ref, d_ref, nm_ref, nv_ref):
        g = r_ref[0].astype(F32)
        for j in range(1, N_DEV):
            g = g + r_ref[j].astype(F32)
        nm = ADAM_B1 * m_ref[...] + (1.0 - ADAM_B1) * g
        nv = ADAM_B2 * v_ref[...] + (1.0 - ADAM_B2) * (g * g)
        m_hat = nm / (1.0 - ADAM_B1 ** ADAM_STEP)
        v_hat = nv / (1.0 - ADAM_B2 ** ADAM_STEP)
        g_ref[...] = g
        nm_ref[...] = nm
        nv_ref[...] = nv
        d_ref[...] = -ADAM_LR * (m_hat / (jnp.sqrt(v_hat) + ADAM_EPS) + ADAM_WD * w_ref[...])

    at = lambda i: tuple(i if a == axis else 0 for a in range(nd))
    one = pl.BlockSpec(block, at)
    shp = jax.ShapeDtypeStruct(shape, F32)
    return pl.pallas_call(
        body, name=name, grid=(parts,),
        in_specs=[pl.BlockSpec((N_DEV,) + block, lambda i: (0,) + at(i)), one, one, one],
        out_specs=[one, one, one, one], out_shape=[shp, shp, shp, shp],
        compiler_params=_cparams("parallel"),
    )(recv, w, m, v)


BIG = ("w_in", "w_mem_kv", "w_out", "pool_w", "w_kv_shared")
SMALL = ("pool_scale", "ln_g", "ln_b", "b_forget")
SMALL_ROWS = 40
ADAM_SPLIT = {"w_in": (1, 4), "w_mem_kv": (0, 2), "w_out": (0, 2), "pool_w": (0, 1), "w_kv_shared": (0, 4)}
PER_LAYER_CUT = {"w_out": (0, D_MIX // N_DEV), "w_mem_kv": (0, D_MODEL // N_DEV), "w_in": (1, D_IN // N_DEV),
                 "pool_w": (1, POOL_GROUP // N_DEV)}
EARLY = ("w_out", "w_mem_kv", "w_in", "w_kv_shared")
EARLY_ITEMS = [(PER_LAYER_CUT[n], i, 1) for i, n in enumerate(EARLY[:3])] + [(None, 3, None)]


def _flat(parts, rows):
    v = jnp.concatenate([p.reshape(-1) for p in parts])
    return jnp.pad(v, (0, rows * LANES - v.shape[0])).reshape(rows, LANES)


def _unflat(flat, shapes):
    v, out, off = flat.reshape(-1), [], 0
    for s in shapes:
        n = math.prod(s)
        out.append(v[off:off + n].reshape(s))
        off += n
    return out


def kernel(x, mem, w_in, w_mem_kv, w_out, ln_g, ln_b, pool_w, pool_scale, w_kv_shared, b_forget, loss_target, m_w_in, m_w_mem_kv, m_w_out, m_ln_g, m_ln_b, m_pool_w, m_pool_scale, m_w_kv_shared, m_b_forget, v_w_in, v_w_mem_kv, v_w_out, v_ln_g, v_ln_b, v_pool_w, v_pool_scale, v_w_kv_shared, v_b_forget):
    w = dict(w_in=w_in, w_mem_kv=w_mem_kv, w_out=w_out, ln_g=ln_g, ln_b=ln_b, pool_w=pool_w[0],
             pool_scale=pool_scale, w_kv_shared=w_kv_shared, b_forget=b_forget)
    m = dict(w_in=m_w_in, w_mem_kv=m_w_mem_kv, w_out=m_w_out, ln_g=m_ln_g, ln_b=m_ln_b, pool_w=m_pool_w[0],
             pool_scale=m_pool_scale, w_kv_shared=m_w_kv_shared, b_forget=m_b_forget)
    v = dict(w_in=v_w_in, w_mem_kv=v_w_mem_kv, w_out=v_w_out, ln_g=v_ln_g, ln_b=v_ln_b, pool_w=v_pool_w[0],
             pool_scale=v_pool_scale, w_kv_shared=v_w_kv_shared, b_forget=v_b_forget)

    wb = {n: w[n].astype(BF16) for n in BIG}
    bdt = wb["w_in"].dtype
    win0, pscale = _all_gather([wb["w_in"][0], jnp.pad(pool_scale, ((0, 7), (0, 0)))], [PER_LAYER_CUT["w_in"], None],
                               name="gather_weights")
    pscale = pscale[:, 0, :].reshape(1, D_MAIN)
    next_srcs = [wb["w_mem_kv"], wb["pool_w"], wb["w_out"][0]]
    next_items = [(SLICED["w_mem_kv"], 0, None), (SLICED["pool_w"], 1, None), (PER_LAYER_CUT["w_out"], 2, None)]
    next_state, _, (win0,) = _exchange_start(
        next_srcs, next_items,
        [jax.ShapeDtypeStruct((2, D_MODEL, D_MODEL), bdt), jax.ShapeDtypeStruct((4, POOL_GROUP, POOL_GROUP), bdt),
         jax.ShapeDtypeStruct((D_MIX, D_MODEL), bdt)], name="gather_next_start", gather=True, carry=[win0])
    late_srcs = [wb["w_in"][1], wb["w_out"][1], wb["w_kv_shared"]]
    late_items = [(PER_LAYER_CUT["w_in"], 0, None), (PER_LAYER_CUT["w_out"], 1, None), (None, 2, None)]
    late = {}
    bias = jnp.pad(b_forget, (0, LANES - FOX_HEADS)).reshape(1, LANES)

    def next_weights(h0):
        wmkv, pw, wout0 = _exchange_wait(next_state, next_items, len(next_srcs), h0, name="gather_next_wait", gather=True)
        late["state"], _, (pw,) = _exchange_start(
            late_srcs, late_items,
            [jax.ShapeDtypeStruct((D_MODEL, D_IN), bdt), jax.ShapeDtypeStruct((D_MIX, D_MODEL), bdt),
             jax.ShapeDtypeStruct((N_DEV,) + w_kv_shared.shape, bdt)], name="gather_late_start", gather=True, carry=[pw])
        return wmkv, pw, wout0

    def late_weights(x1):
        win1, wout1, wkv = _exchange_wait(late["state"], late_items, len(late_srcs), x1, name="gather_late_wait",
                                          gather=True)
        wkv = jnp.pad(wkv.transpose(1, 0, 2).reshape(D_MODEL, -1), ((0, 0), (0, LANES - FOX_HEADS)))
        return win1, wout1, wkv, wkv[:, 2 * D_MAIN:]

    sent, recv = {}, {}
    first_items = [(PER_LAYER_CUT["w_out"], 0, 0), (PER_LAYER_CUT["w_mem_kv"], 1, 0), (PER_LAYER_CUT["pool_w"], 2, None)]
    rest_items = [(PER_LAYER_CUT["w_in"], 0, 0), (None, 1, None)]

    def send_layer1(g):
        srcs = [g[n] for n in EARLY]
        lands = [jax.ShapeDtypeStruct((N_DEV, 2) + w[n].shape[1:], g[n].dtype) for n in EARLY[:3]]
        lands.append(jax.ShapeDtypeStruct(g["w_kv_shared"].shape, g["w_kv_shared"].dtype))
        sent["layer1"], anchor, _ = _exchange_start(srcs, EARLY_ITEMS, lands, name="exchange_early_start")
        return anchor

    def landed_layer1(after):
        recv["w_out"], recv["w_mem_kv"], recv["w_in"], recv["w_kv_shared"] = _exchange_wait(
            sent["layer1"], EARLY_ITEMS, len(EARLY), after, name="exchange_early_wait")

    def send_layer0_first(g, carry):
        srcs = [g["w_out"], g["w_mem_kv"], g["pool_w"]]
        sent["first"], _, (carry,) = _exchange_start(
            srcs, first_items,
            [recv["w_out"], recv["w_mem_kv"], jax.ShapeDtypeStruct((N_DEV,) + w["pool_w"].shape, srcs[2].dtype)],
            name="exchange_mid_start", carry=[carry])
        return carry

    def send_layer0_rest(g, carry):
        small = jnp.concatenate([g["pool_scale"].reshape(N_DEV, -1)]
                                + [jnp.broadcast_to(g[n].reshape(1, -1), (N_DEV, g[n].size)) for n in SMALL[1:]], axis=1)
        small = jnp.pad(small, ((0, 0), (0, SMALL_ROWS * LANES - small.shape[1]))).reshape(N_DEV, SMALL_ROWS, LANES)
        sent["rest"], _, (carry,) = _exchange_start(
            [g["w_in"], small], rest_items, [recv["w_in"], jax.ShapeDtypeStruct(small.shape, small.dtype)],
            name="exchange_last_start", carry=[carry])
        return carry

    comm = types.SimpleNamespace(next_weights=next_weights, late_weights=late_weights, send_layer1=send_layer1,
                                 landed_layer1=landed_layer1, send_layer0_first=send_layer0_first,
                                 send_layer0_rest=send_layer0_rest)
    sq, gx = _local_step(x[0], mem[0], loss_target[0], win0, pscale, ln_g, ln_b, bias, comm)
    loss = lax.psum((0.5 / D_MODEL) * jnp.sum(sq), ("x", "y", "c"))

    recv["w_out"], recv["w_mem_kv"], recv["pool_w"] = _exchange_wait(sent["first"], first_items, len(first_items), gx,
                                                                     name="exchange_mid_wait")
    recv["w_in"], recv["small"] = _exchange_wait(sent["rest"], rest_items, len(rest_items), gx,
                                                 name="exchange_last_wait")

    outs = {}
    for n in BIG:
        res = _adamw(recv[n], w[n], m[n], v[n], split=ADAM_SPLIT[n], name="adamw_" + n)
        for kind, a in zip(("grad", "delta", "new_m", "new_v"), res):
            outs[kind, n] = a[None] if n == "pool_w" else a
    small_shapes = [w[n].shape for n in SMALL]
    res = _adamw(recv["small"], _flat([w[n] for n in SMALL], SMALL_ROWS), _flat([m[n] for n in SMALL], SMALL_ROWS),
                 _flat([v[n] for n in SMALL], SMALL_ROWS), split=(0, 1), name="adamw_small")
    for kind, flat in zip(("grad", "delta", "new_m", "new_v"), res):
        for n, a in zip(SMALL, _unflat(flat, small_shapes)):
            outs[kind, n] = a
    order = ("w_in", "w_mem_kv", "w_out", "ln_g", "ln_b", "pool_w", "pool_scale", "w_kv_shared", "b_forget")
    return (loss, gx[None], *[outs[kind, n] for kind in ("grad", "delta", "new_m", "new_v") for n in order])
```
